```python
import math
import jax, jax.numpy as jnp
from jax import lax
import numpy as np

D_MODEL = 1024
BATCH = 8
SEQ = 4096
DEPTH = 2

ATTN_HEADS = 8
ATTN_HEAD_DIM = 64
ATTN_WIDTH = ATTN_HEADS * ATTN_HEAD_DIM
HGRN_HEADS = 4
HGRN_HEAD_DIM = 128
HGRN_WIDTH = HGRN_HEADS * HGRN_HEAD_DIM
MIX_WIDTH = ATTN_WIDTH + HGRN_WIDTH
IN_PROJ_WIDTH = 3 * ATTN_WIDTH + 4 * HGRN_WIDTH
DILATED_PATTERNS = ((128, 1), (512, 4), (2048, 16))
ROPE_THETA = 10000.0
HGRN_CHUNK = 16
MLP_HIDDEN = 4 * D_MODEL
NORM_EPS = 1e-6
MASK_VALUE = -1e30

kernel_name = 'hymba_hgrn2_dilated_swa_hybrid'


def rms_norm(x, gain):
    xf = x.astype(jnp.float32)
    xf = xf * lax.rsqrt(jnp.mean(xf * xf, axis=-1, keepdims=True) + NORM_EPS)
    return (xf * gain.astype(jnp.float32)).astype(x.dtype)


def split_heads(a, n_heads, head_dim):
    b, s, _ = a.shape
    return a.reshape(b, s, n_heads, head_dim).transpose(0, 2, 1, 3)


def merge_heads(a):
    b, h, s, d = a.shape
    return a.transpose(0, 2, 1, 3).reshape(b, s, h * d)


def rotary(x, positions):
    half = x.shape[-1] // 2
    inv_freq = ROPE_THETA ** (-jnp.arange(half, dtype=jnp.float32) / half)
    ang = positions.astype(jnp.float32)[:, None] * inv_freq[None, :]
    cos, sin = jnp.cos(ang), jnp.sin(ang)
    x1, x2 = x[..., :half], x[..., half:]
    return jnp.concatenate([x1 * cos - x2 * sin, x1 * sin + x2 * cos], axis=-1)


def dilated_window_attention(q, k, v, window, dilation):
    b, h, s, d = q.shape
    span = window // dilation
    unit = dilation * span
    s_pad = -(-s // unit) * unit
    pad = ((0, 0), (0, 0), (0, s_pad - s), (0, 0))
    q, k, v = jnp.pad(q, pad), jnp.pad(k, pad), jnp.pad(v, pad)
    n_sub = s_pad // dilation
    n_blk = n_sub // span

    def to_sub(a):
        a = a.reshape(b, h, n_sub, dilation, d).transpose(0, 1, 3, 2, 4)
        return a.reshape(b, h, dilation, n_blk, span, d)

    def with_prev(a):
        prev = jnp.pad(a, ((0, 0), (0, 0), (0, 0), (1, 0), (0, 0), (0, 0)))[:, :, :, :-1]
        return jnp.concatenate([prev, a], axis=4)

    qs = to_sub(q)
    kc, vc = with_prev(to_sub(k)), with_prev(to_sub(v))
    scores = jnp.einsum('bhrnid,bhrnjd->bhrnij', qs, kc)
    i = jnp.arange(span)[:, None]
    j = jnp.arange(2 * span)[None, :]
    dist = span + i - j
    band = (dist >= 0) & (dist <= span)
    blk = jnp.arange(n_blk)[:, None, None]
    valid = band[None] & ((blk > 0) | (j >= span)[None])
    scores = jnp.where(valid, scores, MASK_VALUE)
    m = jnp.max(scores, axis=-1, keepdims=True)
    p = jnp.where(valid, jnp.exp(scores - m), 0.0)
    l = jnp.sum(p, axis=-1, keepdims=True)
    o = jnp.einsum('bhrnij,bhrnjd->bhrnid', p, vc) / l
    lse = (m + jnp.log(l))[..., 0]

    def from_sub(a):
        tail = a.shape[5:]
        a = a.reshape((b, h, dilation, n_sub) + tail)
        a = jnp.moveaxis(a, 2, 3)
        return a.reshape((b, h, s_pad) + tail)[:, :, :s]

    return from_sub(o), from_sub(lse)


def dilated_attention_group(q_a, k_a, v_a, positions):
    q = rotary(split_heads(q_a, ATTN_HEADS, ATTN_HEAD_DIM).astype(jnp.float32), positions)
    q = q * (ATTN_HEAD_DIM ** -0.5)
    k = rotary(split_heads(k_a, ATTN_HEADS, ATTN_HEAD_DIM).astype(jnp.float32), positions)
    v = split_heads(v_a, ATTN_HEADS, ATTN_HEAD_DIM).astype(jnp.float32)
    outs, lses = [], []
    for window, dilation in DILATED_PATTERNS:
        o, lse = dilated_window_attention(q, k, v, window, dilation)
        outs.append(o)
        lses.append(lse)
    weights = jax.nn.softmax(jnp.stack(lses, axis=0), axis=0)
    o = jnp.einsum('pbhs,pbhsd->bhsd', weights, jnp.stack(outs, axis=0))
    return merge_heads(o)


def hgrn_lower_bounds(lb_logits):
    p = jax.nn.softmax(lb_logits.astype(jnp.float32), axis=0)
    return jnp.cumsum(p, axis=0) - p[0]


def hgrn2_chunkwise(q, k, v, log_f):
    b, h, s, kd = q.shape
    vd = v.shape[-1]
    c = HGRN_CHUNK
    n = s // c
    q = q.reshape(b, h, n, c, kd)
    k = k.reshape(b, h, n, c, kd)
    v = v.reshape(b, h, n, c, vd)
    g = jnp.cumsum(log_f.reshape(b, h, n, c, kd), axis=3)
    g_last = g[:, :, :, -1:]
    causal = jnp.tril(jnp.ones((c, c), dtype=bool))[:, :, None]
    diff = g[:, :, :, :, None, :] - g[:, :, :, None, :, :]
    decay = jnp.where(causal, jnp.exp(jnp.where(causal, diff, 0.0)), 0.0)
    a = jnp.einsum('bhnik,bhnjk,bhnijk->bhnij', q, k, decay)
    o_intra = jnp.einsum('bhnij,bhnjv->bhniv', a, v)
    q_in = q * jnp.exp(g)
    k_out = k * jnp.exp(g_last - g)
    chunk_decay = jnp.exp(g_last[:, :, :, 0])

    def step(state, xs):
        q_n, k_n, v_n, dec_n = xs
        o_n = jnp.einsum('bhik,bhkv->bhiv', q_n, state)
        state = dec_n[..., None] * state + jnp.einsum('bhjk,bhjv->bhkv', k_n, v_n)
        return state, o_n

    xs = (jnp.moveaxis(q_in, 2, 0), jnp.moveaxis(k_out, 2, 0),
          jnp.moveaxis(v, 2, 0), jnp.moveaxis(chunk_decay, 2, 0))
    state0 = jnp.zeros((b, h, kd, vd), dtype=jnp.float32)
    _, o_inter = lax.scan(step, state0, xs)
    o = o_intra + jnp.moveaxis(o_inter, 0, 2)
    return o.reshape(b, h, s, vd)


def hgrn2_group(q_h, f_h, i_h, g_h, lower_bound, out_gain):
    q = jax.nn.silu(split_heads(q_h, HGRN_HEADS, HGRN_HEAD_DIM).astype(jnp.float32))
    q = q * (HGRN_HEAD_DIM ** -0.5)
    z = split_heads(f_h, HGRN_HEADS, HGRN_HEAD_DIM).astype(jnp.float32)
    v = split_heads(i_h, HGRN_HEADS, HGRN_HEAD_DIM).astype(jnp.float32)
    lb = lower_bound.reshape(HGRN_HEADS, 1, HGRN_HEAD_DIM)
    log_f = jnp.log(lb + (1.0 - lb) * jax.nn.sigmoid(z))
    k = (1.0 - lb) * jax.nn.sigmoid(-z)
    o = hgrn2_chunkwise(q, k, v, log_f)
    o = rms_norm(o, out_gain)
    gate = jax.nn.silu(split_heads(g_h, HGRN_HEADS, HGRN_HEAD_DIM).astype(jnp.float32))
    return merge_heads(o * gate)


def _fwd_setup_inputs(seed: int = 0) -> dict:
    key = jax.random.key(seed)
    ks = jax.random.split(key, 12)
    f32 = jnp.float32
    x = jax.random.normal(ks[0], (BATCH, SEQ, D_MODEL), f32)
    norm_mix = 1.0 + 0.02 * jax.random.normal(ks[1], (DEPTH, D_MODEL), f32)
    w_in = jax.random.normal(ks[2], (DEPTH, D_MODEL, IN_PROJ_WIDTH), f32) * D_MODEL ** -0.5
    attn_out_gain = 1.0 + 0.02 * jax.random.normal(ks[3], (DEPTH, ATTN_WIDTH), f32)
    hgrn_lb_logits = 0.1 * jax.random.normal(ks[4], (DEPTH, HGRN_WIDTH), f32)
    hgrn_out_gain = 1.0 + 0.02 * jax.random.normal(ks[5], (DEPTH, HGRN_HEAD_DIM), f32)
    w_out = jax.random.normal(ks[6], (DEPTH, MIX_WIDTH, D_MODEL), f32) * MIX_WIDTH ** -0.5
    norm_mlp = 1.0 + 0.02 * jax.random.normal(ks[7], (DEPTH, D_MODEL), f32)
    w_up = jax.random.normal(ks[8], (DEPTH, D_MODEL, MLP_HIDDEN), f32) * D_MODEL ** -0.5
    w_down = jax.random.normal(ks[9], (DEPTH, MLP_HIDDEN, D_MODEL), f32) * MLP_HIDDEN ** -0.5
    norm_final = 1.0 + 0.02 * jax.random.normal(ks[10], (D_MODEL,), f32)
    return {'x': x, 'norm_mix': norm_mix, 'w_in': w_in, 'attn_out_gain': attn_out_gain,
            'hgrn_lb_logits': hgrn_lb_logits, 'hgrn_out_gain': hgrn_out_gain,
            'w_out': w_out, 'norm_mlp': norm_mlp, 'w_up': w_up, 'w_down': w_down,
            'norm_final': norm_final}


def _fwd_reference(x, norm_mix, w_in, attn_out_gain, hgrn_lb_logits, hgrn_out_gain,
              w_out, norm_mlp, w_up, w_down, norm_final):
    seq = x.shape[1]
    positions = jnp.arange(seq, dtype=jnp.int32)
    lower_bounds = hgrn_lower_bounds(hgrn_lb_logits)
    split_points = [ATTN_WIDTH, 2 * ATTN_WIDTH, 3 * ATTN_WIDTH,
                    3 * ATTN_WIDTH + HGRN_WIDTH, 3 * ATTN_WIDTH + 2 * HGRN_WIDTH,
                    3 * ATTN_WIDTH + 3 * HGRN_WIDTH]
    for layer in range(DEPTH):
        h = rms_norm(x, norm_mix[layer])
        proj = h @ w_in[layer]
        q_a, k_a, v_a, q_h, f_h, i_h, g_h = jnp.split(proj, split_points, axis=-1)
        attn = dilated_attention_group(q_a, k_a, v_a, positions)
        attn = rms_norm(attn, attn_out_gain[layer])
        rec = hgrn2_group(q_h, f_h, i_h, g_h, lower_bounds[layer], hgrn_out_gain[layer])
        mixed = jnp.concatenate([attn, rec], axis=-1).astype(x.dtype)
        x = x + mixed @ w_out[layer]
        h = rms_norm(x, norm_mlp[layer])
        x = x + jnp.square(jax.nn.relu(h @ w_up[layer])) @ w_down[layer]
    return rms_norm(x, norm_final)


import jax as _jax
import jax.numpy as _jnp

TWIN_FORMAT = 'train_step'
FWD_PARAMS = ['x', 'norm_mix', 'w_in', 'attn_out_gain', 'hgrn_lb_logits', 'hgrn_out_gain', 'w_out', 'norm_mlp', 'w_up', 'w_down', 'norm_final']
TWIN_WEIGHTS = ['norm_mix', 'w_in', 'attn_out_gain', 'hgrn_lb_logits', 'hgrn_out_gain', 'w_out', 'norm_mlp', 'w_up', 'w_down', 'norm_final']
TWIN_DIFF_INPUT = 'x'
TWIN_INPUTS = ['x', 'norm_mix', 'w_in', 'attn_out_gain', 'hgrn_lb_logits', 'hgrn_out_gain', 'w_out', 'norm_mlp', 'w_up', 'w_down', 'norm_final', 'loss_target', 'm_norm_mix', 'm_w_in', 'm_attn_out_gain', 'm_hgrn_lb_logits', 'm_hgrn_out_gain', 'm_w_out', 'm_norm_mlp', 'm_w_up', 'm_w_down', 'm_norm_final', 'v_norm_mix', 'v_w_in', 'v_attn_out_gain', 'v_hgrn_lb_logits', 'v_hgrn_out_gain', 'v_w_out', 'v_norm_mlp', 'v_w_up', 'v_w_down', 'v_norm_final']
TWIN_OUTPUTS = ['loss', 'grad_x', 'grad_norm_mix', 'grad_w_in', 'grad_attn_out_gain', 'grad_hgrn_lb_logits', 'grad_hgrn_out_gain', 'grad_w_out', 'grad_norm_mlp', 'grad_w_up', 'grad_w_down', 'grad_norm_final', 'delta_norm_mix', 'delta_w_in', 'delta_attn_out_gain', 'delta_hgrn_lb_logits', 'delta_hgrn_out_gain', 'delta_w_out', 'delta_norm_mlp', 'delta_w_up', 'delta_w_down', 'delta_norm_final', 'new_m_norm_mix', 'new_m_w_in', 'new_m_attn_out_gain', 'new_m_hgrn_lb_logits', 'new_m_hgrn_out_gain', 'new_m_w_out', 'new_m_norm_mlp', 'new_m_w_up', 'new_m_w_down', 'new_m_norm_final', 'new_v_norm_mix', 'new_v_w_in', 'new_v_attn_out_gain', 'new_v_hgrn_lb_logits', 'new_v_hgrn_out_gain', 'new_v_w_out', 'new_v_norm_mlp', 'new_v_w_up', 'new_v_w_down', 'new_v_norm_final']
TWIN_LEAF_KINDS = {'loss': 'loss', 'grad_x': 'grad_x', 'grad_norm_mix': 'grad_w', 'grad_w_in': 'grad_w', 'grad_attn_out_gain': 'grad_w', 'grad_hgrn_lb_logits': 'grad_w', 'grad_hgrn_out_gain': 'grad_w', 'grad_w_out': 'grad_w', 'grad_norm_mlp': 'grad_w', 'grad_w_up': 'grad_w', 'grad_w_down': 'grad_w', 'grad_norm_final': 'grad_w', 'delta_norm_mix': 'delta_w', 'delta_w_in': 'delta_w', 'delta_attn_out_gain': 'delta_w', 'delta_hgrn_lb_logits': 'delta_w', 'delta_hgrn_out_gain': 'delta_w', 'delta_w_out': 'delta_w', 'delta_norm_mlp': 'delta_w', 'delta_w_up': 'delta_w', 'delta_w_down': 'delta_w', 'delta_norm_final': 'delta_w', 'new_m_norm_mix': 'new_m', 'new_m_w_in': 'new_m', 'new_m_attn_out_gain': 'new_m', 'new_m_hgrn_lb_logits': 'new_m', 'new_m_hgrn_out_gain': 'new_m', 'new_m_w_out': 'new_m', 'new_m_norm_mlp': 'new_m', 'new_m_w_up': 'new_m', 'new_m_w_down': 'new_m', 'new_m_norm_final': 'new_m', 'new_v_norm_mix': 'new_v', 'new_v_w_in': 'new_v', 'new_v_attn_out_gain': 'new_v', 'new_v_hgrn_lb_logits': 'new_v', 'new_v_hgrn_out_gain': 'new_v', 'new_v_w_out': 'new_v', 'new_v_norm_mlp': 'new_v', 'new_v_w_up': 'new_v', 'new_v_w_down': 'new_v', 'new_v_norm_final': 'new_v'}


def _forward(args):
    return _fwd_reference(*[args[k] for k in FWD_PARAMS])


def _output_shape():
    out = _jax.eval_shape(lambda: _forward(_fwd_setup_inputs(0)))
    return out.shape, out.dtype

N_MICROBATCH = 1
ADAM_LR = 0.001
ADAM_B1 = 0.9
ADAM_B2 = 0.999
ADAM_EPS = 1e-08
ADAM_WD = 0.01
ADAM_STEP = 10
PER_EXAMPLE_BATCH_AXIS = {'x': 0, 'loss_target': 0}
SHARED_INPUTS = []
_WEIGHT_DTYPES = {'norm_mix': _jnp.float32, 'w_in': _jnp.float32, 'attn_out_gain': _jnp.float32, 'hgrn_lb_logits': _jnp.float32, 'hgrn_out_gain': _jnp.float32, 'w_out': _jnp.float32, 'norm_mlp': _jnp.float32, 'w_up': _jnp.float32, 'w_down': _jnp.float32, 'norm_final': _jnp.float32}
MOMENT_SCALE = {'norm_mix': 1.643063e-01, 'w_in': 8.821577e-02, 'attn_out_gain': 1.468567e-01, 'hgrn_lb_logits': 4.071343e-03, 'hgrn_out_gain': 1.557307e-01, 'w_out': 1.157432e-01, 'norm_mlp': 1.284034e-01, 'w_up': 6.484535e-02, 'w_down': 1.446917e-01, 'norm_final': 3.297652e+01}


def _to_microbatches(a, axis):
    t = _jnp.moveaxis(a, axis, 0)
    t = t.reshape((N_MICROBATCH, t.shape[0] // N_MICROBATCH) + t.shape[1:])
    return _jnp.moveaxis(t, 1, axis + 1)


def setup_inputs(seed: int = 0) -> dict:
    inp = _fwd_setup_inputs(seed)
    key = _jax.random.fold_in(_jax.random.key(seed), 7919)
    shape, _ = _output_shape()
    out = dict(inp)
    out["loss_target"] = _jax.random.normal(_jax.random.fold_in(key, 0), shape, _jnp.float32)
    for i, name in enumerate(TWIN_WEIGHTS):
        w = inp[name].astype(_jnp.float32)
        if MOMENT_SCALE is None:
            s = _jnp.sqrt(_jnp.mean(_jnp.square(w)) + 1e-30)
        else:
            s = MOMENT_SCALE[name]
        km, kv = _jax.random.split(_jax.random.fold_in(key, i + 1))
        out[name] = w
        out["m_" + name] = s * _jax.random.normal(km, w.shape, _jnp.float32)
        out["v_" + name] = (s * s) * _jax.random.uniform(kv, w.shape, _jnp.float32, 0.5, 1.5)
    if N_MICROBATCH > 1:
        for name, axis in PER_EXAMPLE_BATCH_AXIS.items():
            out[name] = _to_microbatches(out[name], axis)
    return {'x': out['x'], 'norm_mix': out['norm_mix'], 'w_in': out['w_in'], 'attn_out_gain': out['attn_out_gain'], 'hgrn_lb_logits': out['hgrn_lb_logits'], 'hgrn_out_gain': out['hgrn_out_gain'], 'w_out': out['w_out'], 'norm_mlp': out['norm_mlp'], 'w_up': out['w_up'], 'w_down': out['w_down'], 'norm_final': out['norm_final'], 'loss_target': out['loss_target'], 'm_norm_mix': out['m_norm_mix'], 'm_w_in': out['m_w_in'], 'm_attn_out_gain': out['m_attn_out_gain'], 'm_hgrn_lb_logits': out['m_hgrn_lb_logits'], 'm_hgrn_out_gain': out['m_hgrn_out_gain'], 'm_w_out': out['m_w_out'], 'm_norm_mlp': out['m_norm_mlp'], 'm_w_up': out['m_w_up'], 'm_w_down': out['m_w_down'], 'm_norm_final': out['m_norm_final'], 'v_norm_mix': out['v_norm_mix'], 'v_w_in': out['v_w_in'], 'v_attn_out_gain': out['v_attn_out_gain'], 'v_hgrn_lb_logits': out['v_hgrn_lb_logits'], 'v_hgrn_out_gain': out['v_hgrn_out_gain'], 'v_w_out': out['v_w_out'], 'v_norm_mlp': out['v_norm_mlp'], 'v_w_up': out['v_w_up'], 'v_w_down': out['v_w_down'], 'v_norm_final': out['v_norm_final']}


def _loss(weights, diff, rest, loss_target):
    with _jax.named_scope("forward"):
        args = {**rest, TWIN_DIFF_INPUT: diff, **{k: w.astype(_WEIGHT_DTYPES[k]) for k, w in weights.items()}}
        y = _forward(args)
    with _jax.named_scope("loss_head"):
        err = _jnp.square(y.astype(_jnp.float32) - loss_target)
        return 0.5 * _jnp.sum(_jnp.mean(err, axis=-1)) if err.ndim else 0.5 * err


def _adamw(w, g, m, v):
    m = ADAM_B1 * m + (1.0 - ADAM_B1) * g
    v = ADAM_B2 * v + (1.0 - ADAM_B2) * _jnp.square(g)
    m_hat = m / (1.0 - ADAM_B1 ** ADAM_STEP)
    v_hat = v / (1.0 - ADAM_B2 ** ADAM_STEP)
    delta = -ADAM_LR * (m_hat / (_jnp.sqrt(v_hat) + ADAM_EPS) + ADAM_WD * w)
    return delta, m, v


def reference(x, norm_mix, w_in, attn_out_gain, hgrn_lb_logits, hgrn_out_gain, w_out, norm_mlp, w_up, w_down, norm_final, loss_target, m_norm_mix, m_w_in, m_attn_out_gain, m_hgrn_lb_logits, m_hgrn_out_gain, m_w_out, m_norm_mlp, m_w_up, m_w_down, m_norm_final, v_norm_mix, v_w_in, v_attn_out_gain, v_hgrn_lb_logits, v_hgrn_out_gain, v_w_out, v_norm_mlp, v_w_up, v_w_down, v_norm_final):
    given = dict(x=x, norm_mix=norm_mix, w_in=w_in, attn_out_gain=attn_out_gain, hgrn_lb_logits=hgrn_lb_logits, hgrn_out_gain=hgrn_out_gain, w_out=w_out, norm_mlp=norm_mlp, w_up=w_up, w_down=w_down, norm_final=norm_final, loss_target=loss_target, m_norm_mix=m_norm_mix, m_w_in=m_w_in, m_attn_out_gain=m_attn_out_gain, m_hgrn_lb_logits=m_hgrn_lb_logits, m_hgrn_out_gain=m_hgrn_out_gain, m_w_out=m_w_out, m_norm_mlp=m_norm_mlp, m_w_up=m_w_up, m_w_down=m_w_down, m_norm_final=m_norm_final, v_norm_mix=v_norm_mix, v_w_in=v_w_in, v_attn_out_gain=v_attn_out_gain, v_hgrn_lb_logits=v_hgrn_lb_logits, v_hgrn_out_gain=v_hgrn_out_gain, v_w_out=v_w_out, v_norm_mlp=v_norm_mlp, v_w_up=v_w_up, v_w_down=v_w_down, v_norm_final=v_norm_final)
    weights = {n: given[n] for n in TWIN_WEIGHTS}
    shared = {n: given[n] for n in SHARED_INPUTS}
    per_example = {n: given[n] for n in ['x']}
    grad_fn = _jax.value_and_grad(_loss, argnums=(0, 1))

    def one_microbatch(ex, loss_target):
        ex = dict(ex)
        diff = ex.pop(TWIN_DIFF_INPUT)
        return grad_fn(weights, diff, {**shared, **ex}, loss_target)

    if N_MICROBATCH == 1:
        loss, (grad_w, grad_x) = one_microbatch(per_example, given["loss_target"])
    else:
        def body(carry, xs):
            loss_sum, grad_sum = carry
            l_k, (gw_k, gx_k) = one_microbatch(xs[0], xs[1])
            with _jax.named_scope("update"):
                return (loss_sum + l_k, _jax.tree.map(_jnp.add, grad_sum, gw_k)), gx_k

        init = (_jnp.zeros((), _jnp.float32), _jax.tree.map(_jnp.zeros_like, weights))
        (loss, grad_w), grad_x = _jax.lax.scan(body, init, (per_example, given["loss_target"]))
    with _jax.named_scope("update"):
        delta_w, new_m, new_v = {}, {}, {}
        for n in TWIN_WEIGHTS:
            delta_w[n], new_m[n], new_v[n] = _adamw(weights[n], grad_w[n], given["m_" + n], given["v_" + n])
    return (loss, grad_x, *[grad_w[n] for n in TWIN_WEIGHTS], *[delta_w[n] for n in TWIN_WEIGHTS],
            *[new_m[n] for n in TWIN_WEIGHTS], *[new_v[n] for n in TWIN_WEIGHTS])
```

```python
import functools
import math

import jax
import jax.numpy as jnp
from jax import lax
from jax.experimental import pallas as pl
from jax.experimental.pallas import tpu as pltpu

F32 = jnp.float32
BF = jnp.bfloat16

N_DEV = 8
ATTN_W = 512
HGRN_W = 512
HGRN_HEADS = 4
HGRN_DIM = 128
SEG = 512
N_SEG = 7
PROJ_W = N_SEG * SEG
MIX_W = ATTN_W + HGRN_W
SPAN = 128
DILATIONS = (1, 4, 16)
HGRN_CHUNK = 16
ROPE_THETA = 10000.0
NORM_EPS = 1e-6
MASK_VALUE = -1e30
ATTN_SCALE = 0.125
HGRN_SCALE = HGRN_DIM ** -0.5
ADAM_LR = 0.001
ADAM_B1 = 0.9
ADAM_B2 = 0.999
ADAM_EPS = 1e-08
ADAM_WD = 0.01
ADAM_STEP = 10
LANES = 128
VMEM_LIMIT = 56 * 1024 * 1024

NN = ((1,), (0,))
NT = ((1,), (1,))
TN = ((0,), (0,))
MESH = pl.DeviceIdType.MESH


def _dot(a, b, dims):
    return lax.dot_general(a, b, (dims, ((), ())), preferred_element_type=F32)


def _cparams(sem):
    return pltpu.CompilerParams(dimension_semantics=sem, vmem_limit_bytes=VMEM_LIMIT)


def _part8(x):
    r, n = x.shape
    return jnp.sum(x.reshape(r // 8, 8, n), axis=0)


def _sigmoid(x):
    return 1.0 / (1.0 + jnp.exp(-x))


def _rms_fwd(x, gain):
    r = lax.rsqrt(jnp.mean(x * x, axis=-1, keepdims=True) + NORM_EPS)
    return x * r * gain


def _rms_bwd(dy, x, gain):
    r = lax.rsqrt(jnp.mean(x * x, axis=-1, keepdims=True) + NORM_EPS)
    xn = x * r
    dxn = dy * gain
    dx = r * (dxn - xn * jnp.mean(dxn * xn, axis=-1, keepdims=True))
    return dx, dy * xn


def _rope_partner(x):
    n = x.shape[-1]
    lane = lax.broadcasted_iota(jnp.int32, x.shape, x.ndim - 1)
    return jnp.where((lane % 64) < 32, pltpu.roll(x, n - 32, x.ndim - 1), pltpu.roll(x, 32, x.ndim - 1))


def _tile_lanes(t, reps):
    return jnp.concatenate([t] * reps, axis=-1)


def _mm(name, grid, arrays, in_specs, out_shapes, out_specs, compute, finish, acc_shape, sem,
        aliases=None):
    n_in, n_out, nk = len(arrays), len(out_shapes), grid[2]

    def body(*refs):
        ins, outs, acc = refs[:n_in], refs[n_in:n_in + n_out], refs[-1]
        k = pl.program_id(2)

        @pl.when(k == 0)
        def _():
            acc[...] = jnp.zeros(acc.shape, F32)

        acc[...] += compute(ins, k)

        @pl.when(k == nk - 1)
        def _():
            finish(acc[...], ins, outs)

    return pl.pallas_call(
        body, grid=grid, in_specs=in_specs, out_specs=out_specs, out_shape=out_shapes,
        scratch_shapes=[pltpu.VMEM(acc_shape, F32)], compiler_params=_cparams(sem), name=name,
        input_output_aliases=aliases or {},
    )(*arrays)


def _mm_tn(name, a, b, out_rows, a_lead=None, out_lead=None, out_block_off=0, prev=None, tm=512, tn=512, tk=512):
    kdim = b.shape[0]
    m = a.shape[-1]
    n = b.shape[1]
    tn = min(tn, n)
    tm = min(tm, m)
    mt = m // tm
    n_lead = a.shape[0] if a_lead else 1
    grid = (n_lead * mt, n // tn, kdim // tk)
    if a_lead:
        a_spec = pl.BlockSpec((None, tk, tm), lambda i, j, k: (i // mt, k, i % mt))
    else:
        a_spec = pl.BlockSpec((tk, tm), lambda i, j, k: (k, i))
    b_spec = pl.BlockSpec((tk, tn), lambda i, j, k: (k, j))
    if out_lead:
        o_shape = jax.ShapeDtypeStruct((out_lead, out_rows, n), BF)
        o_spec = pl.BlockSpec((None, tm, tn), lambda i, j, k: ((i + out_block_off) // (out_rows // tm), (i + out_block_off) % (out_rows // tm), j))
    else:
        o_shape = jax.ShapeDtypeStruct((out_rows, n), BF)
        o_spec = pl.BlockSpec((tm, tn), lambda i, j, k: (i + out_block_off, j))
    arrays, specs, aliases = [a, b], [a_spec, b_spec], None
    if prev is not None:
        arrays.append(prev)
        specs.append(pl.BlockSpec(memory_space=pl.ANY))
        aliases = {2: 0}

    def compute(ins, k):
        return _dot(ins[0][...].astype(BF), ins[1][...].astype(BF), TN)

    def finish(acc, ins, outs):
        outs[0][...] = acc.astype(BF)

    return _mm(name, grid, arrays, specs, [o_shape], [o_spec], compute, finish, (tm, tn),
               ("parallel", "parallel", "arbitrary"), aliases)[0]


def _pack_weights(w_in, w_out, w_up, w_down):
    depth, d, cin = w_in.shape

    def body(win_ref, wout_ref, wup_ref, wdown_ref, oin_ref, oout_ref, oup_ref, odown_ref):
        oin_ref[...] = win_ref[...].T.astype(BF)
        oout_ref[...] = wout_ref[...].astype(BF)
        oup_ref[...] = wup_ref[...].astype(BF)
        odown_ref[...] = wdown_ref[...].astype(BF)

    def spec(a):
        return pl.BlockSpec((None,) + a.shape[1:], lambda l: (l, 0, 0))

    outs = [jax.ShapeDtypeStruct((depth, cin, d), BF), jax.ShapeDtypeStruct(w_out.shape, BF),
            jax.ShapeDtypeStruct(w_up.shape, BF), jax.ShapeDtypeStruct(w_down.shape, BF)]
    return pl.pallas_call(
        body, grid=(depth,), in_specs=[spec(w_in), spec(w_out), spec(w_up), spec(w_down)],
        out_specs=[pl.BlockSpec((None, cin, d), lambda l: (l, 0, 0)), spec(w_out), spec(w_up), spec(w_down)],
        out_shape=outs, compiler_params=_cparams(("arbitrary",)), name="pack_weights",
    )(w_in, w_out, w_up, w_down)


def _my_position():
    x, y, c = lax.axis_index("x"), lax.axis_index("y"), lax.axis_index("c")
    return x, y, c, 4 * x + 2 * y + c


def _peer(x, y, c, k):
    px = 1 - x if k & 4 else x
    py = 1 - y if k & 2 else y
    pc = 1 - c if k & 1 else c
    return (px, py, pc), 4 * px + 2 * py + pc


PEER_ORDER = (1, 2, 4, 3, 5, 6, 7)


def _all_gather_weights(p_in, p_out, p_up, p_down):
    depth, cin, d = p_in.shape
    rout = p_out.shape[1]
    hs = p_up.shape[2]
    n_w = 4

    def body(in_ref, out_ref, up_ref, down_ref, gin_ref, gout_ref, gup_ref, gdown_ref, send_sems, recv_sems, local_sems):
        x, y, c, me = _my_position()

        def slots(dev):
            return (gin_ref.at[:, pl.ds(pl.multiple_of(dev * cin, 16), cin), :],
                    gout_ref.at[:, pl.ds(pl.multiple_of(dev * rout, 16), rout), :],
                    gup_ref.at[:, dev],
                    gdown_ref.at[:, pl.ds(pl.multiple_of(dev * hs, 16), hs), :])

        srcs = (in_ref, out_ref, up_ref, down_ref)
        mine = slots(me)
        local = [pltpu.make_async_copy(srcs[w], mine[w], local_sems.at[w]) for w in range(n_w)]
        for cp in local:
            cp.start()
        sends = []
        for k in PEER_ORDER:
            peer, _ = _peer(x, y, c, k)
            for w in range(n_w):
                cp = pltpu.make_async_remote_copy(src_ref=srcs[w], dst_ref=mine[w], send_sem=send_sems.at[k - 1, w],
                                                  recv_sem=recv_sems.at[k - 1, w], device_id=peer, device_id_type=MESH)
                cp.start()
                sends.append(cp)
        for k in PEER_ORDER:
            peer, pid = _peer(x, y, c, k)
            theirs = slots(pid)
            for w in range(n_w):
                pltpu.make_async_remote_copy(src_ref=srcs[w], dst_ref=theirs[w], send_sem=send_sems.at[k - 1, w],
                                             recv_sem=recv_sems.at[k - 1, w], device_id=peer, device_id_type=MESH).wait_recv()
        for cp in sends:
            cp.wait_send()
        for cp in local:
            cp.wait()

    anyspec = pl.BlockSpec(memory_space=pl.ANY)
    outs = [jax.ShapeDtypeStruct((depth, N_DEV * cin, d), BF), jax.ShapeDtypeStruct((depth, N_DEV * rout, d), BF),
            jax.ShapeDtypeStruct((depth, N_DEV, d, hs), BF), jax.ShapeDtypeStruct((depth, N_DEV * hs, d), BF)]
    return pl.pallas_call(
        body, in_specs=[anyspec] * 4, out_specs=[anyspec] * 4, out_shape=outs,
        scratch_shapes=[pltpu.SemaphoreType.DMA((N_DEV - 1, n_w)), pltpu.SemaphoreType.DMA((N_DEV - 1, n_w)),
                        pltpu.SemaphoreType.DMA((n_w,))],
        name="all_gather_weights",
    )(p_in, p_out, p_up, p_down)


def _reduce_scatter_grads(g_in, g_out, g_up, g_down):
    depth, pw, d = g_in.shape
    cin = pw // N_DEV
    rout = g_out.shape[1] // N_DEV
    hs = g_up.shape[3]
    n_w = 4

    def body(in_ref, out_ref, up_ref, down_ref, rin_ref, rout_ref, rup_ref, rdown_ref, send_sems, recv_sems, local_sems):
        x, y, c, me = _my_position()

        def shard(dev):
            return (in_ref.at[:, pl.ds(pl.multiple_of(dev * cin, 16), cin), :],
                    out_ref.at[:, pl.ds(pl.multiple_of(dev * rout, 16), rout), :],
                    up_ref.at[:, dev],
                    down_ref.at[:, pl.ds(pl.multiple_of(dev * hs, 16), hs), :])

        def slot(src_dev):
            return (rin_ref.at[src_dev], rout_ref.at[src_dev], rup_ref.at[src_dev], rdown_ref.at[src_dev])

        my_slot = slot(me)
        own = shard(me)
        local = [pltpu.make_async_copy(own[w], my_slot[w], local_sems.at[w]) for w in range(n_w)]
        for cp in local:
            cp.start()
        sends = []
        for k in PEER_ORDER:
            peer, pid = _peer(x, y, c, k)
            theirs = shard(pid)
            for w in range(n_w):
                cp = pltpu.make_async_remote_copy(src_ref=theirs[w], dst_ref=my_slot[w], send_sem=send_sems.at[k - 1, w],
                                                  recv_sem=recv_sems.at[k - 1, w], device_id=peer, device_id_type=MESH)
                cp.start()
                sends.append(cp)
        for k in PEER_ORDER:
            peer, pid = _peer(x, y, c, k)
            from_peer = slot(pid)
            for w in range(n_w):
                pltpu.make_async_remote_copy(src_ref=own[w], dst_ref=from_peer[w], send_sem=send_sems.at[k - 1, w],
                                             recv_sem=recv_sems.at[k - 1, w], device_id=peer, device_id_type=MESH).wait_recv()
        for cp in sends:
            cp.wait_send()
        for cp in local:
            cp.wait()

    anyspec = pl.BlockSpec(memory_space=pl.ANY)
    outs = [jax.ShapeDtypeStruct((N_DEV, depth, cin, d), BF), jax.ShapeDtypeStruct((N_DEV, depth, rout, d), BF),
            jax.ShapeDtypeStruct((N_DEV, depth, d, hs), BF), jax.ShapeDtypeStruct((N_DEV, depth, hs, d), BF)]
    return pl.pallas_call(
        body, in_specs=[anyspec] * 4, out_specs=[anyspec] * 4, out_shape=outs,
        scratch_shapes=[pltpu.SemaphoreType.DMA((N_DEV - 1, n_w)), pltpu.SemaphoreType.DMA((N_DEV - 1, n_w)),
                        pltpu.SemaphoreType.DMA((n_w,))],
        name="reduce_scatter_grads",
    )(g_in, g_out, g_up, g_down)


def _all_reduce_small(vec):
    rows = vec.shape[0]

    def body(v_ref, o_ref, buf_ref, send_sems, recv_sems):
        x, y, c, me = _my_position()
        buf_ref[me] = v_ref[...]
        sends = []
        for k in PEER_ORDER:
            peer, _ = _peer(x, y, c, k)
            cp = pltpu.make_async_remote_copy(src_ref=v_ref, dst_ref=buf_ref.at[me], send_sem=send_sems.at[k - 1],
                                              recv_sem=recv_sems.at[k - 1], device_id=peer, device_id_type=MESH)
            cp.start()
            sends.append(cp)
        for k in PEER_ORDER:
            peer, pid = _peer(x, y, c, k)
            pltpu.make_async_remote_copy(src_ref=v_ref, dst_ref=buf_ref.at[pid], send_sem=send_sems.at[k - 1],
                                         recv_sem=recv_sems.at[k - 1], device_id=peer, device_id_type=MESH).wait_recv()
        for cp in sends:
            cp.wait_send()
        total = buf_ref[0]
        for dev in range(1, N_DEV):
            total = total + buf_ref[dev]
        o_ref[...] = total

    vm = pl.BlockSpec(memory_space=pltpu.VMEM)
    return pl.pallas_call(
        body, in_specs=[vm], out_specs=vm, out_shape=jax.ShapeDtypeStruct(vec.shape, F32),
        scratch_shapes=[pltpu.VMEM((N_DEV, rows, LANES), F32), pltpu.SemaphoreType.DMA((N_DEV - 1,)),
                        pltpu.SemaphoreType.DMA((N_DEV - 1,))],
        name="all_reduce_small",
    )(vec)


def _fwd_inproj(layer, x, gain, g_in, cos, sin, tm=512):
    s, d = x.shape

    def body(x_ref, gain_ref, w_ref, cos_ref, sin_ref, proj_ref, h_ref):
        j = pl.program_id(1)

        @pl.when(j == 0)
        def _():
            h_ref[...] = _rms_fwd(x_ref[...], gain_ref[...]).astype(BF)

        acc = _dot(h_ref[...], w_ref[...], NT)

        @pl.when(j < 2)
        def _():
            cs = _tile_lanes(cos_ref[...], SEG // LANES)
            sn = _tile_lanes(sin_ref[...], SEG // LANES)
            rot = acc * cs + _rope_partner(acc) * sn
            proj_ref[...] = jnp.where(j == 0, rot * ATTN_SCALE, rot)

        @pl.when(j >= 2)
        def _():
            proj_ref[...] = acc

    return pl.pallas_call(
        body, grid=(s // tm, N_SEG),
        in_specs=[pl.BlockSpec((tm, d), lambda i, j: (i, 0)), pl.BlockSpec((None, 1, d), lambda i, j: (layer, 0, 0)),
                  pl.BlockSpec((None, SEG, d), lambda i, j: (layer, j, 0)),
                  pl.BlockSpec((tm, LANES), lambda i, j: (i, 0)), pl.BlockSpec((tm, LANES), lambda i, j: (i, 0))],
        out_specs=[pl.BlockSpec((tm, SEG), lambda i, j: (i, j)), pl.BlockSpec((tm, d), lambda i, j: (i, 0))],
        out_shape=[jax.ShapeDtypeStruct((s, PROJ_W), F32), jax.ShapeDtypeStruct((s, d), BF)],
        compiler_params=_cparams(("parallel", "arbitrary")), name=f"fwd_inproj_l{layer}",
    )(x, gain, g_in, cos, sin)


def _attn_masks(n):
    row = lax.broadcasted_iota(jnp.int32, (SPAN, SPAN), 0)
    col = lax.broadcasted_iota(jnp.int32, (SPAN, SPAN), 1)
    return col <= row, (col >= row) & (n > 0), col < 64


def _attn_specs(dil, n_in_extra):
    unit = SPAN * dil
    pairs = ATTN_W // LANES
    q_spec = pl.BlockSpec((unit, LANES), lambda p, n: (n, p))
    kp_spec = pl.BlockSpec((unit, LANES), lambda p, n: (jnp.maximum(n - 1, 0), pairs + p))
    kc_spec = pl.BlockSpec((unit, LANES), lambda p, n: (n, pairs + p))
    vp_spec = pl.BlockSpec((unit, LANES), lambda p, n: (jnp.maximum(n - 1, 0), 2 * pairs + p))
    vc_spec = pl.BlockSpec((unit, LANES), lambda p, n: (n, 2 * pairs + p))
    return [q_spec, kp_spec, kc_spec, vp_spec, vc_spec] + [q_spec] * n_in_extra


def _rows(dil, r):
    return pl.ds(r, SPAN, stride=dil) if dil > 1 else slice(None)


def _attn_fwd(layer, dil, proj):
    s = proj.shape[0]
    unit = SPAN * dil

    def body(q_ref, kp_ref, kc_ref, vp_ref, vc_ref, o_ref, lse_ref):
        m_cur, m_prev, is_a = _attn_masks(pl.program_id(1))
        for r in range(dil):
            rows = _rows(dil, r)
            q = q_ref[rows, :]
            kc, kp = kc_ref[rows, :].astype(BF), kp_ref[rows, :].astype(BF)
            vc, vp = vc_ref[rows, :].astype(BF), vp_ref[rows, :].astype(BF)
            halves = []
            for sel in (is_a, jnp.logical_not(is_a)):
                qh = jnp.where(sel, q, 0.0).astype(BF)
                s_c = jnp.where(m_cur, _dot(qh, kc, NT), MASK_VALUE)
                s_p = jnp.where(m_prev, _dot(qh, kp, NT), MASK_VALUE)
                mx = jnp.maximum(jnp.max(s_c, axis=-1, keepdims=True), jnp.max(s_p, axis=-1, keepdims=True))
                p_c, p_p = jnp.exp(s_c - mx), jnp.exp(s_p - mx)
                den = jnp.sum(p_c, axis=-1, keepdims=True) + jnp.sum(p_p, axis=-1, keepdims=True)
                o = (_dot(p_c.astype(BF), vc, NN) + _dot(p_p.astype(BF), vp, NN)) / den
                halves.append((o, mx + jnp.log(den)))
            o_ref[rows, :] = jnp.where(is_a, halves[0][0], halves[1][0])
            lse_ref[rows, :] = jnp.where(is_a, halves[0][1], halves[1][1])

    out_spec = pl.BlockSpec((unit, LANES), lambda p, n: (n, p))
    return pl.pallas_call(
        body, grid=(ATTN_W // LANES, s // unit), in_specs=_attn_specs(dil, 0), out_specs=[out_spec, out_spec],
        out_shape=[jax.ShapeDtypeStruct((s, ATTN_W), F32)] * 2,
        compiler_params=_cparams(("parallel", "arbitrary")), name=f"attn_fwd_d{dil}_l{layer}",
    )(proj, proj, proj, proj, proj)


def _attn_merge(layer, outs, lses, gain, tm=512):
    s = outs[0].shape[0]

    def body(o1, o2, o3, l1, l2, l3, gain_ref, o_ref, lse_ref, n_ref):
        ls = (l1[...], l2[...], l3[...])
        mx = jnp.maximum(jnp.maximum(ls[0], ls[1]), ls[2])
        ws = [jnp.exp(l - mx) for l in ls]
        den = ws[0] + ws[1] + ws[2]
        o = (ws[0] * o1[...] + ws[1] * o2[...] + ws[2] * o3[...]) / den
        o_ref[...] = o
        lse_ref[...] = mx + jnp.log(den)
        n_ref[...] = _rms_fwd(o, gain_ref[...]).astype(BF)

    blk = pl.BlockSpec((tm, ATTN_W), lambda i: (i, 0))
    return pl.pallas_call(
        body, grid=(s // tm,), in_specs=[blk] * 6 + [pl.BlockSpec((None, 1, ATTN_W), lambda i: (layer, 0, 0))],
        out_specs=[blk, blk, blk],
        out_shape=[jax.ShapeDtypeStruct((s, ATTN_W), F32), jax.ShapeDtypeStruct((s, ATTN_W), F32),
                   jax.ShapeDtypeStruct((s, ATTN_W), BF)],
        compiler_params=_cparams(("parallel",)), name=f"attn_merge_l{layer}",
    )(*outs, *lses, gain)


def _chunk_cumsum(x, reverse=False):
    n = x.shape[0]
    pos = lax.broadcasted_iota(jnp.int32, x.shape, 0) % HGRN_CHUNK
    for sh in (1, 2, 4, 8):
        if reverse:
            x = x + jnp.where(pos < HGRN_CHUNK - sh, pltpu.roll(x, n - sh, 0), 0.0)
        else:
            x = x + jnp.where(pos >= sh, pltpu.roll(x, sh, 0), 0.0)
    return x


def _chunk_row(x, row):
    r, n = x.shape
    x3 = x.reshape(r // HGRN_CHUNK, HGRN_CHUNK, n)
    return jnp.broadcast_to(x3[:, row:row + 1, :], x3.shape).reshape(r, n)


def _hgrn_pre(qh, z, lb):
    sig = _sigmoid(z)
    f = lb + (1.0 - lb) * sig
    k = (1.0 - lb) * _sigmoid(-z)
    sq = _sigmoid(qh)
    q = qh * sq * HGRN_SCALE
    g = _chunk_cumsum(jnp.log(f))
    g_mid = _chunk_row(g, HGRN_CHUNK // 2 - 1)
    g_last = _chunk_row(g, HGRN_CHUNK - 1)
    e_q, e_k = jnp.exp(g - g_mid), jnp.exp(g_mid - g)
    e_in, e_out = jnp.exp(g), jnp.exp(g_last - g)
    return dict(sig=sig, f=f, k=k, sq=sq, q=q, g_last=g_last, e_q=e_q, e_k=e_k, e_in=e_in, e_out=e_out,
                qt=q * e_q, kt=k * e_k, qg=q * e_in, kout=k * e_out)


def _hgrn_mask():
    row = lax.broadcasted_iota(jnp.int32, (LANES, LANES), 0)
    col = lax.broadcasted_iota(jnp.int32, (LANES, LANES), 1)
    return (row // HGRN_CHUNK == col // HGRN_CHUNK) & (col <= row)


def _hgrn_in_specs(layer, rev, nblk):
    def blk(b):
        return nblk - 1 - b if rev else b
    base = 3 * (ATTN_W // LANES)
    per = HGRN_W // LANES
    specs = [pl.BlockSpec((LANES, LANES), functools.partial(lambda h, b, seg: (blk(b), base + seg * per + h), seg=seg))
             for seg in range(4)]
    specs.append(pl.BlockSpec((None, 1, LANES), lambda h, b: (layer, 0, h)))
    specs.append(pl.BlockSpec((None, 1, LANES), lambda h, b: (layer, 0, 0)))
    return specs, blk


def _hgrn_fwd(layer, proj, lb, gain):
    s = proj.shape[0]
    nblk = s // LANES
    cpb = LANES // HGRN_CHUNK

    def body(q_ref, f_ref, i_ref, g_ref, lb_ref, gain_ref, o_ref, rec_ref, st_ref, state):
        @pl.when(pl.program_id(1) == 0)
        def _():
            state[...] = jnp.zeros(state.shape, F32)

        pre = _hgrn_pre(q_ref[...], f_ref[...], lb_ref[...])
        v = i_ref[...].astype(BF)
        a = jnp.where(_hgrn_mask(), _dot(pre["qt"].astype(BF), pre["kt"].astype(BF), NT), 0.0)
        o = _dot(a.astype(BF), v, NN)
        qg, kout = pre["qg"].astype(BF), pre["kout"].astype(BF)
        dec = jnp.exp(pre["g_last"])
        st = state[...]
        inter = []
        for c in range(cpb):
            rows = slice(c * HGRN_CHUNK, (c + 1) * HGRN_CHUNK)
            st_ref[c * LANES:(c + 1) * LANES, :] = st
            inter.append(_dot(qg[rows], st.astype(BF), NT))
            st = st * dec[c * HGRN_CHUNK:c * HGRN_CHUNK + 1, :] + _dot(v[rows], kout[rows], TN)
        state[...] = st
        o = o + jnp.concatenate(inter, axis=0)
        o_ref[...] = o
        gate = g_ref[...]
        rec_ref[...] = (_rms_fwd(o, gain_ref[...]) * (gate * _sigmoid(gate))).astype(BF)

    specs, _ = _hgrn_in_specs(layer, False, nblk)
    blk = pl.BlockSpec((LANES, LANES), lambda h, b: (b, h))
    return pl.pallas_call(
        body, grid=(HGRN_HEADS, nblk), in_specs=specs,
        out_specs=[blk, blk, pl.BlockSpec((None, cpb * LANES, LANES), lambda h, b: (h, b, 0))],
        out_shape=[jax.ShapeDtypeStruct((s, HGRN_W), F32), jax.ShapeDtypeStruct((s, HGRN_W), BF),
                   jax.ShapeDtypeStruct((HGRN_HEADS, nblk * cpb * LANES, LANES), F32)],
        scratch_shapes=[pltpu.VMEM((LANES, LANES), F32)],
        compiler_params=_cparams(("parallel", "arbitrary")), name=f"hgrn_fwd_l{layer}",
    )(proj, proj, proj, proj, lb, gain)


def _fwd_outproj(layer, mixed, g_out, x, tm=512, tn=512, tk=512):
    s, d = x.shape
    kdim = mixed.shape[1]
    tn = min(tn, d)

    def compute(ins, k):
        return _dot(ins[0][...], ins[1][...], NN)

    def finish(acc, ins, outs):
        outs[0][...] = ins[2][...] + acc

    return _mm(f"fwd_outproj_l{layer}", (s // tm, d // tn, kdim // tk), [mixed, g_out, x],
               [pl.BlockSpec((tm, tk), lambda i, j, k: (i, k)), pl.BlockSpec((None, tk, tn), lambda i, j, k: (layer, k, j)),
                pl.BlockSpec((tm, tn), lambda i, j, k: (i, j))],
               [jax.ShapeDtypeStruct((s, d), F32)], [pl.BlockSpec((tm, tn), lambda i, j, k: (i, j))],
               compute, finish, (tm, tn), ("parallel", "parallel", "arbitrary"))[0]


def _fwd_up(layer, x, gain, g_up, tm=512):
    s, d = x.shape
    hs = g_up.shape[3]

    def body(x_ref, gain_ref, w_ref, u_ref, a_ref, h_ref):
        @pl.when(pl.program_id(1) == 0)
        def _():
            h_ref[...] = _rms_fwd(x_ref[...], gain_ref[...]).astype(BF)

        u = _dot(h_ref[...], w_ref[...], NN)
        u_ref[...] = u.astype(BF)
        a_ref[...] = jnp.square(jnp.maximum(u, 0.0)).astype(BF)

    return pl.pallas_call(
        body, grid=(s // tm, N_DEV),
        in_specs=[pl.BlockSpec((tm, d), lambda i, j: (i, 0)), pl.BlockSpec((None, 1, d), lambda i, j: (layer, 0, 0)),
                  pl.BlockSpec((None, None, d, hs), lambda i, j: (layer, j, 0, 0))],
        out_specs=[pl.BlockSpec((tm, hs), lambda i, j: (i, j)), pl.BlockSpec((tm, hs), lambda i, j: (i, j)),
                   pl.BlockSpec((tm, d), lambda i, j: (i, 0))],
        out_shape=[jax.ShapeDtypeStruct((s, N_DEV * hs), BF), jax.ShapeDtypeStruct((s, N_DEV * hs), BF),
                   jax.ShapeDtypeStruct((s, d), BF)],
        compiler_params=_cparams(("parallel", "arbitrary")), name=f"fwd_up_l{layer}",
    )(x, gain, g_up)


def _fwd_down(layer, a, g_down, x, tm=512, tn=512, tk=1024):
    s, d = x.shape
    kdim = a.shape[1]
    tn = min(tn, d)

    def compute(ins, k):
        return _dot(ins[0][...], ins[1][...], NN)

    def finish(acc, ins, outs):
        outs[0][...] = ins[2][...] + acc

    return _mm(f"fwd_down_l{layer}", (s // tm, d // tn, kdim // tk), [a, g_down, x],
               [pl.BlockSpec((tm, tk), lambda i, j, k: (i, k)), pl.BlockSpec((None, tk, tn), lambda i, j, k: (layer, k, j)),
                pl.BlockSpec((tm, tn), lambda i, j, k: (i, j))],
               [jax.ShapeDtypeStruct((s, d), F32)], [pl.BlockSpec((tm, tn), lambda i, j, k: (i, j))],
               compute, finish, (tm, tn), ("parallel", "parallel", "arbitrary"))[0]


def _loss_head(x, gain, target, tm=512):
    s, d = x.shape

    def body(x_ref, gain_ref, t_ref, dx_ref, dgain_ref, loss_ref):
        i = pl.program_id(0)
        xv, gv = x_ref[...], gain_ref[...]
        err = _rms_fwd(xv, gv) - t_ref[...]
        dx, dgain = _rms_bwd(err * (1.0 / d), xv, gv)
        dx_ref[...] = dx
        part = _part8(dgain)
        lpart = _part8(0.5 * jnp.mean(err * err, axis=-1, keepdims=True) * jnp.ones((1, LANES), F32))

        @pl.when(i == 0)
        def _():
            dgain_ref[...] = part
            loss_ref[...] = lpart

        @pl.when(i > 0)
        def _():
            dgain_ref[...] += part
            loss_ref[...] += lpart

    return pl.pallas_call(
        body, grid=(s // tm,),
        in_specs=[pl.BlockSpec((tm, d), lambda i: (i, 0)), pl.BlockSpec((1, d), lambda i: (0, 0)),
                  pl.BlockSpec((tm, d), lambda i: (i, 0))],
        out_specs=[pl.BlockSpec((tm, d), lambda i: (i, 0)), pl.BlockSpec((8, d), lambda i: (0, 0)),
                   pl.BlockSpec((8, LANES), lambda i: (0, 0))],
        out_shape=[jax.ShapeDtypeStruct((s, d), F32), jax.ShapeDtypeStruct((8, d), F32), jax.ShapeDtypeStruct((8, LANES), F32)],
        compiler_params=_cparams(("arbitrary",)), name="loss_head",
    )(x, gain, target)


def _bwd_down(layer, dx, g_down, u, tm=512, tn=512):
    s, d = dx.shape
    hd = u.shape[1]

    def compute(ins, k):
        return _dot(ins[0][...].astype(BF), ins[1][...], NT)

    def finish(acc, ins, outs):
        outs[0][...] = (acc * (2.0 * jnp.maximum(ins[2][...].astype(F32), 0.0))).astype(BF)

    return _mm(f"bwd_down_l{layer}", (s // tm, hd // tn, 1), [dx, g_down, u],
               [pl.BlockSpec((tm, d), lambda i, j, k: (i, 0)), pl.BlockSpec((None, tn, d), lambda i, j, k: (layer, j, 0)),
                pl.BlockSpec((tm, tn), lambda i, j, k: (i, j))],
               [jax.ShapeDtypeStruct((s, hd), BF)], [pl.BlockSpec((tm, tn), lambda i, j, k: (i, j))],
               compute, finish, (tm, tn), ("parallel", "parallel", "arbitrary"))[0]


def _bwd_up(layer, du, g_up, x, gain, dres, tm=256):
    s, d = x.shape
    hs = g_up.shape[3]

    def compute(ins, k):
        return _dot(ins[0][...], ins[1][...], NT)

    def finish(acc, ins, outs):
        dx, dgain = _rms_bwd(acc, ins[2][...], ins[3][...])
        outs[0][...] = ins[4][...] + dx
        part = _part8(dgain)

        @pl.when(pl.program_id(0) == 0)
        def _():
            outs[1][...] = part

        @pl.when(pl.program_id(0) > 0)
        def _():
            outs[1][...] += part

    return _mm(f"bwd_up_l{layer}", (s // tm, 1, N_DEV), [du, g_up, x, gain, dres],
               [pl.BlockSpec((tm, hs), lambda i, j, k: (i, k)), pl.BlockSpec((None, None, d, hs), lambda i, j, k: (layer, k, 0, 0)),
                pl.BlockSpec((tm, d), lambda i, j, k: (i, 0)), pl.BlockSpec((None, 1, d), lambda i, j, k: (layer, 0, 0)),
                pl.BlockSpec((tm, d), lambda i, j, k: (i, 0))],
               [jax.ShapeDtypeStruct((s, d), F32), jax.ShapeDtypeStruct((8, d), F32)],
               [pl.BlockSpec((tm, d), lambda i, j, k: (i, 0)), pl.BlockSpec((8, d), lambda i, j, k: (0, 0))],
               compute, finish, (tm, d), ("arbitrary", "arbitrary", "arbitrary"))


def _bwd_outproj(layer, dx, g_out, tm=512, tn=512):
    s, d = dx.shape
    mw = g_out.shape[1]

    def compute(ins, k):
        return _dot(ins[0][...].astype(BF), ins[1][...], NT)

    def finish(acc, ins, outs):
        outs[0][...] = acc

    return _mm(f"bwd_outproj_l{layer}", (s // tm, mw // tn, 1), [dx, g_out],
               [pl.BlockSpec((tm, d), lambda i, j, k: (i, 0)), pl.BlockSpec((None, tn, d), lambda i, j, k: (layer, j, 0))],
               [jax.ShapeDtypeStruct((s, mw), F32)], [pl.BlockSpec((tm, tn), lambda i, j, k: (i, j))],
               compute, finish, (tm, tn), ("parallel", "parallel", "arbitrary"))[0]


def _attn_norm_bwd(layer, dmixed, o, gain, tm=512):
    s = o.shape[0]

    def body(dm_ref, o_ref, gain_ref, do_ref, delta_ref, dgain_ref):
        i = pl.program_id(0)
        ov = o_ref[...]
        do, dgain = _rms_bwd(dm_ref[...], ov, gain_ref[...])
        do_ref[...] = do
        row = lax.broadcasted_iota(jnp.int32, (ATTN_W, ATTN_W), 0)
        col = lax.broadcasted_iota(jnp.int32, (ATTN_W, ATTN_W), 1)
        same_head = jnp.where(row // 64 == col // 64, 1.0, 0.0)
        delta_ref[...] = jnp.dot(do * ov, same_head, precision=lax.Precision.HIGHEST, preferred_element_type=F32)
        part = _part8(dgain)

        @pl.when(i == 0)
        def _():
            dgain_ref[...] = part

        @pl.when(i > 0)
        def _():
            dgain_ref[...] += part

    blk = pl.BlockSpec((tm, ATTN_W), lambda i: (i, 0))
    return pl.pallas_call(
        body, grid=(s // tm,), in_specs=[blk, blk, pl.BlockSpec((None, 1, ATTN_W), lambda i: (layer, 0, 0))],
        out_specs=[blk, blk, pl.BlockSpec((8, ATTN_W), lambda i: (0, 0))],
        out_shape=[jax.ShapeDtypeStruct((s, ATTN_W), F32), jax.ShapeDtypeStruct((s, ATTN_W), F32),
                   jax.ShapeDtypeStruct((8, ATTN_W), F32)],
        compiler_params=_cparams(("arbitrary",)), name=f"attn_norm_bwd_l{layer}",
    )(dmixed, o, gain)


def _attn_bwd(layer, dil, proj, do, lse, delta):
    s = proj.shape[0]
    unit = SPAN * dil

    def body(q_ref, kp_ref, kc_ref, vp_ref, vc_ref, do_ref, lse_ref, delta_ref, dq_ref, dkc_ref, dkp_ref, dvc_ref, dvp_ref):
        m_cur, m_prev, is_a = _attn_masks(pl.program_id(1))
        for r in range(dil):
            rows = _rows(dil, r)
            q, dov = q_ref[rows, :], do_ref[rows, :]
            kc, kp = kc_ref[rows, :], kp_ref[rows, :]
            vc, vp = vc_ref[rows, :].astype(BF), vp_ref[rows, :].astype(BF)
            lse_v, delta_v = lse_ref[rows, :], delta_ref[rows, :]
            dq = jnp.zeros((SPAN, LANES), F32)
            dkc, dkp, dvc, dvp = dq, dq, dq, dq
            for half, sel in enumerate((is_a, jnp.logical_not(is_a))):
                col = slice(64 * half, 64 * half + 1)
                lse_h, delta_h = lse_v[:, col], delta_v[:, col]
                qh = jnp.where(sel, q, 0.0).astype(BF)
                doh = jnp.where(sel, dov, 0.0).astype(BF)
                for k_f32, v_bf, mask, which in ((kc, vc, m_cur, 0), (kp, vp, m_prev, 1)):
                    kh = jnp.where(sel, k_f32, 0.0).astype(BF)
                    sc = jnp.where(mask, _dot(qh, kh, NT), MASK_VALUE)
                    p = jnp.exp(sc - lse_h)
                    dv = _dot(p.astype(BF), doh, TN)
                    ds = (p * (_dot(doh, v_bf, NT) - delta_h)).astype(BF)
                    dq = dq + _dot(ds, kh, NN)
                    dk = _dot(ds, qh, TN)
                    if which == 0:
                        dkc, dvc = dkc + dk, dvc + dv
                    else:
                        dkp, dvp = dkp + dk, dvp + dv
            dq_ref[rows, :] = dq
            dkc_ref[rows, :] = dkc
            dkp_ref[rows, :] = dkp
            dvc_ref[rows, :] = dvc
            dvp_ref[rows, :] = dvp

    out_spec = pl.BlockSpec((unit, LANES), lambda p, n: (n, p))
    return pl.pallas_call(
        body, grid=(ATTN_W // LANES, s // unit), in_specs=_attn_specs(dil, 3), out_specs=[out_spec] * 5,
        out_shape=[jax.ShapeDtypeStruct((s, ATTN_W), F32)] * 5,
        compiler_params=_cparams(("parallel", "arbitrary")), name=f"attn_bwd_d{dil}_l{layer}",
    )(proj, proj, proj, proj, proj, do, lse, delta)


def _attn_combine(layer, parts, cos, sin):
    s = parts[0][0].shape[0]
    nblk = s // SPAN
    arrays, specs = [], []
    for dil, (dq, dkc, dkp, dvc, dvp) in zip(DILATIONS, parts):
        here = pl.BlockSpec((SPAN, ATTN_W), lambda i: (i, 0))
        ahead = pl.BlockSpec((SPAN, ATTN_W), functools.partial(lambda i, dil: (jnp.minimum(i + dil, nblk - 1), 0), dil=dil))
        arrays += [dq, dkc, dkp, dvc, dvp]
        specs += [here, here, ahead, here, ahead]
    tab = pl.BlockSpec((SPAN, LANES), lambda i: (i, 0))

    def body(*refs):
        cos_ref, sin_ref, out_ref = refs[15], refs[16], refs[17]
        i = pl.program_id(0)
        dq = dk = dv = jnp.zeros((SPAN, ATTN_W), F32)
        for p, dil in enumerate(DILATIONS):
            dq_r, dkc_r, dkp_r, dvc_r, dvp_r = refs[5 * p:5 * p + 5]
            has_next = i + dil < nblk
            dq = dq + dq_r[...]
            dk = dk + dkc_r[...] + jnp.where(has_next, dkp_r[...], 0.0)
            dv = dv + dvc_r[...] + jnp.where(has_next, dvp_r[...], 0.0)
        cs = _tile_lanes(cos_ref[...], ATTN_W // LANES)
        sn = _tile_lanes(sin_ref[...], ATTN_W // LANES)
        out_ref[0] = ((dq * cs - _rope_partner(dq) * sn) * ATTN_SCALE).astype(BF)
        out_ref[1] = (dk * cs - _rope_partner(dk) * sn).astype(BF)
        out_ref[2] = dv.astype(BF)

    return pl.pallas_call(
        body, grid=(nblk,), in_specs=specs + [tab, tab],
        out_specs=pl.BlockSpec((3, SPAN, ATTN_W), lambda i: (0, i, 0)),
        out_shape=jax.ShapeDtypeStruct((3, s, ATTN_W), BF),
        compiler_params=_cparams(("parallel",)), name=f"attn_combine_l{layer}",
    )(*arrays, cos, sin)


def _hgrn_bwd(layer, proj, lb, gain, o, dmixed, states):
    s = proj.shape[0]
    nblk = s // LANES
    cpb = LANES // HGRN_CHUNK

    def body(q_ref, f_ref, i_ref, g_ref, lb_ref, gain_ref, o_ref, drec_ref, st_ref, dseg_ref, dlb_ref, dgain_ref, dstate):
        head, step = pl.program_id(0), pl.program_id(1)

        @pl.when(step == 0)
        def _():
            dstate[...] = jnp.zeros(dstate.shape, F32)

        lbv, gv = lb_ref[...], gain_ref[...]
        qh, z, gate_in = q_ref[...], f_ref[...], g_ref[...]
        pre = _hgrn_pre(qh, z, lbv)
        v = i_ref[...].astype(BF)
        sg = _sigmoid(gate_in)
        gate = gate_in * sg
        ov, drec = o_ref[...], drec_ref[...]
        do, dgain = _rms_bwd(drec * gate, ov, gv)
        dgate_in = drec * _rms_fwd(ov, gv) * (sg * (1.0 + gate_in * (1.0 - sg)))
        mask = _hgrn_mask()
        qt, kt = pre["qt"].astype(BF), pre["kt"].astype(BF)
        do_b = do.astype(BF)
        a = jnp.where(mask, _dot(qt, kt, NT), 0.0).astype(BF)
        da = jnp.where(mask, _dot(do_b, v, NT), 0.0).astype(BF)
        dv = _dot(a, do_b, TN)
        dqt = _dot(da, kt, NN)
        dkt = _dot(da, qt, TN)
        qg, kout = pre["qg"].astype(BF), pre["kout"].astype(BF)
        dec = jnp.exp(pre["g_last"])
        dst = dstate[...]
        dqg_c, dkout_c, dv_c, dgl_c = [None] * cpb, [None] * cpb, [None] * cpb, [None] * cpb
        for c in reversed(range(cpb)):
            rows = slice(c * HGRN_CHUNK, (c + 1) * HGRN_CHUNK)
            st_prev = st_ref[c * LANES:(c + 1) * LANES, :]
            dec_c = dec[c * HGRN_CHUNK:c * HGRN_CHUNK + 1, :]
            dst_b = dst.astype(BF)
            dqg_c[c] = _dot(do_b[rows], st_prev.astype(BF), NN)
            dkout_c[c] = _dot(v[rows], dst_b, NN)
            dv_c[c] = _dot(kout[rows], dst_b, NT)
            dgl_c[c] = (jnp.sum(dst * st_prev, axis=0, keepdims=True) * dec_c
                        + jnp.sum(dkout_c[c] * pre["kout"][rows], axis=0, keepdims=True))
            dst = dst * dec_c + _dot(do_b[rows], qg[rows], TN)
        dstate[...] = dst
        dqg, dkout = jnp.concatenate(dqg_c, axis=0), jnp.concatenate(dkout_c, axis=0)
        dv = dv + jnp.concatenate(dv_c, axis=0)
        dgl = jnp.concatenate([jnp.broadcast_to(t, (HGRN_CHUNK, LANES)) for t in dgl_c], axis=0)
        pos = lax.broadcasted_iota(jnp.int32, (LANES, LANES), 0) % HGRN_CHUNK
        dq = dqt * pre["e_q"] + dqg * pre["e_in"]
        dk = dkt * pre["e_k"] + dkout * pre["e_out"]
        dg = (dqt * pre["qt"] - dkt * pre["kt"] + dqg * pre["qg"] - dkout * pre["kout"]
              + jnp.where(pos == HGRN_CHUNK - 1, dgl, 0.0))
        dlogf = _chunk_cumsum(dg, reverse=True)
        sig, sq = pre["sig"], pre["sq"]
        df = dlogf / pre["f"] - dk
        dseg_ref[0] = (dq * HGRN_SCALE * (sq * (1.0 + qh * (1.0 - sq)))).astype(BF)
        dseg_ref[1] = (df * (1.0 - lbv) * sig * (1.0 - sig)).astype(BF)
        dseg_ref[2] = dv.astype(BF)
        dseg_ref[3] = dgate_in.astype(BF)
        dlb_part = _part8(df * (1.0 - sig))
        dgain_part = _part8(dgain)

        @pl.when(step == 0)
        def _():
            dlb_ref[...] = dlb_part

        @pl.when(step > 0)
        def _():
            dlb_ref[...] += dlb_part

        @pl.when((step == 0) & (head == 0))
        def _():
            dgain_ref[...] = dgain_part

        @pl.when((step > 0) | (head > 0))
        def _():
            dgain_ref[...] += dgain_part

    specs, blk = _hgrn_in_specs(layer, True, nblk)
    per = HGRN_W // LANES
    specs += [pl.BlockSpec((LANES, LANES), lambda h, b: (blk(b), h)),
              pl.BlockSpec((LANES, LANES), lambda h, b: (blk(b), per + h)),
              pl.BlockSpec((None, cpb * LANES, LANES), lambda h, b: (h, blk(b), 0))]
    return pl.pallas_call(
        body, grid=(HGRN_HEADS, nblk), in_specs=specs,
        out_specs=[pl.BlockSpec((4, LANES, LANES), lambda h, b: (0, blk(b), h)),
                   pl.BlockSpec((8, LANES), lambda h, b: (0, h)), pl.BlockSpec((8, LANES), lambda h, b: (0, 0))],
        out_shape=[jax.ShapeDtypeStruct((4, s, HGRN_W), BF), jax.ShapeDtypeStruct((8, HGRN_W), F32),
                   jax.ShapeDtypeStruct((8, LANES), F32)],
        scratch_shapes=[pltpu.VMEM((LANES, LANES), F32)],
        compiler_params=_cparams(("arbitrary", "arbitrary")), name=f"hgrn_bwd_l{layer}",
    )(proj, proj, proj, proj, lb, gain, o, dmixed, states)


def _bwd_inproj(layer, dqkv, dhg, g_in, x, gain, dres, tm=256):
    s, d = x.shape

    def body(dqkv_ref, dhg_ref, w_ref, x_ref, gain_ref, dres_ref, dx_ref, dgain_ref):
        i = pl.program_id(0)
        acc = jnp.zeros((tm, d), F32)
        for seg in range(N_SEG):
            a = dqkv_ref[seg] if seg < 3 else dhg_ref[seg - 3]
            acc = acc + _dot(a, w_ref[seg * SEG:(seg + 1) * SEG, :], NN)
        dx, dgain = _rms_bwd(acc, x_ref[...], gain_ref[...])
        dx_ref[...] = dres_ref[...] + dx
        part = _part8(dgain)

        @pl.when(i == 0)
        def _():
            dgain_ref[...] = part

        @pl.when(i > 0)
        def _():
            dgain_ref[...] += part

    row = pl.BlockSpec((tm, d), lambda i: (i, 0))
    return pl.pallas_call(
        body, grid=(s // tm,),
        in_specs=[pl.BlockSpec((3, tm, SEG), lambda i: (0, i, 0)), pl.BlockSpec((4, tm, SEG), lambda i: (0, i, 0)),
                  pl.BlockSpec((None, PROJ_W, d), lambda i: (layer, 0, 0)), row,
                  pl.BlockSpec((None, 1, d), lambda i: (layer, 0, 0)), row],
        out_specs=[row, pl.BlockSpec((8, d), lambda i: (0, 0))],
        out_shape=[jax.ShapeDtypeStruct((s, d), F32), jax.ShapeDtypeStruct((8, d), F32)],
        compiler_params=_cparams(("arbitrary",)), name=f"bwd_inproj_l{layer}",
    )(dqkv, dhg, g_in, x, gain, dres)


def _adamw(w, g, m, v):
    m2 = ADAM_B1 * m + (1.0 - ADAM_B1) * g
    v2 = ADAM_B2 * v + (1.0 - ADAM_B2) * (g * g)
    m_hat = m2 / (1.0 - ADAM_B1 ** ADAM_STEP)
    v_hat = v2 / (1.0 - ADAM_B2 ** ADAM_STEP)
    delta = -ADAM_LR * (m_hat / (jnp.sqrt(v_hat) + ADAM_EPS) + ADAM_WD * w)
    return delta, m2, v2


def _adam_big(name, parts, w, m, v, transpose, row_tiles):
    depth = w.shape[0]
    r, c = parts.shape[2], parts.shape[3]
    if transpose:
        tc = c // row_tiles
        p_spec = pl.BlockSpec((N_DEV, None, r, tc), lambda l, t: (0, l, 0, t))
        w_spec = pl.BlockSpec((None, tc, r), lambda l, t: (l, t, 0))
    else:
        tr = r // row_tiles
        p_spec = pl.BlockSpec((N_DEV, None, tr, c), lambda l, t: (0, l, t, 0))
        w_spec = pl.BlockSpec((None, tr, c), lambda l, t: (l, t, 0))

    def body(p_ref, w_ref, m_ref, v_ref, g_ref, d_ref, m2_ref, v2_ref):
        g = p_ref[0].astype(F32)
        for dev in range(1, N_DEV):
            g = g + p_ref[dev].astype(F32)
        if transpose:
            g = g.T
        delta, m2, v2 = _adamw(w_ref[...], g, m_ref[...], v_ref[...])
        g_ref[...] = g
        d_ref[...] = delta
        m2_ref[...] = m2
        v2_ref[...] = v2

    return pl.pallas_call(
        body, grid=(depth, row_tiles), in_specs=[p_spec, w_spec, w_spec, w_spec], out_specs=[w_spec] * 4,
        out_shape=[jax.ShapeDtypeStruct(w.shape, F32)] * 4,
        compiler_params=_cparams(("parallel", "parallel")), name=name,
    )(parts, w, m, v)


def _adam_small(g, w, m, v):
    def body(g_ref, w_ref, m_ref, v_ref, d_ref, m2_ref, v2_ref):
        delta, m2, v2 = _adamw(w_ref[...], g_ref[...], m_ref[...], v_ref[...])
        d_ref[...] = delta
        m2_ref[...] = m2
        v2_ref[...] = v2

    vm = pl.BlockSpec(memory_space=pltpu.VMEM)
    return pl.pallas_call(body, in_specs=[vm] * 4, out_specs=[vm] * 3, out_shape=[jax.ShapeDtypeStruct(g.shape, F32)] * 3,
                          name="adam_small")(g, w, m, v)


def _lower_bounds(logits):
    def body(l_ref, lb_ref, jac_ref):
        l0, l1 = l_ref[0:1, :], l_ref[1:2, :]
        mx = jnp.maximum(l0, l1)
        e0, e1 = jnp.exp(l0 - mx), jnp.exp(l1 - mx)
        p0, p1 = e0 / (e0 + e1), e1 / (e0 + e1)
        lb_ref[0:1, :] = p0 - p0
        lb_ref[1:2, :] = (p0 + p1) - p0
        jac_ref[0:1, :] = -p0 * p1
        jac_ref[1:2, :] = p0 * p1

    vm = pl.BlockSpec(memory_space=pltpu.VMEM)
    return pl.pallas_call(body, in_specs=[vm], out_specs=[vm, vm], out_shape=[jax.ShapeDtypeStruct(logits.shape, F32)] * 2,
                          name="hgrn_lower_bounds")(logits)


def _rope_tables(s):
    half = 32
    inv_freq = ROPE_THETA ** (-jnp.arange(half, dtype=F32) / half)
    ang = jnp.arange(s, dtype=jnp.int32).astype(F32)[:, None] * inv_freq[None, :]
    cos, sin = jnp.cos(ang), jnp.sin(ang)
    return jnp.concatenate([cos] * 4, axis=1), jnp.concatenate([-sin, sin, -sin, sin], axis=1)


SMALL_NAMES = ("norm_mix", "attn_out_gain", "hgrn_lb_logits", "hgrn_out_gain", "norm_mlp", "norm_final")


def _pack_small(vals):
    flat = jnp.concatenate([v.reshape(-1) for v in vals])
    rows = -(-flat.shape[0] // (8 * LANES)) * 8
    return jnp.pad(flat, (0, rows * LANES - flat.shape[0])).reshape(rows, LANES)


def _unpack_small(packed, like):
    flat, out, off = packed.reshape(-1), [], 0
    for v in like:
        out.append(flat[off:off + v.size].reshape(v.shape))
        off += v.size
    return out


def kernel(x, norm_mix, w_in, attn_out_gain, hgrn_lb_logits, hgrn_out_gain, w_out, norm_mlp, w_up, w_down, norm_final, loss_target, m_norm_mix, m_w_in, m_attn_out_gain, m_hgrn_lb_logits, m_hgrn_out_gain, m_w_out, m_norm_mlp, m_w_up, m_w_down, m_norm_final, v_norm_mix, v_w_in, v_attn_out_gain, v_hgrn_lb_logits, v_hgrn_out_gain, v_w_out, v_norm_mlp, v_w_up, v_w_down, v_norm_final):
    depth = w_in.shape[0]
    assert depth == 2 and x.shape[0] == 1
    s, d = x.shape[1], x.shape[2]
    x0 = x[0]
    target = loss_target[0]
    cos, sin = _rope_tables(s)
    g_mix, g_attn, g_hg, g_mlp = (norm_mix[:, None, :], attn_out_gain[:, None, :], hgrn_out_gain[:, None, :],
                                  norm_mlp[:, None, :])
    lb, lb_jac = _lower_bounds(hgrn_lb_logits)
    lb3 = lb[:, None, :]

    full_in, full_out, full_up, full_down = _all_gather_weights(*_pack_weights(w_in, w_out, w_up, w_down))

    saved = []
    xl = x0
    for l in range(depth):
        proj, h = _fwd_inproj(l, xl, g_mix, full_in, cos, sin)
        fw = [_attn_fwd(l, dil, proj) for dil in DILATIONS]
        o_attn, lse, attn_n = _attn_merge(l, [f[0] for f in fw], [f[1] for f in fw], g_attn)
        o_hg, rec, states = _hgrn_fwd(l, proj, lb3, g_hg)
        mixed = jnp.concatenate([attn_n, rec], axis=1)
        x_mid = _fwd_outproj(l, mixed, full_out, xl)
        u, a, h2 = _fwd_up(l, x_mid, g_mlp, full_up)
        x_next = _fwd_down(l, a, full_down, x_mid)
        saved.append((xl, proj, h, o_attn, lse, o_hg, states, mixed, x_mid, u, a, h2))
        xl = x_next
    dx, dnorm_final8, loss8 = _loss_head(xl, norm_final[None, :], target)
    loss = lax.psum(jnp.sum(loss8[:, 0]), ("x", "y", "c"))

    gw_in, gw_out, gw_up, gw_down = [None] * depth, [None] * depth, [None] * depth, [None] * depth
    small = {}
    for l in reversed(range(depth)):
        xl, proj, h, o_attn, lse, o_hg, states, mixed, x_mid, u, a, h2 = saved[l]
        du = _bwd_down(l, dx, full_down, u)
        gw_down[l] = _mm_tn(f"grad_w_down_l{l}", a, dx, a.shape[1])
        dx_mid, dmlp8 = _bwd_up(l, du, full_up, x_mid, g_mlp, dx)
        gw_up[l] = _grad_w_up(l, h2, du)
        dmixed = _bwd_outproj(l, dx_mid, full_out)
        gw_out[l] = _mm_tn(f"grad_w_out_l{l}", mixed, dx_mid, mixed.shape[1])
        do, delta, dattn8 = _attn_norm_bwd(l, dmixed, o_attn, g_attn)
        parts = [_attn_bwd(l, dil, proj, do, lse, delta) for dil in DILATIONS]
        dqkv = _attn_combine(l, parts, cos, sin)
        dhg, dlb8, dhgain8 = _hgrn_bwd(l, proj, lb3, g_hg, o_hg, dmixed, states)
        dx, dmix8 = _bwd_inproj(l, dqkv, dhg, full_in, xl, g_mix, dx_mid)
        gin = _mm_tn(f"grad_w_in_qkv_l{l}", dqkv, h, PROJ_W, a_lead=True)
        gw_in[l] = _mm_tn(f"grad_w_in_hg_l{l}", dhg, h, PROJ_W, a_lead=True, out_block_off=3, prev=gin)
        small[l] = (dmix8, dattn8, dlb8, dhgain8, dmlp8)

    def fin(p8):
        return jnp.sum(p8, axis=0)
    dlogits = lb_jac * fin(small[1][2])[None, :]
    small_grads = [jnp.stack([fin(small[l][0]) for l in range(depth)]), jnp.stack([fin(small[l][1]) for l in range(depth)]),
                   dlogits, jnp.stack([fin(small[l][3]) for l in range(depth)]),
                   jnp.stack([fin(small[l][4]) for l in range(depth)]), fin(dnorm_final8)]
    small_w = [norm_mix, attn_out_gain, hgrn_lb_logits, hgrn_out_gain, norm_mlp, norm_final]
    small_m = [m_norm_mix, m_attn_out_gain, m_hgrn_lb_logits, m_hgrn_out_gain, m_norm_mlp, m_norm_final]
    small_v = [v_norm_mix, v_attn_out_gain, v_hgrn_lb_logits, v_hgrn_out_gain, v_norm_mlp, v_norm_final]
    g_small = _all_reduce_small(_pack_small(small_grads))
    d_small, m_small, v_small = _adam_small(g_small, _pack_small(small_w), _pack_small(small_m), _pack_small(small_v))
    gs, ds, ms, vs = (_unpack_small(t, small_w) for t in (g_small, d_small, m_small, v_small))

    r_in, r_out, r_up, r_down = _reduce_scatter_grads(jnp.stack(gw_in), jnp.stack(gw_out), jnp.stack(gw_up), jnp.stack(gw_down))
    big = {
        "w_in": _adam_big("adam_w_in", r_in, w_in, m_w_in, v_w_in, True, 2),
        "w_out": _adam_big("adam_w_out", r_out, w_out, m_w_out, v_w_out, False, 1),
        "w_up": _adam_big("adam_w_up", r_up, w_up, m_w_up, v_w_up, False, 2),
        "w_down": _adam_big("adam_w_down", r_down, w_down, m_w_down, v_w_down, False, 2),
    }

    def gather(idx, small_list):
        by_name = dict(zip(SMALL_NAMES, small_list))
        return [by_name["norm_mix"], big["w_in"][idx], by_name["attn_out_gain"], by_name["hgrn_lb_logits"],
                by_name["hgrn_out_gain"], big["w_out"][idx], by_name["norm_mlp"], big["w_up"][idx], big["w_down"][idx],
                by_name["norm_final"]]

    return (loss, dx[None], *gather(0, gs), *gather(1, ds), *gather(2, ms), *gather(3, vs))


def _grad_w_up(layer, h2, du, tm=512, tk=512):
    s, d = h2.shape
    hs = du.shape[1] // N_DEV
    tm = min(tm, d)

    def compute(ins, k):
        return _dot(ins[0][...], ins[1][...], TN)

    def finish(acc, ins, outs):
        outs[0][...] = acc.astype(BF)

    return _mm(f"grad_w_up_l{layer}", (d // tm, N_DEV, s // tk), [h2, du],
               [pl.BlockSpec((tk, tm), lambda i, j, k: (k, i)), pl.BlockSpec((tk, hs), lambda i, j, k: (k, j))],
               [jax.ShapeDtypeStruct((N_DEV, d, hs), BF)], [pl.BlockSpec((None, tm, hs), lambda i, j, k: (j, i, 0))],
               compute, finish, (tm, hs), ("parallel", "parallel", "arbitrary"))[0]
```

```python
import functools
import math

import jax
import jax.numpy as jnp
from jax import lax
from jax.experimental import pallas as pl
from jax.experimental.pallas import tpu as pltpu

F32 = jnp.float32
BF = jnp.bfloat16

N_DEV = 8
ATTN_W = 512
HGRN_W = 512
HGRN_HEADS = 4
HGRN_DIM = 128
SEG = 512
N_SEG = 7
PROJ_W = N_SEG * SEG
MIX_W = ATTN_W + HGRN_W
SPAN = 128
DILATIONS = (1, 4, 16)
HGRN_CHUNK = 16
ROPE_THETA = 10000.0
NORM_EPS = 1e-6
MASK_VALUE = -1e30
ATTN_SCALE = 0.125
HGRN_SCALE = HGRN_DIM ** -0.5
ADAM_LR = 0.001
ADAM_B1 = 0.9
ADAM_B2 = 0.999
ADAM_EPS = 1e-08
ADAM_WD = 0.01
ADAM_STEP = 10
LANES = 128
VMEM_LIMIT = 56 * 1024 * 1024

NN = ((1,), (0,))
NT = ((1,), (1,))
TN = ((0,), (0,))
MESH = pl.DeviceIdType.MESH


def _dot(a, b, dims):
    return lax.dot_general(a, b, (dims, ((), ())), preferred_element_type=F32)


def _cparams(sem):
    return pltpu.CompilerParams(dimension_semantics=sem, vmem_limit_bytes=VMEM_LIMIT)


def _part8(x):
    r, n = x.shape
    return jnp.sum(x.reshape(r // 8, 8, n), axis=0)


def _sigmoid(x):
    return 1.0 / (1.0 + jnp.exp(-x))


def _rms_fwd(x, gain):
    r = lax.rsqrt(jnp.mean(x * x, axis=-1, keepdims=True) + NORM_EPS)
    return x * r * gain


def _rms_bwd(dy, x, gain):
    r = lax.rsqrt(jnp.mean(x * x, axis=-1, keepdims=True) + NORM_EPS)
    xn = x * r
    dxn = dy * gain
    dx = r * (dxn - xn * jnp.mean(dxn * xn, axis=-1, keepdims=True))
    return dx, dy * xn


def _rope_partner(x):
    n = x.shape[-1]
    lane = lax.broadcasted_iota(jnp.int32, x.shape, x.ndim - 1)
    return jnp.where((lane % 64) < 32, pltpu.roll(x, n - 32, x.ndim - 1), pltpu.roll(x, 32, x.ndim - 1))


def _tile_lanes(t, reps):
    return jnp.concatenate([t] * reps, axis=-1)


def _mm_tn(name, a, b, out_rows, a_lead=False, out_block_off=0, prev=None, out_block_w=None, a_fn=None,
           tm=512, tn=1024, tk=1024):
    kdim, n = b.shape
    m = a.shape[-1]
    tm, tn, tk = min(tm, m), min(tn, n), min(tk, kdim)
    mt, nk = m // tm, kdim // tk
    n_lead = a.shape[0] if a_lead else 1
    if a_lead:
        a_spec = pl.BlockSpec((None, tk, tm), lambda i, j, k: (i // mt, k, i % mt))
    else:
        a_spec = pl.BlockSpec((tk, tm), lambda i, j, k: (k, i))
    b_spec = pl.BlockSpec((tk, tn), lambda i, j, k: (k, j))
    if out_block_w:
        nb = tn // out_block_w
        o_shape = jax.ShapeDtypeStruct((n // out_block_w, out_rows, out_block_w), BF)
        o_spec = pl.BlockSpec((nb, tm, out_block_w), lambda i, j, k: (j, i + out_block_off, 0))
    else:
        nb = 0
        o_shape = jax.ShapeDtypeStruct((out_rows, n), BF)
        o_spec = pl.BlockSpec((tm, tn), lambda i, j, k: (i + out_block_off, j))
    arrays, specs, aliases = [a, b], [a_spec, b_spec], {}
    if prev is not None:
        arrays.append(prev)
        specs.append(pl.BlockSpec(memory_space=pl.ANY))
        aliases = {2: 0}

    def body(*refs):
        a_ref, b_ref, o_ref, acc = refs[0], refs[1], refs[-2], refs[-1]
        k = pl.program_id(2)

        @pl.when(k == 0)
        def _():
            acc[...] = jnp.zeros(acc.shape, F32)

        av = a_ref[...]
        if a_fn is not None:
            av = a_fn(av)
        acc[...] += _dot(av, b_ref[...], TN)

        @pl.when(k == nk - 1)
        def _():
            if nb:
                for t in range(nb):
                    o_ref[t] = acc[:, t * out_block_w:(t + 1) * out_block_w].astype(BF)
            else:
                o_ref[...] = acc[...].astype(BF)

    return pl.pallas_call(
        body, grid=(n_lead * mt, n // tn, nk), in_specs=specs, out_specs=o_spec, out_shape=o_shape,
        scratch_shapes=[pltpu.VMEM((tm, tn), F32)], compiler_params=_cparams(("parallel", "parallel", "arbitrary")),
        name=name, input_output_aliases=aliases,
    )(*arrays)


def _pack_weights(w_in, w_out, w_up, w_down):
    depth, d, cin = w_in.shape

    def body(win_ref, wout_ref, wup_ref, wdown_ref, oin_ref, oout_ref, oup_ref, odown_ref):
        oin_ref[...] = win_ref[...].T.astype(BF)
        oout_ref[...] = wout_ref[...].astype(BF)
        oup_ref[...] = wup_ref[...].astype(BF)
        odown_ref[...] = wdown_ref[...].astype(BF)

    def spec(a):
        return pl.BlockSpec((None,) + a.shape[1:], lambda l: (l, 0, 0))

    outs = [jax.ShapeDtypeStruct((depth, cin, d), BF), jax.ShapeDtypeStruct(w_out.shape, BF),
            jax.ShapeDtypeStruct(w_up.shape, BF), jax.ShapeDtypeStruct(w_down.shape, BF)]
    return pl.pallas_call(
        body, grid=(depth,), in_specs=[spec(w_in), spec(w_out), spec(w_up), spec(w_down)],
        out_specs=[pl.BlockSpec((None, cin, d), lambda l: (l, 0, 0)), spec(w_out), spec(w_up), spec(w_down)],
        out_shape=outs, compiler_params=_cparams(("arbitrary",)), name="pack_weights",
    )(w_in, w_out, w_up, w_down)


def _my_position():
    x, y, c = lax.axis_index("x"), lax.axis_index("y"), lax.axis_index("c")
    return x, y, c, 4 * x + 2 * y + c


def _peer(x, y, c, k):
    px = 1 - x if k & 4 else x
    py = 1 - y if k & 2 else y
    pc = 1 - c if k & 1 else c
    return (px, py, pc), 4 * px + 2 * py + pc


PEER_ORDER = (1, 2, 4, 3, 5, 6, 7)


def _all_gather_weights(p_in, p_out, p_up, p_down):
    depth, cin, d = p_in.shape
    rout = p_out.shape[1]
    hs = p_up.shape[2]
    n_w = 4

    def body(in_ref, out_ref, up_ref, down_ref, gin_ref, gout_ref, gup_ref, gdown_ref, send_sems, recv_sems, local_sems):
        x, y, c, me = _my_position()

        def slots(dev):
            return (gin_ref.at[:, pl.ds(pl.multiple_of(dev * cin, 16), cin), :],
                    gout_ref.at[:, pl.ds(pl.multiple_of(dev * rout, 16), rout), :],
                    gup_ref.at[:, dev],
                    gdown_ref.at[:, pl.ds(pl.multiple_of(dev * hs, 16), hs), :])

        srcs = (in_ref, out_ref, up_ref, down_ref)
        mine = slots(me)
        local = [pltpu.make_async_copy(srcs[w], mine[w], local_sems.at[w]) for w in range(n_w)]
        for cp in local:
            cp.start()
        sends = []
        for k in PEER_ORDER:
            peer, _ = _peer(x, y, c, k)
            for w in range(n_w):
                cp = pltpu.make_async_remote_copy(src_ref=srcs[w], dst_ref=mine[w], send_sem=send_sems.at[k - 1, w],
                                                  recv_sem=recv_sems.at[k - 1, w], device_id=peer, device_id_type=MESH)
                cp.start()
                sends.append(cp)
        for k in PEER_ORDER:
            peer, pid = _peer(x, y, c, k)
            theirs = slots(pid)
            for w in range(n_w):
                pltpu.make_async_remote_copy(src_ref=srcs[w], dst_ref=theirs[w], send_sem=send_sems.at[k - 1, w],
                                             recv_sem=recv_sems.at[k - 1, w], device_id=peer, device_id_type=MESH).wait_recv()
        for cp in sends:
            cp.wait_send()
        for cp in local:
            cp.wait()

    anyspec = pl.BlockSpec(memory_space=pl.ANY)
    outs = [jax.ShapeDtypeStruct((depth, N_DEV * cin, d), BF), jax.ShapeDtypeStruct((depth, N_DEV * rout, d), BF),
            jax.ShapeDtypeStruct((depth, N_DEV, d, hs), BF), jax.ShapeDtypeStruct((depth, N_DEV * hs, d), BF)]
    return pl.pallas_call(
        body, in_specs=[anyspec] * 4, out_specs=[anyspec] * 4, out_shape=outs,
        scratch_shapes=[pltpu.SemaphoreType.DMA((N_DEV - 1, n_w)), pltpu.SemaphoreType.DMA((N_DEV - 1, n_w)),
                        pltpu.SemaphoreType.DMA((n_w,))],
        name="all_gather_weights",
    )(p_in, p_out, p_up, p_down)


def _reduce_scatter_grads(g_in, g_out, g_up, g_down):
    depth = len(g_in)
    pw, d = g_in[0].shape
    cin = pw // N_DEV
    rout = g_out[0].shape[0] // N_DEV
    hs = g_up[0].shape[2]
    n_w = 4 * depth

    def body(*refs):
        in_refs, out_refs, up_refs, down_refs = (refs[w * depth:(w + 1) * depth] for w in range(4))
        rin_ref, rout_ref, rup_ref, rdown_ref, send_sems, recv_sems, local_sems = refs[n_w:]
        x, y, c, me = _my_position()

        def shard(dev):
            return ([r.at[pl.ds(pl.multiple_of(dev * cin, 16), cin), :] for r in in_refs]
                    + [r.at[pl.ds(pl.multiple_of(dev * rout, 16), rout), :] for r in out_refs]
                    + [r.at[dev] for r in up_refs]
                    + [r.at[pl.ds(pl.multiple_of(dev * hs, 16), hs), :] for r in down_refs])

        def slot(src_dev):
            return [r.at[src_dev, l] for r in (rin_ref, rout_ref, rup_ref, rdown_ref) for l in range(depth)]

        my_slot = slot(me)
        own = shard(me)
        local = [pltpu.make_async_copy(own[w], my_slot[w], local_sems.at[w]) for w in range(n_w)]
        for cp in local:
            cp.start()
        sends = []
        for k in PEER_ORDER:
            peer, pid = _peer(x, y, c, k)
            theirs = shard(pid)
            for w in range(n_w):
                cp = pltpu.make_async_remote_copy(src_ref=theirs[w], dst_ref=my_slot[w], send_sem=send_sems.at[k - 1, w],
                                                  recv_sem=recv_sems.at[k - 1, w], device_id=peer, device_id_type=MESH)
                cp.start()
                sends.append(cp)
        for k in PEER_ORDER:
            peer, pid = _peer(x, y, c, k)
            from_peer = slot(pid)
            for w in range(n_w):
                pltpu.make_async_remote_copy(src_ref=own[w], dst_ref=from_peer[w], send_sem=send_sems.at[k - 1, w],
                                             recv_sem=recv_sems.at[k - 1, w], device_id=peer, device_id_type=MESH).wait_recv()
        for cp in sends:
            cp.wait_send()
        for cp in local:
            cp.wait()

    anyspec = pl.BlockSpec(memory_space=pl.ANY)
    outs = [jax.ShapeDtypeStruct((N_DEV, depth, cin, d), BF), jax.ShapeDtypeStruct((N_DEV, depth, rout, d), BF),
            jax.ShapeDtypeStruct((N_DEV, depth, d, hs), BF), jax.ShapeDtypeStruct((N_DEV, depth, hs, d), BF)]
    return pl.pallas_call(
        body, in_specs=[anyspec] * n_w, out_specs=[anyspec] * 4, out_shape=outs,
        scratch_shapes=[pltpu.SemaphoreType.DMA((N_DEV - 1, n_w)), pltpu.SemaphoreType.DMA((N_DEV - 1, n_w)),
                        pltpu.SemaphoreType.DMA((n_w,))],
        name="reduce_scatter_grads",
    )(*g_in, *g_out, *g_up, *g_down)


def _all_reduce_small(vec):
    rows = vec.shape[0]

    def body(v_ref, o_ref, buf_ref, send_sems, recv_sems):
        x, y, c, me = _my_position()
        buf_ref[me] = v_ref[...]
        sends = []
        for k in PEER_ORDER:
            peer, _ = _peer(x, y, c, k)
            cp = pltpu.make_async_remote_copy(src_ref=v_ref, dst_ref=buf_ref.at[me], send_sem=send_sems.at[k - 1],
                                              recv_sem=recv_sems.at[k - 1], device_id=peer, device_id_type=MESH)
            cp.start()
            sends.append(cp)
        for k in PEER_ORDER:
            peer, pid = _peer(x, y, c, k)
            pltpu.make_async_remote_copy(src_ref=v_ref, dst_ref=buf_ref.at[pid], send_sem=send_sems.at[k - 1],
                                         recv_sem=recv_sems.at[k - 1], device_id=peer, device_id_type=MESH).wait_recv()
        for cp in sends:
            cp.wait_send()
        total = buf_ref[0]
        for dev in range(1, N_DEV):
            total = total + buf_ref[dev]
        o_ref[...] = total

    vm = pl.BlockSpec(memory_space=pltpu.VMEM)
    return pl.pallas_call(
        body, in_specs=[vm], out_specs=vm, out_shape=jax.ShapeDtypeStruct(vec.shape, F32),
        scratch_shapes=[pltpu.VMEM((N_DEV, rows, LANES), F32), pltpu.SemaphoreType.DMA((N_DEV - 1,)),
                        pltpu.SemaphoreType.DMA((N_DEV - 1,))],
        name="all_reduce_small",
    )(vec)


def _resident(block_shape, index_map):
    return pl.BlockSpec(block_shape, index_map, pipeline_mode=pl.Buffered(1))


def _fwd_inproj(layer, x, gain, g_in, cos, sin, tm=256):
    s, d = x.shape

    def body(x_ref, gain_ref, w_ref, cos_ref, sin_ref, proj_ref, h_ref):
        h = _rms_fwd(x_ref[...], gain_ref[...]).astype(BF)
        h_ref[...] = h
        cs = _tile_lanes(cos_ref[...], SEG // LANES)
        sn = _tile_lanes(sin_ref[...], SEG // LANES)
        for seg in range(N_SEG):
            acc = _dot(h, w_ref[seg * SEG:(seg + 1) * SEG, :], NT)
            if seg < 2:
                acc = acc * cs + _rope_partner(acc) * sn
            if seg == 0:
                acc = acc * ATTN_SCALE
            proj_ref[:, seg * SEG:(seg + 1) * SEG] = acc

    return pl.pallas_call(
        body, grid=(s // tm,),
        in_specs=[pl.BlockSpec((tm, d), lambda i: (i, 0)), pl.BlockSpec((None, 1, d), lambda i: (layer, 0, 0)),
                  _resident((None, PROJ_W, d), lambda i: (layer, 0, 0)),
                  pl.BlockSpec((tm, LANES), lambda i: (i, 0)), pl.BlockSpec((tm, LANES), lambda i: (i, 0))],
        out_specs=[pl.BlockSpec((tm, PROJ_W), lambda i: (i, 0)), pl.BlockSpec((tm, d), lambda i: (i, 0))],
        out_shape=[jax.ShapeDtypeStruct((s, PROJ_W), F32), jax.ShapeDtypeStruct((s, d), BF)],
        compiler_params=_cparams(("parallel",)), name=f"fwd_inproj_l{layer}",
    )(x, gain, g_in, cos, sin)


def _attn_masks(n):
    row = lax.broadcasted_iota(jnp.int32, (SPAN, SPAN), 0)
    col = lax.broadcasted_iota(jnp.int32, (SPAN, SPAN), 1)
    return col <= row, (col >= row) & (n > 0), col < 64


def _attn_specs(dil, n_in_extra):
    unit = SPAN * dil
    pairs = ATTN_W // LANES
    q_spec = pl.BlockSpec((unit, LANES), lambda p, n: (n, p))
    kp_spec = pl.BlockSpec((unit, LANES), lambda p, n: (jnp.maximum(n - 1, 0), pairs + p))
    kc_spec = pl.BlockSpec((unit, LANES), lambda p, n: (n, pairs + p))
    vp_spec = pl.BlockSpec((unit, LANES), lambda p, n: (jnp.maximum(n - 1, 0), 2 * pairs + p))
    vc_spec = pl.BlockSpec((unit, LANES), lambda p, n: (n, 2 * pairs + p))
    return [q_spec, kp_spec, kc_spec, vp_spec, vc_spec] + [q_spec] * n_in_extra


def _rows(dil, r):
    return pl.ds(r, SPAN, stride=dil) if dil > 1 else slice(None)


def _attn_fwd(layer, dil, proj):
    s = proj.shape[0]
    unit = SPAN * dil

    def body(q_ref, kp_ref, kc_ref, vp_ref, vc_ref, o_ref, lse_ref):
        m_cur, m_prev, is_a = _attn_masks(pl.program_id(1))
        for r in range(dil):
            rows = _rows(dil, r)
            q = q_ref[rows, :]
            kc, kp = kc_ref[rows, :].astype(BF), kp_ref[rows, :].astype(BF)
            vc, vp = vc_ref[rows, :].astype(BF), vp_ref[rows, :].astype(BF)
            halves = []
            for sel in (is_a, jnp.logical_not(is_a)):
                qh = jnp.where(sel, q, 0.0).astype(BF)
                s_c = jnp.where(m_cur, _dot(qh, kc, NT), MASK_VALUE)
                s_p = jnp.where(m_prev, _dot(qh, kp, NT), MASK_VALUE)
                mx = jnp.maximum(jnp.max(s_c, axis=-1, keepdims=True), jnp.max(s_p, axis=-1, keepdims=True))
                p_c, p_p = jnp.exp(s_c - mx), jnp.exp(s_p - mx)
                den = jnp.sum(p_c, axis=-1, keepdims=True) + jnp.sum(p_p, axis=-1, keepdims=True)
                o = (_dot(p_c.astype(BF), vc, NN) + _dot(p_p.astype(BF), vp, NN)) / den
                halves.append((o, mx + jnp.log(den)))
            o_ref[rows, :] = jnp.where(is_a, halves[0][0], halves[1][0])
            lse_ref[rows, :] = jnp.where(is_a, halves[0][1], halves[1][1])

    out_spec = pl.BlockSpec((unit, LANES), lambda p, n: (n, p))
    return pl.pallas_call(
        body, grid=(ATTN_W // LANES, s // unit), in_specs=_attn_specs(dil, 0), out_specs=[out_spec, out_spec],
        out_shape=[jax.ShapeDtypeStruct((s, ATTN_W), F32)] * 2,
        compiler_params=_cparams(("parallel", "arbitrary")), name=f"attn_fwd_d{dil}_l{layer}",
    )(proj, proj, proj, proj, proj)


def _attn_merge(layer, outs, lses, gain, mixed, tm=512):
    s = outs[0].shape[0]

    def body(o1, o2, o3, l1, l2, l3, gain_ref, mixed_ref, o_ref, lse_ref, n_ref):
        ls = (l1[...], l2[...], l3[...])
        mx = jnp.maximum(jnp.maximum(ls[0], ls[1]), ls[2])
        ws = [jnp.exp(l - mx) for l in ls]
        den = ws[0] + ws[1] + ws[2]
        o = (ws[0] * o1[...] + ws[1] * o2[...] + ws[2] * o3[...]) / den
        o_ref[...] = o
        lse_ref[...] = mx + jnp.log(den)
        n_ref[...] = _rms_fwd(o, gain_ref[...]).astype(BF)

    blk = pl.BlockSpec((tm, ATTN_W), lambda i: (i, 0))
    return pl.pallas_call(
        body, grid=(s // tm,),
        in_specs=[blk] * 6 + [pl.BlockSpec((None, 1, ATTN_W), lambda i: (layer, 0, 0)), pl.BlockSpec(memory_space=pl.ANY)],
        out_specs=[blk, blk, blk],
        out_shape=[jax.ShapeDtypeStruct((s, ATTN_W), F32), jax.ShapeDtypeStruct((s, ATTN_W), F32),
                   jax.ShapeDtypeStruct(mixed.shape, BF)],
        input_output_aliases={7: 2},
        compiler_params=_cparams(("parallel",)), name=f"attn_merge_l{layer}",
    )(*outs, *lses, gain, mixed)


def _chunk_cumsum(x, reverse=False):
    n = x.shape[0]
    pos = lax.broadcasted_iota(jnp.int32, x.shape, 0) % HGRN_CHUNK
    for sh in (1, 2, 4, 8):
        if reverse:
            x = x + jnp.where(pos < HGRN_CHUNK - sh, pltpu.roll(x, n - sh, 0), 0.0)
        else:
            x = x + jnp.where(pos >= sh, pltpu.roll(x, sh, 0), 0.0)
    return x


def _chunk_row(x, row):
    r, n = x.shape
    x3 = x.reshape(r // HGRN_CHUNK, HGRN_CHUNK, n)
    return jnp.broadcast_to(x3[:, row:row + 1, :], x3.shape).reshape(r, n)


def _hgrn_pre(qh, z, lb):
    sig = _sigmoid(z)
    f = lb + (1.0 - lb) * sig
    k = (1.0 - lb) * _sigmoid(-z)
    sq = _sigmoid(qh)
    q = qh * sq * HGRN_SCALE
    g = _chunk_cumsum(jnp.log(f))
    g_mid = _chunk_row(g, HGRN_CHUNK // 2 - 1)
    g_last = _chunk_row(g, HGRN_CHUNK - 1)
    e_q, e_k = jnp.exp(g - g_mid), jnp.exp(g_mid - g)
    e_in, e_out = jnp.exp(g), jnp.exp(g_last - g)
    return dict(sig=sig, f=f, k=k, sq=sq, q=q, g_last=g_last, e_q=e_q, e_k=e_k, e_in=e_in, e_out=e_out,
                qt=q * e_q, kt=k * e_k, qg=q * e_in, kout=k * e_out)


def _hgrn_mask():
    row = lax.broadcasted_iota(jnp.int32, (LANES, LANES), 0)
    col = lax.broadcasted_iota(jnp.int32, (LANES, LANES), 1)
    return (row // HGRN_CHUNK == col // HGRN_CHUNK) & (col <= row)


def _hgrn_in_specs(layer, rev, nblk):
    def blk(b):
        return nblk - 1 - b if rev else b
    base = 3 * (ATTN_W // LANES)
    per = HGRN_W // LANES
    specs = [pl.BlockSpec((LANES, LANES), functools.partial(lambda h, b, seg: (blk(b), base + seg * per + h), seg=seg))
             for seg in range(4)]
    specs.append(pl.BlockSpec((None, 1, LANES), lambda h, b: (layer, 0, h)))
    specs.append(pl.BlockSpec((None, 1, LANES), lambda h, b: (layer, 0, 0)))
    return specs, blk


def _hgrn_fwd(layer, proj, lb, gain):
    s = proj.shape[0]
    nblk = s // LANES
    cpb = LANES // HGRN_CHUNK

    def body(q_ref, f_ref, i_ref, g_ref, lb_ref, gain_ref, o_ref, rec_ref, st_ref, state):
        @pl.when(pl.program_id(1) == 0)
        def _():
            state[...] = jnp.zeros(state.shape, F32)

        pre = _hgrn_pre(q_ref[...], f_ref[...], lb_ref[...])
        v = i_ref[...].astype(BF)
        a = jnp.where(_hgrn_mask(), _dot(pre["qt"].astype(BF), pre["kt"].astype(BF), NT), 0.0)
        o = _dot(a.astype(BF), v, NN)
        qg, kout = pre["qg"].astype(BF), pre["kout"].astype(BF)
        dec = jnp.exp(pre["g_last"])
        st = state[...]
        inter = []
        for c in range(cpb):
            rows = slice(c * HGRN_CHUNK, (c + 1) * HGRN_CHUNK)
            st_ref[c * LANES:(c + 1) * LANES, :] = st
            inter.append(_dot(qg[rows], st.astype(BF), NT))
            st = st * dec[c * HGRN_CHUNK:c * HGRN_CHUNK + 1, :] + _dot(v[rows], kout[rows], TN)
        state[...] = st
        o = o + jnp.concatenate(inter, axis=0)
        o_ref[...] = o
        gate = g_ref[...]
        rec_ref[...] = (_rms_fwd(o, gain_ref[...]) * (gate * _sigmoid(gate))).astype(BF)

    specs, _ = _hgrn_in_specs(layer, False, nblk)
    blk = pl.BlockSpec((LANES, LANES), lambda h, b: (b, h))
    return pl.pallas_call(
        body, grid=(HGRN_HEADS, nblk), in_specs=specs,
        out_specs=[blk, pl.BlockSpec((LANES, LANES), lambda h, b: (b, ATTN_W // LANES + h)),
                   pl.BlockSpec((None, cpb * LANES, LANES), lambda h, b: (h, b, 0))],
        out_shape=[jax.ShapeDtypeStruct((s, HGRN_W), F32), jax.ShapeDtypeStruct((s, MIX_W), BF),
                   jax.ShapeDtypeStruct((HGRN_HEADS, nblk * cpb * LANES, LANES), F32)],
        scratch_shapes=[pltpu.VMEM((LANES, LANES), F32)],
        compiler_params=_cparams(("parallel", "arbitrary")), name=f"hgrn_fwd_l{layer}",
    )(proj, proj, proj, proj, lb, gain)


def _fwd_outproj(layer, mixed, g_out, x, tm=512):
    s, d = x.shape
    mw = mixed.shape[1]

    def body(m_ref, w_ref, x_ref, o_ref):
        o_ref[...] = x_ref[...] + _dot(m_ref[...], w_ref[...], NN)

    row = pl.BlockSpec((tm, d), lambda i: (i, 0))
    return pl.pallas_call(
        body, grid=(s // tm,),
        in_specs=[pl.BlockSpec((tm, mw), lambda i: (i, 0)), _resident((None, mw, d), lambda i: (layer, 0, 0)), row],
        out_specs=row, out_shape=jax.ShapeDtypeStruct((s, d), F32),
        compiler_params=_cparams(("parallel",)), name=f"fwd_outproj_l{layer}",
    )(mixed, g_out, x)


def _relu2(u):
    return jnp.square(jnp.maximum(u, 0)).astype(BF)


def _mlp_fwd(layer, x, gain, g_up, g_down, tm=256):
    s, d = x.shape
    nblk, hs = g_up.shape[1], g_up.shape[3]

    def body(x_ref, gain_ref, up_ref, down_ref, o_ref, u_ref, h_ref):
        xv = x_ref[...]
        h = _rms_fwd(xv, gain_ref[...]).astype(BF)
        h_ref[...] = h
        acc = xv
        for j in range(nblk):
            u = _dot(h, up_ref[j], NN)
            u_ref[:, j * hs:(j + 1) * hs] = u.astype(BF)
            acc = acc + _dot(_relu2(u), down_ref[j * hs:(j + 1) * hs, :], NN)
        o_ref[...] = acc

    row = pl.BlockSpec((tm, d), lambda i: (i, 0))
    return pl.pallas_call(
        body, grid=(s // tm,),
        in_specs=[row, pl.BlockSpec((None, 1, d), lambda i: (layer, 0, 0)),
                  _resident((None, nblk, d, hs), lambda i: (layer, 0, 0, 0)),
                  _resident((None, nblk * hs, d), lambda i: (layer, 0, 0))],
        out_specs=[row, pl.BlockSpec((tm, nblk * hs), lambda i: (i, 0)), row],
        out_shape=[jax.ShapeDtypeStruct((s, d), F32), jax.ShapeDtypeStruct((s, nblk * hs), BF),
                   jax.ShapeDtypeStruct((s, d), BF)],
        compiler_params=_cparams(("parallel",)), name=f"mlp_fwd_l{layer}",
    )(x, gain, g_up, g_down)


def _loss_head(x, gain, target, tm=512):
    s, d = x.shape

    def body(x_ref, gain_ref, t_ref, dx_ref, dxb_ref, dgain_ref, loss_ref):
        i = pl.program_id(0)
        xv, gv = x_ref[...], gain_ref[...]
        err = _rms_fwd(xv, gv) - t_ref[...]
        dx, dgain = _rms_bwd(err * (1.0 / d), xv, gv)
        dx_ref[...] = dx
        dxb_ref[...] = dx.astype(BF)
        part = _part8(dgain)
        lpart = _part8(0.5 * jnp.mean(err * err, axis=-1, keepdims=True) * jnp.ones((1, LANES), F32))

        @pl.when(i == 0)
        def _():
            dgain_ref[...] = part
            loss_ref[...] = lpart

        @pl.when(i > 0)
        def _():
            dgain_ref[...] += part
            loss_ref[...] += lpart

    row = pl.BlockSpec((tm, d), lambda i: (i, 0))
    return pl.pallas_call(
        body, grid=(s // tm,),
        in_specs=[row, pl.BlockSpec((1, d), lambda i: (0, 0)), row],
        out_specs=[row, row, pl.BlockSpec((8, d), lambda i: (0, 0)), pl.BlockSpec((8, LANES), lambda i: (0, 0))],
        out_shape=[jax.ShapeDtypeStruct((s, d), F32), jax.ShapeDtypeStruct((s, d), BF), jax.ShapeDtypeStruct((8, d), F32),
                   jax.ShapeDtypeStruct((8, LANES), F32)],
        compiler_params=_cparams(("arbitrary",)), name="loss_head",
    )(x, gain, target)


def _accumulate_rows(i, ref, part):
    @pl.when(i == 0)
    def _():
        ref[...] = part

    @pl.when(i > 0)
    def _():
        ref[...] += part


def _mlp_bwd(layer, dx, dxb, x, gain, u, g_up, g_down, tm=256):
    s, d = x.shape
    nblk, hs = g_up.shape[1], g_up.shape[3]

    def body(dx_ref, dxb_ref, x_ref, gain_ref, u_ref, up_ref, down_ref, o_ref, ob_ref, du_ref, dgain_ref):
        dxb_v = dxb_ref[...]
        acc = jnp.zeros((tm, d), F32)
        for j in range(nblk):
            cols = slice(j * hs, (j + 1) * hs)
            da = _dot(dxb_v, down_ref[cols, :], NT)
            du = (da * (2.0 * jnp.maximum(u_ref[:, cols].astype(F32), 0.0))).astype(BF)
            du_ref[:, cols] = du
            acc = acc + _dot(du, up_ref[j], NT)
        dxn, dgain = _rms_bwd(acc, x_ref[...], gain_ref[...])
        out = dx_ref[...] + dxn
        o_ref[...] = out
        ob_ref[...] = out.astype(BF)
        _accumulate_rows(pl.program_id(0), dgain_ref, _part8(dgain))

    row = pl.BlockSpec((tm, d), lambda i: (i, 0))
    wide = pl.BlockSpec((tm, nblk * hs), lambda i: (i, 0))
    return pl.pallas_call(
        body, grid=(s // tm,),
        in_specs=[row, row, row, pl.BlockSpec((None, 1, d), lambda i: (layer, 0, 0)), wide,
                  _resident((None, nblk, d, hs), lambda i: (layer, 0, 0, 0)),
                  _resident((None, nblk * hs, d), lambda i: (layer, 0, 0))],
        out_specs=[row, row, wide, pl.BlockSpec((8, d), lambda i: (0, 0))],
        out_shape=[jax.ShapeDtypeStruct((s, d), F32), jax.ShapeDtypeStruct((s, d), BF),
                   jax.ShapeDtypeStruct((s, nblk * hs), BF), jax.ShapeDtypeStruct((8, d), F32)],
        compiler_params=_cparams(("arbitrary",)), name=f"mlp_bwd_l{layer}",
    )(dx, dxb, x, gain, u, g_up, g_down)


def _bwd_outproj(layer, dxb, g_out, tm=512):
    s, d = dxb.shape
    mw = g_out.shape[1]

    def body(dx_ref, w_ref, o_ref):
        o_ref[...] = _dot(dx_ref[...], w_ref[...], NT)

    return pl.pallas_call(
        body, grid=(s // tm,),
        in_specs=[pl.BlockSpec((tm, d), lambda i: (i, 0)), _resident((None, mw, d), lambda i: (layer, 0, 0))],
        out_specs=pl.BlockSpec((tm, mw), lambda i: (i, 0)), out_shape=jax.ShapeDtypeStruct((s, mw), F32),
        compiler_params=_cparams(("parallel",)), name=f"bwd_outproj_l{layer}",
    )(dxb, g_out)


def _attn_norm_bwd(layer, dmixed, o, gain, tm=512):
    s = o.shape[0]

    def body(dm_ref, o_ref, gain_ref, do_ref, delta_ref, dgain_ref):
        i = pl.program_id(0)
        ov = o_ref[...]
        do, dgain = _rms_bwd(dm_ref[...], ov, gain_ref[...])
        do_ref[...] = do
        row = lax.broadcasted_iota(jnp.int32, (ATTN_W, ATTN_W), 0)
        col = lax.broadcasted_iota(jnp.int32, (ATTN_W, ATTN_W), 1)
        same_head = jnp.where(row // 64 == col // 64, 1.0, 0.0)
        delta_ref[...] = jnp.dot(do * ov, same_head, precision=lax.Precision.HIGHEST, preferred_element_type=F32)
        part = _part8(dgain)

        @pl.when(i == 0)
        def _():
            dgain_ref[...] = part

        @pl.when(i > 0)
        def _():
            dgain_ref[...] += part

    blk = pl.BlockSpec((tm, ATTN_W), lambda i: (i, 0))
    return pl.pallas_call(
        body, grid=(s // tm,), in_specs=[blk, blk, pl.BlockSpec((None, 1, ATTN_W), lambda i: (layer, 0, 0))],
        out_specs=[blk, blk, pl.BlockSpec((8, ATTN_W), lambda i: (0, 0))],
        out_shape=[jax.ShapeDtypeStruct((s, ATTN_W), F32), jax.ShapeDtypeStruct((s, ATTN_W), F32),
                   jax.ShapeDtypeStruct((8, ATTN_W), F32)],
        compiler_params=_cparams(("arbitrary",)), name=f"attn_norm_bwd_l{layer}",
    )(dmixed, o, gain)


def _attn_bwd(layer, dil, proj, do, lse, delta):
    s = proj.shape[0]
    unit = SPAN * dil

    def body(q_ref, kp_ref, kc_ref, vp_ref, vc_ref, do_ref, lse_ref, delta_ref, dq_ref, dkc_ref, dkp_ref, dvc_ref, dvp_ref):
        m_cur, m_prev, is_a = _attn_masks(pl.program_id(1))
        for r in range(dil):
            rows = _rows(dil, r)
            q, dov = q_ref[rows, :], do_ref[rows, :]
            kc, kp = kc_ref[rows, :], kp_ref[rows, :]
            vc, vp = vc_ref[rows, :].astype(BF), vp_ref[rows, :].astype(BF)
            lse_v, delta_v = lse_ref[rows, :], delta_ref[rows, :]
            dq = jnp.zeros((SPAN, LANES), F32)
            dkc, dkp, dvc, dvp = dq, dq, dq, dq
            for half, sel in enumerate((is_a, jnp.logical_not(is_a))):
                col = slice(64 * half, 64 * half + 1)
                lse_h, delta_h = lse_v[:, col], delta_v[:, col]
                qh = jnp.where(sel, q, 0.0).astype(BF)
                doh = jnp.where(sel, dov, 0.0).astype(BF)
                for k_f32, v_bf, mask, which in ((kc, vc, m_cur, 0), (kp, vp, m_prev, 1)):
                    kh = jnp.where(sel, k_f32, 0.0).astype(BF)
                    sc = jnp.where(mask, _dot(qh, kh, NT), MASK_VALUE)
                    p = jnp.exp(sc - lse_h)
                    dv = _dot(p.astype(BF), doh, TN)
                    ds = (p * (_dot(doh, v_bf, NT) - delta_h)).astype(BF)
                    dq = dq + _dot(ds, kh, NN)
                    dk = _dot(ds, qh, TN)
                    if which == 0:
                        dkc, dvc = dkc + dk, dvc + dv
                    else:
                        dkp, dvp = dkp + dk, dvp + dv
            dq_ref[rows, :] = dq
            dkc_ref[rows, :] = dkc
            dkp_ref[rows, :] = dkp
            dvc_ref[rows, :] = dvc
            dvp_ref[rows, :] = dvp

    out_spec = pl.BlockSpec((unit, LANES), lambda p, n: (n, p))
    return pl.pallas_call(
        body, grid=(ATTN_W // LANES, s // unit), in_specs=_attn_specs(dil, 3), out_specs=[out_spec] * 5,
        out_shape=[jax.ShapeDtypeStruct((s, ATTN_W), F32)] * 5,
        compiler_params=_cparams(("parallel", "arbitrary")), name=f"attn_bwd_d{dil}_l{layer}",
    )(proj, proj, proj, proj, proj, do, lse, delta)


def _attn_combine(layer, parts, cos, sin):
    s = parts[0][0].shape[0]
    nblk = s // SPAN
    arrays, specs = [], []
    for dil, (dq, dkc, dkp, dvc, dvp) in zip(DILATIONS, parts):
        here = pl.BlockSpec((SPAN, ATTN_W), lambda i: (i, 0))
        ahead = pl.BlockSpec((SPAN, ATTN_W), functools.partial(lambda i, dil: (jnp.minimum(i + dil, nblk - 1), 0), dil=dil))
        arrays += [dq, dkc, dkp, dvc, dvp]
        specs += [here, here, ahead, here, ahead]
    tab = pl.BlockSpec((SPAN, LANES), lambda i: (i, 0))

    def body(*refs):
        cos_ref, sin_ref, out_ref = refs[15], refs[16], refs[17]
        i = pl.program_id(0)
        dq = dk = dv = jnp.zeros((SPAN, ATTN_W), F32)
        for p, dil in enumerate(DILATIONS):
            dq_r, dkc_r, dkp_r, dvc_r, dvp_r = refs[5 * p:5 * p + 5]
            has_next = i + dil < nblk
            dq = dq + dq_r[...]
            dk = dk + dkc_r[...] + jnp.where(has_next, dkp_r[...], 0.0)
            dv = dv + dvc_r[...] + jnp.where(has_next, dvp_r[...], 0.0)
        cs = _tile_lanes(cos_ref[...], ATTN_W // LANES)
        sn = _tile_lanes(sin_ref[...], ATTN_W // LANES)
        out_ref[0] = ((dq * cs - _rope_partner(dq) * sn) * ATTN_SCALE).astype(BF)
        out_ref[1] = (dk * cs - _rope_partner(dk) * sn).astype(BF)
        out_ref[2] = dv.astype(BF)

    return pl.pallas_call(
        body, grid=(nblk,), in_specs=specs + [tab, tab],
        out_specs=pl.BlockSpec((3, SPAN, ATTN_W), lambda i: (0, i, 0)),
        out_shape=jax.ShapeDtypeStruct((3, s, ATTN_W), BF),
        compiler_params=_cparams(("parallel",)), name=f"attn_combine_l{layer}",
    )(*arrays, cos, sin)


def _hgrn_bwd(layer, proj, lb, gain, o, dmixed, states):
    s = proj.shape[0]
    nblk = s // LANES
    cpb = LANES // HGRN_CHUNK

    def body(q_ref, f_ref, i_ref, g_ref, lb_ref, gain_ref, o_ref, drec_ref, st_ref, dseg_ref, dlb_ref, dgain_ref, dstate):
        head, step = pl.program_id(0), pl.program_id(1)

        @pl.when(step == 0)
        def _():
            dstate[...] = jnp.zeros(dstate.shape, F32)

        lbv, gv = lb_ref[...], gain_ref[...]
        qh, z, gate_in = q_ref[...], f_ref[...], g_ref[...]
        pre = _hgrn_pre(qh, z, lbv)
        v = i_ref[...].astype(BF)
        sg = _sigmoid(gate_in)
        gate = gate_in * sg
        ov, drec = o_ref[...], drec_ref[...]
        do, dgain = _rms_bwd(drec * gate, ov, gv)
        dgate_in = drec * _rms_fwd(ov, gv) * (sg * (1.0 + gate_in * (1.0 - sg)))
        mask = _hgrn_mask()
        qt, kt = pre["qt"].astype(BF), pre["kt"].astype(BF)
        do_b = do.astype(BF)
        a = jnp.where(mask, _dot(qt, kt, NT), 0.0).astype(BF)
        da = jnp.where(mask, _dot(do_b, v, NT), 0.0).astype(BF)
        dv = _dot(a, do_b, TN)
        dqt = _dot(da, kt, NN)
        dkt = _dot(da, qt, TN)
        qg, kout = pre["qg"].astype(BF), pre["kout"].astype(BF)
        dec = jnp.exp(pre["g_last"])
        dst = dstate[...]
        dqg_c, dkout_c, dv_c, dgl_c = [None] * cpb, [None] * cpb, [None] * cpb, [None] * cpb
        for c in reversed(range(cpb)):
            rows = slice(c * HGRN_CHUNK, (c + 1) * HGRN_CHUNK)
            st_prev = st_ref[c * LANES:(c + 1) * LANES, :]
            dec_c = dec[c * HGRN_CHUNK:c * HGRN_CHUNK + 1, :]
            dst_b = dst.astype(BF)
            dqg_c[c] = _dot(do_b[rows], st_prev.astype(BF), NN)
            dkout_c[c] = _dot(v[rows], dst_b, NN)
            dv_c[c] = _dot(kout[rows], dst_b, NT)
            dgl_c[c] = (jnp.sum(dst * st_prev, axis=0, keepdims=True) * dec_c
                        + jnp.sum(dkout_c[c] * pre["kout"][rows], axis=0, keepdims=True))
            dst = dst * dec_c + _dot(do_b[rows], qg[rows], TN)
        dstate[...] = dst
        dqg, dkout = jnp.concatenate(dqg_c, axis=0), jnp.concatenate(dkout_c, axis=0)
        dv = dv + jnp.concatenate(dv_c, axis=0)
        dgl = jnp.concatenate([jnp.broadcast_to(t, (HGRN_CHUNK, LANES)) for t in dgl_c], axis=0)
        pos = lax.broadcasted_iota(jnp.int32, (LANES, LANES), 0) % HGRN_CHUNK
        dq = dqt * pre["e_q"] + dqg * pre["e_in"]
        dk = dkt * pre["e_k"] + dkout * pre["e_out"]
        dg = (dqt * pre["qt"] - dkt * pre["kt"] + dqg * pre["qg"] - dkout * pre["kout"]
              + jnp.where(pos == HGRN_CHUNK - 1, dgl, 0.0))
        dlogf = _chunk_cumsum(dg, reverse=True)
        sig, sq = pre["sig"], pre["sq"]
        df = dlogf / pre["f"] - dk
        dseg_ref[0] = (dq * HGRN_SCALE * (sq * (1.0 + qh * (1.0 - sq)))).astype(BF)
        dseg_ref[1] = (df * (1.0 - lbv) * sig * (1.0 - sig)).astype(BF)
        dseg_ref[2] = dv.astype(BF)
        dseg_ref[3] = dgate_in.astype(BF)
        dlb_part = _part8(df * (1.0 - sig))
        dgain_part = _part8(dgain)

        @pl.when(step == 0)
        def _():
            dlb_ref[...] = dlb_part

        @pl.when(step > 0)
        def _():
            dlb_ref[...] += dlb_part

        @pl.when((step == 0) & (head == 0))
        def _():
            dgain_ref[...] = dgain_part

        @pl.when((step > 0) | (head > 0))
        def _():
            dgain_ref[...] += dgain_part

    specs, blk = _hgrn_in_specs(layer, True, nblk)
    per = HGRN_W // LANES
    specs += [pl.BlockSpec((LANES, LANES), lambda h, b: (blk(b), h)),
              pl.BlockSpec((LANES, LANES), lambda h, b: (blk(b), per + h)),
              pl.BlockSpec((None, cpb * LANES, LANES), lambda h, b: (h, blk(b), 0))]
    return pl.pallas_call(
        body, grid=(HGRN_HEADS, nblk), in_specs=specs,
        out_specs=[pl.BlockSpec((4, LANES, LANES), lambda h, b: (0, blk(b), h)),
                   pl.BlockSpec((8, LANES), lambda h, b: (0, h)), pl.BlockSpec((8, LANES), lambda h, b: (0, 0))],
        out_shape=[jax.ShapeDtypeStruct((4, s, HGRN_W), BF), jax.ShapeDtypeStruct((8, HGRN_W), F32),
                   jax.ShapeDtypeStruct((8, LANES), F32)],
        scratch_shapes=[pltpu.VMEM((LANES, LANES), F32)],
        compiler_params=_cparams(("arbitrary", "arbitrary")), name=f"hgrn_bwd_l{layer}",
    )(proj, proj, proj, proj, lb, gain, o, dmixed, states)


def _bwd_inproj(layer, dqkv, dhg, g_in, x, gain, dres, tm=256):
    s, d = x.shape

    def body(dqkv_ref, dhg_ref, w_ref, x_ref, gain_ref, dres_ref, dx_ref, dxb_ref, dgain_ref):
        acc = jnp.zeros((tm, d), F32)
        for seg in range(N_SEG):
            a = dqkv_ref[seg] if seg < 3 else dhg_ref[seg - 3]
            acc = acc + _dot(a, w_ref[seg * SEG:(seg + 1) * SEG, :], NN)
        dx, dgain = _rms_bwd(acc, x_ref[...], gain_ref[...])
        out = dres_ref[...] + dx
        dx_ref[...] = out
        dxb_ref[...] = out.astype(BF)
        _accumulate_rows(pl.program_id(0), dgain_ref, _part8(dgain))

    row = pl.BlockSpec((tm, d), lambda i: (i, 0))
    return pl.pallas_call(
        body, grid=(s // tm,),
        in_specs=[pl.BlockSpec((3, tm, SEG), lambda i: (0, i, 0)), pl.BlockSpec((4, tm, SEG), lambda i: (0, i, 0)),
                  _resident((None, PROJ_W, d), lambda i: (layer, 0, 0)), row,
                  pl.BlockSpec((None, 1, d), lambda i: (layer, 0, 0)), row],
        out_specs=[row, row, pl.BlockSpec((8, d), lambda i: (0, 0))],
        out_shape=[jax.ShapeDtypeStruct((s, d), F32), jax.ShapeDtypeStruct((s, d), BF), jax.ShapeDtypeStruct((8, d), F32)],
        compiler_params=_cparams(("arbitrary",)), name=f"bwd_inproj_l{layer}",
    )(dqkv, dhg, g_in, x, gain, dres)


def _adamw(w, g, m, v):
    m2 = ADAM_B1 * m + (1.0 - ADAM_B1) * g
    v2 = ADAM_B2 * v + (1.0 - ADAM_B2) * (g * g)
    m_hat = m2 / (1.0 - ADAM_B1 ** ADAM_STEP)
    v_hat = v2 / (1.0 - ADAM_B2 ** ADAM_STEP)
    delta = -ADAM_LR * (m_hat / (jnp.sqrt(v_hat) + ADAM_EPS) + ADAM_WD * w)
    return delta, m2, v2


def _adam_big(name, parts, w, m, v, transpose, row_tiles):
    depth = w.shape[0]
    r, c = parts.shape[2], parts.shape[3]
    if transpose:
        tc = c // row_tiles
        p_spec = pl.BlockSpec((N_DEV, None, r, tc), lambda l, t: (0, l, 0, t))
        w_spec = pl.BlockSpec((None, tc, r), lambda l, t: (l, t, 0))
    else:
        tr = r // row_tiles
        p_spec = pl.BlockSpec((N_DEV, None, tr, c), lambda l, t: (0, l, t, 0))
        w_spec = pl.BlockSpec((None, tr, c), lambda l, t: (l, t, 0))

    def body(p_ref, w_ref, m_ref, v_ref, g_ref, d_ref, m2_ref, v2_ref):
        g = p_ref[0].astype(F32)
        for dev in range(1, N_DEV):
            g = g + p_ref[dev].astype(F32)
        if transpose:
            g = g.T
        delta, m2, v2 = _adamw(w_ref[...], g, m_ref[...], v_ref[...])
        g_ref[...] = g
        d_ref[...] = delta
        m2_ref[...] = m2
        v2_ref[...] = v2

    return pl.pallas_call(
        body, grid=(depth, row_tiles), in_specs=[p_spec, w_spec, w_spec, w_spec], out_specs=[w_spec] * 4,
        out_shape=[jax.ShapeDtypeStruct(w.shape, F32)] * 4,
        compiler_params=_cparams(("parallel", "parallel")), name=name,
    )(parts, w, m, v)


def _adam_small(g, w, m, v):
    def body(g_ref, w_ref, m_ref, v_ref, d_ref, m2_ref, v2_ref):
        delta, m2, v2 = _adamw(w_ref[...], g_ref[...], m_ref[...], v_ref[...])
        d_ref[...] = delta
        m2_ref[...] = m2
        v2_ref[...] = v2

    vm = pl.BlockSpec(memory_space=pltpu.VMEM)
    return pl.pallas_call(body, in_specs=[vm] * 4, out_specs=[vm] * 3, out_shape=[jax.ShapeDtypeStruct(g.shape, F32)] * 3,
                          name="adam_small")(g, w, m, v)


def _lower_bounds(logits):
    def body(l_ref, lb_ref, jac_ref):
        l0, l1 = l_ref[0:1, :], l_ref[1:2, :]
        mx = jnp.maximum(l0, l1)
        e0, e1 = jnp.exp(l0 - mx), jnp.exp(l1 - mx)
        p0, p1 = e0 / (e0 + e1), e1 / (e0 + e1)
        lb_ref[0:1, :] = p0 - p0
        lb_ref[1:2, :] = (p0 + p1) - p0
        jac_ref[0:1, :] = -p0 * p1
        jac_ref[1:2, :] = p0 * p1

    vm = pl.BlockSpec(memory_space=pltpu.VMEM)
    return pl.pallas_call(body, in_specs=[vm], out_specs=[vm, vm], out_shape=[jax.ShapeDtypeStruct(logits.shape, F32)] * 2,
                          name="hgrn_lower_bounds")(logits)


def _rope_tables(s):
    half = 32
    inv_freq = ROPE_THETA ** (-jnp.arange(half, dtype=F32) / half)
    ang = jnp.arange(s, dtype=jnp.int32).astype(F32)[:, None] * inv_freq[None, :]
    cos, sin = jnp.cos(ang), jnp.sin(ang)
    return jnp.concatenate([cos] * 4, axis=1), jnp.concatenate([-sin, sin, -sin, sin], axis=1)


SMALL_NAMES = ("norm_mix", "attn_out_gain", "hgrn_lb_logits", "hgrn_out_gain", "norm_mlp", "norm_final")


def _pack_small(vals):
    flat = jnp.concatenate([v.reshape(-1) for v in vals])
    rows = -(-flat.shape[0] // (8 * LANES)) * 8
    return jnp.pad(flat, (0, rows * LANES - flat.shape[0])).reshape(rows, LANES)


def _unpack_small(packed, like):
    flat, out, off = packed.reshape(-1), [], 0
    for v in like:
        out.append(flat[off:off + v.size].reshape(v.shape))
        off += v.size
    return out


def kernel(x, norm_mix, w_in, attn_out_gain, hgrn_lb_logits, hgrn_out_gain, w_out, norm_mlp, w_up, w_down, norm_final, loss_target, m_norm_mix, m_w_in, m_attn_out_gain, m_hgrn_lb_logits, m_hgrn_out_gain, m_w_out, m_norm_mlp, m_w_up, m_w_down, m_norm_final, v_norm_mix, v_w_in, v_attn_out_gain, v_hgrn_lb_logits, v_hgrn_out_gain, v_w_out, v_norm_mlp, v_w_up, v_w_down, v_norm_final):
    depth = w_in.shape[0]
    assert depth == 2 and x.shape[0] == 1
    s, d = x.shape[1], x.shape[2]
    x0 = x[0]
    target = loss_target[0]
    cos, sin = _rope_tables(s)
    g_mix, g_attn, g_hg, g_mlp = (norm_mix[:, None, :], attn_out_gain[:, None, :], hgrn_out_gain[:, None, :],
                                  norm_mlp[:, None, :])
    lb, lb_jac = _lower_bounds(hgrn_lb_logits)
    lb3 = lb[:, None, :]

    full_in, full_out, full_up, full_down = _all_gather_weights(*_pack_weights(w_in, w_out, w_up, w_down))

    saved = []
    xl = x0
    for l in range(depth):
        proj, h = _fwd_inproj(l, xl, g_mix, full_in, cos, sin)
        fw = [_attn_fwd(l, dil, proj) for dil in DILATIONS]
        o_hg, mixed, states = _hgrn_fwd(l, proj, lb3, g_hg)
        o_attn, lse, mixed = _attn_merge(l, [f[0] for f in fw], [f[1] for f in fw], g_attn, mixed)
        x_mid = _fwd_outproj(l, mixed, full_out, xl)
        x_next, u, h2 = _mlp_fwd(l, x_mid, g_mlp, full_up, full_down)
        saved.append((xl, proj, h, o_attn, lse, o_hg, states, mixed, x_mid, u, h2))
        xl = x_next
    dx, dxb, dnorm_final8, loss8 = _loss_head(xl, norm_final[None, :], target)
    loss = lax.psum(jnp.sum(loss8[:, 0]), ("x", "y", "c"))

    gw_in, gw_out, gw_up, gw_down = [None] * depth, [None] * depth, [None] * depth, [None] * depth
    small = {}
    for l in reversed(range(depth)):
        xl, proj, h, o_attn, lse, o_hg, states, mixed, x_mid, u, h2 = saved[l]
        hs = full_up.shape[3]
        gw_down[l] = _mm_tn(f"grad_w_down_l{l}", u, dxb, u.shape[1], a_fn=_relu2)
        dx_mid, dx_mid_b, du, dmlp8 = _mlp_bwd(l, dx, dxb, x_mid, g_mlp, u, full_up, full_down)
        gw_up[l] = _mm_tn(f"grad_w_up_l{l}", h2, du, d, out_block_w=hs)
        dmixed = _bwd_outproj(l, dx_mid_b, full_out)
        gw_out[l] = _mm_tn(f"grad_w_out_l{l}", mixed, dx_mid_b, mixed.shape[1])
        do, delta, dattn8 = _attn_norm_bwd(l, dmixed, o_attn, g_attn)
        parts = [_attn_bwd(l, dil, proj, do, lse, delta) for dil in DILATIONS]
        dqkv = _attn_combine(l, parts, cos, sin)
        dhg, dlb8, dhgain8 = _hgrn_bwd(l, proj, lb3, g_hg, o_hg, dmixed, states)
        dx, dxb, dmix8 = _bwd_inproj(l, dqkv, dhg, full_in, xl, g_mix, dx_mid)
        gin = _mm_tn(f"grad_w_in_qkv_l{l}", dqkv, h, PROJ_W, a_lead=True)
        gw_in[l] = _mm_tn(f"grad_w_in_hg_l{l}", dhg, h, PROJ_W, a_lead=True, out_block_off=3, prev=gin)
        small[l] = (dmix8, dattn8, dlb8, dhgain8, dmlp8)

    def fin(p8):
        return jnp.sum(p8, axis=0)
    dlogits = lb_jac * fin(small[1][2])[None, :]
    small_grads = [jnp.stack([fin(small[l][0]) for l in range(depth)]), jnp.stack([fin(small[l][1]) for l in range(depth)]),
                   dlogits, jnp.stack([fin(small[l][3]) for l in range(depth)]),
                   jnp.stack([fin(small[l][4]) for l in range(depth)]), fin(dnorm_final8)]
    small_w = [norm_mix, attn_out_gain, hgrn_lb_logits, hgrn_out_gain, norm_mlp, norm_final]
    small_m = [m_norm_mix, m_attn_out_gain, m_hgrn_lb_logits, m_hgrn_out_gain, m_norm_mlp, m_norm_final]
    small_v = [v_norm_mix, v_attn_out_gain, v_hgrn_lb_logits, v_hgrn_out_gain, v_norm_mlp, v_norm_final]
    g_small = _all_reduce_small(_pack_small(small_grads))
    d_small, m_small, v_small = _adam_small(g_small, _pack_small(small_w), _pack_small(small_m), _pack_small(small_v))
    gs, ds, ms, vs = (_unpack_small(t, small_w) for t in (g_small, d_small, m_small, v_small))

    r_in, r_out, r_up, r_down = _reduce_scatter_grads(gw_in, gw_out, gw_up, gw_down)
    big = {
        "w_in": _adam_big("adam_w_in", r_in, w_in, m_w_in, v_w_in, True, 2),
        "w_out": _adam_big("adam_w_out", r_out, w_out, m_w_out, v_w_out, False, 1),
        "w_up": _adam_big("adam_w_up", r_up, w_up, m_w_up, v_w_up, False, 2),
        "w_down": _adam_big("adam_w_down", r_down, w_down, m_w_down, v_w_down, False, 2),
    }

    def gather(idx, small_list):
        by_name = dict(zip(SMALL_NAMES, small_list))
        return [by_name["norm_mix"], big["w_in"][idx], by_name["attn_out_gain"], by_name["hgrn_lb_logits"],
                by_name["hgrn_out_gain"], big["w_out"][idx], by_name["norm_mlp"], big["w_up"][idx], big["w_down"][idx],
                by_name["norm_final"]]

    return (loss, dx[None], *gather(0, gs), *gather(1, ds), *gather(2, ms), *gather(3, vs))
```

```python
import functools
from typing import Callable, NamedTuple

import jax
import jax.numpy as jnp
from jax import lax
from jax.experimental import pallas as pl
from jax.experimental.pallas import tpu as pltpu

F32 = jnp.float32
BF = jnp.bfloat16

N_DEV = 8
ATTN_W = 512
HGRN_W = 512
HGRN_HEADS = 4
HGRN_DIM = 128
SEG = 512
N_SEG = 7
PROJ_W = N_SEG * SEG
MIX_W = ATTN_W + HGRN_W
SPAN = 128
DILATIONS = (1, 4, 16)
HGRN_CHUNK = 16
ROPE_THETA = 10000.0
NORM_EPS = 1e-6
MASK_VALUE = -1e30
ATTN_SCALE = 0.125
HGRN_SCALE = HGRN_DIM ** -0.5
ADAM_LR = 0.001
ADAM_B1 = 0.9
ADAM_B2 = 0.999
ADAM_EPS = 1e-08
ADAM_WD = 0.01
ADAM_STEP = 10
LANES = 128
VMEM_LIMIT = 56 * 1024 * 1024

NN = ((1,), (0,))
NT = ((1,), (1,))
TN = ((0,), (0,))
MESH = pl.DeviceIdType.MESH


def _dot(a, b, dims):
    return lax.dot_general(a, b, (dims, ((), ())), preferred_element_type=F32)


def _cparams(sem):
    return pltpu.CompilerParams(dimension_semantics=sem, vmem_limit_bytes=VMEM_LIMIT)


def _part8(x):
    r, n = x.shape
    return jnp.sum(x.reshape(r // 8, 8, n), axis=0)


def _sigmoid(x):
    return 1.0 / (1.0 + jnp.exp(-x))


def _rms_fwd(x, gain):
    r = lax.rsqrt(jnp.mean(x * x, axis=-1, keepdims=True) + NORM_EPS)
    return x * r * gain


def _rms_bwd(dy, x, gain):
    r = lax.rsqrt(jnp.mean(x * x, axis=-1, keepdims=True) + NORM_EPS)
    xn = x * r
    dxn = dy * gain
    dx = r * (dxn - xn * jnp.mean(dxn * xn, axis=-1, keepdims=True))
    return dx, dy * xn


def _rope_partner(x):
    n = x.shape[-1]
    lane = lax.broadcasted_iota(jnp.int32, x.shape, x.ndim - 1)
    return jnp.where((lane % 64) < 32, pltpu.roll(x, n - 32, x.ndim - 1), pltpu.roll(x, 32, x.ndim - 1))


def _tile_lanes(t, reps):
    return jnp.concatenate([t] * reps, axis=-1)


def _mm_tn(name, a, b, out_rows, a_lead=False, out_block_off=0, prev=None, out_block_w=None, a_fn=None,
           tm=512, tn=1024, tk=1024):
    kdim, n = b.shape
    m = a.shape[-1]
    tm, tn, tk = min(tm, m), min(tn, n), min(tk, kdim)
    mt, nk = m // tm, kdim // tk
    n_lead = a.shape[0] if a_lead else 1
    if a_lead:
        a_spec = pl.BlockSpec((None, tk, tm), lambda i, j, k: (i // mt, k, i % mt))
    else:
        a_spec = pl.BlockSpec((tk, tm), lambda i, j, k: (k, i))
    b_spec = pl.BlockSpec((tk, tn), lambda i, j, k: (k, j))
    if out_block_w:
        nb = tn // out_block_w
        o_shape = jax.ShapeDtypeStruct((n // out_block_w, out_rows, out_block_w), BF)
        o_spec = pl.BlockSpec((nb, tm, out_block_w), lambda i, j, k: (j, i + out_block_off, 0))
    else:
        nb = 0
        o_shape = jax.ShapeDtypeStruct((out_rows, n), BF)
        o_spec = pl.BlockSpec((tm, tn), lambda i, j, k: (i + out_block_off, j))
    arrays, specs, aliases = [a, b], [a_spec, b_spec], {}
    if prev is not None:
        arrays.append(prev)
        specs.append(pl.BlockSpec(memory_space=pl.ANY))
        aliases = {2: 0}

    def body(*refs):
        a_ref, b_ref, o_ref, acc = refs[0], refs[1], refs[-2], refs[-1]
        k = pl.program_id(2)

        @pl.when(k == 0)
        def _():
            acc[...] = jnp.zeros(acc.shape, F32)

        av = a_ref[...]
        if a_fn is not None:
            av = a_fn(av)
        acc[...] += _dot(av, b_ref[...], TN)

        @pl.when(k == nk - 1)
        def _():
            if nb:
                for t in range(nb):
                    o_ref[t] = acc[:, t * out_block_w:(t + 1) * out_block_w].astype(BF)
            else:
                o_ref[...] = acc[...].astype(BF)

    return pl.pallas_call(
        body, grid=(n_lead * mt, n // tn, nk), in_specs=specs, out_specs=o_spec, out_shape=o_shape,
        scratch_shapes=[pltpu.VMEM((tm, tn), F32)], compiler_params=_cparams(("parallel", "parallel", "arbitrary")),
        name=name, input_output_aliases=aliases,
    )(*arrays)


def _pack_weights(w_in, w_out, w_up, w_down):
    depth, d, cin = w_in.shape

    def body(win_ref, wout_ref, wup_ref, wdown_ref, oin_ref, oout_ref, oup_ref, odown_ref):
        oin_ref[...] = win_ref[...].T.astype(BF)
        oout_ref[...] = wout_ref[...].astype(BF)
        oup_ref[...] = wup_ref[...].astype(BF)
        odown_ref[...] = wdown_ref[...].astype(BF)

    def spec(a):
        return pl.BlockSpec((None,) + a.shape[1:], lambda l: (l, 0, 0))

    outs = [jax.ShapeDtypeStruct((depth, cin, d), BF), jax.ShapeDtypeStruct(w_out.shape, BF),
            jax.ShapeDtypeStruct(w_up.shape, BF), jax.ShapeDtypeStruct(w_down.shape, BF)]
    return pl.pallas_call(
        body, grid=(depth,), in_specs=[spec(w_in), spec(w_out), spec(w_up), spec(w_down)],
        out_specs=[pl.BlockSpec((None, cin, d), lambda l: (l, 0, 0)), spec(w_out), spec(w_up), spec(w_down)],
        out_shape=outs, compiler_params=_cparams(("arbitrary",)), name="pack_weights",
    )(w_in, w_out, w_up, w_down)


def _my_position():
    x, y, c = lax.axis_index("x"), lax.axis_index("y"), lax.axis_index("c")
    return x, y, c, 4 * x + 2 * y + c


def _peer(x, y, c, k):
    px = 1 - x if k & 4 else x
    py = 1 - y if k & 2 else y
    pc = 1 - c if k & 1 else c
    return (px, py, pc), 4 * px + 2 * py + pc


PEER_ORDER = (1, 2, 4, 3, 5, 6, 7)


class _Piece(NamedTuple):
    src: int
    send: Callable
    slot: Callable
    land_shape: tuple


HBM_SPEC = pl.BlockSpec(memory_space=pltpu.HBM)
SEM_SPEC = pl.BlockSpec(memory_space=pltpu.SEMAPHORE)
ANY_SPEC = pl.BlockSpec(memory_space=pl.ANY)


def _in_hbm(arrays):
    return [pltpu.with_memory_space_constraint(a, pltpu.HBM) for a in arrays]


def _hbm_like(arrays):
    return [pltpu.HBM(a.shape, a.dtype) for a in arrays]


def _rows_of(rows):
    return lambda ref, dev: ref.at[pl.ds(pl.multiple_of(dev * rows, 16), rows), :]


def _exchange_own(name, srcs, pieces):
    n_src, n = len(srcs), len(pieces)

    def body(*refs):
        src_refs, lands, sems = refs[:n_src], refs[n_src:n_src + n], refs[-1]
        me = _my_position()[3]
        copies = [pltpu.make_async_copy(p.send(src_refs[p.src], me), p.slot(lands[i], me), sems.at[i])
                  for i, p in enumerate(pieces)]
        for cp in copies:
            cp.start()
        for cp in copies:
            cp.wait()

    return pl.pallas_call(
        body, in_specs=[ANY_SPEC] * n_src, out_specs=[ANY_SPEC] * n,
        out_shape=[jax.ShapeDtypeStruct(p.land_shape, BF) for p in pieces],
        scratch_shapes=[pltpu.SemaphoreType.DMA((n,))], name=name,
    )(*srcs)


def _exchange_start(name, srcs, lands, pieces, groups):
    n_src, n, n_g = len(srcs), len(pieces), len(groups)

    def body(*refs):
        src_refs, land_refs = refs[:n_src], refs[n_src:n_src + n]
        sems, token = refs[n_src + n:n_src + n + 2 * n_g], refs[-1]
        x, y, c, me = _my_position()
        for g, idxs in enumerate(groups):
            for k in PEER_ORDER:
                peer, pid = _peer(x, y, c, k)
                for j, i in enumerate(idxs):
                    p = pieces[i]
                    pltpu.make_async_remote_copy(
                        src_ref=p.send(src_refs[p.src], pid), dst_ref=p.slot(land_refs[i], me),
                        send_sem=sems[2 * g].at[(k - 1) * len(idxs) + j], recv_sem=sems[2 * g + 1].at[(k - 1) * len(idxs) + j],
                        device_id=peer, device_id_type=MESH).start()
        token[...] = jnp.zeros(token.shape, F32)

    sem_shapes = [pltpu.SemaphoreType.DMA(((N_DEV - 1) * len(idxs),)) for idxs in groups for _ in range(2)]
    res = pl.pallas_call(
        body, in_specs=[HBM_SPEC] * (n_src + n),
        out_specs=[SEM_SPEC] * (2 * n_g) + [HBM_SPEC] * (n_src + n) + [pl.BlockSpec(memory_space=pltpu.VMEM)],
        out_shape=sem_shapes + _hbm_like(srcs) + _hbm_like(lands) + [jax.ShapeDtypeStruct((8, LANES), F32)],
        input_output_aliases={i: 2 * n_g + i for i in range(n_src + n)},
        compiler_params=pltpu.CompilerParams(has_side_effects=pltpu.SideEffectType.DATAFLOW_SIDE_EFFECTING),
        name=name,
    )(*_in_hbm(srcs), *_in_hbm(lands))
    sems = [(res[2 * g], res[2 * g + 1]) for g in range(n_g)]
    return sems, list(res[2 * n_g:2 * n_g + n_src]), list(res[2 * n_g + n_src:2 * n_g + n_src + n]), res[-1]


def _exchange_wait(name, srcs, lands, pieces, waits, after):
    n_src, n, n_g = len(srcs), len(lands), len(waits)

    def body(*refs):
        src_refs, land_refs = refs[:n_src], refs[n_src:n_src + n]
        sems = refs[n_src + n:n_src + n + 2 * n_g]
        x, y, c, me = _my_position()
        at = 0
        for g, (idxs, _, _) in enumerate(waits):
            for k in PEER_ORDER:
                peer, pid = _peer(x, y, c, k)
                for j, i in enumerate(idxs):
                    p = pieces[i]
                    cp = pltpu.make_async_remote_copy(
                        src_ref=p.send(src_refs[p.src], pid), dst_ref=p.slot(land_refs[at + j], pid),
                        send_sem=sems[2 * g].at[(k - 1) * len(idxs) + j], recv_sem=sems[2 * g + 1].at[(k - 1) * len(idxs) + j],
                        device_id=peer, device_id_type=MESH)
                    cp.wait_send()
                    cp.wait_recv()
            at += len(idxs)

    sem_args = [s for _, send, recv in waits for s in (send, recv)]
    res = pl.pallas_call(
        body, in_specs=[HBM_SPEC] * (n_src + n) + [SEM_SPEC] * (2 * n_g) + [ANY_SPEC],
        out_specs=[HBM_SPEC] * (n_src + n), out_shape=_hbm_like(srcs) + _hbm_like(lands),
        input_output_aliases={i: i for i in range(n_src + n)},
        compiler_params=pltpu.CompilerParams(has_side_effects=pltpu.SideEffectType.DATAFLOW_SIDE_EFFECTING),
        name=name,
    )(*srcs, *lands, *sem_args, after)
    return list(res[:n_src]), list(res[n_src:])


def _weight_pieces(p_in, p_out, p_up, p_down):
    depth, cin, d = p_in.shape
    rout, hs = p_out.shape[1], p_up.shape[2]
    pieces = []
    for l in range(depth):
        whole = functools.partial(lambda ref, dev, l: ref.at[l], l=l)
        pieces += [
            _Piece(0, whole, lambda ref, dev: _rows_of(cin)(ref.at[0], dev), (1, N_DEV * cin, d)),
            _Piece(1, whole, lambda ref, dev: _rows_of(rout)(ref.at[0], dev), (1, N_DEV * rout, d)),
            _Piece(2, whole, lambda ref, dev: ref.at[0, dev], (1, N_DEV, d, hs)),
            _Piece(3, whole, lambda ref, dev: _rows_of(hs)(ref.at[0], dev), (1, N_DEV * hs, d)),
        ]
    return pieces


def _grad_pieces(g_pair, kinds):
    pieces = []
    for i, (g, kind) in enumerate(zip(g_pair, kinds)):
        if kind == "up":
            pieces.append(_Piece(i, lambda ref, dev: ref.at[dev], lambda ref, dev: ref.at[dev], g.shape))
        else:
            rows = g.shape[0] // N_DEV
            pieces.append(_Piece(i, _rows_of(rows), lambda ref, dev: ref.at[dev], (N_DEV, rows, g.shape[1])))
    return pieces


def _all_reduce_small(vec):
    rows = vec.shape[0]

    def body(v_ref, o_ref, buf_ref, send_sems, recv_sems):
        x, y, c, me = _my_position()
        buf_ref[me] = v_ref[...]
        sends = []
        for k in PEER_ORDER:
            peer, _ = _peer(x, y, c, k)
            cp = pltpu.make_async_remote_copy(src_ref=v_ref, dst_ref=buf_ref.at[me], send_sem=send_sems.at[k - 1],
                                              recv_sem=recv_sems.at[k - 1], device_id=peer, device_id_type=MESH)
            cp.start()
            sends.append(cp)
        for k in PEER_ORDER:
            peer, pid = _peer(x, y, c, k)
            pltpu.make_async_remote_copy(src_ref=v_ref, dst_ref=buf_ref.at[pid], send_sem=send_sems.at[k - 1],
                                         recv_sem=recv_sems.at[k - 1], device_id=peer, device_id_type=MESH).wait_recv()
        for cp in sends:
            cp.wait_send()
        total = buf_ref[0]
        for dev in range(1, N_DEV):
            total = total + buf_ref[dev]
        o_ref[...] = total

    vm = pl.BlockSpec(memory_space=pltpu.VMEM)
    return pl.pallas_call(
        body, in_specs=[vm], out_specs=vm, out_shape=jax.ShapeDtypeStruct(vec.shape, F32),
        scratch_shapes=[pltpu.VMEM((N_DEV, rows, LANES), F32), pltpu.SemaphoreType.DMA((N_DEV - 1,)),
                        pltpu.SemaphoreType.DMA((N_DEV - 1,))],
        name="all_reduce_small",
    )(vec)


def _resident(block_shape, index_map):
    return pl.BlockSpec(block_shape, index_map, pipeline_mode=pl.Buffered(1))


def _fwd_inproj(layer, x, gain, g_in, cos, sin, tm=256):
    s, d = x.shape

    def body(x_ref, gain_ref, w_ref, cos_ref, sin_ref, proj_ref, h_ref):
        h = _rms_fwd(x_ref[...], gain_ref[...]).astype(BF)
        h_ref[...] = h
        cs = _tile_lanes(cos_ref[...], SEG // LANES)
        sn = _tile_lanes(sin_ref[...], SEG // LANES)
        for seg in range(N_SEG):
            acc = _dot(h, w_ref[seg * SEG:(seg + 1) * SEG, :], NT)
            if seg < 2:
                acc = acc * cs + _rope_partner(acc) * sn
            if seg == 0:
                acc = acc * ATTN_SCALE
            proj_ref[:, seg * SEG:(seg + 1) * SEG] = acc

    return pl.pallas_call(
        body, grid=(s // tm,),
        in_specs=[pl.BlockSpec((tm, d), lambda i: (i, 0)), pl.BlockSpec((None, 1, d), lambda i: (layer, 0, 0)),
                  _resident((None, PROJ_W, d), lambda i: (0, 0, 0)),
                  pl.BlockSpec((tm, LANES), lambda i: (i, 0)), pl.BlockSpec((tm, LANES), lambda i: (i, 0))],
        out_specs=[pl.BlockSpec((tm, PROJ_W), lambda i: (i, 0)), pl.BlockSpec((tm, d), lambda i: (i, 0))],
        out_shape=[jax.ShapeDtypeStruct((s, PROJ_W), F32), jax.ShapeDtypeStruct((s, d), BF)],
        compiler_params=_cparams(("parallel",)), name=f"fwd_inproj_l{layer}",
    )(x, gain, g_in, cos, sin)


def _attn_masks(n):
    row = lax.broadcasted_iota(jnp.int32, (SPAN, SPAN), 0)
    col = lax.broadcasted_iota(jnp.int32, (SPAN, SPAN), 1)
    return col <= row, (col >= row) & (n > 0), col < 64


def _attn_specs(dil, n_in_extra):
    unit = SPAN * dil
    pairs = ATTN_W // LANES
    q_spec = pl.BlockSpec((unit, LANES), lambda p, n: (n, p))
    kp_spec = pl.BlockSpec((unit, LANES), lambda p, n: (jnp.maximum(n - 1, 0), pairs + p))
    kc_spec = pl.BlockSpec((unit, LANES), lambda p, n: (n, pairs + p))
    vp_spec = pl.BlockSpec((unit, LANES), lambda p, n: (jnp.maximum(n - 1, 0), 2 * pairs + p))
    vc_spec = pl.BlockSpec((unit, LANES), lambda p, n: (n, 2 * pairs + p))
    return [q_spec, kp_spec, kc_spec, vp_spec, vc_spec] + [q_spec] * n_in_extra


def _rows(dil, r):
    return pl.ds(r, SPAN, stride=dil) if dil > 1 else slice(None)


def _attn_fwd(layer, dil, proj):
    s = proj.shape[0]
    unit = SPAN * dil

    def body(q_ref, kp_ref, kc_ref, vp_ref, vc_ref, o_ref, lse_ref):
        m_cur, m_prev, is_a = _attn_masks(pl.program_id(1))
        for r in range(dil):
            rows = _rows(dil, r)
            q = q_ref[rows, :]
            kc, kp = kc_ref[rows, :].astype(BF), kp_ref[rows, :].astype(BF)
            vc, vp = vc_ref[rows, :].astype(BF), vp_ref[rows, :].astype(BF)
            halves = []
            for sel in (is_a, jnp.logical_not(is_a)):
                qh = jnp.where(sel, q, 0.0).astype(BF)
                s_c = jnp.where(m_cur, _dot(qh, kc, NT), MASK_VALUE)
                s_p = jnp.where(m_prev, _dot(qh, kp, NT), MASK_VALUE)
                mx = jnp.maximum(jnp.max(s_c, axis=-1, keepdims=True), jnp.max(s_p, axis=-1, keepdims=True))
                p_c, p_p = jnp.exp(s_c - mx), jnp.exp(s_p - mx)
                den = jnp.sum(p_c, axis=-1, keepdims=True) + jnp.sum(p_p, axis=-1, keepdims=True)
                o = (_dot(p_c.astype(BF), vc, NN) + _dot(p_p.astype(BF), vp, NN)) / den
                halves.append((o, mx + jnp.log(den)))
            o_ref[rows, :] = jnp.where(is_a, halves[0][0], halves[1][0])
            lse_ref[rows, :] = jnp.where(is_a, halves[0][1], halves[1][1])

    out_spec = pl.BlockSpec((unit, LANES), lambda p, n: (n, p))
    return pl.pallas_call(
        body, grid=(ATTN_W // LANES, s // unit), in_specs=_attn_specs(dil, 0), out_specs=[out_spec, out_spec],
        out_shape=[jax.ShapeDtypeStruct((s, ATTN_W), F32)] * 2,
        compiler_params=_cparams(("parallel", "arbitrary")), name=f"attn_fwd_d{dil}_l{layer}",
    )(proj, proj, proj, proj, proj)


def _attn_merge(layer, outs, lses, gain, mixed, tm=512):
    s = outs[0].shape[0]

    def body(o1, o2, o3, l1, l2, l3, gain_ref, mixed_ref, o_ref, lse_ref, n_ref):
        ls = (l1[...], l2[...], l3[...])
        mx = jnp.maximum(jnp.maximum(ls[0], ls[1]), ls[2])
        ws = [jnp.exp(l - mx) for l in ls]
        den = ws[0] + ws[1] + ws[2]
        o = (ws[0] * o1[...] + ws[1] * o2[...] + ws[2] * o3[...]) / den
        o_ref[...] = o
        lse_ref[...] = mx + jnp.log(den)
        n_ref[...] = _rms_fwd(o, gain_ref[...]).astype(BF)

    blk = pl.BlockSpec((tm, ATTN_W), lambda i: (i, 0))
    return pl.pallas_call(
        body, grid=(s // tm,),
        in_specs=[blk] * 6 + [pl.BlockSpec((None, 1, ATTN_W), lambda i: (layer, 0, 0)), pl.BlockSpec(memory_space=pl.ANY)],
        out_specs=[blk, blk, blk],
        out_shape=[jax.ShapeDtypeStruct((s, ATTN_W), F32), jax.ShapeDtypeStruct((s, ATTN_W), F32),
                   jax.ShapeDtypeStruct(mixed.shape, BF)],
        input_output_aliases={7: 2},
        compiler_params=_cparams(("parallel",)), name=f"attn_merge_l{layer}",
    )(*outs, *lses, gain, mixed)


def _chunk_cumsum(x, reverse=False):
    n = x.shape[0]
    pos = lax.broadcasted_iota(jnp.int32, x.shape, 0) % HGRN_CHUNK
    for sh in (1, 2, 4, 8):
        if reverse:
            x = x + jnp.where(pos < HGRN_CHUNK - sh, pltpu.roll(x, n - sh, 0), 0.0)
        else:
            x = x + jnp.where(pos >= sh, pltpu.roll(x, sh, 0), 0.0)
    return x


def _chunk_row(x, row):
    r, n = x.shape
    x3 = x.reshape(r // HGRN_CHUNK, HGRN_CHUNK, n)
    return jnp.broadcast_to(x3[:, row:row + 1, :], x3.shape).reshape(r, n)


def _hgrn_pre(qh, z, lb):
    sig = _sigmoid(z)
    f = lb + (1.0 - lb) * sig
    k = (1.0 - lb) * _sigmoid(-z)
    sq = _sigmoid(qh)
    q = qh * sq * HGRN_SCALE
    g = _chunk_cumsum(jnp.log(f))
    g_mid = _chunk_row(g, HGRN_CHUNK // 2 - 1)
    g_last = _chunk_row(g, HGRN_CHUNK - 1)
    e_q, e_k = jnp.exp(g - g_mid), jnp.exp(g_mid - g)
    e_in, e_out = jnp.exp(g), jnp.exp(g_last - g)
    return dict(sig=sig, f=f, k=k, sq=sq, q=q, g_last=g_last, e_q=e_q, e_k=e_k, e_in=e_in, e_out=e_out,
                qt=q * e_q, kt=k * e_k, qg=q * e_in, kout=k * e_out)


def _hgrn_mask():
    row = lax.broadcasted_iota(jnp.int32, (LANES, LANES), 0)
    col = lax.broadcasted_iota(jnp.int32, (LANES, LANES), 1)
    return (row // HGRN_CHUNK == col // HGRN_CHUNK) & (col <= row)


def _hgrn_in_specs(layer, rev, nblk):
    def blk(b):
        return nblk - 1 - b if rev else b
    base = 3 * (ATTN_W // LANES)
    per = HGRN_W // LANES
    specs = [pl.BlockSpec((LANES, LANES), functools.partial(lambda h, b, seg: (blk(b), base + seg * per + h), seg=seg))
             for seg in range(4)]
    specs.append(pl.BlockSpec((None, 1, LANES), lambda h, b: (layer, 0, h)))
    specs.append(pl.BlockSpec((None, 1, LANES), lambda h, b: (layer, 0, 0)))
    return specs, blk


def _hgrn_fwd(layer, proj, lb, gain):
    s = proj.shape[0]
    nblk = s // LANES
    cpb = LANES // HGRN_CHUNK

    def body(q_ref, f_ref, i_ref, g_ref, lb_ref, gain_ref, o_ref, rec_ref, st_ref, state):
        @pl.when(pl.program_id(1) == 0)
        def _():
            state[...] = jnp.zeros(state.shape, F32)

        pre = _hgrn_pre(q_ref[...], f_ref[...], lb_ref[...])
        v = i_ref[...].astype(BF)
        a = jnp.where(_hgrn_mask(), _dot(pre["qt"].astype(BF), pre["kt"].astype(BF), NT), 0.0)
        o = _dot(a.astype(BF), v, NN)
        qg, kout = pre["qg"].astype(BF), pre["kout"].astype(BF)
        dec = jnp.exp(pre["g_last"])
        st = state[...]
        inter = []
        for c in range(cpb):
            rows = slice(c * HGRN_CHUNK, (c + 1) * HGRN_CHUNK)
            st_ref[c * LANES:(c + 1) * LANES, :] = st
            inter.append(_dot(qg[rows], st.astype(BF), NT))
            st = st * dec[c * HGRN_CHUNK:c * HGRN_CHUNK + 1, :] + _dot(v[rows], kout[rows], TN)
        state[...] = st
        o = o + jnp.concatenate(inter, axis=0)
        o_ref[...] = o
        gate = g_ref[...]
        rec_ref[...] = (_rms_fwd(o, gain_ref[...]) * (gate * _sigmoid(gate))).astype(BF)

    specs, _ = _hgrn_in_specs(layer, False, nblk)
    blk = pl.BlockSpec((LANES, LANES), lambda h, b: (b, h))
    return pl.pallas_call(
        body, grid=(HGRN_HEADS, nblk), in_specs=specs,
        out_specs=[blk, pl.BlockSpec((LANES, LANES), lambda h, b: (b, ATTN_W // LANES + h)),
                   pl.BlockSpec((None, cpb * LANES, LANES), lambda h, b: (h, b, 0))],
        out_shape=[jax.ShapeDtypeStruct((s, HGRN_W), F32), jax.ShapeDtypeStruct((s, MIX_W), BF),
                   jax.ShapeDtypeStruct((HGRN_HEADS, nblk * cpb * LANES, LANES), F32)],
        scratch_shapes=[pltpu.VMEM((LANES, LANES), F32)],
        compiler_params=_cparams(("parallel", "arbitrary")), name=f"hgrn_fwd_l{layer}",
    )(proj, proj, proj, proj, lb, gain)


def _fwd_outproj(layer, mixed, g_out, x, tm=512):
    s, d = x.shape
    mw = mixed.shape[1]

    def body(m_ref, w_ref, x_ref, o_ref):
        o_ref[...] = x_ref[...] + _dot(m_ref[...], w_ref[...], NN)

    row = pl.BlockSpec((tm, d), lambda i: (i, 0))
    return pl.pallas_call(
        body, grid=(s // tm,),
        in_specs=[pl.BlockSpec((tm, mw), lambda i: (i, 0)), _resident((None, mw, d), lambda i: (0, 0, 0)), row],
        out_specs=row, out_shape=jax.ShapeDtypeStruct((s, d), F32),
        compiler_params=_cparams(("parallel",)), name=f"fwd_outproj_l{layer}",
    )(mixed, g_out, x)


def _relu2(u):
    return jnp.square(jnp.maximum(u, 0)).astype(BF)


def _mlp_fwd(layer, x, gain, g_up, g_down, tm=256):
    s, d = x.shape
    nblk, hs = g_up.shape[1], g_up.shape[3]

    def body(x_ref, gain_ref, up_ref, down_ref, o_ref, u_ref, h_ref):
        xv = x_ref[...]
        h = _rms_fwd(xv, gain_ref[...]).astype(BF)
        h_ref[...] = h
        acc = xv
        for j in range(nblk):
            u = _dot(h, up_ref[j], NN)
            u_ref[:, j * hs:(j + 1) * hs] = u.astype(BF)
            acc = acc + _dot(_relu2(u), down_ref[j * hs:(j + 1) * hs, :], NN)
        o_ref[...] = acc

    row = pl.BlockSpec((tm, d), lambda i: (i, 0))
    return pl.pallas_call(
        body, grid=(s // tm,),
        in_specs=[row, pl.BlockSpec((None, 1, d), lambda i: (layer, 0, 0)),
                  _resident((None, nblk, d, hs), lambda i: (0, 0, 0, 0)),
                  _resident((None, nblk * hs, d), lambda i: (0, 0, 0))],
        out_specs=[row, pl.BlockSpec((tm, nblk * hs), lambda i: (i, 0)), row],
        out_shape=[jax.ShapeDtypeStruct((s, d), F32), jax.ShapeDtypeStruct((s, nblk * hs), BF),
                   jax.ShapeDtypeStruct((s, d), BF)],
        compiler_params=_cparams(("parallel",)), name=f"mlp_fwd_l{layer}",
    )(x, gain, g_up, g_down)


def _loss_head(x, gain, target, tm=512):
    s, d = x.shape

    def body(x_ref, gain_ref, t_ref, dx_ref, dxb_ref, dgain_ref, loss_ref):
        i = pl.program_id(0)
        xv, gv = x_ref[...], gain_ref[...]
        err = _rms_fwd(xv, gv) - t_ref[...]
        dx, dgain = _rms_bwd(err * (1.0 / d), xv, gv)
        dx_ref[...] = dx
        dxb_ref[...] = dx.astype(BF)
        part = _part8(dgain)
        lpart = _part8(0.5 * jnp.mean(err * err, axis=-1, keepdims=True) * jnp.ones((1, LANES), F32))

        @pl.when(i == 0)
        def _():
            dgain_ref[...] = part
            loss_ref[...] = lpart

        @pl.when(i > 0)
        def _():
            dgain_ref[...] += part
            loss_ref[...] += lpart

    row = pl.BlockSpec((tm, d), lambda i: (i, 0))
    return pl.pallas_call(
        body, grid=(s // tm,),
        in_specs=[row, pl.BlockSpec((1, d), lambda i: (0, 0)), row],
        out_specs=[row, row, pl.BlockSpec((8, d), lambda i: (0, 0)), pl.BlockSpec((8, LANES), lambda i: (0, 0))],
        out_shape=[jax.ShapeDtypeStruct((s, d), F32), jax.ShapeDtypeStruct((s, d), BF), jax.ShapeDtypeStruct((8, d), F32),
                   jax.ShapeDtypeStruct((8, LANES), F32)],
        compiler_params=_cparams(("arbitrary",)), name="loss_head",
    )(x, gain, target)


def _accumulate_rows(i, ref, part):
    @pl.when(i == 0)
    def _():
        ref[...] = part

    @pl.when(i > 0)
    def _():
        ref[...] += part


def _mlp_bwd(layer, dx, dxb, x, gain, u, g_up, g_down, tm=256):
    s, d = x.shape
    nblk, hs = g_up.shape[1], g_up.shape[3]

    def body(dx_ref, dxb_ref, x_ref, gain_ref, u_ref, up_ref, down_ref, o_ref, ob_ref, du_ref, dgain_ref):
        dxb_v = dxb_ref[...]
        acc = jnp.zeros((tm, d), F32)
        for j in range(nblk):
            cols = slice(j * hs, (j + 1) * hs)
            da = _dot(dxb_v, down_ref[cols, :], NT)
            du = (da * (2.0 * jnp.maximum(u_ref[:, cols].astype(F32), 0.0))).astype(BF)
            du_ref[:, cols] = du
            acc = acc + _dot(du, up_ref[j], NT)
        dxn, dgain = _rms_bwd(acc, x_ref[...], gain_ref[...])
        out = dx_ref[...] + dxn
        o_ref[...] = out
        ob_ref[...] = out.astype(BF)
        _accumulate_rows(pl.program_id(0), dgain_ref, _part8(dgain))

    row = pl.BlockSpec((tm, d), lambda i: (i, 0))
    wide = pl.BlockSpec((tm, nblk * hs), lambda i: (i, 0))
    return pl.pallas_call(
        body, grid=(s // tm,),
        in_specs=[row, row, row, pl.BlockSpec((None, 1, d), lambda i: (layer, 0, 0)), wide,
                  _resident((None, nblk, d, hs), lambda i: (0, 0, 0, 0)),
                  _resident((None, nblk * hs, d), lambda i: (0, 0, 0))],
        out_specs=[row, row, wide, pl.BlockSpec((8, d), lambda i: (0, 0))],
        out_shape=[jax.ShapeDtypeStruct((s, d), F32), jax.ShapeDtypeStruct((s, d), BF),
                   jax.ShapeDtypeStruct((s, nblk * hs), BF), jax.ShapeDtypeStruct((8, d), F32)],
        compiler_params=_cparams(("arbitrary",)), name=f"mlp_bwd_l{layer}",
    )(dx, dxb, x, gain, u, g_up, g_down)


def _bwd_outproj(layer, dxb, g_out, tm=512):
    s, d = dxb.shape
    mw = g_out.shape[1]

    def body(dx_ref, w_ref, o_ref):
        o_ref[...] = _dot(dx_ref[...], w_ref[...], NT)

    return pl.pallas_call(
        body, grid=(s // tm,),
        in_specs=[pl.BlockSpec((tm, d), lambda i: (i, 0)), _resident((None, mw, d), lambda i: (0, 0, 0))],
        out_specs=pl.BlockSpec((tm, mw), lambda i: (i, 0)), out_shape=jax.ShapeDtypeStruct((s, mw), F32),
        compiler_params=_cparams(("parallel",)), name=f"bwd_outproj_l{layer}",
    )(dxb, g_out)


def _attn_norm_bwd(layer, dmixed, o, gain, tm=512):
    s = o.shape[0]

    def body(dm_ref, o_ref, gain_ref, do_ref, delta_ref, dgain_ref):
        i = pl.program_id(0)
        ov = o_ref[...]
        do, dgain = _rms_bwd(dm_ref[...], ov, gain_ref[...])
        do_ref[...] = do
        row = lax.broadcasted_iota(jnp.int32, (ATTN_W, ATTN_W), 0)
        col = lax.broadcasted_iota(jnp.int32, (ATTN_W, ATTN_W), 1)
        same_head = jnp.where(row // 64 == col // 64, 1.0, 0.0)
        delta_ref[...] = jnp.dot(do * ov, same_head, precision=lax.Precision.HIGHEST, preferred_element_type=F32)
        part = _part8(dgain)

        @pl.when(i == 0)
        def _():
            dgain_ref[...] = part

        @pl.when(i > 0)
        def _():
            dgain_ref[...] += part

    blk = pl.BlockSpec((tm, ATTN_W), lambda i: (i, 0))
    return pl.pallas_call(
        body, grid=(s // tm,), in_specs=[blk, blk, pl.BlockSpec((None, 1, ATTN_W), lambda i: (layer, 0, 0))],
        out_specs=[blk, blk, pl.BlockSpec((8, ATTN_W), lambda i: (0, 0))],
        out_shape=[jax.ShapeDtypeStruct((s, ATTN_W), F32), jax.ShapeDtypeStruct((s, ATTN_W), F32),
                   jax.ShapeDtypeStruct((8, ATTN_W), F32)],
        compiler_params=_cparams(("arbitrary",)), name=f"attn_norm_bwd_l{layer}",
    )(dmixed, o, gain)


def _attn_bwd(layer, dil, proj, do, lse, delta):
    s = proj.shape[0]
    unit = SPAN * dil

    def body(q_ref, kp_ref, kc_ref, vp_ref, vc_ref, do_ref, lse_ref, delta_ref, dq_ref, dkc_ref, dkp_ref, dvc_ref, dvp_ref):
        m_cur, m_prev, is_a = _attn_masks(pl.program_id(1))
        for r in range(dil):
            rows = _rows(dil, r)
            q, dov = q_ref[rows, :], do_ref[rows, :]
            kc, kp = kc_ref[rows, :], kp_ref[rows, :]
            vc, vp = vc_ref[rows, :].astype(BF), vp_ref[rows, :].astype(BF)
            lse_v, delta_v = lse_ref[rows, :], delta_ref[rows, :]
            dq = jnp.zeros((SPAN, LANES), F32)
            dkc, dkp, dvc, dvp = dq, dq, dq, dq
            for half, sel in enumerate((is_a, jnp.logical_not(is_a))):
                col = slice(64 * half, 64 * half + 1)
                lse_h, delta_h = lse_v[:, col], delta_v[:, col]
                qh = jnp.where(sel, q, 0.0).astype(BF)
                doh = jnp.where(sel, dov, 0.0).astype(BF)
                for k_f32, v_bf, mask, which in ((kc, vc, m_cur, 0), (kp, vp, m_prev, 1)):
                    kh = jnp.where(sel, k_f32, 0.0).astype(BF)
                    sc = jnp.where(mask, _dot(qh, kh, NT), MASK_VALUE)
                    p = jnp.exp(sc - lse_h)
                    dv = _dot(p.astype(BF), doh, TN)
                    ds = (p * (_dot(doh, v_bf, NT) - delta_h)).astype(BF)
                    dq = dq + _dot(ds, kh, NN)
                    dk = _dot(ds, qh, TN)
                    if which == 0:
                        dkc, dvc = dkc + dk, dvc + dv
                    else:
                        dkp, dvp = dkp + dk, dvp + dv
            dq_ref[rows, :] = dq
            dkc_ref[rows, :] = dkc
            dkp_ref[rows, :] = dkp
            dvc_ref[rows, :] = dvc
            dvp_ref[rows, :] = dvp

    out_spec = pl.BlockSpec((unit, LANES), lambda p, n: (n, p))
    return pl.pallas_call(
        body, grid=(ATTN_W // LANES, s // unit), in_specs=_attn_specs(dil, 3), out_specs=[out_spec] * 5,
        out_shape=[jax.ShapeDtypeStruct((s, ATTN_W), F32)] * 5,
        compiler_params=_cparams(("parallel", "arbitrary")), name=f"attn_bwd_d{dil}_l{layer}",
    )(proj, proj, proj, proj, proj, do, lse, delta)


def _attn_combine(layer, parts, cos, sin):
    s = parts[0][0].shape[0]
    nblk = s // SPAN
    arrays, specs = [], []
    for dil, (dq, dkc, dkp, dvc, dvp) in zip(DILATIONS, parts):
        here = pl.BlockSpec((SPAN, ATTN_W), lambda i: (i, 0))
        ahead = pl.BlockSpec((SPAN, ATTN_W), functools.partial(lambda i, dil: (jnp.minimum(i + dil, nblk - 1), 0), dil=dil))
        arrays += [dq, dkc, dkp, dvc, dvp]
        specs += [here, here, ahead, here, ahead]
    tab = pl.BlockSpec((SPAN, LANES), lambda i: (i, 0))

    def body(*refs):
        cos_ref, sin_ref, out_ref = refs[15], refs[16], refs[17]
        i = pl.program_id(0)
        dq = dk = dv = jnp.zeros((SPAN, ATTN_W), F32)
        for p, dil in enumerate(DILATIONS):
            dq_r, dkc_r, dkp_r, dvc_r, dvp_r = refs[5 * p:5 * p + 5]
            has_next = i + dil < nblk
            dq = dq + dq_r[...]
            dk = dk + dkc_r[...] + jnp.where(has_next, dkp_r[...], 0.0)
            dv = dv + dvc_r[...] + jnp.where(has_next, dvp_r[...], 0.0)
        cs = _tile_lanes(cos_ref[...], ATTN_W // LANES)
        sn = _tile_lanes(sin_ref[...], ATTN_W // LANES)
        out_ref[0] = ((dq * cs - _rope_partner(dq) * sn) * ATTN_SCALE).astype(BF)
        out_ref[1] = (dk * cs - _rope_partner(dk) * sn).astype(BF)
        out_ref[2] = dv.astype(BF)

    return pl.pallas_call(
        body, grid=(nblk,), in_specs=specs + [tab, tab],
        out_specs=pl.BlockSpec((3, SPAN, ATTN_W), lambda i: (0, i, 0)),
        out_shape=jax.ShapeDtypeStruct((3, s, ATTN_W), BF),
        compiler_params=_cparams(("parallel",)), name=f"attn_combine_l{layer}",
    )(*arrays, cos, sin)


def _hgrn_bwd(layer, proj, lb, gain, o, dmixed, states):
    s = proj.shape[0]
    nblk = s // LANES
    cpb = LANES // HGRN_CHUNK

    def body(q_ref, f_ref, i_ref, g_ref, lb_ref, gain_ref, o_ref, drec_ref, st_ref, dseg_ref, dlb_ref, dgain_ref, dstate):
        head, step = pl.program_id(0), pl.program_id(1)

        @pl.when(step == 0)
        def _():
            dstate[...] = jnp.zeros(dstate.shape, F32)

        lbv, gv = lb_ref[...], gain_ref[...]
        qh, z, gate_in = q_ref[...], f_ref[...], g_ref[...]
        pre = _hgrn_pre(qh, z, lbv)
        v = i_ref[...].astype(BF)
        sg = _sigmoid(gate_in)
        gate = gate_in * sg
        ov, drec = o_ref[...], drec_ref[...]
        do, dgain = _rms_bwd(drec * gate, ov, gv)
        dgate_in = drec * _rms_fwd(ov, gv) * (sg * (1.0 + gate_in * (1.0 - sg)))
        mask = _hgrn_mask()
        qt, kt = pre["qt"].astype(BF), pre["kt"].astype(BF)
        do_b = do.astype(BF)
        a = jnp.where(mask, _dot(qt, kt, NT), 0.0).astype(BF)
        da = jnp.where(mask, _dot(do_b, v, NT), 0.0).astype(BF)
        dv = _dot(a, do_b, TN)
        dqt = _dot(da, kt, NN)
        dkt = _dot(da, qt, TN)
        qg, kout = pre["qg"].astype(BF), pre["kout"].astype(BF)
        dec = jnp.exp(pre["g_last"])
        dst = dstate[...]
        dqg_c, dkout_c, dv_c, dgl_c = [None] * cpb, [None] * cpb, [None] * cpb, [None] * cpb
        for c in reversed(range(cpb)):
            rows = slice(c * HGRN_CHUNK, (c + 1) * HGRN_CHUNK)
            st_prev = st_ref[c * LANES:(c + 1) * LANES, :]
            dec_c = dec[c * HGRN_CHUNK:c * HGRN_CHUNK + 1, :]
            dst_b = dst.astype(BF)
            dqg_c[c] = _dot(do_b[rows], st_prev.astype(BF), NN)
            dkout_c[c] = _dot(v[rows], dst_b, NN)
            dv_c[c] = _dot(kout[rows], dst_b, NT)
            dgl_c[c] = (jnp.sum(dst * st_prev, axis=0, keepdims=True) * dec_c
                        + jnp.sum(dkout_c[c] * pre["kout"][rows], axis=0, keepdims=True))
            dst = dst * dec_c + _dot(do_b[rows], qg[rows], TN)
        dstate[...] = dst
        dqg, dkout = jnp.concatenate(dqg_c, axis=0), jnp.concatenate(dkout_c, axis=0)
        dv = dv + jnp.concatenate(dv_c, axis=0)
        dgl = jnp.concatenate([jnp.broadcast_to(t, (HGRN_CHUNK, LANES)) for t in dgl_c], axis=0)
        pos = lax.broadcasted_iota(jnp.int32, (LANES, LANES), 0) % HGRN_CHUNK
        dq = dqt * pre["e_q"] + dqg * pre["e_in"]
        dk = dkt * pre["e_k"] + dkout * pre["e_out"]
        dg = (dqt * pre["qt"] - dkt * pre["kt"] + dqg * pre["qg"] - dkout * pre["kout"]
              + jnp.where(pos == HGRN_CHUNK - 1, dgl, 0.0))
        dlogf = _chunk_cumsum(dg, reverse=True)
        sig, sq = pre["sig"], pre["sq"]
        df = dlogf / pre["f"] - dk
        dseg_ref[0] = (dq * HGRN_SCALE * (sq * (1.0 + qh * (1.0 - sq)))).astype(BF)
        dseg_ref[1] = (df * (1.0 - lbv) * sig * (1.0 - sig)).astype(BF)
        dseg_ref[2] = dv.astype(BF)
        dseg_ref[3] = dgate_in.astype(BF)
        dlb_part = _part8(df * (1.0 - sig))
        dgain_part = _part8(dgain)

        @pl.when(step == 0)
        def _():
            dlb_ref[...] = dlb_part

        @pl.when(step > 0)
        def _():
            dlb_ref[...] += dlb_part

        @pl.when((step == 0) & (head == 0))
        def _():
            dgain_ref[...] = dgain_part

        @pl.when((step > 0) | (head > 0))
        def _():
            dgain_ref[...] += dgain_part

    specs, blk = _hgrn_in_specs(layer, True, nblk)
    per = HGRN_W // LANES
    specs += [pl.BlockSpec((LANES, LANES), lambda h, b: (blk(b), h)),
              pl.BlockSpec((LANES, LANES), lambda h, b: (blk(b), per + h)),
              pl.BlockSpec((None, cpb * LANES, LANES), lambda h, b: (h, blk(b), 0))]
    return pl.pallas_call(
        body, grid=(HGRN_HEADS, nblk), in_specs=specs,
        out_specs=[pl.BlockSpec((4, LANES, LANES), lambda h, b: (0, blk(b), h)),
                   pl.BlockSpec((8, LANES), lambda h, b: (0, h)), pl.BlockSpec((8, LANES), lambda h, b: (0, 0))],
        out_shape=[jax.ShapeDtypeStruct((4, s, HGRN_W), BF), jax.ShapeDtypeStruct((8, HGRN_W), F32),
                   jax.ShapeDtypeStruct((8, LANES), F32)],
        scratch_shapes=[pltpu.VMEM((LANES, LANES), F32)],
        compiler_params=_cparams(("arbitrary", "arbitrary")), name=f"hgrn_bwd_l{layer}",
    )(proj, proj, proj, proj, lb, gain, o, dmixed, states)


def _bwd_inproj(layer, dqkv, dhg, g_in, x, gain, dres, tm=256):
    s, d = x.shape

    def body(dqkv_ref, dhg_ref, w_ref, x_ref, gain_ref, dres_ref, dx_ref, dxb_ref, dgain_ref):
        acc = jnp.zeros((tm, d), F32)
        for seg in range(N_SEG):
            a = dqkv_ref[seg] if seg < 3 else dhg_ref[seg - 3]
            acc = acc + _dot(a, w_ref[seg * SEG:(seg + 1) * SEG, :], NN)
        dx, dgain = _rms_bwd(acc, x_ref[...], gain_ref[...])
        out = dres_ref[...] + dx
        dx_ref[...] = out
        dxb_ref[...] = out.astype(BF)
        _accumulate_rows(pl.program_id(0), dgain_ref, _part8(dgain))

    row = pl.BlockSpec((tm, d), lambda i: (i, 0))
    return pl.pallas_call(
        body, grid=(s // tm,),
        in_specs=[pl.BlockSpec((3, tm, SEG), lambda i: (0, i, 0)), pl.BlockSpec((4, tm, SEG), lambda i: (0, i, 0)),
                  _resident((None, PROJ_W, d), lambda i: (0, 0, 0)), row,
                  pl.BlockSpec((None, 1, d), lambda i: (layer, 0, 0)), row],
        out_specs=[row, row, pl.BlockSpec((8, d), lambda i: (0, 0))],
        out_shape=[jax.ShapeDtypeStruct((s, d), F32), jax.ShapeDtypeStruct((s, d), BF), jax.ShapeDtypeStruct((8, d), F32)],
        compiler_params=_cparams(("arbitrary",)), name=f"bwd_inproj_l{layer}",
    )(dqkv, dhg, g_in, x, gain, dres)


def _adamw(w, g, m, v):
    m2 = ADAM_B1 * m + (1.0 - ADAM_B1) * g
    v2 = ADAM_B2 * v + (1.0 - ADAM_B2) * (g * g)
    m_hat = m2 / (1.0 - ADAM_B1 ** ADAM_STEP)
    v_hat = v2 / (1.0 - ADAM_B2 ** ADAM_STEP)
    delta = -ADAM_LR * (m_hat / (jnp.sqrt(v_hat) + ADAM_EPS) + ADAM_WD * w)
    return delta, m2, v2


def _adam_big(name, parts, w, m, v, transpose, row_tiles):
    depth = w.shape[0]
    r, c = parts[0].shape[1], parts[0].shape[2]
    if transpose:
        tc = c // row_tiles
        p_spec = pl.BlockSpec((N_DEV, r, tc), lambda t: (0, 0, t))
        w_spec = pl.BlockSpec((depth, tc, r), lambda t: (0, t, 0))
    else:
        tr = r // row_tiles
        p_spec = pl.BlockSpec((N_DEV, tr, c), lambda t: (0, t, 0))
        w_spec = pl.BlockSpec((depth, tr, c), lambda t: (0, t, 0))

    def body(*refs):
        p_refs = refs[:depth]
        w_ref, m_ref, v_ref, g_ref, d_ref, m2_ref, v2_ref = refs[depth:]
        for l in range(depth):
            g = p_refs[l][0].astype(F32)
            for dev in range(1, N_DEV):
                g = g + p_refs[l][dev].astype(F32)
            if transpose:
                g = g.T
            delta, m2, v2 = _adamw(w_ref[l], g, m_ref[l], v_ref[l])
            g_ref[l] = g
            d_ref[l] = delta
            m2_ref[l] = m2
            v2_ref[l] = v2

    return pl.pallas_call(
        body, grid=(row_tiles,), in_specs=[p_spec] * depth + [w_spec] * 3, out_specs=[w_spec] * 4,
        out_shape=[jax.ShapeDtypeStruct(w.shape, F32)] * 4,
        compiler_params=_cparams(("parallel",)), name=name,
    )(*parts, w, m, v)


def _adam_small(g, w, m, v):
    def body(g_ref, w_ref, m_ref, v_ref, d_ref, m2_ref, v2_ref):
        delta, m2, v2 = _adamw(w_ref[...], g_ref[...], m_ref[...], v_ref[...])
        d_ref[...] = delta
        m2_ref[...] = m2
        v2_ref[...] = v2

    vm = pl.BlockSpec(memory_space=pltpu.VMEM)
    return pl.pallas_call(body, in_specs=[vm] * 4, out_specs=[vm] * 3, out_shape=[jax.ShapeDtypeStruct(g.shape, F32)] * 3,
                          name="adam_small")(g, w, m, v)


def _lower_bounds(logits):
    def body(l_ref, lb_ref, jac_ref):
        l0, l1 = l_ref[0:1, :], l_ref[1:2, :]
        mx = jnp.maximum(l0, l1)
        e0, e1 = jnp.exp(l0 - mx), jnp.exp(l1 - mx)
        p0, p1 = e0 / (e0 + e1), e1 / (e0 + e1)
        lb_ref[0:1, :] = p0 - p0
        lb_ref[1:2, :] = (p0 + p1) - p0
        jac_ref[0:1, :] = -p0 * p1
        jac_ref[1:2, :] = p0 * p1

    vm = pl.BlockSpec(memory_space=pltpu.VMEM)
    return pl.pallas_call(body, in_specs=[vm], out_specs=[vm, vm], out_shape=[jax.ShapeDtypeStruct(logits.shape, F32)] * 2,
                          name="hgrn_lower_bounds")(logits)


def _rope_tables(s):
    half = 32
    inv_freq = ROPE_THETA ** (-jnp.arange(half, dtype=F32) / half)
    ang = jnp.arange(s, dtype=jnp.int32).astype(F32)[:, None] * inv_freq[None, :]
    cos, sin = jnp.cos(ang), jnp.sin(ang)
    return jnp.concatenate([cos] * 4, axis=1), jnp.concatenate([-sin, sin, -sin, sin], axis=1)


SMALL_NAMES = ("norm_mix", "attn_out_gain", "hgrn_lb_logits", "hgrn_out_gain", "norm_mlp", "norm_final")


def _pack_small(vals):
    flat = jnp.concatenate([v.reshape(-1) for v in vals])
    rows = -(-flat.shape[0] // (8 * LANES)) * 8
    return jnp.pad(flat, (0, rows * LANES - flat.shape[0])).reshape(rows, LANES)


def _unpack_small(packed, like):
    flat, out, off = packed.reshape(-1), [], 0
    for v in like:
        out.append(flat[off:off + v.size].reshape(v.shape))
        off += v.size
    return out


def kernel(x, norm_mix, w_in, attn_out_gain, hgrn_lb_logits, hgrn_out_gain, w_out, norm_mlp, w_up, w_down, norm_final, loss_target, m_norm_mix, m_w_in, m_attn_out_gain, m_hgrn_lb_logits, m_hgrn_out_gain, m_w_out, m_norm_mlp, m_w_up, m_w_down, m_norm_final, v_norm_mix, v_w_in, v_attn_out_gain, v_hgrn_lb_logits, v_hgrn_out_gain, v_w_out, v_norm_mlp, v_w_up, v_w_down, v_norm_final):
    depth = w_in.shape[0]
    assert depth == 2 and x.shape[0] == 1
    s, d = x.shape[1], x.shape[2]
    x0 = x[0]
    target = loss_target[0]
    cos, sin = _rope_tables(s)
    g_mix, g_attn, g_hg, g_mlp = (norm_mix[:, None, :], attn_out_gain[:, None, :], hgrn_out_gain[:, None, :],
                                  norm_mlp[:, None, :])
    lb, lb_jac = _lower_bounds(hgrn_lb_logits)
    lb3 = lb[:, None, :]

    shards = list(_pack_weights(w_in, w_out, w_up, w_down))
    w_pieces = _weight_pieces(*shards)
    w_groups = [[0], [1, 2, 3], [4, 5, 6, 7]]
    lands = _exchange_own("all_gather_own", shards, w_pieces)
    w_sems, shards, lands, token = _exchange_start("all_gather_start", shards, lands, w_pieces, w_groups)

    def weights_ready(group, after):
        nonlocal shards
        idxs = w_groups[group]
        shards, got = _exchange_wait(f"all_gather_wait{group}", shards, [lands[i] for i in idxs], w_pieces,
                                     [(idxs, *w_sems[group])], after)
        return got

    def tied(small_arr, tok):
        return small_arr + tok[0, 0]

    saved = []
    xl = x0
    full = [None] * depth
    for l in range(depth):
        if l == 0:
            (full_in,) = weights_ready(0, token)
        else:
            full_in, full_out, full_up, full_down = weights_ready(2, xl)
        proj, h = _fwd_inproj(l, xl, g_mix, full_in, cos, sin)
        fw = [_attn_fwd(l, dil, proj) for dil in DILATIONS]
        o_hg, mixed, states = _hgrn_fwd(l, proj, lb3, g_hg)
        o_attn, lse, mixed = _attn_merge(l, [f[0] for f in fw], [f[1] for f in fw], g_attn, mixed)
        if l == 0:
            full_out, full_up, full_down = weights_ready(1, mixed)
        x_mid = _fwd_outproj(l, mixed, full_out, xl)
        x_next, u, h2 = _mlp_fwd(l, x_mid, g_mlp, full_up, full_down)
        saved.append((xl, proj, h, o_attn, lse, o_hg, states, mixed, x_mid, u, h2))
        full[l] = (full_in, full_out, full_up, full_down)
        xl = x_next
    dx, dxb, dnorm_final8, loss8 = _loss_head(xl, norm_final[None, :], target)
    loss = lax.psum(jnp.sum(loss8[:, 0]), ("x", "y", "c"))

    exchanges = []

    def scatter(tag, grads, kinds):
        pieces = _grad_pieces(grads, kinds)
        own = _exchange_own(f"reduce_scatter_own_{tag}", grads, pieces)
        sems, grads, own, tok = _exchange_start(f"reduce_scatter_start_{tag}", grads, own, pieces, [list(range(len(pieces)))])
        exchanges.append((grads, own, pieces, sems[0]))
        return tok

    small = {}
    for l in reversed(range(depth)):
        xl, proj, h, o_attn, lse, o_hg, states, mixed, x_mid, u, h2 = saved[l]
        full_in, full_out, full_up, full_down = full[l]
        hs = full_up.shape[3]
        gw_down = _mm_tn(f"grad_w_down_l{l}", u, dxb, u.shape[1], a_fn=_relu2)
        dx_mid, dx_mid_b, du, dmlp8 = _mlp_bwd(l, dx, dxb, x_mid, g_mlp, u, full_up, full_down)
        gw_up = _mm_tn(f"grad_w_up_l{l}", h2, du, d, out_block_w=hs)
        g_attn_t = tied(g_attn, scatter(f"mlp_l{l}", [gw_down, gw_up], ["rows", "up"]))
        dmixed = _bwd_outproj(l, dx_mid_b, full_out)
        gw_out = _mm_tn(f"grad_w_out_l{l}", mixed, dx_mid_b, mixed.shape[1])
        do, delta, dattn8 = _attn_norm_bwd(l, dmixed, o_attn, g_attn_t)
        parts = [_attn_bwd(l, dil, proj, do, lse, delta) for dil in DILATIONS]
        dqkv = _attn_combine(l, parts, cos, sin)
        dhg, dlb8, dhgain8 = _hgrn_bwd(l, proj, lb3, g_hg, o_hg, dmixed, states)
        gin = _mm_tn(f"grad_w_in_qkv_l{l}", dqkv, h, PROJ_W, a_lead=True)
        gw_in = _mm_tn(f"grad_w_in_hg_l{l}", dhg, h, PROJ_W, a_lead=True, out_block_off=3, prev=gin)
        g_mix_t = tied(g_mix, scatter(f"mix_l{l}", [gw_out, gw_in], ["rows", "rows"]))
        dx, dxb, dmix8 = _bwd_inproj(l, dqkv, dhg, full_in, xl, g_mix_t, dx_mid)
        small[l] = (dmix8, dattn8, dlb8, dhgain8, dmlp8)

    def fin(p8):
        return jnp.sum(p8, axis=0)
    dlogits = lb_jac * fin(small[1][2])[None, :]
    small_grads = [jnp.stack([fin(small[l][0]) for l in range(depth)]), jnp.stack([fin(small[l][1]) for l in range(depth)]),
                   dlogits, jnp.stack([fin(small[l][3]) for l in range(depth)]),
                   jnp.stack([fin(small[l][4]) for l in range(depth)]), fin(dnorm_final8)]
    small_w = [norm_mix, attn_out_gain, hgrn_lb_logits, hgrn_out_gain, norm_mlp, norm_final]
    small_m = [m_norm_mix, m_attn_out_gain, m_hgrn_lb_logits, m_hgrn_out_gain, m_norm_mlp, m_norm_final]
    small_v = [v_norm_mix, v_attn_out_gain, v_hgrn_lb_logits, v_hgrn_out_gain, v_norm_mlp, v_norm_final]
    g_small = _all_reduce_small(_pack_small(small_grads))
    d_small, m_small, v_small = _adam_small(g_small, _pack_small(small_w), _pack_small(small_m), _pack_small(small_v))
    gs, ds, ms, vs = (_unpack_small(t, small_w) for t in (g_small, d_small, m_small, v_small))

    all_grads, all_lands, all_pieces, waits = [], [], [], []
    for grads, own, pieces, (send, recv) in exchanges:
        first = len(all_pieces)
        all_pieces += [p._replace(src=p.src + len(all_grads)) for p in pieces]
        waits.append((list(range(first, first + len(pieces))), send, recv))
        all_grads += grads
        all_lands += own
    _, landed = _exchange_wait("reduce_scatter_wait", all_grads, all_lands, all_pieces, waits, d_small)
    r_down, r_up, r_out, r_in = ([landed[4 + i], landed[i]] for i in range(4))
    big = {
        "w_in": _adam_big("adam_w_in", r_in, w_in, m_w_in, v_w_in, True, 4),
        "w_out": _adam_big("adam_w_out", r_out, w_out, m_w_out, v_w_out, False, 1),
        "w_up": _adam_big("adam_w_up", r_up, w_up, m_w_up, v_w_up, False, 2),
        "w_down": _adam_big("adam_w_down", r_down, w_down, m_w_down, v_w_down, False, 4),
    }

    def gather(idx, small_list):
        by_name = dict(zip(SMALL_NAMES, small_list))
        return [by_name["norm_mix"], big["w_in"][idx], by_name["attn_out_gain"], by_name["hgrn_lb_logits"],
                by_name["hgrn_out_gain"], big["w_out"][idx], by_name["norm_mlp"], big["w_up"][idx], big["w_down"][idx],
                by_name["norm_final"]]

    return (loss, dx[None], *gather(0, gs), *gather(1, ds), *gather(2, ms), *gather(3, vs))
```

```python
import functools
from typing import Callable, NamedTuple

import jax
import jax.numpy as jnp
from jax import lax
from jax.experimental import pallas as pl
from jax.experimental.pallas import tpu as pltpu

F32 = jnp.float32
BF = jnp.bfloat16

N_DEV = 8
ATTN_W = 512
HGRN_W = 512
HGRN_HEADS = 4
HGRN_DIM = 128
SEG = 512
N_SEG = 7
PROJ_W = N_SEG * SEG
MIX_W = ATTN_W + HGRN_W
SPAN = 128
DILATIONS = (1, 4, 16)
HGRN_CHUNK = 16
ROPE_THETA = 10000.0
NORM_EPS = 1e-6
MASK_VALUE = -1e30
ATTN_SCALE = 0.125
HGRN_SCALE = HGRN_DIM ** -0.5
ADAM_LR = 0.001
ADAM_B1 = 0.9
ADAM_B2 = 0.999
ADAM_EPS = 1e-08
ADAM_WD = 0.01
ADAM_STEP = 10
LANES = 128
VMEM_LIMIT = 56 * 1024 * 1024

NN = ((1,), (0,))
NT = ((1,), (1,))
TN = ((0,), (0,))
MESH = pl.DeviceIdType.MESH


def _dot(a, b, dims):
    return lax.dot_general(a, b, (dims, ((), ())), preferred_element_type=F32)


def _cparams(sem):
    return pltpu.CompilerParams(dimension_semantics=sem, vmem_limit_bytes=VMEM_LIMIT)


def _part8(x):
    r, n = x.shape
    return jnp.sum(x.reshape(r // 8, 8, n), axis=0)


def _sigmoid(x):
    return 1.0 / (1.0 + jnp.exp(-x))


def _rms_fwd(x, gain):
    r = lax.rsqrt(jnp.mean(x * x, axis=-1, keepdims=True) + NORM_EPS)
    return x * r * gain


def _rms_bwd(dy, x, gain):
    r = lax.rsqrt(jnp.mean(x * x, axis=-1, keepdims=True) + NORM_EPS)
    xn = x * r
    dxn = dy * gain
    dx = r * (dxn - xn * jnp.mean(dxn * xn, axis=-1, keepdims=True))
    return dx, dy * xn


def _rope_partner(x):
    n = x.shape[-1]
    lane = lax.broadcasted_iota(jnp.int32, x.shape, x.ndim - 1)
    return jnp.where((lane % 64) < 32, pltpu.roll(x, n - 32, x.ndim - 1), pltpu.roll(x, 32, x.ndim - 1))


def _tile_lanes(t, reps):
    return jnp.concatenate([t] * reps, axis=-1)


def _mm_tn(name, a, b, out_rows, a_lead=False, out_block_off=0, prev=None, out_block_w=None, a_fn=None,
           tm=512, tn=1024, tk=1024):
    kdim, n = b.shape
    m = a.shape[-1]
    tm, tn, tk = min(tm, m), min(tn, n), min(tk, kdim)
    mt, nk = m // tm, kdim // tk
    n_lead = a.shape[0] if a_lead else 1
    if a_lead:
        a_spec = pl.BlockSpec((None, tk, tm), lambda i, j, k: (i // mt, k, i % mt))
    else:
        a_spec = pl.BlockSpec((tk, tm), lambda i, j, k: (k, i))
    b_spec = pl.BlockSpec((tk, tn), lambda i, j, k: (k, j))
    if out_block_w:
        nb = tn // out_block_w
        o_shape = jax.ShapeDtypeStruct((n // out_block_w, out_rows, out_block_w), BF)
        o_spec = pl.BlockSpec((nb, tm, out_block_w), lambda i, j, k: (j, i + out_block_off, 0))
    else:
        nb = 0
        o_shape = jax.ShapeDtypeStruct((out_rows, n), BF)
        o_spec = pl.BlockSpec((tm, tn), lambda i, j, k: (i + out_block_off, j))
    arrays, specs, aliases = [a, b], [a_spec, b_spec], {}
    if prev is not None:
        arrays.append(prev)
        specs.append(pl.BlockSpec(memory_space=pl.ANY))
        aliases = {2: 0}

    def body(*refs):
        a_ref, b_ref, o_ref, acc = refs[0], refs[1], refs[-2], refs[-1]
        k = pl.program_id(2)

        @pl.when(k == 0)
        def _():
            acc[...] = jnp.zeros(acc.shape, F32)

        av = a_ref[...]
        if a_fn is not None:
            av = a_fn(av)
        acc[...] += _dot(av, b_ref[...], TN)

        @pl.when(k == nk - 1)
        def _():
            if nb:
                for t in range(nb):
                    o_ref[t] = acc[:, t * out_block_w:(t + 1) * out_block_w].astype(BF)
            else:
                o_ref[...] = acc[...].astype(BF)

    return pl.pallas_call(
        body, grid=(n_lead * mt, n // tn, nk), in_specs=specs, out_specs=o_spec, out_shape=o_shape,
        scratch_shapes=[pltpu.VMEM((tm, tn), F32)], compiler_params=_cparams(("parallel", "parallel", "arbitrary")),
        name=name, input_output_aliases=aliases,
    )(*arrays)


def _pack_weights(w_in, w_out, w_up, w_down):
    depth, d, cin = w_in.shape

    def body(win_ref, wout_ref, wup_ref, wdown_ref, oin_ref, oout_ref, oup_ref, odown_ref):
        oin_ref[...] = win_ref[...].T.astype(BF)
        oout_ref[...] = wout_ref[...].astype(BF)
        oup_ref[...] = wup_ref[...].astype(BF)
        odown_ref[...] = wdown_ref[...].astype(BF)

    def spec(a):
        return pl.BlockSpec((None,) + a.shape[1:], lambda l: (l, 0, 0))

    outs = [jax.ShapeDtypeStruct((depth, cin, d), BF), jax.ShapeDtypeStruct(w_out.shape, BF),
            jax.ShapeDtypeStruct(w_up.shape, BF), jax.ShapeDtypeStruct(w_down.shape, BF)]
    return pl.pallas_call(
        body, grid=(depth,), in_specs=[spec(w_in), spec(w_out), spec(w_up), spec(w_down)],
        out_specs=[pl.BlockSpec((None, cin, d), lambda l: (l, 0, 0)), spec(w_out), spec(w_up), spec(w_down)],
        out_shape=outs, compiler_params=_cparams(("arbitrary",)), name="pack_weights",
    )(w_in, w_out, w_up, w_down)


def _my_position():
    x, y, c = lax.axis_index("x"), lax.axis_index("y"), lax.axis_index("c")
    return x, y, c, 4 * x + 2 * y + c


def _peer(x, y, c, k):
    px = 1 - x if k & 4 else x
    py = 1 - y if k & 2 else y
    pc = 1 - c if k & 1 else c
    return (px, py, pc), 4 * px + 2 * py + pc


PEER_ORDER = (1, 2, 4, 3, 5, 6, 7)


class _Piece(NamedTuple):
    src: int
    send: Callable
    slot: Callable
    land_shape: tuple
    own_src: tuple
    own_slot: tuple


HBM_SPEC = pl.BlockSpec(memory_space=pltpu.HBM)
SEM_SPEC = pl.BlockSpec(memory_space=pltpu.SEMAPHORE)
ANY_SPEC = pl.BlockSpec(memory_space=pl.ANY)


def _in_hbm(arrays):
    return [pltpu.with_memory_space_constraint(a, pltpu.HBM) for a in arrays]


def _hbm_like(arrays):
    return [pltpu.HBM(a.shape, a.dtype) for a in arrays]


def _rows_of(rows):
    return lambda ref, dev: ref.at[pl.ds(pl.multiple_of(dev * rows, 16), rows), :]


def _exchange_own(name, me, srcs, pieces):
    n = len(pieces)

    def body(me_ref, *refs):
        for i in range(n):
            refs[n + i][...] = refs[i][...]

    def spec(block_and_index):
        block, index = block_and_index
        return pl.BlockSpec(block, lambda i, me_ref: index(me_ref[0]))

    return pl.pallas_call(
        body,
        grid_spec=pltpu.PrefetchScalarGridSpec(
            num_scalar_prefetch=1, grid=(1,), in_specs=[spec(p.own_src) for p in pieces],
            out_specs=[spec(p.own_slot) for p in pieces]),
        out_shape=[jax.ShapeDtypeStruct(p.land_shape, BF) for p in pieces],
        compiler_params=_cparams(("arbitrary",)), name=name,
    )(me, *[srcs[p.src] for p in pieces])


def _exchange_start(name, srcs, lands, pieces, groups):
    n_src, n, n_g = len(srcs), len(pieces), len(groups)

    def body(*refs):
        src_refs, land_refs = refs[:n_src], refs[n_src:n_src + n]
        sems, token = refs[n_src + n:n_src + n + 2 * n_g], refs[-1]
        x, y, c, me = _my_position()
        for g, idxs in enumerate(groups):
            for k in PEER_ORDER:
                peer, pid = _peer(x, y, c, k)
                for j, i in enumerate(idxs):
                    p = pieces[i]
                    pltpu.make_async_remote_copy(
                        src_ref=p.send(src_refs[p.src], pid), dst_ref=p.slot(land_refs[i], me),
                        send_sem=sems[2 * g].at[(k - 1) * len(idxs) + j], recv_sem=sems[2 * g + 1].at[(k - 1) * len(idxs) + j],
                        device_id=peer, device_id_type=MESH).start()
        token[...] = jnp.zeros(token.shape, F32)

    sem_shapes = [pltpu.SemaphoreType.DMA(((N_DEV - 1) * len(idxs),)) for idxs in groups for _ in range(2)]
    res = pl.pallas_call(
        body, in_specs=[HBM_SPEC] * (n_src + n),
        out_specs=[SEM_SPEC] * (2 * n_g) + [HBM_SPEC] * (n_src + n) + [pl.BlockSpec(memory_space=pltpu.VMEM)],
        out_shape=sem_shapes + _hbm_like(srcs) + _hbm_like(lands) + [jax.ShapeDtypeStruct((8, LANES), F32)],
        input_output_aliases={i: 2 * n_g + i for i in range(n_src + n)},
        compiler_params=pltpu.CompilerParams(has_side_effects=pltpu.SideEffectType.DATAFLOW_SIDE_EFFECTING),
        name=name,
    )(*_in_hbm(srcs), *_in_hbm(lands))
    sems = [(res[2 * g], res[2 * g + 1]) for g in range(n_g)]
    return sems, list(res[2 * n_g:2 * n_g + n_src]), list(res[2 * n_g + n_src:2 * n_g + n_src + n]), res[-1]


def _exchange_wait(name, srcs, lands, pieces, waits, after):
    n_src, n, n_g = len(srcs), len(lands), len(waits)

    def body(*refs):
        src_refs, land_refs = refs[:n_src], refs[n_src:n_src + n]
        sems = refs[n_src + n:n_src + n + 2 * n_g]
        x, y, c, me = _my_position()
        at = 0
        for g, (idxs, _, _) in enumerate(waits):
            for k in PEER_ORDER:
                peer, pid = _peer(x, y, c, k)
                for j, i in enumerate(idxs):
                    p = pieces[i]
                    cp = pltpu.make_async_remote_copy(
                        src_ref=p.send(src_refs[p.src], pid), dst_ref=p.slot(land_refs[at + j], pid),
                        send_sem=sems[2 * g].at[(k - 1) * len(idxs) + j], recv_sem=sems[2 * g + 1].at[(k - 1) * len(idxs) + j],
                        device_id=peer, device_id_type=MESH)
                    cp.wait_send()
                    cp.wait_recv()
            at += len(idxs)

    sem_args = [s for _, send, recv in waits for s in (send, recv)]
    res = pl.pallas_call(
        body, in_specs=[HBM_SPEC] * (n_src + n) + [SEM_SPEC] * (2 * n_g) + [ANY_SPEC],
        out_specs=[HBM_SPEC] * (n_src + n), out_shape=_hbm_like(srcs) + _hbm_like(lands),
        input_output_aliases={i: i for i in range(n_src + n)},
        compiler_params=pltpu.CompilerParams(has_side_effects=pltpu.SideEffectType.DATAFLOW_SIDE_EFFECTING),
        name=name,
    )(*srcs, *lands, *sem_args, after)
    return list(res[:n_src]), list(res[n_src:])


def _weight_pieces(p_in, p_out, p_up, p_down):
    depth, cin, d = p_in.shape
    rout, hs = p_out.shape[1], p_up.shape[2]
    pieces = []
    for l in range(depth):
        whole = functools.partial(lambda ref, dev, l: ref.at[l], l=l)
        layer = functools.partial(lambda dev, l: (l, 0, 0), l=l)

        def rows(src, n_rows, whole=whole, layer=layer):
            return _Piece(src, whole, lambda ref, dev: _rows_of(n_rows)(ref.at[0], dev), (1, N_DEV * n_rows, d),
                          ((None, n_rows, d), layer), ((None, n_rows, d), lambda dev: (0, dev, 0)))

        pieces += [
            rows(0, cin), rows(1, rout),
            _Piece(2, whole, lambda ref, dev: ref.at[0, dev], (1, N_DEV, d, hs),
                   ((None, d, hs), layer), ((None, None, d, hs), lambda dev: (0, dev, 0, 0))),
            rows(3, hs),
        ]
    return pieces


def _grad_pieces(g_pair, kinds):
    pieces = []
    for i, (g, kind) in enumerate(zip(g_pair, kinds)):
        lead = lambda dev: (dev, 0, 0)
        if kind == "up":
            blk = ((None,) + g.shape[1:], lead)
            pieces.append(_Piece(i, lambda ref, dev: ref.at[dev], lambda ref, dev: ref.at[dev], g.shape, blk, blk))
        else:
            rows, cols = g.shape[0] // N_DEV, g.shape[1]
            pieces.append(_Piece(i, _rows_of(rows), lambda ref, dev: ref.at[dev], (N_DEV, rows, cols),
                                 ((rows, cols), lambda dev: (dev, 0)), ((None, rows, cols), lead)))
    return pieces


def _all_reduce_small(vec):
    rows = vec.shape[0]

    def body(v_ref, o_ref, buf_ref, send_sems, recv_sems):
        x, y, c, me = _my_position()
        buf_ref[me] = v_ref[...]
        sends = []
        for k in PEER_ORDER:
            peer, _ = _peer(x, y, c, k)
            cp = pltpu.make_async_remote_copy(src_ref=v_ref, dst_ref=buf_ref.at[me], send_sem=send_sems.at[k - 1],
                                              recv_sem=recv_sems.at[k - 1], device_id=peer, device_id_type=MESH)
            cp.start()
            sends.append(cp)
        for k in PEER_ORDER:
            peer, pid = _peer(x, y, c, k)
            pltpu.make_async_remote_copy(src_ref=v_ref, dst_ref=buf_ref.at[pid], send_sem=send_sems.at[k - 1],
                                         recv_sem=recv_sems.at[k - 1], device_id=peer, device_id_type=MESH).wait_recv()
        for cp in sends:
            cp.wait_send()
        total = buf_ref[0]
        for dev in range(1, N_DEV):
            total = total + buf_ref[dev]
        o_ref[...] = total

    vm = pl.BlockSpec(memory_space=pltpu.VMEM)
    return pl.pallas_call(
        body, in_specs=[vm], out_specs=vm, out_shape=jax.ShapeDtypeStruct(vec.shape, F32),
        scratch_shapes=[pltpu.VMEM((N_DEV, rows, LANES), F32), pltpu.SemaphoreType.DMA((N_DEV - 1,)),
                        pltpu.SemaphoreType.DMA((N_DEV - 1,))],
        name="all_reduce_small",
    )(vec)


def _resident(block_shape, index_map):
    return pl.BlockSpec(block_shape, index_map, pipeline_mode=pl.Buffered(1))


def _fwd_inproj(layer, x, gain, g_in, cos, sin, tm=256):
    s, d = x.shape

    def body(x_ref, gain_ref, w_ref, cos_ref, sin_ref, proj_ref, h_ref):
        h = _rms_fwd(x_ref[...], gain_ref[...]).astype(BF)
        h_ref[...] = h
        cs = _tile_lanes(cos_ref[...], SEG // LANES)
        sn = _tile_lanes(sin_ref[...], SEG // LANES)
        for seg in range(N_SEG):
            acc = _dot(h, w_ref[seg * SEG:(seg + 1) * SEG, :], NT)
            if seg < 2:
                acc = acc * cs + _rope_partner(acc) * sn
            if seg == 0:
                acc = acc * ATTN_SCALE
            proj_ref[:, seg * SEG:(seg + 1) * SEG] = acc

    return pl.pallas_call(
        body, grid=(s // tm,),
        in_specs=[pl.BlockSpec((tm, d), lambda i: (i, 0)), pl.BlockSpec((None, 1, d), lambda i: (layer, 0, 0)),
                  _resident((None, PROJ_W, d), lambda i: (0, 0, 0)),
                  pl.BlockSpec((tm, LANES), lambda i: (i, 0)), pl.BlockSpec((tm, LANES), lambda i: (i, 0))],
        out_specs=[pl.BlockSpec((tm, PROJ_W), lambda i: (i, 0)), pl.BlockSpec((tm, d), lambda i: (i, 0))],
        out_shape=[jax.ShapeDtypeStruct((s, PROJ_W), F32), jax.ShapeDtypeStruct((s, d), BF)],
        compiler_params=_cparams(("parallel",)), name=f"fwd_inproj_l{layer}",
    )(x, gain, g_in, cos, sin)


def _attn_masks(n):
    row = lax.broadcasted_iota(jnp.int32, (SPAN, SPAN), 0)
    col = lax.broadcasted_iota(jnp.int32, (SPAN, SPAN), 1)
    return col <= row, (col >= row) & (n > 0), col < 64


def _attn_specs(dil, n_in_extra):
    unit = SPAN * dil
    pairs = ATTN_W // LANES
    q_spec = pl.BlockSpec((unit, LANES), lambda p, n: (n, p))
    kp_spec = pl.BlockSpec((unit, LANES), lambda p, n: (jnp.maximum(n - 1, 0), pairs + p))
    kc_spec = pl.BlockSpec((unit, LANES), lambda p, n: (n, pairs + p))
    vp_spec = pl.BlockSpec((unit, LANES), lambda p, n: (jnp.maximum(n - 1, 0), 2 * pairs + p))
    vc_spec = pl.BlockSpec((unit, LANES), lambda p, n: (n, 2 * pairs + p))
    return [q_spec, kp_spec, kc_spec, vp_spec, vc_spec] + [q_spec] * n_in_extra


def _rows(dil, r):
    return pl.ds(r, SPAN, stride=dil) if dil > 1 else slice(None)


def _attn_fwd(layer, dil, proj):
    s = proj.shape[0]
    unit = SPAN * dil

    def body(q_ref, kp_ref, kc_ref, vp_ref, vc_ref, o_ref, lse_ref):
        m_cur, m_prev, is_a = _attn_masks(pl.program_id(1))
        for r in range(dil):
            rows = _rows(dil, r)
            q = q_ref[rows, :]
            kc, kp = kc_ref[rows, :].astype(BF), kp_ref[rows, :].astype(BF)
            vc, vp = vc_ref[rows, :].astype(BF), vp_ref[rows, :].astype(BF)
            halves = []
            for sel in (is_a, jnp.logical_not(is_a)):
                qh = jnp.where(sel, q, 0.0).astype(BF)
                s_c = jnp.where(m_cur, _dot(qh, kc, NT), MASK_VALUE)
                s_p = jnp.where(m_prev, _dot(qh, kp, NT), MASK_VALUE)
                mx = jnp.maximum(jnp.max(s_c, axis=-1, keepdims=True), jnp.max(s_p, axis=-1, keepdims=True))
                p_c, p_p = jnp.exp(s_c - mx), jnp.exp(s_p - mx)
                den = jnp.sum(p_c, axis=-1, keepdims=True) + jnp.sum(p_p, axis=-1, keepdims=True)
                o = (_dot(p_c.astype(BF), vc, NN) + _dot(p_p.astype(BF), vp, NN)) / den
                halves.append((o, mx + jnp.log(den)))
            o_ref[rows, :] = jnp.where(is_a, halves[0][0], halves[1][0])
            lse_ref[rows, :] = jnp.where(is_a, halves[0][1], halves[1][1])

    out_spec = pl.BlockSpec((unit, LANES), lambda p, n: (n, p))
    return pl.pallas_call(
        body, grid=(ATTN_W // LANES, s // unit), in_specs=_attn_specs(dil, 0), out_specs=[out_spec, out_spec],
        out_shape=[jax.ShapeDtypeStruct((s, ATTN_W), F32)] * 2,
        compiler_params=_cparams(("parallel", "arbitrary")), name=f"attn_fwd_d{dil}_l{layer}",
    )(proj, proj, proj, proj, proj)


def _attn_merge(layer, outs, lses, gain, mixed, tm=512):
    s = outs[0].shape[0]

    def body(o1, o2, o3, l1, l2, l3, gain_ref, mixed_ref, o_ref, lse_ref, n_ref):
        ls = (l1[...], l2[...], l3[...])
        mx = jnp.maximum(jnp.maximum(ls[0], ls[1]), ls[2])
        ws = [jnp.exp(l - mx) for l in ls]
        den = ws[0] + ws[1] + ws[2]
        o = (ws[0] * o1[...] + ws[1] * o2[...] + ws[2] * o3[...]) / den
        o_ref[...] = o
        lse_ref[...] = mx + jnp.log(den)
        n_ref[...] = _rms_fwd(o, gain_ref[...]).astype(BF)

    blk = pl.BlockSpec((tm, ATTN_W), lambda i: (i, 0))
    return pl.pallas_call(
        body, grid=(s // tm,),
        in_specs=[blk] * 6 + [pl.BlockSpec((None, 1, ATTN_W), lambda i: (layer, 0, 0)), pl.BlockSpec(memory_space=pl.ANY)],
        out_specs=[blk, blk, blk],
        out_shape=[jax.ShapeDtypeStruct((s, ATTN_W), F32), jax.ShapeDtypeStruct((s, ATTN_W), F32),
                   jax.ShapeDtypeStruct(mixed.shape, BF)],
        input_output_aliases={7: 2},
        compiler_params=_cparams(("parallel",)), name=f"attn_merge_l{layer}",
    )(*outs, *lses, gain, mixed)


def _chunk_cumsum(x, reverse=False):
    n = x.shape[0]
    pos = lax.broadcasted_iota(jnp.int32, x.shape, 0) % HGRN_CHUNK
    for sh in (1, 2, 4, 8):
        if reverse:
            x = x + jnp.where(pos < HGRN_CHUNK - sh, pltpu.roll(x, n - sh, 0), 0.0)
        else:
            x = x + jnp.where(pos >= sh, pltpu.roll(x, sh, 0), 0.0)
    return x


def _chunk_row(x, row):
    r, n = x.shape
    x3 = x.reshape(r // HGRN_CHUNK, HGRN_CHUNK, n)
    return jnp.broadcast_to(x3[:, row:row + 1, :], x3.shape).reshape(r, n)


def _hgrn_pre(qh, z, lb):
    sig = _sigmoid(z)
    f = lb + (1.0 - lb) * sig
    k = (1.0 - lb) * _sigmoid(-z)
    sq = _sigmoid(qh)
    q = qh * sq * HGRN_SCALE
    g = _chunk_cumsum(jnp.log(f))
    g_mid = _chunk_row(g, HGRN_CHUNK // 2 - 1)
    g_last = _chunk_row(g, HGRN_CHUNK - 1)
    e_q, e_k = jnp.exp(g - g_mid), jnp.exp(g_mid - g)
    e_in, e_out = jnp.exp(g), jnp.exp(g_last - g)
    return dict(sig=sig, f=f, k=k, sq=sq, q=q, g_last=g_last, e_q=e_q, e_k=e_k, e_in=e_in, e_out=e_out,
                qt=q * e_q, kt=k * e_k, qg=q * e_in, kout=k * e_out)


def _hgrn_mask():
    row = lax.broadcasted_iota(jnp.int32, (LANES, LANES), 0)
    col = lax.broadcasted_iota(jnp.int32, (LANES, LANES), 1)
    return (row // HGRN_CHUNK == col // HGRN_CHUNK) & (col <= row)


def _hgrn_in_specs(layer, rev, nblk):
    def blk(b):
        return nblk - 1 - b if rev else b
    base = 3 * (ATTN_W // LANES)
    per = HGRN_W // LANES
    specs = [pl.BlockSpec((LANES, LANES), functools.partial(lambda h, b, seg: (blk(b), base + seg * per + h), seg=seg))
             for seg in range(4)]
    specs.append(pl.BlockSpec((None, 1, LANES), lambda h, b: (layer, 0, h)))
    specs.append(pl.BlockSpec((None, 1, LANES), lambda h, b: (layer, 0, 0)))
    return specs, blk


def _hgrn_fwd(layer, proj, lb, gain):
    s = proj.shape[0]
    nblk = s // LANES
    cpb = LANES // HGRN_CHUNK

    def body(q_ref, f_ref, i_ref, g_ref, lb_ref, gain_ref, o_ref, rec_ref, st_ref, state):
        @pl.when(pl.program_id(1) == 0)
        def _():
            state[...] = jnp.zeros(state.shape, F32)

        pre = _hgrn_pre(q_ref[...], f_ref[...], lb_ref[...])
        v = i_ref[...].astype(BF)
        a = jnp.where(_hgrn_mask(), _dot(pre["qt"].astype(BF), pre["kt"].astype(BF), NT), 0.0)
        o = _dot(a.astype(BF), v, NN)
        qg, kout = pre["qg"].astype(BF), pre["kout"].astype(BF)
        dec = jnp.exp(pre["g_last"])
        st = state[...]
        inter = []
        for c in range(cpb):
            rows = slice(c * HGRN_CHUNK, (c + 1) * HGRN_CHUNK)
            st_ref[c * LANES:(c + 1) * LANES, :] = st
            inter.append(_dot(qg[rows], st.astype(BF), NT))
            st = st * dec[c * HGRN_CHUNK:c * HGRN_CHUNK + 1, :] + _dot(v[rows], kout[rows], TN)
        state[...] = st
        o = o + jnp.concatenate(inter, axis=0)
        o_ref[...] = o
        gate = g_ref[...]
        rec_ref[...] = (_rms_fwd(o, gain_ref[...]) * (gate * _sigmoid(gate))).astype(BF)

    specs, _ = _hgrn_in_specs(layer, False, nblk)
    blk = pl.BlockSpec((LANES, LANES), lambda h, b: (b, h))
    return pl.pallas_call(
        body, grid=(HGRN_HEADS, nblk), in_specs=specs,
        out_specs=[blk, pl.BlockSpec((LANES, LANES), lambda h, b: (b, ATTN_W // LANES + h)),
                   pl.BlockSpec((None, cpb * LANES, LANES), lambda h, b: (h, b, 0))],
        out_shape=[jax.ShapeDtypeStruct((s, HGRN_W), F32), jax.ShapeDtypeStruct((s, MIX_W), BF),
                   jax.ShapeDtypeStruct((HGRN_HEADS, nblk * cpb * LANES, LANES), F32)],
        scratch_shapes=[pltpu.VMEM((LANES, LANES), F32)],
        compiler_params=_cparams(("parallel", "arbitrary")), name=f"hgrn_fwd_l{layer}",
    )(proj, proj, proj, proj, lb, gain)


def _fwd_outproj(layer, mixed, g_out, x, tm=512):
    s, d = x.shape
    mw = mixed.shape[1]

    def body(m_ref, w_ref, x_ref, o_ref):
        o_ref[...] = x_ref[...] + _dot(m_ref[...], w_ref[...], NN)

    row = pl.BlockSpec((tm, d), lambda i: (i, 0))
    return pl.pallas_call(
        body, grid=(s // tm,),
        in_specs=[pl.BlockSpec((tm, mw), lambda i: (i, 0)), _resident((None, mw, d), lambda i: (0, 0, 0)), row],
        out_specs=row, out_shape=jax.ShapeDtypeStruct((s, d), F32),
        compiler_params=_cparams(("parallel",)), name=f"fwd_outproj_l{layer}",
    )(mixed, g_out, x)


def _relu2(u):
    return jnp.square(jnp.maximum(u, 0)).astype(BF)


def _mlp_fwd(layer, x, gain, g_up, g_down, tm=256):
    s, d = x.shape
    nblk, hs = g_up.shape[1], g_up.shape[3]

    def body(x_ref, gain_ref, up_ref, down_ref, o_ref, u_ref, h_ref):
        xv = x_ref[...]
        h = _rms_fwd(xv, gain_ref[...]).astype(BF)
        h_ref[...] = h
        acc = xv
        for j in range(nblk):
            u = _dot(h, up_ref[j], NN)
            u_ref[:, j * hs:(j + 1) * hs] = u.astype(BF)
            acc = acc + _dot(_relu2(u), down_ref[j * hs:(j + 1) * hs, :], NN)
        o_ref[...] = acc

    row = pl.BlockSpec((tm, d), lambda i: (i, 0))
    return pl.pallas_call(
        body, grid=(s // tm,),
        in_specs=[row, pl.BlockSpec((None, 1, d), lambda i: (layer, 0, 0)),
                  _resident((None, nblk, d, hs), lambda i: (0, 0, 0, 0)),
                  _resident((None, nblk * hs, d), lambda i: (0, 0, 0))],
        out_specs=[row, pl.BlockSpec((tm, nblk * hs), lambda i: (i, 0)), row],
        out_shape=[jax.ShapeDtypeStruct((s, d), F32), jax.ShapeDtypeStruct((s, nblk * hs), BF),
                   jax.ShapeDtypeStruct((s, d), BF)],
        compiler_params=_cparams(("parallel",)), name=f"mlp_fwd_l{layer}",
    )(x, gain, g_up, g_down)


def _loss_head(x, gain, target, tm=512):
    s, d = x.shape

    def body(x_ref, gain_ref, t_ref, dx_ref, dxb_ref, dgain_ref, loss_ref):
        i = pl.program_id(0)
        xv, gv = x_ref[...], gain_ref[...]
        err = _rms_fwd(xv, gv) - t_ref[...]
        dx, dgain = _rms_bwd(err * (1.0 / d), xv, gv)
        dx_ref[...] = dx
        dxb_ref[...] = dx.astype(BF)
        part = _part8(dgain)
        lpart = _part8(0.5 * jnp.mean(err * err, axis=-1, keepdims=True) * jnp.ones((1, LANES), F32))

        @pl.when(i == 0)
        def _():
            dgain_ref[...] = part
            loss_ref[...] = lpart

        @pl.when(i > 0)
        def _():
            dgain_ref[...] += part
            loss_ref[...] += lpart

    row = pl.BlockSpec((tm, d), lambda i: (i, 0))
    return pl.pallas_call(
        body, grid=(s // tm,),
        in_specs=[row, pl.BlockSpec((1, d), lambda i: (0, 0)), row],
        out_specs=[row, row, pl.BlockSpec((8, d), lambda i: (0, 0)), pl.BlockSpec((8, LANES), lambda i: (0, 0))],
        out_shape=[jax.ShapeDtypeStruct((s, d), F32), jax.ShapeDtypeStruct((s, d), BF), jax.ShapeDtypeStruct((8, d), F32),
                   jax.ShapeDtypeStruct((8, LANES), F32)],
        compiler_params=_cparams(("arbitrary",)), name="loss_head",
    )(x, gain, target)


def _accumulate_rows(i, ref, part):
    @pl.when(i == 0)
    def _():
        ref[...] = part

    @pl.when(i > 0)
    def _():
        ref[...] += part


def _mlp_bwd(layer, dx, dxb, x, gain, u, g_up, g_down, tm=256):
    s, d = x.shape
    nblk, hs = g_up.shape[1], g_up.shape[3]

    def body(dx_ref, dxb_ref, x_ref, gain_ref, u_ref, up_ref, down_ref, o_ref, ob_ref, du_ref, dgain_ref):
        dxb_v = dxb_ref[...]
        acc = jnp.zeros((tm, d), F32)
        for j in range(nblk):
            cols = slice(j * hs, (j + 1) * hs)
            da = _dot(dxb_v, down_ref[cols, :], NT)
            du = (da * (2.0 * jnp.maximum(u_ref[:, cols].astype(F32), 0.0))).astype(BF)
            du_ref[:, cols] = du
            acc = acc + _dot(du, up_ref[j], NT)
        dxn, dgain = _rms_bwd(acc, x_ref[...], gain_ref[...])
        out = dx_ref[...] + dxn
        o_ref[...] = out
        ob_ref[...] = out.astype(BF)
        _accumulate_rows(pl.program_id(0), dgain_ref, _part8(dgain))

    row = pl.BlockSpec((tm, d), lambda i: (i, 0))
    wide = pl.BlockSpec((tm, nblk * hs), lambda i: (i, 0))
    return pl.pallas_call(
        body, grid=(s // tm,),
        in_specs=[row, row, row, pl.BlockSpec((None, 1, d), lambda i: (layer, 0, 0)), wide,
                  _resident((None, nblk, d, hs), lambda i: (0, 0, 0, 0)),
                  _resident((None, nblk * hs, d), lambda i: (0, 0, 0))],
        out_specs=[row, row, wide, pl.BlockSpec((8, d), lambda i: (0, 0))],
        out_shape=[jax.ShapeDtypeStruct((s, d), F32), jax.ShapeDtypeStruct((s, d), BF),
                   jax.ShapeDtypeStruct((s, nblk * hs), BF), jax.ShapeDtypeStruct((8, d), F32)],
        compiler_params=_cparams(("arbitrary",)), name=f"mlp_bwd_l{layer}",
    )(dx, dxb, x, gain, u, g_up, g_down)


def _bwd_outproj(layer, dxb, g_out, tm=512):
    s, d = dxb.shape
    mw = g_out.shape[1]

    def body(dx_ref, w_ref, o_ref):
        o_ref[...] = _dot(dx_ref[...], w_ref[...], NT)

    return pl.pallas_call(
        body, grid=(s // tm,),
        in_specs=[pl.BlockSpec((tm, d), lambda i: (i, 0)), _resident((None, mw, d), lambda i: (0, 0, 0))],
        out_specs=pl.BlockSpec((tm, mw), lambda i: (i, 0)), out_shape=jax.ShapeDtypeStruct((s, mw), F32),
        compiler_params=_cparams(("parallel",)), name=f"bwd_outproj_l{layer}",
    )(dxb, g_out)


def _attn_norm_bwd(layer, dmixed, o, gain, tm=512):
    s = o.shape[0]

    def body(dm_ref, o_ref, gain_ref, do_ref, delta_ref, dgain_ref):
        i = pl.program_id(0)
        ov = o_ref[...]
        do, dgain = _rms_bwd(dm_ref[...], ov, gain_ref[...])
        do_ref[...] = do
        row = lax.broadcasted_iota(jnp.int32, (ATTN_W, ATTN_W), 0)
        col = lax.broadcasted_iota(jnp.int32, (ATTN_W, ATTN_W), 1)
        same_head = jnp.where(row // 64 == col // 64, 1.0, 0.0)
        delta_ref[...] = jnp.dot(do * ov, same_head, precision=lax.Precision.HIGHEST, preferred_element_type=F32)
        part = _part8(dgain)

        @pl.when(i == 0)
        def _():
            dgain_ref[...] = part

        @pl.when(i > 0)
        def _():
            dgain_ref[...] += part

    blk = pl.BlockSpec((tm, ATTN_W), lambda i: (i, 0))
    return pl.pallas_call(
        body, grid=(s // tm,), in_specs=[blk, blk, pl.BlockSpec((None, 1, ATTN_W), lambda i: (layer, 0, 0))],
        out_specs=[blk, blk, pl.BlockSpec((8, ATTN_W), lambda i: (0, 0))],
        out_shape=[jax.ShapeDtypeStruct((s, ATTN_W), F32), jax.ShapeDtypeStruct((s, ATTN_W), F32),
                   jax.ShapeDtypeStruct((8, ATTN_W), F32)],
        compiler_params=_cparams(("arbitrary",)), name=f"attn_norm_bwd_l{layer}",
    )(dmixed, o, gain)


def _attn_bwd(layer, dil, proj, do, lse, delta):
    s = proj.shape[0]
    unit = SPAN * dil

    def body(q_ref, kp_ref, kc_ref, vp_ref, vc_ref, do_ref, lse_ref, delta_ref, dq_ref, dkc_ref, dkp_ref, dvc_ref, dvp_ref):
        m_cur, m_prev, is_a = _attn_masks(pl.program_id(1))
        for r in range(dil):
            rows = _rows(dil, r)
            q, dov = q_ref[rows, :], do_ref[rows, :]
            kc, kp = kc_ref[rows, :], kp_ref[rows, :]
            vc, vp = vc_ref[rows, :].astype(BF), vp_ref[rows, :].astype(BF)
            lse_v, delta_v = lse_ref[rows, :], delta_ref[rows, :]
            dq = jnp.zeros((SPAN, LANES), F32)
            dkc, dkp, dvc, dvp = dq, dq, dq, dq
            for half, sel in enumerate((is_a, jnp.logical_not(is_a))):
                col = slice(64 * half, 64 * half + 1)
                lse_h, delta_h = lse_v[:, col], delta_v[:, col]
                qh = jnp.where(sel, q, 0.0).astype(BF)
                doh = jnp.where(sel, dov, 0.0).astype(BF)
                for k_f32, v_bf, mask, which in ((kc, vc, m_cur, 0), (kp, vp, m_prev, 1)):
                    kh = jnp.where(sel, k_f32, 0.0).astype(BF)
                    sc = jnp.where(mask, _dot(qh, kh, NT), MASK_VALUE)
                    p = jnp.exp(sc - lse_h)
                    dv = _dot(p.astype(BF), doh, TN)
                    ds = (p * (_dot(doh, v_bf, NT) - delta_h)).astype(BF)
                    dq = dq + _dot(ds, kh, NN)
                    dk = _dot(ds, qh, TN)
                    if which == 0:
                        dkc, dvc = dkc + dk, dvc + dv
                    else:
                        dkp, dvp = dkp + dk, dvp + dv
            dq_ref[rows, :] = dq
            dkc_ref[rows, :] = dkc
            dkp_ref[rows, :] = dkp
            dvc_ref[rows, :] = dvc
            dvp_ref[rows, :] = dvp

    out_spec = pl.BlockSpec((unit, LANES), lambda p, n: (n, p))
    return pl.pallas_call(
        body, grid=(ATTN_W // LANES, s // unit), in_specs=_attn_specs(dil, 3), out_specs=[out_spec] * 5,
        out_shape=[jax.ShapeDtypeStruct((s, ATTN_W), F32)] * 5,
        compiler_params=_cparams(("parallel", "arbitrary")), name=f"attn_bwd_d{dil}_l{layer}",
    )(proj, proj, proj, proj, proj, do, lse, delta)


def _attn_combine(layer, parts, cos, sin):
    s = parts[0][0].shape[0]
    nblk = s // SPAN
    arrays, specs = [], []
    for dil, (dq, dkc, dkp, dvc, dvp) in zip(DILATIONS, parts):
        here = pl.BlockSpec((SPAN, ATTN_W), lambda i: (i, 0))
        ahead = pl.BlockSpec((SPAN, ATTN_W), functools.partial(lambda i, dil: (jnp.minimum(i + dil, nblk - 1), 0), dil=dil))
        arrays += [dq, dkc, dkp, dvc, dvp]
        specs += [here, here, ahead, here, ahead]
    tab = pl.BlockSpec((SPAN, LANES), lambda i: (i, 0))

    def body(*refs):
        cos_ref, sin_ref, out_ref = refs[15], refs[16], refs[17]
        i = pl.program_id(0)
        dq = dk = dv = jnp.zeros((SPAN, ATTN_W), F32)
        for p, dil in enumerate(DILATIONS):
            dq_r, dkc_r, dkp_r, dvc_r, dvp_r = refs[5 * p:5 * p + 5]
            has_next = i + dil < nblk
            dq = dq + dq_r[...]
            dk = dk + dkc_r[...] + jnp.where(has_next, dkp_r[...], 0.0)
            dv = dv + dvc_r[...] + jnp.where(has_next, dvp_r[...], 0.0)
        cs = _tile_lanes(cos_ref[...], ATTN_W // LANES)
        sn = _tile_lanes(sin_ref[...], ATTN_W // LANES)
        out_ref[0] = ((dq * cs - _rope_partner(dq) * sn) * ATTN_SCALE).astype(BF)
        out_ref[1] = (dk * cs - _rope_partner(dk) * sn).astype(BF)
        out_ref[2] = dv.astype(BF)

    return pl.pallas_call(
        body, grid=(nblk,), in_specs=specs + [tab, tab],
        out_specs=pl.BlockSpec((3, SPAN, ATTN_W), lambda i: (0, i, 0)),
        out_shape=jax.ShapeDtypeStruct((3, s, ATTN_W), BF),
        compiler_params=_cparams(("parallel",)), name=f"attn_combine_l{layer}",
    )(*arrays, cos, sin)


def _hgrn_bwd(layer, proj, lb, gain, o, dmixed, states):
    s = proj.shape[0]
    nblk = s // LANES
    cpb = LANES // HGRN_CHUNK

    def body(q_ref, f_ref, i_ref, g_ref, lb_ref, gain_ref, o_ref, drec_ref, st_ref, dseg_ref, dlb_ref, dgain_ref, dstate):
        head, step = pl.program_id(0), pl.program_id(1)

        @pl.when(step == 0)
        def _():
            dstate[...] = jnp.zeros(dstate.shape, F32)

        lbv, gv = lb_ref[...], gain_ref[...]
        qh, z, gate_in = q_ref[...], f_ref[...], g_ref[...]
        pre = _hgrn_pre(qh, z, lbv)
        v = i_ref[...].astype(BF)
        sg = _sigmoid(gate_in)
        gate = gate_in * sg
        ov, drec = o_ref[...], drec_ref[...]
        do, dgain = _rms_bwd(drec * gate, ov, gv)
        dgate_in = drec * _rms_fwd(ov, gv) * (sg * (1.0 + gate_in * (1.0 - sg)))
        mask = _hgrn_mask()
        qt, kt = pre["qt"].astype(BF), pre["kt"].astype(BF)
        do_b = do.astype(BF)
        a = jnp.where(mask, _dot(qt, kt, NT), 0.0).astype(BF)
        da = jnp.where(mask, _dot(do_b, v, NT), 0.0).astype(BF)
        dv = _dot(a, do_b, TN)
        dqt = _dot(da, kt, NN)
        dkt = _dot(da, qt, TN)
        qg, kout = pre["qg"].astype(BF), pre["kout"].astype(BF)
        dec = jnp.exp(pre["g_last"])
        dst = dstate[...]
        dqg_c, dkout_c, dv_c, dgl_c = [None] * cpb, [None] * cpb, [None] * cpb, [None] * cpb
        for c in reversed(range(cpb)):
            rows = slice(c * HGRN_CHUNK, (c + 1) * HGRN_CHUNK)
            st_prev = st_ref[c * LANES:(c + 1) * LANES, :]
            dec_c = dec[c * HGRN_CHUNK:c * HGRN_CHUNK + 1, :]
            dst_b = dst.astype(BF)
            dqg_c[c] = _dot(do_b[rows], st_prev.astype(BF), NN)
            dkout_c[c] = _dot(v[rows], dst_b, NN)
            dv_c[c] = _dot(kout[rows], dst_b, NT)
            dgl_c[c] = (jnp.sum(dst * st_prev, axis=0, keepdims=True) * dec_c
                        + jnp.sum(dkout_c[c] * pre["kout"][rows], axis=0, keepdims=True))
            dst = dst * dec_c + _dot(do_b[rows], qg[rows], TN)
        dstate[...] = dst
        dqg, dkout = jnp.concatenate(dqg_c, axis=0), jnp.concatenate(dkout_c, axis=0)
        dv = dv + jnp.concatenate(dv_c, axis=0)
        dgl = jnp.concatenate([jnp.broadcast_to(t, (HGRN_CHUNK, LANES)) for t in dgl_c], axis=0)
        pos = lax.broadcasted_iota(jnp.int32, (LANES, LANES), 0) % HGRN_CHUNK
        dq = dqt * pre["e_q"] + dqg * pre["e_in"]
        dk = dkt * pre["e_k"] + dkout * pre["e_out"]
        dg = (dqt * pre["qt"] - dkt * pre["kt"] + dqg * pre["qg"] - dkout * pre["kout"]
              + jnp.where(pos == HGRN_CHUNK - 1, dgl, 0.0))
        dlogf = _chunk_cumsum(dg, reverse=True)
        sig, sq = pre["sig"], pre["sq"]
        df = dlogf / pre["f"] - dk
        dseg_ref[0] = (dq * HGRN_SCALE * (sq * (1.0 + qh * (1.0 - sq)))).astype(BF)
        dseg_ref[1] = (df * (1.0 - lbv) * sig * (1.0 - sig)).astype(BF)
        dseg_ref[2] = dv.astype(BF)
        dseg_ref[3] = dgate_in.astype(BF)
        dlb_part = _part8(df * (1.0 - sig))
        dgain_part = _part8(dgain)

        @pl.when(step == 0)
        def _():
            dlb_ref[...] = dlb_part

        @pl.when(step > 0)
        def _():
            dlb_ref[...] += dlb_part

        @pl.when((step == 0) & (head == 0))
        def _():
            dgain_ref[...] = dgain_part

        @pl.when((step > 0) | (head > 0))
        def _():
            dgain_ref[...] += dgain_part

    specs, blk = _hgrn_in_specs(layer, True, nblk)
    per = HGRN_W // LANES
    specs += [pl.BlockSpec((LANES, LANES), lambda h, b: (blk(b), h)),
              pl.BlockSpec((LANES, LANES), lambda h, b: (blk(b), per + h)),
              pl.BlockSpec((None, cpb * LANES, LANES), lambda h, b: (h, blk(b), 0))]
    return pl.pallas_call(
        body, grid=(HGRN_HEADS, nblk), in_specs=specs,
        out_specs=[pl.BlockSpec((4, LANES, LANES), lambda h, b: (0, blk(b), h)),
                   pl.BlockSpec((8, LANES), lambda h, b: (0, h)), pl.BlockSpec((8, LANES), lambda h, b: (0, 0))],
        out_shape=[jax.ShapeDtypeStruct((4, s, HGRN_W), BF), jax.ShapeDtypeStruct((8, HGRN_W), F32),
                   jax.ShapeDtypeStruct((8, LANES), F32)],
        scratch_shapes=[pltpu.VMEM((LANES, LANES), F32)],
        compiler_params=_cparams(("arbitrary", "arbitrary")), name=f"hgrn_bwd_l{layer}",
    )(proj, proj, proj, proj, lb, gain, o, dmixed, states)


def _bwd_inproj(layer, dqkv, dhg, g_in, x, gain, dres, tm=256):
    s, d = x.shape

    def body(dqkv_ref, dhg_ref, w_ref, x_ref, gain_ref, dres_ref, dx_ref, dxb_ref, dgain_ref):
        acc = jnp.zeros((tm, d), F32)
        for seg in range(N_SEG):
            a = dqkv_ref[seg] if seg < 3 else dhg_ref[seg - 3]
            acc = acc + _dot(a, w_ref[seg * SEG:(seg + 1) * SEG, :], NN)
        dx, dgain = _rms_bwd(acc, x_ref[...], gain_ref[...])
        out = dres_ref[...] + dx
        dx_ref[...] = out
        dxb_ref[...] = out.astype(BF)
        _accumulate_rows(pl.program_id(0), dgain_ref, _part8(dgain))

    row = pl.BlockSpec((tm, d), lambda i: (i, 0))
    return pl.pallas_call(
        body, grid=(s // tm,),
        in_specs=[pl.BlockSpec((3, tm, SEG), lambda i: (0, i, 0)), pl.BlockSpec((4, tm, SEG), lambda i: (0, i, 0)),
                  _resident((None, PROJ_W, d), lambda i: (0, 0, 0)), row,
                  pl.BlockSpec((None, 1, d), lambda i: (layer, 0, 0)), row],
        out_specs=[row, row, pl.BlockSpec((8, d), lambda i: (0, 0))],
        out_shape=[jax.ShapeDtypeStruct((s, d), F32), jax.ShapeDtypeStruct((s, d), BF), jax.ShapeDtypeStruct((8, d), F32)],
        compiler_params=_cparams(("arbitrary",)), name=f"bwd_inproj_l{layer}",
    )(dqkv, dhg, g_in, x, gain, dres)


def _adamw(w, g, m, v):
    m2 = ADAM_B1 * m + (1.0 - ADAM_B1) * g
    v2 = ADAM_B2 * v + (1.0 - ADAM_B2) * (g * g)
    m_hat = m2 / (1.0 - ADAM_B1 ** ADAM_STEP)
    v_hat = v2 / (1.0 - ADAM_B2 ** ADAM_STEP)
    delta = -ADAM_LR * (m_hat / (jnp.sqrt(v_hat) + ADAM_EPS) + ADAM_WD * w)
    return delta, m2, v2


def _adam_big(name, parts, w, m, v, transpose, row_tiles):
    depth = w.shape[0]
    r, c = parts[0].shape[1], parts[0].shape[2]
    if transpose:
        tc = c // row_tiles
        p_spec = pl.BlockSpec((N_DEV, r, tc), lambda t: (0, 0, t))
        w_spec = pl.BlockSpec((depth, tc, r), lambda t: (0, t, 0))
    else:
        tr = r // row_tiles
        p_spec = pl.BlockSpec((N_DEV, tr, c), lambda t: (0, t, 0))
        w_spec = pl.BlockSpec((depth, tr, c), lambda t: (0, t, 0))

    def body(*refs):
        p_refs = refs[:depth]
        w_ref, m_ref, v_ref, g_ref, d_ref, m2_ref, v2_ref = refs[depth:]
        for l in range(depth):
            g = p_refs[l][0].astype(F32)
            for dev in range(1, N_DEV):
                g = g + p_refs[l][dev].astype(F32)
            if transpose:
                g = g.T
            delta, m2, v2 = _adamw(w_ref[l], g, m_ref[l], v_ref[l])
            g_ref[l] = g
            d_ref[l] = delta
            m2_ref[l] = m2
            v2_ref[l] = v2

    return pl.pallas_call(
        body, grid=(row_tiles,), in_specs=[p_spec] * depth + [w_spec] * 3, out_specs=[w_spec] * 4,
        out_shape=[jax.ShapeDtypeStruct(w.shape, F32)] * 4,
        compiler_params=_cparams(("parallel",)), name=name,
    )(*parts, w, m, v)


def _adam_small(g, w, m, v):
    def body(g_ref, w_ref, m_ref, v_ref, d_ref, m2_ref, v2_ref):
        delta, m2, v2 = _adamw(w_ref[...], g_ref[...], m_ref[...], v_ref[...])
        d_ref[...] = delta
        m2_ref[...] = m2
        v2_ref[...] = v2

    vm = pl.BlockSpec(memory_space=pltpu.VMEM)
    return pl.pallas_call(body, in_specs=[vm] * 4, out_specs=[vm] * 3, out_shape=[jax.ShapeDtypeStruct(g.shape, F32)] * 3,
                          name="adam_small")(g, w, m, v)


def _lower_bounds(logits):
    def body(l_ref, lb_ref, jac_ref):
        l0, l1 = l_ref[0:1, :], l_ref[1:2, :]
        mx = jnp.maximum(l0, l1)
        e0, e1 = jnp.exp(l0 - mx), jnp.exp(l1 - mx)
        p0, p1 = e0 / (e0 + e1), e1 / (e0 + e1)
        lb_ref[0:1, :] = p0 - p0
        lb_ref[1:2, :] = (p0 + p1) - p0
        jac_ref[0:1, :] = -p0 * p1
        jac_ref[1:2, :] = p0 * p1

    vm = pl.BlockSpec(memory_space=pltpu.VMEM)
    return pl.pallas_call(body, in_specs=[vm], out_specs=[vm, vm], out_shape=[jax.ShapeDtypeStruct(logits.shape, F32)] * 2,
                          name="hgrn_lower_bounds")(logits)


def _rope_tables(s):
    half = 32
    inv_freq = ROPE_THETA ** (-jnp.arange(half, dtype=F32) / half)
    ang = jnp.arange(s, dtype=jnp.int32).astype(F32)[:, None] * inv_freq[None, :]
    cos, sin = jnp.cos(ang), jnp.sin(ang)
    return jnp.concatenate([cos] * 4, axis=1), jnp.concatenate([-sin, sin, -sin, sin], axis=1)


SMALL_NAMES = ("norm_mix", "attn_out_gain", "hgrn_lb_logits", "hgrn_out_gain", "norm_mlp", "norm_final")


def _pack_small(vals):
    flat = jnp.concatenate([v.reshape(-1) for v in vals])
    rows = -(-flat.shape[0] // (8 * LANES)) * 8
    return jnp.pad(flat, (0, rows * LANES - flat.shape[0])).reshape(rows, LANES)


def _unpack_small(packed, like):
    flat, out, off = packed.reshape(-1), [], 0
    for v in like:
        out.append(flat[off:off + v.size].reshape(v.shape))
        off += v.size
    return out


def kernel(x, norm_mix, w_in, attn_out_gain, hgrn_lb_logits, hgrn_out_gain, w_out, norm_mlp, w_up, w_down, norm_final, loss_target, m_norm_mix, m_w_in, m_attn_out_gain, m_hgrn_lb_logits, m_hgrn_out_gain, m_w_out, m_norm_mlp, m_w_up, m_w_down, m_norm_final, v_norm_mix, v_w_in, v_attn_out_gain, v_hgrn_lb_logits, v_hgrn_out_gain, v_w_out, v_norm_mlp, v_w_up, v_w_down, v_norm_final):
    depth = w_in.shape[0]
    assert depth == 2 and x.shape[0] == 1
    s, d = x.shape[1], x.shape[2]
    x0 = x[0]
    target = loss_target[0]
    cos, sin = _rope_tables(s)
    g_mix, g_attn, g_hg, g_mlp = (norm_mix[:, None, :], attn_out_gain[:, None, :], hgrn_out_gain[:, None, :],
                                  norm_mlp[:, None, :])
    lb, lb_jac = _lower_bounds(hgrn_lb_logits)
    lb3 = lb[:, None, :]

    shards = list(_pack_weights(w_in, w_out, w_up, w_down))
    w_pieces = _weight_pieces(*shards)
    w_groups = [[0], [1, 2, 3], [4, 5, 6, 7]]
    me = (4 * lax.axis_index("x") + 2 * lax.axis_index("y") + lax.axis_index("c")).astype(jnp.int32).reshape(1)
    lands = _exchange_own("all_gather_own", me, shards, w_pieces)
    w_sems, shards, lands, token = _exchange_start("all_gather_start", shards, lands, w_pieces, w_groups)

    def weights_ready(group, after):
        nonlocal shards
        idxs = w_groups[group]
        shards, got = _exchange_wait(f"all_gather_wait{group}", shards, [lands[i] for i in idxs], w_pieces,
                                     [(idxs, *w_sems[group])], after)
        return got

    def tied(small_arr, tok):
        return small_arr + tok[0, 0]

    saved = []
    xl = x0
    full = [None] * depth
    for l in range(depth):
        if l == 0:
            (full_in,) = weights_ready(0, token)
        else:
            full_in, full_out, full_up, full_down = weights_ready(2, xl)
        proj, h = _fwd_inproj(l, xl, g_mix, full_in, cos, sin)
        fw = [_attn_fwd(l, dil, proj) for dil in DILATIONS]
        o_hg, mixed, states = _hgrn_fwd(l, proj, lb3, g_hg)
        o_attn, lse, mixed = _attn_merge(l, [f[0] for f in fw], [f[1] for f in fw], g_attn, mixed)
        if l == 0:
            full_out, full_up, full_down = weights_ready(1, mixed)
        x_mid = _fwd_outproj(l, mixed, full_out, xl)
        x_next, u, h2 = _mlp_fwd(l, x_mid, g_mlp, full_up, full_down)
        saved.append((xl, proj, h, o_attn, lse, o_hg, states, mixed, x_mid, u, h2))
        full[l] = (full_in, full_out, full_up, full_down)
        xl = x_next
    dx, dxb, dnorm_final8, loss8 = _loss_head(xl, norm_final[None, :], target)
    loss = lax.psum(jnp.sum(loss8[:, 0]), ("x", "y", "c"))

    exchanges = []

    def scatter(tag, grads, kinds):
        pieces = _grad_pieces(grads, kinds)
        own = _exchange_own(f"reduce_scatter_own_{tag}", me, grads, pieces)
        sems, grads, own, tok = _exchange_start(f"reduce_scatter_start_{tag}", grads, own, pieces, [list(range(len(pieces)))])
        exchanges.append((grads, own, pieces, sems[0]))
        return tok

    small = {}
    for l in reversed(range(depth)):
        xl, proj, h, o_attn, lse, o_hg, states, mixed, x_mid, u, h2 = saved[l]
        full_in, full_out, full_up, full_down = full[l]
        hs = full_up.shape[3]
        gw_down = _mm_tn(f"grad_w_down_l{l}", u, dxb, u.shape[1], a_fn=_relu2)
        dx_mid, dx_mid_b, du, dmlp8 = _mlp_bwd(l, dx, dxb, x_mid, g_mlp, u, full_up, full_down)
        gw_up = _mm_tn(f"grad_w_up_l{l}", h2, du, d, out_block_w=hs)
        g_attn_t = tied(g_attn, scatter(f"mlp_l{l}", [gw_down, gw_up], ["rows", "up"]))
        dmixed = _bwd_outproj(l, dx_mid_b, full_out)
        gw_out = _mm_tn(f"grad_w_out_l{l}", mixed, dx_mid_b, mixed.shape[1])
        do, delta, dattn8 = _attn_norm_bwd(l, dmixed, o_attn, g_attn_t)
        parts = [_attn_bwd(l, dil, proj, do, lse, delta) for dil in DILATIONS]
        dqkv = _attn_combine(l, parts, cos, sin)
        dhg, dlb8, dhgain8 = _hgrn_bwd(l, proj, lb3, g_hg, o_hg, dmixed, states)
        gin = _mm_tn(f"grad_w_in_qkv_l{l}", dqkv, h, PROJ_W, a_lead=True)
        gw_in = _mm_tn(f"grad_w_in_hg_l{l}", dhg, h, PROJ_W, a_lead=True, out_block_off=3, prev=gin)
        g_mix_t = tied(g_mix, scatter(f"mix_l{l}", [gw_out, gw_in], ["rows", "rows"]))
        dx, dxb, dmix8 = _bwd_inproj(l, dqkv, dhg, full_in, xl, g_mix_t, dx_mid)
        small[l] = (dmix8, dattn8, dlb8, dhgain8, dmlp8)

    def fin(p8):
        return jnp.sum(p8, axis=0)
    dlogits = lb_jac * fin(small[1][2])[None, :]
    small_grads = [jnp.stack([fin(small[l][0]) for l in range(depth)]), jnp.stack([fin(small[l][1]) for l in range(depth)]),
                   dlogits, jnp.stack([fin(small[l][3]) for l in range(depth)]),
                   jnp.stack([fin(small[l][4]) for l in range(depth)]), fin(dnorm_final8)]
    small_w = [norm_mix, attn_out_gain, hgrn_lb_logits, hgrn_out_gain, norm_mlp, norm_final]
    small_m = [m_norm_mix, m_attn_out_gain, m_hgrn_lb_logits, m_hgrn_out_gain, m_norm_mlp, m_norm_final]
    small_v = [v_norm_mix, v_attn_out_gain, v_hgrn_lb_logits, v_hgrn_out_gain, v_norm_mlp, v_norm_final]
    g_small = _all_reduce_small(_pack_small(small_grads))
    d_small, m_small, v_small = _adam_small(g_small, _pack_small(small_w), _pack_small(small_m), _pack_small(small_v))
    gs, ds, ms, vs = (_unpack_small(t, small_w) for t in (g_small, d_small, m_small, v_small))

    all_grads, all_lands, all_pieces, waits = [], [], [], []
    for grads, own, pieces, (send, recv) in exchanges:
        first = len(all_pieces)
        all_pieces += [p._replace(src=p.src + len(all_grads)) for p in pieces]
        waits.append((list(range(first, first + len(pieces))), send, recv))
        all_grads += grads
        all_lands += own
    _, landed = _exchange_wait("reduce_scatter_wait", all_grads, all_lands, all_pieces, waits, d_small)
    r_down, r_up, r_out, r_in = ([landed[4 + i], landed[i]] for i in range(4))
    big = {
        "w_in": _adam_big("adam_w_in", r_in, w_in, m_w_in, v_w_in, True, 4),
        "w_out": _adam_big("adam_w_out", r_out, w_out, m_w_out, v_w_out, False, 1),
        "w_up": _adam_big("adam_w_up", r_up, w_up, m_w_up, v_w_up, False, 2),
        "w_down": _adam_big("adam_w_down", r_down, w_down, m_w_down, v_w_down, False, 4),
    }

    def gather(idx, small_list):
        by_name = dict(zip(SMALL_NAMES, small_list))
        return [by_name["norm_mix"], big["w_in"][idx], by_name["attn_out_gain"], by_name["hgrn_lb_logits"],
                by_name["hgrn_out_gain"], big["w_out"][idx], by_name["norm_mlp"], big["w_up"][idx], big["w_down"][idx],
                by_name["norm_final"]]

    return (loss, dx[None], *gather(0, gs), *gather(1, ds), *gather(2, ms), *gather(3, vs))
```

```python
import functools
from typing import Callable, NamedTuple

import jax
import jax.numpy as jnp
from jax import lax
from jax.experimental import pallas as pl
from jax.experimental.pallas import tpu as pltpu

F32 = jnp.float32
BF = jnp.bfloat16

N_DEV = 8
ATTN_W = 512
HGRN_W = 512
HGRN_HEADS = 4
HGRN_DIM = 128
SEG = 512
N_SEG = 7
PROJ_W = N_SEG * SEG
MIX_W = ATTN_W + HGRN_W
SPAN = 128
DILATIONS = (1, 4, 16)
HGRN_CHUNK = 16
ROPE_THETA = 10000.0
NORM_EPS = 1e-6
MASK_VALUE = -1e30
ATTN_SCALE = 0.125
HGRN_SCALE = HGRN_DIM ** -0.5
ADAM_LR = 0.001
ADAM_B1 = 0.9
ADAM_B2 = 0.999
ADAM_EPS = 1e-08
ADAM_WD = 0.01
ADAM_STEP = 10
LANES = 128
VMEM_LIMIT = 56 * 1024 * 1024

NN = ((1,), (0,))
NT = ((1,), (1,))
TN = ((0,), (0,))
MESH = pl.DeviceIdType.MESH


def _dot(a, b, dims):
    return lax.dot_general(a, b, (dims, ((), ())), preferred_element_type=F32)


def _cparams(sem):
    return pltpu.CompilerParams(dimension_semantics=sem, vmem_limit_bytes=VMEM_LIMIT)


def _part8(x):
    r, n = x.shape
    return jnp.sum(x.reshape(r // 8, 8, n), axis=0)


def _sigmoid(x):
    return 1.0 / (1.0 + jnp.exp(-x))


def _rms_fwd(x, gain):
    r = lax.rsqrt(jnp.mean(x * x, axis=-1, keepdims=True) + NORM_EPS)
    return x * r * gain


def _rms_bwd(dy, x, gain):
    r = lax.rsqrt(jnp.mean(x * x, axis=-1, keepdims=True) + NORM_EPS)
    xn = x * r
    dxn = dy * gain
    dx = r * (dxn - xn * jnp.mean(dxn * xn, axis=-1, keepdims=True))
    return dx, dy * xn


def _rope_partner(x):
    n = x.shape[-1]
    lane = lax.broadcasted_iota(jnp.int32, x.shape, x.ndim - 1)
    return jnp.where((lane % 64) < 32, pltpu.roll(x, n - 32, x.ndim - 1), pltpu.roll(x, 32, x.ndim - 1))


def _tile_lanes(t, reps):
    return jnp.concatenate([t] * reps, axis=-1)


def _mm_tn(name, a, b, out_rows, a_lead=False, out_block_off=0, prev=None, out_block_w=None, a_fn=None,
           tm=512, tn=1024, tk=1024):
    kdim, n = b.shape
    m = a.shape[-1]
    tm, tn, tk = min(tm, m), min(tn, n), min(tk, kdim)
    mt, nk = m // tm, kdim // tk
    n_lead = a.shape[0] if a_lead else 1
    if a_lead:
        a_spec = pl.BlockSpec((None, tk, tm), lambda i, j, k: (i // mt, k, i % mt))
    else:
        a_spec = pl.BlockSpec((tk, tm), lambda i, j, k: (k, i))
    b_spec = pl.BlockSpec((tk, tn), lambda i, j, k: (k, j))
    if out_block_w:
        nb = tn // out_block_w
        o_shape = jax.ShapeDtypeStruct((n // out_block_w, out_rows, out_block_w), BF)
        o_spec = pl.BlockSpec((nb, tm, out_block_w), lambda i, j, k: (j, i + out_block_off, 0))
    else:
        nb = 0
        o_shape = jax.ShapeDtypeStruct((out_rows, n), BF)
        o_spec = pl.BlockSpec((tm, tn), lambda i, j, k: (i + out_block_off, j))
    arrays, specs, aliases = [a, b], [a_spec, b_spec], {}
    if prev is not None:
        arrays.append(prev)
        specs.append(pl.BlockSpec(memory_space=pl.ANY))
        aliases = {2: 0}

    def body(*refs):
        a_ref, b_ref, o_ref, acc = refs[0], refs[1], refs[-2], refs[-1]
        k = pl.program_id(2)

        @pl.when(k == 0)
        def _():
            acc[...] = jnp.zeros(acc.shape, F32)

        av = a_ref[...]
        if a_fn is not None:
            av = a_fn(av)
        acc[...] += _dot(av, b_ref[...], TN)

        @pl.when(k == nk - 1)
        def _():
            if nb:
                for t in range(nb):
                    o_ref[t] = acc[:, t * out_block_w:(t + 1) * out_block_w].astype(BF)
            else:
                o_ref[...] = acc[...].astype(BF)

    return pl.pallas_call(
        body, grid=(n_lead * mt, n // tn, nk), in_specs=specs, out_specs=o_spec, out_shape=o_shape,
        scratch_shapes=[pltpu.VMEM((tm, tn), F32)], compiler_params=_cparams(("parallel", "parallel", "arbitrary")),
        name=name, input_output_aliases=aliases,
    )(*arrays)


def _pack_weights(w_in, w_out, w_up, w_down):
    depth, d, cin = w_in.shape

    def body(win_ref, wout_ref, wup_ref, wdown_ref, oin_ref, oout_ref, oup_ref, odown_ref):
        oin_ref[...] = win_ref[...].T.astype(BF)
        oout_ref[...] = wout_ref[...].astype(BF)
        oup_ref[...] = wup_ref[...].astype(BF)
        odown_ref[...] = wdown_ref[...].astype(BF)

    def spec(a):
        return pl.BlockSpec((None,) + a.shape[1:], lambda l: (l, 0, 0))

    outs = [jax.ShapeDtypeStruct((depth, cin, d), BF), jax.ShapeDtypeStruct(w_out.shape, BF),
            jax.ShapeDtypeStruct(w_up.shape, BF), jax.ShapeDtypeStruct(w_down.shape, BF)]
    return pl.pallas_call(
        body, grid=(depth,), in_specs=[spec(w_in), spec(w_out), spec(w_up), spec(w_down)],
        out_specs=[pl.BlockSpec((None, cin, d), lambda l: (l, 0, 0)), spec(w_out), spec(w_up), spec(w_down)],
        out_shape=outs, compiler_params=_cparams(("arbitrary",)), name="pack_weights",
    )(w_in, w_out, w_up, w_down)


def _my_position():
    x, y, c = lax.axis_index("x"), lax.axis_index("y"), lax.axis_index("c")
    return x, y, c, 4 * x + 2 * y + c


def _peer(x, y, c, k):
    px = 1 - x if k & 4 else x
    py = 1 - y if k & 2 else y
    pc = 1 - c if k & 1 else c
    return (px, py, pc), 4 * px + 2 * py + pc


PEER_ORDER = (1, 2, 4, 3, 5, 6, 7)


class _Piece(NamedTuple):
    src: int
    send: Callable
    slot: Callable
    land_shape: tuple
    own_src: tuple
    own_slot: tuple


HBM_SPEC = pl.BlockSpec(memory_space=pltpu.HBM)
SEM_SPEC = pl.BlockSpec(memory_space=pltpu.SEMAPHORE)
ANY_SPEC = pl.BlockSpec(memory_space=pl.ANY)


def _in_hbm(arrays):
    return [pltpu.with_memory_space_constraint(a, pltpu.HBM) for a in arrays]


def _hbm_like(arrays):
    return [pltpu.HBM(a.shape, a.dtype) for a in arrays]


def _rows_of(rows):
    return lambda ref, dev: ref.at[pl.ds(pl.multiple_of(dev * rows, 16), rows), :]


def _exchange_own(name, me, srcs, pieces):
    n = len(pieces)

    def body(me_ref, *refs):
        for i in range(n):
            refs[n + i][...] = refs[i][...]

    def spec(block_and_index):
        block, index = block_and_index
        return pl.BlockSpec(block, lambda i, me_ref: index(me_ref[0]))

    return pl.pallas_call(
        body,
        grid_spec=pltpu.PrefetchScalarGridSpec(
            num_scalar_prefetch=1, grid=(1,), in_specs=[spec(p.own_src) for p in pieces],
            out_specs=[spec(p.own_slot) for p in pieces]),
        out_shape=[jax.ShapeDtypeStruct(p.land_shape, BF) for p in pieces],
        compiler_params=_cparams(("arbitrary",)), name=name,
    )(me, *[srcs[p.src] for p in pieces])


def _exchange_start(name, srcs, lands, pieces, groups):
    n_src, n, n_g = len(srcs), len(pieces), len(groups)

    def body(*refs):
        src_refs, land_refs = refs[:n_src], refs[n_src:n_src + n]
        sems, token = refs[n_src + n:n_src + n + 2 * n_g], refs[-1]
        x, y, c, me = _my_position()
        for g, idxs in enumerate(groups):
            for k in PEER_ORDER:
                peer, pid = _peer(x, y, c, k)
                for j, i in enumerate(idxs):
                    p = pieces[i]
                    pltpu.make_async_remote_copy(
                        src_ref=p.send(src_refs[p.src], pid), dst_ref=p.slot(land_refs[i], me),
                        send_sem=sems[2 * g].at[(k - 1) * len(idxs) + j], recv_sem=sems[2 * g + 1].at[(k - 1) * len(idxs) + j],
                        device_id=peer, device_id_type=MESH).start()
        token[...] = jnp.zeros(token.shape, F32)

    sem_shapes = [pltpu.SemaphoreType.DMA(((N_DEV - 1) * len(idxs),)) for idxs in groups for _ in range(2)]
    res = pl.pallas_call(
        body, in_specs=[HBM_SPEC] * (n_src + n),
        out_specs=[SEM_SPEC] * (2 * n_g) + [HBM_SPEC] * (n_src + n) + [pl.BlockSpec(memory_space=pltpu.VMEM)],
        out_shape=sem_shapes + _hbm_like(srcs) + _hbm_like(lands) + [jax.ShapeDtypeStruct((8, LANES), F32)],
        input_output_aliases={i: 2 * n_g + i for i in range(n_src + n)},
        compiler_params=pltpu.CompilerParams(has_side_effects=pltpu.SideEffectType.DATAFLOW_SIDE_EFFECTING),
        name=name,
    )(*_in_hbm(srcs), *_in_hbm(lands))
    sems = [(res[2 * g], res[2 * g + 1]) for g in range(n_g)]
    return sems, list(res[2 * n_g:2 * n_g + n_src]), list(res[2 * n_g + n_src:2 * n_g + n_src + n]), res[-1]


def _exchange_wait(name, srcs, lands, pieces, waits, after):
    n_src, n, n_g = len(srcs), len(lands), len(waits)

    def body(*refs):
        src_refs, land_refs = refs[:n_src], refs[n_src:n_src + n]
        sems = refs[n_src + n:n_src + n + 2 * n_g]
        x, y, c, me = _my_position()
        at = 0
        for g, (idxs, _, _) in enumerate(waits):
            for k in PEER_ORDER:
                peer, pid = _peer(x, y, c, k)
                for j, i in enumerate(idxs):
                    p = pieces[i]
                    cp = pltpu.make_async_remote_copy(
                        src_ref=p.send(src_refs[p.src], pid), dst_ref=p.slot(land_refs[at + j], pid),
                        send_sem=sems[2 * g].at[(k - 1) * len(idxs) + j], recv_sem=sems[2 * g + 1].at[(k - 1) * len(idxs) + j],
                        device_id=peer, device_id_type=MESH)
                    cp.wait_send()
                    cp.wait_recv()
            at += len(idxs)

    sem_args = [s for _, send, recv in waits for s in (send, recv)]
    res = pl.pallas_call(
        body, in_specs=[HBM_SPEC] * (n_src + n) + [SEM_SPEC] * (2 * n_g) + [ANY_SPEC],
        out_specs=[HBM_SPEC] * (n_src + n), out_shape=_hbm_like(srcs) + _hbm_like(lands),
        input_output_aliases={i: i for i in range(n_src + n)},
        compiler_params=pltpu.CompilerParams(has_side_effects=pltpu.SideEffectType.DATAFLOW_SIDE_EFFECTING),
        name=name,
    )(*srcs, *lands, *sem_args, after)
    return list(res[:n_src]), list(res[n_src:])


def _weight_pieces(p_in, p_out, p_up, p_down):
    depth, cin, d = p_in.shape
    rout, hs = p_out.shape[1], p_up.shape[2]
    pieces = []
    for l in range(depth):
        whole = functools.partial(lambda ref, dev, l: ref.at[l], l=l)
        layer = functools.partial(lambda dev, l: (l, 0, 0), l=l)

        def rows(src, n_rows, whole=whole, layer=layer):
            return _Piece(src, whole, lambda ref, dev: _rows_of(n_rows)(ref.at[0], dev), (1, N_DEV * n_rows, d),
                          ((None, n_rows, d), layer), ((None, n_rows, d), lambda dev: (0, dev, 0)))

        pieces += [
            rows(0, cin), rows(1, rout),
            _Piece(2, whole, lambda ref, dev: ref.at[0, dev], (1, N_DEV, d, hs),
                   ((None, d, hs), layer), ((None, None, d, hs), lambda dev: (0, dev, 0, 0))),
            rows(3, hs),
        ]
    return pieces


def _grad_pieces(g_pair, kinds):
    pieces = []
    for i, (g, kind) in enumerate(zip(g_pair, kinds)):
        lead = lambda dev: (dev, 0, 0)
        if kind == "up":
            blk = ((None,) + g.shape[1:], lead)
            pieces.append(_Piece(i, lambda ref, dev: ref.at[dev], lambda ref, dev: ref.at[dev], g.shape, blk, blk))
        else:
            rows, cols = g.shape[0] // N_DEV, g.shape[1]
            pieces.append(_Piece(i, _rows_of(rows), lambda ref, dev: ref.at[dev], (N_DEV, rows, cols),
                                 ((rows, cols), lambda dev: (dev, 0)), ((None, rows, cols), lead)))
    return pieces


def _all_reduce_small(vec):
    rows = vec.shape[0]

    def body(v_ref, o_ref, buf_ref, send_sems, recv_sems):
        x, y, c, me = _my_position()
        buf_ref[me] = v_ref[...]
        sends = []
        for k in PEER_ORDER:
            peer, _ = _peer(x, y, c, k)
            cp = pltpu.make_async_remote_copy(src_ref=v_ref, dst_ref=buf_ref.at[me], send_sem=send_sems.at[k - 1],
                                              recv_sem=recv_sems.at[k - 1], device_id=peer, device_id_type=MESH)
            cp.start()
            sends.append(cp)
        for k in PEER_ORDER:
            peer, pid = _peer(x, y, c, k)
            pltpu.make_async_remote_copy(src_ref=v_ref, dst_ref=buf_ref.at[pid], send_sem=send_sems.at[k - 1],
                                         recv_sem=recv_sems.at[k - 1], device_id=peer, device_id_type=MESH).wait_recv()
        for cp in sends:
            cp.wait_send()
        total = buf_ref[0]
        for dev in range(1, N_DEV):
            total = total + buf_ref[dev]
        o_ref[...] = total

    vm = pl.BlockSpec(memory_space=pltpu.VMEM)
    return pl.pallas_call(
        body, in_specs=[vm], out_specs=vm, out_shape=jax.ShapeDtypeStruct(vec.shape, F32),
        scratch_shapes=[pltpu.VMEM((N_DEV, rows, LANES), F32), pltpu.SemaphoreType.DMA((N_DEV - 1,)),
                        pltpu.SemaphoreType.DMA((N_DEV - 1,))],
        name="all_reduce_small",
    )(vec)


def _resident(block_shape, index_map):
    return pl.BlockSpec(block_shape, index_map, pipeline_mode=pl.Buffered(1))


def _fwd_inproj(layer, x, gain, g_in, cos, sin, tm=256):
    s, d = x.shape

    def body(x_ref, gain_ref, w_ref, cos_ref, sin_ref, proj_ref, h_ref):
        h = _rms_fwd(x_ref[...], gain_ref[...]).astype(BF)
        h_ref[...] = h
        cs = _tile_lanes(cos_ref[...], SEG // LANES)
        sn = _tile_lanes(sin_ref[...], SEG // LANES)
        for seg in range(N_SEG):
            acc = _dot(h, w_ref[seg * SEG:(seg + 1) * SEG, :], NT)
            if seg < 2:
                acc = acc * cs + _rope_partner(acc) * sn
            if seg == 0:
                acc = acc * ATTN_SCALE
            proj_ref[:, seg * SEG:(seg + 1) * SEG] = acc

    return pl.pallas_call(
        body, grid=(s // tm,),
        in_specs=[pl.BlockSpec((tm, d), lambda i: (i, 0)), pl.BlockSpec((None, 1, d), lambda i: (layer, 0, 0)),
                  _resident((None, PROJ_W, d), lambda i: (0, 0, 0)),
                  pl.BlockSpec((tm, LANES), lambda i: (i, 0)), pl.BlockSpec((tm, LANES), lambda i: (i, 0))],
        out_specs=[pl.BlockSpec((tm, PROJ_W), lambda i: (i, 0)), pl.BlockSpec((tm, d), lambda i: (i, 0))],
        out_shape=[jax.ShapeDtypeStruct((s, PROJ_W), F32), jax.ShapeDtypeStruct((s, d), BF)],
        compiler_params=_cparams(("parallel",)), name=f"fwd_inproj_l{layer}",
    )(x, gain, g_in, cos, sin)


ATTN_BLOCKS = {1: 4, 4: 1, 16: 1}
ATTN_GROUP = 4


def _attn_masks(first_block_has_prev):
    row = lax.broadcasted_iota(jnp.int32, (SPAN, 2 * SPAN), 0)
    col = lax.broadcasted_iota(jnp.int32, (SPAN, 2 * SPAN), 1)
    band = (col >= row) & (col <= row + SPAN)
    lane = lax.broadcasted_iota(jnp.int32, (SPAN, LANES), 1)
    return band & ((col >= SPAN) | first_block_has_prev), band, lane < 64


def _attn_specs(dil, n_in_extra):
    blk = SPAN * dil
    nb = ATTN_BLOCKS[dil]
    pairs = ATTN_W // LANES
    q_spec = pl.BlockSpec((nb * blk, LANES), lambda p, n: (n, p))

    def prev(seg):
        return pl.BlockSpec((blk, LANES), lambda p, n: (jnp.maximum(n * nb - 1, 0), seg * pairs + p))

    def cur(seg):
        return pl.BlockSpec((nb * blk, LANES), lambda p, n: (n, seg * pairs + p))

    return [q_spec, prev(1), cur(1), prev(2), cur(2)] + [q_spec] * n_in_extra


def _strided(ref, dil, r, n_blocks):
    return ref[pl.ds(r, n_blocks * SPAN, stride=dil), :] if dil > 1 else ref[...]


def _attn_fwd(layer, dil, proj):
    s = proj.shape[0]
    nb = ATTN_BLOCKS[dil]
    unit = nb * SPAN * dil

    def body(q_ref, kp_ref, kc_ref, vp_ref, vc_ref, o_ref, lse_ref):
        m_first, m_rest, is_a = _attn_masks(pl.program_id(1) > 0)
        sels = (is_a, jnp.logical_not(is_a))
        ones = jnp.ones((2 * SPAN, LANES), BF)
        for r0 in range(0, dil, ATTN_GROUP):
            group = range(r0, min(r0 + ATTN_GROUP, dil))
            q_all = {r: _strided(q_ref, dil, r, nb) for r in group}
            k_all = {r: jnp.concatenate([_strided(kp_ref, dil, r, 1), _strided(kc_ref, dil, r, nb)], axis=0).astype(BF)
                     for r in group}
            v_all = {r: jnp.concatenate([_strided(vp_ref, dil, r, 1), _strided(vc_ref, dil, r, nb)], axis=0).astype(BF)
                     for r in group}
            items = [(r, b, h) for r in group for b in range(nb) for h in range(2)]
            sc = [jnp.where(m_first if b == 0 else m_rest,
                            _dot(jnp.where(sels[h], q_all[r][b * SPAN:(b + 1) * SPAN], 0.0).astype(BF),
                                 k_all[r][b * SPAN:(b + 2) * SPAN], NT), MASK_VALUE) for r, b, h in items]
            mx = [jnp.max(jnp.maximum(t[:, :SPAN], t[:, SPAN:]), axis=-1, keepdims=True) for t in sc]
            p = [jnp.exp(t - m).astype(BF) for t, m in zip(sc, mx)]
            den = [_dot(t, ones, NN) for t in p]
            num = [_dot(t, v_all[r][b * SPAN:(b + 2) * SPAN], NN) for t, (r, b, h) in zip(p, items)]
            o = [a / d for a, d in zip(num, den)]
            lse = [m + jnp.log(d) for m, d in zip(mx, den)]
            for r in group:
                at = [items.index((r, b, 0)) for b in range(nb)]
                o_val = jnp.concatenate([jnp.where(is_a, o[i], o[i + 1]) for i in at], axis=0)
                lse_val = jnp.concatenate([jnp.where(is_a, lse[i], lse[i + 1]) for i in at], axis=0)
                if dil > 1:
                    o_ref[pl.ds(r, nb * SPAN, stride=dil), :] = o_val
                    lse_ref[pl.ds(r, nb * SPAN, stride=dil), :] = lse_val
                else:
                    o_ref[...] = o_val
                    lse_ref[...] = lse_val

    out_spec = pl.BlockSpec((unit, LANES), lambda p, n: (n, p))
    return pl.pallas_call(
        body, grid=(ATTN_W // LANES, s // unit), in_specs=_attn_specs(dil, 0), out_specs=[out_spec, out_spec],
        out_shape=[jax.ShapeDtypeStruct((s, ATTN_W), F32)] * 2,
        compiler_params=_cparams(("parallel", "arbitrary")), name=f"attn_fwd_d{dil}_l{layer}",
    )(proj, proj, proj, proj, proj)


def _attn_merge(layer, outs, lses, gain, mixed, tm=512):
    s = outs[0].shape[0]

    def body(o1, o2, o3, l1, l2, l3, gain_ref, mixed_ref, o_ref, lse_ref, n_ref):
        ls = (l1[...], l2[...], l3[...])
        mx = jnp.maximum(jnp.maximum(ls[0], ls[1]), ls[2])
        ws = [jnp.exp(l - mx) for l in ls]
        den = ws[0] + ws[1] + ws[2]
        o = (ws[0] * o1[...] + ws[1] * o2[...] + ws[2] * o3[...]) / den
        o_ref[...] = o
        lse_ref[...] = mx + jnp.log(den)
        n_ref[...] = _rms_fwd(o, gain_ref[...]).astype(BF)

    blk = pl.BlockSpec((tm, ATTN_W), lambda i: (i, 0))
    return pl.pallas_call(
        body, grid=(s // tm,),
        in_specs=[blk] * 6 + [pl.BlockSpec((None, 1, ATTN_W), lambda i: (layer, 0, 0)), pl.BlockSpec(memory_space=pl.ANY)],
        out_specs=[blk, blk, blk],
        out_shape=[jax.ShapeDtypeStruct((s, ATTN_W), F32), jax.ShapeDtypeStruct((s, ATTN_W), F32),
                   jax.ShapeDtypeStruct(mixed.shape, BF)],
        input_output_aliases={7: 2},
        compiler_params=_cparams(("parallel",)), name=f"attn_merge_l{layer}",
    )(*outs, *lses, gain, mixed)


def _chunk_cumsum(x, reverse=False):
    n = x.shape[0]
    pos = lax.broadcasted_iota(jnp.int32, x.shape, 0) % HGRN_CHUNK
    for sh in (1, 2, 4, 8):
        if reverse:
            x = x + jnp.where(pos < HGRN_CHUNK - sh, pltpu.roll(x, n - sh, 0), 0.0)
        else:
            x = x + jnp.where(pos >= sh, pltpu.roll(x, sh, 0), 0.0)
    return x


def _chunk_row(x, row):
    r, n = x.shape
    x3 = x.reshape(r // HGRN_CHUNK, HGRN_CHUNK, n)
    return jnp.broadcast_to(x3[:, row:row + 1, :], x3.shape).reshape(r, n)


def _hgrn_pre(qh, z, lb):
    sig = _sigmoid(z)
    f = lb + (1.0 - lb) * sig
    k = (1.0 - lb) * _sigmoid(-z)
    sq = _sigmoid(qh)
    q = qh * sq * HGRN_SCALE
    g = _chunk_cumsum(jnp.log(f))
    g_mid = _chunk_row(g, HGRN_CHUNK // 2 - 1)
    g_last = _chunk_row(g, HGRN_CHUNK - 1)
    e_q, e_k = jnp.exp(g - g_mid), jnp.exp(g_mid - g)
    e_in, e_out = jnp.exp(g), jnp.exp(g_last - g)
    return dict(sig=sig, f=f, k=k, sq=sq, q=q, g_last=g_last, e_q=e_q, e_k=e_k, e_in=e_in, e_out=e_out,
                qt=q * e_q, kt=k * e_k, qg=q * e_in, kout=k * e_out)


def _hgrn_mask():
    row = lax.broadcasted_iota(jnp.int32, (LANES, LANES), 0)
    col = lax.broadcasted_iota(jnp.int32, (LANES, LANES), 1)
    return (row // HGRN_CHUNK == col // HGRN_CHUNK) & (col <= row)


def _hgrn_in_specs(layer, rev, nblk):
    def blk(b):
        return nblk - 1 - b if rev else b
    first = 3 * ATTN_W // HGRN_W
    specs = [pl.BlockSpec((LANES, HGRN_W), functools.partial(lambda b, seg: (blk(b), first + seg), seg=seg))
             for seg in range(4)]
    specs.append(pl.BlockSpec((None, 1, HGRN_W), lambda b: (layer, 0, 0)))
    specs.append(pl.BlockSpec((None, 1, HGRN_DIM), lambda b: (layer, 0, 0)))
    return specs, blk


def _head(x, h):
    return x[:, h * HGRN_DIM:(h + 1) * HGRN_DIM]


def _chunk(x, c):
    return x[c * HGRN_CHUNK:(c + 1) * HGRN_CHUNK]


HEADS = range(HGRN_HEADS)
CHUNKS = range(LANES // HGRN_CHUNK)


def _hgrn_fwd(layer, proj, lb, gain):
    s = proj.shape[0]
    nblk = s // LANES
    cpb = len(CHUNKS)

    def body(q_ref, f_ref, i_ref, g_ref, lb_ref, gain_ref, o_ref, rec_ref, st_ref, state):
        @pl.when(pl.program_id(0) == 0)
        def _():
            state[...] = jnp.zeros(state.shape, F32)

        pre = _hgrn_pre(q_ref[...], f_ref[...], lb_ref[...])
        v = i_ref[...].astype(BF)
        qt, kt, qg, kout = (pre[n].astype(BF) for n in ("qt", "kt", "qg", "kout"))
        dec = jnp.exp(pre["g_last"])
        mask = _hgrn_mask()
        a = [jnp.where(mask, _dot(_head(qt, h), _head(kt, h), NT), 0.0).astype(BF) for h in HEADS]
        o_intra = [_dot(a[h], _head(v, h), NN) for h in HEADS]
        update = [[_dot(_chunk(_head(v, h), c), _chunk(_head(kout, h), c), TN) for c in CHUNKS] for h in HEADS]
        for h in HEADS:
            st = state[h]
            for c in CHUNKS:
                st_ref[h, c * LANES:(c + 1) * LANES, :] = st
                st = st * _head(dec, h)[c * HGRN_CHUNK:c * HGRN_CHUNK + 1, :] + update[h][c]
            state[h] = st
        inter = [[_dot(_chunk(_head(qg, h), c), st_ref[h, c * LANES:(c + 1) * LANES, :].astype(BF), NT) for c in CHUNKS]
                 for h in HEADS]
        o = [o_intra[h] + jnp.concatenate(inter[h], axis=0) for h in HEADS]
        o_ref[...] = jnp.concatenate(o, axis=1)
        gate = g_ref[...]
        normed = jnp.concatenate([_rms_fwd(o[h], gain_ref[...]) for h in HEADS], axis=1)
        rec_ref[...] = (normed * (gate * _sigmoid(gate))).astype(BF)

    specs, _ = _hgrn_in_specs(layer, False, nblk)
    return pl.pallas_call(
        body, grid=(nblk,), in_specs=specs,
        out_specs=[pl.BlockSpec((LANES, HGRN_W), lambda b: (b, 0)), pl.BlockSpec((LANES, HGRN_W), lambda b: (b, 1)),
                   pl.BlockSpec((HGRN_HEADS, cpb * LANES, LANES), lambda b: (0, b, 0))],
        out_shape=[jax.ShapeDtypeStruct((s, HGRN_W), F32), jax.ShapeDtypeStruct((s, MIX_W), BF),
                   jax.ShapeDtypeStruct((HGRN_HEADS, nblk * cpb * LANES, LANES), F32)],
        scratch_shapes=[pltpu.VMEM((HGRN_HEADS, LANES, LANES), F32)],
        compiler_params=_cparams(("arbitrary",)), name=f"hgrn_fwd_l{layer}",
    )(proj, proj, proj, proj, lb, gain)


def _fwd_outproj(layer, mixed, g_out, x, tm=512):
    s, d = x.shape
    mw = mixed.shape[1]

    def body(m_ref, w_ref, x_ref, o_ref):
        o_ref[...] = x_ref[...] + _dot(m_ref[...], w_ref[...], NN)

    row = pl.BlockSpec((tm, d), lambda i: (i, 0))
    return pl.pallas_call(
        body, grid=(s // tm,),
        in_specs=[pl.BlockSpec((tm, mw), lambda i: (i, 0)), _resident((None, mw, d), lambda i: (0, 0, 0)), row],
        out_specs=row, out_shape=jax.ShapeDtypeStruct((s, d), F32),
        compiler_params=_cparams(("parallel",)), name=f"fwd_outproj_l{layer}",
    )(mixed, g_out, x)


def _relu2(u):
    return jnp.square(jnp.maximum(u, 0)).astype(BF)


def _mlp_fwd(layer, x, gain, g_up, g_down, tm=256):
    s, d = x.shape
    nblk, hs = g_up.shape[1], g_up.shape[3]

    def body(x_ref, gain_ref, up_ref, down_ref, o_ref, u_ref, h_ref):
        xv = x_ref[...]
        h = _rms_fwd(xv, gain_ref[...]).astype(BF)
        h_ref[...] = h
        acc = xv
        for j in range(nblk):
            u = _dot(h, up_ref[j], NN)
            u_ref[:, j * hs:(j + 1) * hs] = u.astype(BF)
            acc = acc + _dot(_relu2(u), down_ref[j * hs:(j + 1) * hs, :], NN)
        o_ref[...] = acc

    row = pl.BlockSpec((tm, d), lambda i: (i, 0))
    return pl.pallas_call(
        body, grid=(s // tm,),
        in_specs=[row, pl.BlockSpec((None, 1, d), lambda i: (layer, 0, 0)),
                  _resident((None, nblk, d, hs), lambda i: (0, 0, 0, 0)),
                  _resident((None, nblk * hs, d), lambda i: (0, 0, 0))],
        out_specs=[row, pl.BlockSpec((tm, nblk * hs), lambda i: (i, 0)), row],
        out_shape=[jax.ShapeDtypeStruct((s, d), F32), jax.ShapeDtypeStruct((s, nblk * hs), BF),
                   jax.ShapeDtypeStruct((s, d), BF)],
        compiler_params=_cparams(("parallel",)), name=f"mlp_fwd_l{layer}",
    )(x, gain, g_up, g_down)


def _loss_head(x, gain, target, tm=512):
    s, d = x.shape

    def body(x_ref, gain_ref, t_ref, dx_ref, dxb_ref, dgain_ref, loss_ref):
        i = pl.program_id(0)
        xv, gv = x_ref[...], gain_ref[...]
        err = _rms_fwd(xv, gv) - t_ref[...]
        dx, dgain = _rms_bwd(err * (1.0 / d), xv, gv)
        dx_ref[...] = dx
        dxb_ref[...] = dx.astype(BF)
        part = _part8(dgain)
        lpart = _part8(0.5 * jnp.mean(err * err, axis=-1, keepdims=True) * jnp.ones((1, LANES), F32))

        @pl.when(i == 0)
        def _():
            dgain_ref[...] = part
            loss_ref[...] = lpart

        @pl.when(i > 0)
        def _():
            dgain_ref[...] += part
            loss_ref[...] += lpart

    row = pl.BlockSpec((tm, d), lambda i: (i, 0))
    return pl.pallas_call(
        body, grid=(s // tm,),
        in_specs=[row, pl.BlockSpec((1, d), lambda i: (0, 0)), row],
        out_specs=[row, row, pl.BlockSpec((8, d), lambda i: (0, 0)), pl.BlockSpec((8, LANES), lambda i: (0, 0))],
        out_shape=[jax.ShapeDtypeStruct((s, d), F32), jax.ShapeDtypeStruct((s, d), BF), jax.ShapeDtypeStruct((8, d), F32),
                   jax.ShapeDtypeStruct((8, LANES), F32)],
        compiler_params=_cparams(("arbitrary",)), name="loss_head",
    )(x, gain, target)


def _accumulate_rows(i, ref, part):
    @pl.when(i == 0)
    def _():
        ref[...] = part

    @pl.when(i > 0)
    def _():
        ref[...] += part


def _mlp_bwd(layer, dx, dxb, x, gain, u, g_up, g_down, tm=256):
    s, d = x.shape
    nblk, hs = g_up.shape[1], g_up.shape[3]

    def body(dx_ref, dxb_ref, x_ref, gain_ref, u_ref, up_ref, down_ref, o_ref, ob_ref, du_ref, dgain_ref):
        dxb_v = dxb_ref[...]
        acc = jnp.zeros((tm, d), F32)
        for j in range(nblk):
            cols = slice(j * hs, (j + 1) * hs)
            da = _dot(dxb_v, down_ref[cols, :], NT)
            du = (da * (2.0 * jnp.maximum(u_ref[:, cols].astype(F32), 0.0))).astype(BF)
            du_ref[:, cols] = du
            acc = acc + _dot(du, up_ref[j], NT)
        dxn, dgain = _rms_bwd(acc, x_ref[...], gain_ref[...])
        out = dx_ref[...] + dxn
        o_ref[...] = out
        ob_ref[...] = out.astype(BF)
        _accumulate_rows(pl.program_id(0), dgain_ref, _part8(dgain))

    row = pl.BlockSpec((tm, d), lambda i: (i, 0))
    wide = pl.BlockSpec((tm, nblk * hs), lambda i: (i, 0))
    return pl.pallas_call(
        body, grid=(s // tm,),
        in_specs=[row, row, row, pl.BlockSpec((None, 1, d), lambda i: (layer, 0, 0)), wide,
                  _resident((None, nblk, d, hs), lambda i: (0, 0, 0, 0)),
                  _resident((None, nblk * hs, d), lambda i: (0, 0, 0))],
        out_specs=[row, row, wide, pl.BlockSpec((8, d), lambda i: (0, 0))],
        out_shape=[jax.ShapeDtypeStruct((s, d), F32), jax.ShapeDtypeStruct((s, d), BF),
                   jax.ShapeDtypeStruct((s, nblk * hs), BF), jax.ShapeDtypeStruct((8, d), F32)],
        compiler_params=_cparams(("arbitrary",)), name=f"mlp_bwd_l{layer}",
    )(dx, dxb, x, gain, u, g_up, g_down)


def _bwd_outproj(layer, dxb, g_out, tm=512):
    s, d = dxb.shape
    mw = g_out.shape[1]

    def body(dx_ref, w_ref, o_ref):
        o_ref[...] = _dot(dx_ref[...], w_ref[...], NT)

    return pl.pallas_call(
        body, grid=(s // tm,),
        in_specs=[pl.BlockSpec((tm, d), lambda i: (i, 0)), _resident((None, mw, d), lambda i: (0, 0, 0))],
        out_specs=pl.BlockSpec((tm, mw), lambda i: (i, 0)), out_shape=jax.ShapeDtypeStruct((s, mw), F32),
        compiler_params=_cparams(("parallel",)), name=f"bwd_outproj_l{layer}",
    )(dxb, g_out)


def _attn_norm_bwd(layer, dmixed, o, gain, tm=512):
    s = o.shape[0]

    def body(dm_ref, o_ref, gain_ref, do_ref, delta_ref, dgain_ref):
        i = pl.program_id(0)
        ov = o_ref[...]
        do, dgain = _rms_bwd(dm_ref[...], ov, gain_ref[...])
        do_ref[...] = do
        row = lax.broadcasted_iota(jnp.int32, (ATTN_W, ATTN_W), 0)
        col = lax.broadcasted_iota(jnp.int32, (ATTN_W, ATTN_W), 1)
        same_head = jnp.where(row // 64 == col // 64, 1.0, 0.0)
        delta_ref[...] = jnp.dot(do * ov, same_head, precision=lax.Precision.HIGHEST, preferred_element_type=F32)
        part = _part8(dgain)

        @pl.when(i == 0)
        def _():
            dgain_ref[...] = part

        @pl.when(i > 0)
        def _():
            dgain_ref[...] += part

    blk = pl.BlockSpec((tm, ATTN_W), lambda i: (i, 0))
    return pl.pallas_call(
        body, grid=(s // tm,), in_specs=[blk, blk, pl.BlockSpec((None, 1, ATTN_W), lambda i: (layer, 0, 0))],
        out_specs=[blk, blk, pl.BlockSpec((8, ATTN_W), lambda i: (0, 0))],
        out_shape=[jax.ShapeDtypeStruct((s, ATTN_W), F32), jax.ShapeDtypeStruct((s, ATTN_W), F32),
                   jax.ShapeDtypeStruct((8, ATTN_W), F32)],
        compiler_params=_cparams(("arbitrary",)), name=f"attn_norm_bwd_l{layer}",
    )(dmixed, o, gain)


def _attn_bwd(layer, dil, proj, do, lse, delta):
    s = proj.shape[0]
    nb = ATTN_BLOCKS[dil]
    blk = SPAN * dil
    unit = nb * blk

    def body(q_ref, kp_ref, kc_ref, vp_ref, vc_ref, do_ref, lse_ref, delta_ref, dq_ref, dkc_ref, dkp_ref, dvc_ref, dvp_ref):
        m_first, m_rest, is_a = _attn_masks(pl.program_id(1) > 0)
        is_a_keys = lax.broadcasted_iota(jnp.int32, (2 * SPAN, LANES), 1) < 64
        sels = (is_a, jnp.logical_not(is_a))
        key_sels = (is_a_keys, jnp.logical_not(is_a_keys))

        def blk(x, b, n=1):
            return x[b * SPAN:(b + n) * SPAN]

        for r0 in range(0, dil, ATTN_GROUP):
            group = range(r0, min(r0 + ATTN_GROUP, dil))
            q_all = {r: _strided(q_ref, dil, r, nb) for r in group}
            do_all = {r: _strided(do_ref, dil, r, nb) for r in group}
            lse_all = {r: _strided(lse_ref, dil, r, nb) for r in group}
            delta_all = {r: _strided(delta_ref, dil, r, nb) for r in group}
            k_all = {r: jnp.concatenate([_strided(kp_ref, dil, r, 1), _strided(kc_ref, dil, r, nb)], axis=0) for r in group}
            v_all = {r: jnp.concatenate([_strided(vp_ref, dil, r, 1), _strided(vc_ref, dil, r, nb)], axis=0).astype(BF)
                     for r in group}
            items = [(r, b, h) for r in group for b in range(nb) for h in range(2)]
            qh = [jnp.where(sels[h], blk(q_all[r], b), 0.0).astype(BF) for r, b, h in items]
            doh = [jnp.where(sels[h], blk(do_all[r], b), 0.0).astype(BF) for r, b, h in items]
            kh = [jnp.where(key_sels[h], blk(k_all[r], b, 2), 0.0).astype(BF) for r, b, h in items]
            sc = [jnp.where(m_first if b == 0 else m_rest, _dot(qh[i], kh[i], NT), MASK_VALUE)
                  for i, (r, b, h) in enumerate(items)]
            dp = [_dot(doh[i], blk(v_all[r], b, 2), NT) for i, (r, b, h) in enumerate(items)]
            p = [jnp.exp(sc[i] - blk(lse_all[r], b)[:, 64 * h:64 * h + 1]) for i, (r, b, h) in enumerate(items)]
            ds = [(p[i] * (dp[i] - blk(delta_all[r], b)[:, 64 * h:64 * h + 1])).astype(BF)
                  for i, (r, b, h) in enumerate(items)]
            dv = [_dot(p[i].astype(BF), doh[i], TN) for i in range(len(items))]
            dq = [_dot(ds[i], kh[i], NN) for i in range(len(items))]
            dk = [_dot(ds[i], qh[i], TN) for i in range(len(items))]
            for r in group:
                at = [items.index((r, b, 0)) for b in range(nb)]
                dq_blocks = [dq[i] + dq[i + 1] for i in at]
                dk_blocks = [dk[i] + dk[i + 1] for i in at]
                dv_blocks = [dv[i] + dv[i + 1] for i in at]
                dk_own = [blk(dk_blocks[b], 1) + (blk(dk_blocks[b + 1], 0) if b + 1 < nb else 0.0) for b in range(nb)]
                dv_own = [blk(dv_blocks[b], 1) + (blk(dv_blocks[b + 1], 0) if b + 1 < nb else 0.0) for b in range(nb)]
                own = ((dq_ref, dq_blocks), (dkc_ref, dk_own), (dvc_ref, dv_own))
                if dil > 1:
                    for ref, blocks in own:
                        ref[pl.ds(r, nb * SPAN, stride=dil), :] = jnp.concatenate(blocks, axis=0)
                    dkp_ref[pl.ds(r, SPAN, stride=dil), :] = blk(dk_blocks[0], 0)
                    dvp_ref[pl.ds(r, SPAN, stride=dil), :] = blk(dv_blocks[0], 0)
                else:
                    for ref, blocks in own:
                        ref[...] = jnp.concatenate(blocks, axis=0)
                    dkp_ref[...] = blk(dk_blocks[0], 0)
                    dvp_ref[...] = blk(dv_blocks[0], 0)

    own_spec = pl.BlockSpec((unit, LANES), lambda p, n: (n, p))
    prev_spec = pl.BlockSpec((blk, LANES), lambda p, n: (n, p))
    full, compact = jax.ShapeDtypeStruct((s, ATTN_W), F32), jax.ShapeDtypeStruct((s // nb, ATTN_W), F32)
    return pl.pallas_call(
        body, grid=(ATTN_W // LANES, s // unit), in_specs=_attn_specs(dil, 3),
        out_specs=[own_spec, own_spec, prev_spec, own_spec, prev_spec], out_shape=[full, full, compact, full, compact],
        compiler_params=_cparams(("parallel", "arbitrary")), name=f"attn_bwd_d{dil}_l{layer}",
    )(proj, proj, proj, proj, proj, do, lse, delta)


def _attn_combine(layer, parts, cos, sin):
    s = parts[0][0].shape[0]
    nblk = s // SPAN
    arrays, specs, gates = [], [], []
    for dil, (dq, dkc, dkp, dvc, dvp) in zip(DILATIONS, parts):
        nb = ATTN_BLOCKS[dil]
        n_units = s // (nb * SPAN * dil)

        def sender(i, dil=dil, nb=nb):
            nxt = (i // dil + 1) // nb
            return nxt, nxt * dil + i % dil

        here = pl.BlockSpec((SPAN, ATTN_W), lambda i: (i, 0))
        ahead = pl.BlockSpec((SPAN, ATTN_W), functools.partial(
            lambda i, sender, last: (jnp.minimum(sender(i)[1], last), 0), sender=sender, last=n_units * dil - 1))
        gates.append(functools.partial(
            lambda i, sender, dil, nb, n_units: ((i // dil + 1) % nb == 0) & (sender(i)[0] < n_units),
            sender=sender, dil=dil, nb=nb, n_units=n_units))
        arrays += [dq, dkc, dkp, dvc, dvp]
        specs += [here, here, ahead, here, ahead]
    tab = pl.BlockSpec((SPAN, LANES), lambda i: (i, 0))

    def body(*refs):
        cos_ref, sin_ref, out_ref = refs[15], refs[16], refs[17]
        i = pl.program_id(0)
        dq = dk = dv = jnp.zeros((SPAN, ATTN_W), F32)
        for p in range(len(DILATIONS)):
            dq_r, dkc_r, dkp_r, dvc_r, dvp_r = refs[5 * p:5 * p + 5]
            has_next = gates[p](i)
            dq = dq + dq_r[...]
            dk = dk + dkc_r[...] + jnp.where(has_next, dkp_r[...], 0.0)
            dv = dv + dvc_r[...] + jnp.where(has_next, dvp_r[...], 0.0)
        cs = _tile_lanes(cos_ref[...], ATTN_W // LANES)
        sn = _tile_lanes(sin_ref[...], ATTN_W // LANES)
        out_ref[0] = ((dq * cs - _rope_partner(dq) * sn) * ATTN_SCALE).astype(BF)
        out_ref[1] = (dk * cs - _rope_partner(dk) * sn).astype(BF)
        out_ref[2] = dv.astype(BF)

    return pl.pallas_call(
        body, grid=(nblk,), in_specs=specs + [tab, tab],
        out_specs=pl.BlockSpec((3, SPAN, ATTN_W), lambda i: (0, i, 0)),
        out_shape=jax.ShapeDtypeStruct((3, s, ATTN_W), BF),
        compiler_params=_cparams(("parallel",)), name=f"attn_combine_l{layer}",
    )(*arrays, cos, sin)


def _hgrn_bwd(layer, proj, lb, gain, o, dmixed, states):
    s = proj.shape[0]
    nblk = s // LANES
    cpb = len(CHUNKS)

    def body(q_ref, f_ref, i_ref, g_ref, lb_ref, gain_ref, o_ref, drec_ref, st_ref, dseg_ref, dlb_ref, dgain_ref,
             dstate, dst_buf):
        step = pl.program_id(0)

        @pl.when(step == 0)
        def _():
            dstate[...] = jnp.zeros(dstate.shape, F32)

        lbv, gv = lb_ref[...], gain_ref[...]
        qh, z, gate_in = q_ref[...], f_ref[...], g_ref[...]
        pre = _hgrn_pre(qh, z, lbv)
        v = i_ref[...].astype(BF)
        sg = _sigmoid(gate_in)
        ov, drec = o_ref[...], drec_ref[...]
        dnormed = drec * (gate_in * sg)
        back = [_rms_bwd(_head(dnormed, h), _head(ov, h), gv) for h in HEADS]
        do_b = jnp.concatenate([b[0] for b in back], axis=1).astype(BF)
        dgain = back[0][1] + back[1][1] + back[2][1] + back[3][1]
        normed = jnp.concatenate([_rms_fwd(_head(ov, h), gv) for h in HEADS], axis=1)
        dgate_in = drec * normed * (sg * (1.0 + gate_in * (1.0 - sg)))
        mask = _hgrn_mask()
        qt, kt, qg, kout = (pre[n].astype(BF) for n in ("qt", "kt", "qg", "kout"))
        dec = jnp.exp(pre["g_last"])
        a = [jnp.where(mask, _dot(_head(qt, h), _head(kt, h), NT), 0.0).astype(BF) for h in HEADS]
        da = [jnp.where(mask, _dot(_head(do_b, h), _head(v, h), NT), 0.0).astype(BF) for h in HEADS]
        dv_intra = [_dot(a[h], _head(do_b, h), TN) for h in HEADS]
        dqt = jnp.concatenate([_dot(da[h], _head(kt, h), NN) for h in HEADS], axis=1)
        dkt = jnp.concatenate([_dot(da[h], _head(qt, h), TN) for h in HEADS], axis=1)
        feed = [[_dot(_chunk(_head(do_b, h), c), _chunk(_head(qg, h), c), TN) for c in CHUNKS] for h in HEADS]
        for h in HEADS:
            dst = dstate[h]
            for c in reversed(CHUNKS):
                dst_buf[h, c * LANES:(c + 1) * LANES, :] = dst
                dst = dst * _head(dec, h)[c * HGRN_CHUNK:c * HGRN_CHUNK + 1, :] + feed[h][c]
            dstate[h] = dst

        def per_chunk(fn):
            cols = []
            for h in HEADS:
                rows = [jnp.broadcast_to(t, (HGRN_CHUNK, HGRN_DIM)) for t in (fn(h, c) for c in CHUNKS)]
                cols.append(jnp.concatenate(rows, axis=0))
            return jnp.concatenate(cols, axis=1)

        def st_prev(h, c):
            return st_ref[h, c * LANES:(c + 1) * LANES, :]

        def dst_at(h, c):
            return dst_buf[h, c * LANES:(c + 1) * LANES, :]

        dqg = per_chunk(lambda h, c: _dot(_chunk(_head(do_b, h), c), st_prev(h, c).astype(BF), NN))
        dkout = per_chunk(lambda h, c: _dot(_chunk(_head(v, h), c), dst_at(h, c).astype(BF), NN))
        dv_inter = per_chunk(lambda h, c: _dot(_chunk(_head(kout, h), c), dst_at(h, c).astype(BF), NT))
        dg_state = per_chunk(lambda h, c: jnp.sum(dst_at(h, c) * st_prev(h, c), axis=0, keepdims=True))
        dg_kout = per_chunk(lambda h, c: jnp.sum(_chunk(_head(dkout * pre["kout"], h), c), axis=0, keepdims=True))
        dv = jnp.concatenate(dv_intra, axis=1) + dv_inter
        pos = lax.broadcasted_iota(jnp.int32, (LANES, HGRN_W), 0) % HGRN_CHUNK
        dq = dqt * pre["e_q"] + dqg * pre["e_in"]
        dk = dkt * pre["e_k"] + dkout * pre["e_out"]
        dg = (dqt * pre["qt"] - dkt * pre["kt"] + dqg * pre["qg"] - dkout * pre["kout"]
              + jnp.where(pos == HGRN_CHUNK - 1, dg_state * dec + dg_kout, 0.0))
        dlogf = _chunk_cumsum(dg, reverse=True)
        sig, sq = pre["sig"], pre["sq"]
        df = dlogf / pre["f"] - dk
        dseg_ref[0] = (dq * HGRN_SCALE * (sq * (1.0 + qh * (1.0 - sq)))).astype(BF)
        dseg_ref[1] = (df * (1.0 - lbv) * sig * (1.0 - sig)).astype(BF)
        dseg_ref[2] = dv.astype(BF)
        dseg_ref[3] = dgate_in.astype(BF)
        _accumulate_rows(step, dlb_ref, _part8(df * (1.0 - sig)))
        _accumulate_rows(step, dgain_ref, _part8(dgain))

    specs, blk = _hgrn_in_specs(layer, True, nblk)
    specs += [pl.BlockSpec((LANES, HGRN_W), lambda b: (blk(b), 0)),
              pl.BlockSpec((LANES, HGRN_W), lambda b: (blk(b), 1)),
              pl.BlockSpec((HGRN_HEADS, cpb * LANES, LANES), lambda b: (0, blk(b), 0))]
    return pl.pallas_call(
        body, grid=(nblk,), in_specs=specs,
        out_specs=[pl.BlockSpec((4, LANES, HGRN_W), lambda b: (0, blk(b), 0)),
                   pl.BlockSpec((8, HGRN_W), lambda b: (0, 0)), pl.BlockSpec((8, HGRN_DIM), lambda b: (0, 0))],
        out_shape=[jax.ShapeDtypeStruct((4, s, HGRN_W), BF), jax.ShapeDtypeStruct((8, HGRN_W), F32),
                   jax.ShapeDtypeStruct((8, HGRN_DIM), F32)],
        scratch_shapes=[pltpu.VMEM((HGRN_HEADS, LANES, LANES), F32), pltpu.VMEM((HGRN_HEADS, cpb * LANES, LANES), F32)],
        compiler_params=_cparams(("arbitrary",)), name=f"hgrn_bwd_l{layer}",
    )(proj, proj, proj, proj, lb, gain, o, dmixed, states)


def _bwd_inproj(layer, dqkv, dhg, g_in, x, gain, dres, tm=256):
    s, d = x.shape

    def body(dqkv_ref, dhg_ref, w_ref, x_ref, gain_ref, dres_ref, dx_ref, dxb_ref, dgain_ref):
        acc = jnp.zeros((tm, d), F32)
        for seg in range(N_SEG):
            a = dqkv_ref[seg] if seg < 3 else dhg_ref[seg - 3]
            acc = acc + _dot(a, w_ref[seg * SEG:(seg + 1) * SEG, :], NN)
        dx, dgain = _rms_bwd(acc, x_ref[...], gain_ref[...])
        out = dres_ref[...] + dx
        dx_ref[...] = out
        dxb_ref[...] = out.astype(BF)
        _accumulate_rows(pl.program_id(0), dgain_ref, _part8(dgain))

    row = pl.BlockSpec((tm, d), lambda i: (i, 0))
    return pl.pallas_call(
        body, grid=(s // tm,),
        in_specs=[pl.BlockSpec((3, tm, SEG), lambda i: (0, i, 0)), pl.BlockSpec((4, tm, SEG), lambda i: (0, i, 0)),
                  _resident((None, PROJ_W, d), lambda i: (0, 0, 0)), row,
                  pl.BlockSpec((None, 1, d), lambda i: (layer, 0, 0)), row],
        out_specs=[row, row, pl.BlockSpec((8, d), lambda i: (0, 0))],
        out_shape=[jax.ShapeDtypeStruct((s, d), F32), jax.ShapeDtypeStruct((s, d), BF), jax.ShapeDtypeStruct((8, d), F32)],
        compiler_params=_cparams(("arbitrary",)), name=f"bwd_inproj_l{layer}",
    )(dqkv, dhg, g_in, x, gain, dres)


def _adamw(w, g, m, v):
    m2 = ADAM_B1 * m + (1.0 - ADAM_B1) * g
    v2 = ADAM_B2 * v + (1.0 - ADAM_B2) * (g * g)
    m_hat = m2 / (1.0 - ADAM_B1 ** ADAM_STEP)
    v_hat = v2 / (1.0 - ADAM_B2 ** ADAM_STEP)
    delta = -ADAM_LR * (m_hat / (jnp.sqrt(v_hat) + ADAM_EPS) + ADAM_WD * w)
    return delta, m2, v2


def _adam_big(name, parts, w, m, v, transpose, row_tiles):
    depth = w.shape[0]
    r, c = parts[0].shape[1], parts[0].shape[2]
    if transpose:
        tc = c // row_tiles
        p_spec = pl.BlockSpec((N_DEV, r, tc), lambda t: (0, 0, t))
        w_spec = pl.BlockSpec((depth, tc, r), lambda t: (0, t, 0))
    else:
        tr = r // row_tiles
        p_spec = pl.BlockSpec((N_DEV, tr, c), lambda t: (0, t, 0))
        w_spec = pl.BlockSpec((depth, tr, c), lambda t: (0, t, 0))

    def body(*refs):
        p_refs = refs[:depth]
        w_ref, m_ref, v_ref, g_ref, d_ref, m2_ref, v2_ref = refs[depth:]
        for l in range(depth):
            g = p_refs[l][0].astype(F32)
            for dev in range(1, N_DEV):
                g = g + p_refs[l][dev].astype(F32)
            if transpose:
                g = g.T
            delta, m2, v2 = _adamw(w_ref[l], g, m_ref[l], v_ref[l])
            g_ref[l] = g
            d_ref[l] = delta
            m2_ref[l] = m2
            v2_ref[l] = v2

    return pl.pallas_call(
        body, grid=(row_tiles,), in_specs=[p_spec] * depth + [w_spec] * 3, out_specs=[w_spec] * 4,
        out_shape=[jax.ShapeDtypeStruct(w.shape, F32)] * 4,
        compiler_params=_cparams(("parallel",)), name=name,
    )(*parts, w, m, v)


def _adam_small(g, w, m, v):
    def body(g_ref, w_ref, m_ref, v_ref, d_ref, m2_ref, v2_ref):
        delta, m2, v2 = _adamw(w_ref[...], g_ref[...], m_ref[...], v_ref[...])
        d_ref[...] = delta
        m2_ref[...] = m2
        v2_ref[...] = v2

    vm = pl.BlockSpec(memory_space=pltpu.VMEM)
    return pl.pallas_call(body, in_specs=[vm] * 4, out_specs=[vm] * 3, out_shape=[jax.ShapeDtypeStruct(g.shape, F32)] * 3,
                          name="adam_small")(g, w, m, v)


def _lower_bounds(logits):
    def body(l_ref, lb_ref, jac_ref):
        l0, l1 = l_ref[0:1, :], l_ref[1:2, :]
        mx = jnp.maximum(l0, l1)
        e0, e1 = jnp.exp(l0 - mx), jnp.exp(l1 - mx)
        p0, p1 = e0 / (e0 + e1), e1 / (e0 + e1)
        lb_ref[0:1, :] = p0 - p0
        lb_ref[1:2, :] = (p0 + p1) - p0
        jac_ref[0:1, :] = -p0 * p1
        jac_ref[1:2, :] = p0 * p1

    vm = pl.BlockSpec(memory_space=pltpu.VMEM)
    return pl.pallas_call(body, in_specs=[vm], out_specs=[vm, vm], out_shape=[jax.ShapeDtypeStruct(logits.shape, F32)] * 2,
                          name="hgrn_lower_bounds")(logits)


def _rope_tables(s):
    half = 32
    inv_freq = ROPE_THETA ** (-jnp.arange(half, dtype=F32) / half)
    ang = jnp.arange(s, dtype=jnp.int32).astype(F32)[:, None] * inv_freq[None, :]
    cos, sin = jnp.cos(ang), jnp.sin(ang)
    return jnp.concatenate([cos] * 4, axis=1), jnp.concatenate([-sin, sin, -sin, sin], axis=1)


SMALL_NAMES = ("norm_mix", "attn_out_gain", "hgrn_lb_logits", "hgrn_out_gain", "norm_mlp", "norm_final")


def _pack_small(vals):
    flat = jnp.concatenate([v.reshape(-1) for v in vals])
    rows = -(-flat.shape[0] // (8 * LANES)) * 8
    return jnp.pad(flat, (0, rows * LANES - flat.shape[0])).reshape(rows, LANES)


def _unpack_small(packed, like):
    flat, out, off = packed.reshape(-1), [], 0
    for v in like:
        out.append(flat[off:off + v.size].reshape(v.shape))
        off += v.size
    return out


def kernel(x, norm_mix, w_in, attn_out_gain, hgrn_lb_logits, hgrn_out_gain, w_out, norm_mlp, w_up, w_down, norm_final, loss_target, m_norm_mix, m_w_in, m_attn_out_gain, m_hgrn_lb_logits, m_hgrn_out_gain, m_w_out, m_norm_mlp, m_w_up, m_w_down, m_norm_final, v_norm_mix, v_w_in, v_attn_out_gain, v_hgrn_lb_logits, v_hgrn_out_gain, v_w_out, v_norm_mlp, v_w_up, v_w_down, v_norm_final):
    depth = w_in.shape[0]
    assert depth == 2 and x.shape[0] == 1
    s, d = x.shape[1], x.shape[2]
    x0 = x[0]
    target = loss_target[0]
    cos, sin = _rope_tables(s)
    g_mix, g_attn, g_hg, g_mlp = (norm_mix[:, None, :], attn_out_gain[:, None, :], hgrn_out_gain[:, None, :],
                                  norm_mlp[:, None, :])
    lb, lb_jac = _lower_bounds(hgrn_lb_logits)
    lb3 = lb[:, None, :]

    shards = list(_pack_weights(w_in, w_out, w_up, w_down))
    w_pieces = _weight_pieces(*shards)
    w_groups = [[0], [1, 2, 3], [4, 5, 6, 7]]
    me = (4 * lax.axis_index("x") + 2 * lax.axis_index("y") + lax.axis_index("c")).astype(jnp.int32).reshape(1)
    lands = _exchange_own("all_gather_own", me, shards, w_pieces)
    w_sems, shards, lands, token = _exchange_start("all_gather_start", shards, lands, w_pieces, w_groups)

    def weights_ready(group, after):
        nonlocal shards
        idxs = w_groups[group]
        shards, got = _exchange_wait(f"all_gather_wait{group}", shards, [lands[i] for i in idxs], w_pieces,
                                     [(idxs, *w_sems[group])], after)
        return got

    def tied(small_arr, tok):
        return small_arr + tok[0, 0]

    saved = []
    xl = x0
    full = [None] * depth
    for l in range(depth):
        if l == 0:
            (full_in,) = weights_ready(0, token)
        else:
            full_in, full_out, full_up, full_down = weights_ready(2, xl)
        proj, h = _fwd_inproj(l, xl, g_mix, full_in, cos, sin)
        fw = [_attn_fwd(l, dil, proj) for dil in DILATIONS]
        o_hg, mixed, states = _hgrn_fwd(l, proj, lb3, g_hg)
        o_attn, lse, mixed = _attn_merge(l, [f[0] for f in fw], [f[1] for f in fw], g_attn, mixed)
        if l == 0:
            full_out, full_up, full_down = weights_ready(1, mixed)
        x_mid = _fwd_outproj(l, mixed, full_out, xl)
        x_next, u, h2 = _mlp_fwd(l, x_mid, g_mlp, full_up, full_down)
        saved.append((xl, proj, h, o_attn, lse, o_hg, states, mixed, x_mid, u, h2))
        full[l] = (full_in, full_out, full_up, full_down)
        xl = x_next
    dx, dxb, dnorm_final8, loss8 = _loss_head(xl, norm_final[None, :], target)
    loss = lax.psum(jnp.sum(loss8[:, 0]), ("x", "y", "c"))

    exchanges = []

    def scatter(tag, grads, kinds):
        pieces = _grad_pieces(grads, kinds)
        own = _exchange_own(f"reduce_scatter_own_{tag}", me, grads, pieces)
        sems, grads, own, tok = _exchange_start(f"reduce_scatter_start_{tag}", grads, own, pieces, [list(range(len(pieces)))])
        exchanges.append((grads, own, pieces, sems[0]))
        return tok

    small = {}
    for l in reversed(range(depth)):
        xl, proj, h, o_attn, lse, o_hg, states, mixed, x_mid, u, h2 = saved[l]
        full_in, full_out, full_up, full_down = full[l]
        hs = full_up.shape[3]
        gw_down = _mm_tn(f"grad_w_down_l{l}", u, dxb, u.shape[1], a_fn=_relu2)
        dx_mid, dx_mid_b, du, dmlp8 = _mlp_bwd(l, dx, dxb, x_mid, g_mlp, u, full_up, full_down)
        gw_up = _mm_tn(f"grad_w_up_l{l}", h2, du, d, out_block_w=hs)
        g_attn_t = tied(g_attn, scatter(f"mlp_l{l}", [gw_down, gw_up], ["rows", "up"]))
        dmixed = _bwd_outproj(l, dx_mid_b, full_out)
        gw_out = _mm_tn(f"grad_w_out_l{l}", mixed, dx_mid_b, mixed.shape[1])
        do, delta, dattn8 = _attn_norm_bwd(l, dmixed, o_attn, g_attn_t)
        parts = [_attn_bwd(l, dil, proj, do, lse, delta) for dil in DILATIONS]
        dqkv = _attn_combine(l, parts, cos, sin)
        dhg, dlb8, dhgain8 = _hgrn_bwd(l, proj, lb3, g_hg, o_hg, dmixed, states)
        gin = _mm_tn(f"grad_w_in_qkv_l{l}", dqkv, h, PROJ_W, a_lead=True)
        gw_in = _mm_tn(f"grad_w_in_hg_l{l}", dhg, h, PROJ_W, a_lead=True, out_block_off=3, prev=gin)
        g_mix_t = tied(g_mix, scatter(f"mix_l{l}", [gw_out, gw_in], ["rows", "rows"]))
        dx, dxb, dmix8 = _bwd_inproj(l, dqkv, dhg, full_in, xl, g_mix_t, dx_mid)
        small[l] = (dmix8, dattn8, dlb8, dhgain8, dmlp8)

    def fin(p8):
        return jnp.sum(p8, axis=0)
    dlogits = lb_jac * fin(small[1][2])[None, :]
    small_grads = [jnp.stack([fin(small[l][0]) for l in range(depth)]), jnp.stack([fin(small[l][1]) for l in range(depth)]),
                   dlogits, jnp.stack([fin(small[l][3]) for l in range(depth)]),
                   jnp.stack([fin(small[l][4]) for l in range(depth)]), fin(dnorm_final8)]
    small_w = [norm_mix, attn_out_gain, hgrn_lb_logits, hgrn_out_gain, norm_mlp, norm_final]
    small_m = [m_norm_mix, m_attn_out_gain, m_hgrn_lb_logits, m_hgrn_out_gain, m_norm_mlp, m_norm_final]
    small_v = [v_norm_mix, v_attn_out_gain, v_hgrn_lb_logits, v_hgrn_out_gain, v_norm_mlp, v_norm_final]
    g_small = _all_reduce_small(_pack_small(small_grads))
    d_small, m_small, v_small = _adam_small(g_small, _pack_small(small_w), _pack_small(small_m), _pack_small(small_v))
    gs, ds, ms, vs = (_unpack_small(t, small_w) for t in (g_small, d_small, m_small, v_small))

    all_grads, all_lands, all_pieces, waits = [], [], [], []
    for grads, own, pieces, (send, recv) in exchanges:
        first = len(all_pieces)
        all_pieces += [p._replace(src=p.src + len(all_grads)) for p in pieces]
        waits.append((list(range(first, first + len(pieces))), send, recv))
        all_grads += grads
        all_lands += own
    _, landed = _exchange_wait("reduce_scatter_wait", all_grads, all_lands, all_pieces, waits, d_small)
    r_down, r_up, r_out, r_in = ([landed[4 + i], landed[i]] for i in range(4))
    big = {
        "w_in": _adam_big("adam_w_in", r_in, w_in, m_w_in, v_w_in, True, 4),
        "w_out": _adam_big("adam_w_out", r_out, w_out, m_w_out, v_w_out, False, 1),
        "w_up": _adam_big("adam_w_up", r_up, w_up, m_w_up, v_w_up, False, 2),
        "w_down": _adam_big("adam_w_down", r_down, w_down, m_w_down, v_w_down, False, 4),
    }

    def gather(idx, small_list):
        by_name = dict(zip(SMALL_NAMES, small_list))
        return [by_name["norm_mix"], big["w_in"][idx], by_name["attn_out_gain"], by_name["hgrn_lb_logits"],
                by_name["hgrn_out_gain"], big["w_out"][idx], by_name["norm_mlp"], big["w_up"][idx], big["w_down"][idx],
                by_name["norm_final"]]

    return (loss, dx[None], *gather(0, gs), *gather(1, ds), *gather(2, ms), *gather(3, vs))
```

```python
import functools
from typing import Callable, NamedTuple

import jax
import jax.numpy as jnp
from jax import lax
from jax.experimental import pallas as pl
from jax.experimental.pallas import tpu as pltpu

F32 = jnp.float32
BF = jnp.bfloat16

N_DEV = 8
ATTN_W = 512
HGRN_W = 512
HGRN_HEADS = 4
HGRN_DIM = 128
SEG = 512
N_SEG = 7
PROJ_W = N_SEG * SEG
MIX_W = ATTN_W + HGRN_W
SPAN = 128
DILATIONS = (1, 4, 16)
HGRN_CHUNK = 16
ROPE_THETA = 10000.0
NORM_EPS = 1e-6
MASK_VALUE = -1e30
ATTN_SCALE = 0.125
HGRN_SCALE = HGRN_DIM ** -0.5
ADAM_LR = 0.001
ADAM_B1 = 0.9
ADAM_B2 = 0.999
ADAM_EPS = 1e-08
ADAM_WD = 0.01
ADAM_STEP = 10
LANES = 128
VMEM_LIMIT = 56 * 1024 * 1024

NN = ((1,), (0,))
NT = ((1,), (1,))
TN = ((0,), (0,))
MESH = pl.DeviceIdType.MESH


def _dot(a, b, dims):
    return lax.dot_general(a, b, (dims, ((), ())), preferred_element_type=F32)


def _cparams(sem):
    return pltpu.CompilerParams(dimension_semantics=sem, vmem_limit_bytes=VMEM_LIMIT)


def _part8(x):
    r, n = x.shape
    return jnp.sum(x.reshape(r // 8, 8, n), axis=0)


def _sigmoid(x):
    return 1.0 / (1.0 + jnp.exp(-x))


def _rms_fwd(x, gain):
    r = lax.rsqrt(jnp.mean(x * x, axis=-1, keepdims=True) + NORM_EPS)
    return x * r * gain


def _rms_bwd(dy, x, gain):
    r = lax.rsqrt(jnp.mean(x * x, axis=-1, keepdims=True) + NORM_EPS)
    xn = x * r
    dxn = dy * gain
    dx = r * (dxn - xn * jnp.mean(dxn * xn, axis=-1, keepdims=True))
    return dx, dy * xn


def _rope_partner(x):
    n = x.shape[-1]
    lane = lax.broadcasted_iota(jnp.int32, x.shape, x.ndim - 1)
    return jnp.where((lane % 64) < 32, pltpu.roll(x, n - 32, x.ndim - 1), pltpu.roll(x, 32, x.ndim - 1))


def _tile_lanes(t, reps):
    return jnp.concatenate([t] * reps, axis=-1)


def _mm_tn(name, a, b, out_rows, a_lead=False, out_block_off=0, prev=None, out_block_w=None, a_fn=None,
           tm=512, tn=1024, sub=512):
    kdim, n = b.shape
    m = a.shape[-1]
    tm, tn, sub = min(tm, m), min(tn, n), min(sub, kdim)
    mt = m // tm
    n_lead = a.shape[0] if a_lead else 1
    if a_lead:
        a_spec = pl.BlockSpec((None, kdim, tm), lambda j, i: (i // mt, 0, i % mt))
    else:
        a_spec = pl.BlockSpec((kdim, tm), lambda j, i: (0, i))
    b_spec = pl.BlockSpec((kdim, tn), lambda j, i: (0, j))
    if out_block_w:
        nb = tn // out_block_w
        o_shape = jax.ShapeDtypeStruct((n // out_block_w, out_rows, out_block_w), BF)
        o_spec = pl.BlockSpec((nb, tm, out_block_w), lambda j, i: (j, i + out_block_off, 0))
    else:
        nb = 0
        o_shape = jax.ShapeDtypeStruct((out_rows, n), BF)
        o_spec = pl.BlockSpec((tm, tn), lambda j, i: (i + out_block_off, j))
    arrays, specs, aliases = [a, b], [a_spec, b_spec], {}
    if prev is not None:
        arrays.append(prev)
        specs.append(pl.BlockSpec(memory_space=pl.ANY))
        aliases = {2: 0}

    def body(*refs):
        a_ref, b_ref, o_ref = refs[0], refs[1], refs[-1]
        acc = None
        for k in range(kdim // sub):
            av = a_ref[k * sub:(k + 1) * sub, :]
            if a_fn is not None:
                av = a_fn(av)
            part = _dot(av, b_ref[k * sub:(k + 1) * sub, :], TN)
            acc = part if acc is None else acc + part
        if nb:
            for t in range(nb):
                o_ref[t] = acc[:, t * out_block_w:(t + 1) * out_block_w].astype(BF)
        else:
            o_ref[...] = acc.astype(BF)

    return pl.pallas_call(
        body, grid=(n // tn, n_lead * mt), in_specs=specs, out_specs=o_spec, out_shape=o_shape,
        compiler_params=_cparams(("parallel", "parallel")), name=name, input_output_aliases=aliases,
    )(*arrays)


def _pack_weights(w_in, w_out, w_up, w_down):
    depth, d, cin = w_in.shape

    def body(win_ref, wout_ref, wup_ref, wdown_ref, oin_ref, oout_ref, oup_ref, odown_ref):
        oin_ref[...] = win_ref[...].T.astype(BF)
        oout_ref[...] = wout_ref[...].astype(BF)
        oup_ref[...] = wup_ref[...].astype(BF)
        odown_ref[...] = wdown_ref[...].astype(BF)

    def spec(a):
        return pl.BlockSpec((None,) + a.shape[1:], lambda l: (l, 0, 0))

    outs = [jax.ShapeDtypeStruct((depth, cin, d), BF), jax.ShapeDtypeStruct(w_out.shape, BF),
            jax.ShapeDtypeStruct(w_up.shape, BF), jax.ShapeDtypeStruct(w_down.shape, BF)]
    return pl.pallas_call(
        body, grid=(depth,), in_specs=[spec(w_in), spec(w_out), spec(w_up), spec(w_down)],
        out_specs=[pl.BlockSpec((None, cin, d), lambda l: (l, 0, 0)), spec(w_out), spec(w_up), spec(w_down)],
        out_shape=outs, compiler_params=_cparams(("arbitrary",)), name="pack_weights",
    )(w_in, w_out, w_up, w_down)


def _my_position():
    x, y, c = lax.axis_index("x"), lax.axis_index("y"), lax.axis_index("c")
    return x, y, c, 4 * x + 2 * y + c


def _peer(x, y, c, k):
    px = 1 - x if k & 4 else x
    py = 1 - y if k & 2 else y
    pc = 1 - c if k & 1 else c
    return (px, py, pc), 4 * px + 2 * py + pc


PEER_ORDER = (1, 2, 4, 3, 5, 6, 7)


class _Piece(NamedTuple):
    src: int
    send: Callable
    slot: Callable
    land_shape: tuple
    own_src: tuple
    own_slot: tuple


HBM_SPEC = pl.BlockSpec(memory_space=pltpu.HBM)
SEM_SPEC = pl.BlockSpec(memory_space=pltpu.SEMAPHORE)
ANY_SPEC = pl.BlockSpec(memory_space=pl.ANY)


def _in_hbm(arrays):
    return [pltpu.with_memory_space_constraint(a, pltpu.HBM) for a in arrays]


def _hbm_like(arrays):
    return [pltpu.HBM(a.shape, a.dtype) for a in arrays]


def _rows_of(rows):
    return lambda ref, dev: ref.at[pl.ds(pl.multiple_of(dev * rows, 16), rows), :]


def _exchange_own(name, me, srcs, pieces):
    n = len(pieces)

    def body(me_ref, *refs):
        for i in range(n):
            refs[n + i][...] = refs[i][...]

    def spec(block_and_index):
        block, index = block_and_index
        return pl.BlockSpec(block, lambda i, me_ref: index(me_ref[0]))

    return pl.pallas_call(
        body,
        grid_spec=pltpu.PrefetchScalarGridSpec(
            num_scalar_prefetch=1, grid=(1,), in_specs=[spec(p.own_src) for p in pieces],
            out_specs=[spec(p.own_slot) for p in pieces]),
        out_shape=[jax.ShapeDtypeStruct(p.land_shape, BF) for p in pieces],
        compiler_params=_cparams(("arbitrary",)), name=name,
    )(me, *[srcs[p.src] for p in pieces])


def _exchange_start(name, srcs, lands, pieces, groups):
    n_src, n, n_g = len(srcs), len(pieces), len(groups)

    def body(*refs):
        src_refs, land_refs = refs[:n_src], refs[n_src:n_src + n]
        sems, token = refs[n_src + n:n_src + n + 2 * n_g], refs[-1]
        x, y, c, me = _my_position()
        for g, idxs in enumerate(groups):
            for k in PEER_ORDER:
                peer, pid = _peer(x, y, c, k)
                for j, i in enumerate(idxs):
                    p = pieces[i]
                    pltpu.make_async_remote_copy(
                        src_ref=p.send(src_refs[p.src], pid), dst_ref=p.slot(land_refs[i], me),
                        send_sem=sems[2 * g].at[(k - 1) * len(idxs) + j], recv_sem=sems[2 * g + 1].at[(k - 1) * len(idxs) + j],
                        device_id=peer, device_id_type=MESH).start()
        token[...] = jnp.zeros(token.shape, F32)

    sem_shapes = [pltpu.SemaphoreType.DMA(((N_DEV - 1) * len(idxs),)) for idxs in groups for _ in range(2)]
    res = pl.pallas_call(
        body, in_specs=[HBM_SPEC] * (n_src + n),
        out_specs=[SEM_SPEC] * (2 * n_g) + [HBM_SPEC] * (n_src + n) + [pl.BlockSpec(memory_space=pltpu.VMEM)],
        out_shape=sem_shapes + _hbm_like(srcs) + _hbm_like(lands) + [jax.ShapeDtypeStruct((8, LANES), F32)],
        input_output_aliases={i: 2 * n_g + i for i in range(n_src + n)},
        compiler_params=pltpu.CompilerParams(has_side_effects=pltpu.SideEffectType.DATAFLOW_SIDE_EFFECTING),
        name=name,
    )(*_in_hbm(srcs), *_in_hbm(lands))
    sems = [(res[2 * g], res[2 * g + 1]) for g in range(n_g)]
    return sems, list(res[2 * n_g:2 * n_g + n_src]), list(res[2 * n_g + n_src:2 * n_g + n_src + n]), res[-1]


def _exchange_wait(name, srcs, lands, pieces, waits, after):
    n_src, n, n_g = len(srcs), len(lands), len(waits)

    def body(*refs):
        src_refs, land_refs = refs[:n_src], refs[n_src:n_src + n]
        sems = refs[n_src + n:n_src + n + 2 * n_g]
        x, y, c, me = _my_position()
        at = 0
        for g, (idxs, _, _) in enumerate(waits):
            for k in PEER_ORDER:
                peer, pid = _peer(x, y, c, k)
                for j, i in enumerate(idxs):
                    p = pieces[i]
                    cp = pltpu.make_async_remote_copy(
                        src_ref=p.send(src_refs[p.src], pid), dst_ref=p.slot(land_refs[at + j], pid),
                        send_sem=sems[2 * g].at[(k - 1) * len(idxs) + j], recv_sem=sems[2 * g + 1].at[(k - 1) * len(idxs) + j],
                        device_id=peer, device_id_type=MESH)
                    cp.wait_send()
                    cp.wait_recv()
            at += len(idxs)

    sem_args = [s for _, send, recv in waits for s in (send, recv)]
    res = pl.pallas_call(
        body, in_specs=[HBM_SPEC] * (n_src + n) + [SEM_SPEC] * (2 * n_g) + [ANY_SPEC],
        out_specs=[HBM_SPEC] * (n_src + n), out_shape=_hbm_like(srcs) + _hbm_like(lands),
        input_output_aliases={i: i for i in range(n_src + n)},
        compiler_params=pltpu.CompilerParams(has_side_effects=pltpu.SideEffectType.DATAFLOW_SIDE_EFFECTING),
        name=name,
    )(*srcs, *lands, *sem_args, after)
    return list(res[:n_src]), list(res[n_src:])


def _weight_pieces(p_in, p_out, p_up, p_down):
    depth, cin, d = p_in.shape
    rout, hs = p_out.shape[1], p_up.shape[2]
    pieces = []
    for l in range(depth):
        whole = functools.partial(lambda ref, dev, l: ref.at[l], l=l)
        layer = functools.partial(lambda dev, l: (l, 0, 0), l=l)

        def rows(src, n_rows, whole=whole, layer=layer):
            return _Piece(src, whole, lambda ref, dev: _rows_of(n_rows)(ref.at[0], dev), (1, N_DEV * n_rows, d),
                          ((None, n_rows, d), layer), ((None, n_rows, d), lambda dev: (0, dev, 0)))

        pieces += [
            rows(0, cin), rows(1, rout),
            _Piece(2, whole, lambda ref, dev: ref.at[0, dev], (1, N_DEV, d, hs),
                   ((None, d, hs), layer), ((None, None, d, hs), lambda dev: (0, dev, 0, 0))),
            rows(3, hs),
        ]
    return pieces


def _grad_pieces(g_pair, kinds):
    pieces = []
    for i, (g, kind) in enumerate(zip(g_pair, kinds)):
        lead = lambda dev: (dev, 0, 0)
        if kind == "up":
            blk = ((None,) + g.shape[1:], lead)
            pieces.append(_Piece(i, lambda ref, dev: ref.at[dev], lambda ref, dev: ref.at[dev], g.shape, blk, blk))
        else:
            rows, cols = g.shape[0] // N_DEV, g.shape[1]
            pieces.append(_Piece(i, _rows_of(rows), lambda ref, dev: ref.at[dev], (N_DEV, rows, cols),
                                 ((rows, cols), lambda dev: (dev, 0)), ((None, rows, cols), lead)))
    return pieces


def _all_reduce_small(vec):
    rows = vec.shape[0]

    def body(v_ref, o_ref, buf_ref, send_sems, recv_sems):
        x, y, c, me = _my_position()
        buf_ref[me] = v_ref[...]
        sends = []
        for k in PEER_ORDER:
            peer, _ = _peer(x, y, c, k)
            cp = pltpu.make_async_remote_copy(src_ref=v_ref, dst_ref=buf_ref.at[me], send_sem=send_sems.at[k - 1],
                                              recv_sem=recv_sems.at[k - 1], device_id=peer, device_id_type=MESH)
            cp.start()
            sends.append(cp)
        for k in PEER_ORDER:
            peer, pid = _peer(x, y, c, k)
            pltpu.make_async_remote_copy(src_ref=v_ref, dst_ref=buf_ref.at[pid], send_sem=send_sems.at[k - 1],
                                         recv_sem=recv_sems.at[k - 1], device_id=peer, device_id_type=MESH).wait_recv()
        for cp in sends:
            cp.wait_send()
        total = buf_ref[0]
        for dev in range(1, N_DEV):
            total = total + buf_ref[dev]
        o_ref[...] = total

    vm = pl.BlockSpec(memory_space=pltpu.VMEM)
    return pl.pallas_call(
        body, in_specs=[vm], out_specs=vm, out_shape=jax.ShapeDtypeStruct(vec.shape, F32),
        scratch_shapes=[pltpu.VMEM((N_DEV, rows, LANES), F32), pltpu.SemaphoreType.DMA((N_DEV - 1,)),
                        pltpu.SemaphoreType.DMA((N_DEV - 1,))],
        name="all_reduce_small",
    )(vec)


def _resident(block_shape, index_map):
    return pl.BlockSpec(block_shape, index_map, pipeline_mode=pl.Buffered(1))


def _fwd_inproj(layer, x, gain, g_in, cos, sin, tm=256):
    s, d = x.shape

    def body(x_ref, gain_ref, w_ref, cos_ref, sin_ref, proj_ref, h_ref):
        h = _rms_fwd(x_ref[...], gain_ref[...]).astype(BF)
        h_ref[...] = h
        cs = _tile_lanes(cos_ref[...], SEG // LANES)
        sn = _tile_lanes(sin_ref[...], SEG // LANES)
        for seg in range(N_SEG):
            acc = _dot(h, w_ref[seg * SEG:(seg + 1) * SEG, :], NT)
            if seg < 2:
                acc = acc * cs + _rope_partner(acc) * sn
            if seg == 0:
                acc = acc * ATTN_SCALE
            proj_ref[:, seg * SEG:(seg + 1) * SEG] = acc

    return pl.pallas_call(
        body, grid=(s // tm,),
        in_specs=[pl.BlockSpec((tm, d), lambda i: (i, 0)), pl.BlockSpec((None, 1, d), lambda i: (layer, 0, 0)),
                  _resident((None, PROJ_W, d), lambda i: (0, 0, 0)),
                  pl.BlockSpec((tm, LANES), lambda i: (i, 0)), pl.BlockSpec((tm, LANES), lambda i: (i, 0))],
        out_specs=[pl.BlockSpec((tm, PROJ_W), lambda i: (i, 0)), pl.BlockSpec((tm, d), lambda i: (i, 0))],
        out_shape=[jax.ShapeDtypeStruct((s, PROJ_W), F32), jax.ShapeDtypeStruct((s, d), BF)],
        compiler_params=_cparams(("parallel",)), name=f"fwd_inproj_l{layer}",
    )(x, gain, g_in, cos, sin)


ATTN_BLOCKS = {1: 4, 4: 1, 16: 1}
ATTN_GROUP = 4


def _attn_masks(first_block_has_prev):
    row = lax.broadcasted_iota(jnp.int32, (SPAN, 2 * SPAN), 0)
    col = lax.broadcasted_iota(jnp.int32, (SPAN, 2 * SPAN), 1)
    band = (col >= row) & (col <= row + SPAN)
    lane = lax.broadcasted_iota(jnp.int32, (SPAN, LANES), 1)
    return band & ((col >= SPAN) | first_block_has_prev), band, lane < 64


def _attn_specs(dil, n_in_extra):
    blk = SPAN * dil
    nb = ATTN_BLOCKS[dil]
    pairs = ATTN_W // LANES
    q_spec = pl.BlockSpec((nb * blk, LANES), lambda p, n: (n, p))

    def prev(seg):
        return pl.BlockSpec((blk, LANES), lambda p, n: (jnp.maximum(n * nb - 1, 0), seg * pairs + p))

    def cur(seg):
        return pl.BlockSpec((nb * blk, LANES), lambda p, n: (n, seg * pairs + p))

    return [q_spec, prev(1), cur(1), prev(2), cur(2)] + [q_spec] * n_in_extra


def _strided(ref, dil, r, n_blocks):
    return ref[pl.ds(r, n_blocks * SPAN, stride=dil), :] if dil > 1 else ref[...]


def _attn_fwd(layer, dil, proj):
    s = proj.shape[0]
    nb = ATTN_BLOCKS[dil]
    unit = nb * SPAN * dil

    def body(q_ref, kp_ref, kc_ref, vp_ref, vc_ref, o_ref, lse_ref):
        m_first, m_rest, is_a = _attn_masks(pl.program_id(1) > 0)
        sels = (is_a, jnp.logical_not(is_a))
        ones = jnp.ones((2 * SPAN, LANES), BF)
        for r0 in range(0, dil, ATTN_GROUP):
            group = range(r0, min(r0 + ATTN_GROUP, dil))
            q_all = {r: _strided(q_ref, dil, r, nb) for r in group}
            k_all = {r: jnp.concatenate([_strided(kp_ref, dil, r, 1), _strided(kc_ref, dil, r, nb)], axis=0).astype(BF)
                     for r in group}
            v_all = {r: jnp.concatenate([_strided(vp_ref, dil, r, 1), _strided(vc_ref, dil, r, nb)], axis=0).astype(BF)
                     for r in group}
            items = [(r, b, h) for r in group for b in range(nb) for h in range(2)]
            sc = [jnp.where(m_first if b == 0 else m_rest,
                            _dot(jnp.where(sels[h], q_all[r][b * SPAN:(b + 1) * SPAN], 0.0).astype(BF),
                                 k_all[r][b * SPAN:(b + 2) * SPAN], NT), MASK_VALUE) for r, b, h in items]
            mx = [jnp.max(jnp.maximum(t[:, :SPAN], t[:, SPAN:]), axis=-1, keepdims=True) for t in sc]
            p = [jnp.exp(t - m).astype(BF) for t, m in zip(sc, mx)]
            den = [_dot(t, ones, NN) for t in p]
            num = [_dot(t, v_all[r][b * SPAN:(b + 2) * SPAN], NN) for t, (r, b, h) in zip(p, items)]
            o = [a / d for a, d in zip(num, den)]
            lse = [m + jnp.log(d) for m, d in zip(mx, den)]
            for r in group:
                at = [items.index((r, b, 0)) for b in range(nb)]
                o_val = jnp.concatenate([jnp.where(is_a, o[i], o[i + 1]) for i in at], axis=0)
                lse_val = jnp.concatenate([jnp.where(is_a, lse[i], lse[i + 1]) for i in at], axis=0)
                if dil > 1:
                    o_ref[pl.ds(r, nb * SPAN, stride=dil), :] = o_val
                    lse_ref[pl.ds(r, nb * SPAN, stride=dil), :] = lse_val
                else:
                    o_ref[...] = o_val
                    lse_ref[...] = lse_val

    out_spec = pl.BlockSpec((unit, LANES), lambda p, n: (n, p))
    return pl.pallas_call(
        body, grid=(ATTN_W // LANES, s // unit), in_specs=_attn_specs(dil, 0), out_specs=[out_spec, out_spec],
        out_shape=[jax.ShapeDtypeStruct((s, ATTN_W), F32)] * 2,
        compiler_params=_cparams(("parallel", "arbitrary")), name=f"attn_fwd_d{dil}_l{layer}",
    )(proj, proj, proj, proj, proj)


def _attn_merge(layer, outs, lses, gain, mixed, tm=512):
    s = outs[0].shape[0]

    def body(o1, o2, o3, l1, l2, l3, gain_ref, mixed_ref, o_ref, lse_ref, n_ref):
        ls = (l1[...], l2[...], l3[...])
        mx = jnp.maximum(jnp.maximum(ls[0], ls[1]), ls[2])
        ws = [jnp.exp(l - mx) for l in ls]
        den = ws[0] + ws[1] + ws[2]
        o = (ws[0] * o1[...] + ws[1] * o2[...] + ws[2] * o3[...]) / den
        o_ref[...] = o
        lse_ref[...] = mx + jnp.log(den)
        n_ref[...] = _rms_fwd(o, gain_ref[...]).astype(BF)

    blk = pl.BlockSpec((tm, ATTN_W), lambda i: (i, 0))
    return pl.pallas_call(
        body, grid=(s // tm,),
        in_specs=[blk] * 6 + [pl.BlockSpec((None, 1, ATTN_W), lambda i: (layer, 0, 0)), pl.BlockSpec(memory_space=pl.ANY)],
        out_specs=[blk, blk, blk],
        out_shape=[jax.ShapeDtypeStruct((s, ATTN_W), F32), jax.ShapeDtypeStruct((s, ATTN_W), F32),
                   jax.ShapeDtypeStruct(mixed.shape, BF)],
        input_output_aliases={7: 2},
        compiler_params=_cparams(("parallel",)), name=f"attn_merge_l{layer}",
    )(*outs, *lses, gain, mixed)


def _chunk_cumsum(x, reverse=False):
    n = x.shape[0]
    pos = lax.broadcasted_iota(jnp.int32, x.shape, 0) % HGRN_CHUNK
    for sh in (1, 2, 4, 8):
        if reverse:
            x = x + jnp.where(pos < HGRN_CHUNK - sh, pltpu.roll(x, n - sh, 0), 0.0)
        else:
            x = x + jnp.where(pos >= sh, pltpu.roll(x, sh, 0), 0.0)
    return x


def _chunk_row(x, row):
    r, n = x.shape
    x3 = x.reshape(r // HGRN_CHUNK, HGRN_CHUNK, n)
    return jnp.broadcast_to(x3[:, row:row + 1, :], x3.shape).reshape(r, n)


def _hgrn_pre(qh, z, lb):
    sig = _sigmoid(z)
    f = lb + (1.0 - lb) * sig
    k = (1.0 - lb) * _sigmoid(-z)
    sq = _sigmoid(qh)
    q = qh * sq * HGRN_SCALE
    g = _chunk_cumsum(jnp.log(f))
    g_mid = _chunk_row(g, HGRN_CHUNK // 2 - 1)
    g_last = _chunk_row(g, HGRN_CHUNK - 1)
    e_q, e_k = jnp.exp(g - g_mid), jnp.exp(g_mid - g)
    e_in, e_out = jnp.exp(g), jnp.exp(g_last - g)
    return dict(sig=sig, f=f, k=k, sq=sq, q=q, g_last=g_last, e_q=e_q, e_k=e_k, e_in=e_in, e_out=e_out,
                qt=q * e_q, kt=k * e_k, qg=q * e_in, kout=k * e_out)


def _hgrn_mask():
    row = lax.broadcasted_iota(jnp.int32, (LANES, LANES), 0)
    col = lax.broadcasted_iota(jnp.int32, (LANES, LANES), 1)
    return (row // HGRN_CHUNK == col // HGRN_CHUNK) & (col <= row)


def _hgrn_in_specs(layer, rev, nblk):
    def blk(b):
        return nblk - 1 - b if rev else b
    first = 3 * ATTN_W // HGRN_W
    specs = [pl.BlockSpec((LANES, HGRN_W), functools.partial(lambda b, seg: (blk(b), first + seg), seg=seg))
             for seg in range(4)]
    specs.append(pl.BlockSpec((None, 1, HGRN_W), lambda b: (layer, 0, 0)))
    specs.append(pl.BlockSpec((None, 1, HGRN_DIM), lambda b: (layer, 0, 0)))
    return specs, blk


def _head(x, h):
    return x[:, h * HGRN_DIM:(h + 1) * HGRN_DIM]


def _chunk(x, c):
    return x[c * HGRN_CHUNK:(c + 1) * HGRN_CHUNK]


HEADS = range(HGRN_HEADS)
CHUNKS = range(LANES // HGRN_CHUNK)


def _hgrn_fwd(layer, proj, lb, gain):
    s = proj.shape[0]
    nblk = s // LANES
    cpb = len(CHUNKS)

    def body(q_ref, f_ref, i_ref, g_ref, lb_ref, gain_ref, o_ref, rec_ref, st_ref, state):
        @pl.when(pl.program_id(0) == 0)
        def _():
            state[...] = jnp.zeros(state.shape, F32)

        pre = _hgrn_pre(q_ref[...], f_ref[...], lb_ref[...])
        v = i_ref[...].astype(BF)
        qt, kt, qg, kout = (pre[n].astype(BF) for n in ("qt", "kt", "qg", "kout"))
        dec = jnp.exp(pre["g_last"])
        mask = _hgrn_mask()
        a = [jnp.where(mask, _dot(_head(qt, h), _head(kt, h), NT), 0.0).astype(BF) for h in HEADS]
        o_intra = [_dot(a[h], _head(v, h), NN) for h in HEADS]
        update = [[_dot(_chunk(_head(v, h), c), _chunk(_head(kout, h), c), TN) for c in CHUNKS] for h in HEADS]
        for h in HEADS:
            st = state[h]
            for c in CHUNKS:
                st_ref[h, c * LANES:(c + 1) * LANES, :] = st.astype(BF)
                st = st * _head(dec, h)[c * HGRN_CHUNK:c * HGRN_CHUNK + 1, :] + update[h][c]
            state[h] = st
        inter = [[_dot(_chunk(_head(qg, h), c), st_ref[h, c * LANES:(c + 1) * LANES, :].astype(BF), NT) for c in CHUNKS]
                 for h in HEADS]
        o = [o_intra[h] + jnp.concatenate(inter[h], axis=0) for h in HEADS]
        o_ref[...] = jnp.concatenate(o, axis=1)
        gate = g_ref[...]
        normed = jnp.concatenate([_rms_fwd(o[h], gain_ref[...]) for h in HEADS], axis=1)
        rec_ref[...] = (normed * (gate * _sigmoid(gate))).astype(BF)

    specs, _ = _hgrn_in_specs(layer, False, nblk)
    return pl.pallas_call(
        body, grid=(nblk,), in_specs=specs,
        out_specs=[pl.BlockSpec((LANES, HGRN_W), lambda b: (b, 0)), pl.BlockSpec((LANES, HGRN_W), lambda b: (b, 1)),
                   pl.BlockSpec((HGRN_HEADS, cpb * LANES, LANES), lambda b: (0, b, 0))],
        out_shape=[jax.ShapeDtypeStruct((s, HGRN_W), F32), jax.ShapeDtypeStruct((s, MIX_W), BF),
                   jax.ShapeDtypeStruct((HGRN_HEADS, nblk * cpb * LANES, LANES), BF)],
        scratch_shapes=[pltpu.VMEM((HGRN_HEADS, LANES, LANES), F32)],
        compiler_params=_cparams(("arbitrary",)), name=f"hgrn_fwd_l{layer}",
    )(proj, proj, proj, proj, lb, gain)


def _fwd_outproj(layer, mixed, g_out, x, tm=512):
    s, d = x.shape
    mw = mixed.shape[1]

    def body(m_ref, w_ref, x_ref, o_ref):
        o_ref[...] = x_ref[...] + _dot(m_ref[...], w_ref[...], NN)

    row = pl.BlockSpec((tm, d), lambda i: (i, 0))
    return pl.pallas_call(
        body, grid=(s // tm,),
        in_specs=[pl.BlockSpec((tm, mw), lambda i: (i, 0)), _resident((None, mw, d), lambda i: (0, 0, 0)), row],
        out_specs=row, out_shape=jax.ShapeDtypeStruct((s, d), F32),
        compiler_params=_cparams(("parallel",)), name=f"fwd_outproj_l{layer}",
    )(mixed, g_out, x)


def _relu2(u):
    return jnp.square(jnp.maximum(u, 0)).astype(BF)


def _mlp_fwd(layer, x, gain, g_up, g_down, tm=256):
    s, d = x.shape
    nblk, hs = g_up.shape[1], g_up.shape[3]

    def body(x_ref, gain_ref, up_ref, down_ref, o_ref, u_ref, h_ref, a_buf):
        xv = x_ref[...]
        h = _rms_fwd(xv, gain_ref[...]).astype(BF)
        h_ref[...] = h
        for j in range(nblk):
            u = _dot(h, up_ref[j], NN)
            u_ref[:, j * hs:(j + 1) * hs] = u.astype(BF)
            a_buf[:, j * hs:(j + 1) * hs] = _relu2(u)
        acc = xv
        for j in range(nblk):
            acc = acc + _dot(a_buf[:, j * hs:(j + 1) * hs], down_ref[j * hs:(j + 1) * hs, :], NN)
        o_ref[...] = acc

    row = pl.BlockSpec((tm, d), lambda i: (i, 0))
    return pl.pallas_call(
        body, grid=(s // tm,),
        in_specs=[row, pl.BlockSpec((None, 1, d), lambda i: (layer, 0, 0)),
                  _resident((None, nblk, d, hs), lambda i: (0, 0, 0, 0)),
                  _resident((None, nblk * hs, d), lambda i: (0, 0, 0))],
        out_specs=[row, pl.BlockSpec((tm, nblk * hs), lambda i: (i, 0)), row],
        out_shape=[jax.ShapeDtypeStruct((s, d), F32), jax.ShapeDtypeStruct((s, nblk * hs), BF),
                   jax.ShapeDtypeStruct((s, d), BF)],
        scratch_shapes=[pltpu.VMEM((tm, nblk * hs), BF)],
        compiler_params=_cparams(("parallel",)), name=f"mlp_fwd_l{layer}",
    )(x, gain, g_up, g_down)


def _loss_head(x, gain, target, tm=512):
    s, d = x.shape

    def body(x_ref, gain_ref, t_ref, dx_ref, dxb_ref, dgain_ref, loss_ref):
        i = pl.program_id(0)
        xv, gv = x_ref[...], gain_ref[...]
        err = _rms_fwd(xv, gv) - t_ref[...]
        dx, dgain = _rms_bwd(err * (1.0 / d), xv, gv)
        dx_ref[...] = dx
        dxb_ref[...] = dx.astype(BF)
        part = _part8(dgain)
        lpart = _part8(0.5 * jnp.mean(err * err, axis=-1, keepdims=True) * jnp.ones((1, LANES), F32))

        @pl.when(i == 0)
        def _():
            dgain_ref[...] = part
            loss_ref[...] = lpart

        @pl.when(i > 0)
        def _():
            dgain_ref[...] += part
            loss_ref[...] += lpart

    row = pl.BlockSpec((tm, d), lambda i: (i, 0))
    return pl.pallas_call(
        body, grid=(s // tm,),
        in_specs=[row, pl.BlockSpec((1, d), lambda i: (0, 0)), row],
        out_specs=[row, row, pl.BlockSpec((8, d), lambda i: (0, 0)), pl.BlockSpec((8, LANES), lambda i: (0, 0))],
        out_shape=[jax.ShapeDtypeStruct((s, d), F32), jax.ShapeDtypeStruct((s, d), BF), jax.ShapeDtypeStruct((8, d), F32),
                   jax.ShapeDtypeStruct((8, LANES), F32)],
        compiler_params=_cparams(("arbitrary",)), name="loss_head",
    )(x, gain, target)


def _accumulate_rows(i, ref, part):
    @pl.when(i == 0)
    def _():
        ref[...] = part

    @pl.when(i > 0)
    def _():
        ref[...] += part


def _mlp_bwd(layer, dx, dxb, x, gain, u, g_up, g_down, tm=256):
    s, d = x.shape
    nblk, hs = g_up.shape[1], g_up.shape[3]

    def body(dx_ref, dxb_ref, x_ref, gain_ref, u_ref, up_ref, down_ref, o_ref, ob_ref, du_ref, dgain_ref):
        dxb_v = dxb_ref[...]
        for j in range(nblk):
            cols = slice(j * hs, (j + 1) * hs)
            da = _dot(dxb_v, down_ref[cols, :], NT)
            du_ref[:, cols] = (da * (2.0 * jnp.maximum(u_ref[:, cols].astype(F32), 0.0))).astype(BF)
        acc = jnp.zeros((tm, d), F32)
        for j in range(nblk):
            acc = acc + _dot(du_ref[:, j * hs:(j + 1) * hs], up_ref[j], NT)
        dxn, dgain = _rms_bwd(acc, x_ref[...], gain_ref[...])
        out = dx_ref[...] + dxn
        o_ref[...] = out
        ob_ref[...] = out.astype(BF)
        _accumulate_rows(pl.program_id(0), dgain_ref, _part8(dgain))

    row = pl.BlockSpec((tm, d), lambda i: (i, 0))
    wide = pl.BlockSpec((tm, nblk * hs), lambda i: (i, 0))
    return pl.pallas_call(
        body, grid=(s // tm,),
        in_specs=[row, row, row, pl.BlockSpec((None, 1, d), lambda i: (layer, 0, 0)), wide,
                  _resident((None, nblk, d, hs), lambda i: (0, 0, 0, 0)),
                  _resident((None, nblk * hs, d), lambda i: (0, 0, 0))],
        out_specs=[row, row, wide, pl.BlockSpec((8, d), lambda i: (0, 0))],
        out_shape=[jax.ShapeDtypeStruct((s, d), F32), jax.ShapeDtypeStruct((s, d), BF),
                   jax.ShapeDtypeStruct((s, nblk * hs), BF), jax.ShapeDtypeStruct((8, d), F32)],
        compiler_params=_cparams(("arbitrary",)), name=f"mlp_bwd_l{layer}",
    )(dx, dxb, x, gain, u, g_up, g_down)


def _bwd_outproj(layer, dxb, g_out, tm=512):
    s, d = dxb.shape
    mw = g_out.shape[1]

    def body(dx_ref, w_ref, o_ref):
        o_ref[...] = _dot(dx_ref[...], w_ref[...], NT)

    return pl.pallas_call(
        body, grid=(s // tm,),
        in_specs=[pl.BlockSpec((tm, d), lambda i: (i, 0)), _resident((None, mw, d), lambda i: (0, 0, 0))],
        out_specs=pl.BlockSpec((tm, mw), lambda i: (i, 0)), out_shape=jax.ShapeDtypeStruct((s, mw), F32),
        compiler_params=_cparams(("parallel",)), name=f"bwd_outproj_l{layer}",
    )(dxb, g_out)


def _attn_norm_bwd(layer, dmixed, o, gain, tm=512):
    s = o.shape[0]

    def body(dm_ref, o_ref, gain_ref, do_ref, delta_ref, dgain_ref):
        i = pl.program_id(0)
        ov = o_ref[...]
        do, dgain = _rms_bwd(dm_ref[...], ov, gain_ref[...])
        do_ref[...] = do
        row = lax.broadcasted_iota(jnp.int32, (ATTN_W, ATTN_W), 0)
        col = lax.broadcasted_iota(jnp.int32, (ATTN_W, ATTN_W), 1)
        same_head = jnp.where(row // 64 == col // 64, 1.0, 0.0)
        delta_ref[...] = jnp.dot(do * ov, same_head, precision=lax.Precision.HIGHEST, preferred_element_type=F32)
        part = _part8(dgain)

        @pl.when(i == 0)
        def _():
            dgain_ref[...] = part

        @pl.when(i > 0)
        def _():
            dgain_ref[...] += part

    blk = pl.BlockSpec((tm, ATTN_W), lambda i: (i, 0))
    return pl.pallas_call(
        body, grid=(s // tm,), in_specs=[blk, blk, pl.BlockSpec((None, 1, ATTN_W), lambda i: (layer, 0, 0))],
        out_specs=[blk, blk, pl.BlockSpec((8, ATTN_W), lambda i: (0, 0))],
        out_shape=[jax.ShapeDtypeStruct((s, ATTN_W), F32), jax.ShapeDtypeStruct((s, ATTN_W), F32),
                   jax.ShapeDtypeStruct((8, ATTN_W), F32)],
        compiler_params=_cparams(("arbitrary",)), name=f"attn_norm_bwd_l{layer}",
    )(dmixed, o, gain)


def _attn_bwd(layer, dil, proj, do, lse, delta):
    s = proj.shape[0]
    nb = ATTN_BLOCKS[dil]
    blk = SPAN * dil
    unit = nb * blk

    def body(q_ref, kp_ref, kc_ref, vp_ref, vc_ref, do_ref, lse_ref, delta_ref, dq_ref, dkc_ref, dkp_ref, dvc_ref, dvp_ref):
        m_first, m_rest, is_a = _attn_masks(pl.program_id(1) > 0)
        is_a_keys = lax.broadcasted_iota(jnp.int32, (2 * SPAN, LANES), 1) < 64
        sels = (is_a, jnp.logical_not(is_a))
        key_sels = (is_a_keys, jnp.logical_not(is_a_keys))

        def blk(x, b, n=1):
            return x[b * SPAN:(b + n) * SPAN]

        for r0 in range(0, dil, ATTN_GROUP):
            group = range(r0, min(r0 + ATTN_GROUP, dil))
            q_all = {r: _strided(q_ref, dil, r, nb) for r in group}
            do_all = {r: _strided(do_ref, dil, r, nb) for r in group}
            lse_all = {r: _strided(lse_ref, dil, r, nb) for r in group}
            delta_all = {r: _strided(delta_ref, dil, r, nb) for r in group}
            k_all = {r: jnp.concatenate([_strided(kp_ref, dil, r, 1), _strided(kc_ref, dil, r, nb)], axis=0) for r in group}
            v_all = {r: jnp.concatenate([_strided(vp_ref, dil, r, 1), _strided(vc_ref, dil, r, nb)], axis=0).astype(BF)
                     for r in group}
            items = [(r, b, h) for r in group for b in range(nb) for h in range(2)]
            qh = [jnp.where(sels[h], blk(q_all[r], b), 0.0).astype(BF) for r, b, h in items]
            doh = [jnp.where(sels[h], blk(do_all[r], b), 0.0).astype(BF) for r, b, h in items]
            kh = [jnp.where(key_sels[h], blk(k_all[r], b, 2), 0.0).astype(BF) for r, b, h in items]
            sc = [jnp.where(m_first if b == 0 else m_rest, _dot(qh[i], kh[i], NT), MASK_VALUE)
                  for i, (r, b, h) in enumerate(items)]
            dp = [_dot(doh[i], blk(v_all[r], b, 2), NT) for i, (r, b, h) in enumerate(items)]
            p = [jnp.exp(sc[i] - blk(lse_all[r], b)[:, 64 * h:64 * h + 1]) for i, (r, b, h) in enumerate(items)]
            ds = [(p[i] * (dp[i] - blk(delta_all[r], b)[:, 64 * h:64 * h + 1])).astype(BF)
                  for i, (r, b, h) in enumerate(items)]
            dv = [_dot(p[i].astype(BF), doh[i], TN) for i in range(len(items))]
            dq = [_dot(ds[i], kh[i], NN) for i in range(len(items))]
            dk = [_dot(ds[i], qh[i], TN) for i in range(len(items))]
            for r in group:
                at = [items.index((r, b, 0)) for b in range(nb)]
                dq_blocks = [dq[i] + dq[i + 1] for i in at]
                dk_blocks = [dk[i] + dk[i + 1] for i in at]
                dv_blocks = [dv[i] + dv[i + 1] for i in at]
                dk_own = [blk(dk_blocks[b], 1) + (blk(dk_blocks[b + 1], 0) if b + 1 < nb else 0.0) for b in range(nb)]
                dv_own = [blk(dv_blocks[b], 1) + (blk(dv_blocks[b + 1], 0) if b + 1 < nb else 0.0) for b in range(nb)]
                own = ((dq_ref, dq_blocks), (dkc_ref, dk_own), (dvc_ref, dv_own))
                if dil > 1:
                    for ref, blocks in own:
                        ref[pl.ds(r, nb * SPAN, stride=dil), :] = jnp.concatenate(blocks, axis=0)
                    dkp_ref[pl.ds(r, SPAN, stride=dil), :] = blk(dk_blocks[0], 0)
                    dvp_ref[pl.ds(r, SPAN, stride=dil), :] = blk(dv_blocks[0], 0)
                else:
                    for ref, blocks in own:
                        ref[...] = jnp.concatenate(blocks, axis=0)
                    dkp_ref[...] = blk(dk_blocks[0], 0)
                    dvp_ref[...] = blk(dv_blocks[0], 0)

    own_spec = pl.BlockSpec((unit, LANES), lambda p, n: (n, p))
    prev_spec = pl.BlockSpec((blk, LANES), lambda p, n: (n, p))
    full, compact = jax.ShapeDtypeStruct((s, ATTN_W), F32), jax.ShapeDtypeStruct((s // nb, ATTN_W), F32)
    return pl.pallas_call(
        body, grid=(ATTN_W // LANES, s // unit), in_specs=_attn_specs(dil, 3),
        out_specs=[own_spec, own_spec, prev_spec, own_spec, prev_spec], out_shape=[full, full, compact, full, compact],
        compiler_params=_cparams(("parallel", "arbitrary")), name=f"attn_bwd_d{dil}_l{layer}",
    )(proj, proj, proj, proj, proj, do, lse, delta)


def _attn_combine(layer, parts, cos, sin):
    s = parts[0][0].shape[0]
    nblk = s // SPAN
    arrays, specs, gates = [], [], []
    for dil, (dq, dkc, dkp, dvc, dvp) in zip(DILATIONS, parts):
        nb = ATTN_BLOCKS[dil]
        n_units = s // (nb * SPAN * dil)

        def sender(i, dil=dil, nb=nb):
            nxt = (i // dil + 1) // nb
            return nxt, nxt * dil + i % dil

        here = pl.BlockSpec((SPAN, ATTN_W), lambda i: (i, 0))
        ahead = pl.BlockSpec((SPAN, ATTN_W), functools.partial(
            lambda i, sender, last: (jnp.minimum(sender(i)[1], last), 0), sender=sender, last=n_units * dil - 1))
        gates.append(functools.partial(
            lambda i, sender, dil, nb, n_units: ((i // dil + 1) % nb == 0) & (sender(i)[0] < n_units),
            sender=sender, dil=dil, nb=nb, n_units=n_units))
        arrays += [dq, dkc, dkp, dvc, dvp]
        specs += [here, here, ahead, here, ahead]
    tab = pl.BlockSpec((SPAN, LANES), lambda i: (i, 0))

    def body(*refs):
        cos_ref, sin_ref, out_ref = refs[15], refs[16], refs[17]
        i = pl.program_id(0)
        dq = dk = dv = jnp.zeros((SPAN, ATTN_W), F32)
        for p in range(len(DILATIONS)):
            dq_r, dkc_r, dkp_r, dvc_r, dvp_r = refs[5 * p:5 * p + 5]
            has_next = gates[p](i)
            dq = dq + dq_r[...]
            dk = dk + dkc_r[...] + jnp.where(has_next, dkp_r[...], 0.0)
            dv = dv + dvc_r[...] + jnp.where(has_next, dvp_r[...], 0.0)
        cs = _tile_lanes(cos_ref[...], ATTN_W // LANES)
        sn = _tile_lanes(sin_ref[...], ATTN_W // LANES)
        out_ref[0] = ((dq * cs - _rope_partner(dq) * sn) * ATTN_SCALE).astype(BF)
        out_ref[1] = (dk * cs - _rope_partner(dk) * sn).astype(BF)
        out_ref[2] = dv.astype(BF)

    return pl.pallas_call(
        body, grid=(nblk,), in_specs=specs + [tab, tab],
        out_specs=pl.BlockSpec((3, SPAN, ATTN_W), lambda i: (0, i, 0)),
        out_shape=jax.ShapeDtypeStruct((3, s, ATTN_W), BF),
        compiler_params=_cparams(("parallel",)), name=f"attn_combine_l{layer}",
    )(*arrays, cos, sin)


def _hgrn_bwd(layer, proj, lb, gain, o, dmixed, states):
    s = proj.shape[0]
    nblk = s // LANES
    cpb = len(CHUNKS)

    def body(q_ref, f_ref, i_ref, g_ref, lb_ref, gain_ref, o_ref, drec_ref, st_ref, dseg_ref, dlb_ref, dgain_ref,
             dstate, dst_buf):
        step = pl.program_id(0)

        @pl.when(step == 0)
        def _():
            dstate[...] = jnp.zeros(dstate.shape, F32)

        lbv, gv = lb_ref[...], gain_ref[...]
        qh, z, gate_in = q_ref[...], f_ref[...], g_ref[...]
        pre = _hgrn_pre(qh, z, lbv)
        v = i_ref[...].astype(BF)
        sg = _sigmoid(gate_in)
        ov, drec = o_ref[...], drec_ref[...]
        dnormed = drec * (gate_in * sg)
        back = [_rms_bwd(_head(dnormed, h), _head(ov, h), gv) for h in HEADS]
        do_b = jnp.concatenate([b[0] for b in back], axis=1).astype(BF)
        dgain = back[0][1] + back[1][1] + back[2][1] + back[3][1]
        normed = jnp.concatenate([_rms_fwd(_head(ov, h), gv) for h in HEADS], axis=1)
        dgate_in = drec * normed * (sg * (1.0 + gate_in * (1.0 - sg)))
        mask = _hgrn_mask()
        qt, kt, qg, kout = (pre[n].astype(BF) for n in ("qt", "kt", "qg", "kout"))
        dec = jnp.exp(pre["g_last"])
        a = [jnp.where(mask, _dot(_head(qt, h), _head(kt, h), NT), 0.0).astype(BF) for h in HEADS]
        da = [jnp.where(mask, _dot(_head(do_b, h), _head(v, h), NT), 0.0).astype(BF) for h in HEADS]
        dv_intra = [_dot(a[h], _head(do_b, h), TN) for h in HEADS]
        dqt = jnp.concatenate([_dot(da[h], _head(kt, h), NN) for h in HEADS], axis=1)
        dkt = jnp.concatenate([_dot(da[h], _head(qt, h), TN) for h in HEADS], axis=1)
        feed = [[_dot(_chunk(_head(do_b, h), c), _chunk(_head(qg, h), c), TN) for c in CHUNKS] for h in HEADS]
        for h in HEADS:
            dst = dstate[h]
            for c in reversed(CHUNKS):
                dst_buf[h, c * LANES:(c + 1) * LANES, :] = dst
                dst = dst * _head(dec, h)[c * HGRN_CHUNK:c * HGRN_CHUNK + 1, :] + feed[h][c]
            dstate[h] = dst

        def per_chunk(fn):
            cols = []
            for h in HEADS:
                rows = [jnp.broadcast_to(t, (HGRN_CHUNK, HGRN_DIM)) for t in (fn(h, c) for c in CHUNKS)]
                cols.append(jnp.concatenate(rows, axis=0))
            return jnp.concatenate(cols, axis=1)

        def st_prev(h, c):
            return st_ref[h, c * LANES:(c + 1) * LANES, :]

        def dst_at(h, c):
            return dst_buf[h, c * LANES:(c + 1) * LANES, :]

        dqg = per_chunk(lambda h, c: _dot(_chunk(_head(do_b, h), c), st_prev(h, c).astype(BF), NN))
        dkout = per_chunk(lambda h, c: _dot(_chunk(_head(v, h), c), dst_at(h, c).astype(BF), NN))
        dv_inter = per_chunk(lambda h, c: _dot(_chunk(_head(kout, h), c), dst_at(h, c).astype(BF), NT))
        dg_state = per_chunk(lambda h, c: jnp.sum(dst_at(h, c) * st_prev(h, c).astype(F32), axis=0, keepdims=True))
        dg_kout = per_chunk(lambda h, c: jnp.sum(_chunk(_head(dkout * pre["kout"], h), c), axis=0, keepdims=True))
        dv = jnp.concatenate(dv_intra, axis=1) + dv_inter
        pos = lax.broadcasted_iota(jnp.int32, (LANES, HGRN_W), 0) % HGRN_CHUNK
        dq = dqt * pre["e_q"] + dqg * pre["e_in"]
        dk = dkt * pre["e_k"] + dkout * pre["e_out"]
        dg = (dqt * pre["qt"] - dkt * pre["kt"] + dqg * pre["qg"] - dkout * pre["kout"]
              + jnp.where(pos == HGRN_CHUNK - 1, dg_state * dec + dg_kout, 0.0))
        dlogf = _chunk_cumsum(dg, reverse=True)
        sig, sq = pre["sig"], pre["sq"]
        df = dlogf / pre["f"] - dk
        dseg_ref[0] = (dq * HGRN_SCALE * (sq * (1.0 + qh * (1.0 - sq)))).astype(BF)
        dseg_ref[1] = (df * (1.0 - lbv) * sig * (1.0 - sig)).astype(BF)
        dseg_ref[2] = dv.astype(BF)
        dseg_ref[3] = dgate_in.astype(BF)
        _accumulate_rows(step, dlb_ref, _part8(df * (1.0 - sig)))
        _accumulate_rows(step, dgain_ref, _part8(dgain))

    specs, blk = _hgrn_in_specs(layer, True, nblk)
    specs += [pl.BlockSpec((LANES, HGRN_W), lambda b: (blk(b), 0)),
              pl.BlockSpec((LANES, HGRN_W), lambda b: (blk(b), 1)),
              pl.BlockSpec((HGRN_HEADS, cpb * LANES, LANES), lambda b: (0, blk(b), 0))]
    return pl.pallas_call(
        body, grid=(nblk,), in_specs=specs,
        out_specs=[pl.BlockSpec((4, LANES, HGRN_W), lambda b: (0, blk(b), 0)),
                   pl.BlockSpec((8, HGRN_W), lambda b: (0, 0)), pl.BlockSpec((8, HGRN_DIM), lambda b: (0, 0))],
        out_shape=[jax.ShapeDtypeStruct((4, s, HGRN_W), BF), jax.ShapeDtypeStruct((8, HGRN_W), F32),
                   jax.ShapeDtypeStruct((8, HGRN_DIM), F32)],
        scratch_shapes=[pltpu.VMEM((HGRN_HEADS, LANES, LANES), F32), pltpu.VMEM((HGRN_HEADS, cpb * LANES, LANES), F32)],
        compiler_params=_cparams(("arbitrary",)), name=f"hgrn_bwd_l{layer}",
    )(proj, proj, proj, proj, lb, gain, o, dmixed, states)


def _bwd_inproj(layer, dqkv, dhg, g_in, x, gain, dres, tm=256):
    s, d = x.shape

    def body(dqkv_ref, dhg_ref, w_ref, x_ref, gain_ref, dres_ref, dx_ref, dxb_ref, dgain_ref):
        acc = jnp.zeros((tm, d), F32)
        for seg in range(N_SEG):
            a = dqkv_ref[seg] if seg < 3 else dhg_ref[seg - 3]
            acc = acc + _dot(a, w_ref[seg * SEG:(seg + 1) * SEG, :], NN)
        dx, dgain = _rms_bwd(acc, x_ref[...], gain_ref[...])
        out = dres_ref[...] + dx
        dx_ref[...] = out
        dxb_ref[...] = out.astype(BF)
        _accumulate_rows(pl.program_id(0), dgain_ref, _part8(dgain))

    row = pl.BlockSpec((tm, d), lambda i: (i, 0))
    return pl.pallas_call(
        body, grid=(s // tm,),
        in_specs=[pl.BlockSpec((3, tm, SEG), lambda i: (0, i, 0)), pl.BlockSpec((4, tm, SEG), lambda i: (0, i, 0)),
                  _resident((None, PROJ_W, d), lambda i: (0, 0, 0)), row,
                  pl.BlockSpec((None, 1, d), lambda i: (layer, 0, 0)), row],
        out_specs=[row, row, pl.BlockSpec((8, d), lambda i: (0, 0))],
        out_shape=[jax.ShapeDtypeStruct((s, d), F32), jax.ShapeDtypeStruct((s, d), BF), jax.ShapeDtypeStruct((8, d), F32)],
        compiler_params=_cparams(("arbitrary",)), name=f"bwd_inproj_l{layer}",
    )(dqkv, dhg, g_in, x, gain, dres)


def _adamw(w, g, m, v):
    m2 = ADAM_B1 * m + (1.0 - ADAM_B1) * g
    v2 = ADAM_B2 * v + (1.0 - ADAM_B2) * (g * g)
    m_hat = m2 / (1.0 - ADAM_B1 ** ADAM_STEP)
    v_hat = v2 / (1.0 - ADAM_B2 ** ADAM_STEP)
    delta = -ADAM_LR * (m_hat / (jnp.sqrt(v_hat) + ADAM_EPS) + ADAM_WD * w)
    return delta, m2, v2


def _adam_big(name, parts, w, m, v, transpose, row_tiles):
    depth = w.shape[0]
    r, c = parts[0].shape[1], parts[0].shape[2]
    if transpose:
        tc = c // row_tiles
        p_spec = pl.BlockSpec((N_DEV, r, tc), lambda t: (0, 0, t))
        w_spec = pl.BlockSpec((depth, tc, r), lambda t: (0, t, 0))
    else:
        tr = r // row_tiles
        p_spec = pl.BlockSpec((N_DEV, tr, c), lambda t: (0, t, 0))
        w_spec = pl.BlockSpec((depth, tr, c), lambda t: (0, t, 0))

    def body(*refs):
        p_refs = refs[:depth]
        w_ref, m_ref, v_ref, g_ref, d_ref, m2_ref, v2_ref = refs[depth:]
        for l in range(depth):
            g = p_refs[l][0].astype(F32)
            for dev in range(1, N_DEV):
                g = g + p_refs[l][dev].astype(F32)
            if transpose:
                g = g.T
            delta, m2, v2 = _adamw(w_ref[l], g, m_ref[l], v_ref[l])
            g_ref[l] = g
            d_ref[l] = delta
            m2_ref[l] = m2
            v2_ref[l] = v2

    return pl.pallas_call(
        body, grid=(row_tiles,), in_specs=[p_spec] * depth + [w_spec] * 3, out_specs=[w_spec] * 4,
        out_shape=[jax.ShapeDtypeStruct(w.shape, F32)] * 4,
        compiler_params=_cparams(("parallel",)), name=name,
    )(*parts, w, m, v)


def _adam_small(g, w, m, v):
    def body(g_ref, w_ref, m_ref, v_ref, d_ref, m2_ref, v2_ref):
        delta, m2, v2 = _adamw(w_ref[...], g_ref[...], m_ref[...], v_ref[...])
        d_ref[...] = delta
        m2_ref[...] = m2
        v2_ref[...] = v2

    vm = pl.BlockSpec(memory_space=pltpu.VMEM)
    return pl.pallas_call(body, in_specs=[vm] * 4, out_specs=[vm] * 3, out_shape=[jax.ShapeDtypeStruct(g.shape, F32)] * 3,
                          name="adam_small")(g, w, m, v)


def _lower_bounds(logits):
    def body(l_ref, lb_ref, jac_ref):
        l0, l1 = l_ref[0:1, :], l_ref[1:2, :]
        mx = jnp.maximum(l0, l1)
        e0, e1 = jnp.exp(l0 - mx), jnp.exp(l1 - mx)
        p0, p1 = e0 / (e0 + e1), e1 / (e0 + e1)
        lb_ref[0:1, :] = p0 - p0
        lb_ref[1:2, :] = (p0 + p1) - p0
        jac_ref[0:1, :] = -p0 * p1
        jac_ref[1:2, :] = p0 * p1

    vm = pl.BlockSpec(memory_space=pltpu.VMEM)
    return pl.pallas_call(body, in_specs=[vm], out_specs=[vm, vm], out_shape=[jax.ShapeDtypeStruct(logits.shape, F32)] * 2,
                          name="hgrn_lower_bounds")(logits)


def _rope_tables(s):
    half = 32
    inv_freq = ROPE_THETA ** (-jnp.arange(half, dtype=F32) / half)
    ang = jnp.arange(s, dtype=jnp.int32).astype(F32)[:, None] * inv_freq[None, :]
    cos, sin = jnp.cos(ang), jnp.sin(ang)
    return jnp.concatenate([cos] * 4, axis=1), jnp.concatenate([-sin, sin, -sin, sin], axis=1)


SMALL_NAMES = ("norm_mix", "attn_out_gain", "hgrn_lb_logits", "hgrn_out_gain", "norm_mlp", "norm_final")


def _pack_small(vals):
    flat = jnp.concatenate([v.reshape(-1) for v in vals])
    rows = -(-flat.shape[0] // (8 * LANES)) * 8
    return jnp.pad(flat, (0, rows * LANES - flat.shape[0])).reshape(rows, LANES)


def _unpack_small(packed, like):
    flat, out, off = packed.reshape(-1), [], 0
    for v in like:
        out.append(flat[off:off + v.size].reshape(v.shape))
        off += v.size
    return out


def kernel(x, norm_mix, w_in, attn_out_gain, hgrn_lb_logits, hgrn_out_gain, w_out, norm_mlp, w_up, w_down, norm_final, loss_target, m_norm_mix, m_w_in, m_attn_out_gain, m_hgrn_lb_logits, m_hgrn_out_gain, m_w_out, m_norm_mlp, m_w_up, m_w_down, m_norm_final, v_norm_mix, v_w_in, v_attn_out_gain, v_hgrn_lb_logits, v_hgrn_out_gain, v_w_out, v_norm_mlp, v_w_up, v_w_down, v_norm_final):
    depth = w_in.shape[0]
    assert depth == 2 and x.shape[0] == 1
    s, d = x.shape[1], x.shape[2]
    x0 = x[0]
    target = loss_target[0]
    cos, sin = _rope_tables(s)
    g_mix, g_attn, g_hg, g_mlp = (norm_mix[:, None, :], attn_out_gain[:, None, :], hgrn_out_gain[:, None, :],
                                  norm_mlp[:, None, :])
    lb, lb_jac = _lower_bounds(hgrn_lb_logits)
    lb3 = lb[:, None, :]

    shards = list(_pack_weights(w_in, w_out, w_up, w_down))
    w_pieces = _weight_pieces(*shards)
    w_groups = [[0], [1, 2, 3], [4], [5, 6, 7]]
    me = (4 * lax.axis_index("x") + 2 * lax.axis_index("y") + lax.axis_index("c")).astype(jnp.int32).reshape(1)
    lands = _exchange_own("all_gather_own", me, shards, w_pieces)
    w_sems, shards, lands, token = _exchange_start("all_gather_start", shards, lands, w_pieces, w_groups)

    def weights_ready(group, after):
        nonlocal shards
        idxs = w_groups[group]
        shards, got = _exchange_wait(f"all_gather_wait{group}", shards, [lands[i] for i in idxs], w_pieces,
                                     [(idxs, *w_sems[group])], after)
        return got

    def tied(small_arr, tok):
        return small_arr + tok[0, 0]

    saved = []
    xl = x0
    full = [None] * depth
    for l in range(depth):
        (full_in,) = weights_ready(2 * l, token if l == 0 else xl)
        proj, h = _fwd_inproj(l, xl, g_mix, full_in, cos, sin)
        fw = [_attn_fwd(l, dil, proj) for dil in DILATIONS]
        o_hg, mixed, states = _hgrn_fwd(l, proj, lb3, g_hg)
        o_attn, lse, mixed = _attn_merge(l, [f[0] for f in fw], [f[1] for f in fw], g_attn, mixed)
        full_out, full_up, full_down = weights_ready(2 * l + 1, mixed)
        x_mid = _fwd_outproj(l, mixed, full_out, xl)
        x_next, u, h2 = _mlp_fwd(l, x_mid, g_mlp, full_up, full_down)
        saved.append((xl, proj, h, o_attn, lse, o_hg, states, mixed, x_mid, u, h2))
        full[l] = (full_in, full_out, full_up, full_down)
        xl = x_next
    dx, dxb, dnorm_final8, loss8 = _loss_head(xl, norm_final[None, :], target)
    loss = lax.psum(jnp.sum(loss8[:, 0]), ("x", "y", "c"))

    exchanges = []

    def scatter(tag, grads, kinds):
        pieces = _grad_pieces(grads, kinds)
        own = _exchange_own(f"reduce_scatter_own_{tag}", me, grads, pieces)
        sems, grads, own, tok = _exchange_start(f"reduce_scatter_start_{tag}", grads, own, pieces, [list(range(len(pieces)))])
        exchanges.append((grads, own, pieces, sems[0]))
        return tok

    small = {}
    for l in reversed(range(depth)):
        xl, proj, h, o_attn, lse, o_hg, states, mixed, x_mid, u, h2 = saved[l]
        full_in, full_out, full_up, full_down = full[l]
        hs = full_up.shape[3]
        gw_down = _mm_tn(f"grad_w_down_l{l}", u, dxb, u.shape[1], a_fn=_relu2)
        dx_mid, dx_mid_b, du, dmlp8 = _mlp_bwd(l, dx, dxb, x_mid, g_mlp, u, full_up, full_down)
        gw_up = _mm_tn(f"grad_w_up_l{l}", h2, du, d, out_block_w=hs)
        gw_out = _mm_tn(f"grad_w_out_l{l}", mixed, dx_mid_b, mixed.shape[1])
        g_attn_t = tied(g_attn, scatter(f"mlp_l{l}", [gw_down, gw_up, gw_out], ["rows", "up", "rows"]))
        dmixed = _bwd_outproj(l, dx_mid_b, full_out)
        do, delta, dattn8 = _attn_norm_bwd(l, dmixed, o_attn, g_attn_t)
        parts = [_attn_bwd(l, dil, proj, do, lse, delta) for dil in DILATIONS]
        dqkv = _attn_combine(l, parts, cos, sin)
        dhg, dlb8, dhgain8 = _hgrn_bwd(l, proj, lb3, g_hg, o_hg, dmixed, states)
        gin = _mm_tn(f"grad_w_in_qkv_l{l}", dqkv, h, PROJ_W, a_lead=True)
        gw_in = _mm_tn(f"grad_w_in_hg_l{l}", dhg, h, PROJ_W, a_lead=True, out_block_off=3, prev=gin)
        g_mix_t = tied(g_mix, scatter(f"mix_l{l}", [gw_in], ["rows"]))
        dx, dxb, dmix8 = _bwd_inproj(l, dqkv, dhg, full_in, xl, g_mix_t, dx_mid)
        small[l] = (dmix8, dattn8, dlb8, dhgain8, dmlp8)

    def fin(p8):
        return jnp.sum(p8, axis=0)
    dlogits = lb_jac * fin(small[1][2])[None, :]
    small_grads = [jnp.stack([fin(small[l][0]) for l in range(depth)]), jnp.stack([fin(small[l][1]) for l in range(depth)]),
                   dlogits, jnp.stack([fin(small[l][3]) for l in range(depth)]),
                   jnp.stack([fin(small[l][4]) for l in range(depth)]), fin(dnorm_final8)]
    small_w = [norm_mix, attn_out_gain, hgrn_lb_logits, hgrn_out_gain, norm_mlp, norm_final]
    small_m = [m_norm_mix, m_attn_out_gain, m_hgrn_lb_logits, m_hgrn_out_gain, m_norm_mlp, m_norm_final]
    small_v = [v_norm_mix, v_attn_out_gain, v_hgrn_lb_logits, v_hgrn_out_gain, v_norm_mlp, v_norm_final]
    g_small = _all_reduce_small(_pack_small(small_grads))
    d_small, m_small, v_small = _adam_small(g_small, _pack_small(small_w), _pack_small(small_m), _pack_small(small_v))
    gs, ds, ms, vs = (_unpack_small(t, small_w) for t in (g_small, d_small, m_small, v_small))

    all_grads, all_lands, all_pieces, waits = [], [], [], []
    for grads, own, pieces, (send, recv) in exchanges:
        first = len(all_pieces)
        all_pieces += [p._replace(src=p.src + len(all_grads)) for p in pieces]
        waits.append((list(range(first, first + len(pieces))), send, recv))
        all_grads += grads
        all_lands += own
    _, landed = _exchange_wait("reduce_scatter_wait", all_grads, all_lands, all_pieces, waits, d_small)
    r_down, r_up, r_out, r_in = ([landed[4 + i], landed[i]] for i in range(4))
    big = {
        "w_in": _adam_big("adam_w_in", r_in, w_in, m_w_in, v_w_in, True, 4),
        "w_out": _adam_big("adam_w_out", r_out, w_out, m_w_out, v_w_out, False, 1),
        "w_up": _adam_big("adam_w_up", r_up, w_up, m_w_up, v_w_up, False, 2),
        "w_down": _adam_big("adam_w_down", r_down, w_down, m_w_down, v_w_down, False, 4),
    }

    def gather(idx, small_list):
        by_name = dict(zip(SMALL_NAMES, small_list))
        return [by_name["norm_mix"], big["w_in"][idx], by_name["attn_out_gain"], by_name["hgrn_lb_logits"],
                by_name["hgrn_out_gain"], big["w_out"][idx], by_name["norm_mlp"], big["w_up"][idx], big["w_down"][idx],
                by_name["norm_final"]]

    return (loss, dx[None], *gather(0, gs), *gather(1, ds), *gather(2, ms), *gather(3, vs))
```

```python
import functools
from typing import Callable, NamedTuple

import jax
import jax.numpy as jnp
from jax import lax
from jax.experimental import pallas as pl
from jax.experimental.pallas import tpu as pltpu

F32 = jnp.float32
BF = jnp.bfloat16

N_DEV = 8
ATTN_W = 512
HGRN_W = 512
HGRN_HEADS = 4
HGRN_DIM = 128
SEG = 512
N_SEG = 7
PROJ_W = N_SEG * SEG
MIX_W = ATTN_W + HGRN_W
SPAN = 128
DILATIONS = (1, 4, 16)
HGRN_CHUNK = 16
ROPE_THETA = 10000.0
NORM_EPS = 1e-6
MASK_VALUE = -1e30
ATTN_SCALE = 0.125
HGRN_SCALE = HGRN_DIM ** -0.5
ADAM_LR = 0.001
ADAM_B1 = 0.9
ADAM_B2 = 0.999
ADAM_EPS = 1e-08
ADAM_WD = 0.01
ADAM_STEP = 10
LANES = 128
VMEM_LIMIT = 56 * 1024 * 1024

NN = ((1,), (0,))
NT = ((1,), (1,))
TN = ((0,), (0,))
MESH = pl.DeviceIdType.MESH


def _dot(a, b, dims):
    return lax.dot_general(a, b, (dims, ((), ())), preferred_element_type=F32)


def _cparams(sem):
    return pltpu.CompilerParams(dimension_semantics=sem, vmem_limit_bytes=VMEM_LIMIT)


def _part8(x):
    r, n = x.shape
    return jnp.sum(x.reshape(r // 8, 8, n), axis=0)


def _sigmoid(x):
    return 1.0 / (1.0 + jnp.exp(-x))


def _rms_fwd(x, gain):
    r = lax.rsqrt(jnp.mean(x * x, axis=-1, keepdims=True) + NORM_EPS)
    return x * r * gain


def _rms_bwd(dy, x, gain):
    r = lax.rsqrt(jnp.mean(x * x, axis=-1, keepdims=True) + NORM_EPS)
    xn = x * r
    dxn = dy * gain
    dx = r * (dxn - xn * jnp.mean(dxn * xn, axis=-1, keepdims=True))
    return dx, dy * xn


def _rope_partner(x):
    n = x.shape[-1]
    lane = lax.broadcasted_iota(jnp.int32, x.shape, x.ndim - 1)
    return jnp.where((lane % 64) < 32, pltpu.roll(x, n - 32, x.ndim - 1), pltpu.roll(x, 32, x.ndim - 1))


def _tile_lanes(t, reps):
    return jnp.concatenate([t] * reps, axis=-1)


def _mm_tn(name, a, b, out_rows, a_lead=False, out_block_off=0, prev=None, out_block_w=None, a_fn=None,
           tm=512, tn=1024, sub=512):
    kdim, n = b.shape
    m = a.shape[-1]
    tm, tn, sub = min(tm, m), min(tn, n), min(sub, kdim)
    mt = m // tm
    n_lead = a.shape[0] if a_lead else 1
    if a_lead:
        a_spec = pl.BlockSpec((None, kdim, tm), lambda j, i: (i // mt, 0, i % mt))
    else:
        a_spec = pl.BlockSpec((kdim, tm), lambda j, i: (0, i))
    b_spec = pl.BlockSpec((kdim, tn), lambda j, i: (0, j))
    if out_block_w:
        nb = tn // out_block_w
        o_shape = jax.ShapeDtypeStruct((n // out_block_w, out_rows, out_block_w), BF)
        o_spec = pl.BlockSpec((nb, tm, out_block_w), lambda j, i: (j, i + out_block_off, 0))
    else:
        nb = 0
        o_shape = jax.ShapeDtypeStruct((out_rows, n), BF)
        o_spec = pl.BlockSpec((tm, tn), lambda j, i: (i + out_block_off, j))
    arrays, specs, aliases = [a, b], [a_spec, b_spec], {}
    if prev is not None:
        arrays.append(prev)
        specs.append(pl.BlockSpec(memory_space=pl.ANY))
        aliases = {2: 0}

    def body(*refs):
        a_ref, b_ref, o_ref = refs[0], refs[1], refs[-1]
        acc = None
        for k in range(kdim // sub):
            av = a_ref[k * sub:(k + 1) * sub, :]
            if a_fn is not None:
                av = a_fn(av)
            part = _dot(av, b_ref[k * sub:(k + 1) * sub, :], TN)
            acc = part if acc is None else acc + part
        if nb:
            for t in range(nb):
                o_ref[t] = acc[:, t * out_block_w:(t + 1) * out_block_w].astype(BF)
        else:
            o_ref[...] = acc.astype(BF)

    return pl.pallas_call(
        body, grid=(n // tn, n_lead * mt), in_specs=specs, out_specs=o_spec, out_shape=o_shape,
        compiler_params=_cparams(("parallel", "parallel")), name=name, input_output_aliases=aliases,
    )(*arrays)


def _pack_weights(w_in, w_out, w_up, w_down):
    depth, d, cin = w_in.shape

    def body(win_ref, wout_ref, wup_ref, wdown_ref, oin_ref, oout_ref, oup_ref, odown_ref):
        oin_ref[...] = win_ref[...].T.astype(BF)
        oout_ref[...] = wout_ref[...].astype(BF)
        oup_ref[...] = wup_ref[...].astype(BF)
        odown_ref[...] = wdown_ref[...].astype(BF)

    def spec(a):
        return pl.BlockSpec((None,) + a.shape[1:], lambda l: (l, 0, 0))

    outs = [jax.ShapeDtypeStruct((depth, cin, d), BF), jax.ShapeDtypeStruct(w_out.shape, BF),
            jax.ShapeDtypeStruct(w_up.shape, BF), jax.ShapeDtypeStruct(w_down.shape, BF)]
    return pl.pallas_call(
        body, grid=(depth,), in_specs=[spec(w_in), spec(w_out), spec(w_up), spec(w_down)],
        out_specs=[pl.BlockSpec((None, cin, d), lambda l: (l, 0, 0)), spec(w_out), spec(w_up), spec(w_down)],
        out_shape=outs, compiler_params=_cparams(("arbitrary",)), name="pack_weights",
    )(w_in, w_out, w_up, w_down)


def _my_position():
    x, y, c = lax.axis_index("x"), lax.axis_index("y"), lax.axis_index("c")
    return x, y, c, 4 * x + 2 * y + c


def _peer(x, y, c, k):
    px = 1 - x if k & 4 else x
    py = 1 - y if k & 2 else y
    pc = 1 - c if k & 1 else c
    return (px, py, pc), 4 * px + 2 * py + pc


PEER_ORDER = (1, 2, 4, 3, 5, 6, 7)


class _Piece(NamedTuple):
    src: int
    send: Callable
    slot: Callable
    land_shape: tuple
    own_src: tuple
    own_slot: tuple


HBM_SPEC = pl.BlockSpec(memory_space=pltpu.HBM)
SEM_SPEC = pl.BlockSpec(memory_space=pltpu.SEMAPHORE)
ANY_SPEC = pl.BlockSpec(memory_space=pl.ANY)


def _in_hbm(arrays):
    return [pltpu.with_memory_space_constraint(a, pltpu.HBM) for a in arrays]


def _hbm_like(arrays):
    return [pltpu.HBM(a.shape, a.dtype) for a in arrays]


def _rows_of(rows):
    return lambda ref, dev: ref.at[pl.ds(pl.multiple_of(dev * rows, 16), rows), :]


def _exchange_own(name, me, srcs, pieces):
    n = len(pieces)

    def body(me_ref, *refs):
        for i in range(n):
            refs[n + i][...] = refs[i][...]

    def spec(block_and_index):
        block, index = block_and_index
        return pl.BlockSpec(block, lambda i, me_ref: index(me_ref[0]))

    return pl.pallas_call(
        body,
        grid_spec=pltpu.PrefetchScalarGridSpec(
            num_scalar_prefetch=1, grid=(1,), in_specs=[spec(p.own_src) for p in pieces],
            out_specs=[spec(p.own_slot) for p in pieces]),
        out_shape=[jax.ShapeDtypeStruct(p.land_shape, BF) for p in pieces],
        compiler_params=_cparams(("arbitrary",)), name=name,
    )(me, *[srcs[p.src] for p in pieces])


def _exchange_start(name, srcs, lands, pieces, groups):
    n_src, n, n_g = len(srcs), len(pieces), len(groups)

    def body(*refs):
        src_refs, land_refs = refs[:n_src], refs[n_src:n_src + n]
        sems, token = refs[n_src + n:n_src + n + 2 * n_g], refs[-1]
        x, y, c, me = _my_position()
        for g, idxs in enumerate(groups):
            for k in PEER_ORDER:
                peer, pid = _peer(x, y, c, k)
                for j, i in enumerate(idxs):
                    p = pieces[i]
                    pltpu.make_async_remote_copy(
                        src_ref=p.send(src_refs[p.src], pid), dst_ref=p.slot(land_refs[i], me),
                        send_sem=sems[2 * g].at[(k - 1) * len(idxs) + j], recv_sem=sems[2 * g + 1].at[(k - 1) * len(idxs) + j],
                        device_id=peer, device_id_type=MESH).start()
        token[...] = jnp.zeros(token.shape, F32)

    sem_shapes = [pltpu.SemaphoreType.DMA(((N_DEV - 1) * len(idxs),)) for idxs in groups for _ in range(2)]
    res = pl.pallas_call(
        body, in_specs=[HBM_SPEC] * (n_src + n),
        out_specs=[SEM_SPEC] * (2 * n_g) + [HBM_SPEC] * (n_src + n) + [pl.BlockSpec(memory_space=pltpu.VMEM)],
        out_shape=sem_shapes + _hbm_like(srcs) + _hbm_like(lands) + [jax.ShapeDtypeStruct((8, LANES), F32)],
        input_output_aliases={i: 2 * n_g + i for i in range(n_src + n)},
        compiler_params=pltpu.CompilerParams(has_side_effects=pltpu.SideEffectType.DATAFLOW_SIDE_EFFECTING),
        name=name,
    )(*_in_hbm(srcs), *_in_hbm(lands))
    sems = [(res[2 * g], res[2 * g + 1]) for g in range(n_g)]
    return sems, list(res[2 * n_g:2 * n_g + n_src]), list(res[2 * n_g + n_src:2 * n_g + n_src + n]), res[-1]


def _exchange_wait(name, srcs, lands, pieces, waits, after):
    n_src, n, n_g = len(srcs), len(lands), len(waits)

    def body(*refs):
        src_refs, land_refs = refs[:n_src], refs[n_src:n_src + n]
        sems = refs[n_src + n:n_src + n + 2 * n_g]
        x, y, c, me = _my_position()
        at = 0
        for g, (idxs, _, _) in enumerate(waits):
            for k in PEER_ORDER:
                peer, pid = _peer(x, y, c, k)
                for j, i in enumerate(idxs):
                    p = pieces[i]
                    cp = pltpu.make_async_remote_copy(
                        src_ref=p.send(src_refs[p.src], pid), dst_ref=p.slot(land_refs[at + j], pid),
                        send_sem=sems[2 * g].at[(k - 1) * len(idxs) + j], recv_sem=sems[2 * g + 1].at[(k - 1) * len(idxs) + j],
                        device_id=peer, device_id_type=MESH)
                    cp.wait_send()
                    cp.wait_recv()
            at += len(idxs)

    sem_args = [s for _, send, recv in waits for s in (send, recv)]
    res = pl.pallas_call(
        body, in_specs=[HBM_SPEC] * (n_src + n) + [SEM_SPEC] * (2 * n_g) + [ANY_SPEC],
        out_specs=[HBM_SPEC] * (n_src + n), out_shape=_hbm_like(srcs) + _hbm_like(lands),
        input_output_aliases={i: i for i in range(n_src + n)},
        compiler_params=pltpu.CompilerParams(has_side_effects=pltpu.SideEffectType.DATAFLOW_SIDE_EFFECTING),
        name=name,
    )(*srcs, *lands, *sem_args, after)
    return list(res[:n_src]), list(res[n_src:])


def _weight_pieces(p_in, p_out, p_up, p_down):
    depth, cin, d = p_in.shape
    rout, hs = p_out.shape[1], p_up.shape[2]
    pieces = []
    for l in range(depth):
        whole = functools.partial(lambda ref, dev, l: ref.at[l], l=l)
        layer = functools.partial(lambda dev, l: (l, 0, 0), l=l)

        def rows(src, n_rows, whole=whole, layer=layer):
            return _Piece(src, whole, lambda ref, dev: _rows_of(n_rows)(ref.at[0], dev), (1, N_DEV * n_rows, d),
                          ((None, n_rows, d), layer), ((None, n_rows, d), lambda dev: (0, dev, 0)))

        pieces += [
            rows(0, cin), rows(1, rout),
            _Piece(2, whole, lambda ref, dev: ref.at[0, dev], (1, N_DEV, d, hs),
                   ((None, d, hs), layer), ((None, None, d, hs), lambda dev: (0, dev, 0, 0))),
            rows(3, hs),
        ]
    return pieces


def _grad_pieces(g_pair, kinds):
    pieces = []
    for i, (g, kind) in enumerate(zip(g_pair, kinds)):
        lead = lambda dev: (dev, 0, 0)
        if kind == "up":
            blk = ((None,) + g.shape[1:], lead)
            pieces.append(_Piece(i, lambda ref, dev: ref.at[dev], lambda ref, dev: ref.at[dev], g.shape, blk, blk))
        else:
            rows, cols = g.shape[0] // N_DEV, g.shape[1]
            pieces.append(_Piece(i, _rows_of(rows), lambda ref, dev: ref.at[dev], (N_DEV, rows, cols),
                                 ((rows, cols), lambda dev: (dev, 0)), ((None, rows, cols), lead)))
    return pieces


def _all_reduce_small(vec):
    rows = vec.shape[0]

    def body(v_ref, o_ref, buf_ref, send_sems, recv_sems):
        x, y, c, me = _my_position()
        buf_ref[me] = v_ref[...]
        sends = []
        for k in PEER_ORDER:
            peer, _ = _peer(x, y, c, k)
            cp = pltpu.make_async_remote_copy(src_ref=v_ref, dst_ref=buf_ref.at[me], send_sem=send_sems.at[k - 1],
                                              recv_sem=recv_sems.at[k - 1], device_id=peer, device_id_type=MESH)
            cp.start()
            sends.append(cp)
        for k in PEER_ORDER:
            peer, pid = _peer(x, y, c, k)
            pltpu.make_async_remote_copy(src_ref=v_ref, dst_ref=buf_ref.at[pid], send_sem=send_sems.at[k - 1],
                                         recv_sem=recv_sems.at[k - 1], device_id=peer, device_id_type=MESH).wait_recv()
        for cp in sends:
            cp.wait_send()
        total = buf_ref[0]
        for dev in range(1, N_DEV):
            total = total + buf_ref[dev]
        o_ref[...] = total

    vm = pl.BlockSpec(memory_space=pltpu.VMEM)
    return pl.pallas_call(
        body, in_specs=[vm], out_specs=vm, out_shape=jax.ShapeDtypeStruct(vec.shape, F32),
        scratch_shapes=[pltpu.VMEM((N_DEV, rows, LANES), F32), pltpu.SemaphoreType.DMA((N_DEV - 1,)),
                        pltpu.SemaphoreType.DMA((N_DEV - 1,))],
        name="all_reduce_small",
    )(vec)


def _resident(block_shape, index_map):
    return pl.BlockSpec(block_shape, index_map, pipeline_mode=pl.Buffered(1))


def _fwd_inproj(layer, x, gain, g_in, cos, sin, tm=256):
    s, d = x.shape

    def body(x_ref, gain_ref, w_ref, cos_ref, sin_ref, proj_ref, h_ref):
        h = _rms_fwd(x_ref[...], gain_ref[...]).astype(BF)
        h_ref[...] = h
        cs = _tile_lanes(cos_ref[...], SEG // LANES)
        sn = _tile_lanes(sin_ref[...], SEG // LANES)
        for seg in range(N_SEG):
            acc = _dot(h, w_ref[seg * SEG:(seg + 1) * SEG, :], NT)
            if seg < 2:
                acc = acc * cs + _rope_partner(acc) * sn
            if seg == 0:
                acc = acc * ATTN_SCALE
            proj_ref[:, seg * SEG:(seg + 1) * SEG] = acc

    return pl.pallas_call(
        body, grid=(s // tm,),
        in_specs=[pl.BlockSpec((tm, d), lambda i: (i, 0)), pl.BlockSpec((None, 1, d), lambda i: (layer, 0, 0)),
                  _resident((None, PROJ_W, d), lambda i: (0, 0, 0)),
                  pl.BlockSpec((tm, LANES), lambda i: (i, 0)), pl.BlockSpec((tm, LANES), lambda i: (i, 0))],
        out_specs=[pl.BlockSpec((tm, PROJ_W), lambda i: (i, 0)), pl.BlockSpec((tm, d), lambda i: (i, 0))],
        out_shape=[jax.ShapeDtypeStruct((s, PROJ_W), F32), jax.ShapeDtypeStruct((s, d), BF)],
        compiler_params=_cparams(("parallel",)), name=f"fwd_inproj_l{layer}",
    )(x, gain, g_in, cos, sin)


ATTN_UNIT = SPAN * max(DILATIONS)
ATTN_GROUP = 4


def _attn_masks(first_block_has_prev):
    row = lax.broadcasted_iota(jnp.int32, (SPAN, 2 * SPAN), 0)
    col = lax.broadcasted_iota(jnp.int32, (SPAN, 2 * SPAN), 1)
    band = (col >= row) & (col <= row + SPAN)
    lane = lax.broadcasted_iota(jnp.int32, (SPAN, LANES), 1)
    return band & ((col >= SPAN) | first_block_has_prev), band, lane < 64


def _attn_specs(n_in_extra):
    pairs = ATTN_W // LANES
    q_spec = pl.BlockSpec((ATTN_UNIT, LANES), lambda p, n: (n, p))

    def prev(seg):
        return pl.BlockSpec((ATTN_UNIT, LANES), lambda p, n: (jnp.maximum(n - 1, 0), seg * pairs + p))

    def cur(seg):
        return pl.BlockSpec((ATTN_UNIT, LANES), lambda p, n: (n, seg * pairs + p))

    return [q_spec, prev(1), cur(1), prev(2), cur(2)] + [q_spec] * n_in_extra


def _attn_groups(dil):
    blocks = ATTN_UNIT // (SPAN * dil)
    pairs = [(r, b) for r in range(dil) for b in range(blocks)]
    return [pairs[i:i + ATTN_GROUP] for i in range(0, len(pairs), ATTN_GROUP)]


def _block_rows(dil, r, b, n=1):
    start = r + dil * SPAN * b
    return pl.ds(start, n * SPAN, stride=dil) if dil > 1 else pl.ds(start, n * SPAN)


def _block_keys(prev_ref, cur_ref, dil, r, b):
    if b > 0:
        return cur_ref[_block_rows(dil, r, b - 1, 2), :]
    last = ATTN_UNIT // (SPAN * dil) - 1
    return jnp.concatenate([prev_ref[_block_rows(dil, r, last), :], cur_ref[_block_rows(dil, r, 0), :]], axis=0)


def _attn_fwd(layer, proj):
    s = proj.shape[0]
    n_pat = len(DILATIONS)
    merge_rows = 256

    def body(q_ref, kp_ref, kc_ref, vp_ref, vc_ref, o_ref, lse_ref, o_scr, lse_scr):
        m_first, m_rest, is_a = _attn_masks(pl.program_id(1) > 0)
        sels = (is_a, jnp.logical_not(is_a))
        ones = jnp.ones((2 * SPAN, LANES), BF)
        for pi, dil in enumerate(DILATIONS):
            for group in _attn_groups(dil):
                items = [(r, b, h) for r, b in group for h in range(2)]
                q = {rb: q_ref[_block_rows(dil, *rb), :] for rb in group}
                k = {rb: _block_keys(kp_ref, kc_ref, dil, *rb).astype(BF) for rb in group}
                v = {rb: _block_keys(vp_ref, vc_ref, dil, *rb).astype(BF) for rb in group}
                sc = [jnp.where(m_first if b == 0 else m_rest,
                                _dot(jnp.where(sels[h], q[r, b], 0.0).astype(BF), k[r, b], NT), MASK_VALUE)
                      for r, b, h in items]
                mx = [jnp.max(jnp.maximum(t[:, :SPAN], t[:, SPAN:]), axis=-1, keepdims=True) for t in sc]
                p = [jnp.exp(t - m).astype(BF) for t, m in zip(sc, mx)]
                den = [_dot(t, ones, NN) for t in p]
                num = [_dot(t, v[r, b], NN) for t, (r, b, h) in zip(p, items)]
                o = [a / d for a, d in zip(num, den)]
                lse = [m + jnp.log(d) for m, d in zip(mx, den)]
                for j, (r, b) in enumerate(group):
                    o_scr[pi, _block_rows(dil, r, b), :] = jnp.where(is_a, o[2 * j], o[2 * j + 1])
                    lse_scr[pi, _block_rows(dil, r, b), :] = jnp.where(is_a, lse[2 * j], lse[2 * j + 1])
        for c in range(ATTN_UNIT // merge_rows):
            rows = slice(c * merge_rows, (c + 1) * merge_rows)
            ls = [lse_scr[pi, rows, :] for pi in range(n_pat)]
            mx = functools.reduce(jnp.maximum, ls)
            ws = [jnp.exp(l - mx) for l in ls]
            den = functools.reduce(jnp.add, ws)
            o_ref[rows, :] = functools.reduce(jnp.add, [w * o_scr[pi, rows, :] for pi, w in enumerate(ws)]) / den
            lse_ref[rows, :] = mx + jnp.log(den)

    out_spec = pl.BlockSpec((ATTN_UNIT, LANES), lambda p, n: (n, p))
    return pl.pallas_call(
        body, grid=(ATTN_W // LANES, s // ATTN_UNIT), in_specs=_attn_specs(0), out_specs=[out_spec, out_spec],
        out_shape=[jax.ShapeDtypeStruct((s, ATTN_W), F32)] * 2,
        scratch_shapes=[pltpu.VMEM((n_pat, ATTN_UNIT, LANES), F32)] * 2,
        compiler_params=_cparams(("parallel", "arbitrary")), name=f"attn_fwd_l{layer}",
    )(proj, proj, proj, proj, proj)


def _attn_norm(layer, o, gain, mixed, tm=512):
    s = o.shape[0]

    def body(o_ref, gain_ref, mixed_ref, n_ref):
        n_ref[...] = _rms_fwd(o_ref[...], gain_ref[...]).astype(BF)

    blk = pl.BlockSpec((tm, ATTN_W), lambda i: (i, 0))
    return pl.pallas_call(
        body, grid=(s // tm,),
        in_specs=[blk, pl.BlockSpec((None, 1, ATTN_W), lambda i: (layer, 0, 0)), pl.BlockSpec(memory_space=pl.ANY)],
        out_specs=blk, out_shape=jax.ShapeDtypeStruct(mixed.shape, BF), input_output_aliases={2: 0},
        compiler_params=_cparams(("parallel",)), name=f"attn_norm_l{layer}",
    )(o, gain, mixed)


def _chunk_cumsum(x, reverse=False):
    n = x.shape[0]
    pos = lax.broadcasted_iota(jnp.int32, x.shape, 0) % HGRN_CHUNK
    for sh in (1, 2, 4, 8):
        if reverse:
            x = x + jnp.where(pos < HGRN_CHUNK - sh, pltpu.roll(x, n - sh, 0), 0.0)
        else:
            x = x + jnp.where(pos >= sh, pltpu.roll(x, sh, 0), 0.0)
    return x


def _chunk_row(x, row):
    r, n = x.shape
    x3 = x.reshape(r // HGRN_CHUNK, HGRN_CHUNK, n)
    return jnp.broadcast_to(x3[:, row:row + 1, :], x3.shape).reshape(r, n)


def _hgrn_pre(qh, z, lb):
    sig = _sigmoid(z)
    f = lb + (1.0 - lb) * sig
    k = (1.0 - lb) * _sigmoid(-z)
    sq = _sigmoid(qh)
    q = qh * sq * HGRN_SCALE
    g = _chunk_cumsum(jnp.log(f))
    g_mid = _chunk_row(g, HGRN_CHUNK // 2 - 1)
    g_last = _chunk_row(g, HGRN_CHUNK - 1)
    e_q, e_k = jnp.exp(g - g_mid), jnp.exp(g_mid - g)
    e_in, e_out = jnp.exp(g), jnp.exp(g_last - g)
    return dict(sig=sig, f=f, k=k, sq=sq, q=q, g_last=g_last, e_q=e_q, e_k=e_k, e_in=e_in, e_out=e_out,
                qt=q * e_q, kt=k * e_k, qg=q * e_in, kout=k * e_out)


def _hgrn_mask():
    row = lax.broadcasted_iota(jnp.int32, (LANES, LANES), 0)
    col = lax.broadcasted_iota(jnp.int32, (LANES, LANES), 1)
    return (row // HGRN_CHUNK == col // HGRN_CHUNK) & (col <= row)


def _hgrn_in_specs(layer, rev, nblk):
    def blk(b):
        return nblk - 1 - b if rev else b
    first = 3 * ATTN_W // HGRN_W
    specs = [pl.BlockSpec((LANES, HGRN_W), functools.partial(lambda b, seg: (blk(b), first + seg), seg=seg))
             for seg in range(4)]
    specs.append(pl.BlockSpec((None, 1, HGRN_W), lambda b: (layer, 0, 0)))
    specs.append(pl.BlockSpec((None, 1, HGRN_DIM), lambda b: (layer, 0, 0)))
    return specs, blk


def _head(x, h):
    return x[:, h * HGRN_DIM:(h + 1) * HGRN_DIM]


def _chunk(x, c):
    return x[c * HGRN_CHUNK:(c + 1) * HGRN_CHUNK]


HEADS = range(HGRN_HEADS)
CHUNKS = range(LANES // HGRN_CHUNK)


def _hgrn_fwd(layer, proj, lb, gain):
    s = proj.shape[0]
    nblk = s // LANES
    cpb = len(CHUNKS)

    def body(q_ref, f_ref, i_ref, g_ref, lb_ref, gain_ref, o_ref, rec_ref, st_ref, state):
        @pl.when(pl.program_id(0) == 0)
        def _():
            state[...] = jnp.zeros(state.shape, F32)

        pre = _hgrn_pre(q_ref[...], f_ref[...], lb_ref[...])
        v = i_ref[...].astype(BF)
        qt, kt, qg, kout = (pre[n].astype(BF) for n in ("qt", "kt", "qg", "kout"))
        dec = jnp.exp(pre["g_last"])
        mask = _hgrn_mask()
        a = [jnp.where(mask, _dot(_head(qt, h), _head(kt, h), NT), 0.0).astype(BF) for h in HEADS]
        o_intra = [_dot(a[h], _head(v, h), NN) for h in HEADS]
        update = [[_dot(_chunk(_head(v, h), c), _chunk(_head(kout, h), c), TN) for c in CHUNKS] for h in HEADS]
        for h in HEADS:
            st = state[h]
            for c in CHUNKS:
                st_ref[h, c * LANES:(c + 1) * LANES, :] = st.astype(BF)
                st = st * _head(dec, h)[c * HGRN_CHUNK:c * HGRN_CHUNK + 1, :] + update[h][c]
            state[h] = st
        inter = [[_dot(_chunk(_head(qg, h), c), st_ref[h, c * LANES:(c + 1) * LANES, :].astype(BF), NT) for c in CHUNKS]
                 for h in HEADS]
        o = [o_intra[h] + jnp.concatenate(inter[h], axis=0) for h in HEADS]
        o_ref[...] = jnp.concatenate(o, axis=1)
        gate = g_ref[...]
        normed = jnp.concatenate([_rms_fwd(o[h], gain_ref[...]) for h in HEADS], axis=1)
        rec_ref[...] = (normed * (gate * _sigmoid(gate))).astype(BF)

    specs, _ = _hgrn_in_specs(layer, False, nblk)
    return pl.pallas_call(
        body, grid=(nblk,), in_specs=specs,
        out_specs=[pl.BlockSpec((LANES, HGRN_W), lambda b: (b, 0)), pl.BlockSpec((LANES, HGRN_W), lambda b: (b, 1)),
                   pl.BlockSpec((HGRN_HEADS, cpb * LANES, LANES), lambda b: (0, b, 0))],
        out_shape=[jax.ShapeDtypeStruct((s, HGRN_W), F32), jax.ShapeDtypeStruct((s, MIX_W), BF),
                   jax.ShapeDtypeStruct((HGRN_HEADS, nblk * cpb * LANES, LANES), BF)],
        scratch_shapes=[pltpu.VMEM((HGRN_HEADS, LANES, LANES), F32)],
        compiler_params=_cparams(("arbitrary",)), name=f"hgrn_fwd_l{layer}",
    )(proj, proj, proj, proj, lb, gain)


def _fwd_outproj(layer, mixed, g_out, x, tm=512):
    s, d = x.shape
    mw = mixed.shape[1]

    def body(m_ref, w_ref, x_ref, o_ref):
        o_ref[...] = x_ref[...] + _dot(m_ref[...], w_ref[...], NN)

    row = pl.BlockSpec((tm, d), lambda i: (i, 0))
    return pl.pallas_call(
        body, grid=(s // tm,),
        in_specs=[pl.BlockSpec((tm, mw), lambda i: (i, 0)), _resident((None, mw, d), lambda i: (0, 0, 0)), row],
        out_specs=row, out_shape=jax.ShapeDtypeStruct((s, d), F32),
        compiler_params=_cparams(("parallel",)), name=f"fwd_outproj_l{layer}",
    )(mixed, g_out, x)


def _relu2(u):
    return jnp.square(jnp.maximum(u, 0)).astype(BF)


def _mlp_fwd(layer, x, gain, g_up, g_down, tm=256):
    s, d = x.shape
    nblk, hs = g_up.shape[1], g_up.shape[3]

    def body(x_ref, gain_ref, up_ref, down_ref, o_ref, u_ref, h_ref, a_buf):
        xv = x_ref[...]
        h = _rms_fwd(xv, gain_ref[...]).astype(BF)
        h_ref[...] = h
        for j in range(nblk):
            u = _dot(h, up_ref[j], NN)
            u_ref[:, j * hs:(j + 1) * hs] = u.astype(BF)
            a_buf[:, j * hs:(j + 1) * hs] = _relu2(u)
        acc = xv
        for j in range(nblk):
            acc = acc + _dot(a_buf[:, j * hs:(j + 1) * hs], down_ref[j * hs:(j + 1) * hs, :], NN)
        o_ref[...] = acc

    row = pl.BlockSpec((tm, d), lambda i: (i, 0))
    return pl.pallas_call(
        body, grid=(s // tm,),
        in_specs=[row, pl.BlockSpec((None, 1, d), lambda i: (layer, 0, 0)),
                  _resident((None, nblk, d, hs), lambda i: (0, 0, 0, 0)),
                  _resident((None, nblk * hs, d), lambda i: (0, 0, 0))],
        out_specs=[row, pl.BlockSpec((tm, nblk * hs), lambda i: (i, 0)), row],
        out_shape=[jax.ShapeDtypeStruct((s, d), F32), jax.ShapeDtypeStruct((s, nblk * hs), BF),
                   jax.ShapeDtypeStruct((s, d), BF)],
        scratch_shapes=[pltpu.VMEM((tm, nblk * hs), BF)],
        compiler_params=_cparams(("parallel",)), name=f"mlp_fwd_l{layer}",
    )(x, gain, g_up, g_down)


def _loss_head(x, gain, target, tm=512):
    s, d = x.shape

    def body(x_ref, gain_ref, t_ref, dx_ref, dxb_ref, dgain_ref, loss_ref):
        i = pl.program_id(0)
        xv, gv = x_ref[...], gain_ref[...]
        err = _rms_fwd(xv, gv) - t_ref[...]
        dx, dgain = _rms_bwd(err * (1.0 / d), xv, gv)
        dx_ref[...] = dx
        dxb_ref[...] = dx.astype(BF)
        part = _part8(dgain)
        lpart = _part8(0.5 * jnp.mean(err * err, axis=-1, keepdims=True) * jnp.ones((1, LANES), F32))

        @pl.when(i == 0)
        def _():
            dgain_ref[...] = part
            loss_ref[...] = lpart

        @pl.when(i > 0)
        def _():
            dgain_ref[...] += part
            loss_ref[...] += lpart

    row = pl.BlockSpec((tm, d), lambda i: (i, 0))
    return pl.pallas_call(
        body, grid=(s // tm,),
        in_specs=[row, pl.BlockSpec((1, d), lambda i: (0, 0)), row],
        out_specs=[row, row, pl.BlockSpec((8, d), lambda i: (0, 0)), pl.BlockSpec((8, LANES), lambda i: (0, 0))],
        out_shape=[jax.ShapeDtypeStruct((s, d), F32), jax.ShapeDtypeStruct((s, d), BF), jax.ShapeDtypeStruct((8, d), F32),
                   jax.ShapeDtypeStruct((8, LANES), F32)],
        compiler_params=_cparams(("arbitrary",)), name="loss_head",
    )(x, gain, target)


def _accumulate_rows(i, ref, part):
    @pl.when(i == 0)
    def _():
        ref[...] = part

    @pl.when(i > 0)
    def _():
        ref[...] += part


def _mlp_bwd(layer, dx, dxb, x, gain, u, g_up, g_down, tm=256):
    s, d = x.shape
    nblk, hs = g_up.shape[1], g_up.shape[3]

    def body(dx_ref, dxb_ref, x_ref, gain_ref, u_ref, up_ref, down_ref, o_ref, ob_ref, du_ref, dgain_ref):
        dxb_v = dxb_ref[...]
        for j in range(nblk):
            cols = slice(j * hs, (j + 1) * hs)
            da = _dot(dxb_v, down_ref[cols, :], NT)
            du_ref[:, cols] = (da * (2.0 * jnp.maximum(u_ref[:, cols].astype(F32), 0.0))).astype(BF)
        acc = jnp.zeros((tm, d), F32)
        for j in range(nblk):
            acc = acc + _dot(du_ref[:, j * hs:(j + 1) * hs], up_ref[j], NT)
        dxn, dgain = _rms_bwd(acc, x_ref[...], gain_ref[...])
        out = dx_ref[...] + dxn
        o_ref[...] = out
        ob_ref[...] = out.astype(BF)
        _accumulate_rows(pl.program_id(0), dgain_ref, _part8(dgain))

    row = pl.BlockSpec((tm, d), lambda i: (i, 0))
    wide = pl.BlockSpec((tm, nblk * hs), lambda i: (i, 0))
    return pl.pallas_call(
        body, grid=(s // tm,),
        in_specs=[row, row, row, pl.BlockSpec((None, 1, d), lambda i: (layer, 0, 0)), wide,
                  _resident((None, nblk, d, hs), lambda i: (0, 0, 0, 0)),
                  _resident((None, nblk * hs, d), lambda i: (0, 0, 0))],
        out_specs=[row, row, wide, pl.BlockSpec((8, d), lambda i: (0, 0))],
        out_shape=[jax.ShapeDtypeStruct((s, d), F32), jax.ShapeDtypeStruct((s, d), BF),
                   jax.ShapeDtypeStruct((s, nblk * hs), BF), jax.ShapeDtypeStruct((8, d), F32)],
        compiler_params=_cparams(("arbitrary",)), name=f"mlp_bwd_l{layer}",
    )(dx, dxb, x, gain, u, g_up, g_down)


def _bwd_outproj(layer, dxb, g_out, tm=512):
    s, d = dxb.shape
    mw = g_out.shape[1]

    def body(dx_ref, w_ref, o_ref):
        o_ref[...] = _dot(dx_ref[...], w_ref[...], NT)

    return pl.pallas_call(
        body, grid=(s // tm,),
        in_specs=[pl.BlockSpec((tm, d), lambda i: (i, 0)), _resident((None, mw, d), lambda i: (0, 0, 0))],
        out_specs=pl.BlockSpec((tm, mw), lambda i: (i, 0)), out_shape=jax.ShapeDtypeStruct((s, mw), F32),
        compiler_params=_cparams(("parallel",)), name=f"bwd_outproj_l{layer}",
    )(dxb, g_out)


def _attn_norm_bwd(layer, dmixed, o, gain, tm=512):
    s = o.shape[0]

    def body(dm_ref, o_ref, gain_ref, do_ref, delta_ref, dgain_ref):
        i = pl.program_id(0)
        ov = o_ref[...]
        do, dgain = _rms_bwd(dm_ref[...], ov, gain_ref[...])
        do_ref[...] = do
        row = lax.broadcasted_iota(jnp.int32, (ATTN_W, ATTN_W), 0)
        col = lax.broadcasted_iota(jnp.int32, (ATTN_W, ATTN_W), 1)
        same_head = jnp.where(row // 64 == col // 64, 1.0, 0.0)
        delta_ref[...] = jnp.dot(do * ov, same_head, precision=lax.Precision.HIGHEST, preferred_element_type=F32)
        part = _part8(dgain)

        @pl.when(i == 0)
        def _():
            dgain_ref[...] = part

        @pl.when(i > 0)
        def _():
            dgain_ref[...] += part

    blk = pl.BlockSpec((tm, ATTN_W), lambda i: (i, 0))
    return pl.pallas_call(
        body, grid=(s // tm,), in_specs=[blk, blk, pl.BlockSpec((None, 1, ATTN_W), lambda i: (layer, 0, 0))],
        out_specs=[blk, blk, pl.BlockSpec((8, ATTN_W), lambda i: (0, 0))],
        out_shape=[jax.ShapeDtypeStruct((s, ATTN_W), F32), jax.ShapeDtypeStruct((s, ATTN_W), F32),
                   jax.ShapeDtypeStruct((8, ATTN_W), F32)],
        compiler_params=_cparams(("arbitrary",)), name=f"attn_norm_bwd_l{layer}",
    )(dmixed, o, gain)


def _attn_bwd(layer, proj, do, lse, delta):
    s = proj.shape[0]

    def body(q_ref, kp_ref, kc_ref, vp_ref, vc_ref, do_ref, lse_ref, delta_ref, dq_ref, dk_ref, dkp_ref, dv_ref, dvp_ref):
        m_first, m_rest, is_a = _attn_masks(pl.program_id(1) > 0)
        is_a_keys = lax.broadcasted_iota(jnp.int32, (2 * SPAN, LANES), 1) < 64
        sels = (is_a, jnp.logical_not(is_a))
        key_sels = (is_a_keys, jnp.logical_not(is_a_keys))
        for ref in (dq_ref, dk_ref, dkp_ref, dv_ref, dvp_ref):
            ref[...] = jnp.zeros(ref.shape, F32)
        for dil in DILATIONS:
            last = ATTN_UNIT // (SPAN * dil) - 1
            for group in _attn_groups(dil):
                items = [(r, b, h) for r, b in group for h in range(2)]
                q = {rb: q_ref[_block_rows(dil, *rb), :] for rb in group}
                dov = {rb: do_ref[_block_rows(dil, *rb), :] for rb in group}
                lse_v = {rb: lse_ref[_block_rows(dil, *rb), :] for rb in group}
                delta_v = {rb: delta_ref[_block_rows(dil, *rb), :] for rb in group}
                k = {rb: _block_keys(kp_ref, kc_ref, dil, *rb) for rb in group}
                v = {rb: _block_keys(vp_ref, vc_ref, dil, *rb).astype(BF) for rb in group}
                qh = [jnp.where(sels[h], q[r, b], 0.0).astype(BF) for r, b, h in items]
                doh = [jnp.where(sels[h], dov[r, b], 0.0).astype(BF) for r, b, h in items]
                kh = [jnp.where(key_sels[h], k[r, b], 0.0).astype(BF) for r, b, h in items]
                sc = [jnp.where(m_first if b == 0 else m_rest, _dot(qh[i], kh[i], NT), MASK_VALUE)
                      for i, (r, b, h) in enumerate(items)]
                dp = [_dot(doh[i], v[r, b], NT) for i, (r, b, h) in enumerate(items)]
                p = [jnp.exp(sc[i] - lse_v[r, b][:, 64 * h:64 * h + 1]) for i, (r, b, h) in enumerate(items)]
                ds = [(p[i] * (dp[i] - delta_v[r, b][:, 64 * h:64 * h + 1])).astype(BF) for i, (r, b, h) in enumerate(items)]
                dv = [_dot(p[i].astype(BF), doh[i], TN) for i in range(len(items))]
                dq = [_dot(ds[i], kh[i], NN) for i in range(len(items))]
                dk = [_dot(ds[i], qh[i], TN) for i in range(len(items))]
                for j, (r, b) in enumerate(group):
                    own = _block_rows(dil, r, b)
                    dq_ref[own, :] += dq[2 * j] + dq[2 * j + 1]
                    dk2, dv2 = dk[2 * j] + dk[2 * j + 1], dv[2 * j] + dv[2 * j + 1]
                    dk_ref[own, :] += dk2[SPAN:]
                    dv_ref[own, :] += dv2[SPAN:]
                    if b > 0:
                        before = _block_rows(dil, r, b - 1)
                        dk_ref[before, :] += dk2[:SPAN]
                        dv_ref[before, :] += dv2[:SPAN]
                    else:
                        before = _block_rows(dil, r, last)
                        dkp_ref[before, :] += dk2[:SPAN]
                        dvp_ref[before, :] += dv2[:SPAN]

    out_spec = pl.BlockSpec((ATTN_UNIT, LANES), lambda p, n: (n, p))
    return pl.pallas_call(
        body, grid=(ATTN_W // LANES, s // ATTN_UNIT), in_specs=_attn_specs(3), out_specs=[out_spec] * 5,
        out_shape=[jax.ShapeDtypeStruct((s, ATTN_W), F32)] * 5,
        compiler_params=_cparams(("parallel", "arbitrary")), name=f"attn_bwd_l{layer}",
    )(proj, proj, proj, proj, proj, do, lse, delta)


def _attn_combine(layer, parts, cos, sin):
    dq, dk, dkp, dv, dvp = parts
    s = dq.shape[0]
    nblk = s // SPAN
    per_unit = ATTN_UNIT // SPAN
    here = pl.BlockSpec((SPAN, ATTN_W), lambda i: (i, 0))
    ahead = pl.BlockSpec((SPAN, ATTN_W), lambda i: (jnp.minimum(i + per_unit, nblk - 1), 0))
    tab = pl.BlockSpec((SPAN, LANES), lambda i: (i, 0))

    def body(dq_ref, dk_ref, dkp_ref, dv_ref, dvp_ref, cos_ref, sin_ref, out_ref):
        has_next = pl.program_id(0) + per_unit < nblk
        dqv = dq_ref[...]
        dkv = dk_ref[...] + jnp.where(has_next, dkp_ref[...], 0.0)
        dvv = dv_ref[...] + jnp.where(has_next, dvp_ref[...], 0.0)
        cs = _tile_lanes(cos_ref[...], ATTN_W // LANES)
        sn = _tile_lanes(sin_ref[...], ATTN_W // LANES)
        out_ref[0] = ((dqv * cs - _rope_partner(dqv) * sn) * ATTN_SCALE).astype(BF)
        out_ref[1] = (dkv * cs - _rope_partner(dkv) * sn).astype(BF)
        out_ref[2] = dvv.astype(BF)

    return pl.pallas_call(
        body, grid=(nblk,), in_specs=[here, here, ahead, here, ahead, tab, tab],
        out_specs=pl.BlockSpec((3, SPAN, ATTN_W), lambda i: (0, i, 0)),
        out_shape=jax.ShapeDtypeStruct((3, s, ATTN_W), BF),
        compiler_params=_cparams(("parallel",)), name=f"attn_combine_l{layer}",
    )(dq, dk, dkp, dv, dvp, cos, sin)


def _hgrn_bwd(layer, proj, lb, gain, o, dmixed, states):
    s = proj.shape[0]
    nblk = s // LANES
    cpb = len(CHUNKS)

    def body(q_ref, f_ref, i_ref, g_ref, lb_ref, gain_ref, o_ref, drec_ref, st_ref, dseg_ref, dlb_ref, dgain_ref,
             dstate, dst_buf):
        step = pl.program_id(0)

        @pl.when(step == 0)
        def _():
            dstate[...] = jnp.zeros(dstate.shape, F32)

        lbv, gv = lb_ref[...], gain_ref[...]
        qh, z, gate_in = q_ref[...], f_ref[...], g_ref[...]
        pre = _hgrn_pre(qh, z, lbv)
        v = i_ref[...].astype(BF)
        sg = _sigmoid(gate_in)
        ov, drec = o_ref[...], drec_ref[...]
        dnormed = drec * (gate_in * sg)
        back = [_rms_bwd(_head(dnormed, h), _head(ov, h), gv) for h in HEADS]
        do_b = jnp.concatenate([b[0] for b in back], axis=1).astype(BF)
        dgain = back[0][1] + back[1][1] + back[2][1] + back[3][1]
        normed = jnp.concatenate([_rms_fwd(_head(ov, h), gv) for h in HEADS], axis=1)
        dgate_in = drec * normed * (sg * (1.0 + gate_in * (1.0 - sg)))
        mask = _hgrn_mask()
        qt, kt, qg, kout = (pre[n].astype(BF) for n in ("qt", "kt", "qg", "kout"))
        dec = jnp.exp(pre["g_last"])
        a = [jnp.where(mask, _dot(_head(qt, h), _head(kt, h), NT), 0.0).astype(BF) for h in HEADS]
        da = [jnp.where(mask, _dot(_head(do_b, h), _head(v, h), NT), 0.0).astype(BF) for h in HEADS]
        dv_intra = [_dot(a[h], _head(do_b, h), TN) for h in HEADS]
        dqt = jnp.concatenate([_dot(da[h], _head(kt, h), NN) for h in HEADS], axis=1)
        dkt = jnp.concatenate([_dot(da[h], _head(qt, h), TN) for h in HEADS], axis=1)
        feed = [[_dot(_chunk(_head(do_b, h), c), _chunk(_head(qg, h), c), TN) for c in CHUNKS] for h in HEADS]
        for h in HEADS:
            dst = dstate[h]
            for c in reversed(CHUNKS):
                dst_buf[h, c * LANES:(c + 1) * LANES, :] = dst
                dst = dst * _head(dec, h)[c * HGRN_CHUNK:c * HGRN_CHUNK + 1, :] + feed[h][c]
            dstate[h] = dst

        def per_chunk(fn):
            cols = []
            for h in HEADS:
                rows = [jnp.broadcast_to(t, (HGRN_CHUNK, HGRN_DIM)) for t in (fn(h, c) for c in CHUNKS)]
                cols.append(jnp.concatenate(rows, axis=0))
            return jnp.concatenate(cols, axis=1)

        def st_prev(h, c):
            return st_ref[h, c * LANES:(c + 1) * LANES, :]

        def dst_at(h, c):
            return dst_buf[h, c * LANES:(c + 1) * LANES, :]

        dqg = per_chunk(lambda h, c: _dot(_chunk(_head(do_b, h), c), st_prev(h, c).astype(BF), NN))
        dkout = per_chunk(lambda h, c: _dot(_chunk(_head(v, h), c), dst_at(h, c).astype(BF), NN))
        dv_inter = per_chunk(lambda h, c: _dot(_chunk(_head(kout, h), c), dst_at(h, c).astype(BF), NT))
        dg_state = per_chunk(lambda h, c: jnp.sum(dst_at(h, c) * st_prev(h, c).astype(F32), axis=0, keepdims=True))
        dg_kout = per_chunk(lambda h, c: jnp.sum(_chunk(_head(dkout * pre["kout"], h), c), axis=0, keepdims=True))
        dv = jnp.concatenate(dv_intra, axis=1) + dv_inter
        pos = lax.broadcasted_iota(jnp.int32, (LANES, HGRN_W), 0) % HGRN_CHUNK
        dq = dqt * pre["e_q"] + dqg * pre["e_in"]
        dk = dkt * pre["e_k"] + dkout * pre["e_out"]
        dg = (dqt * pre["qt"] - dkt * pre["kt"] + dqg * pre["qg"] - dkout * pre["kout"]
              + jnp.where(pos == HGRN_CHUNK - 1, dg_state * dec + dg_kout, 0.0))
        dlogf = _chunk_cumsum(dg, reverse=True)
        sig, sq = pre["sig"], pre["sq"]
        df = dlogf / pre["f"] - dk
        dseg_ref[0] = (dq * HGRN_SCALE * (sq * (1.0 + qh * (1.0 - sq)))).astype(BF)
        dseg_ref[1] = (df * (1.0 - lbv) * sig * (1.0 - sig)).astype(BF)
        dseg_ref[2] = dv.astype(BF)
        dseg_ref[3] = dgate_in.astype(BF)
        _accumulate_rows(step, dlb_ref, _part8(df * (1.0 - sig)))
        _accumulate_rows(step, dgain_ref, _part8(dgain))

    specs, blk = _hgrn_in_specs(layer, True, nblk)
    specs += [pl.BlockSpec((LANES, HGRN_W), lambda b: (blk(b), 0)),
              pl.BlockSpec((LANES, HGRN_W), lambda b: (blk(b), 1)),
              pl.BlockSpec((HGRN_HEADS, cpb * LANES, LANES), lambda b: (0, blk(b), 0))]
    return pl.pallas_call(
        body, grid=(nblk,), in_specs=specs,
        out_specs=[pl.BlockSpec((4, LANES, HGRN_W), lambda b: (0, blk(b), 0)),
                   pl.BlockSpec((8, HGRN_W), lambda b: (0, 0)), pl.BlockSpec((8, HGRN_DIM), lambda b: (0, 0))],
        out_shape=[jax.ShapeDtypeStruct((4, s, HGRN_W), BF), jax.ShapeDtypeStruct((8, HGRN_W), F32),
                   jax.ShapeDtypeStruct((8, HGRN_DIM), F32)],
        scratch_shapes=[pltpu.VMEM((HGRN_HEADS, LANES, LANES), F32), pltpu.VMEM((HGRN_HEADS, cpb * LANES, LANES), F32)],
        compiler_params=_cparams(("arbitrary",)), name=f"hgrn_bwd_l{layer}",
    )(proj, proj, proj, proj, lb, gain, o, dmixed, states)


def _bwd_inproj(layer, dqkv, dhg, g_in, x, gain, dres, tm=256):
    s, d = x.shape

    def body(dqkv_ref, dhg_ref, w_ref, x_ref, gain_ref, dres_ref, dx_ref, dxb_ref, dgain_ref):
        acc = jnp.zeros((tm, d), F32)
        for seg in range(N_SEG):
            a = dqkv_ref[seg] if seg < 3 else dhg_ref[seg - 3]
            acc = acc + _dot(a, w_ref[seg * SEG:(seg + 1) * SEG, :], NN)
        dx, dgain = _rms_bwd(acc, x_ref[...], gain_ref[...])
        out = dres_ref[...] + dx
        dx_ref[...] = out
        dxb_ref[...] = out.astype(BF)
        _accumulate_rows(pl.program_id(0), dgain_ref, _part8(dgain))

    row = pl.BlockSpec((tm, d), lambda i: (i, 0))
    return pl.pallas_call(
        body, grid=(s // tm,),
        in_specs=[pl.BlockSpec((3, tm, SEG), lambda i: (0, i, 0)), pl.BlockSpec((4, tm, SEG), lambda i: (0, i, 0)),
                  _resident((None, PROJ_W, d), lambda i: (0, 0, 0)), row,
                  pl.BlockSpec((None, 1, d), lambda i: (layer, 0, 0)), row],
        out_specs=[row, row, pl.BlockSpec((8, d), lambda i: (0, 0))],
        out_shape=[jax.ShapeDtypeStruct((s, d), F32), jax.ShapeDtypeStruct((s, d), BF), jax.ShapeDtypeStruct((8, d), F32)],
        compiler_params=_cparams(("arbitrary",)), name=f"bwd_inproj_l{layer}",
    )(dqkv, dhg, g_in, x, gain, dres)


def _adamw(w, g, m, v):
    m2 = ADAM_B1 * m + (1.0 - ADAM_B1) * g
    v2 = ADAM_B2 * v + (1.0 - ADAM_B2) * (g * g)
    m_hat = m2 / (1.0 - ADAM_B1 ** ADAM_STEP)
    v_hat = v2 / (1.0 - ADAM_B2 ** ADAM_STEP)
    delta = -ADAM_LR * (m_hat / (jnp.sqrt(v_hat) + ADAM_EPS) + ADAM_WD * w)
    return delta, m2, v2


def _adam_big(name, parts, w, m, v, transpose, row_tiles):
    depth = w.shape[0]
    r, c = parts[0].shape[1], parts[0].shape[2]
    if transpose:
        tc = c // row_tiles
        p_spec = pl.BlockSpec((N_DEV, r, tc), lambda t: (0, 0, t))
        w_spec = pl.BlockSpec((depth, tc, r), lambda t: (0, t, 0))
    else:
        tr = r // row_tiles
        p_spec = pl.BlockSpec((N_DEV, tr, c), lambda t: (0, t, 0))
        w_spec = pl.BlockSpec((depth, tr, c), lambda t: (0, t, 0))

    def body(*refs):
        p_refs = refs[:depth]
        w_ref, m_ref, v_ref, g_ref, d_ref, m2_ref, v2_ref = refs[depth:]
        for l in range(depth):
            g = p_refs[l][0].astype(F32)
            for dev in range(1, N_DEV):
                g = g + p_refs[l][dev].astype(F32)
            if transpose:
                g = g.T
            delta, m2, v2 = _adamw(w_ref[l], g, m_ref[l], v_ref[l])
            g_ref[l] = g
            d_ref[l] = delta
            m2_ref[l] = m2
            v2_ref[l] = v2

    return pl.pallas_call(
        body, grid=(row_tiles,), in_specs=[p_spec] * depth + [w_spec] * 3, out_specs=[w_spec] * 4,
        out_shape=[jax.ShapeDtypeStruct(w.shape, F32)] * 4,
        compiler_params=_cparams(("parallel",)), name=name,
    )(*parts, w, m, v)


def _adam_small(g, w, m, v):
    def body(g_ref, w_ref, m_ref, v_ref, d_ref, m2_ref, v2_ref):
        delta, m2, v2 = _adamw(w_ref[...], g_ref[...], m_ref[...], v_ref[...])
        d_ref[...] = delta
        m2_ref[...] = m2
        v2_ref[...] = v2

    vm = pl.BlockSpec(memory_space=pltpu.VMEM)
    return pl.pallas_call(body, in_specs=[vm] * 4, out_specs=[vm] * 3, out_shape=[jax.ShapeDtypeStruct(g.shape, F32)] * 3,
                          name="adam_small")(g, w, m, v)


def _lower_bounds(logits):
    def body(l_ref, lb_ref, jac_ref):
        l0, l1 = l_ref[0:1, :], l_ref[1:2, :]
        mx = jnp.maximum(l0, l1)
        e0, e1 = jnp.exp(l0 - mx), jnp.exp(l1 - mx)
        p0, p1 = e0 / (e0 + e1), e1 / (e0 + e1)
        lb_ref[0:1, :] = p0 - p0
        lb_ref[1:2, :] = (p0 + p1) - p0
        jac_ref[0:1, :] = -p0 * p1
        jac_ref[1:2, :] = p0 * p1

    vm = pl.BlockSpec(memory_space=pltpu.VMEM)
    return pl.pallas_call(body, in_specs=[vm], out_specs=[vm, vm], out_shape=[jax.ShapeDtypeStruct(logits.shape, F32)] * 2,
                          name="hgrn_lower_bounds")(logits)


def _rope_tables(s):
    half = 32
    inv_freq = ROPE_THETA ** (-jnp.arange(half, dtype=F32) / half)
    ang = jnp.arange(s, dtype=jnp.int32).astype(F32)[:, None] * inv_freq[None, :]
    cos, sin = jnp.cos(ang), jnp.sin(ang)
    return jnp.concatenate([cos] * 4, axis=1), jnp.concatenate([-sin, sin, -sin, sin], axis=1)


SMALL_NAMES = ("norm_mix", "attn_out_gain", "hgrn_lb_logits", "hgrn_out_gain", "norm_mlp", "norm_final")


def _pack_small(vals):
    flat = jnp.concatenate([v.reshape(-1) for v in vals])
    rows = -(-flat.shape[0] // (8 * LANES)) * 8
    return jnp.pad(flat, (0, rows * LANES - flat.shape[0])).reshape(rows, LANES)


def _unpack_small(packed, like):
    flat, out, off = packed.reshape(-1), [], 0
    for v in like:
        out.append(flat[off:off + v.size].reshape(v.shape))
        off += v.size
    return out


def kernel(x, norm_mix, w_in, attn_out_gain, hgrn_lb_logits, hgrn_out_gain, w_out, norm_mlp, w_up, w_down, norm_final, loss_target, m_norm_mix, m_w_in, m_attn_out_gain, m_hgrn_lb_logits, m_hgrn_out_gain, m_w_out, m_norm_mlp, m_w_up, m_w_down, m_norm_final, v_norm_mix, v_w_in, v_attn_out_gain, v_hgrn_lb_logits, v_hgrn_out_gain, v_w_out, v_norm_mlp, v_w_up, v_w_down, v_norm_final):
    depth = w_in.shape[0]
    assert depth == 2 and x.shape[0] == 1
    s, d = x.shape[1], x.shape[2]
    x0 = x[0]
    target = loss_target[0]
    cos, sin = _rope_tables(s)
    g_mix, g_attn, g_hg, g_mlp = (norm_mix[:, None, :], attn_out_gain[:, None, :], hgrn_out_gain[:, None, :],
                                  norm_mlp[:, None, :])
    lb, lb_jac = _lower_bounds(hgrn_lb_logits)
    lb3 = lb[:, None, :]

    shards = list(_pack_weights(w_in, w_out, w_up, w_down))
    w_pieces = _weight_pieces(*shards)
    w_groups = [[0], [1, 2, 3], [4], [5, 6, 7]]
    me = (4 * lax.axis_index("x") + 2 * lax.axis_index("y") + lax.axis_index("c")).astype(jnp.int32).reshape(1)
    lands = _exchange_own("all_gather_own", me, shards, w_pieces)
    w_sems, shards, lands, token = _exchange_start("all_gather_start", shards, lands, w_pieces, w_groups)

    def weights_ready(group, after):
        nonlocal shards
        idxs = w_groups[group]
        shards, got = _exchange_wait(f"all_gather_wait{group}", shards, [lands[i] for i in idxs], w_pieces,
                                     [(idxs, *w_sems[group])], after)
        return got

    def tied(small_arr, tok):
        return small_arr + tok[0, 0]

    saved = []
    xl = x0
    full = [None] * depth
    for l in range(depth):
        (full_in,) = weights_ready(2 * l, token if l == 0 else xl)
        proj, h = _fwd_inproj(l, xl, g_mix, full_in, cos, sin)
        o_attn, lse = _attn_fwd(l, proj)
        o_hg, mixed, states = _hgrn_fwd(l, proj, lb3, g_hg)
        mixed = _attn_norm(l, o_attn, g_attn, mixed)
        full_out, full_up, full_down = weights_ready(2 * l + 1, mixed)
        x_mid = _fwd_outproj(l, mixed, full_out, xl)
        x_next, u, h2 = _mlp_fwd(l, x_mid, g_mlp, full_up, full_down)
        saved.append((xl, proj, h, o_attn, lse, o_hg, states, mixed, x_mid, u, h2))
        full[l] = (full_in, full_out, full_up, full_down)
        xl = x_next
    dx, dxb, dnorm_final8, loss8 = _loss_head(xl, norm_final[None, :], target)
    loss = lax.psum(jnp.sum(loss8[:, 0]), ("x", "y", "c"))

    exchanges = []

    def scatter(tag, grads, kinds):
        pieces = _grad_pieces(grads, kinds)
        own = _exchange_own(f"reduce_scatter_own_{tag}", me, grads, pieces)
        sems, grads, own, tok = _exchange_start(f"reduce_scatter_start_{tag}", grads, own, pieces, [list(range(len(pieces)))])
        exchanges.append((grads, own, pieces, sems[0]))
        return tok

    small = {}
    for l in reversed(range(depth)):
        xl, proj, h, o_attn, lse, o_hg, states, mixed, x_mid, u, h2 = saved[l]
        full_in, full_out, full_up, full_down = full[l]
        hs = full_up.shape[3]
        gw_down = _mm_tn(f"grad_w_down_l{l}", u, dxb, u.shape[1], a_fn=_relu2)
        dx_mid, dx_mid_b, du, dmlp8 = _mlp_bwd(l, dx, dxb, x_mid, g_mlp, u, full_up, full_down)
        gw_up = _mm_tn(f"grad_w_up_l{l}", h2, du, d, out_block_w=hs)
        gw_out = _mm_tn(f"grad_w_out_l{l}", mixed, dx_mid_b, mixed.shape[1])
        g_attn_t = tied(g_attn, scatter(f"mlp_l{l}", [gw_down, gw_up, gw_out], ["rows", "up", "rows"]))
        dmixed = _bwd_outproj(l, dx_mid_b, full_out)
        do, delta, dattn8 = _attn_norm_bwd(l, dmixed, o_attn, g_attn_t)
        dqkv = _attn_combine(l, _attn_bwd(l, proj, do, lse, delta), cos, sin)
        dhg, dlb8, dhgain8 = _hgrn_bwd(l, proj, lb3, g_hg, o_hg, dmixed, states)
        gin = _mm_tn(f"grad_w_in_qkv_l{l}", dqkv, h, PROJ_W, a_lead=True)
        gw_in = _mm_tn(f"grad_w_in_hg_l{l}", dhg, h, PROJ_W, a_lead=True, out_block_off=3, prev=gin)
        g_mix_t = tied(g_mix, scatter(f"mix_l{l}", [gw_in], ["rows"]))
        dx, dxb, dmix8 = _bwd_inproj(l, dqkv, dhg, full_in, xl, g_mix_t, dx_mid)
        small[l] = (dmix8, dattn8, dlb8, dhgain8, dmlp8)

    def fin(p8):
        return jnp.sum(p8, axis=0)
    dlogits = lb_jac * fin(small[1][2])[None, :]
    small_grads = [jnp.stack([fin(small[l][0]) for l in range(depth)]), jnp.stack([fin(small[l][1]) for l in range(depth)]),
                   dlogits, jnp.stack([fin(small[l][3]) for l in range(depth)]),
                   jnp.stack([fin(small[l][4]) for l in range(depth)]), fin(dnorm_final8)]
    small_w = [norm_mix, attn_out_gain, hgrn_lb_logits, hgrn_out_gain, norm_mlp, norm_final]
    small_m = [m_norm_mix, m_attn_out_gain, m_hgrn_lb_logits, m_hgrn_out_gain, m_norm_mlp, m_norm_final]
    small_v = [v_norm_mix, v_attn_out_gain, v_hgrn_lb_logits, v_hgrn_out_gain, v_norm_mlp, v_norm_final]
    g_small = _all_reduce_small(_pack_small(small_grads))
    d_small, m_small, v_small = _adam_small(g_small, _pack_small(small_w), _pack_small(small_m), _pack_small(small_v))
    gs, ds, ms, vs = (_unpack_small(t, small_w) for t in (g_small, d_small, m_small, v_small))

    all_grads, all_lands, all_pieces, waits = [], [], [], []
    for grads, own, pieces, (send, recv) in exchanges:
        first = len(all_pieces)
        all_pieces += [p._replace(src=p.src + len(all_grads)) for p in pieces]
        waits.append((list(range(first, first + len(pieces))), send, recv))
        all_grads += grads
        all_lands += own
    _, landed = _exchange_wait("reduce_scatter_wait", all_grads, all_lands, all_pieces, waits, d_small)
    r_down, r_up, r_out, r_in = ([landed[4 + i], landed[i]] for i in range(4))
    big = {
        "w_in": _adam_big("adam_w_in", r_in, w_in, m_w_in, v_w_in, True, 4),
        "w_out": _adam_big("adam_w_out", r_out, w_out, m_w_out, v_w_out, False, 1),
        "w_up": _adam_big("adam_w_up", r_up, w_up, m_w_up, v_w_up, False, 2),
        "w_down": _adam_big("adam_w_down", r_down, w_down, m_w_down, v_w_down, False, 4),
    }

    def gather(idx, small_list):
        by_name = dict(zip(SMALL_NAMES, small_list))
        return [by_name["norm_mix"], big["w_in"][idx], by_name["attn_out_gain"], by_name["hgrn_lb_logits"],
                by_name["hgrn_out_gain"], big["w_out"][idx], by_name["norm_mlp"], big["w_up"][idx], big["w_down"][idx],
                by_name["norm_final"]]

    return (loss, dx[None], *gather(0, gs), *gather(1, ds), *gather(2, ms), *gather(3, vs))
```

```python
import functools
from typing import Callable, NamedTuple

import jax
import jax.numpy as jnp
from jax import lax
from jax.experimental import pallas as pl
from jax.experimental.pallas import tpu as pltpu

F32 = jnp.float32
BF = jnp.bfloat16

N_DEV = 8
ATTN_W = 512
HGRN_W = 512
HGRN_HEADS = 4
HGRN_DIM = 128
SEG = 512
N_SEG = 7
PROJ_W = N_SEG * SEG
MIX_W = ATTN_W + HGRN_W
SPAN = 128
DILATIONS = (1, 4, 16)
HGRN_CHUNK = 16
ROPE_THETA = 10000.0
NORM_EPS = 1e-6
MASK_VALUE = -1e30
ATTN_SCALE = 0.125
HGRN_SCALE = HGRN_DIM ** -0.5
ADAM_LR = 0.001
ADAM_B1 = 0.9
ADAM_B2 = 0.999
ADAM_EPS = 1e-08
ADAM_WD = 0.01
ADAM_STEP = 10
LANES = 128
VMEM_LIMIT = 56 * 1024 * 1024

NN = ((1,), (0,))
NT = ((1,), (1,))
TN = ((0,), (0,))
MESH = pl.DeviceIdType.MESH


def _dot(a, b, dims):
    return lax.dot_general(a, b, (dims, ((), ())), preferred_element_type=F32)


def _cparams(sem):
    return pltpu.CompilerParams(dimension_semantics=sem, vmem_limit_bytes=VMEM_LIMIT)


def _part8(x):
    r, n = x.shape
    return jnp.sum(x.reshape(r // 8, 8, n), axis=0)


def _sigmoid(x):
    return 1.0 / (1.0 + jnp.exp(-x))


def _rms_fwd(x, gain):
    r = lax.rsqrt(jnp.mean(x * x, axis=-1, keepdims=True) + NORM_EPS)
    return x * r * gain


def _rms_bwd(dy, x, gain):
    r = lax.rsqrt(jnp.mean(x * x, axis=-1, keepdims=True) + NORM_EPS)
    xn = x * r
    dxn = dy * gain
    dx = r * (dxn - xn * jnp.mean(dxn * xn, axis=-1, keepdims=True))
    return dx, dy * xn


def _rope_partner(x):
    n = x.shape[-1]
    lane = lax.broadcasted_iota(jnp.int32, x.shape, x.ndim - 1)
    return jnp.where((lane % 64) < 32, pltpu.roll(x, n - 32, x.ndim - 1), pltpu.roll(x, 32, x.ndim - 1))


def _tile_lanes(t, reps):
    return jnp.concatenate([t] * reps, axis=-1)


def _mm_tn(name, a, b, out_rows, a_lead=False, out_block_off=0, prev=None, out_block_w=None, a_fn=None,
           tm=512, tn=1024, sub=512):
    kdim, n = b.shape
    m = a.shape[-1]
    tm, tn, sub = min(tm, m), min(tn, n), min(sub, kdim)
    mt = m // tm
    n_lead = a.shape[0] if a_lead else 1
    if a_lead:
        a_spec = pl.BlockSpec((None, kdim, tm), lambda j, i: (i // mt, 0, i % mt))
    else:
        a_spec = pl.BlockSpec((kdim, tm), lambda j, i: (0, i))
    b_spec = pl.BlockSpec((kdim, tn), lambda j, i: (0, j))
    if out_block_w:
        nb = tn // out_block_w
        o_shape = jax.ShapeDtypeStruct((n // out_block_w, out_rows, out_block_w), BF)
        o_spec = pl.BlockSpec((nb, tm, out_block_w), lambda j, i: (j, i + out_block_off, 0))
    else:
        nb = 0
        o_shape = jax.ShapeDtypeStruct((out_rows, n), BF)
        o_spec = pl.BlockSpec((tm, tn), lambda j, i: (i + out_block_off, j))
    arrays, specs, aliases = [a, b], [a_spec, b_spec], {}
    if prev is not None:
        arrays.append(prev)
        specs.append(pl.BlockSpec(memory_space=pl.ANY))
        aliases = {2: 0}

    def body(*refs):
        a_ref, b_ref, o_ref = refs[0], refs[1], refs[-1]
        acc = None
        for k in range(kdim // sub):
            av = a_ref[k * sub:(k + 1) * sub, :]
            if a_fn is not None:
                av = a_fn(av)
            part = _dot(av, b_ref[k * sub:(k + 1) * sub, :], TN)
            acc = part if acc is None else acc + part
        if nb:
            for t in range(nb):
                o_ref[t] = acc[:, t * out_block_w:(t + 1) * out_block_w].astype(BF)
        else:
            o_ref[...] = acc.astype(BF)

    return pl.pallas_call(
        body, grid=(n // tn, n_lead * mt), in_specs=specs, out_specs=o_spec, out_shape=o_shape,
        compiler_params=_cparams(("parallel", "parallel")), name=name, input_output_aliases=aliases,
    )(*arrays)


def _pack_weights(w_in_t, w_out, w_up, w_down):
    depth = w_in_t.shape[0]
    arrays = (w_in_t, w_out, w_up, w_down)

    def body(*refs):
        for src, dst in zip(refs[:4], refs[4:]):
            dst[...] = src[...].astype(BF)

    specs = [pl.BlockSpec((None,) + a.shape[1:], lambda l: (l, 0, 0)) for a in arrays]
    return pl.pallas_call(
        body, grid=(depth,), in_specs=specs, out_specs=specs,
        out_shape=[jax.ShapeDtypeStruct(a.shape, BF) for a in arrays],
        compiler_params=_cparams(("arbitrary",)), name="pack_weights",
    )(*arrays)


def _my_position():
    x, y, c = lax.axis_index("x"), lax.axis_index("y"), lax.axis_index("c")
    return x, y, c, 4 * x + 2 * y + c


def _peer(x, y, c, k):
    px = 1 - x if k & 4 else x
    py = 1 - y if k & 2 else y
    pc = 1 - c if k & 1 else c
    return (px, py, pc), 4 * px + 2 * py + pc


PEER_ORDER = (1, 2, 4, 3, 5, 6, 7)


class _Piece(NamedTuple):
    src: int
    send: Callable
    slot: Callable
    land_shape: tuple
    own_src: tuple
    own_slot: tuple


HBM_SPEC = pl.BlockSpec(memory_space=pltpu.HBM)
SEM_SPEC = pl.BlockSpec(memory_space=pltpu.SEMAPHORE)
ANY_SPEC = pl.BlockSpec(memory_space=pl.ANY)


def _in_hbm(arrays):
    return [pltpu.with_memory_space_constraint(a, pltpu.HBM) for a in arrays]


def _hbm_like(arrays):
    return [pltpu.HBM(a.shape, a.dtype) for a in arrays]


def _rows_of(rows):
    return lambda ref, dev: ref.at[pl.ds(pl.multiple_of(dev * rows, 16), rows), :]


def _exchange_own(name, me, srcs, pieces):
    n = len(pieces)

    def body(me_ref, *refs):
        for i in range(n):
            refs[n + i][...] = refs[i][...]

    def spec(block_and_index):
        block, index = block_and_index
        return pl.BlockSpec(block, lambda i, me_ref: index(me_ref[0]))

    return pl.pallas_call(
        body,
        grid_spec=pltpu.PrefetchScalarGridSpec(
            num_scalar_prefetch=1, grid=(1,), in_specs=[spec(p.own_src) for p in pieces],
            out_specs=[spec(p.own_slot) for p in pieces]),
        out_shape=[jax.ShapeDtypeStruct(p.land_shape, BF) for p in pieces],
        compiler_params=_cparams(("arbitrary",)), name=name,
    )(me, *[srcs[p.src] for p in pieces])


def _exchange_start(name, srcs, lands, pieces, groups):
    n_src, n, n_g = len(srcs), len(pieces), len(groups)

    def body(*refs):
        src_refs, land_refs = refs[:n_src], refs[n_src:n_src + n]
        sems, token = refs[n_src + n:n_src + n + 2 * n_g], refs[-1]
        x, y, c, me = _my_position()
        for g, idxs in enumerate(groups):
            for k in PEER_ORDER:
                peer, pid = _peer(x, y, c, k)
                for j, i in enumerate(idxs):
                    p = pieces[i]
                    pltpu.make_async_remote_copy(
                        src_ref=p.send(src_refs[p.src], pid), dst_ref=p.slot(land_refs[i], me),
                        send_sem=sems[2 * g].at[(k - 1) * len(idxs) + j], recv_sem=sems[2 * g + 1].at[(k - 1) * len(idxs) + j],
                        device_id=peer, device_id_type=MESH).start()
        token[...] = jnp.zeros(token.shape, F32)

    sem_shapes = [pltpu.SemaphoreType.DMA(((N_DEV - 1) * len(idxs),)) for idxs in groups for _ in range(2)]
    res = pl.pallas_call(
        body, in_specs=[HBM_SPEC] * (n_src + n),
        out_specs=[SEM_SPEC] * (2 * n_g) + [HBM_SPEC] * (n_src + n) + [pl.BlockSpec(memory_space=pltpu.VMEM)],
        out_shape=sem_shapes + _hbm_like(srcs) + _hbm_like(lands) + [jax.ShapeDtypeStruct((8, LANES), F32)],
        input_output_aliases={i: 2 * n_g + i for i in range(n_src + n)},
        compiler_params=pltpu.CompilerParams(has_side_effects=pltpu.SideEffectType.DATAFLOW_SIDE_EFFECTING),
        name=name,
    )(*_in_hbm(srcs), *_in_hbm(lands))
    sems = [(res[2 * g], res[2 * g + 1]) for g in range(n_g)]
    return sems, list(res[2 * n_g:2 * n_g + n_src]), list(res[2 * n_g + n_src:2 * n_g + n_src + n]), res[-1]


def _exchange_wait(name, srcs, lands, pieces, waits, after):
    n_src, n, n_g = len(srcs), len(lands), len(waits)

    def body(*refs):
        src_refs, land_refs = refs[:n_src], refs[n_src:n_src + n]
        sems = refs[n_src + n:n_src + n + 2 * n_g]
        x, y, c, me = _my_position()
        at = 0
        for g, (idxs, _, _) in enumerate(waits):
            for k in PEER_ORDER:
                peer, pid = _peer(x, y, c, k)
                for j, i in enumerate(idxs):
                    p = pieces[i]
                    cp = pltpu.make_async_remote_copy(
                        src_ref=p.send(src_refs[p.src], pid), dst_ref=p.slot(land_refs[at + j], pid),
                        send_sem=sems[2 * g].at[(k - 1) * len(idxs) + j], recv_sem=sems[2 * g + 1].at[(k - 1) * len(idxs) + j],
                        device_id=peer, device_id_type=MESH)
                    cp.wait_send()
                    cp.wait_recv()
            at += len(idxs)

    sem_args = [s for _, send, recv in waits for s in (send, recv)]
    res = pl.pallas_call(
        body, in_specs=[HBM_SPEC] * (n_src + n) + [SEM_SPEC] * (2 * n_g) + [ANY_SPEC],
        out_specs=[HBM_SPEC] * (n_src + n), out_shape=_hbm_like(srcs) + _hbm_like(lands),
        input_output_aliases={i: i for i in range(n_src + n)},
        compiler_params=pltpu.CompilerParams(has_side_effects=pltpu.SideEffectType.DATAFLOW_SIDE_EFFECTING),
        name=name,
    )(*srcs, *lands, *sem_args, after)
    return list(res[:n_src]), list(res[n_src:])


def _weight_pieces(p_in, p_out, p_up, p_down):
    depth, cin, d = p_in.shape
    rout, hs = p_out.shape[1], p_up.shape[2]
    pieces = []
    for l in range(depth):
        whole = functools.partial(lambda ref, dev, l: ref.at[l], l=l)
        layer = functools.partial(lambda dev, l: (l, 0, 0), l=l)

        def rows(src, n_rows, whole=whole, layer=layer):
            return _Piece(src, whole, lambda ref, dev: _rows_of(n_rows)(ref.at[0], dev), (1, N_DEV * n_rows, d),
                          ((None, n_rows, d), layer), ((None, n_rows, d), lambda dev: (0, dev, 0)))

        pieces += [
            rows(0, cin), rows(1, rout),
            _Piece(2, whole, lambda ref, dev: ref.at[0, dev], (1, N_DEV, d, hs),
                   ((None, d, hs), layer), ((None, None, d, hs), lambda dev: (0, dev, 0, 0))),
            rows(3, hs),
        ]
    return pieces


def _grad_pieces(g_pair, kinds):
    pieces = []
    for i, (g, kind) in enumerate(zip(g_pair, kinds)):
        lead = lambda dev: (dev, 0, 0)
        if kind == "up":
            blk = ((None,) + g.shape[1:], lead)
            pieces.append(_Piece(i, lambda ref, dev: ref.at[dev], lambda ref, dev: ref.at[dev], g.shape, blk, blk))
        else:
            rows, cols = g.shape[0] // N_DEV, g.shape[1]
            pieces.append(_Piece(i, _rows_of(rows), lambda ref, dev: ref.at[dev], (N_DEV, rows, cols),
                                 ((rows, cols), lambda dev: (dev, 0)), ((None, rows, cols), lead)))
    return pieces


def _all_reduce_small(vec):
    rows = vec.shape[0]

    def body(v_ref, o_ref, buf_ref, send_sems, recv_sems):
        x, y, c, me = _my_position()
        buf_ref[me] = v_ref[...]
        sends = []
        for k in PEER_ORDER:
            peer, _ = _peer(x, y, c, k)
            cp = pltpu.make_async_remote_copy(src_ref=v_ref, dst_ref=buf_ref.at[me], send_sem=send_sems.at[k - 1],
                                              recv_sem=recv_sems.at[k - 1], device_id=peer, device_id_type=MESH)
            cp.start()
            sends.append(cp)
        for k in PEER_ORDER:
            peer, pid = _peer(x, y, c, k)
            pltpu.make_async_remote_copy(src_ref=v_ref, dst_ref=buf_ref.at[pid], send_sem=send_sems.at[k - 1],
                                         recv_sem=recv_sems.at[k - 1], device_id=peer, device_id_type=MESH).wait_recv()
        for cp in sends:
            cp.wait_send()
        total = buf_ref[0]
        for dev in range(1, N_DEV):
            total = total + buf_ref[dev]
        o_ref[...] = total

    vm = pl.BlockSpec(memory_space=pltpu.VMEM)
    return pl.pallas_call(
        body, in_specs=[vm], out_specs=vm, out_shape=jax.ShapeDtypeStruct(vec.shape, F32),
        scratch_shapes=[pltpu.VMEM((N_DEV, rows, LANES), F32), pltpu.SemaphoreType.DMA((N_DEV - 1,)),
                        pltpu.SemaphoreType.DMA((N_DEV - 1,))],
        name="all_reduce_small",
    )(vec)


def _resident(block_shape, index_map):
    return pl.BlockSpec(block_shape, index_map, pipeline_mode=pl.Buffered(1))


def _fwd_inproj(layer, x, gain, g_in, cos, sin, tm=256):
    s, d = x.shape

    def body(x_ref, gain_ref, w_ref, cos_ref, sin_ref, proj_ref, h_ref):
        h = _rms_fwd(x_ref[...], gain_ref[...]).astype(BF)
        h_ref[...] = h
        cs = _tile_lanes(cos_ref[...], SEG // LANES)
        sn = _tile_lanes(sin_ref[...], SEG // LANES)
        for seg in range(N_SEG):
            acc = _dot(h, w_ref[seg * SEG:(seg + 1) * SEG, :], NT)
            if seg < 2:
                acc = acc * cs + _rope_partner(acc) * sn
            if seg == 0:
                acc = acc * ATTN_SCALE
            proj_ref[:, seg * SEG:(seg + 1) * SEG] = acc

    return pl.pallas_call(
        body, grid=(s // tm,),
        in_specs=[pl.BlockSpec((tm, d), lambda i: (i, 0)), pl.BlockSpec((None, 1, d), lambda i: (layer, 0, 0)),
                  _resident((None, PROJ_W, d), lambda i: (0, 0, 0)),
                  pl.BlockSpec((tm, LANES), lambda i: (i, 0)), pl.BlockSpec((tm, LANES), lambda i: (i, 0))],
        out_specs=[pl.BlockSpec((tm, PROJ_W), lambda i: (i, 0)), pl.BlockSpec((tm, d), lambda i: (i, 0))],
        out_shape=[jax.ShapeDtypeStruct((s, PROJ_W), F32), jax.ShapeDtypeStruct((s, d), BF)],
        compiler_params=_cparams(("parallel",)), name=f"fwd_inproj_l{layer}",
    )(x, gain, g_in, cos, sin)


ATTN_UNIT = SPAN * max(DILATIONS)
ATTN_GROUP = 4


def _attn_masks(first_block_has_prev):
    row = lax.broadcasted_iota(jnp.int32, (SPAN, 2 * SPAN), 0)
    col = lax.broadcasted_iota(jnp.int32, (SPAN, 2 * SPAN), 1)
    band = (col >= row) & (col <= row + SPAN)
    lane = lax.broadcasted_iota(jnp.int32, (SPAN, LANES), 1)
    return band & ((col >= SPAN) | first_block_has_prev), band, lane < 64


def _attn_specs(n_in_extra):
    pairs = ATTN_W // LANES
    q_spec = pl.BlockSpec((ATTN_UNIT, LANES), lambda p, n: (n, p))

    def prev(seg):
        return pl.BlockSpec((ATTN_UNIT, LANES), lambda p, n: (jnp.maximum(n - 1, 0), seg * pairs + p))

    def cur(seg):
        return pl.BlockSpec((ATTN_UNIT, LANES), lambda p, n: (n, seg * pairs + p))

    return [q_spec, prev(1), cur(1), prev(2), cur(2)] + [q_spec] * n_in_extra


def _attn_groups(dil):
    blocks = ATTN_UNIT // (SPAN * dil)
    pairs = [(r, b) for r in range(dil) for b in range(blocks)]
    return [pairs[i:i + ATTN_GROUP] for i in range(0, len(pairs), ATTN_GROUP)]


def _block_rows(dil, r, b, n=1):
    start = r + dil * SPAN * b
    return pl.ds(start, n * SPAN, stride=dil) if dil > 1 else pl.ds(start, n * SPAN)


def _block_keys(prev_ref, cur_ref, dil, r, b):
    if b > 0:
        return cur_ref[_block_rows(dil, r, b - 1, 2), :]
    last = ATTN_UNIT // (SPAN * dil) - 1
    return jnp.concatenate([prev_ref[_block_rows(dil, r, last), :], cur_ref[_block_rows(dil, r, 0), :]], axis=0)


def _attn_fwd(layer, proj):
    s = proj.shape[0]
    n_pat = len(DILATIONS)
    merge_rows = 256

    def body(q_ref, kp_ref, kc_ref, vp_ref, vc_ref, o_ref, lse_ref, o_scr, lse_scr):
        m_first, m_rest, is_a = _attn_masks(pl.program_id(1) > 0)
        sels = (is_a, jnp.logical_not(is_a))
        ones = jnp.ones((2 * SPAN, LANES), BF)
        for pi, dil in enumerate(DILATIONS):
            for group in _attn_groups(dil):
                items = [(r, b, h) for r, b in group for h in range(2)]
                q = {rb: q_ref[_block_rows(dil, *rb), :] for rb in group}
                k = {rb: _block_keys(kp_ref, kc_ref, dil, *rb).astype(BF) for rb in group}
                v = {rb: _block_keys(vp_ref, vc_ref, dil, *rb).astype(BF) for rb in group}
                sc = [jnp.where(m_first if b == 0 else m_rest,
                                _dot(jnp.where(sels[h], q[r, b], 0.0).astype(BF), k[r, b], NT), MASK_VALUE)
                      for r, b, h in items]
                mx = [jnp.max(jnp.maximum(t[:, :SPAN], t[:, SPAN:]), axis=-1, keepdims=True) for t in sc]
                p = [jnp.exp(t - m).astype(BF) for t, m in zip(sc, mx)]
                den = [_dot(t, ones, NN) for t in p]
                num = [_dot(t, v[r, b], NN) for t, (r, b, h) in zip(p, items)]
                o = [a / d for a, d in zip(num, den)]
                lse = [m + jnp.log(d) for m, d in zip(mx, den)]
                for j, (r, b) in enumerate(group):
                    o_scr[pi, _block_rows(dil, r, b), :] = jnp.where(is_a, o[2 * j], o[2 * j + 1])
                    lse_scr[pi, _block_rows(dil, r, b), :] = jnp.where(is_a, lse[2 * j], lse[2 * j + 1])
        for c in range(ATTN_UNIT // merge_rows):
            rows = slice(c * merge_rows, (c + 1) * merge_rows)
            ls = [lse_scr[pi, rows, :] for pi in range(n_pat)]
            mx = functools.reduce(jnp.maximum, ls)
            ws = [jnp.exp(l - mx) for l in ls]
            den = functools.reduce(jnp.add, ws)
            o_ref[rows, :] = functools.reduce(jnp.add, [w * o_scr[pi, rows, :] for pi, w in enumerate(ws)]) / den
            lse_ref[rows, :] = mx + jnp.log(den)

    out_spec = pl.BlockSpec((ATTN_UNIT, LANES), lambda p, n: (n, p))
    return pl.pallas_call(
        body, grid=(ATTN_W // LANES, s // ATTN_UNIT), in_specs=_attn_specs(0), out_specs=[out_spec, out_spec],
        out_shape=[jax.ShapeDtypeStruct((s, ATTN_W), F32)] * 2,
        scratch_shapes=[pltpu.VMEM((n_pat, ATTN_UNIT, LANES), F32)] * 2,
        compiler_params=_cparams(("parallel", "arbitrary")), name=f"attn_fwd_l{layer}",
    )(proj, proj, proj, proj, proj)


def _attn_norm(layer, o, gain, mixed, tm=512):
    s = o.shape[0]

    def body(o_ref, gain_ref, mixed_ref, n_ref):
        n_ref[...] = _rms_fwd(o_ref[...], gain_ref[...]).astype(BF)

    blk = pl.BlockSpec((tm, ATTN_W), lambda i: (i, 0))
    return pl.pallas_call(
        body, grid=(s // tm,),
        in_specs=[blk, pl.BlockSpec((None, 1, ATTN_W), lambda i: (layer, 0, 0)), pl.BlockSpec(memory_space=pl.ANY)],
        out_specs=blk, out_shape=jax.ShapeDtypeStruct(mixed.shape, BF), input_output_aliases={2: 0},
        compiler_params=_cparams(("parallel",)), name=f"attn_norm_l{layer}",
    )(o, gain, mixed)


def _chunk_cumsum(x, reverse=False):
    n = x.shape[0]
    pos = lax.broadcasted_iota(jnp.int32, x.shape, 0) % HGRN_CHUNK
    for sh in (1, 2, 4, 8):
        if reverse:
            x = x + jnp.where(pos < HGRN_CHUNK - sh, pltpu.roll(x, n - sh, 0), 0.0)
        else:
            x = x + jnp.where(pos >= sh, pltpu.roll(x, sh, 0), 0.0)
    return x


def _chunk_row(x, row):
    r, n = x.shape
    x3 = x.reshape(r // HGRN_CHUNK, HGRN_CHUNK, n)
    return jnp.broadcast_to(x3[:, row:row + 1, :], x3.shape).reshape(r, n)


def _hgrn_pre(qh, z, lb):
    sig = _sigmoid(z)
    f = lb + (1.0 - lb) * sig
    k = (1.0 - lb) * _sigmoid(-z)
    sq = _sigmoid(qh)
    q = qh * sq * HGRN_SCALE
    g = _chunk_cumsum(jnp.log(f))
    g_mid = _chunk_row(g, HGRN_CHUNK // 2 - 1)
    g_last = _chunk_row(g, HGRN_CHUNK - 1)
    e_q, e_k = jnp.exp(g - g_mid), jnp.exp(g_mid - g)
    e_in, e_out = jnp.exp(g), jnp.exp(g_last - g)
    return dict(sig=sig, f=f, k=k, sq=sq, q=q, g_last=g_last, e_q=e_q, e_k=e_k, e_in=e_in, e_out=e_out,
                qt=q * e_q, kt=k * e_k, qg=q * e_in, kout=k * e_out)


def _hgrn_mask():
    row = lax.broadcasted_iota(jnp.int32, (LANES, LANES), 0)
    col = lax.broadcasted_iota(jnp.int32, (LANES, LANES), 1)
    return (row // HGRN_CHUNK == col // HGRN_CHUNK) & (col <= row)


def _hgrn_in_specs(layer, rev, nblk):
    def blk(b):
        return nblk - 1 - b if rev else b
    first = 3 * ATTN_W // HGRN_W
    specs = [pl.BlockSpec((HGRN_ROWS, HGRN_W), functools.partial(lambda b, seg: (blk(b), first + seg), seg=seg))
             for seg in range(4)]
    specs.append(pl.BlockSpec((None, 1, HGRN_W), lambda b: (layer, 0, 0)))
    specs.append(pl.BlockSpec((None, 1, HGRN_DIM), lambda b: (layer, 0, 0)))
    return specs, blk


def _head(x, h):
    return x[:, h * HGRN_DIM:(h + 1) * HGRN_DIM]


def _chunk(x, c):
    return x[c * HGRN_CHUNK:(c + 1) * HGRN_CHUNK]


def _sub(x, sb):
    return x[sb * LANES:(sb + 1) * LANES]


HGRN_ROWS = 256
HEADS = range(HGRN_HEADS)
SUBS = range(HGRN_ROWS // LANES)
CHUNKS = range(HGRN_ROWS // HGRN_CHUNK)


def _hgrn_fwd(layer, proj, lb, gain):
    s = proj.shape[0]
    nblk = s // HGRN_ROWS
    cpb = len(CHUNKS)

    def body(q_ref, f_ref, i_ref, g_ref, lb_ref, gain_ref, o_ref, rec_ref, st_ref, state):
        @pl.when(pl.program_id(0) == 0)
        def _():
            state[...] = jnp.zeros(state.shape, F32)

        pre = _hgrn_pre(q_ref[...], f_ref[...], lb_ref[...])
        v = i_ref[...].astype(BF)
        qt, kt, qg, kout = (pre[n].astype(BF) for n in ("qt", "kt", "qg", "kout"))
        dec = jnp.exp(pre["g_last"])
        mask = _hgrn_mask()
        a = [[jnp.where(mask, _dot(_sub(_head(qt, h), sb), _sub(_head(kt, h), sb), NT), 0.0).astype(BF) for sb in SUBS]
             for h in HEADS]
        o_intra = [[_dot(a[h][sb], _sub(_head(v, h), sb), NN) for sb in SUBS] for h in HEADS]
        update = [[_dot(_chunk(_head(v, h), c), _chunk(_head(kout, h), c), TN) for c in CHUNKS] for h in HEADS]
        for h in HEADS:
            st = state[h]
            for c in CHUNKS:
                st_ref[h, c * LANES:(c + 1) * LANES, :] = st.astype(BF)
                st = st * _head(dec, h)[c * HGRN_CHUNK:c * HGRN_CHUNK + 1, :] + update[h][c]
            state[h] = st
        inter = [[_dot(_chunk(_head(qg, h), c), st_ref[h, c * LANES:(c + 1) * LANES, :].astype(BF), NT) for c in CHUNKS]
                 for h in HEADS]
        o = [jnp.concatenate(o_intra[h], axis=0) + jnp.concatenate(inter[h], axis=0) for h in HEADS]
        o_ref[...] = jnp.concatenate(o, axis=1)
        gate = g_ref[...]
        normed = jnp.concatenate([_rms_fwd(o[h], gain_ref[...]) for h in HEADS], axis=1)
        rec_ref[...] = (normed * (gate * _sigmoid(gate))).astype(BF)

    specs, _ = _hgrn_in_specs(layer, False, nblk)
    return pl.pallas_call(
        body, grid=(nblk,), in_specs=specs,
        out_specs=[pl.BlockSpec((HGRN_ROWS, HGRN_W), lambda b: (b, 0)), pl.BlockSpec((HGRN_ROWS, HGRN_W), lambda b: (b, 1)),
                   pl.BlockSpec((HGRN_HEADS, cpb * LANES, LANES), lambda b: (0, b, 0))],
        out_shape=[jax.ShapeDtypeStruct((s, HGRN_W), F32), jax.ShapeDtypeStruct((s, MIX_W), BF),
                   jax.ShapeDtypeStruct((HGRN_HEADS, nblk * cpb * LANES, LANES), BF)],
        scratch_shapes=[pltpu.VMEM((HGRN_HEADS, LANES, LANES), F32)],
        compiler_params=_cparams(("arbitrary",)), name=f"hgrn_fwd_l{layer}",
    )(proj, proj, proj, proj, lb, gain)


def _fwd_outproj(layer, mixed, g_out, x, tm=512):
    s, d = x.shape
    mw = mixed.shape[1]

    def body(m_ref, w_ref, x_ref, o_ref):
        o_ref[...] = x_ref[...] + _dot(m_ref[...], w_ref[...], NN)

    row = pl.BlockSpec((tm, d), lambda i: (i, 0))
    return pl.pallas_call(
        body, grid=(s // tm,),
        in_specs=[pl.BlockSpec((tm, mw), lambda i: (i, 0)), _resident((None, mw, d), lambda i: (0, 0, 0)), row],
        out_specs=row, out_shape=jax.ShapeDtypeStruct((s, d), F32),
        compiler_params=_cparams(("parallel",)), name=f"fwd_outproj_l{layer}",
    )(mixed, g_out, x)


def _relu2(u):
    return jnp.square(jnp.maximum(u, 0)).astype(BF)


def _mlp_fwd(layer, x, gain, g_up, g_down, tm=256):
    s, d = x.shape
    nblk, hs = g_up.shape[1], g_up.shape[3]

    def body(x_ref, gain_ref, up_ref, down_ref, o_ref, u_ref, h_ref, a_buf):
        xv = x_ref[...]
        h = _rms_fwd(xv, gain_ref[...]).astype(BF)
        h_ref[...] = h
        for j in range(nblk):
            u = _dot(h, up_ref[j], NN)
            u_ref[:, j * hs:(j + 1) * hs] = u.astype(BF)
            a_buf[:, j * hs:(j + 1) * hs] = _relu2(u)
        acc = xv
        for j in range(nblk):
            acc = acc + _dot(a_buf[:, j * hs:(j + 1) * hs], down_ref[j * hs:(j + 1) * hs, :], NN)
        o_ref[...] = acc

    row = pl.BlockSpec((tm, d), lambda i: (i, 0))
    return pl.pallas_call(
        body, grid=(s // tm,),
        in_specs=[row, pl.BlockSpec((None, 1, d), lambda i: (layer, 0, 0)),
                  _resident((None, nblk, d, hs), lambda i: (0, 0, 0, 0)),
                  _resident((None, nblk * hs, d), lambda i: (0, 0, 0))],
        out_specs=[row, pl.BlockSpec((tm, nblk * hs), lambda i: (i, 0)), row],
        out_shape=[jax.ShapeDtypeStruct((s, d), F32), jax.ShapeDtypeStruct((s, nblk * hs), BF),
                   jax.ShapeDtypeStruct((s, d), BF)],
        scratch_shapes=[pltpu.VMEM((tm, nblk * hs), BF)],
        compiler_params=_cparams(("parallel",)), name=f"mlp_fwd_l{layer}",
    )(x, gain, g_up, g_down)


def _loss_head(x, gain, target, tm=512):
    s, d = x.shape

    def body(x_ref, gain_ref, t_ref, dx_ref, dxb_ref, dgain_ref, loss_ref):
        i = pl.program_id(0)
        xv, gv = x_ref[...], gain_ref[...]
        err = _rms_fwd(xv, gv) - t_ref[...]
        dx, dgain = _rms_bwd(err * (1.0 / d), xv, gv)
        dx_ref[...] = dx
        dxb_ref[...] = dx.astype(BF)
        part = _part8(dgain)
        lpart = _part8(0.5 * jnp.mean(err * err, axis=-1, keepdims=True) * jnp.ones((1, LANES), F32))

        @pl.when(i == 0)
        def _():
            dgain_ref[...] = part
            loss_ref[...] = lpart

        @pl.when(i > 0)
        def _():
            dgain_ref[...] += part
            loss_ref[...] += lpart

    row = pl.BlockSpec((tm, d), lambda i: (i, 0))
    return pl.pallas_call(
        body, grid=(s // tm,),
        in_specs=[row, pl.BlockSpec((1, d), lambda i: (0, 0)), row],
        out_specs=[row, row, pl.BlockSpec((8, d), lambda i: (0, 0)), pl.BlockSpec((8, LANES), lambda i: (0, 0))],
        out_shape=[jax.ShapeDtypeStruct((s, d), F32), jax.ShapeDtypeStruct((s, d), BF), jax.ShapeDtypeStruct((8, d), F32),
                   jax.ShapeDtypeStruct((8, LANES), F32)],
        compiler_params=_cparams(("arbitrary",)), name="loss_head",
    )(x, gain, target)


def _accumulate_rows(i, ref, part):
    @pl.when(i == 0)
    def _():
        ref[...] = part

    @pl.when(i > 0)
    def _():
        ref[...] += part


def _mlp_bwd(layer, dx, dxb, x, gain, u, g_up, g_down, tm=256):
    s, d = x.shape
    nblk, hs = g_up.shape[1], g_up.shape[3]

    def body(dx_ref, dxb_ref, x_ref, gain_ref, u_ref, up_ref, down_ref, o_ref, ob_ref, du_ref, dgain_ref):
        dxb_v = dxb_ref[...]
        for j in range(nblk):
            cols = slice(j * hs, (j + 1) * hs)
            da = _dot(dxb_v, down_ref[cols, :], NT)
            du_ref[:, cols] = (da * (2.0 * jnp.maximum(u_ref[:, cols].astype(F32), 0.0))).astype(BF)
        acc = jnp.zeros((tm, d), F32)
        for j in range(nblk):
            acc = acc + _dot(du_ref[:, j * hs:(j + 1) * hs], up_ref[j], NT)
        dxn, dgain = _rms_bwd(acc, x_ref[...], gain_ref[...])
        out = dx_ref[...] + dxn
        o_ref[...] = out
        ob_ref[...] = out.astype(BF)
        _accumulate_rows(pl.program_id(0), dgain_ref, _part8(dgain))

    row = pl.BlockSpec((tm, d), lambda i: (i, 0))
    wide = pl.BlockSpec((tm, nblk * hs), lambda i: (i, 0))
    return pl.pallas_call(
        body, grid=(s // tm,),
        in_specs=[row, row, row, pl.BlockSpec((None, 1, d), lambda i: (layer, 0, 0)), wide,
                  _resident((None, nblk, d, hs), lambda i: (0, 0, 0, 0)),
                  _resident((None, nblk * hs, d), lambda i: (0, 0, 0))],
        out_specs=[row, row, wide, pl.BlockSpec((8, d), lambda i: (0, 0))],
        out_shape=[jax.ShapeDtypeStruct((s, d), F32), jax.ShapeDtypeStruct((s, d), BF),
                   jax.ShapeDtypeStruct((s, nblk * hs), BF), jax.ShapeDtypeStruct((8, d), F32)],
        compiler_params=_cparams(("arbitrary",)), name=f"mlp_bwd_l{layer}",
    )(dx, dxb, x, gain, u, g_up, g_down)


def _bwd_outproj(layer, dxb, g_out, tm=512):
    s, d = dxb.shape
    mw = g_out.shape[1]

    def body(dx_ref, w_ref, o_ref):
        o_ref[...] = _dot(dx_ref[...], w_ref[...], NT)

    return pl.pallas_call(
        body, grid=(s // tm,),
        in_specs=[pl.BlockSpec((tm, d), lambda i: (i, 0)), _resident((None, mw, d), lambda i: (0, 0, 0))],
        out_specs=pl.BlockSpec((tm, mw), lambda i: (i, 0)), out_shape=jax.ShapeDtypeStruct((s, mw), F32),
        compiler_params=_cparams(("parallel",)), name=f"bwd_outproj_l{layer}",
    )(dxb, g_out)


def _attn_norm_bwd(layer, dmixed, o, gain, tm=512):
    s = o.shape[0]

    def body(dm_ref, o_ref, gain_ref, do_ref, delta_ref, dgain_ref):
        i = pl.program_id(0)
        ov = o_ref[...]
        do, dgain = _rms_bwd(dm_ref[...], ov, gain_ref[...])
        do_ref[...] = do
        row = lax.broadcasted_iota(jnp.int32, (ATTN_W, ATTN_W), 0)
        col = lax.broadcasted_iota(jnp.int32, (ATTN_W, ATTN_W), 1)
        same_head = jnp.where(row // 64 == col // 64, 1.0, 0.0)
        delta_ref[...] = jnp.dot(do * ov, same_head, precision=lax.Precision.HIGHEST, preferred_element_type=F32)
        part = _part8(dgain)

        @pl.when(i == 0)
        def _():
            dgain_ref[...] = part

        @pl.when(i > 0)
        def _():
            dgain_ref[...] += part

    blk = pl.BlockSpec((tm, ATTN_W), lambda i: (i, 0))
    return pl.pallas_call(
        body, grid=(s // tm,), in_specs=[blk, blk, pl.BlockSpec((None, 1, ATTN_W), lambda i: (layer, 0, 0))],
        out_specs=[blk, blk, pl.BlockSpec((8, ATTN_W), lambda i: (0, 0))],
        out_shape=[jax.ShapeDtypeStruct((s, ATTN_W), F32), jax.ShapeDtypeStruct((s, ATTN_W), F32),
                   jax.ShapeDtypeStruct((8, ATTN_W), F32)],
        compiler_params=_cparams(("arbitrary",)), name=f"attn_norm_bwd_l{layer}",
    )(dmixed, o, gain)


def _attn_bwd(layer, proj, do, lse, delta):
    s = proj.shape[0]

    def body(q_ref, kp_ref, kc_ref, vp_ref, vc_ref, do_ref, lse_ref, delta_ref, dq_ref, dk_ref, dkp_ref, dv_ref, dvp_ref):
        m_first, m_rest, is_a = _attn_masks(pl.program_id(1) > 0)
        is_a_keys = lax.broadcasted_iota(jnp.int32, (2 * SPAN, LANES), 1) < 64
        sels = (is_a, jnp.logical_not(is_a))
        key_sels = (is_a_keys, jnp.logical_not(is_a_keys))
        for ref in (dq_ref, dk_ref, dkp_ref, dv_ref, dvp_ref):
            ref[...] = jnp.zeros(ref.shape, F32)
        for dil in DILATIONS:
            last = ATTN_UNIT // (SPAN * dil) - 1
            for group in _attn_groups(dil):
                items = [(r, b, h) for r, b in group for h in range(2)]
                q = {rb: q_ref[_block_rows(dil, *rb), :] for rb in group}
                dov = {rb: do_ref[_block_rows(dil, *rb), :] for rb in group}
                lse_v = {rb: lse_ref[_block_rows(dil, *rb), :] for rb in group}
                delta_v = {rb: delta_ref[_block_rows(dil, *rb), :] for rb in group}
                k = {rb: _block_keys(kp_ref, kc_ref, dil, *rb) for rb in group}
                v = {rb: _block_keys(vp_ref, vc_ref, dil, *rb).astype(BF) for rb in group}
                qh = [jnp.where(sels[h], q[r, b], 0.0).astype(BF) for r, b, h in items]
                doh = [jnp.where(sels[h], dov[r, b], 0.0).astype(BF) for r, b, h in items]
                kh = [jnp.where(key_sels[h], k[r, b], 0.0).astype(BF) for r, b, h in items]
                sc = [jnp.where(m_first if b == 0 else m_rest, _dot(qh[i], kh[i], NT), MASK_VALUE)
                      for i, (r, b, h) in enumerate(items)]
                dp = [_dot(doh[i], v[r, b], NT) for i, (r, b, h) in enumerate(items)]
                p = [jnp.exp(sc[i] - lse_v[r, b][:, 64 * h:64 * h + 1]) for i, (r, b, h) in enumerate(items)]
                ds = [(p[i] * (dp[i] - delta_v[r, b][:, 64 * h:64 * h + 1])).astype(BF) for i, (r, b, h) in enumerate(items)]
                dv = [_dot(p[i].astype(BF), doh[i], TN) for i in range(len(items))]
                dq = [_dot(ds[i], kh[i], NN) for i in range(len(items))]
                dk = [_dot(ds[i], qh[i], TN) for i in range(len(items))]
                for j, (r, b) in enumerate(group):
                    own = _block_rows(dil, r, b)
                    dq_ref[own, :] += dq[2 * j] + dq[2 * j + 1]
                    dk2, dv2 = dk[2 * j] + dk[2 * j + 1], dv[2 * j] + dv[2 * j + 1]
                    dk_ref[own, :] += dk2[SPAN:]
                    dv_ref[own, :] += dv2[SPAN:]
                    if b > 0:
                        before = _block_rows(dil, r, b - 1)
                        dk_ref[before, :] += dk2[:SPAN]
                        dv_ref[before, :] += dv2[:SPAN]
                    else:
                        before = _block_rows(dil, r, last)
                        dkp_ref[before, :] += dk2[:SPAN]
                        dvp_ref[before, :] += dv2[:SPAN]

    out_spec = pl.BlockSpec((ATTN_UNIT, LANES), lambda p, n: (n, p))
    return pl.pallas_call(
        body, grid=(ATTN_W // LANES, s // ATTN_UNIT), in_specs=_attn_specs(3), out_specs=[out_spec] * 5,
        out_shape=[jax.ShapeDtypeStruct((s, ATTN_W), F32)] * 5,
        compiler_params=_cparams(("parallel", "arbitrary")), name=f"attn_bwd_l{layer}",
    )(proj, proj, proj, proj, proj, do, lse, delta)


def _attn_combine(layer, parts, cos, sin):
    dq, dk, dkp, dv, dvp = parts
    s = dq.shape[0]
    nblk = s // SPAN
    per_unit = ATTN_UNIT // SPAN
    here = pl.BlockSpec((SPAN, ATTN_W), lambda i: (i, 0))
    ahead = pl.BlockSpec((SPAN, ATTN_W), lambda i: (jnp.minimum(i + per_unit, nblk - 1), 0))
    tab = pl.BlockSpec((SPAN, LANES), lambda i: (i, 0))

    def body(dq_ref, dk_ref, dkp_ref, dv_ref, dvp_ref, cos_ref, sin_ref, out_ref):
        has_next = pl.program_id(0) + per_unit < nblk
        dqv = dq_ref[...]
        dkv = dk_ref[...] + jnp.where(has_next, dkp_ref[...], 0.0)
        dvv = dv_ref[...] + jnp.where(has_next, dvp_ref[...], 0.0)
        cs = _tile_lanes(cos_ref[...], ATTN_W // LANES)
        sn = _tile_lanes(sin_ref[...], ATTN_W // LANES)
        out_ref[0] = ((dqv * cs - _rope_partner(dqv) * sn) * ATTN_SCALE).astype(BF)
        out_ref[1] = (dkv * cs - _rope_partner(dkv) * sn).astype(BF)
        out_ref[2] = dvv.astype(BF)

    return pl.pallas_call(
        body, grid=(nblk,), in_specs=[here, here, ahead, here, ahead, tab, tab],
        out_specs=pl.BlockSpec((3, SPAN, ATTN_W), lambda i: (0, i, 0)),
        out_shape=jax.ShapeDtypeStruct((3, s, ATTN_W), BF),
        compiler_params=_cparams(("parallel",)), name=f"attn_combine_l{layer}",
    )(dq, dk, dkp, dv, dvp, cos, sin)


def _hgrn_bwd(layer, proj, lb, gain, o, dmixed, states):
    s = proj.shape[0]
    nblk = s // HGRN_ROWS
    cpb = len(CHUNKS)

    def body(q_ref, f_ref, i_ref, g_ref, lb_ref, gain_ref, o_ref, drec_ref, st_ref, dseg_ref, dlb_ref, dgain_ref,
             dstate, dst_buf):
        step = pl.program_id(0)

        @pl.when(step == 0)
        def _():
            dstate[...] = jnp.zeros(dstate.shape, F32)

        lbv, gv = lb_ref[...], gain_ref[...]
        qh, z, gate_in = q_ref[...], f_ref[...], g_ref[...]
        pre = _hgrn_pre(qh, z, lbv)
        v = i_ref[...].astype(BF)
        sg = _sigmoid(gate_in)
        ov, drec = o_ref[...], drec_ref[...]
        dnormed = drec * (gate_in * sg)
        back = [_rms_bwd(_head(dnormed, h), _head(ov, h), gv) for h in HEADS]
        do_b = jnp.concatenate([b[0] for b in back], axis=1).astype(BF)
        dgain = back[0][1] + back[1][1] + back[2][1] + back[3][1]
        normed = jnp.concatenate([_rms_fwd(_head(ov, h), gv) for h in HEADS], axis=1)
        dgate_in = drec * normed * (sg * (1.0 + gate_in * (1.0 - sg)))
        mask = _hgrn_mask()
        qt, kt, qg, kout = (pre[n].astype(BF) for n in ("qt", "kt", "qg", "kout"))
        dec = jnp.exp(pre["g_last"])
        def intra(fn):
            return jnp.concatenate([jnp.concatenate([fn(h, sb) for sb in SUBS], axis=0) for h in HEADS], axis=1)

        def hs(x, h, sb):
            return _sub(_head(x, h), sb)

        a = [[jnp.where(mask, _dot(hs(qt, h, sb), hs(kt, h, sb), NT), 0.0).astype(BF) for sb in SUBS] for h in HEADS]
        da = [[jnp.where(mask, _dot(hs(do_b, h, sb), hs(v, h, sb), NT), 0.0).astype(BF) for sb in SUBS] for h in HEADS]
        dv_intra = intra(lambda h, sb: _dot(a[h][sb], hs(do_b, h, sb), TN))
        dqt = intra(lambda h, sb: _dot(da[h][sb], hs(kt, h, sb), NN))
        dkt = intra(lambda h, sb: _dot(da[h][sb], hs(qt, h, sb), TN))
        feed = [[_dot(_chunk(_head(do_b, h), c), _chunk(_head(qg, h), c), TN) for c in CHUNKS] for h in HEADS]
        for h in HEADS:
            dst = dstate[h]
            for c in reversed(CHUNKS):
                dst_buf[h, c * LANES:(c + 1) * LANES, :] = dst
                dst = dst * _head(dec, h)[c * HGRN_CHUNK:c * HGRN_CHUNK + 1, :] + feed[h][c]
            dstate[h] = dst

        def per_chunk(fn):
            cols = []
            for h in HEADS:
                rows = [jnp.broadcast_to(t, (HGRN_CHUNK, HGRN_DIM)) for t in (fn(h, c) for c in CHUNKS)]
                cols.append(jnp.concatenate(rows, axis=0))
            return jnp.concatenate(cols, axis=1)

        def st_prev(h, c):
            return st_ref[h, c * LANES:(c + 1) * LANES, :]

        def dst_at(h, c):
            return dst_buf[h, c * LANES:(c + 1) * LANES, :]

        dqg = per_chunk(lambda h, c: _dot(_chunk(_head(do_b, h), c), st_prev(h, c).astype(BF), NN))
        dkout = per_chunk(lambda h, c: _dot(_chunk(_head(v, h), c), dst_at(h, c).astype(BF), NN))
        dv_inter = per_chunk(lambda h, c: _dot(_chunk(_head(kout, h), c), dst_at(h, c).astype(BF), NT))
        dg_state = per_chunk(lambda h, c: jnp.sum(dst_at(h, c) * st_prev(h, c).astype(F32), axis=0, keepdims=True))
        dg_kout = per_chunk(lambda h, c: jnp.sum(_chunk(_head(dkout * pre["kout"], h), c), axis=0, keepdims=True))
        dv = dv_intra + dv_inter
        pos = lax.broadcasted_iota(jnp.int32, (HGRN_ROWS, HGRN_W), 0) % HGRN_CHUNK
        dq = dqt * pre["e_q"] + dqg * pre["e_in"]
        dk = dkt * pre["e_k"] + dkout * pre["e_out"]
        dg = (dqt * pre["qt"] - dkt * pre["kt"] + dqg * pre["qg"] - dkout * pre["kout"]
              + jnp.where(pos == HGRN_CHUNK - 1, dg_state * dec + dg_kout, 0.0))
        dlogf = _chunk_cumsum(dg, reverse=True)
        sig, sq = pre["sig"], pre["sq"]
        df = dlogf / pre["f"] - dk
        dseg_ref[0] = (dq * HGRN_SCALE * (sq * (1.0 + qh * (1.0 - sq)))).astype(BF)
        dseg_ref[1] = (df * (1.0 - lbv) * sig * (1.0 - sig)).astype(BF)
        dseg_ref[2] = dv.astype(BF)
        dseg_ref[3] = dgate_in.astype(BF)
        _accumulate_rows(step, dlb_ref, _part8(df * (1.0 - sig)))
        _accumulate_rows(step, dgain_ref, _part8(dgain))

    specs, blk = _hgrn_in_specs(layer, True, nblk)
    specs += [pl.BlockSpec((HGRN_ROWS, HGRN_W), lambda b: (blk(b), 0)),
              pl.BlockSpec((HGRN_ROWS, HGRN_W), lambda b: (blk(b), 1)),
              pl.BlockSpec((HGRN_HEADS, cpb * LANES, LANES), lambda b: (0, blk(b), 0))]
    return pl.pallas_call(
        body, grid=(nblk,), in_specs=specs,
        out_specs=[pl.BlockSpec((4, HGRN_ROWS, HGRN_W), lambda b: (0, blk(b), 0)),
                   pl.BlockSpec((8, HGRN_W), lambda b: (0, 0)), pl.BlockSpec((8, HGRN_DIM), lambda b: (0, 0))],
        out_shape=[jax.ShapeDtypeStruct((4, s, HGRN_W), BF), jax.ShapeDtypeStruct((8, HGRN_W), F32),
                   jax.ShapeDtypeStruct((8, HGRN_DIM), F32)],
        scratch_shapes=[pltpu.VMEM((HGRN_HEADS, LANES, LANES), F32), pltpu.VMEM((HGRN_HEADS, cpb * LANES, LANES), F32)],
        compiler_params=_cparams(("arbitrary",)), name=f"hgrn_bwd_l{layer}",
    )(proj, proj, proj, proj, lb, gain, o, dmixed, states)


def _bwd_inproj(layer, dqkv, dhg, g_in, x, gain, dres, tm=256):
    s, d = x.shape

    def body(dqkv_ref, dhg_ref, w_ref, x_ref, gain_ref, dres_ref, dx_ref, dxb_ref, dgain_ref):
        acc = jnp.zeros((tm, d), F32)
        for seg in range(N_SEG):
            a = dqkv_ref[seg] if seg < 3 else dhg_ref[seg - 3]
            acc = acc + _dot(a, w_ref[seg * SEG:(seg + 1) * SEG, :], NN)
        dx, dgain = _rms_bwd(acc, x_ref[...], gain_ref[...])
        out = dres_ref[...] + dx
        dx_ref[...] = out
        dxb_ref[...] = out.astype(BF)
        _accumulate_rows(pl.program_id(0), dgain_ref, _part8(dgain))

    row = pl.BlockSpec((tm, d), lambda i: (i, 0))
    return pl.pallas_call(
        body, grid=(s // tm,),
        in_specs=[pl.BlockSpec((3, tm, SEG), lambda i: (0, i, 0)), pl.BlockSpec((4, tm, SEG), lambda i: (0, i, 0)),
                  _resident((None, PROJ_W, d), lambda i: (0, 0, 0)), row,
                  pl.BlockSpec((None, 1, d), lambda i: (layer, 0, 0)), row],
        out_specs=[row, row, pl.BlockSpec((8, d), lambda i: (0, 0))],
        out_shape=[jax.ShapeDtypeStruct((s, d), F32), jax.ShapeDtypeStruct((s, d), BF), jax.ShapeDtypeStruct((8, d), F32)],
        compiler_params=_cparams(("arbitrary",)), name=f"bwd_inproj_l{layer}",
    )(dqkv, dhg, g_in, x, gain, dres)


def _adamw(w, g, m, v):
    m2 = ADAM_B1 * m + (1.0 - ADAM_B1) * g
    v2 = ADAM_B2 * v + (1.0 - ADAM_B2) * (g * g)
    m_hat = m2 / (1.0 - ADAM_B1 ** ADAM_STEP)
    v_hat = v2 / (1.0 - ADAM_B2 ** ADAM_STEP)
    delta = -ADAM_LR * (m_hat / (jnp.sqrt(v_hat) + ADAM_EPS) + ADAM_WD * w)
    return delta, m2, v2


def _adam_big(name, parts, w, m, v, row_tiles):
    depth = w.shape[0]
    r, c = parts[0].shape[1], parts[0].shape[2]
    tr = r // row_tiles
    p_spec = pl.BlockSpec((N_DEV, tr, c), lambda t: (0, t, 0))
    w_spec = pl.BlockSpec((depth, tr, c), lambda t: (0, t, 0))

    def body(*refs):
        p_refs = refs[:depth]
        w_ref, m_ref, v_ref, g_ref, d_ref, m2_ref, v2_ref = refs[depth:]
        for l in range(depth):
            g = p_refs[l][0].astype(F32)
            for dev in range(1, N_DEV):
                g = g + p_refs[l][dev].astype(F32)
            delta, m2, v2 = _adamw(w_ref[l], g, m_ref[l], v_ref[l])
            g_ref[l] = g
            d_ref[l] = delta
            m2_ref[l] = m2
            v2_ref[l] = v2

    return pl.pallas_call(
        body, grid=(row_tiles,), in_specs=[p_spec] * depth + [w_spec] * 3, out_specs=[w_spec] * 4,
        out_shape=[jax.ShapeDtypeStruct(w.shape, F32)] * 4,
        compiler_params=_cparams(("parallel",)), name=name,
    )(*parts, w, m, v)


def _adam_small(g, w, m, v):
    def body(g_ref, w_ref, m_ref, v_ref, d_ref, m2_ref, v2_ref):
        delta, m2, v2 = _adamw(w_ref[...], g_ref[...], m_ref[...], v_ref[...])
        d_ref[...] = delta
        m2_ref[...] = m2
        v2_ref[...] = v2

    vm = pl.BlockSpec(memory_space=pltpu.VMEM)
    return pl.pallas_call(body, in_specs=[vm] * 4, out_specs=[vm] * 3, out_shape=[jax.ShapeDtypeStruct(g.shape, F32)] * 3,
                          name="adam_small")(g, w, m, v)


def _lower_bounds(logits):
    def body(l_ref, lb_ref, jac_ref):
        l0, l1 = l_ref[0:1, :], l_ref[1:2, :]
        mx = jnp.maximum(l0, l1)
        e0, e1 = jnp.exp(l0 - mx), jnp.exp(l1 - mx)
        p0, p1 = e0 / (e0 + e1), e1 / (e0 + e1)
        lb_ref[0:1, :] = p0 - p0
        lb_ref[1:2, :] = (p0 + p1) - p0
        jac_ref[0:1, :] = -p0 * p1
        jac_ref[1:2, :] = p0 * p1

    vm = pl.BlockSpec(memory_space=pltpu.VMEM)
    return pl.pallas_call(body, in_specs=[vm], out_specs=[vm, vm], out_shape=[jax.ShapeDtypeStruct(logits.shape, F32)] * 2,
                          name="hgrn_lower_bounds")(logits)


def _rope_tables(s):
    half = 32
    inv_freq = ROPE_THETA ** (-jnp.arange(half, dtype=F32) / half)
    ang = jnp.arange(s, dtype=jnp.int32).astype(F32)[:, None] * inv_freq[None, :]
    cos, sin = jnp.cos(ang), jnp.sin(ang)
    return jnp.concatenate([cos] * 4, axis=1), jnp.concatenate([-sin, sin, -sin, sin], axis=1)


SMALL_NAMES = ("norm_mix", "attn_out_gain", "hgrn_lb_logits", "hgrn_out_gain", "norm_mlp", "norm_final")


def _pack_small(vals):
    flat = jnp.concatenate([v.reshape(-1) for v in vals])
    rows = -(-flat.shape[0] // (8 * LANES)) * 8
    return jnp.pad(flat, (0, rows * LANES - flat.shape[0])).reshape(rows, LANES)


def _unpack_small(packed, like):
    flat, out, off = packed.reshape(-1), [], 0
    for v in like:
        out.append(flat[off:off + v.size].reshape(v.shape))
        off += v.size
    return out


def kernel(x, norm_mix, w_in, attn_out_gain, hgrn_lb_logits, hgrn_out_gain, w_out, norm_mlp, w_up, w_down, norm_final, loss_target, m_norm_mix, m_w_in, m_attn_out_gain, m_hgrn_lb_logits, m_hgrn_out_gain, m_w_out, m_norm_mlp, m_w_up, m_w_down, m_norm_final, v_norm_mix, v_w_in, v_attn_out_gain, v_hgrn_lb_logits, v_hgrn_out_gain, v_w_out, v_norm_mlp, v_w_up, v_w_down, v_norm_final):
    depth = w_in.shape[0]
    assert depth == 2 and x.shape[0] == 1
    s, d = x.shape[1], x.shape[2]
    x0 = x[0]
    target = loss_target[0]
    cos, sin = _rope_tables(s)
    g_mix, g_attn, g_hg, g_mlp = (norm_mix[:, None, :], attn_out_gain[:, None, :], hgrn_out_gain[:, None, :],
                                  norm_mlp[:, None, :])
    lb, lb_jac = _lower_bounds(hgrn_lb_logits)
    lb3 = lb[:, None, :]

    def flip(a):
        return jnp.swapaxes(a, 1, 2)

    shards = list(_pack_weights(flip(w_in), w_out, w_up, w_down))
    w_pieces = _weight_pieces(*shards)
    w_groups = [[0], [1, 2, 3], [4], [5, 6, 7]]
    me = (4 * lax.axis_index("x") + 2 * lax.axis_index("y") + lax.axis_index("c")).astype(jnp.int32).reshape(1)
    lands = _exchange_own("all_gather_own", me, shards, w_pieces)
    w_sems, shards, lands, token = _exchange_start("all_gather_start", shards, lands, w_pieces, w_groups)

    def weights_ready(group, after):
        nonlocal shards
        idxs = w_groups[group]
        shards, got = _exchange_wait(f"all_gather_wait{group}", shards, [lands[i] for i in idxs], w_pieces,
                                     [(idxs, *w_sems[group])], after)
        return got

    def tied(small_arr, tok):
        return small_arr + tok[0, 0]

    saved = []
    xl = x0
    full = [None] * depth
    for l in range(depth):
        (full_in,) = weights_ready(2 * l, token if l == 0 else xl)
        proj, h = _fwd_inproj(l, xl, g_mix, full_in, cos, sin)
        o_attn, lse = _attn_fwd(l, proj)
        o_hg, mixed, states = _hgrn_fwd(l, proj, lb3, g_hg)
        mixed = _attn_norm(l, o_attn, g_attn, mixed)
        full_out, full_up, full_down = weights_ready(2 * l + 1, mixed)
        x_mid = _fwd_outproj(l, mixed, full_out, xl)
        x_next, u, h2 = _mlp_fwd(l, x_mid, g_mlp, full_up, full_down)
        saved.append((xl, proj, h, o_attn, lse, o_hg, states, mixed, x_mid, u, h2))
        full[l] = (full_in, full_out, full_up, full_down)
        xl = x_next
    dx, dxb, dnorm_final8, loss8 = _loss_head(xl, norm_final[None, :], target)
    loss = lax.psum(jnp.sum(loss8[:, 0]), ("x", "y", "c"))

    exchanges = []

    def scatter(tag, grads, kinds):
        pieces = _grad_pieces(grads, kinds)
        own = _exchange_own(f"reduce_scatter_own_{tag}", me, grads, pieces)
        sems, grads, own, tok = _exchange_start(f"reduce_scatter_start_{tag}", grads, own, pieces, [list(range(len(pieces)))])
        exchanges.append((grads, own, pieces, sems[0]))
        return tok

    small = {}
    for l in reversed(range(depth)):
        xl, proj, h, o_attn, lse, o_hg, states, mixed, x_mid, u, h2 = saved[l]
        full_in, full_out, full_up, full_down = full[l]
        hs = full_up.shape[3]
        gw_down = _mm_tn(f"grad_w_down_l{l}", u, dxb, u.shape[1], a_fn=_relu2)
        dx_mid, dx_mid_b, du, dmlp8 = _mlp_bwd(l, dx, dxb, x_mid, g_mlp, u, full_up, full_down)
        gw_up = _mm_tn(f"grad_w_up_l{l}", h2, du, d, out_block_w=hs)
        gw_out = _mm_tn(f"grad_w_out_l{l}", mixed, dx_mid_b, mixed.shape[1])
        g_attn_t = tied(g_attn, scatter(f"mlp_l{l}", [gw_down, gw_up, gw_out], ["rows", "up", "rows"]))
        dmixed = _bwd_outproj(l, dx_mid_b, full_out)
        do, delta, dattn8 = _attn_norm_bwd(l, dmixed, o_attn, g_attn_t)
        dqkv = _attn_combine(l, _attn_bwd(l, proj, do, lse, delta), cos, sin)
        dhg, dlb8, dhgain8 = _hgrn_bwd(l, proj, lb3, g_hg, o_hg, dmixed, states)
        gin = _mm_tn(f"grad_w_in_qkv_l{l}", dqkv, h, PROJ_W, a_lead=True)
        gw_in = _mm_tn(f"grad_w_in_hg_l{l}", dhg, h, PROJ_W, a_lead=True, out_block_off=3, prev=gin)
        g_mix_t = tied(g_mix, scatter(f"mix_l{l}", [gw_in], ["rows"]))
        dx, dxb, dmix8 = _bwd_inproj(l, dqkv, dhg, full_in, xl, g_mix_t, dx_mid)
        small[l] = (dmix8, dattn8, dlb8, dhgain8, dmlp8)

    def fin(p8):
        return jnp.sum(p8, axis=0)
    dlogits = lb_jac * fin(small[1][2])[None, :]
    small_grads = [jnp.stack([fin(small[l][0]) for l in range(depth)]), jnp.stack([fin(small[l][1]) for l in range(depth)]),
                   dlogits, jnp.stack([fin(small[l][3]) for l in range(depth)]),
                   jnp.stack([fin(small[l][4]) for l in range(depth)]), fin(dnorm_final8)]
    small_w = [norm_mix, attn_out_gain, hgrn_lb_logits, hgrn_out_gain, norm_mlp, norm_final]
    small_m = [m_norm_mix, m_attn_out_gain, m_hgrn_lb_logits, m_hgrn_out_gain, m_norm_mlp, m_norm_final]
    small_v = [v_norm_mix, v_attn_out_gain, v_hgrn_lb_logits, v_hgrn_out_gain, v_norm_mlp, v_norm_final]
    g_small = _all_reduce_small(_pack_small(small_grads))
    d_small, m_small, v_small = _adam_small(g_small, _pack_small(small_w), _pack_small(small_m), _pack_small(small_v))
    gs, ds, ms, vs = (_unpack_small(t, small_w) for t in (g_small, d_small, m_small, v_small))

    all_grads, all_lands, all_pieces, waits = [], [], [], []
    for grads, own, pieces, (send, recv) in exchanges:
        first = len(all_pieces)
        all_pieces += [p._replace(src=p.src + len(all_grads)) for p in pieces]
        waits.append((list(range(first, first + len(pieces))), send, recv))
        all_grads += grads
        all_lands += own
    _, landed = _exchange_wait("reduce_scatter_wait", all_grads, all_lands, all_pieces, waits, d_small)
    r_down, r_up, r_out, r_in = ([landed[4 + i], landed[i]] for i in range(4))
    big = {
        "w_in": [flip(t) for t in _adam_big("adam_w_in", r_in, flip(w_in), flip(m_w_in), flip(v_w_in), 2)],
        "w_out": _adam_big("adam_w_out", r_out, w_out, m_w_out, v_w_out, 1),
        "w_up": _adam_big("adam_w_up", r_up, w_up, m_w_up, v_w_up, 2),
        "w_down": _adam_big("adam_w_down", r_down, w_down, m_w_down, v_w_down, 4),
    }

    def gather(idx, small_list):
        by_name = dict(zip(SMALL_NAMES, small_list))
        return [by_name["norm_mix"], big["w_in"][idx], by_name["attn_out_gain"], by_name["hgrn_lb_logits"],
                by_name["hgrn_out_gain"], big["w_out"][idx], by_name["norm_mlp"], big["w_up"][idx], big["w_down"][idx],
                by_name["norm_final"]]

    return (loss, dx[None], *gather(0, gs), *gather(1, ds), *gather(2, ms), *gather(3, vs))
```

```python
import functools
from typing import Callable, NamedTuple

import jax
import jax.numpy as jnp
from jax import lax
from jax.experimental import pallas as pl
from jax.experimental.pallas import tpu as pltpu

F32 = jnp.float32
BF = jnp.bfloat16

N_DEV = 8
ATTN_W = 512
HGRN_W = 512
HGRN_HEADS = 4
HGRN_DIM = 128
SEG = 512
N_SEG = 7
PROJ_W = N_SEG * SEG
MIX_W = ATTN_W + HGRN_W
SPAN = 128
DILATIONS = (1, 4, 16)
HGRN_CHUNK = 16
ROPE_THETA = 10000.0
NORM_EPS = 1e-6
MASK_VALUE = -1e30
ATTN_SCALE = 0.125
HGRN_SCALE = HGRN_DIM ** -0.5
ADAM_LR = 0.001
ADAM_B1 = 0.9
ADAM_B2 = 0.999
ADAM_EPS = 1e-08
ADAM_WD = 0.01
ADAM_STEP = 10
LANES = 128
VMEM_LIMIT = 56 * 1024 * 1024

NN = ((1,), (0,))
NT = ((1,), (1,))
TN = ((0,), (0,))
MESH = pl.DeviceIdType.MESH


def _dot(a, b, dims):
    return lax.dot_general(a, b, (dims, ((), ())), preferred_element_type=F32)


def _cparams(sem):
    return pltpu.CompilerParams(dimension_semantics=sem, vmem_limit_bytes=VMEM_LIMIT)


def _part8(x):
    r, n = x.shape
    return jnp.sum(x.reshape(r // 8, 8, n), axis=0)


def _sigmoid(x):
    return 1.0 / (1.0 + jnp.exp(-x))


def _rms_fwd(x, gain):
    r = lax.rsqrt(jnp.mean(x * x, axis=-1, keepdims=True) + NORM_EPS)
    return x * r * gain


def _rms_bwd(dy, x, gain):
    r = lax.rsqrt(jnp.mean(x * x, axis=-1, keepdims=True) + NORM_EPS)
    xn = x * r
    dxn = dy * gain
    dx = r * (dxn - xn * jnp.mean(dxn * xn, axis=-1, keepdims=True))
    return dx, dy * xn


def _rope_partner(x):
    n = x.shape[-1]
    lane = lax.broadcasted_iota(jnp.int32, x.shape, x.ndim - 1)
    return jnp.where((lane % 64) < 32, pltpu.roll(x, n - 32, x.ndim - 1), pltpu.roll(x, 32, x.ndim - 1))


def _tile_lanes(t, reps):
    return jnp.concatenate([t] * reps, axis=-1)


def _mm_tn(name, a, b, out_rows, a_lead=False, out_block_off=0, prev=None, out_block_w=None, a_fn=None,
           tm=512, tn=1024, sub=512):
    kdim, n = b.shape
    m = a.shape[-1]
    tm, tn, sub = min(tm, m), min(tn, n), min(sub, kdim)
    mt = m // tm
    n_lead = a.shape[0] if a_lead else 1
    if a_lead:
        a_spec = pl.BlockSpec((None, kdim, tm), lambda j, i: (i // mt, 0, i % mt))
    else:
        a_spec = pl.BlockSpec((kdim, tm), lambda j, i: (0, i))
    b_spec = pl.BlockSpec((kdim, tn), lambda j, i: (0, j))
    if out_block_w:
        nb = tn // out_block_w
        o_shape = jax.ShapeDtypeStruct((n // out_block_w, out_rows, out_block_w), BF)
        o_spec = pl.BlockSpec((nb, tm, out_block_w), lambda j, i: (j, i + out_block_off, 0))
    else:
        nb = 0
        o_shape = jax.ShapeDtypeStruct((out_rows, n), BF)
        o_spec = pl.BlockSpec((tm, tn), lambda j, i: (i + out_block_off, j))
    arrays, specs, aliases = [a, b], [a_spec, b_spec], {}
    if prev is not None:
        arrays.append(prev)
        specs.append(pl.BlockSpec(memory_space=pl.ANY))
        aliases = {2: 0}

    def body(*refs):
        a_ref, b_ref, o_ref = refs[0], refs[1], refs[-1]
        acc = None
        for k in range(kdim // sub):
            av = a_ref[k * sub:(k + 1) * sub, :]
            if a_fn is not None:
                av = a_fn(av)
            part = _dot(av, b_ref[k * sub:(k + 1) * sub, :], TN)
            acc = part if acc is None else acc + part
        if nb:
            for t in range(nb):
                o_ref[t] = acc[:, t * out_block_w:(t + 1) * out_block_w].astype(BF)
        else:
            o_ref[...] = acc.astype(BF)

    return pl.pallas_call(
        body, grid=(n // tn, n_lead * mt), in_specs=specs, out_specs=o_spec, out_shape=o_shape,
        compiler_params=_cparams(("parallel", "parallel")), name=name, input_output_aliases=aliases,
    )(*arrays)


def _pack_weights(w_in_t, w_out, w_up, w_down):
    depth = w_in_t.shape[0]
    arrays = (w_in_t, w_out, w_up, w_down)

    def body(*refs):
        for src, dst in zip(refs[:4], refs[4:]):
            dst[...] = src[...].astype(BF)

    specs = [pl.BlockSpec((None,) + a.shape[1:], lambda l: (l, 0, 0)) for a in arrays]
    return pl.pallas_call(
        body, grid=(depth,), in_specs=specs, out_specs=specs,
        out_shape=[jax.ShapeDtypeStruct(a.shape, BF) for a in arrays],
        compiler_params=_cparams(("arbitrary",)), name="pack_weights",
    )(*arrays)


def _my_position():
    x, y, c = lax.axis_index("x"), lax.axis_index("y"), lax.axis_index("c")
    return x, y, c, 4 * x + 2 * y + c


def _peer(x, y, c, k):
    px = 1 - x if k & 4 else x
    py = 1 - y if k & 2 else y
    pc = 1 - c if k & 1 else c
    return (px, py, pc), 4 * px + 2 * py + pc


PEER_ORDER = (1, 2, 4, 3, 5, 6, 7)


class _Piece(NamedTuple):
    src: int
    send: Callable
    slot: Callable
    land_shape: tuple
    own_src: tuple
    own_slot: tuple


HBM_SPEC = pl.BlockSpec(memory_space=pltpu.HBM)
SEM_SPEC = pl.BlockSpec(memory_space=pltpu.SEMAPHORE)
ANY_SPEC = pl.BlockSpec(memory_space=pl.ANY)


def _in_hbm(arrays):
    return [pltpu.with_memory_space_constraint(a, pltpu.HBM) for a in arrays]


def _hbm_like(arrays):
    return [pltpu.HBM(a.shape, a.dtype) for a in arrays]


def _rows_of(rows):
    return lambda ref, dev: ref.at[pl.ds(pl.multiple_of(dev * rows, 16), rows), :]


def _exchange_own(name, me, srcs, pieces):
    n = len(pieces)

    def body(me_ref, *refs):
        for i in range(n):
            refs[n + i][...] = refs[i][...]

    def spec(block_and_index):
        block, index = block_and_index
        return pl.BlockSpec(block, lambda i, me_ref: index(me_ref[0]))

    return pl.pallas_call(
        body,
        grid_spec=pltpu.PrefetchScalarGridSpec(
            num_scalar_prefetch=1, grid=(1,), in_specs=[spec(p.own_src) for p in pieces],
            out_specs=[spec(p.own_slot) for p in pieces]),
        out_shape=[jax.ShapeDtypeStruct(p.land_shape, BF) for p in pieces],
        compiler_params=_cparams(("arbitrary",)), name=name,
    )(me, *[srcs[p.src] for p in pieces])


def _exchange_start(name, srcs, lands, pieces, groups):
    n_src, n, n_g = len(srcs), len(pieces), len(groups)

    def body(*refs):
        src_refs, land_refs = refs[:n_src], refs[n_src:n_src + n]
        sems, token = refs[n_src + n:n_src + n + 2 * n_g], refs[-1]
        x, y, c, me = _my_position()
        for g, idxs in enumerate(groups):
            for k in PEER_ORDER:
                peer, pid = _peer(x, y, c, k)
                for j, i in enumerate(idxs):
                    p = pieces[i]
                    pltpu.make_async_remote_copy(
                        src_ref=p.send(src_refs[p.src], pid), dst_ref=p.slot(land_refs[i], me),
                        send_sem=sems[2 * g].at[(k - 1) * len(idxs) + j], recv_sem=sems[2 * g + 1].at[(k - 1) * len(idxs) + j],
                        device_id=peer, device_id_type=MESH).start()
        token[...] = jnp.zeros(token.shape, F32)

    sem_shapes = [pltpu.SemaphoreType.DMA(((N_DEV - 1) * len(idxs),)) for idxs in groups for _ in range(2)]
    res = pl.pallas_call(
        body, in_specs=[HBM_SPEC] * (n_src + n),
        out_specs=[SEM_SPEC] * (2 * n_g) + [HBM_SPEC] * (n_src + n) + [pl.BlockSpec(memory_space=pltpu.VMEM)],
        out_shape=sem_shapes + _hbm_like(srcs) + _hbm_like(lands) + [jax.ShapeDtypeStruct((8, LANES), F32)],
        input_output_aliases={i: 2 * n_g + i for i in range(n_src + n)},
        compiler_params=pltpu.CompilerParams(has_side_effects=pltpu.SideEffectType.DATAFLOW_SIDE_EFFECTING),
        name=name,
    )(*_in_hbm(srcs), *_in_hbm(lands))
    sems = [(res[2 * g], res[2 * g + 1]) for g in range(n_g)]
    return sems, list(res[2 * n_g:2 * n_g + n_src]), list(res[2 * n_g + n_src:2 * n_g + n_src + n]), res[-1]


def _exchange_wait(name, srcs, lands, pieces, waits, after):
    n_src, n, n_g = len(srcs), len(lands), len(waits)

    def body(*refs):
        src_refs, land_refs = refs[:n_src], refs[n_src:n_src + n]
        sems = refs[n_src + n:n_src + n + 2 * n_g]
        x, y, c, me = _my_position()
        at = 0
        for g, (idxs, _, _) in enumerate(waits):
            for k in PEER_ORDER:
                peer, pid = _peer(x, y, c, k)
                for j, i in enumerate(idxs):
                    p = pieces[i]
                    cp = pltpu.make_async_remote_copy(
                        src_ref=p.send(src_refs[p.src], pid), dst_ref=p.slot(land_refs[at + j], pid),
                        send_sem=sems[2 * g].at[(k - 1) * len(idxs) + j], recv_sem=sems[2 * g + 1].at[(k - 1) * len(idxs) + j],
                        device_id=peer, device_id_type=MESH)
                    cp.wait_send()
                    cp.wait_recv()
            at += len(idxs)

    sem_args = [s for _, send, recv in waits for s in (send, recv)]
    res = pl.pallas_call(
        body, in_specs=[HBM_SPEC] * (n_src + n) + [SEM_SPEC] * (2 * n_g) + [ANY_SPEC],
        out_specs=[HBM_SPEC] * (n_src + n), out_shape=_hbm_like(srcs) + _hbm_like(lands),
        input_output_aliases={i: i for i in range(n_src + n)},
        compiler_params=pltpu.CompilerParams(has_side_effects=pltpu.SideEffectType.DATAFLOW_SIDE_EFFECTING),
        name=name,
    )(*srcs, *lands, *sem_args, after)
    return list(res[:n_src]), list(res[n_src:])


def _weight_pieces(p_in, p_out, p_up, p_down):
    depth, cin, d = p_in.shape
    rout, hs = p_out.shape[1], p_up.shape[2]
    pieces = []
    for l in range(depth):
        whole = functools.partial(lambda ref, dev, l: ref.at[l], l=l)
        layer = functools.partial(lambda dev, l: (l, 0, 0), l=l)

        def rows(src, n_rows, whole=whole, layer=layer):
            return _Piece(src, whole, lambda ref, dev: _rows_of(n_rows)(ref.at[0], dev), (1, N_DEV * n_rows, d),
                          ((None, n_rows, d), layer), ((None, n_rows, d), lambda dev: (0, dev, 0)))

        pieces += [
            rows(0, cin), rows(1, rout),
            _Piece(2, whole, lambda ref, dev: ref.at[0, dev], (1, N_DEV, d, hs),
                   ((None, d, hs), layer), ((None, None, d, hs), lambda dev: (0, dev, 0, 0))),
            rows(3, hs),
        ]
    return pieces


def _grad_pieces(g_pair, kinds):
    pieces = []
    for i, (g, kind) in enumerate(zip(g_pair, kinds)):
        lead = lambda dev: (dev, 0, 0)
        if kind == "up":
            blk = ((None,) + g.shape[1:], lead)
            pieces.append(_Piece(i, lambda ref, dev: ref.at[dev], lambda ref, dev: ref.at[dev], g.shape, blk, blk))
        else:
            rows, cols = g.shape[0] // N_DEV, g.shape[1]
            pieces.append(_Piece(i, _rows_of(rows), lambda ref, dev: ref.at[dev], (N_DEV, rows, cols),
                                 ((rows, cols), lambda dev: (dev, 0)), ((None, rows, cols), lead)))
    return pieces


def _all_reduce_small(vec):
    rows = vec.shape[0]

    def body(v_ref, o_ref, buf_ref, send_sems, recv_sems):
        x, y, c, me = _my_position()
        buf_ref[me] = v_ref[...]
        sends = []
        for k in PEER_ORDER:
            peer, _ = _peer(x, y, c, k)
            cp = pltpu.make_async_remote_copy(src_ref=v_ref, dst_ref=buf_ref.at[me], send_sem=send_sems.at[k - 1],
                                              recv_sem=recv_sems.at[k - 1], device_id=peer, device_id_type=MESH)
            cp.start()
            sends.append(cp)
        for k in PEER_ORDER:
            peer, pid = _peer(x, y, c, k)
            pltpu.make_async_remote_copy(src_ref=v_ref, dst_ref=buf_ref.at[pid], send_sem=send_sems.at[k - 1],
                                         recv_sem=recv_sems.at[k - 1], device_id=peer, device_id_type=MESH).wait_recv()
        for cp in sends:
            cp.wait_send()
        total = buf_ref[0]
        for dev in range(1, N_DEV):
            total = total + buf_ref[dev]
        o_ref[...] = total

    vm = pl.BlockSpec(memory_space=pltpu.VMEM)
    return pl.pallas_call(
        body, in_specs=[vm], out_specs=vm, out_shape=jax.ShapeDtypeStruct(vec.shape, F32),
        scratch_shapes=[pltpu.VMEM((N_DEV, rows, LANES), F32), pltpu.SemaphoreType.DMA((N_DEV - 1,)),
                        pltpu.SemaphoreType.DMA((N_DEV - 1,))],
        name="all_reduce_small",
    )(vec)


def _resident(block_shape, index_map):
    return pl.BlockSpec(block_shape, index_map, pipeline_mode=pl.Buffered(1))


def _fwd_inproj(layer, x, gain, g_in, cos, sin, tm=512):
    s, d = x.shape

    def body(x_ref, gain_ref, w_ref, cos_ref, sin_ref, proj_ref, h_ref):
        h = _rms_fwd(x_ref[...], gain_ref[...]).astype(BF)
        h_ref[...] = h
        cs = _tile_lanes(cos_ref[...], SEG // LANES)
        sn = _tile_lanes(sin_ref[...], SEG // LANES)
        for seg in range(N_SEG):
            acc = _dot(h, w_ref[seg * SEG:(seg + 1) * SEG, :], NT)
            if seg < 2:
                acc = acc * cs + _rope_partner(acc) * sn
            if seg == 0:
                acc = acc * ATTN_SCALE
            proj_ref[:, seg * SEG:(seg + 1) * SEG] = acc

    return pl.pallas_call(
        body, grid=(s // tm,),
        in_specs=[pl.BlockSpec((tm, d), lambda i: (i, 0)), pl.BlockSpec((None, 1, d), lambda i: (layer, 0, 0)),
                  _resident((None, PROJ_W, d), lambda i: (0, 0, 0)),
                  pl.BlockSpec((tm, LANES), lambda i: (i, 0)), pl.BlockSpec((tm, LANES), lambda i: (i, 0))],
        out_specs=[pl.BlockSpec((tm, PROJ_W), lambda i: (i, 0)), pl.BlockSpec((tm, d), lambda i: (i, 0))],
        out_shape=[jax.ShapeDtypeStruct((s, PROJ_W), F32), jax.ShapeDtypeStruct((s, d), BF)],
        compiler_params=_cparams(("parallel",)), name=f"fwd_inproj_l{layer}",
    )(x, gain, g_in, cos, sin)


ATTN_UNIT = SPAN * max(DILATIONS)
ATTN_GROUP = 4


def _attn_masks(first_block_has_prev):
    row = lax.broadcasted_iota(jnp.int32, (SPAN, 2 * SPAN), 0)
    col = lax.broadcasted_iota(jnp.int32, (SPAN, 2 * SPAN), 1)
    band = (col >= row) & (col <= row + SPAN)
    lane = lax.broadcasted_iota(jnp.int32, (SPAN, LANES), 1)
    return band & ((col >= SPAN) | first_block_has_prev), band, lane < 64


def _attn_specs(n_in_extra):
    pairs = ATTN_W // LANES
    q_spec = pl.BlockSpec((ATTN_UNIT, LANES), lambda p, n: (n, p))

    def prev(seg):
        return pl.BlockSpec((ATTN_UNIT, LANES), lambda p, n: (jnp.maximum(n - 1, 0), seg * pairs + p))

    def cur(seg):
        return pl.BlockSpec((ATTN_UNIT, LANES), lambda p, n: (n, seg * pairs + p))

    return [q_spec, prev(1), cur(1), prev(2), cur(2)] + [q_spec] * n_in_extra


def _attn_groups(dil):
    blocks = ATTN_UNIT // (SPAN * dil)
    pairs = [(r, b) for r in range(dil) for b in range(blocks)]
    return [pairs[i:i + ATTN_GROUP] for i in range(0, len(pairs), ATTN_GROUP)]


def _block_rows(dil, r, b, n=1):
    start = r + dil * SPAN * b
    return pl.ds(start, n * SPAN, stride=dil) if dil > 1 else pl.ds(start, n * SPAN)


def _block_keys(prev_ref, cur_ref, dil, r, b):
    if b > 0:
        return cur_ref[_block_rows(dil, r, b - 1, 2), :]
    last = ATTN_UNIT // (SPAN * dil) - 1
    return jnp.concatenate([prev_ref[_block_rows(dil, r, last), :], cur_ref[_block_rows(dil, r, 0), :]], axis=0)


def _attn_fwd(layer, proj):
    s = proj.shape[0]
    n_pat = len(DILATIONS)
    merge_rows = 256

    def body(q_ref, kp_ref, kc_ref, vp_ref, vc_ref, o_ref, lse_ref, o_scr, lse_scr):
        m_first, m_rest, is_a = _attn_masks(pl.program_id(1) > 0)
        sels = (is_a, jnp.logical_not(is_a))
        is_a_keys = lax.broadcasted_iota(jnp.int32, (2 * SPAN, LANES), 1) < 64
        for pi, dil in enumerate(DILATIONS):
            for group in _attn_groups(dil):
                items = [(r, b, h) for r, b in group for h in range(2)]
                q = {rb: q_ref[_block_rows(dil, *rb), :] for rb in group}
                k = {rb: _block_keys(kp_ref, kc_ref, dil, *rb).astype(BF) for rb in group}
                v = {rb: _block_keys(vp_ref, vc_ref, dil, *rb).astype(BF) for rb in group}
                v_sum = {rb: (jnp.where(is_a_keys, v[rb], 1.0), jnp.where(is_a_keys, 1.0, v[rb])) for rb in group}
                sc = [jnp.where(m_first if b == 0 else m_rest,
                                _dot(jnp.where(sels[h], q[r, b], 0.0).astype(BF), k[r, b], NT), MASK_VALUE)
                      for r, b, h in items]
                mx = [jnp.max(jnp.maximum(t[:, :SPAN], t[:, SPAN:]), axis=-1, keepdims=True) for t in sc]
                p = [jnp.exp(t - m).astype(BF) for t, m in zip(sc, mx)]
                both = [_dot(t, v_sum[r, b][h], NN) for t, (r, b, h) in zip(p, items)]
                for j, (r, b) in enumerate(group):
                    t_a, t_b = both[2 * j], both[2 * j + 1]
                    den = pltpu.roll(jnp.where(is_a, t_b, t_a), 64, 1)
                    o_scr[pi, _block_rows(dil, r, b), :] = jnp.where(is_a, t_a, t_b) / den
                    lse_scr[pi, _block_rows(dil, r, b), :] = jnp.where(is_a, mx[2 * j], mx[2 * j + 1]) + jnp.log(den)
        for c in range(ATTN_UNIT // merge_rows):
            rows = slice(c * merge_rows, (c + 1) * merge_rows)
            ls = [lse_scr[pi, rows, :] for pi in range(n_pat)]
            mx = functools.reduce(jnp.maximum, ls)
            ws = [jnp.exp(l - mx) for l in ls]
            den = functools.reduce(jnp.add, ws)
            o_ref[rows, :] = functools.reduce(jnp.add, [w * o_scr[pi, rows, :] for pi, w in enumerate(ws)]) / den
            lse_ref[rows, :] = mx + jnp.log(den)

    out_spec = pl.BlockSpec((ATTN_UNIT, LANES), lambda p, n: (n, p))
    return pl.pallas_call(
        body, grid=(ATTN_W // LANES, s // ATTN_UNIT), in_specs=_attn_specs(0), out_specs=[out_spec, out_spec],
        out_shape=[jax.ShapeDtypeStruct((s, ATTN_W), F32)] * 2,
        scratch_shapes=[pltpu.VMEM((n_pat, ATTN_UNIT, LANES), F32)] * 2,
        compiler_params=_cparams(("parallel", "arbitrary")), name=f"attn_fwd_l{layer}",
    )(proj, proj, proj, proj, proj)


def _attn_norm(layer, o, gain, mixed, tm=512):
    s = o.shape[0]

    def body(o_ref, gain_ref, mixed_ref, n_ref):
        n_ref[...] = _rms_fwd(o_ref[...], gain_ref[...]).astype(BF)

    blk = pl.BlockSpec((tm, ATTN_W), lambda i: (i, 0))
    return pl.pallas_call(
        body, grid=(s // tm,),
        in_specs=[blk, pl.BlockSpec((None, 1, ATTN_W), lambda i: (layer, 0, 0)), pl.BlockSpec(memory_space=pl.ANY)],
        out_specs=blk, out_shape=jax.ShapeDtypeStruct(mixed.shape, BF), input_output_aliases={2: 0},
        compiler_params=_cparams(("parallel",)), name=f"attn_norm_l{layer}",
    )(o, gain, mixed)


def _chunk_cumsum(x, reverse=False):
    n = x.shape[0]
    pos = lax.broadcasted_iota(jnp.int32, x.shape, 0) % HGRN_CHUNK
    for sh in (1, 2, 4, 8):
        if reverse:
            x = x + jnp.where(pos < HGRN_CHUNK - sh, pltpu.roll(x, n - sh, 0), 0.0)
        else:
            x = x + jnp.where(pos >= sh, pltpu.roll(x, sh, 0), 0.0)
    return x


def _chunk_row(x, row):
    r, n = x.shape
    x3 = x.reshape(r // HGRN_CHUNK, HGRN_CHUNK, n)
    return jnp.broadcast_to(x3[:, row:row + 1, :], x3.shape).reshape(r, n)


def _hgrn_pre(qh, z, lb):
    sig = _sigmoid(z)
    f = lb + (1.0 - lb) * sig
    k = (1.0 - lb) * _sigmoid(-z)
    sq = _sigmoid(qh)
    q = qh * sq * HGRN_SCALE
    g = _chunk_cumsum(jnp.log(f))
    g_mid = _chunk_row(g, HGRN_CHUNK // 2 - 1)
    g_last = _chunk_row(g, HGRN_CHUNK - 1)
    e_q, e_k = jnp.exp(g - g_mid), jnp.exp(g_mid - g)
    e_in, e_out = jnp.exp(g), jnp.exp(g_last - g)
    return dict(sig=sig, f=f, k=k, sq=sq, q=q, g_last=g_last, e_q=e_q, e_k=e_k, e_in=e_in, e_out=e_out,
                qt=q * e_q, kt=k * e_k, qg=q * e_in, kout=k * e_out)


def _hgrn_mask():
    row = lax.broadcasted_iota(jnp.int32, (LANES, LANES), 0)
    col = lax.broadcasted_iota(jnp.int32, (LANES, LANES), 1)
    return (row // HGRN_CHUNK == col // HGRN_CHUNK) & (col <= row)


def _hgrn_in_specs(layer, rev, nblk):
    def blk(b):
        return nblk - 1 - b if rev else b
    first = 3 * ATTN_W // HGRN_W
    specs = [pl.BlockSpec((HGRN_ROWS, HGRN_W), functools.partial(lambda b, seg: (blk(b), first + seg), seg=seg))
             for seg in range(4)]
    specs.append(pl.BlockSpec((None, 1, HGRN_W), lambda b: (layer, 0, 0)))
    specs.append(pl.BlockSpec((None, 1, HGRN_DIM), lambda b: (layer, 0, 0)))
    return specs, blk


def _head(x, h):
    return x[:, h * HGRN_DIM:(h + 1) * HGRN_DIM]


def _chunk(x, c):
    return x[c * HGRN_CHUNK:(c + 1) * HGRN_CHUNK]


def _sub(x, sb):
    return x[sb * LANES:(sb + 1) * LANES]


HGRN_ROWS = 256
HEADS = range(HGRN_HEADS)
SUBS = range(HGRN_ROWS // LANES)
CHUNKS = range(HGRN_ROWS // HGRN_CHUNK)


def _hgrn_fwd(layer, proj, lb, gain):
    s = proj.shape[0]
    nblk = s // HGRN_ROWS
    cpb = len(CHUNKS)

    def body(q_ref, f_ref, i_ref, g_ref, lb_ref, gain_ref, o_ref, rec_ref, st_ref, state):
        @pl.when(pl.program_id(0) == 0)
        def _():
            state[...] = jnp.zeros(state.shape, F32)

        pre = _hgrn_pre(q_ref[...], f_ref[...], lb_ref[...])
        v = i_ref[...].astype(BF)
        qt, kt, qg, kout = (pre[n].astype(BF) for n in ("qt", "kt", "qg", "kout"))
        dec = jnp.exp(pre["g_last"])
        mask = _hgrn_mask()
        a = [[jnp.where(mask, _dot(_sub(_head(qt, h), sb), _sub(_head(kt, h), sb), NT), 0.0).astype(BF) for sb in SUBS]
             for h in HEADS]
        o_intra = [[_dot(a[h][sb], _sub(_head(v, h), sb), NN) for sb in SUBS] for h in HEADS]
        update = [[_dot(_chunk(_head(v, h), c), _chunk(_head(kout, h), c), TN) for c in CHUNKS] for h in HEADS]
        for h in HEADS:
            st = state[h]
            for c in CHUNKS:
                st_ref[h, c * LANES:(c + 1) * LANES, :] = st.astype(BF)
                st = st * _head(dec, h)[c * HGRN_CHUNK:c * HGRN_CHUNK + 1, :] + update[h][c]
            state[h] = st
        inter = [[_dot(_chunk(_head(qg, h), c), st_ref[h, c * LANES:(c + 1) * LANES, :].astype(BF), NT) for c in CHUNKS]
                 for h in HEADS]
        o = [jnp.concatenate(o_intra[h], axis=0) + jnp.concatenate(inter[h], axis=0) for h in HEADS]
        o_ref[...] = jnp.concatenate(o, axis=1)
        gate = g_ref[...]
        normed = jnp.concatenate([_rms_fwd(o[h], gain_ref[...]) for h in HEADS], axis=1)
        rec_ref[...] = (normed * (gate * _sigmoid(gate))).astype(BF)

    specs, _ = _hgrn_in_specs(layer, False, nblk)
    return pl.pallas_call(
        body, grid=(nblk,), in_specs=specs,
        out_specs=[pl.BlockSpec((HGRN_ROWS, HGRN_W), lambda b: (b, 0)), pl.BlockSpec((HGRN_ROWS, HGRN_W), lambda b: (b, 1)),
                   pl.BlockSpec((HGRN_HEADS, cpb * LANES, LANES), lambda b: (0, b, 0))],
        out_shape=[jax.ShapeDtypeStruct((s, HGRN_W), F32), jax.ShapeDtypeStruct((s, MIX_W), BF),
                   jax.ShapeDtypeStruct((HGRN_HEADS, nblk * cpb * LANES, LANES), BF)],
        scratch_shapes=[pltpu.VMEM((HGRN_HEADS, LANES, LANES), F32)],
        compiler_params=_cparams(("arbitrary",)), name=f"hgrn_fwd_l{layer}",
    )(proj, proj, proj, proj, lb, gain)


def _relu2(u):
    return jnp.square(jnp.maximum(u, 0)).astype(BF)


def _mlp_fwd(layer, x, mixed, gain, g_out, g_up, g_down, tm=256):
    s, d = x.shape
    mw = mixed.shape[1]
    nblk, hs = g_up.shape[1], g_up.shape[3]

    def body(x_ref, m_ref, gain_ref, out_w_ref, up_ref, down_ref, o_ref, mid_ref, u_ref, h_ref, a_buf):
        xv = x_ref[...] + _dot(m_ref[...], out_w_ref[...], NN)
        mid_ref[...] = xv
        h = _rms_fwd(xv, gain_ref[...]).astype(BF)
        h_ref[...] = h
        for j in range(nblk):
            u = _dot(h, up_ref[j], NN)
            u_ref[:, j * hs:(j + 1) * hs] = u.astype(BF)
            a_buf[:, j * hs:(j + 1) * hs] = _relu2(u)
        acc = xv
        for j in range(nblk):
            acc = acc + _dot(a_buf[:, j * hs:(j + 1) * hs], down_ref[j * hs:(j + 1) * hs, :], NN)
        o_ref[...] = acc

    row = pl.BlockSpec((tm, d), lambda i: (i, 0))
    return pl.pallas_call(
        body, grid=(s // tm,),
        in_specs=[row, pl.BlockSpec((tm, mw), lambda i: (i, 0)), pl.BlockSpec((None, 1, d), lambda i: (layer, 0, 0)),
                  _resident((None, mw, d), lambda i: (0, 0, 0)),
                  _resident((None, nblk, d, hs), lambda i: (0, 0, 0, 0)),
                  _resident((None, nblk * hs, d), lambda i: (0, 0, 0))],
        out_specs=[row, row, pl.BlockSpec((tm, nblk * hs), lambda i: (i, 0)), row],
        out_shape=[jax.ShapeDtypeStruct((s, d), F32), jax.ShapeDtypeStruct((s, d), F32),
                   jax.ShapeDtypeStruct((s, nblk * hs), BF), jax.ShapeDtypeStruct((s, d), BF)],
        scratch_shapes=[pltpu.VMEM((tm, nblk * hs), BF)],
        compiler_params=_cparams(("parallel",)), name=f"mlp_fwd_l{layer}",
    )(x, mixed, gain, g_out, g_up, g_down)


def _loss_head(x, gain, target, tm=512):
    s, d = x.shape

    def body(x_ref, gain_ref, t_ref, dx_ref, dxb_ref, dgain_ref, loss_ref):
        i = pl.program_id(0)
        xv, gv = x_ref[...], gain_ref[...]
        err = _rms_fwd(xv, gv) - t_ref[...]
        dx, dgain = _rms_bwd(err * (1.0 / d), xv, gv)
        dx_ref[...] = dx
        dxb_ref[...] = dx.astype(BF)
        part = _part8(dgain)
        lpart = _part8(0.5 * jnp.mean(err * err, axis=-1, keepdims=True) * jnp.ones((1, LANES), F32))

        @pl.when(i == 0)
        def _():
            dgain_ref[...] = part
            loss_ref[...] = lpart

        @pl.when(i > 0)
        def _():
            dgain_ref[...] += part
            loss_ref[...] += lpart

    row = pl.BlockSpec((tm, d), lambda i: (i, 0))
    return pl.pallas_call(
        body, grid=(s // tm,),
        in_specs=[row, pl.BlockSpec((1, d), lambda i: (0, 0)), row],
        out_specs=[row, row, pl.BlockSpec((8, d), lambda i: (0, 0)), pl.BlockSpec((8, LANES), lambda i: (0, 0))],
        out_shape=[jax.ShapeDtypeStruct((s, d), F32), jax.ShapeDtypeStruct((s, d), BF), jax.ShapeDtypeStruct((8, d), F32),
                   jax.ShapeDtypeStruct((8, LANES), F32)],
        compiler_params=_cparams(("arbitrary",)), name="loss_head",
    )(x, gain, target)


def _accumulate_rows(i, ref, part):
    @pl.when(i == 0)
    def _():
        ref[...] = part

    @pl.when(i > 0)
    def _():
        ref[...] += part


def _mlp_bwd(layer, dx, dxb, x, gain, u, g_out, g_up, g_down, tm=256):
    s, d = x.shape
    mw = g_out.shape[1]
    nblk, hs = g_up.shape[1], g_up.shape[3]

    def body(dx_ref, dxb_ref, x_ref, gain_ref, u_ref, out_w_ref, up_ref, down_ref, o_ref, ob_ref, du_ref, dm_ref,
             dgain_ref):
        dxb_v = dxb_ref[...]
        for j in range(nblk):
            cols = slice(j * hs, (j + 1) * hs)
            da = _dot(dxb_v, down_ref[cols, :], NT)
            du_ref[:, cols] = (da * (2.0 * jnp.maximum(u_ref[:, cols].astype(F32), 0.0))).astype(BF)
        acc = jnp.zeros((tm, d), F32)
        for j in range(nblk):
            acc = acc + _dot(du_ref[:, j * hs:(j + 1) * hs], up_ref[j], NT)
        dxn, dgain = _rms_bwd(acc, x_ref[...], gain_ref[...])
        out = dx_ref[...] + dxn
        out_b = out.astype(BF)
        o_ref[...] = out
        ob_ref[...] = out_b
        dm_ref[...] = _dot(out_b, out_w_ref[...], NT)
        _accumulate_rows(pl.program_id(0), dgain_ref, _part8(dgain))

    row = pl.BlockSpec((tm, d), lambda i: (i, 0))
    wide = pl.BlockSpec((tm, nblk * hs), lambda i: (i, 0))
    return pl.pallas_call(
        body, grid=(s // tm,),
        in_specs=[row, row, row, pl.BlockSpec((None, 1, d), lambda i: (layer, 0, 0)), wide,
                  _resident((None, mw, d), lambda i: (0, 0, 0)),
                  _resident((None, nblk, d, hs), lambda i: (0, 0, 0, 0)),
                  _resident((None, nblk * hs, d), lambda i: (0, 0, 0))],
        out_specs=[row, row, wide, pl.BlockSpec((tm, mw), lambda i: (i, 0)), pl.BlockSpec((8, d), lambda i: (0, 0))],
        out_shape=[jax.ShapeDtypeStruct((s, d), F32), jax.ShapeDtypeStruct((s, d), BF),
                   jax.ShapeDtypeStruct((s, nblk * hs), BF), jax.ShapeDtypeStruct((s, mw), F32),
                   jax.ShapeDtypeStruct((8, d), F32)],
        compiler_params=_cparams(("arbitrary",)), name=f"mlp_bwd_l{layer}",
    )(dx, dxb, x, gain, u, g_out, g_up, g_down)


def _attn_norm_bwd(layer, dmixed, o, gain, tm=512):
    s = o.shape[0]

    def body(dm_ref, o_ref, gain_ref, do_ref, delta_ref, dgain_ref):
        i = pl.program_id(0)
        ov = o_ref[...]
        do, dgain = _rms_bwd(dm_ref[...], ov, gain_ref[...])
        do_ref[...] = do
        row = lax.broadcasted_iota(jnp.int32, (ATTN_W, ATTN_W), 0)
        col = lax.broadcasted_iota(jnp.int32, (ATTN_W, ATTN_W), 1)
        same_head = jnp.where(row // 64 == col // 64, 1.0, 0.0)
        delta_ref[...] = jnp.dot(do * ov, same_head, precision=lax.Precision.HIGHEST, preferred_element_type=F32)
        part = _part8(dgain)

        @pl.when(i == 0)
        def _():
            dgain_ref[...] = part

        @pl.when(i > 0)
        def _():
            dgain_ref[...] += part

    blk = pl.BlockSpec((tm, ATTN_W), lambda i: (i, 0))
    return pl.pallas_call(
        body, grid=(s // tm,), in_specs=[blk, blk, pl.BlockSpec((None, 1, ATTN_W), lambda i: (layer, 0, 0))],
        out_specs=[blk, blk, pl.BlockSpec((8, ATTN_W), lambda i: (0, 0))],
        out_shape=[jax.ShapeDtypeStruct((s, ATTN_W), F32), jax.ShapeDtypeStruct((s, ATTN_W), F32),
                   jax.ShapeDtypeStruct((8, ATTN_W), F32)],
        compiler_params=_cparams(("arbitrary",)), name=f"attn_norm_bwd_l{layer}",
    )(dmixed, o, gain)


def _attn_bwd(layer, proj, do, lse, delta):
    s = proj.shape[0]

    def body(q_ref, kp_ref, kc_ref, vp_ref, vc_ref, do_ref, lse_ref, delta_ref, dq_ref, dk_ref, dkp_ref, dv_ref, dvp_ref):
        m_first, m_rest, is_a = _attn_masks(pl.program_id(1) > 0)
        is_a_keys = lax.broadcasted_iota(jnp.int32, (2 * SPAN, LANES), 1) < 64
        sels = (is_a, jnp.logical_not(is_a))
        key_sels = (is_a_keys, jnp.logical_not(is_a_keys))
        for ref in (dq_ref, dk_ref, dkp_ref, dv_ref, dvp_ref):
            ref[...] = jnp.zeros(ref.shape, F32)
        for dil in DILATIONS:
            last = ATTN_UNIT // (SPAN * dil) - 1
            for group in _attn_groups(dil):
                items = [(r, b, h) for r, b in group for h in range(2)]
                q = {rb: q_ref[_block_rows(dil, *rb), :] for rb in group}
                dov = {rb: do_ref[_block_rows(dil, *rb), :] for rb in group}
                lse_v = {rb: lse_ref[_block_rows(dil, *rb), :] for rb in group}
                delta_v = {rb: delta_ref[_block_rows(dil, *rb), :] for rb in group}
                k = {rb: _block_keys(kp_ref, kc_ref, dil, *rb) for rb in group}
                v = {rb: _block_keys(vp_ref, vc_ref, dil, *rb).astype(BF) for rb in group}
                qh = [jnp.where(sels[h], q[r, b], 0.0).astype(BF) for r, b, h in items]
                doh = [jnp.where(sels[h], dov[r, b], 0.0).astype(BF) for r, b, h in items]
                kh = [jnp.where(key_sels[h], k[r, b], 0.0).astype(BF) for r, b, h in items]
                sc = [jnp.where(m_first if b == 0 else m_rest, _dot(qh[i], kh[i], NT), MASK_VALUE)
                      for i, (r, b, h) in enumerate(items)]
                dp = [_dot(doh[i], v[r, b], NT) for i, (r, b, h) in enumerate(items)]
                p = [jnp.exp(sc[i] - lse_v[r, b][:, 64 * h:64 * h + 1]) for i, (r, b, h) in enumerate(items)]
                ds = [(p[i] * (dp[i] - delta_v[r, b][:, 64 * h:64 * h + 1])).astype(BF) for i, (r, b, h) in enumerate(items)]
                dv = [_dot(p[i].astype(BF), doh[i], TN) for i in range(len(items))]
                dq = [_dot(ds[i], kh[i], NN) for i in range(len(items))]
                dk = [_dot(ds[i], qh[i], TN) for i in range(len(items))]
                for j, (r, b) in enumerate(group):
                    own = _block_rows(dil, r, b)
                    dq_ref[own, :] += dq[2 * j] + dq[2 * j + 1]
                    dk2, dv2 = dk[2 * j] + dk[2 * j + 1], dv[2 * j] + dv[2 * j + 1]
                    dk_ref[own, :] += dk2[SPAN:]
                    dv_ref[own, :] += dv2[SPAN:]
                    if b > 0:
                        before = _block_rows(dil, r, b - 1)
                        dk_ref[before, :] += dk2[:SPAN]
                        dv_ref[before, :] += dv2[:SPAN]
                    else:
                        before = _block_rows(dil, r, last)
                        dkp_ref[before, :] += dk2[:SPAN]
                        dvp_ref[before, :] += dv2[:SPAN]

    out_spec = pl.BlockSpec((ATTN_UNIT, LANES), lambda p, n: (n, p))
    return pl.pallas_call(
        body, grid=(ATTN_W // LANES, s // ATTN_UNIT), in_specs=_attn_specs(3), out_specs=[out_spec] * 5,
        out_shape=[jax.ShapeDtypeStruct((s, ATTN_W), F32)] * 5,
        compiler_params=_cparams(("parallel", "arbitrary")), name=f"attn_bwd_l{layer}",
    )(proj, proj, proj, proj, proj, do, lse, delta)


def _attn_combine(layer, parts, cos, sin):
    dq, dk, dkp, dv, dvp = parts
    s = dq.shape[0]
    nblk = s // SPAN
    per_unit = ATTN_UNIT // SPAN
    here = pl.BlockSpec((SPAN, ATTN_W), lambda i: (i, 0))
    ahead = pl.BlockSpec((SPAN, ATTN_W), lambda i: (jnp.minimum(i + per_unit, nblk - 1), 0))
    tab = pl.BlockSpec((SPAN, LANES), lambda i: (i, 0))

    def body(dq_ref, dk_ref, dkp_ref, dv_ref, dvp_ref, cos_ref, sin_ref, out_ref):
        has_next = pl.program_id(0) + per_unit < nblk
        dqv = dq_ref[...]
        dkv = dk_ref[...] + jnp.where(has_next, dkp_ref[...], 0.0)
        dvv = dv_ref[...] + jnp.where(has_next, dvp_ref[...], 0.0)
        cs = _tile_lanes(cos_ref[...], ATTN_W // LANES)
        sn = _tile_lanes(sin_ref[...], ATTN_W // LANES)
        out_ref[0] = ((dqv * cs - _rope_partner(dqv) * sn) * ATTN_SCALE).astype(BF)
        out_ref[1] = (dkv * cs - _rope_partner(dkv) * sn).astype(BF)
        out_ref[2] = dvv.astype(BF)

    return pl.pallas_call(
        body, grid=(nblk,), in_specs=[here, here, ahead, here, ahead, tab, tab],
        out_specs=pl.BlockSpec((3, SPAN, ATTN_W), lambda i: (0, i, 0)),
        out_shape=jax.ShapeDtypeStruct((3, s, ATTN_W), BF),
        compiler_params=_cparams(("parallel",)), name=f"attn_combine_l{layer}",
    )(dq, dk, dkp, dv, dvp, cos, sin)


def _hgrn_bwd(layer, proj, lb, gain, o, dmixed, states):
    s = proj.shape[0]
    nblk = s // HGRN_ROWS
    cpb = len(CHUNKS)

    def body(q_ref, f_ref, i_ref, g_ref, lb_ref, gain_ref, o_ref, drec_ref, st_ref, dseg_ref, dlb_ref, dgain_ref,
             dstate, dst_buf):
        step = pl.program_id(0)

        @pl.when(step == 0)
        def _():
            dstate[...] = jnp.zeros(dstate.shape, F32)

        lbv, gv = lb_ref[...], gain_ref[...]
        qh, z, gate_in = q_ref[...], f_ref[...], g_ref[...]
        pre = _hgrn_pre(qh, z, lbv)
        v = i_ref[...].astype(BF)
        sg = _sigmoid(gate_in)
        ov, drec = o_ref[...], drec_ref[...]
        dnormed = drec * (gate_in * sg)
        back = [_rms_bwd(_head(dnormed, h), _head(ov, h), gv) for h in HEADS]
        do_b = jnp.concatenate([b[0] for b in back], axis=1).astype(BF)
        dgain = back[0][1] + back[1][1] + back[2][1] + back[3][1]
        normed = jnp.concatenate([_rms_fwd(_head(ov, h), gv) for h in HEADS], axis=1)
        dgate_in = drec * normed * (sg * (1.0 + gate_in * (1.0 - sg)))
        mask = _hgrn_mask()
        qt, kt, qg, kout = (pre[n].astype(BF) for n in ("qt", "kt", "qg", "kout"))
        dec = jnp.exp(pre["g_last"])
        def intra(fn):
            return jnp.concatenate([jnp.concatenate([fn(h, sb) for sb in SUBS], axis=0) for h in HEADS], axis=1)

        def hs(x, h, sb):
            return _sub(_head(x, h), sb)

        a = [[jnp.where(mask, _dot(hs(qt, h, sb), hs(kt, h, sb), NT), 0.0).astype(BF) for sb in SUBS] for h in HEADS]
        da = [[jnp.where(mask, _dot(hs(do_b, h, sb), hs(v, h, sb), NT), 0.0).astype(BF) for sb in SUBS] for h in HEADS]
        dv_intra = intra(lambda h, sb: _dot(a[h][sb], hs(do_b, h, sb), TN))
        dqt = intra(lambda h, sb: _dot(da[h][sb], hs(kt, h, sb), NN))
        dkt = intra(lambda h, sb: _dot(da[h][sb], hs(qt, h, sb), TN))
        feed = [[_dot(_chunk(_head(do_b, h), c), _chunk(_head(qg, h), c), TN) for c in CHUNKS] for h in HEADS]
        for h in HEADS:
            dst = dstate[h]
            for c in reversed(CHUNKS):
                dst_buf[h, c * LANES:(c + 1) * LANES, :] = dst
                dst = dst * _head(dec, h)[c * HGRN_CHUNK:c * HGRN_CHUNK + 1, :] + feed[h][c]
            dstate[h] = dst

        def per_chunk(fn):
            cols = []
            for h in HEADS:
                rows = [jnp.broadcast_to(t, (HGRN_CHUNK, HGRN_DIM)) for t in (fn(h, c) for c in CHUNKS)]
                cols.append(jnp.concatenate(rows, axis=0))
            return jnp.concatenate(cols, axis=1)

        def st_prev(h, c):
            return st_ref[h, c * LANES:(c + 1) * LANES, :]

        def dst_at(h, c):
            return dst_buf[h, c * LANES:(c + 1) * LANES, :]

        dqg = per_chunk(lambda h, c: _dot(_chunk(_head(do_b, h), c), st_prev(h, c).astype(BF), NN))
        dkout = per_chunk(lambda h, c: _dot(_chunk(_head(v, h), c), dst_at(h, c).astype(BF), NN))
        dv_inter = per_chunk(lambda h, c: _dot(_chunk(_head(kout, h), c), dst_at(h, c).astype(BF), NT))
        dg_state = per_chunk(lambda h, c: jnp.sum(dst_at(h, c) * st_prev(h, c).astype(F32), axis=0, keepdims=True))
        dg_kout = per_chunk(lambda h, c: jnp.sum(_chunk(_head(dkout * pre["kout"], h), c), axis=0, keepdims=True))
        dv = dv_intra + dv_inter
        pos = lax.broadcasted_iota(jnp.int32, (HGRN_ROWS, HGRN_W), 0) % HGRN_CHUNK
        dq = dqt * pre["e_q"] + dqg * pre["e_in"]
        dk = dkt * pre["e_k"] + dkout * pre["e_out"]
        dg = (dqt * pre["qt"] - dkt * pre["kt"] + dqg * pre["qg"] - dkout * pre["kout"]
              + jnp.where(pos == HGRN_CHUNK - 1, dg_state * dec + dg_kout, 0.0))
        dlogf = _chunk_cumsum(dg, reverse=True)
        sig, sq = pre["sig"], pre["sq"]
        df = dlogf / pre["f"] - dk
        dseg_ref[0] = (dq * HGRN_SCALE * (sq * (1.0 + qh * (1.0 - sq)))).astype(BF)
        dseg_ref[1] = (df * (1.0 - lbv) * sig * (1.0 - sig)).astype(BF)
        dseg_ref[2] = dv.astype(BF)
        dseg_ref[3] = dgate_in.astype(BF)
        _accumulate_rows(step, dlb_ref, _part8(df * (1.0 - sig)))
        _accumulate_rows(step, dgain_ref, _part8(dgain))

    specs, blk = _hgrn_in_specs(layer, True, nblk)
    specs += [pl.BlockSpec((HGRN_ROWS, HGRN_W), lambda b: (blk(b), 0)),
              pl.BlockSpec((HGRN_ROWS, HGRN_W), lambda b: (blk(b), 1)),
              pl.BlockSpec((HGRN_HEADS, cpb * LANES, LANES), lambda b: (0, blk(b), 0))]
    return pl.pallas_call(
        body, grid=(nblk,), in_specs=specs,
        out_specs=[pl.BlockSpec((4, HGRN_ROWS, HGRN_W), lambda b: (0, blk(b), 0)),
                   pl.BlockSpec((8, HGRN_W), lambda b: (0, 0)), pl.BlockSpec((8, HGRN_DIM), lambda b: (0, 0))],
        out_shape=[jax.ShapeDtypeStruct((4, s, HGRN_W), BF), jax.ShapeDtypeStruct((8, HGRN_W), F32),
                   jax.ShapeDtypeStruct((8, HGRN_DIM), F32)],
        scratch_shapes=[pltpu.VMEM((HGRN_HEADS, LANES, LANES), F32), pltpu.VMEM((HGRN_HEADS, cpb * LANES, LANES), F32)],
        compiler_params=_cparams(("arbitrary",)), name=f"hgrn_bwd_l{layer}",
    )(proj, proj, proj, proj, lb, gain, o, dmixed, states)


def _bwd_inproj(layer, dqkv, dhg, g_in, x, gain, dres, tm=512):
    s, d = x.shape

    def body(dqkv_ref, dhg_ref, w_ref, x_ref, gain_ref, dres_ref, dx_ref, dxb_ref, dgain_ref):
        acc = jnp.zeros((tm, d), F32)
        for seg in range(N_SEG):
            a = dqkv_ref[seg] if seg < 3 else dhg_ref[seg - 3]
            acc = acc + _dot(a, w_ref[seg * SEG:(seg + 1) * SEG, :], NN)
        dx, dgain = _rms_bwd(acc, x_ref[...], gain_ref[...])
        out = dres_ref[...] + dx
        dx_ref[...] = out
        dxb_ref[...] = out.astype(BF)
        _accumulate_rows(pl.program_id(0), dgain_ref, _part8(dgain))

    row = pl.BlockSpec((tm, d), lambda i: (i, 0))
    return pl.pallas_call(
        body, grid=(s // tm,),
        in_specs=[pl.BlockSpec((3, tm, SEG), lambda i: (0, i, 0)), pl.BlockSpec((4, tm, SEG), lambda i: (0, i, 0)),
                  _resident((None, PROJ_W, d), lambda i: (0, 0, 0)), row,
                  pl.BlockSpec((None, 1, d), lambda i: (layer, 0, 0)), row],
        out_specs=[row, row, pl.BlockSpec((8, d), lambda i: (0, 0))],
        out_shape=[jax.ShapeDtypeStruct((s, d), F32), jax.ShapeDtypeStruct((s, d), BF), jax.ShapeDtypeStruct((8, d), F32)],
        compiler_params=_cparams(("arbitrary",)), name=f"bwd_inproj_l{layer}",
    )(dqkv, dhg, g_in, x, gain, dres)


def _adamw(w, g, m, v):
    m2 = ADAM_B1 * m + (1.0 - ADAM_B1) * g
    v2 = ADAM_B2 * v + (1.0 - ADAM_B2) * (g * g)
    m_hat = m2 / (1.0 - ADAM_B1 ** ADAM_STEP)
    v_hat = v2 / (1.0 - ADAM_B2 ** ADAM_STEP)
    delta = -ADAM_LR * (m_hat / (jnp.sqrt(v_hat) + ADAM_EPS) + ADAM_WD * w)
    return delta, m2, v2


def _adam_big(name, parts, w, m, v, row_tiles):
    depth = w.shape[0]
    r, c = parts[0].shape[1], parts[0].shape[2]
    tr = r // row_tiles
    p_spec = pl.BlockSpec((N_DEV, tr, c), lambda t: (0, t, 0))
    w_spec = pl.BlockSpec((depth, tr, c), lambda t: (0, t, 0))

    def body(*refs):
        p_refs = refs[:depth]
        w_ref, m_ref, v_ref, g_ref, d_ref, m2_ref, v2_ref = refs[depth:]
        for l in range(depth):
            g = p_refs[l][0].astype(F32)
            for dev in range(1, N_DEV):
                g = g + p_refs[l][dev].astype(F32)
            delta, m2, v2 = _adamw(w_ref[l], g, m_ref[l], v_ref[l])
            g_ref[l] = g
            d_ref[l] = delta
            m2_ref[l] = m2
            v2_ref[l] = v2

    return pl.pallas_call(
        body, grid=(row_tiles,), in_specs=[p_spec] * depth + [w_spec] * 3, out_specs=[w_spec] * 4,
        out_shape=[jax.ShapeDtypeStruct(w.shape, F32)] * 4,
        compiler_params=_cparams(("parallel",)), name=name,
    )(*parts, w, m, v)


def _adam_small(g, w, m, v):
    def body(g_ref, w_ref, m_ref, v_ref, d_ref, m2_ref, v2_ref):
        delta, m2, v2 = _adamw(w_ref[...], g_ref[...], m_ref[...], v_ref[...])
        d_ref[...] = delta
        m2_ref[...] = m2
        v2_ref[...] = v2

    vm = pl.BlockSpec(memory_space=pltpu.VMEM)
    return pl.pallas_call(body, in_specs=[vm] * 4, out_specs=[vm] * 3, out_shape=[jax.ShapeDtypeStruct(g.shape, F32)] * 3,
                          name="adam_small")(g, w, m, v)


def _lower_bounds(logits):
    def body(l_ref, lb_ref, jac_ref):
        l0, l1 = l_ref[0:1, :], l_ref[1:2, :]
        mx = jnp.maximum(l0, l1)
        e0, e1 = jnp.exp(l0 - mx), jnp.exp(l1 - mx)
        p0, p1 = e0 / (e0 + e1), e1 / (e0 + e1)
        lb_ref[0:1, :] = p0 - p0
        lb_ref[1:2, :] = (p0 + p1) - p0
        jac_ref[0:1, :] = -p0 * p1
        jac_ref[1:2, :] = p0 * p1

    vm = pl.BlockSpec(memory_space=pltpu.VMEM)
    return pl.pallas_call(body, in_specs=[vm], out_specs=[vm, vm], out_shape=[jax.ShapeDtypeStruct(logits.shape, F32)] * 2,
                          name="hgrn_lower_bounds")(logits)


def _rope_tables(s):
    half = 32
    inv_freq = ROPE_THETA ** (-jnp.arange(half, dtype=F32) / half)
    ang = jnp.arange(s, dtype=jnp.int32).astype(F32)[:, None] * inv_freq[None, :]
    cos, sin = jnp.cos(ang), jnp.sin(ang)
    return jnp.concatenate([cos] * 4, axis=1), jnp.concatenate([-sin, sin, -sin, sin], axis=1)


SMALL_NAMES = ("norm_mix", "attn_out_gain", "hgrn_lb_logits", "hgrn_out_gain", "norm_mlp", "norm_final")


def _pack_small(vals):
    flat = jnp.concatenate([v.reshape(-1) for v in vals])
    rows = -(-flat.shape[0] // (8 * LANES)) * 8
    return jnp.pad(flat, (0, rows * LANES - flat.shape[0])).reshape(rows, LANES)


def _unpack_small(packed, like):
    flat, out, off = packed.reshape(-1), [], 0
    for v in like:
        out.append(flat[off:off + v.size].reshape(v.shape))
        off += v.size
    return out


def kernel(x, norm_mix, w_in, attn_out_gain, hgrn_lb_logits, hgrn_out_gain, w_out, norm_mlp, w_up, w_down, norm_final, loss_target, m_norm_mix, m_w_in, m_attn_out_gain, m_hgrn_lb_logits, m_hgrn_out_gain, m_w_out, m_norm_mlp, m_w_up, m_w_down, m_norm_final, v_norm_mix, v_w_in, v_attn_out_gain, v_hgrn_lb_logits, v_hgrn_out_gain, v_w_out, v_norm_mlp, v_w_up, v_w_down, v_norm_final):
    depth = w_in.shape[0]
    assert depth == 2 and x.shape[0] == 1
    s, d = x.shape[1], x.shape[2]
    x0 = x[0]
    target = loss_target[0]
    cos, sin = _rope_tables(s)
    g_mix, g_attn, g_hg, g_mlp = (norm_mix[:, None, :], attn_out_gain[:, None, :], hgrn_out_gain[:, None, :],
                                  norm_mlp[:, None, :])
    lb, lb_jac = _lower_bounds(hgrn_lb_logits)
    lb3 = lb[:, None, :]

    def flip(a):
        return jnp.swapaxes(a, 1, 2)

    shards = list(_pack_weights(flip(w_in), w_out, w_up, w_down))
    w_pieces = _weight_pieces(*shards)
    w_groups = [[0], [1, 2, 3], [4], [5, 6, 7]]
    me = (4 * lax.axis_index("x") + 2 * lax.axis_index("y") + lax.axis_index("c")).astype(jnp.int32).reshape(1)
    lands = _exchange_own("all_gather_own", me, shards, w_pieces)
    w_sems, shards, lands, token = _exchange_start("all_gather_start", shards, lands, w_pieces, w_groups)

    def weights_ready(group, after):
        nonlocal shards
        idxs = w_groups[group]
        shards, got = _exchange_wait(f"all_gather_wait{group}", shards, [lands[i] for i in idxs], w_pieces,
                                     [(idxs, *w_sems[group])], after)
        return got

    def tied(small_arr, tok):
        return small_arr + tok[0, 0]

    saved = []
    xl = x0
    full = [None] * depth
    for l in range(depth):
        (full_in,) = weights_ready(2 * l, token if l == 0 else xl)
        proj, h = _fwd_inproj(l, xl, g_mix, full_in, cos, sin)
        o_attn, lse = _attn_fwd(l, proj)
        o_hg, mixed, states = _hgrn_fwd(l, proj, lb3, g_hg)
        mixed = _attn_norm(l, o_attn, g_attn, mixed)
        full_out, full_up, full_down = weights_ready(2 * l + 1, mixed)
        x_next, x_mid, u, h2 = _mlp_fwd(l, xl, mixed, g_mlp, full_out, full_up, full_down)
        saved.append((xl, proj, h, o_attn, lse, o_hg, states, mixed, x_mid, u, h2))
        full[l] = (full_in, full_out, full_up, full_down)
        xl = x_next
    dx, dxb, dnorm_final8, loss8 = _loss_head(xl, norm_final[None, :], target)
    loss = lax.psum(jnp.sum(loss8[:, 0]), ("x", "y", "c"))

    exchanges = []

    def scatter(tag, grads, kinds):
        pieces = _grad_pieces(grads, kinds)
        own = _exchange_own(f"reduce_scatter_own_{tag}", me, grads, pieces)
        sems, grads, own, tok = _exchange_start(f"reduce_scatter_start_{tag}", grads, own, pieces, [list(range(len(pieces)))])
        exchanges.append((grads, own, pieces, sems[0]))
        return tok

    small = {}
    for l in reversed(range(depth)):
        xl, proj, h, o_attn, lse, o_hg, states, mixed, x_mid, u, h2 = saved[l]
        full_in, full_out, full_up, full_down = full[l]
        hs = full_up.shape[3]
        gw_down = _mm_tn(f"grad_w_down_l{l}", u, dxb, u.shape[1], a_fn=_relu2)
        dx_mid, dx_mid_b, du, dmixed, dmlp8 = _mlp_bwd(l, dx, dxb, x_mid, g_mlp, u, full_out, full_up, full_down)
        gw_up = _mm_tn(f"grad_w_up_l{l}", h2, du, d, out_block_w=hs)
        gw_out = _mm_tn(f"grad_w_out_l{l}", mixed, dx_mid_b, mixed.shape[1])
        g_attn_t = tied(g_attn, scatter(f"mlp_l{l}", [gw_down, gw_up, gw_out], ["rows", "up", "rows"]))
        do, delta, dattn8 = _attn_norm_bwd(l, dmixed, o_attn, g_attn_t)
        dqkv = _attn_combine(l, _attn_bwd(l, proj, do, lse, delta), cos, sin)
        dhg, dlb8, dhgain8 = _hgrn_bwd(l, proj, lb3, g_hg, o_hg, dmixed, states)
        gin = _mm_tn(f"grad_w_in_qkv_l{l}", dqkv, h, PROJ_W, a_lead=True)
        gw_in = _mm_tn(f"grad_w_in_hg_l{l}", dhg, h, PROJ_W, a_lead=True, out_block_off=3, prev=gin)
        g_mix_t = tied(g_mix, scatter(f"mix_l{l}", [gw_in], ["rows"]))
        dx, dxb, dmix8 = _bwd_inproj(l, dqkv, dhg, full_in, xl, g_mix_t, dx_mid)
        small[l] = (dmix8, dattn8, dlb8, dhgain8, dmlp8)

    def fin(p8):
        return jnp.sum(p8, axis=0)
    dlogits = lb_jac * fin(small[1][2])[None, :]
    small_grads = [jnp.stack([fin(small[l][0]) for l in range(depth)]), jnp.stack([fin(small[l][1]) for l in range(depth)]),
                   dlogits, jnp.stack([fin(small[l][3]) for l in range(depth)]),
                   jnp.stack([fin(small[l][4]) for l in range(depth)]), fin(dnorm_final8)]
    small_w = [norm_mix, attn_out_gain, hgrn_lb_logits, hgrn_out_gain, norm_mlp, norm_final]
    small_m = [m_norm_mix, m_attn_out_gain, m_hgrn_lb_logits, m_hgrn_out_gain, m_norm_mlp, m_norm_final]
    small_v = [v_norm_mix, v_attn_out_gain, v_hgrn_lb_logits, v_hgrn_out_gain, v_norm_mlp, v_norm_final]
    g_small = _all_reduce_small(_pack_small(small_grads))
    d_small, m_small, v_small = _adam_small(g_small, _pack_small(small_w), _pack_small(small_m), _pack_small(small_v))
    gs, ds, ms, vs = (_unpack_small(t, small_w) for t in (g_small, d_small, m_small, v_small))

    all_grads, all_lands, all_pieces, waits = [], [], [], []
    for grads, own, pieces, (send, recv) in exchanges:
        first = len(all_pieces)
        all_pieces += [p._replace(src=p.src + len(all_grads)) for p in pieces]
        waits.append((list(range(first, first + len(pieces))), send, recv))
        all_grads += grads
        all_lands += own
    _, landed = _exchange_wait("reduce_scatter_wait", all_grads, all_lands, all_pieces, waits, d_small)
    r_down, r_up, r_out, r_in = ([landed[4 + i], landed[i]] for i in range(4))
    big = {
        "w_in": [flip(t) for t in _adam_big("adam_w_in", r_in, flip(w_in), flip(m_w_in), flip(v_w_in), 2)],
        "w_out": _adam_big("adam_w_out", r_out, w_out, m_w_out, v_w_out, 1),
        "w_up": _adam_big("adam_w_up", r_up, w_up, m_w_up, v_w_up, 2),
        "w_down": _adam_big("adam_w_down", r_down, w_down, m_w_down, v_w_down, 4),
    }

    def gather(idx, small_list):
        by_name = dict(zip(SMALL_NAMES, small_list))
        return [by_name["norm_mix"], big["w_in"][idx], by_name["attn_out_gain"], by_name["hgrn_lb_logits"],
                by_name["hgrn_out_gain"], big["w_out"][idx], by_name["norm_mlp"], big["w_up"][idx], big["w_down"][idx],
                by_name["norm_final"]]

    return (loss, dx[None], *gather(0, gs), *gather(1, ds), *gather(2, ms), *gather(3, vs))
```

```python
import functools
from typing import Callable, NamedTuple

import jax
import jax.numpy as jnp
from jax import lax
from jax.experimental import pallas as pl
from jax.experimental.pallas import tpu as pltpu

F32 = jnp.float32
BF = jnp.bfloat16

N_DEV = 8
ATTN_W = 512
HGRN_W = 512
HGRN_HEADS = 4
HGRN_DIM = 128
SEG = 512
N_SEG = 7
PROJ_W = N_SEG * SEG
MIX_W = ATTN_W + HGRN_W
SPAN = 128
DILATIONS = (1, 4, 16)
HGRN_CHUNK = 16
ROPE_THETA = 10000.0
NORM_EPS = 1e-6
MASK_VALUE = -1e30
ATTN_SCALE = 0.125
HGRN_SCALE = HGRN_DIM ** -0.5
ADAM_LR = 0.001
ADAM_B1 = 0.9
ADAM_B2 = 0.999
ADAM_EPS = 1e-08
ADAM_WD = 0.01
ADAM_STEP = 10
LANES = 128
VMEM_LIMIT = 56 * 1024 * 1024

NN = ((1,), (0,))
NT = ((1,), (1,))
TN = ((0,), (0,))
MESH = pl.DeviceIdType.MESH


def _dot(a, b, dims):
    return lax.dot_general(a, b, (dims, ((), ())), preferred_element_type=F32)


def _cparams(sem):
    return pltpu.CompilerParams(dimension_semantics=sem, vmem_limit_bytes=VMEM_LIMIT)


def _part8(x):
    r, n = x.shape
    return jnp.sum(x.reshape(r // 8, 8, n), axis=0)


def _sigmoid(x):
    return 1.0 / (1.0 + jnp.exp(-x))


def _rms_fwd(x, gain):
    r = lax.rsqrt(jnp.mean(x * x, axis=-1, keepdims=True) + NORM_EPS)
    return x * r * gain


def _rms_bwd(dy, x, gain):
    r = lax.rsqrt(jnp.mean(x * x, axis=-1, keepdims=True) + NORM_EPS)
    xn = x * r
    dxn = dy * gain
    dx = r * (dxn - xn * jnp.mean(dxn * xn, axis=-1, keepdims=True))
    return dx, dy * xn


def _rope_partner(x):
    n = x.shape[-1]
    lane = lax.broadcasted_iota(jnp.int32, x.shape, x.ndim - 1)
    return jnp.where((lane % 64) < 32, pltpu.roll(x, n - 32, x.ndim - 1), pltpu.roll(x, 32, x.ndim - 1))


def _tile_lanes(t, reps):
    return jnp.concatenate([t] * reps, axis=-1)


def _mm_tn(name, a, b, out_rows, a_lead=False, out_block_off=0, prev=None, out_block_w=None, a_fn=None,
           tm=512, tn=1024, sub=512):
    kdim, n = b.shape
    m = a.shape[-1]
    tm, tn, sub = min(tm, m), min(tn, n), min(sub, kdim)
    mt = m // tm
    n_lead = a.shape[0] if a_lead else 1
    if a_lead:
        a_spec = pl.BlockSpec((None, kdim, tm), lambda j, i: (i // mt, 0, i % mt))
    else:
        a_spec = pl.BlockSpec((kdim, tm), lambda j, i: (0, i))
    b_spec = pl.BlockSpec((kdim, tn), lambda j, i: (0, j))
    if out_block_w:
        nb = tn // out_block_w
        o_shape = jax.ShapeDtypeStruct((n // out_block_w, out_rows, out_block_w), BF)
        o_spec = pl.BlockSpec((nb, tm, out_block_w), lambda j, i: (j, i + out_block_off, 0))
    else:
        nb = 0
        o_shape = jax.ShapeDtypeStruct((out_rows, n), BF)
        o_spec = pl.BlockSpec((tm, tn), lambda j, i: (i + out_block_off, j))
    arrays, specs, aliases = [a, b], [a_spec, b_spec], {}
    if prev is not None:
        arrays.append(prev)
        specs.append(pl.BlockSpec(memory_space=pl.ANY))
        aliases = {2: 0}

    def body(*refs):
        a_ref, b_ref, o_ref = refs[0], refs[1], refs[-1]
        acc = None
        for k in range(kdim // sub):
            av = a_ref[k * sub:(k + 1) * sub, :]
            if a_fn is not None:
                av = a_fn(av)
            part = _dot(av, b_ref[k * sub:(k + 1) * sub, :], TN)
            acc = part if acc is None else acc + part
        if nb:
            for t in range(nb):
                o_ref[t] = acc[:, t * out_block_w:(t + 1) * out_block_w].astype(BF)
        else:
            o_ref[...] = acc.astype(BF)

    return pl.pallas_call(
        body, grid=(n // tn, n_lead * mt), in_specs=specs, out_specs=o_spec, out_shape=o_shape,
        compiler_params=_cparams(("parallel", "parallel")), name=name, input_output_aliases=aliases,
    )(*arrays)


def _pack_weights(w_in_t, w_out, w_up, w_down):
    depth = w_in_t.shape[0]
    arrays = (w_in_t, w_out, w_up, w_down)

    def body(*refs):
        for src, dst in zip(refs[:4], refs[4:]):
            dst[...] = src[...].astype(BF)

    specs = [pl.BlockSpec((None,) + a.shape[1:], lambda l: (l, 0, 0)) for a in arrays]
    return pl.pallas_call(
        body, grid=(depth,), in_specs=specs, out_specs=specs,
        out_shape=[jax.ShapeDtypeStruct(a.shape, BF) for a in arrays],
        compiler_params=_cparams(("arbitrary",)), name="pack_weights",
    )(*arrays)


def _my_position():
    x, y, c = lax.axis_index("x"), lax.axis_index("y"), lax.axis_index("c")
    return x, y, c, 4 * x + 2 * y + c


def _peer(x, y, c, k):
    px = 1 - x if k & 4 else x
    py = 1 - y if k & 2 else y
    pc = 1 - c if k & 1 else c
    return (px, py, pc), 4 * px + 2 * py + pc


PEER_ORDER = (1, 2, 4, 3, 5, 6, 7)


class _Piece(NamedTuple):
    src: int
    send: Callable
    slot: Callable
    land_shape: tuple
    own_src: tuple
    own_slot: tuple


HBM_SPEC = pl.BlockSpec(memory_space=pltpu.HBM)
SEM_SPEC = pl.BlockSpec(memory_space=pltpu.SEMAPHORE)
ANY_SPEC = pl.BlockSpec(memory_space=pl.ANY)


def _in_hbm(arrays):
    return [pltpu.with_memory_space_constraint(a, pltpu.HBM) for a in arrays]


def _hbm_like(arrays):
    return [pltpu.HBM(a.shape, a.dtype) for a in arrays]


def _rows_of(rows):
    return lambda ref, dev: ref.at[pl.ds(pl.multiple_of(dev * rows, 16), rows), :]


def _exchange_own(name, me, srcs, pieces):
    n = len(pieces)

    def body(me_ref, *refs):
        for i in range(n):
            refs[n + i][...] = refs[i][...]

    def spec(block_and_index):
        block, index = block_and_index
        return pl.BlockSpec(block, lambda i, me_ref: index(me_ref[0]))

    return pl.pallas_call(
        body,
        grid_spec=pltpu.PrefetchScalarGridSpec(
            num_scalar_prefetch=1, grid=(1,), in_specs=[spec(p.own_src) for p in pieces],
            out_specs=[spec(p.own_slot) for p in pieces]),
        out_shape=[jax.ShapeDtypeStruct(p.land_shape, BF) for p in pieces],
        compiler_params=_cparams(("arbitrary",)), name=name,
    )(me, *[srcs[p.src] for p in pieces])


def _exchange_start(name, srcs, lands, pieces, groups):
    n_src, n, n_g = len(srcs), len(pieces), len(groups)

    def body(*refs):
        src_refs, land_refs = refs[:n_src], refs[n_src:n_src + n]
        sems, token = refs[n_src + n:n_src + n + 2 * n_g], refs[-1]
        x, y, c, me = _my_position()
        for g, idxs in enumerate(groups):
            for k in PEER_ORDER:
                peer, pid = _peer(x, y, c, k)
                for j, i in enumerate(idxs):
                    p = pieces[i]
                    pltpu.make_async_remote_copy(
                        src_ref=p.send(src_refs[p.src], pid), dst_ref=p.slot(land_refs[i], me),
                        send_sem=sems[2 * g].at[(k - 1) * len(idxs) + j], recv_sem=sems[2 * g + 1].at[(k - 1) * len(idxs) + j],
                        device_id=peer, device_id_type=MESH).start()
        token[...] = jnp.zeros(token.shape, F32)

    sem_shapes = [pltpu.SemaphoreType.DMA(((N_DEV - 1) * len(idxs),)) for idxs in groups for _ in range(2)]
    res = pl.pallas_call(
        body, in_specs=[HBM_SPEC] * (n_src + n),
        out_specs=[SEM_SPEC] * (2 * n_g) + [HBM_SPEC] * (n_src + n) + [pl.BlockSpec(memory_space=pltpu.VMEM)],
        out_shape=sem_shapes + _hbm_like(srcs) + _hbm_like(lands) + [jax.ShapeDtypeStruct((8, LANES), F32)],
        input_output_aliases={i: 2 * n_g + i for i in range(n_src + n)},
        compiler_params=pltpu.CompilerParams(has_side_effects=pltpu.SideEffectType.DATAFLOW_SIDE_EFFECTING),
        name=name,
    )(*_in_hbm(srcs), *_in_hbm(lands))
    sems = [(res[2 * g], res[2 * g + 1]) for g in range(n_g)]
    return sems, list(res[2 * n_g:2 * n_g + n_src]), list(res[2 * n_g + n_src:2 * n_g + n_src + n]), res[-1]


def _exchange_wait(name, srcs, lands, pieces, waits, after):
    n_src, n, n_g = len(srcs), len(lands), len(waits)

    def body(*refs):
        src_refs, land_refs = refs[:n_src], refs[n_src:n_src + n]
        sems = refs[n_src + n:n_src + n + 2 * n_g]
        x, y, c, me = _my_position()
        at = 0
        for g, (idxs, _, _) in enumerate(waits):
            for k in PEER_ORDER:
                peer, pid = _peer(x, y, c, k)
                for j, i in enumerate(idxs):
                    p = pieces[i]
                    cp = pltpu.make_async_remote_copy(
                        src_ref=p.send(src_refs[p.src], pid), dst_ref=p.slot(land_refs[at + j], pid),
                        send_sem=sems[2 * g].at[(k - 1) * len(idxs) + j], recv_sem=sems[2 * g + 1].at[(k - 1) * len(idxs) + j],
                        device_id=peer, device_id_type=MESH)
                    cp.wait_send()
                    cp.wait_recv()
            at += len(idxs)

    sem_args = [s for _, send, recv in waits for s in (send, recv)]
    res = pl.pallas_call(
        body, in_specs=[HBM_SPEC] * (n_src + n) + [SEM_SPEC] * (2 * n_g) + [ANY_SPEC],
        out_specs=[HBM_SPEC] * (n_src + n), out_shape=_hbm_like(srcs) + _hbm_like(lands),
        input_output_aliases={i: i for i in range(n_src + n)},
        compiler_params=pltpu.CompilerParams(has_side_effects=pltpu.SideEffectType.DATAFLOW_SIDE_EFFECTING),
        name=name,
    )(*srcs, *lands, *sem_args, after)
    return list(res[:n_src]), list(res[n_src:])


def _weight_pieces(p_in, p_out, p_up, p_down):
    depth, cin, d = p_in.shape
    rout, hs = p_out.shape[1], p_up.shape[2]
    pieces = []
    for l in range(depth):
        whole = functools.partial(lambda ref, dev, l: ref.at[l], l=l)
        layer = functools.partial(lambda dev, l: (l, 0, 0), l=l)

        def rows(src, n_rows, whole=whole, layer=layer):
            return _Piece(src, whole, lambda ref, dev: _rows_of(n_rows)(ref.at[0], dev), (1, N_DEV * n_rows, d),
                          ((None, n_rows, d), layer), ((None, n_rows, d), lambda dev: (0, dev, 0)))

        pieces += [
            rows(0, cin), rows(1, rout),
            _Piece(2, whole, lambda ref, dev: ref.at[0, dev], (1, N_DEV, d, hs),
                   ((None, d, hs), layer), ((None, None, d, hs), lambda dev: (0, dev, 0, 0))),
            rows(3, hs),
        ]
    return pieces


def _grad_pieces(g_pair, kinds):
    pieces = []
    for i, (g, kind) in enumerate(zip(g_pair, kinds)):
        lead = lambda dev: (dev, 0, 0)
        if kind == "up":
            blk = ((None,) + g.shape[1:], lead)
            pieces.append(_Piece(i, lambda ref, dev: ref.at[dev], lambda ref, dev: ref.at[dev], g.shape, blk, blk))
        else:
            rows, cols = g.shape[0] // N_DEV, g.shape[1]
            pieces.append(_Piece(i, _rows_of(rows), lambda ref, dev: ref.at[dev], (N_DEV, rows, cols),
                                 ((rows, cols), lambda dev: (dev, 0)), ((None, rows, cols), lead)))
    return pieces


def _all_reduce_small(vec):
    rows = vec.shape[0]

    def body(v_ref, o_ref, buf_ref, send_sems, recv_sems):
        x, y, c, me = _my_position()
        buf_ref[me] = v_ref[...]
        sends = []
        for k in PEER_ORDER:
            peer, _ = _peer(x, y, c, k)
            cp = pltpu.make_async_remote_copy(src_ref=v_ref, dst_ref=buf_ref.at[me], send_sem=send_sems.at[k - 1],
                                              recv_sem=recv_sems.at[k - 1], device_id=peer, device_id_type=MESH)
            cp.start()
            sends.append(cp)
        for k in PEER_ORDER:
            peer, pid = _peer(x, y, c, k)
            pltpu.make_async_remote_copy(src_ref=v_ref, dst_ref=buf_ref.at[pid], send_sem=send_sems.at[k - 1],
                                         recv_sem=recv_sems.at[k - 1], device_id=peer, device_id_type=MESH).wait_recv()
        for cp in sends:
            cp.wait_send()
        total = buf_ref[0]
        for dev in range(1, N_DEV):
            total = total + buf_ref[dev]
        o_ref[...] = total

    vm = pl.BlockSpec(memory_space=pltpu.VMEM)
    return pl.pallas_call(
        body, in_specs=[vm], out_specs=vm, out_shape=jax.ShapeDtypeStruct(vec.shape, F32),
        scratch_shapes=[pltpu.VMEM((N_DEV, rows, LANES), F32), pltpu.SemaphoreType.DMA((N_DEV - 1,)),
                        pltpu.SemaphoreType.DMA((N_DEV - 1,))],
        name="all_reduce_small",
    )(vec)


def _resident(block_shape, index_map):
    return pl.BlockSpec(block_shape, index_map, pipeline_mode=pl.Buffered(1))


def _fwd_inproj(layer, x, gain, g_in, cos, sin, tm=512):
    s, d = x.shape

    def body(x_ref, gain_ref, w_ref, cos_ref, sin_ref, proj_ref, h_ref):
        h = _rms_fwd(x_ref[...], gain_ref[...]).astype(BF)
        h_ref[...] = h
        cs = _tile_lanes(cos_ref[...], SEG // LANES)
        sn = _tile_lanes(sin_ref[...], SEG // LANES)
        for seg in range(N_SEG):
            acc = _dot(h, w_ref[seg * SEG:(seg + 1) * SEG, :], NT)
            if seg < 2:
                acc = acc * cs + _rope_partner(acc) * sn
            if seg == 0:
                acc = acc * ATTN_SCALE
            proj_ref[:, seg * SEG:(seg + 1) * SEG] = acc

    return pl.pallas_call(
        body, grid=(s // tm,),
        in_specs=[pl.BlockSpec((tm, d), lambda i: (i, 0)), pl.BlockSpec((None, 1, d), lambda i: (layer, 0, 0)),
                  _resident((None, PROJ_W, d), lambda i: (0, 0, 0)),
                  pl.BlockSpec((tm, LANES), lambda i: (i, 0)), pl.BlockSpec((tm, LANES), lambda i: (i, 0))],
        out_specs=[pl.BlockSpec((tm, PROJ_W), lambda i: (i, 0)), pl.BlockSpec((tm, d), lambda i: (i, 0))],
        out_shape=[jax.ShapeDtypeStruct((s, PROJ_W), F32), jax.ShapeDtypeStruct((s, d), BF)],
        compiler_params=_cparams(("parallel",)), name=f"fwd_inproj_l{layer}",
    )(x, gain, g_in, cos, sin)


ATTN_UNIT = SPAN * max(DILATIONS)
ATTN_GROUP = 4


def _attn_masks(first_block_has_prev):
    row = lax.broadcasted_iota(jnp.int32, (SPAN, 2 * SPAN), 0)
    col = lax.broadcasted_iota(jnp.int32, (SPAN, 2 * SPAN), 1)
    band = (col >= row) & (col <= row + SPAN)
    lane = lax.broadcasted_iota(jnp.int32, (SPAN, LANES), 1)
    return band & ((col >= SPAN) | first_block_has_prev), band, lane < 64


def _attn_specs(n_in_extra, unit_of=lambda n: n):
    pairs = ATTN_W // LANES
    q_spec = pl.BlockSpec((ATTN_UNIT, LANES), lambda p, n: (unit_of(n), p))

    def prev(seg):
        return pl.BlockSpec((ATTN_UNIT, LANES), lambda p, n: (jnp.maximum(unit_of(n) - 1, 0), seg * pairs + p))

    def cur(seg):
        return pl.BlockSpec((ATTN_UNIT, LANES), lambda p, n: (unit_of(n), seg * pairs + p))

    return [q_spec, prev(1), cur(1), prev(2), cur(2)] + [q_spec] * n_in_extra


def _attn_groups(dil):
    blocks = ATTN_UNIT // (SPAN * dil)
    pairs = [(r, b) for r in range(dil) for b in range(blocks)]
    return [pairs[i:i + ATTN_GROUP] for i in range(0, len(pairs), ATTN_GROUP)]


def _block_rows(dil, r, b, n=1):
    start = r + dil * SPAN * b
    return pl.ds(start, n * SPAN, stride=dil) if dil > 1 else pl.ds(start, n * SPAN)


def _block_keys(prev_ref, cur_ref, dil, r, b):
    if b > 0:
        return cur_ref[_block_rows(dil, r, b - 1, 2), :]
    last = ATTN_UNIT // (SPAN * dil) - 1
    return jnp.concatenate([prev_ref[_block_rows(dil, r, last), :], cur_ref[_block_rows(dil, r, 0), :]], axis=0)


def _attn_fwd(layer, proj):
    s = proj.shape[0]
    n_pat = len(DILATIONS)
    merge_rows = 256

    def body(q_ref, kp_ref, kc_ref, vp_ref, vc_ref, o_ref, lse_ref, o_scr, lse_scr):
        m_first, m_rest, is_a = _attn_masks(pl.program_id(1) > 0)
        sels = (is_a, jnp.logical_not(is_a))
        is_a_keys = lax.broadcasted_iota(jnp.int32, (2 * SPAN, LANES), 1) < 64
        for pi, dil in enumerate(DILATIONS):
            for group in _attn_groups(dil):
                items = [(r, b, h) for r, b in group for h in range(2)]
                q = {rb: q_ref[_block_rows(dil, *rb), :] for rb in group}
                k = {rb: _block_keys(kp_ref, kc_ref, dil, *rb).astype(BF) for rb in group}
                v = {rb: _block_keys(vp_ref, vc_ref, dil, *rb).astype(BF) for rb in group}
                v_sum = {rb: (jnp.where(is_a_keys, v[rb], 1.0), jnp.where(is_a_keys, 1.0, v[rb])) for rb in group}
                sc = [jnp.where(m_first if b == 0 else m_rest,
                                _dot(jnp.where(sels[h], q[r, b], 0.0).astype(BF), k[r, b], NT), MASK_VALUE)
                      for r, b, h in items]
                mx = [jnp.max(jnp.maximum(t[:, :SPAN], t[:, SPAN:]), axis=-1, keepdims=True) for t in sc]
                p = [jnp.exp(t - m).astype(BF) for t, m in zip(sc, mx)]
                both = [_dot(t, v_sum[r, b][h], NN) for t, (r, b, h) in zip(p, items)]
                for j, (r, b) in enumerate(group):
                    t_a, t_b = both[2 * j], both[2 * j + 1]
                    den = pltpu.roll(jnp.where(is_a, t_b, t_a), 64, 1)
                    o_scr[pi, _block_rows(dil, r, b), :] = jnp.where(is_a, t_a, t_b) / den
                    lse_scr[pi, _block_rows(dil, r, b), :] = jnp.where(is_a, mx[2 * j], mx[2 * j + 1]) + jnp.log(den)
        for c in range(ATTN_UNIT // merge_rows):
            rows = slice(c * merge_rows, (c + 1) * merge_rows)
            ls = [lse_scr[pi, rows, :] for pi in range(n_pat)]
            mx = functools.reduce(jnp.maximum, ls)
            ws = [jnp.exp(l - mx) for l in ls]
            den = functools.reduce(jnp.add, ws)
            o_ref[rows, :] = functools.reduce(jnp.add, [w * o_scr[pi, rows, :] for pi, w in enumerate(ws)]) / den
            lse_ref[rows, :] = mx + jnp.log(den)

    out_spec = pl.BlockSpec((ATTN_UNIT, LANES), lambda p, n: (n, p))
    return pl.pallas_call(
        body, grid=(ATTN_W // LANES, s // ATTN_UNIT), in_specs=_attn_specs(0), out_specs=[out_spec, out_spec],
        out_shape=[jax.ShapeDtypeStruct((s, ATTN_W), F32)] * 2,
        scratch_shapes=[pltpu.VMEM((n_pat, ATTN_UNIT, LANES), F32)] * 2,
        compiler_params=_cparams(("parallel", "arbitrary")), name=f"attn_fwd_l{layer}",
    )(proj, proj, proj, proj, proj)


def _attn_norm(layer, o, gain, mixed, tm=512):
    s = o.shape[0]

    def body(o_ref, gain_ref, mixed_ref, n_ref):
        n_ref[...] = _rms_fwd(o_ref[...], gain_ref[...]).astype(BF)

    blk = pl.BlockSpec((tm, ATTN_W), lambda i: (i, 0))
    return pl.pallas_call(
        body, grid=(s // tm,),
        in_specs=[blk, pl.BlockSpec((None, 1, ATTN_W), lambda i: (layer, 0, 0)), pl.BlockSpec(memory_space=pl.ANY)],
        out_specs=blk, out_shape=jax.ShapeDtypeStruct(mixed.shape, BF), input_output_aliases={2: 0},
        compiler_params=_cparams(("parallel",)), name=f"attn_norm_l{layer}",
    )(o, gain, mixed)


def _chunk_cumsum(x, reverse=False):
    n = x.shape[0]
    pos = lax.broadcasted_iota(jnp.int32, x.shape, 0) % HGRN_CHUNK
    for sh in (1, 2, 4, 8):
        if reverse:
            x = x + jnp.where(pos < HGRN_CHUNK - sh, pltpu.roll(x, n - sh, 0), 0.0)
        else:
            x = x + jnp.where(pos >= sh, pltpu.roll(x, sh, 0), 0.0)
    return x


def _chunk_row(x, row):
    r, n = x.shape
    x3 = x.reshape(r // HGRN_CHUNK, HGRN_CHUNK, n)
    return jnp.broadcast_to(x3[:, row:row + 1, :], x3.shape).reshape(r, n)


def _hgrn_pre(qh, z, lb):
    sig = _sigmoid(z)
    f = lb + (1.0 - lb) * sig
    k = (1.0 - lb) * _sigmoid(-z)
    sq = _sigmoid(qh)
    q = qh * sq * HGRN_SCALE
    g = _chunk_cumsum(jnp.log(f))
    g_mid = _chunk_row(g, HGRN_CHUNK // 2 - 1)
    g_last = _chunk_row(g, HGRN_CHUNK - 1)
    e_q, e_k = jnp.exp(g - g_mid), jnp.exp(g_mid - g)
    e_in, e_out = jnp.exp(g), jnp.exp(g_last - g)
    return dict(sig=sig, f=f, k=k, sq=sq, q=q, g_last=g_last, e_q=e_q, e_k=e_k, e_in=e_in, e_out=e_out,
                qt=q * e_q, kt=k * e_k, qg=q * e_in, kout=k * e_out)


def _hgrn_mask():
    row = lax.broadcasted_iota(jnp.int32, (LANES, LANES), 0)
    col = lax.broadcasted_iota(jnp.int32, (LANES, LANES), 1)
    return (row // HGRN_CHUNK == col // HGRN_CHUNK) & (col <= row)


def _hgrn_in_specs(layer, rev, nblk):
    def blk(b):
        return nblk - 1 - b if rev else b
    first = 3 * ATTN_W // HGRN_W
    specs = [pl.BlockSpec((HGRN_ROWS, HGRN_W), functools.partial(lambda b, seg: (blk(b), first + seg), seg=seg))
             for seg in range(4)]
    specs.append(pl.BlockSpec((None, 1, HGRN_W), lambda b: (layer, 0, 0)))
    specs.append(pl.BlockSpec((None, 1, HGRN_DIM), lambda b: (layer, 0, 0)))
    return specs, blk


def _head(x, h):
    return x[:, h * HGRN_DIM:(h + 1) * HGRN_DIM]


def _chunk(x, c):
    return x[c * HGRN_CHUNK:(c + 1) * HGRN_CHUNK]


def _sub(x, sb):
    return x[sb * LANES:(sb + 1) * LANES]


HGRN_ROWS = 256
HEADS = range(HGRN_HEADS)
SUBS = range(HGRN_ROWS // LANES)
CHUNKS = range(HGRN_ROWS // HGRN_CHUNK)


def _hgrn_fwd(layer, proj, lb, gain):
    s = proj.shape[0]
    nblk = s // HGRN_ROWS
    cpb = len(CHUNKS)

    def body(q_ref, f_ref, i_ref, g_ref, lb_ref, gain_ref, o_ref, rec_ref, st_ref, state):
        @pl.when(pl.program_id(0) == 0)
        def _():
            state[...] = jnp.zeros(state.shape, F32)

        pre = _hgrn_pre(q_ref[...], f_ref[...], lb_ref[...])
        v = i_ref[...].astype(BF)
        qt, kt, qg, kout = (pre[n].astype(BF) for n in ("qt", "kt", "qg", "kout"))
        dec = jnp.exp(pre["g_last"])
        mask = _hgrn_mask()
        a = [[jnp.where(mask, _dot(_sub(_head(qt, h), sb), _sub(_head(kt, h), sb), NT), 0.0).astype(BF) for sb in SUBS]
             for h in HEADS]
        o_intra = [[_dot(a[h][sb], _sub(_head(v, h), sb), NN) for sb in SUBS] for h in HEADS]
        update = [[_dot(_chunk(_head(v, h), c), _chunk(_head(kout, h), c), TN) for c in CHUNKS] for h in HEADS]
        for h in HEADS:
            st = state[h]
            for c in CHUNKS:
                st_ref[h, c * LANES:(c + 1) * LANES, :] = st.astype(BF)
                st = st * _head(dec, h)[c * HGRN_CHUNK:c * HGRN_CHUNK + 1, :] + update[h][c]
            state[h] = st
        inter = [[_dot(_chunk(_head(qg, h), c), st_ref[h, c * LANES:(c + 1) * LANES, :].astype(BF), NT) for c in CHUNKS]
                 for h in HEADS]
        o = [jnp.concatenate(o_intra[h], axis=0) + jnp.concatenate(inter[h], axis=0) for h in HEADS]
        o_ref[...] = jnp.concatenate(o, axis=1)
        gate = g_ref[...]
        normed = jnp.concatenate([_rms_fwd(o[h], gain_ref[...]) for h in HEADS], axis=1)
        rec_ref[...] = (normed * (gate * _sigmoid(gate))).astype(BF)

    specs, _ = _hgrn_in_specs(layer, False, nblk)
    return pl.pallas_call(
        body, grid=(nblk,), in_specs=specs,
        out_specs=[pl.BlockSpec((HGRN_ROWS, HGRN_W), lambda b: (b, 0)), pl.BlockSpec((HGRN_ROWS, HGRN_W), lambda b: (b, 1)),
                   pl.BlockSpec((HGRN_HEADS, cpb * LANES, LANES), lambda b: (0, b, 0))],
        out_shape=[jax.ShapeDtypeStruct((s, HGRN_W), F32), jax.ShapeDtypeStruct((s, MIX_W), BF),
                   jax.ShapeDtypeStruct((HGRN_HEADS, nblk * cpb * LANES, LANES), BF)],
        scratch_shapes=[pltpu.VMEM((HGRN_HEADS, LANES, LANES), F32)],
        compiler_params=_cparams(("arbitrary",)), name=f"hgrn_fwd_l{layer}",
    )(proj, proj, proj, proj, lb, gain)


def _relu2(u):
    return jnp.square(jnp.maximum(u, 0)).astype(BF)


def _mlp_fwd(layer, x, mixed, gain, g_out, g_up, g_down, tm=256):
    s, d = x.shape
    mw = mixed.shape[1]
    nblk, hs = g_up.shape[1], g_up.shape[3]

    def body(x_ref, m_ref, gain_ref, out_w_ref, up_ref, down_ref, o_ref, mid_ref, u_ref, h_ref, a_buf):
        xv = x_ref[...] + _dot(m_ref[...], out_w_ref[...], NN)
        mid_ref[...] = xv
        h = _rms_fwd(xv, gain_ref[...]).astype(BF)
        h_ref[...] = h
        for j in range(nblk):
            u = _dot(h, up_ref[j], NN)
            u_ref[:, j * hs:(j + 1) * hs] = u.astype(BF)
            a_buf[:, j * hs:(j + 1) * hs] = _relu2(u)
        acc = xv
        for j in range(nblk):
            acc = acc + _dot(a_buf[:, j * hs:(j + 1) * hs], down_ref[j * hs:(j + 1) * hs, :], NN)
        o_ref[...] = acc

    row = pl.BlockSpec((tm, d), lambda i: (i, 0))
    return pl.pallas_call(
        body, grid=(s // tm,),
        in_specs=[row, pl.BlockSpec((tm, mw), lambda i: (i, 0)), pl.BlockSpec((None, 1, d), lambda i: (layer, 0, 0)),
                  _resident((None, mw, d), lambda i: (0, 0, 0)),
                  _resident((None, nblk, d, hs), lambda i: (0, 0, 0, 0)),
                  _resident((None, nblk * hs, d), lambda i: (0, 0, 0))],
        out_specs=[row, row, pl.BlockSpec((tm, nblk * hs), lambda i: (i, 0)), row],
        out_shape=[jax.ShapeDtypeStruct((s, d), F32), jax.ShapeDtypeStruct((s, d), F32),
                   jax.ShapeDtypeStruct((s, nblk * hs), BF), jax.ShapeDtypeStruct((s, d), BF)],
        scratch_shapes=[pltpu.VMEM((tm, nblk * hs), BF)],
        compiler_params=_cparams(("parallel",)), name=f"mlp_fwd_l{layer}",
    )(x, mixed, gain, g_out, g_up, g_down)


def _loss_head(x, gain, target, tm=512):
    s, d = x.shape

    def body(x_ref, gain_ref, t_ref, dx_ref, dxb_ref, dgain_ref, loss_ref):
        i = pl.program_id(0)
        xv, gv = x_ref[...], gain_ref[...]
        err = _rms_fwd(xv, gv) - t_ref[...]
        dx, dgain = _rms_bwd(err * (1.0 / d), xv, gv)
        dx_ref[...] = dx
        dxb_ref[...] = dx.astype(BF)
        part = _part8(dgain)
        lpart = _part8(0.5 * jnp.mean(err * err, axis=-1, keepdims=True) * jnp.ones((1, LANES), F32))

        @pl.when(i == 0)
        def _():
            dgain_ref[...] = part
            loss_ref[...] = lpart

        @pl.when(i > 0)
        def _():
            dgain_ref[...] += part
            loss_ref[...] += lpart

    row = pl.BlockSpec((tm, d), lambda i: (i, 0))
    return pl.pallas_call(
        body, grid=(s // tm,),
        in_specs=[row, pl.BlockSpec((1, d), lambda i: (0, 0)), row],
        out_specs=[row, row, pl.BlockSpec((8, d), lambda i: (0, 0)), pl.BlockSpec((8, LANES), lambda i: (0, 0))],
        out_shape=[jax.ShapeDtypeStruct((s, d), F32), jax.ShapeDtypeStruct((s, d), BF), jax.ShapeDtypeStruct((8, d), F32),
                   jax.ShapeDtypeStruct((8, LANES), F32)],
        compiler_params=_cparams(("arbitrary",)), name="loss_head",
    )(x, gain, target)


def _accumulate_rows(i, ref, part):
    @pl.when(i == 0)
    def _():
        ref[...] = part

    @pl.when(i > 0)
    def _():
        ref[...] += part


def _mlp_bwd(layer, dx, dxb, x, gain, u, g_out, g_up, g_down, tm=256):
    s, d = x.shape
    mw = g_out.shape[1]
    nblk, hs = g_up.shape[1], g_up.shape[3]

    def body(dx_ref, dxb_ref, x_ref, gain_ref, u_ref, out_w_ref, up_ref, down_ref, o_ref, ob_ref, du_ref, dm_ref,
             dgain_ref):
        dxb_v = dxb_ref[...]
        for j in range(nblk):
            cols = slice(j * hs, (j + 1) * hs)
            da = _dot(dxb_v, down_ref[cols, :], NT)
            du_ref[:, cols] = (da * (2.0 * jnp.maximum(u_ref[:, cols].astype(F32), 0.0))).astype(BF)
        acc = jnp.zeros((tm, d), F32)
        for j in range(nblk):
            acc = acc + _dot(du_ref[:, j * hs:(j + 1) * hs], up_ref[j], NT)
        dxn, dgain = _rms_bwd(acc, x_ref[...], gain_ref[...])
        out = dx_ref[...] + dxn
        out_b = out.astype(BF)
        o_ref[...] = out
        ob_ref[...] = out_b
        dm_ref[...] = _dot(out_b, out_w_ref[...], NT)
        _accumulate_rows(pl.program_id(0), dgain_ref, _part8(dgain))

    row = pl.BlockSpec((tm, d), lambda i: (i, 0))
    wide = pl.BlockSpec((tm, nblk * hs), lambda i: (i, 0))
    return pl.pallas_call(
        body, grid=(s // tm,),
        in_specs=[row, row, row, pl.BlockSpec((None, 1, d), lambda i: (layer, 0, 0)), wide,
                  _resident((None, mw, d), lambda i: (0, 0, 0)),
                  _resident((None, nblk, d, hs), lambda i: (0, 0, 0, 0)),
                  _resident((None, nblk * hs, d), lambda i: (0, 0, 0))],
        out_specs=[row, row, wide, pl.BlockSpec((tm, mw), lambda i: (i, 0)), pl.BlockSpec((8, d), lambda i: (0, 0))],
        out_shape=[jax.ShapeDtypeStruct((s, d), F32), jax.ShapeDtypeStruct((s, d), BF),
                   jax.ShapeDtypeStruct((s, nblk * hs), BF), jax.ShapeDtypeStruct((s, mw), F32),
                   jax.ShapeDtypeStruct((8, d), F32)],
        compiler_params=_cparams(("arbitrary",)), name=f"mlp_bwd_l{layer}",
    )(dx, dxb, x, gain, u, g_out, g_up, g_down)


def _attn_norm_bwd(layer, dmixed, o, gain, tm=512):
    s = o.shape[0]

    def body(dm_ref, o_ref, gain_ref, do_ref, delta_ref, dgain_ref):
        i = pl.program_id(0)
        ov = o_ref[...]
        do, dgain = _rms_bwd(dm_ref[...], ov, gain_ref[...])
        do_ref[...] = do
        row = lax.broadcasted_iota(jnp.int32, (ATTN_W, ATTN_W), 0)
        col = lax.broadcasted_iota(jnp.int32, (ATTN_W, ATTN_W), 1)
        same_head = jnp.where(row // 64 == col // 64, 1.0, 0.0)
        delta_ref[...] = jnp.dot(do * ov, same_head, precision=lax.Precision.HIGHEST, preferred_element_type=F32)
        part = _part8(dgain)

        @pl.when(i == 0)
        def _():
            dgain_ref[...] = part

        @pl.when(i > 0)
        def _():
            dgain_ref[...] += part

    blk = pl.BlockSpec((tm, ATTN_W), lambda i: (i, 0))
    return pl.pallas_call(
        body, grid=(s // tm,), in_specs=[blk, blk, pl.BlockSpec((None, 1, ATTN_W), lambda i: (layer, 0, 0))],
        out_specs=[blk, blk, pl.BlockSpec((8, ATTN_W), lambda i: (0, 0))],
        out_shape=[jax.ShapeDtypeStruct((s, ATTN_W), F32), jax.ShapeDtypeStruct((s, ATTN_W), F32),
                   jax.ShapeDtypeStruct((8, ATTN_W), F32)],
        compiler_params=_cparams(("arbitrary",)), name=f"attn_norm_bwd_l{layer}",
    )(dmixed, o, gain)


def _attn_bwd(layer, proj, do, lse, delta, cos, sin):
    s = proj.shape[0]
    n_units = s // ATTN_UNIT
    out_rows = 256

    def unit_of(n):
        return n_units - 1 - n

    def body(q_ref, kp_ref, kc_ref, vp_ref, vc_ref, do_ref, lse_ref, delta_ref, cos_ref, sin_ref, out_ref,
             dq_ref, dk_ref, dkp_ref, dv_ref, dvp_ref, carry_k, carry_v):
        step = pl.program_id(1)
        m_first, m_rest, is_a = _attn_masks(unit_of(step) > 0)
        is_a_keys = lax.broadcasted_iota(jnp.int32, (2 * SPAN, LANES), 1) < 64
        sels = (is_a, jnp.logical_not(is_a))
        key_sels = (is_a_keys, jnp.logical_not(is_a_keys))
        for ref in (dq_ref, dk_ref, dkp_ref, dv_ref, dvp_ref):
            ref[...] = jnp.zeros(ref.shape, F32)
        for dil in DILATIONS:
            last = ATTN_UNIT // (SPAN * dil) - 1
            for group in _attn_groups(dil):
                items = [(r, b, h) for r, b in group for h in range(2)]
                q = {rb: q_ref[_block_rows(dil, *rb), :] for rb in group}
                dov = {rb: do_ref[_block_rows(dil, *rb), :] for rb in group}
                lse_v = {rb: lse_ref[_block_rows(dil, *rb), :] for rb in group}
                delta_v = {rb: delta_ref[_block_rows(dil, *rb), :] for rb in group}
                k = {rb: _block_keys(kp_ref, kc_ref, dil, *rb) for rb in group}
                v = {rb: _block_keys(vp_ref, vc_ref, dil, *rb).astype(BF) for rb in group}
                qh = [jnp.where(sels[h], q[r, b], 0.0).astype(BF) for r, b, h in items]
                doh = [jnp.where(sels[h], dov[r, b], 0.0).astype(BF) for r, b, h in items]
                kh = [jnp.where(key_sels[h], k[r, b], 0.0).astype(BF) for r, b, h in items]
                sc = [jnp.where(m_first if b == 0 else m_rest, _dot(qh[i], kh[i], NT), MASK_VALUE)
                      for i, (r, b, h) in enumerate(items)]
                dp = [_dot(doh[i], v[r, b], NT) for i, (r, b, h) in enumerate(items)]
                p = [jnp.exp(sc[i] - lse_v[r, b][:, 64 * h:64 * h + 1]) for i, (r, b, h) in enumerate(items)]
                ds = [(p[i] * (dp[i] - delta_v[r, b][:, 64 * h:64 * h + 1])).astype(BF) for i, (r, b, h) in enumerate(items)]
                dv = [_dot(p[i].astype(BF), doh[i], TN) for i in range(len(items))]
                dq = [_dot(ds[i], kh[i], NN) for i in range(len(items))]
                dk = [_dot(ds[i], qh[i], TN) for i in range(len(items))]
                for j, (r, b) in enumerate(group):
                    own = _block_rows(dil, r, b)
                    dq_ref[own, :] += dq[2 * j] + dq[2 * j + 1]
                    dk2, dv2 = dk[2 * j] + dk[2 * j + 1], dv[2 * j] + dv[2 * j + 1]
                    dk_ref[own, :] += dk2[SPAN:]
                    dv_ref[own, :] += dv2[SPAN:]
                    if b > 0:
                        before = _block_rows(dil, r, b - 1)
                        dk_ref[before, :] += dk2[:SPAN]
                        dv_ref[before, :] += dv2[:SPAN]
                    else:
                        before = _block_rows(dil, r, last)
                        dkp_ref[before, :] += dk2[:SPAN]
                        dvp_ref[before, :] += dv2[:SPAN]
        has_next = step > 0
        for c in range(ATTN_UNIT // out_rows):
            rows = slice(c * out_rows, (c + 1) * out_rows)
            cs, sn = cos_ref[rows, :], sin_ref[rows, :]
            dqv = dq_ref[rows, :]
            dkv = dk_ref[rows, :] + jnp.where(has_next, carry_k[rows, :], 0.0)
            dvv = dv_ref[rows, :] + jnp.where(has_next, carry_v[rows, :], 0.0)
            out_ref[0, rows, :] = ((dqv * cs - _rope_partner(dqv) * sn) * ATTN_SCALE).astype(BF)
            out_ref[1, rows, :] = (dkv * cs - _rope_partner(dkv) * sn).astype(BF)
            out_ref[2, rows, :] = dvv.astype(BF)
        carry_k[...] = dkp_ref[...]
        carry_v[...] = dvp_ref[...]

    tab = pl.BlockSpec((ATTN_UNIT, LANES), lambda p, n: (unit_of(n), 0))
    return pl.pallas_call(
        body, grid=(ATTN_W // LANES, n_units), in_specs=_attn_specs(3, unit_of) + [tab, tab],
        out_specs=pl.BlockSpec((3, ATTN_UNIT, LANES), lambda p, n: (0, unit_of(n), p)),
        out_shape=jax.ShapeDtypeStruct((3, s, ATTN_W), BF),
        scratch_shapes=[pltpu.VMEM((ATTN_UNIT, LANES), F32)] * 7,
        compiler_params=_cparams(("parallel", "arbitrary")), name=f"attn_bwd_l{layer}",
    )(proj, proj, proj, proj, proj, do, lse, delta, cos, sin)


def _hgrn_bwd(layer, proj, lb, gain, o, dmixed, states):
    s = proj.shape[0]
    nblk = s // HGRN_ROWS
    cpb = len(CHUNKS)

    def body(q_ref, f_ref, i_ref, g_ref, lb_ref, gain_ref, o_ref, drec_ref, st_ref, dseg_ref, dlb_ref, dgain_ref,
             dstate, dst_buf):
        step = pl.program_id(0)

        @pl.when(step == 0)
        def _():
            dstate[...] = jnp.zeros(dstate.shape, F32)

        lbv, gv = lb_ref[...], gain_ref[...]
        qh, z, gate_in = q_ref[...], f_ref[...], g_ref[...]
        pre = _hgrn_pre(qh, z, lbv)
        v = i_ref[...].astype(BF)
        sg = _sigmoid(gate_in)
        ov, drec = o_ref[...], drec_ref[...]
        dnormed = drec * (gate_in * sg)
        back = [_rms_bwd(_head(dnormed, h), _head(ov, h), gv) for h in HEADS]
        do_b = jnp.concatenate([b[0] for b in back], axis=1).astype(BF)
        dgain = back[0][1] + back[1][1] + back[2][1] + back[3][1]
        normed = jnp.concatenate([_rms_fwd(_head(ov, h), gv) for h in HEADS], axis=1)
        dgate_in = drec * normed * (sg * (1.0 + gate_in * (1.0 - sg)))
        mask = _hgrn_mask()
        qt, kt, qg, kout = (pre[n].astype(BF) for n in ("qt", "kt", "qg", "kout"))
        dec = jnp.exp(pre["g_last"])
        def intra(fn):
            return jnp.concatenate([jnp.concatenate([fn(h, sb) for sb in SUBS], axis=0) for h in HEADS], axis=1)

        def hs(x, h, sb):
            return _sub(_head(x, h), sb)

        a = [[jnp.where(mask, _dot(hs(qt, h, sb), hs(kt, h, sb), NT), 0.0).astype(BF) for sb in SUBS] for h in HEADS]
        da = [[jnp.where(mask, _dot(hs(do_b, h, sb), hs(v, h, sb), NT), 0.0).astype(BF) for sb in SUBS] for h in HEADS]
        dv_intra = intra(lambda h, sb: _dot(a[h][sb], hs(do_b, h, sb), TN))
        dqt = intra(lambda h, sb: _dot(da[h][sb], hs(kt, h, sb), NN))
        dkt = intra(lambda h, sb: _dot(da[h][sb], hs(qt, h, sb), TN))
        feed = [[_dot(_chunk(_head(do_b, h), c), _chunk(_head(qg, h), c), TN) for c in CHUNKS] for h in HEADS]
        for h in HEADS:
            dst = dstate[h]
            for c in reversed(CHUNKS):
                dst_buf[h, c * LANES:(c + 1) * LANES, :] = dst
                dst = dst * _head(dec, h)[c * HGRN_CHUNK:c * HGRN_CHUNK + 1, :] + feed[h][c]
            dstate[h] = dst

        def per_chunk(fn):
            cols = []
            for h in HEADS:
                rows = [jnp.broadcast_to(t, (HGRN_CHUNK, HGRN_DIM)) for t in (fn(h, c) for c in CHUNKS)]
                cols.append(jnp.concatenate(rows, axis=0))
            return jnp.concatenate(cols, axis=1)

        def st_prev(h, c):
            return st_ref[h, c * LANES:(c + 1) * LANES, :]

        def dst_at(h, c):
            return dst_buf[h, c * LANES:(c + 1) * LANES, :]

        dqg = per_chunk(lambda h, c: _dot(_chunk(_head(do_b, h), c), st_prev(h, c).astype(BF), NN))
        dkout = per_chunk(lambda h, c: _dot(_chunk(_head(v, h), c), dst_at(h, c).astype(BF), NN))
        dv_inter = per_chunk(lambda h, c: _dot(_chunk(_head(kout, h), c), dst_at(h, c).astype(BF), NT))
        dg_state = per_chunk(lambda h, c: jnp.sum(dst_at(h, c) * st_prev(h, c).astype(F32), axis=0, keepdims=True))
        dg_kout = per_chunk(lambda h, c: jnp.sum(_chunk(_head(dkout * pre["kout"], h), c), axis=0, keepdims=True))
        dv = dv_intra + dv_inter
        pos = lax.broadcasted_iota(jnp.int32, (HGRN_ROWS, HGRN_W), 0) % HGRN_CHUNK
        dq = dqt * pre["e_q"] + dqg * pre["e_in"]
        dk = dkt * pre["e_k"] + dkout * pre["e_out"]
        dg = (dqt * pre["qt"] - dkt * pre["kt"] + dqg * pre["qg"] - dkout * pre["kout"]
              + jnp.where(pos == HGRN_CHUNK - 1, dg_state * dec + dg_kout, 0.0))
        dlogf = _chunk_cumsum(dg, reverse=True)
        sig, sq = pre["sig"], pre["sq"]
        df = dlogf / pre["f"] - dk
        dseg_ref[0] = (dq * HGRN_SCALE * (sq * (1.0 + qh * (1.0 - sq)))).astype(BF)
        dseg_ref[1] = (df * (1.0 - lbv) * sig * (1.0 - sig)).astype(BF)
        dseg_ref[2] = dv.astype(BF)
        dseg_ref[3] = dgate_in.astype(BF)
        _accumulate_rows(step, dlb_ref, _part8(df * (1.0 - sig)))
        _accumulate_rows(step, dgain_ref, _part8(dgain))

    specs, blk = _hgrn_in_specs(layer, True, nblk)
    specs += [pl.BlockSpec((HGRN_ROWS, HGRN_W), lambda b: (blk(b), 0)),
              pl.BlockSpec((HGRN_ROWS, HGRN_W), lambda b: (blk(b), 1)),
              pl.BlockSpec((HGRN_HEADS, cpb * LANES, LANES), lambda b: (0, blk(b), 0))]
    return pl.pallas_call(
        body, grid=(nblk,), in_specs=specs,
        out_specs=[pl.BlockSpec((4, HGRN_ROWS, HGRN_W), lambda b: (0, blk(b), 0)),
                   pl.BlockSpec((8, HGRN_W), lambda b: (0, 0)), pl.BlockSpec((8, HGRN_DIM), lambda b: (0, 0))],
        out_shape=[jax.ShapeDtypeStruct((4, s, HGRN_W), BF), jax.ShapeDtypeStruct((8, HGRN_W), F32),
                   jax.ShapeDtypeStruct((8, HGRN_DIM), F32)],
        scratch_shapes=[pltpu.VMEM((HGRN_HEADS, LANES, LANES), F32), pltpu.VMEM((HGRN_HEADS, cpb * LANES, LANES), F32)],
        compiler_params=_cparams(("arbitrary",)), name=f"hgrn_bwd_l{layer}",
    )(proj, proj, proj, proj, lb, gain, o, dmixed, states)


def _bwd_inproj(layer, dqkv, dhg, g_in, x, gain, dres, tm=512):
    s, d = x.shape

    def body(dqkv_ref, dhg_ref, w_ref, x_ref, gain_ref, dres_ref, dx_ref, dxb_ref, dgain_ref):
        acc = jnp.zeros((tm, d), F32)
        for seg in range(N_SEG):
            a = dqkv_ref[seg] if seg < 3 else dhg_ref[seg - 3]
            acc = acc + _dot(a, w_ref[seg * SEG:(seg + 1) * SEG, :], NN)
        dx, dgain = _rms_bwd(acc, x_ref[...], gain_ref[...])
        out = dres_ref[...] + dx
        dx_ref[...] = out
        dxb_ref[...] = out.astype(BF)
        _accumulate_rows(pl.program_id(0), dgain_ref, _part8(dgain))

    row = pl.BlockSpec((tm, d), lambda i: (i, 0))
    return pl.pallas_call(
        body, grid=(s // tm,),
        in_specs=[pl.BlockSpec((3, tm, SEG), lambda i: (0, i, 0)), pl.BlockSpec((4, tm, SEG), lambda i: (0, i, 0)),
                  _resident((None, PROJ_W, d), lambda i: (0, 0, 0)), row,
                  pl.BlockSpec((None, 1, d), lambda i: (layer, 0, 0)), row],
        out_specs=[row, row, pl.BlockSpec((8, d), lambda i: (0, 0))],
        out_shape=[jax.ShapeDtypeStruct((s, d), F32), jax.ShapeDtypeStruct((s, d), BF), jax.ShapeDtypeStruct((8, d), F32)],
        compiler_params=_cparams(("arbitrary",)), name=f"bwd_inproj_l{layer}",
    )(dqkv, dhg, g_in, x, gain, dres)


def _adamw(w, g, m, v):
    m2 = ADAM_B1 * m + (1.0 - ADAM_B1) * g
    v2 = ADAM_B2 * v + (1.0 - ADAM_B2) * (g * g)
    m_hat = m2 / (1.0 - ADAM_B1 ** ADAM_STEP)
    v_hat = v2 / (1.0 - ADAM_B2 ** ADAM_STEP)
    delta = -ADAM_LR * (m_hat / (jnp.sqrt(v_hat) + ADAM_EPS) + ADAM_WD * w)
    return delta, m2, v2


def _adam_big(name, parts, w, m, v, row_tiles):
    depth = w.shape[0]
    r, c = parts[0].shape[1], parts[0].shape[2]
    tr = r // row_tiles
    p_spec = pl.BlockSpec((N_DEV, tr, c), lambda t: (0, t, 0))
    w_spec = pl.BlockSpec((depth, tr, c), lambda t: (0, t, 0))

    def body(*refs):
        p_refs = refs[:depth]
        w_ref, m_ref, v_ref, g_ref, d_ref, m2_ref, v2_ref = refs[depth:]
        for l in range(depth):
            g = p_refs[l][0].astype(F32)
            for dev in range(1, N_DEV):
                g = g + p_refs[l][dev].astype(F32)
            delta, m2, v2 = _adamw(w_ref[l], g, m_ref[l], v_ref[l])
            g_ref[l] = g
            d_ref[l] = delta
            m2_ref[l] = m2
            v2_ref[l] = v2

    return pl.pallas_call(
        body, grid=(row_tiles,), in_specs=[p_spec] * depth + [w_spec] * 3, out_specs=[w_spec] * 4,
        out_shape=[jax.ShapeDtypeStruct(w.shape, F32)] * 4,
        compiler_params=_cparams(("parallel",)), name=name,
    )(*parts, w, m, v)


def _adam_small(g, w, m, v):
    def body(g_ref, w_ref, m_ref, v_ref, d_ref, m2_ref, v2_ref):
        delta, m2, v2 = _adamw(w_ref[...], g_ref[...], m_ref[...], v_ref[...])
        d_ref[...] = delta
        m2_ref[...] = m2
        v2_ref[...] = v2

    vm = pl.BlockSpec(memory_space=pltpu.VMEM)
    return pl.pallas_call(body, in_specs=[vm] * 4, out_specs=[vm] * 3, out_shape=[jax.ShapeDtypeStruct(g.shape, F32)] * 3,
                          name="adam_small")(g, w, m, v)


def _lower_bounds(logits):
    def body(l_ref, lb_ref, jac_ref):
        l0, l1 = l_ref[0:1, :], l_ref[1:2, :]
        mx = jnp.maximum(l0, l1)
        e0, e1 = jnp.exp(l0 - mx), jnp.exp(l1 - mx)
        p0, p1 = e0 / (e0 + e1), e1 / (e0 + e1)
        lb_ref[0:1, :] = p0 - p0
        lb_ref[1:2, :] = (p0 + p1) - p0
        jac_ref[0:1, :] = -p0 * p1
        jac_ref[1:2, :] = p0 * p1

    vm = pl.BlockSpec(memory_space=pltpu.VMEM)
    return pl.pallas_call(body, in_specs=[vm], out_specs=[vm, vm], out_shape=[jax.ShapeDtypeStruct(logits.shape, F32)] * 2,
                          name="hgrn_lower_bounds")(logits)


def _rope_tables(s):
    half = 32
    inv_freq = ROPE_THETA ** (-jnp.arange(half, dtype=F32) / half)
    ang = jnp.arange(s, dtype=jnp.int32).astype(F32)[:, None] * inv_freq[None, :]
    cos, sin = jnp.cos(ang), jnp.sin(ang)
    return jnp.concatenate([cos] * 4, axis=1), jnp.concatenate([-sin, sin, -sin, sin], axis=1)


SMALL_NAMES = ("norm_mix", "attn_out_gain", "hgrn_lb_logits", "hgrn_out_gain", "norm_mlp", "norm_final")


def _pack_small(vals):
    flat = jnp.concatenate([v.reshape(-1) for v in vals])
    rows = -(-flat.shape[0] // (8 * LANES)) * 8
    return jnp.pad(flat, (0, rows * LANES - flat.shape[0])).reshape(rows, LANES)


def _unpack_small(packed, like):
    flat, out, off = packed.reshape(-1), [], 0
    for v in like:
        out.append(flat[off:off + v.size].reshape(v.shape))
        off += v.size
    return out


def kernel(x, norm_mix, w_in, attn_out_gain, hgrn_lb_logits, hgrn_out_gain, w_out, norm_mlp, w_up, w_down, norm_final, loss_target, m_norm_mix, m_w_in, m_attn_out_gain, m_hgrn_lb_logits, m_hgrn_out_gain, m_w_out, m_norm_mlp, m_w_up, m_w_down, m_norm_final, v_norm_mix, v_w_in, v_attn_out_gain, v_hgrn_lb_logits, v_hgrn_out_gain, v_w_out, v_norm_mlp, v_w_up, v_w_down, v_norm_final):
    depth = w_in.shape[0]
    assert depth == 2 and x.shape[0] == 1
    s, d = x.shape[1], x.shape[2]
    x0 = x[0]
    target = loss_target[0]
    cos, sin = _rope_tables(s)
    g_mix, g_attn, g_hg, g_mlp = (norm_mix[:, None, :], attn_out_gain[:, None, :], hgrn_out_gain[:, None, :],
                                  norm_mlp[:, None, :])
    lb, lb_jac = _lower_bounds(hgrn_lb_logits)
    lb3 = lb[:, None, :]

    def flip(a):
        return jnp.swapaxes(a, 1, 2)

    shards = list(_pack_weights(flip(w_in), w_out, w_up, w_down))
    w_pieces = _weight_pieces(*shards)
    w_groups = [[0], [1, 2, 3], [4], [5, 6, 7]]
    me = (4 * lax.axis_index("x") + 2 * lax.axis_index("y") + lax.axis_index("c")).astype(jnp.int32).reshape(1)
    lands = _exchange_own("all_gather_own", me, shards, w_pieces)
    w_sems, shards, lands, token = _exchange_start("all_gather_start", shards, lands, w_pieces, w_groups)

    def weights_ready(group, after):
        nonlocal shards
        idxs = w_groups[group]
        shards, got = _exchange_wait(f"all_gather_wait{group}", shards, [lands[i] for i in idxs], w_pieces,
                                     [(idxs, *w_sems[group])], after)
        return got

    def tied(small_arr, tok):
        return small_arr + tok[0, 0]

    saved = []
    xl = x0
    full = [None] * depth
    for l in range(depth):
        (full_in,) = weights_ready(2 * l, token if l == 0 else xl)
        proj, h = _fwd_inproj(l, xl, g_mix, full_in, cos, sin)
        o_attn, lse = _attn_fwd(l, proj)
        o_hg, mixed, states = _hgrn_fwd(l, proj, lb3, g_hg)
        mixed = _attn_norm(l, o_attn, g_attn, mixed)
        full_out, full_up, full_down = weights_ready(2 * l + 1, mixed)
        x_next, x_mid, u, h2 = _mlp_fwd(l, xl, mixed, g_mlp, full_out, full_up, full_down)
        saved.append((xl, proj, h, o_attn, lse, o_hg, states, mixed, x_mid, u, h2))
        full[l] = (full_in, full_out, full_up, full_down)
        xl = x_next
    dx, dxb, dnorm_final8, loss8 = _loss_head(xl, norm_final[None, :], target)
    loss = lax.psum(jnp.sum(loss8[:, 0]), ("x", "y", "c"))

    exchanges = []

    def scatter(tag, grads, kinds):
        pieces = _grad_pieces(grads, kinds)
        own = _exchange_own(f"reduce_scatter_own_{tag}", me, grads, pieces)
        sems, grads, own, tok = _exchange_start(f"reduce_scatter_start_{tag}", grads, own, pieces, [list(range(len(pieces)))])
        exchanges.append((grads, own, pieces, sems[0]))
        return tok

    small = {}
    for l in reversed(range(depth)):
        xl, proj, h, o_attn, lse, o_hg, states, mixed, x_mid, u, h2 = saved[l]
        full_in, full_out, full_up, full_down = full[l]
        hs = full_up.shape[3]
        gw_down = _mm_tn(f"grad_w_down_l{l}", u, dxb, u.shape[1], a_fn=_relu2)
        dx_mid, dx_mid_b, du, dmixed, dmlp8 = _mlp_bwd(l, dx, dxb, x_mid, g_mlp, u, full_out, full_up, full_down)
        gw_up = _mm_tn(f"grad_w_up_l{l}", h2, du, d, out_block_w=hs)
        gw_out = _mm_tn(f"grad_w_out_l{l}", mixed, dx_mid_b, mixed.shape[1])
        g_attn_t = tied(g_attn, scatter(f"mlp_l{l}", [gw_down, gw_up, gw_out], ["rows", "up", "rows"]))
        do, delta, dattn8 = _attn_norm_bwd(l, dmixed, o_attn, g_attn_t)
        dqkv = _attn_bwd(l, proj, do, lse, delta, cos, sin)
        dhg, dlb8, dhgain8 = _hgrn_bwd(l, proj, lb3, g_hg, o_hg, dmixed, states)
        gin = _mm_tn(f"grad_w_in_qkv_l{l}", dqkv, h, PROJ_W, a_lead=True)
        gw_in = _mm_tn(f"grad_w_in_hg_l{l}", dhg, h, PROJ_W, a_lead=True, out_block_off=3, prev=gin)
        g_mix_t = tied(g_mix, scatter(f"mix_l{l}", [gw_in], ["rows"]))
        dx, dxb, dmix8 = _bwd_inproj(l, dqkv, dhg, full_in, xl, g_mix_t, dx_mid)
        small[l] = (dmix8, dattn8, dlb8, dhgain8, dmlp8)

    def fin(p8):
        return jnp.sum(p8, axis=0)
    dlogits = lb_jac * fin(small[1][2])[None, :]
    small_grads = [jnp.stack([fin(small[l][0]) for l in range(depth)]), jnp.stack([fin(small[l][1]) for l in range(depth)]),
                   dlogits, jnp.stack([fin(small[l][3]) for l in range(depth)]),
                   jnp.stack([fin(small[l][4]) for l in range(depth)]), fin(dnorm_final8)]
    small_w = [norm_mix, attn_out_gain, hgrn_lb_logits, hgrn_out_gain, norm_mlp, norm_final]
    small_m = [m_norm_mix, m_attn_out_gain, m_hgrn_lb_logits, m_hgrn_out_gain, m_norm_mlp, m_norm_final]
    small_v = [v_norm_mix, v_attn_out_gain, v_hgrn_lb_logits, v_hgrn_out_gain, v_norm_mlp, v_norm_final]
    g_small = _all_reduce_small(_pack_small(small_grads))
    d_small, m_small, v_small = _adam_small(g_small, _pack_small(small_w), _pack_small(small_m), _pack_small(small_v))
    gs, ds, ms, vs = (_unpack_small(t, small_w) for t in (g_small, d_small, m_small, v_small))

    all_grads, all_lands, all_pieces, waits = [], [], [], []
    for grads, own, pieces, (send, recv) in exchanges:
        first = len(all_pieces)
        all_pieces += [p._replace(src=p.src + len(all_grads)) for p in pieces]
        waits.append((list(range(first, first + len(pieces))), send, recv))
        all_grads += grads
        all_lands += own
    _, landed = _exchange_wait("reduce_scatter_wait", all_grads, all_lands, all_pieces, waits, d_small)
    r_down, r_up, r_out, r_in = ([landed[4 + i], landed[i]] for i in range(4))
    big = {
        "w_in": [flip(t) for t in _adam_big("adam_w_in", r_in, flip(w_in), flip(m_w_in), flip(v_w_in), 2)],
        "w_out": _adam_big("adam_w_out", r_out, w_out, m_w_out, v_w_out, 1),
        "w_up": _adam_big("adam_w_up", r_up, w_up, m_w_up, v_w_up, 2),
        "w_down": _adam_big("adam_w_down", r_down, w_down, m_w_down, v_w_down, 4),
    }

    def gather(idx, small_list):
        by_name = dict(zip(SMALL_NAMES, small_list))
        return [by_name["norm_mix"], big["w_in"][idx], by_name["attn_out_gain"], by_name["hgrn_lb_logits"],
                by_name["hgrn_out_gain"], big["w_out"][idx], by_name["norm_mlp"], big["w_up"][idx], big["w_down"][idx],
                by_name["norm_final"]]

    return (loss, dx[None], *gather(0, gs), *gather(1, ds), *gather(2, ms), *gather(3, vs))
```

```python
import functools
from typing import Callable, NamedTuple

import jax
import jax.numpy as jnp
from jax import lax
from jax.experimental import pallas as pl
from jax.experimental.pallas import tpu as pltpu

F32 = jnp.float32
BF = jnp.bfloat16

N_DEV = 8
ATTN_W = 512
HGRN_W = 512
HGRN_HEADS = 4
HGRN_DIM = 128
SEG = 512
N_SEG = 7
PROJ_W = N_SEG * SEG
MIX_W = ATTN_W + HGRN_W
SPAN = 128
DILATIONS = (1, 4, 16)
HGRN_CHUNK = 16
ROPE_THETA = 10000.0
NORM_EPS = 1e-6
MASK_VALUE = -1e30
ATTN_SCALE = 0.125
HGRN_SCALE = HGRN_DIM ** -0.5
ADAM_LR = 0.001
ADAM_B1 = 0.9
ADAM_B2 = 0.999
ADAM_EPS = 1e-08
ADAM_WD = 0.01
ADAM_STEP = 10
LANES = 128
VMEM_LIMIT = 56 * 1024 * 1024

NN = ((1,), (0,))
NT = ((1,), (1,))
TN = ((0,), (0,))
MESH = pl.DeviceIdType.MESH


def _dot(a, b, dims):
    return lax.dot_general(a, b, (dims, ((), ())), preferred_element_type=F32)


def _cparams(sem):
    return pltpu.CompilerParams(dimension_semantics=sem, vmem_limit_bytes=VMEM_LIMIT)


def _part8(x):
    r, n = x.shape
    return jnp.sum(x.reshape(r // 8, 8, n), axis=0)


def _sigmoid(x):
    return 1.0 / (1.0 + jnp.exp(-x))


def _rms_fwd(x, gain):
    r = lax.rsqrt(jnp.mean(x * x, axis=-1, keepdims=True) + NORM_EPS)
    return x * r * gain


def _rms_bwd(dy, x, gain):
    r = lax.rsqrt(jnp.mean(x * x, axis=-1, keepdims=True) + NORM_EPS)
    xn = x * r
    dxn = dy * gain
    dx = r * (dxn - xn * jnp.mean(dxn * xn, axis=-1, keepdims=True))
    return dx, dy * xn


def _rope_partner(x):
    n = x.shape[-1]
    lane = lax.broadcasted_iota(jnp.int32, x.shape, x.ndim - 1)
    return jnp.where((lane % 64) < 32, pltpu.roll(x, n - 32, x.ndim - 1), pltpu.roll(x, 32, x.ndim - 1))


def _tile_lanes(t, reps):
    return jnp.concatenate([t] * reps, axis=-1)


def _mm_tn(name, a, b, out_rows, a_lead=False, out_block_off=0, prev=None, out_block_w=None, a_fn=None,
           tm=512, tn=1024, sub=512):
    kdim, n = b.shape
    m = a.shape[-1]
    tm, tn, sub = min(tm, m), min(tn, n), min(sub, kdim)
    mt = m // tm
    n_lead = a.shape[0] if a_lead else 1
    if a_lead:
        a_spec = pl.BlockSpec((None, kdim, tm), lambda j, i: (i // mt, 0, i % mt))
    else:
        a_spec = pl.BlockSpec((kdim, tm), lambda j, i: (0, i))
    b_spec = pl.BlockSpec((kdim, tn), lambda j, i: (0, j))
    if out_block_w:
        nb = tn // out_block_w
        o_shape = jax.ShapeDtypeStruct((n // out_block_w, out_rows, out_block_w), BF)
        o_spec = pl.BlockSpec((nb, tm, out_block_w), lambda j, i: (j, i + out_block_off, 0))
    else:
        nb = 0
        o_shape = jax.ShapeDtypeStruct((out_rows, n), BF)
        o_spec = pl.BlockSpec((tm, tn), lambda j, i: (i + out_block_off, j))
    arrays, specs, aliases = [a, b], [a_spec, b_spec], {}
    if prev is not None:
        arrays.append(prev)
        specs.append(pl.BlockSpec(memory_space=pl.ANY))
        aliases = {2: 0}

    def body(*refs):
        a_ref, b_ref, o_ref = refs[0], refs[1], refs[-1]
        acc = None
        for k in range(kdim // sub):
            av = a_ref[k * sub:(k + 1) * sub, :]
            if a_fn is not None:
                av = a_fn(av)
            part = _dot(av, b_ref[k * sub:(k + 1) * sub, :], TN)
            acc = part if acc is None else acc + part
        if nb:
            for t in range(nb):
                o_ref[t] = acc[:, t * out_block_w:(t + 1) * out_block_w].astype(BF)
        else:
            o_ref[...] = acc.astype(BF)

    return pl.pallas_call(
        body, grid=(n // tn, n_lead * mt), in_specs=specs, out_specs=o_spec, out_shape=o_shape,
        compiler_params=_cparams(("parallel", "parallel")), name=name, input_output_aliases=aliases,
    )(*arrays)


def _pack_weights(w_in_t, w_out, w_up, w_down):
    depth = w_in_t.shape[0]
    arrays = (w_in_t, w_out, w_up, w_down)

    def body(*refs):
        for src, dst in zip(refs[:4], refs[4:]):
            dst[...] = src[...].astype(BF)

    specs = [pl.BlockSpec((None,) + a.shape[1:], lambda l: (l, 0, 0)) for a in arrays]
    return pl.pallas_call(
        body, grid=(depth,), in_specs=specs, out_specs=specs,
        out_shape=[jax.ShapeDtypeStruct(a.shape, BF) for a in arrays],
        compiler_params=_cparams(("arbitrary",)), name="pack_weights",
    )(*arrays)


def _my_position():
    x, y, c = lax.axis_index("x"), lax.axis_index("y"), lax.axis_index("c")
    return x, y, c, 4 * x + 2 * y + c


def _peer(x, y, c, k):
    px = 1 - x if k & 4 else x
    py = 1 - y if k & 2 else y
    pc = 1 - c if k & 1 else c
    return (px, py, pc), 4 * px + 2 * py + pc


PEER_ORDER = (1, 2, 4, 3, 5, 6, 7)


class _Piece(NamedTuple):
    src: int
    send: Callable
    slot: Callable
    land_shape: tuple
    own_src: tuple
    own_slot: tuple


HBM_SPEC = pl.BlockSpec(memory_space=pltpu.HBM)
SEM_SPEC = pl.BlockSpec(memory_space=pltpu.SEMAPHORE)
ANY_SPEC = pl.BlockSpec(memory_space=pl.ANY)


def _in_hbm(arrays):
    return [pltpu.with_memory_space_constraint(a, pltpu.HBM) for a in arrays]


def _hbm_like(arrays):
    return [pltpu.HBM(a.shape, a.dtype) for a in arrays]


def _rows_of(rows):
    return lambda ref, dev: ref.at[pl.ds(pl.multiple_of(dev * rows, 16), rows), :]


def _exchange_own(name, me, srcs, pieces):
    n = len(pieces)

    def body(me_ref, *refs):
        for i in range(n):
            refs[n + i][...] = refs[i][...]

    def spec(block_and_index):
        block, index = block_and_index
        return pl.BlockSpec(block, lambda i, me_ref: index(me_ref[0]))

    return pl.pallas_call(
        body,
        grid_spec=pltpu.PrefetchScalarGridSpec(
            num_scalar_prefetch=1, grid=(1,), in_specs=[spec(p.own_src) for p in pieces],
            out_specs=[spec(p.own_slot) for p in pieces]),
        out_shape=[jax.ShapeDtypeStruct(p.land_shape, BF) for p in pieces],
        compiler_params=_cparams(("arbitrary",)), name=name,
    )(me, *[srcs[p.src] for p in pieces])


def _exchange_start(name, srcs, lands, pieces, groups):
    n_src, n, n_g = len(srcs), len(pieces), len(groups)

    def body(*refs):
        src_refs, land_refs = refs[:n_src], refs[n_src:n_src + n]
        sems, token = refs[n_src + n:n_src + n + 2 * n_g], refs[-1]
        x, y, c, me = _my_position()
        for g, idxs in enumerate(groups):
            for k in PEER_ORDER:
                peer, pid = _peer(x, y, c, k)
                for j, i in enumerate(idxs):
                    p = pieces[i]
                    pltpu.make_async_remote_copy(
                        src_ref=p.send(src_refs[p.src], pid), dst_ref=p.slot(land_refs[i], me),
                        send_sem=sems[2 * g].at[(k - 1) * len(idxs) + j], recv_sem=sems[2 * g + 1].at[(k - 1) * len(idxs) + j],
                        device_id=peer, device_id_type=MESH).start()
        token[...] = jnp.zeros(token.shape, F32)

    sem_shapes = [pltpu.SemaphoreType.DMA(((N_DEV - 1) * len(idxs),)) for idxs in groups for _ in range(2)]
    res = pl.pallas_call(
        body, in_specs=[HBM_SPEC] * (n_src + n),
        out_specs=[SEM_SPEC] * (2 * n_g) + [HBM_SPEC] * (n_src + n) + [pl.BlockSpec(memory_space=pltpu.VMEM)],
        out_shape=sem_shapes + _hbm_like(srcs) + _hbm_like(lands) + [jax.ShapeDtypeStruct((8, LANES), F32)],
        input_output_aliases={i: 2 * n_g + i for i in range(n_src + n)},
        compiler_params=pltpu.CompilerParams(has_side_effects=pltpu.SideEffectType.DATAFLOW_SIDE_EFFECTING),
        name=name,
    )(*_in_hbm(srcs), *_in_hbm(lands))
    sems = [(res[2 * g], res[2 * g + 1]) for g in range(n_g)]
    return sems, list(res[2 * n_g:2 * n_g + n_src]), list(res[2 * n_g + n_src:2 * n_g + n_src + n]), res[-1]


def _exchange_wait(name, srcs, lands, pieces, waits, after):
    n_src, n, n_g = len(srcs), len(lands), len(waits)

    def body(*refs):
        src_refs, land_refs = refs[:n_src], refs[n_src:n_src + n]
        sems = refs[n_src + n:n_src + n + 2 * n_g]
        x, y, c, me = _my_position()
        at = 0
        for g, (idxs, _, _) in enumerate(waits):
            for k in PEER_ORDER:
                peer, pid = _peer(x, y, c, k)
                for j, i in enumerate(idxs):
                    p = pieces[i]
                    cp = pltpu.make_async_remote_copy(
                        src_ref=p.send(src_refs[p.src], pid), dst_ref=p.slot(land_refs[at + j], pid),
                        send_sem=sems[2 * g].at[(k - 1) * len(idxs) + j], recv_sem=sems[2 * g + 1].at[(k - 1) * len(idxs) + j],
                        device_id=peer, device_id_type=MESH)
                    cp.wait_send()
                    cp.wait_recv()
            at += len(idxs)

    sem_args = [s for _, send, recv in waits for s in (send, recv)]
    res = pl.pallas_call(
        body, in_specs=[HBM_SPEC] * (n_src + n) + [SEM_SPEC] * (2 * n_g) + [ANY_SPEC],
        out_specs=[HBM_SPEC] * (n_src + n), out_shape=_hbm_like(srcs) + _hbm_like(lands),
        input_output_aliases={i: i for i in range(n_src + n)},
        compiler_params=pltpu.CompilerParams(has_side_effects=pltpu.SideEffectType.DATAFLOW_SIDE_EFFECTING),
        name=name,
    )(*srcs, *lands, *sem_args, after)
    return list(res[:n_src]), list(res[n_src:])


def _weight_pieces(p_in, p_out, p_up, p_down):
    depth, cin, d = p_in.shape
    rout, hs = p_out.shape[1], p_up.shape[2]
    pieces = []
    for l in range(depth):
        whole = functools.partial(lambda ref, dev, l: ref.at[l], l=l)
        layer = functools.partial(lambda dev, l: (l, 0, 0), l=l)

        def rows(src, n_rows, whole=whole, layer=layer):
            return _Piece(src, whole, lambda ref, dev: _rows_of(n_rows)(ref.at[0], dev), (1, N_DEV * n_rows, d),
                          ((None, n_rows, d), layer), ((None, n_rows, d), lambda dev: (0, dev, 0)))

        pieces += [
            rows(0, cin), rows(1, rout),
            _Piece(2, whole, lambda ref, dev: ref.at[0, dev], (1, N_DEV, d, hs),
                   ((None, d, hs), layer), ((None, None, d, hs), lambda dev: (0, dev, 0, 0))),
            rows(3, hs),
        ]
    return pieces


def _grad_pieces(g_pair, kinds):
    pieces = []
    for i, (g, kind) in enumerate(zip(g_pair, kinds)):
        lead = lambda dev: (dev, 0, 0)
        if kind == "up":
            blk = ((None,) + g.shape[1:], lead)
            pieces.append(_Piece(i, lambda ref, dev: ref.at[dev], lambda ref, dev: ref.at[dev], g.shape, blk, blk))
        else:
            rows, cols = g.shape[0] // N_DEV, g.shape[1]
            pieces.append(_Piece(i, _rows_of(rows), lambda ref, dev: ref.at[dev], (N_DEV, rows, cols),
                                 ((rows, cols), lambda dev: (dev, 0)), ((None, rows, cols), lead)))
    return pieces


def _all_reduce_small(vec):
    rows = vec.shape[0]

    def body(v_ref, o_ref, buf_ref, send_sems, recv_sems):
        x, y, c, me = _my_position()
        buf_ref[me] = v_ref[...]
        sends = []
        for k in PEER_ORDER:
            peer, _ = _peer(x, y, c, k)
            cp = pltpu.make_async_remote_copy(src_ref=v_ref, dst_ref=buf_ref.at[me], send_sem=send_sems.at[k - 1],
                                              recv_sem=recv_sems.at[k - 1], device_id=peer, device_id_type=MESH)
            cp.start()
            sends.append(cp)
        for k in PEER_ORDER:
            peer, pid = _peer(x, y, c, k)
            pltpu.make_async_remote_copy(src_ref=v_ref, dst_ref=buf_ref.at[pid], send_sem=send_sems.at[k - 1],
                                         recv_sem=recv_sems.at[k - 1], device_id=peer, device_id_type=MESH).wait_recv()
        for cp in sends:
            cp.wait_send()
        total = buf_ref[0]
        for dev in range(1, N_DEV):
            total = total + buf_ref[dev]
        o_ref[...] = total

    vm = pl.BlockSpec(memory_space=pltpu.VMEM)
    return pl.pallas_call(
        body, in_specs=[vm], out_specs=vm, out_shape=jax.ShapeDtypeStruct(vec.shape, F32),
        scratch_shapes=[pltpu.VMEM((N_DEV, rows, LANES), F32), pltpu.SemaphoreType.DMA((N_DEV - 1,)),
                        pltpu.SemaphoreType.DMA((N_DEV - 1,))],
        name="all_reduce_small",
    )(vec)


def _weights_arrive(step, pieces, sems):
    copies = [pltpu.make_async_copy(src, dst, sems.at[i]) for i, (src, dst) in enumerate(pieces)]

    @pl.when(step == 0)
    def _():
        for cp in copies:
            cp.start()

    def ready(i):
        @pl.when(step == 0)
        def _():
            copies[i].wait()

    return ready


def _fwd_inproj(layer, x, gain, g_in, cos, sin, tm=512):
    s, d = x.shape

    def body(x_ref, gain_ref, w_hbm, cos_ref, sin_ref, proj_ref, h_ref, w_ref, sems):
        ready = _weights_arrive(pl.program_id(0), [(w_hbm.at[0, pl.ds(seg * SEG, SEG), :], w_ref.at[pl.ds(seg * SEG, SEG), :])
                                                   for seg in range(N_SEG)], sems)
        h = _rms_fwd(x_ref[...], gain_ref[...]).astype(BF)
        h_ref[...] = h
        cs = _tile_lanes(cos_ref[...], SEG // LANES)
        sn = _tile_lanes(sin_ref[...], SEG // LANES)
        for seg in range(N_SEG):
            ready(seg)
            acc = _dot(h, w_ref[seg * SEG:(seg + 1) * SEG, :], NT)
            if seg < 2:
                acc = acc * cs + _rope_partner(acc) * sn
            if seg == 0:
                acc = acc * ATTN_SCALE
            proj_ref[:, seg * SEG:(seg + 1) * SEG] = acc

    return pl.pallas_call(
        body, grid=(s // tm,),
        in_specs=[pl.BlockSpec((tm, d), lambda i: (i, 0)), pl.BlockSpec((None, 1, d), lambda i: (layer, 0, 0)),
                  ANY_SPEC,
                  pl.BlockSpec((tm, LANES), lambda i: (i, 0)), pl.BlockSpec((tm, LANES), lambda i: (i, 0))],
        out_specs=[pl.BlockSpec((tm, PROJ_W), lambda i: (i, 0)), pl.BlockSpec((tm, d), lambda i: (i, 0))],
        out_shape=[jax.ShapeDtypeStruct((s, PROJ_W), F32), jax.ShapeDtypeStruct((s, d), BF)],
        scratch_shapes=[pltpu.VMEM((PROJ_W, d), BF), pltpu.SemaphoreType.DMA((N_SEG,))],
        compiler_params=_cparams(("arbitrary",)), name=f"fwd_inproj_l{layer}",
    )(x, gain, g_in, cos, sin)


ATTN_UNIT = SPAN * max(DILATIONS)
ATTN_GROUP = 4


def _attn_masks(first_block_has_prev):
    row = lax.broadcasted_iota(jnp.int32, (SPAN, 2 * SPAN), 0)
    col = lax.broadcasted_iota(jnp.int32, (SPAN, 2 * SPAN), 1)
    band = (col >= row) & (col <= row + SPAN)
    lane = lax.broadcasted_iota(jnp.int32, (SPAN, LANES), 1)
    return band & ((col >= SPAN) | first_block_has_prev), band, lane < 64


def _attn_specs(n_in_extra, unit_of=lambda n: n):
    pairs = ATTN_W // LANES
    q_spec = pl.BlockSpec((ATTN_UNIT, LANES), lambda p, n: (unit_of(n), p))

    def prev(seg):
        return pl.BlockSpec((ATTN_UNIT, LANES), lambda p, n: (jnp.maximum(unit_of(n) - 1, 0), seg * pairs + p))

    def cur(seg):
        return pl.BlockSpec((ATTN_UNIT, LANES), lambda p, n: (unit_of(n), seg * pairs + p))

    return [q_spec, prev(1), cur(1), prev(2), cur(2)] + [q_spec] * n_in_extra


def _attn_groups(dil):
    blocks = ATTN_UNIT // (SPAN * dil)
    pairs = [(r, b) for r in range(dil) for b in range(blocks)]
    return [pairs[i:i + ATTN_GROUP] for i in range(0, len(pairs), ATTN_GROUP)]


def _block_rows(dil, r, b, n=1):
    start = r + dil * SPAN * b
    return pl.ds(start, n * SPAN, stride=dil) if dil > 1 else pl.ds(start, n * SPAN)


def _block_keys(prev_ref, cur_ref, dil, r, b):
    if b > 0:
        return cur_ref[_block_rows(dil, r, b - 1, 2), :]
    last = ATTN_UNIT // (SPAN * dil) - 1
    return jnp.concatenate([prev_ref[_block_rows(dil, r, last), :], cur_ref[_block_rows(dil, r, 0), :]], axis=0)


def _attn_fwd(layer, proj):
    s = proj.shape[0]
    n_pat = len(DILATIONS)
    merge_rows = 256

    def body(q_ref, kp_ref, kc_ref, vp_ref, vc_ref, o_ref, lse_ref, o_scr, lse_scr):
        m_first, m_rest, is_a = _attn_masks(pl.program_id(1) > 0)
        sels = (is_a, jnp.logical_not(is_a))
        is_a_keys = lax.broadcasted_iota(jnp.int32, (2 * SPAN, LANES), 1) < 64
        for pi, dil in enumerate(DILATIONS):
            for group in _attn_groups(dil):
                items = [(r, b, h) for r, b in group for h in range(2)]
                q = {rb: q_ref[_block_rows(dil, *rb), :] for rb in group}
                k = {rb: _block_keys(kp_ref, kc_ref, dil, *rb).astype(BF) for rb in group}
                v = {rb: _block_keys(vp_ref, vc_ref, dil, *rb).astype(BF) for rb in group}
                v_sum = {rb: (jnp.where(is_a_keys, v[rb], 1.0), jnp.where(is_a_keys, 1.0, v[rb])) for rb in group}
                sc = [jnp.where(m_first if b == 0 else m_rest,
                                _dot(jnp.where(sels[h], q[r, b], 0.0).astype(BF), k[r, b], NT), MASK_VALUE)
                      for r, b, h in items]
                mx = [jnp.max(jnp.maximum(t[:, :SPAN], t[:, SPAN:]), axis=-1, keepdims=True) for t in sc]
                p = [jnp.exp(t - m).astype(BF) for t, m in zip(sc, mx)]
                both = [_dot(t, v_sum[r, b][h], NN) for t, (r, b, h) in zip(p, items)]
                for j, (r, b) in enumerate(group):
                    t_a, t_b = both[2 * j], both[2 * j + 1]
                    den = pltpu.roll(jnp.where(is_a, t_b, t_a), 64, 1)
                    o_scr[pi, _block_rows(dil, r, b), :] = jnp.where(is_a, t_a, t_b) / den
                    lse_scr[pi, _block_rows(dil, r, b), :] = jnp.where(is_a, mx[2 * j], mx[2 * j + 1]) + jnp.log(den)
        for c in range(ATTN_UNIT // merge_rows):
            rows = slice(c * merge_rows, (c + 1) * merge_rows)
            ls = [lse_scr[pi, rows, :] for pi in range(n_pat)]
            mx = functools.reduce(jnp.maximum, ls)
            ws = [jnp.exp(l - mx) for l in ls]
            den = functools.reduce(jnp.add, ws)
            o_ref[rows, :] = functools.reduce(jnp.add, [w * o_scr[pi, rows, :] for pi, w in enumerate(ws)]) / den
            lse_ref[rows, :] = mx + jnp.log(den)

    out_spec = pl.BlockSpec((ATTN_UNIT, LANES), lambda p, n: (n, p))
    return pl.pallas_call(
        body, grid=(ATTN_W // LANES, s // ATTN_UNIT), in_specs=_attn_specs(0), out_specs=[out_spec, out_spec],
        out_shape=[jax.ShapeDtypeStruct((s, ATTN_W), F32)] * 2,
        scratch_shapes=[pltpu.VMEM((n_pat, ATTN_UNIT, LANES), F32)] * 2,
        compiler_params=_cparams(("parallel", "arbitrary")), name=f"attn_fwd_l{layer}",
    )(proj, proj, proj, proj, proj)


def _attn_norm(layer, o, gain, mixed, tm=512):
    s = o.shape[0]

    def body(o_ref, gain_ref, mixed_ref, n_ref):
        n_ref[...] = _rms_fwd(o_ref[...], gain_ref[...]).astype(BF)

    blk = pl.BlockSpec((tm, ATTN_W), lambda i: (i, 0))
    return pl.pallas_call(
        body, grid=(s // tm,),
        in_specs=[blk, pl.BlockSpec((None, 1, ATTN_W), lambda i: (layer, 0, 0)), pl.BlockSpec(memory_space=pl.ANY)],
        out_specs=blk, out_shape=jax.ShapeDtypeStruct(mixed.shape, BF), input_output_aliases={2: 0},
        compiler_params=_cparams(("parallel",)), name=f"attn_norm_l{layer}",
    )(o, gain, mixed)


def _chunk_cumsum(x, reverse=False):
    n = x.shape[0]
    pos = lax.broadcasted_iota(jnp.int32, x.shape, 0) % HGRN_CHUNK
    for sh in (1, 2, 4, 8):
        if reverse:
            x = x + jnp.where(pos < HGRN_CHUNK - sh, pltpu.roll(x, n - sh, 0), 0.0)
        else:
            x = x + jnp.where(pos >= sh, pltpu.roll(x, sh, 0), 0.0)
    return x


def _chunk_row(x, row):
    r, n = x.shape
    x3 = x.reshape(r // HGRN_CHUNK, HGRN_CHUNK, n)
    return jnp.broadcast_to(x3[:, row:row + 1, :], x3.shape).reshape(r, n)


def _hgrn_pre(qh, z, lb):
    sig = _sigmoid(z)
    f = lb + (1.0 - lb) * sig
    k = (1.0 - lb) * _sigmoid(-z)
    sq = _sigmoid(qh)
    q = qh * sq * HGRN_SCALE
    g = _chunk_cumsum(jnp.log(f))
    g_mid = _chunk_row(g, HGRN_CHUNK // 2 - 1)
    g_last = _chunk_row(g, HGRN_CHUNK - 1)
    e_q, e_k = jnp.exp(g - g_mid), jnp.exp(g_mid - g)
    e_in, e_out = jnp.exp(g), jnp.exp(g_last - g)
    return dict(sig=sig, f=f, k=k, sq=sq, q=q, g_last=g_last, e_q=e_q, e_k=e_k, e_in=e_in, e_out=e_out,
                qt=q * e_q, kt=k * e_k, qg=q * e_in, kout=k * e_out)


def _hgrn_mask():
    row = lax.broadcasted_iota(jnp.int32, (LANES, LANES), 0)
    col = lax.broadcasted_iota(jnp.int32, (LANES, LANES), 1)
    return (row // HGRN_CHUNK == col // HGRN_CHUNK) & (col <= row)


def _hgrn_in_specs(layer, rev, nblk):
    def blk(b):
        return nblk - 1 - b if rev else b
    first = 3 * ATTN_W // HGRN_W
    specs = [pl.BlockSpec((HGRN_ROWS, HGRN_W), functools.partial(lambda b, seg: (blk(b), first + seg), seg=seg))
             for seg in range(4)]
    specs.append(pl.BlockSpec((None, 1, HGRN_W), lambda b: (layer, 0, 0)))
    specs.append(pl.BlockSpec((None, 1, HGRN_DIM), lambda b: (layer, 0, 0)))
    return specs, blk


def _head(x, h):
    return x[:, h * HGRN_DIM:(h + 1) * HGRN_DIM]


def _chunk(x, c):
    return x[c * HGRN_CHUNK:(c + 1) * HGRN_CHUNK]


def _sub(x, sb):
    return x[sb * LANES:(sb + 1) * LANES]


HGRN_ROWS = 256
HEADS = range(HGRN_HEADS)
SUBS = range(HGRN_ROWS // LANES)
CHUNKS = range(HGRN_ROWS // HGRN_CHUNK)


def _hgrn_fwd(layer, proj, lb, gain):
    s = proj.shape[0]
    nblk = s // HGRN_ROWS
    cpb = len(CHUNKS)

    def body(q_ref, f_ref, i_ref, g_ref, lb_ref, gain_ref, o_ref, rec_ref, st_ref, state):
        @pl.when(pl.program_id(0) == 0)
        def _():
            state[...] = jnp.zeros(state.shape, F32)

        pre = _hgrn_pre(q_ref[...], f_ref[...], lb_ref[...])
        v = i_ref[...].astype(BF)
        qt, kt, qg, kout = (pre[n].astype(BF) for n in ("qt", "kt", "qg", "kout"))
        dec = jnp.exp(pre["g_last"])
        mask = _hgrn_mask()
        a = [[jnp.where(mask, _dot(_sub(_head(qt, h), sb), _sub(_head(kt, h), sb), NT), 0.0).astype(BF) for sb in SUBS]
             for h in HEADS]
        o_intra = [[_dot(a[h][sb], _sub(_head(v, h), sb), NN) for sb in SUBS] for h in HEADS]
        update = [[_dot(_chunk(_head(v, h), c), _chunk(_head(kout, h), c), TN) for c in CHUNKS] for h in HEADS]
        for h in HEADS:
            st = state[h]
            for c in CHUNKS:
                st_ref[h, c * LANES:(c + 1) * LANES, :] = st.astype(BF)
                st = st * _head(dec, h)[c * HGRN_CHUNK:c * HGRN_CHUNK + 1, :] + update[h][c]
            state[h] = st
        inter = [[_dot(_chunk(_head(qg, h), c), st_ref[h, c * LANES:(c + 1) * LANES, :].astype(BF), NT) for c in CHUNKS]
                 for h in HEADS]
        o = [jnp.concatenate(o_intra[h], axis=0) + jnp.concatenate(inter[h], axis=0) for h in HEADS]
        o_ref[...] = jnp.concatenate(o, axis=1)
        gate = g_ref[...]
        normed = jnp.concatenate([_rms_fwd(o[h], gain_ref[...]) for h in HEADS], axis=1)
        rec_ref[...] = (normed * (gate * _sigmoid(gate))).astype(BF)

    specs, _ = _hgrn_in_specs(layer, False, nblk)
    return pl.pallas_call(
        body, grid=(nblk,), in_specs=specs,
        out_specs=[pl.BlockSpec((HGRN_ROWS, HGRN_W), lambda b: (b, 0)), pl.BlockSpec((HGRN_ROWS, HGRN_W), lambda b: (b, 1)),
                   pl.BlockSpec((HGRN_HEADS, cpb * LANES, LANES), lambda b: (0, b, 0))],
        out_shape=[jax.ShapeDtypeStruct((s, HGRN_W), F32), jax.ShapeDtypeStruct((s, MIX_W), BF),
                   jax.ShapeDtypeStruct((HGRN_HEADS, nblk * cpb * LANES, LANES), BF)],
        scratch_shapes=[pltpu.VMEM((HGRN_HEADS, LANES, LANES), F32)],
        compiler_params=_cparams(("arbitrary",)), name=f"hgrn_fwd_l{layer}",
    )(proj, proj, proj, proj, lb, gain)


def _relu2(u):
    return jnp.square(jnp.maximum(u, 0)).astype(BF)


def _mlp_weight_pieces(out_hbm, up_hbm, down_hbm, out_w, up_w, down_w):
    nblk, hs = up_w.shape[0], up_w.shape[2]
    return ([(out_hbm.at[0], out_w)] + [(up_hbm.at[0, j], up_w.at[j]) for j in range(nblk)]
            + [(down_hbm.at[0, pl.ds(j * hs, hs), :], down_w.at[pl.ds(j * hs, hs), :]) for j in range(nblk)])


def _mlp_weight_scratch(g_out, g_up, g_down):
    return [pltpu.VMEM(g.shape[1:], BF) for g in (g_out, g_up, g_down)] + [pltpu.SemaphoreType.DMA((1 + 2 * g_up.shape[1],))]


def _mlp_fwd(layer, x, mixed, gain, g_out, g_up, g_down, tm=256):
    s, d = x.shape
    mw = mixed.shape[1]
    nblk, hs = g_up.shape[1], g_up.shape[3]

    def body(x_ref, m_ref, gain_ref, out_hbm, up_hbm, down_hbm, o_ref, mid_ref, u_ref, h_ref, a_buf,
             out_w_ref, up_ref, down_ref, sems):
        ready = _weights_arrive(pl.program_id(0), _mlp_weight_pieces(out_hbm, up_hbm, down_hbm, out_w_ref, up_ref, down_ref),
                                sems)
        ready(0)
        xv = x_ref[...] + _dot(m_ref[...], out_w_ref[...], NN)
        mid_ref[...] = xv
        h = _rms_fwd(xv, gain_ref[...]).astype(BF)
        h_ref[...] = h
        for j in range(nblk):
            ready(1 + j)
            u = _dot(h, up_ref[j], NN)
            u_ref[:, j * hs:(j + 1) * hs] = u.astype(BF)
            a_buf[:, j * hs:(j + 1) * hs] = _relu2(u)
        acc = xv
        for j in range(nblk):
            ready(1 + nblk + j)
            acc = acc + _dot(a_buf[:, j * hs:(j + 1) * hs], down_ref[j * hs:(j + 1) * hs, :], NN)
        o_ref[...] = acc

    row = pl.BlockSpec((tm, d), lambda i: (i, 0))
    return pl.pallas_call(
        body, grid=(s // tm,),
        in_specs=[row, pl.BlockSpec((tm, mw), lambda i: (i, 0)), pl.BlockSpec((None, 1, d), lambda i: (layer, 0, 0)),
                  ANY_SPEC, ANY_SPEC, ANY_SPEC],
        out_specs=[row, row, pl.BlockSpec((tm, nblk * hs), lambda i: (i, 0)), row],
        out_shape=[jax.ShapeDtypeStruct((s, d), F32), jax.ShapeDtypeStruct((s, d), F32),
                   jax.ShapeDtypeStruct((s, nblk * hs), BF), jax.ShapeDtypeStruct((s, d), BF)],
        scratch_shapes=[pltpu.VMEM((tm, nblk * hs), BF)] + _mlp_weight_scratch(g_out, g_up, g_down),
        compiler_params=_cparams(("arbitrary",)), name=f"mlp_fwd_l{layer}",
    )(x, mixed, gain, g_out, g_up, g_down)


def _loss_head(x, gain, target, tm=512):
    s, d = x.shape

    def body(x_ref, gain_ref, t_ref, dx_ref, dxb_ref, dgain_ref, loss_ref):
        i = pl.program_id(0)
        xv, gv = x_ref[...], gain_ref[...]
        err = _rms_fwd(xv, gv) - t_ref[...]
        dx, dgain = _rms_bwd(err * (1.0 / d), xv, gv)
        dx_ref[...] = dx
        dxb_ref[...] = dx.astype(BF)
        part = _part8(dgain)
        lpart = _part8(0.5 * jnp.mean(err * err, axis=-1, keepdims=True) * jnp.ones((1, LANES), F32))

        @pl.when(i == 0)
        def _():
            dgain_ref[...] = part
            loss_ref[...] = lpart

        @pl.when(i > 0)
        def _():
            dgain_ref[...] += part
            loss_ref[...] += lpart

    row = pl.BlockSpec((tm, d), lambda i: (i, 0))
    return pl.pallas_call(
        body, grid=(s // tm,),
        in_specs=[row, pl.BlockSpec((1, d), lambda i: (0, 0)), row],
        out_specs=[row, row, pl.BlockSpec((8, d), lambda i: (0, 0)), pl.BlockSpec((8, LANES), lambda i: (0, 0))],
        out_shape=[jax.ShapeDtypeStruct((s, d), F32), jax.ShapeDtypeStruct((s, d), BF), jax.ShapeDtypeStruct((8, d), F32),
                   jax.ShapeDtypeStruct((8, LANES), F32)],
        compiler_params=_cparams(("arbitrary",)), name="loss_head",
    )(x, gain, target)


def _accumulate_rows(i, ref, part):
    @pl.when(i == 0)
    def _():
        ref[...] = part

    @pl.when(i > 0)
    def _():
        ref[...] += part


def _mlp_bwd(layer, dx, dxb, x, gain, u, g_out, g_up, g_down, tm=256):
    s, d = x.shape
    mw = g_out.shape[1]
    nblk, hs = g_up.shape[1], g_up.shape[3]

    def body(dx_ref, dxb_ref, x_ref, gain_ref, u_ref, out_hbm, up_hbm, down_hbm, o_ref, ob_ref, du_ref, dm_ref,
             dgain_ref, out_w_ref, up_ref, down_ref, sems):
        ready = _weights_arrive(pl.program_id(0), _mlp_weight_pieces(out_hbm, up_hbm, down_hbm, out_w_ref, up_ref, down_ref),
                                sems)
        dxb_v = dxb_ref[...]
        for j in range(nblk):
            ready(1 + nblk + j)
            cols = slice(j * hs, (j + 1) * hs)
            da = _dot(dxb_v, down_ref[cols, :], NT)
            du_ref[:, cols] = (da * (2.0 * jnp.maximum(u_ref[:, cols].astype(F32), 0.0))).astype(BF)
        acc = jnp.zeros((tm, d), F32)
        for j in range(nblk):
            ready(1 + j)
            acc = acc + _dot(du_ref[:, j * hs:(j + 1) * hs], up_ref[j], NT)
        dxn, dgain = _rms_bwd(acc, x_ref[...], gain_ref[...])
        out = dx_ref[...] + dxn
        out_b = out.astype(BF)
        o_ref[...] = out
        ob_ref[...] = out_b
        ready(0)
        dm_ref[...] = _dot(out_b, out_w_ref[...], NT)
        _accumulate_rows(pl.program_id(0), dgain_ref, _part8(dgain))

    row = pl.BlockSpec((tm, d), lambda i: (i, 0))
    wide = pl.BlockSpec((tm, nblk * hs), lambda i: (i, 0))
    return pl.pallas_call(
        body, grid=(s // tm,),
        in_specs=[row, row, row, pl.BlockSpec((None, 1, d), lambda i: (layer, 0, 0)), wide,
                  ANY_SPEC, ANY_SPEC, ANY_SPEC],
        out_specs=[row, row, wide, pl.BlockSpec((tm, mw), lambda i: (i, 0)), pl.BlockSpec((8, d), lambda i: (0, 0))],
        out_shape=[jax.ShapeDtypeStruct((s, d), F32), jax.ShapeDtypeStruct((s, d), BF),
                   jax.ShapeDtypeStruct((s, nblk * hs), BF), jax.ShapeDtypeStruct((s, mw), F32),
                   jax.ShapeDtypeStruct((8, d), F32)],
        scratch_shapes=_mlp_weight_scratch(g_out, g_up, g_down),
        compiler_params=_cparams(("arbitrary",)), name=f"mlp_bwd_l{layer}",
    )(dx, dxb, x, gain, u, g_out, g_up, g_down)


def _attn_norm_bwd(layer, dmixed, o, gain, tm=512):
    s = o.shape[0]

    def body(dm_ref, o_ref, gain_ref, do_ref, delta_ref, dgain_ref):
        i = pl.program_id(0)
        ov = o_ref[...]
        do, dgain = _rms_bwd(dm_ref[...], ov, gain_ref[...])
        do_ref[...] = do
        row = lax.broadcasted_iota(jnp.int32, (ATTN_W, ATTN_W), 0)
        col = lax.broadcasted_iota(jnp.int32, (ATTN_W, ATTN_W), 1)
        same_head = jnp.where(row // 64 == col // 64, 1.0, 0.0)
        delta_ref[...] = jnp.dot(do * ov, same_head, precision=lax.Precision.HIGHEST, preferred_element_type=F32)
        part = _part8(dgain)

        @pl.when(i == 0)
        def _():
            dgain_ref[...] = part

        @pl.when(i > 0)
        def _():
            dgain_ref[...] += part

    blk = pl.BlockSpec((tm, ATTN_W), lambda i: (i, 0))
    return pl.pallas_call(
        body, grid=(s // tm,), in_specs=[blk, blk, pl.BlockSpec((None, 1, ATTN_W), lambda i: (layer, 0, 0))],
        out_specs=[blk, blk, pl.BlockSpec((8, ATTN_W), lambda i: (0, 0))],
        out_shape=[jax.ShapeDtypeStruct((s, ATTN_W), F32), jax.ShapeDtypeStruct((s, ATTN_W), F32),
                   jax.ShapeDtypeStruct((8, ATTN_W), F32)],
        compiler_params=_cparams(("arbitrary",)), name=f"attn_norm_bwd_l{layer}",
    )(dmixed, o, gain)


def _attn_bwd(layer, proj, do, lse, delta, cos, sin):
    s = proj.shape[0]
    n_units = s // ATTN_UNIT
    out_rows = 256

    def unit_of(n):
        return n_units - 1 - n

    def body(q_ref, kp_ref, kc_ref, vp_ref, vc_ref, do_ref, lse_ref, delta_ref, cos_ref, sin_ref, out_ref,
             dq_ref, dk_ref, dkp_ref, dv_ref, dvp_ref, carry_k, carry_v):
        step = pl.program_id(1)
        m_first, m_rest, is_a = _attn_masks(unit_of(step) > 0)
        sels = (is_a, jnp.logical_not(is_a))
        for ref in (dq_ref, dk_ref, dkp_ref, dv_ref, dvp_ref):
            ref[...] = jnp.zeros(ref.shape, F32)
        for dil in DILATIONS:
            last = ATTN_UNIT // (SPAN * dil) - 1
            for group in _attn_groups(dil):
                items = [(r, b, h) for r, b in group for h in range(2)]
                q = {rb: q_ref[_block_rows(dil, *rb), :] for rb in group}
                dov = {rb: do_ref[_block_rows(dil, *rb), :] for rb in group}
                lse_v = {rb: lse_ref[_block_rows(dil, *rb), :] for rb in group}
                delta_v = {rb: delta_ref[_block_rows(dil, *rb), :] for rb in group}
                k = {rb: _block_keys(kp_ref, kc_ref, dil, *rb).astype(BF) for rb in group}
                v = {rb: _block_keys(vp_ref, vc_ref, dil, *rb).astype(BF) for rb in group}
                qh = [jnp.where(sels[h], q[r, b], 0.0).astype(BF) for r, b, h in items]
                doh = [jnp.where(sels[h], dov[r, b], 0.0).astype(BF) for r, b, h in items]
                sc = [jnp.where(m_first if b == 0 else m_rest, _dot(qh[i], k[r, b], NT), MASK_VALUE)
                      for i, (r, b, h) in enumerate(items)]
                p = [jnp.exp(sc[i] - lse_v[r, b][:, 64 * h:64 * h + 1]) for i, (r, b, h) in enumerate(items)]
                ds = [(p[i] * (_dot(doh[i], v[r, b], NT) - delta_v[r, b][:, 64 * h:64 * h + 1])).astype(BF)
                      for i, (r, b, h) in enumerate(items)]
                dv = [_dot(p[i].astype(BF), doh[i], TN) for i in range(len(items))]
                dq = [_dot(ds[i], k[r, b], NN) for i, (r, b, h) in enumerate(items)]
                dk = [_dot(ds[i], qh[i], TN) for i in range(len(items))]
                for j, (r, b) in enumerate(group):
                    own = _block_rows(dil, r, b)
                    dq_ref[own, :] += jnp.where(is_a, dq[2 * j], dq[2 * j + 1])
                    dk2, dv2 = dk[2 * j] + dk[2 * j + 1], dv[2 * j] + dv[2 * j + 1]
                    dk_ref[own, :] += dk2[SPAN:]
                    dv_ref[own, :] += dv2[SPAN:]
                    if b > 0:
                        before = _block_rows(dil, r, b - 1)
                        dk_ref[before, :] += dk2[:SPAN]
                        dv_ref[before, :] += dv2[:SPAN]
                    else:
                        before = _block_rows(dil, r, last)
                        dkp_ref[before, :] += dk2[:SPAN]
                        dvp_ref[before, :] += dv2[:SPAN]
        has_next = step > 0
        for c in range(ATTN_UNIT // out_rows):
            rows = slice(c * out_rows, (c + 1) * out_rows)
            cs, sn = cos_ref[rows, :], sin_ref[rows, :]
            dqv = dq_ref[rows, :]
            dkv = dk_ref[rows, :] + jnp.where(has_next, carry_k[rows, :], 0.0)
            dvv = dv_ref[rows, :] + jnp.where(has_next, carry_v[rows, :], 0.0)
            out_ref[0, rows, :] = ((dqv * cs - _rope_partner(dqv) * sn) * ATTN_SCALE).astype(BF)
            out_ref[1, rows, :] = (dkv * cs - _rope_partner(dkv) * sn).astype(BF)
            out_ref[2, rows, :] = dvv.astype(BF)
        carry_k[...] = dkp_ref[...]
        carry_v[...] = dvp_ref[...]

    tab = pl.BlockSpec((ATTN_UNIT, LANES), lambda p, n: (unit_of(n), 0))
    return pl.pallas_call(
        body, grid=(ATTN_W // LANES, n_units), in_specs=_attn_specs(3, unit_of) + [tab, tab],
        out_specs=pl.BlockSpec((3, ATTN_UNIT, LANES), lambda p, n: (0, unit_of(n), p)),
        out_shape=jax.ShapeDtypeStruct((3, s, ATTN_W), BF),
        scratch_shapes=[pltpu.VMEM((ATTN_UNIT, LANES), F32)] * 7,
        compiler_params=_cparams(("parallel", "arbitrary")), name=f"attn_bwd_l{layer}",
    )(proj, proj, proj, proj, proj, do, lse, delta, cos, sin)


def _hgrn_bwd(layer, proj, lb, gain, o, dmixed, states):
    s = proj.shape[0]
    nblk = s // HGRN_ROWS
    cpb = len(CHUNKS)

    def body(q_ref, f_ref, i_ref, g_ref, lb_ref, gain_ref, o_ref, drec_ref, st_ref, dseg_ref, dlb_ref, dgain_ref,
             dstate, dst_buf):
        step = pl.program_id(0)

        @pl.when(step == 0)
        def _():
            dstate[...] = jnp.zeros(dstate.shape, F32)

        lbv, gv = lb_ref[...], gain_ref[...]
        qh, z, gate_in = q_ref[...], f_ref[...], g_ref[...]
        pre = _hgrn_pre(qh, z, lbv)
        v = i_ref[...].astype(BF)
        sg = _sigmoid(gate_in)
        ov, drec = o_ref[...], drec_ref[...]
        dnormed = drec * (gate_in * sg)
        back = [_rms_bwd(_head(dnormed, h), _head(ov, h), gv) for h in HEADS]
        do_b = jnp.concatenate([b[0] for b in back], axis=1).astype(BF)
        dgain = back[0][1] + back[1][1] + back[2][1] + back[3][1]
        normed = jnp.concatenate([_rms_fwd(_head(ov, h), gv) for h in HEADS], axis=1)
        dgate_in = drec * normed * (sg * (1.0 + gate_in * (1.0 - sg)))
        mask = _hgrn_mask()
        qt, kt, qg, kout = (pre[n].astype(BF) for n in ("qt", "kt", "qg", "kout"))
        dec = jnp.exp(pre["g_last"])
        def intra(fn):
            return jnp.concatenate([jnp.concatenate([fn(h, sb) for sb in SUBS], axis=0) for h in HEADS], axis=1)

        def hs(x, h, sb):
            return _sub(_head(x, h), sb)

        a = [[jnp.where(mask, _dot(hs(qt, h, sb), hs(kt, h, sb), NT), 0.0).astype(BF) for sb in SUBS] for h in HEADS]
        da = [[jnp.where(mask, _dot(hs(do_b, h, sb), hs(v, h, sb), NT), 0.0).astype(BF) for sb in SUBS] for h in HEADS]
        dv_intra = intra(lambda h, sb: _dot(a[h][sb], hs(do_b, h, sb), TN))
        dqt = intra(lambda h, sb: _dot(da[h][sb], hs(kt, h, sb), NN))
        dkt = intra(lambda h, sb: _dot(da[h][sb], hs(qt, h, sb), TN))
        feed = [[_dot(_chunk(_head(do_b, h), c), _chunk(_head(qg, h), c), TN) for c in CHUNKS] for h in HEADS]
        for h in HEADS:
            dst = dstate[h]
            for c in reversed(CHUNKS):
                dst_buf[h, c * LANES:(c + 1) * LANES, :] = dst
                dst = dst * _head(dec, h)[c * HGRN_CHUNK:c * HGRN_CHUNK + 1, :] + feed[h][c]
            dstate[h] = dst

        def per_chunk(fn):
            cols = []
            for h in HEADS:
                rows = [jnp.broadcast_to(t, (HGRN_CHUNK, HGRN_DIM)) for t in (fn(h, c) for c in CHUNKS)]
                cols.append(jnp.concatenate(rows, axis=0))
            return jnp.concatenate(cols, axis=1)

        def st_prev(h, c):
            return st_ref[h, c * LANES:(c + 1) * LANES, :]

        def dst_at(h, c):
            return dst_buf[h, c * LANES:(c + 1) * LANES, :]

        dqg = per_chunk(lambda h, c: _dot(_chunk(_head(do_b, h), c), st_prev(h, c).astype(BF), NN))
        dkout = per_chunk(lambda h, c: _dot(_chunk(_head(v, h), c), dst_at(h, c).astype(BF), NN))
        dv_inter = per_chunk(lambda h, c: _dot(_chunk(_head(kout, h), c), dst_at(h, c).astype(BF), NT))
        dg_state = per_chunk(lambda h, c: jnp.sum(dst_at(h, c) * st_prev(h, c).astype(F32), axis=0, keepdims=True))
        dg_kout = per_chunk(lambda h, c: jnp.sum(_chunk(_head(dkout * pre["kout"], h), c), axis=0, keepdims=True))
        dv = dv_intra + dv_inter
        pos = lax.broadcasted_iota(jnp.int32, (HGRN_ROWS, HGRN_W), 0) % HGRN_CHUNK
        dq = dqt * pre["e_q"] + dqg * pre["e_in"]
        dk = dkt * pre["e_k"] + dkout * pre["e_out"]
        dg = (dqt * pre["qt"] - dkt * pre["kt"] + dqg * pre["qg"] - dkout * pre["kout"]
              + jnp.where(pos == HGRN_CHUNK - 1, dg_state * dec + dg_kout, 0.0))
        dlogf = _chunk_cumsum(dg, reverse=True)
        sig, sq = pre["sig"], pre["sq"]
        df = dlogf / pre["f"] - dk
        dseg_ref[0] = (dq * HGRN_SCALE * (sq * (1.0 + qh * (1.0 - sq)))).astype(BF)
        dseg_ref[1] = (df * (1.0 - lbv) * sig * (1.0 - sig)).astype(BF)
        dseg_ref[2] = dv.astype(BF)
        dseg_ref[3] = dgate_in.astype(BF)
        _accumulate_rows(step, dlb_ref, _part8(df * (1.0 - sig)))
        _accumulate_rows(step, dgain_ref, _part8(dgain))

    specs, blk = _hgrn_in_specs(layer, True, nblk)
    specs += [pl.BlockSpec((HGRN_ROWS, HGRN_W), lambda b: (blk(b), 0)),
              pl.BlockSpec((HGRN_ROWS, HGRN_W), lambda b: (blk(b), 1)),
              pl.BlockSpec((HGRN_HEADS, cpb * LANES, LANES), lambda b: (0, blk(b), 0))]
    return pl.pallas_call(
        body, grid=(nblk,), in_specs=specs,
        out_specs=[pl.BlockSpec((4, HGRN_ROWS, HGRN_W), lambda b: (0, blk(b), 0)),
                   pl.BlockSpec((8, HGRN_W), lambda b: (0, 0)), pl.BlockSpec((8, HGRN_DIM), lambda b: (0, 0))],
        out_shape=[jax.ShapeDtypeStruct((4, s, HGRN_W), BF), jax.ShapeDtypeStruct((8, HGRN_W), F32),
                   jax.ShapeDtypeStruct((8, HGRN_DIM), F32)],
        scratch_shapes=[pltpu.VMEM((HGRN_HEADS, LANES, LANES), F32), pltpu.VMEM((HGRN_HEADS, cpb * LANES, LANES), F32)],
        compiler_params=_cparams(("arbitrary",)), name=f"hgrn_bwd_l{layer}",
    )(proj, proj, proj, proj, lb, gain, o, dmixed, states)


def _bwd_inproj(layer, dqkv, dhg, g_in, x, gain, dres, tm=512):
    s, d = x.shape

    def body(dqkv_ref, dhg_ref, w_hbm, x_ref, gain_ref, dres_ref, dx_ref, dxb_ref, dgain_ref, w_ref, sems):
        ready = _weights_arrive(pl.program_id(0), [(w_hbm.at[0, pl.ds(seg * SEG, SEG), :], w_ref.at[pl.ds(seg * SEG, SEG), :])
                                                   for seg in range(N_SEG)], sems)
        acc = jnp.zeros((tm, d), F32)
        for seg in range(N_SEG):
            ready(seg)
            a = dqkv_ref[seg] if seg < 3 else dhg_ref[seg - 3]
            acc = acc + _dot(a, w_ref[seg * SEG:(seg + 1) * SEG, :], NN)
        dx, dgain = _rms_bwd(acc, x_ref[...], gain_ref[...])
        out = dres_ref[...] + dx
        dx_ref[...] = out
        dxb_ref[...] = out.astype(BF)
        _accumulate_rows(pl.program_id(0), dgain_ref, _part8(dgain))

    row = pl.BlockSpec((tm, d), lambda i: (i, 0))
    return pl.pallas_call(
        body, grid=(s // tm,),
        in_specs=[pl.BlockSpec((3, tm, SEG), lambda i: (0, i, 0)), pl.BlockSpec((4, tm, SEG), lambda i: (0, i, 0)),
                  ANY_SPEC, row,
                  pl.BlockSpec((None, 1, d), lambda i: (layer, 0, 0)), row],
        out_specs=[row, row, pl.BlockSpec((8, d), lambda i: (0, 0))],
        out_shape=[jax.ShapeDtypeStruct((s, d), F32), jax.ShapeDtypeStruct((s, d), BF), jax.ShapeDtypeStruct((8, d), F32)],
        scratch_shapes=[pltpu.VMEM((PROJ_W, d), BF), pltpu.SemaphoreType.DMA((N_SEG,))],
        compiler_params=_cparams(("arbitrary",)), name=f"bwd_inproj_l{layer}",
    )(dqkv, dhg, g_in, x, gain, dres)


def _adamw(w, g, m, v):
    m2 = ADAM_B1 * m + (1.0 - ADAM_B1) * g
    v2 = ADAM_B2 * v + (1.0 - ADAM_B2) * (g * g)
    m_hat = m2 / (1.0 - ADAM_B1 ** ADAM_STEP)
    v_hat = v2 / (1.0 - ADAM_B2 ** ADAM_STEP)
    delta = -ADAM_LR * (m_hat / (jnp.sqrt(v_hat) + ADAM_EPS) + ADAM_WD * w)
    return delta, m2, v2


def _adam_big(name, parts, w, m, v, row_tiles):
    depth = w.shape[0]
    r, c = parts[0].shape[1], parts[0].shape[2]
    tr = r // row_tiles
    p_spec = pl.BlockSpec((N_DEV, tr, c), lambda t: (0, t, 0))
    w_spec = pl.BlockSpec((depth, tr, c), lambda t: (0, t, 0))

    def body(*refs):
        p_refs = refs[:depth]
        w_ref, m_ref, v_ref, g_ref, d_ref, m2_ref, v2_ref = refs[depth:]
        for l in range(depth):
            g = p_refs[l][0].astype(F32)
            for dev in range(1, N_DEV):
                g = g + p_refs[l][dev].astype(F32)
            delta, m2, v2 = _adamw(w_ref[l], g, m_ref[l], v_ref[l])
            g_ref[l] = g
            d_ref[l] = delta
            m2_ref[l] = m2
            v2_ref[l] = v2

    return pl.pallas_call(
        body, grid=(row_tiles,), in_specs=[p_spec] * depth + [w_spec] * 3, out_specs=[w_spec] * 4,
        out_shape=[jax.ShapeDtypeStruct(w.shape, F32)] * 4,
        compiler_params=_cparams(("parallel",)), name=name,
    )(*parts, w, m, v)


def _adam_small(g, w, m, v):
    def body(g_ref, w_ref, m_ref, v_ref, d_ref, m2_ref, v2_ref):
        delta, m2, v2 = _adamw(w_ref[...], g_ref[...], m_ref[...], v_ref[...])
        d_ref[...] = delta
        m2_ref[...] = m2
        v2_ref[...] = v2

    vm = pl.BlockSpec(memory_space=pltpu.VMEM)
    return pl.pallas_call(body, in_specs=[vm] * 4, out_specs=[vm] * 3, out_shape=[jax.ShapeDtypeStruct(g.shape, F32)] * 3,
                          name="adam_small")(g, w, m, v)


def _lower_bounds(logits):
    def body(l_ref, lb_ref, jac_ref):
        l0, l1 = l_ref[0:1, :], l_ref[1:2, :]
        mx = jnp.maximum(l0, l1)
        e0, e1 = jnp.exp(l0 - mx), jnp.exp(l1 - mx)
        p0, p1 = e0 / (e0 + e1), e1 / (e0 + e1)
        lb_ref[0:1, :] = p0 - p0
        lb_ref[1:2, :] = (p0 + p1) - p0
        jac_ref[0:1, :] = -p0 * p1
        jac_ref[1:2, :] = p0 * p1

    vm = pl.BlockSpec(memory_space=pltpu.VMEM)
    return pl.pallas_call(body, in_specs=[vm], out_specs=[vm, vm], out_shape=[jax.ShapeDtypeStruct(logits.shape, F32)] * 2,
                          name="hgrn_lower_bounds")(logits)


def _rope_tables(s):
    half = 32
    inv_freq = ROPE_THETA ** (-jnp.arange(half, dtype=F32) / half)
    ang = jnp.arange(s, dtype=jnp.int32).astype(F32)[:, None] * inv_freq[None, :]
    cos, sin = jnp.cos(ang), jnp.sin(ang)
    return jnp.concatenate([cos] * 4, axis=1), jnp.concatenate([-sin, sin, -sin, sin], axis=1)


SMALL_NAMES = ("norm_mix", "attn_out_gain", "hgrn_lb_logits", "hgrn_out_gain", "norm_mlp", "norm_final")


def _pack_small(vals):
    flat = jnp.concatenate([v.reshape(-1) for v in vals])
    rows = -(-flat.shape[0] // (8 * LANES)) * 8
    return jnp.pad(flat, (0, rows * LANES - flat.shape[0])).reshape(rows, LANES)


def _unpack_small(packed, like):
    flat, out, off = packed.reshape(-1), [], 0
    for v in like:
        out.append(flat[off:off + v.size].reshape(v.shape))
        off += v.size
    return out


def kernel(x, norm_mix, w_in, attn_out_gain, hgrn_lb_logits, hgrn_out_gain, w_out, norm_mlp, w_up, w_down, norm_final, loss_target, m_norm_mix, m_w_in, m_attn_out_gain, m_hgrn_lb_logits, m_hgrn_out_gain, m_w_out, m_norm_mlp, m_w_up, m_w_down, m_norm_final, v_norm_mix, v_w_in, v_attn_out_gain, v_hgrn_lb_logits, v_hgrn_out_gain, v_w_out, v_norm_mlp, v_w_up, v_w_down, v_norm_final):
    depth = w_in.shape[0]
    assert depth == 2 and x.shape[0] == 1
    s, d = x.shape[1], x.shape[2]
    x0 = x[0]
    target = loss_target[0]
    cos, sin = _rope_tables(s)
    g_mix, g_attn, g_hg, g_mlp = (norm_mix[:, None, :], attn_out_gain[:, None, :], hgrn_out_gain[:, None, :],
                                  norm_mlp[:, None, :])
    lb, lb_jac = _lower_bounds(hgrn_lb_logits)
    lb3 = lb[:, None, :]

    def flip(a):
        return jnp.swapaxes(a, 1, 2)

    shards = list(_pack_weights(flip(w_in), w_out, w_up, w_down))
    w_pieces = _weight_pieces(*shards)
    w_groups = [[0], [1, 2, 3], [4], [5, 6, 7]]
    me = (4 * lax.axis_index("x") + 2 * lax.axis_index("y") + lax.axis_index("c")).astype(jnp.int32).reshape(1)
    lands = _exchange_own("all_gather_own", me, shards, w_pieces)
    w_sems, shards, lands, token = _exchange_start("all_gather_start", shards, lands, w_pieces, w_groups)

    def weights_ready(group, after):
        nonlocal shards
        idxs = w_groups[group]
        shards, got = _exchange_wait(f"all_gather_wait{group}", shards, [lands[i] for i in idxs], w_pieces,
                                     [(idxs, *w_sems[group])], after)
        return got

    def tied(small_arr, tok):
        return small_arr + tok[0, 0]

    saved = []
    xl = x0
    full = [None] * depth
    for l in range(depth):
        (full_in,) = weights_ready(2 * l, token if l == 0 else xl)
        proj, h = _fwd_inproj(l, xl, g_mix, full_in, cos, sin)
        o_attn, lse = _attn_fwd(l, proj)
        o_hg, mixed, states = _hgrn_fwd(l, proj, lb3, g_hg)
        mixed = _attn_norm(l, o_attn, g_attn, mixed)
        full_out, full_up, full_down = weights_ready(2 * l + 1, mixed)
        x_next, x_mid, u, h2 = _mlp_fwd(l, xl, mixed, g_mlp, full_out, full_up, full_down)
        saved.append((xl, proj, h, o_attn, lse, o_hg, states, mixed, x_mid, u, h2))
        full[l] = (full_in, full_out, full_up, full_down)
        xl = x_next
    dx, dxb, dnorm_final8, loss8 = _loss_head(xl, norm_final[None, :], target)
    loss = lax.psum(jnp.sum(loss8[:, 0]), ("x", "y", "c"))

    exchanges = []

    def scatter(tag, grads, kinds):
        pieces = _grad_pieces(grads, kinds)
        own = _exchange_own(f"reduce_scatter_own_{tag}", me, grads, pieces)
        sems, grads, own, tok = _exchange_start(f"reduce_scatter_start_{tag}", grads, own, pieces, [list(range(len(pieces)))])
        exchanges.append((grads, own, pieces, sems[0]))
        return tok

    small = {}
    for l in reversed(range(depth)):
        xl, proj, h, o_attn, lse, o_hg, states, mixed, x_mid, u, h2 = saved[l]
        full_in, full_out, full_up, full_down = full[l]
        hs = full_up.shape[3]
        gw_down = _mm_tn(f"grad_w_down_l{l}", u, dxb, u.shape[1], a_fn=_relu2)
        dx_mid, dx_mid_b, du, dmixed, dmlp8 = _mlp_bwd(l, dx, dxb, x_mid, g_mlp, u, full_out, full_up, full_down)
        gw_up = _mm_tn(f"grad_w_up_l{l}", h2, du, d, out_block_w=hs)
        gw_out = _mm_tn(f"grad_w_out_l{l}", mixed, dx_mid_b, mixed.shape[1])
        g_attn_t = tied(g_attn, scatter(f"mlp_l{l}", [gw_down, gw_up, gw_out], ["rows", "up", "rows"]))
        do, delta, dattn8 = _attn_norm_bwd(l, dmixed, o_attn, g_attn_t)
        dqkv = _attn_bwd(l, proj, do, lse, delta, cos, sin)
        dhg, dlb8, dhgain8 = _hgrn_bwd(l, proj, lb3, g_hg, o_hg, dmixed, states)
        gin = _mm_tn(f"grad_w_in_qkv_l{l}", dqkv, h, PROJ_W, a_lead=True)
        gw_in = _mm_tn(f"grad_w_in_hg_l{l}", dhg, h, PROJ_W, a_lead=True, out_block_off=3, prev=gin)
        g_mix_t = tied(g_mix, scatter(f"mix_l{l}", [gw_in], ["rows"]))
        dx, dxb, dmix8 = _bwd_inproj(l, dqkv, dhg, full_in, xl, g_mix_t, dx_mid)
        small[l] = (dmix8, dattn8, dlb8, dhgain8, dmlp8)

    def fin(p8):
        return jnp.sum(p8, axis=0)
    dlogits = lb_jac * fin(small[1][2])[None, :]
    small_grads = [jnp.stack([fin(small[l][0]) for l in range(depth)]), jnp.stack([fin(small[l][1]) for l in range(depth)]),
                   dlogits, jnp.stack([fin(small[l][3]) for l in range(depth)]),
                   jnp.stack([fin(small[l][4]) for l in range(depth)]), fin(dnorm_final8)]
    small_w = [norm_mix, attn_out_gain, hgrn_lb_logits, hgrn_out_gain, norm_mlp, norm_final]
    small_m = [m_norm_mix, m_attn_out_gain, m_hgrn_lb_logits, m_hgrn_out_gain, m_norm_mlp, m_norm_final]
    small_v = [v_norm_mix, v_attn_out_gain, v_hgrn_lb_logits, v_hgrn_out_gain, v_norm_mlp, v_norm_final]
    g_small = _all_reduce_small(_pack_small(small_grads))
    d_small, m_small, v_small = _adam_small(g_small, _pack_small(small_w), _pack_small(small_m), _pack_small(small_v))
    gs, ds, ms, vs = (_unpack_small(t, small_w) for t in (g_small, d_small, m_small, v_small))

    all_grads, all_lands, all_pieces, waits = [], [], [], []
    for grads, own, pieces, (send, recv) in exchanges:
        first = len(all_pieces)
        all_pieces += [p._replace(src=p.src + len(all_grads)) for p in pieces]
        waits.append((list(range(first, first + len(pieces))), send, recv))
        all_grads += grads
        all_lands += own
    _, landed = _exchange_wait("reduce_scatter_wait", all_grads, all_lands, all_pieces, waits, d_small)
    r_down, r_up, r_out, r_in = ([landed[4 + i], landed[i]] for i in range(4))
    big = {
        "w_in": [flip(t) for t in _adam_big("adam_w_in", r_in, flip(w_in), flip(m_w_in), flip(v_w_in), 2)],
        "w_out": _adam_big("adam_w_out", r_out, w_out, m_w_out, v_w_out, 1),
        "w_up": _adam_big("adam_w_up", r_up, w_up, m_w_up, v_w_up, 2),
        "w_down": _adam_big("adam_w_down", r_down, w_down, m_w_down, v_w_down, 4),
    }

    def gather(idx, small_list):
        by_name = dict(zip(SMALL_NAMES, small_list))
        return [by_name["norm_mix"], big["w_in"][idx], by_name["attn_out_gain"], by_name["hgrn_lb_logits"],
                by_name["hgrn_out_gain"], big["w_out"][idx], by_name["norm_mlp"], big["w_up"][idx], big["w_down"][idx],
                by_name["norm_final"]]

    return (loss, dx[None], *gather(0, gs), *gather(1, ds), *gather(2, ms), *gather(3, vs))
```

```python
import functools
from typing import Callable, NamedTuple

import jax
import jax.numpy as jnp
from jax import lax
from jax.experimental import pallas as pl
from jax.experimental.pallas import tpu as pltpu

F32 = jnp.float32
BF = jnp.bfloat16

N_DEV = 8
ATTN_W = 512
HGRN_W = 512
HGRN_HEADS = 4
HGRN_DIM = 128
SEG = 512
N_SEG = 7
PROJ_W = N_SEG * SEG
MIX_W = ATTN_W + HGRN_W
SPAN = 128
DILATIONS = (1, 4, 16)
HGRN_CHUNK = 16
ROPE_THETA = 10000.0
NORM_EPS = 1e-6
MASK_VALUE = -1e30
ATTN_SCALE = 0.125
HGRN_SCALE = HGRN_DIM ** -0.5
ADAM_LR = 0.001
ADAM_B1 = 0.9
ADAM_B2 = 0.999
ADAM_EPS = 1e-08
ADAM_WD = 0.01
ADAM_STEP = 10
LANES = 128
VMEM_LIMIT = 56 * 1024 * 1024

NN = ((1,), (0,))
NT = ((1,), (1,))
TN = ((0,), (0,))
MESH = pl.DeviceIdType.MESH


def _dot(a, b, dims):
    return lax.dot_general(a, b, (dims, ((), ())), preferred_element_type=F32)


def _cparams(sem):
    return pltpu.CompilerParams(dimension_semantics=sem, vmem_limit_bytes=VMEM_LIMIT)


def _part8(x):
    r, n = x.shape
    return jnp.sum(x.reshape(r // 8, 8, n), axis=0)


def _sigmoid(x):
    return 1.0 / (1.0 + jnp.exp(-x))


def _rms_fwd(x, gain):
    r = lax.rsqrt(jnp.mean(x * x, axis=-1, keepdims=True) + NORM_EPS)
    return x * r * gain


def _rms_bwd(dy, x, gain):
    r = lax.rsqrt(jnp.mean(x * x, axis=-1, keepdims=True) + NORM_EPS)
    xn = x * r
    dxn = dy * gain
    dx = r * (dxn - xn * jnp.mean(dxn * xn, axis=-1, keepdims=True))
    return dx, dy * xn


def _rope_partner(x):
    n = x.shape[-1]
    lane = lax.broadcasted_iota(jnp.int32, x.shape, x.ndim - 1)
    return jnp.where((lane % 64) < 32, pltpu.roll(x, n - 32, x.ndim - 1), pltpu.roll(x, 32, x.ndim - 1))


def _tile_lanes(t, reps):
    return jnp.concatenate([t] * reps, axis=-1)


def _mm_tn(name, a, b, out_rows, a_lead=False, out_block_off=0, prev=None, out_block_w=None, a_fn=None,
           tm=512, tn=1024, sub=512):
    kdim, n = b.shape
    m = a.shape[-1]
    tm, tn, sub = min(tm, m), min(tn, n), min(sub, kdim)
    mt = m // tm
    n_lead = a.shape[0] if a_lead else 1
    if a_lead:
        a_spec = pl.BlockSpec((None, kdim, tm), lambda j, i: (i // mt, 0, i % mt))
    else:
        a_spec = pl.BlockSpec((kdim, tm), lambda j, i: (0, i))
    b_spec = pl.BlockSpec((kdim, tn), lambda j, i: (0, j))
    if out_block_w:
        nb = tn // out_block_w
        o_shape = jax.ShapeDtypeStruct((n // out_block_w, out_rows, out_block_w), BF)
        o_spec = pl.BlockSpec((nb, tm, out_block_w), lambda j, i: (j, i + out_block_off, 0))
    else:
        nb = 0
        o_shape = jax.ShapeDtypeStruct((out_rows, n), BF)
        o_spec = pl.BlockSpec((tm, tn), lambda j, i: (i + out_block_off, j))
    arrays, specs, aliases = [a, b], [a_spec, b_spec], {}
    if prev is not None:
        arrays.append(prev)
        specs.append(pl.BlockSpec(memory_space=pl.ANY))
        aliases = {2: 0}

    def body(*refs):
        a_ref, b_ref, o_ref = refs[0], refs[1], refs[-1]
        acc = None
        for k in range(kdim // sub):
            av = a_ref[k * sub:(k + 1) * sub, :]
            if a_fn is not None:
                av = a_fn(av)
            part = _dot(av, b_ref[k * sub:(k + 1) * sub, :], TN)
            acc = part if acc is None else acc + part
        if nb:
            for t in range(nb):
                o_ref[t] = acc[:, t * out_block_w:(t + 1) * out_block_w].astype(BF)
        else:
            o_ref[...] = acc.astype(BF)

    return pl.pallas_call(
        body, grid=(n // tn, n_lead * mt), in_specs=specs, out_specs=o_spec, out_shape=o_shape,
        compiler_params=_cparams(("parallel", "parallel")), name=name, input_output_aliases=aliases,
    )(*arrays)


def _pack_weights(w_in_t, w_out, w_up, w_down):
    depth = w_in_t.shape[0]
    arrays = (w_in_t, w_out, w_up, w_down)

    def body(*refs):
        for src, dst in zip(refs[:4], refs[4:]):
            dst[...] = src[...].astype(BF)

    specs = [pl.BlockSpec((None,) + a.shape[1:], lambda l: (l, 0, 0)) for a in arrays]
    return pl.pallas_call(
        body, grid=(depth,), in_specs=specs, out_specs=specs,
        out_shape=[jax.ShapeDtypeStruct(a.shape, BF) for a in arrays],
        compiler_params=_cparams(("arbitrary",)), name="pack_weights",
    )(*arrays)


def _my_position():
    x, y, c = lax.axis_index("x"), lax.axis_index("y"), lax.axis_index("c")
    return x, y, c, 4 * x + 2 * y + c


def _peer(x, y, c, k):
    px = 1 - x if k & 4 else x
    py = 1 - y if k & 2 else y
    pc = 1 - c if k & 1 else c
    return (px, py, pc), 4 * px + 2 * py + pc


PEER_ORDER = (1, 2, 4, 3, 5, 6, 7)


class _Piece(NamedTuple):
    src: int
    send: Callable
    slot: Callable
    land_shape: tuple
    own_src: tuple
    own_slot: tuple


HBM_SPEC = pl.BlockSpec(memory_space=pltpu.HBM)
SEM_SPEC = pl.BlockSpec(memory_space=pltpu.SEMAPHORE)
ANY_SPEC = pl.BlockSpec(memory_space=pl.ANY)


def _in_hbm(arrays):
    return [pltpu.with_memory_space_constraint(a, pltpu.HBM) for a in arrays]


def _hbm_like(arrays):
    return [pltpu.HBM(a.shape, a.dtype) for a in arrays]


def _rows_of(rows):
    return lambda ref, dev: ref.at[pl.ds(pl.multiple_of(dev * rows, 16), rows), :]


def _exchange_own(name, me, srcs, pieces):
    n = len(pieces)

    def body(me_ref, *refs):
        for i in range(n):
            refs[n + i][...] = refs[i][...]

    def spec(block_and_index):
        block, index = block_and_index
        return pl.BlockSpec(block, lambda i, me_ref: index(me_ref[0]))

    return pl.pallas_call(
        body,
        grid_spec=pltpu.PrefetchScalarGridSpec(
            num_scalar_prefetch=1, grid=(1,), in_specs=[spec(p.own_src) for p in pieces],
            out_specs=[spec(p.own_slot) for p in pieces]),
        out_shape=[jax.ShapeDtypeStruct(p.land_shape, BF) for p in pieces],
        compiler_params=_cparams(("arbitrary",)), name=name,
    )(me, *[srcs[p.src] for p in pieces])


def _exchange_start(name, srcs, lands, pieces, groups):
    n_src, n, n_g = len(srcs), len(pieces), len(groups)

    def body(*refs):
        src_refs, land_refs = refs[:n_src], refs[n_src:n_src + n]
        sems, token = refs[n_src + n:n_src + n + 2 * n_g], refs[-1]
        x, y, c, me = _my_position()
        for g, idxs in enumerate(groups):
            for k in PEER_ORDER:
                peer, pid = _peer(x, y, c, k)
                for j, i in enumerate(idxs):
                    p = pieces[i]
                    pltpu.make_async_remote_copy(
                        src_ref=p.send(src_refs[p.src], pid), dst_ref=p.slot(land_refs[i], me),
                        send_sem=sems[2 * g].at[(k - 1) * len(idxs) + j], recv_sem=sems[2 * g + 1].at[(k - 1) * len(idxs) + j],
                        device_id=peer, device_id_type=MESH).start()
        token[...] = jnp.zeros(token.shape, F32)

    sem_shapes = [pltpu.SemaphoreType.DMA(((N_DEV - 1) * len(idxs),)) for idxs in groups for _ in range(2)]
    res = pl.pallas_call(
        body, in_specs=[HBM_SPEC] * (n_src + n),
        out_specs=[SEM_SPEC] * (2 * n_g) + [HBM_SPEC] * (n_src + n) + [pl.BlockSpec(memory_space=pltpu.VMEM)],
        out_shape=sem_shapes + _hbm_like(srcs) + _hbm_like(lands) + [jax.ShapeDtypeStruct((8, LANES), F32)],
        input_output_aliases={i: 2 * n_g + i for i in range(n_src + n)},
        compiler_params=pltpu.CompilerParams(has_side_effects=pltpu.SideEffectType.DATAFLOW_SIDE_EFFECTING),
        name=name,
    )(*_in_hbm(srcs), *_in_hbm(lands))
    sems = [(res[2 * g], res[2 * g + 1]) for g in range(n_g)]
    return sems, list(res[2 * n_g:2 * n_g + n_src]), list(res[2 * n_g + n_src:2 * n_g + n_src + n]), res[-1]


def _exchange_wait(name, srcs, lands, pieces, waits, after):
    n_src, n, n_g = len(srcs), len(lands), len(waits)

    def body(*refs):
        src_refs, land_refs = refs[:n_src], refs[n_src:n_src + n]
        sems = refs[n_src + n:n_src + n + 2 * n_g]
        x, y, c, me = _my_position()
        at = 0
        for g, (idxs, _, _) in enumerate(waits):
            for k in PEER_ORDER:
                peer, pid = _peer(x, y, c, k)
                for j, i in enumerate(idxs):
                    p = pieces[i]
                    cp = pltpu.make_async_remote_copy(
                        src_ref=p.send(src_refs[p.src], pid), dst_ref=p.slot(land_refs[at + j], pid),
                        send_sem=sems[2 * g].at[(k - 1) * len(idxs) + j], recv_sem=sems[2 * g + 1].at[(k - 1) * len(idxs) + j],
                        device_id=peer, device_id_type=MESH)
                    cp.wait_send()
                    cp.wait_recv()
            at += len(idxs)

    sem_args = [s for _, send, recv in waits for s in (send, recv)]
    res = pl.pallas_call(
        body, in_specs=[HBM_SPEC] * (n_src + n) + [SEM_SPEC] * (2 * n_g) + [ANY_SPEC],
        out_specs=[HBM_SPEC] * (n_src + n), out_shape=_hbm_like(srcs) + _hbm_like(lands),
        input_output_aliases={i: i for i in range(n_src + n)},
        compiler_params=pltpu.CompilerParams(has_side_effects=pltpu.SideEffectType.DATAFLOW_SIDE_EFFECTING),
        name=name,
    )(*srcs, *lands, *sem_args, after)
    return list(res[:n_src]), list(res[n_src:])


def _weight_pieces(p_in, p_out, p_up, p_down):
    depth, cin, d = p_in.shape
    rout, hs = p_out.shape[1], p_up.shape[2]
    pieces = []
    for l in range(depth):
        whole = functools.partial(lambda ref, dev, l: ref.at[l], l=l)
        layer = functools.partial(lambda dev, l: (l, 0, 0), l=l)

        def rows(src, n_rows, whole=whole, layer=layer):
            return _Piece(src, whole, lambda ref, dev: _rows_of(n_rows)(ref.at[0], dev), (1, N_DEV * n_rows, d),
                          ((None, n_rows, d), layer), ((None, n_rows, d), lambda dev: (0, dev, 0)))

        pieces += [
            rows(0, cin), rows(1, rout),
            _Piece(2, whole, lambda ref, dev: ref.at[0, dev], (1, N_DEV, d, hs),
                   ((None, d, hs), layer), ((None, None, d, hs), lambda dev: (0, dev, 0, 0))),
            rows(3, hs),
        ]
    return pieces


def _grad_pieces(g_pair, kinds):
    pieces = []
    for i, (g, kind) in enumerate(zip(g_pair, kinds)):
        lead = lambda dev: (dev, 0, 0)
        if kind == "up":
            blk = ((None,) + g.shape[1:], lead)
            pieces.append(_Piece(i, lambda ref, dev: ref.at[dev], lambda ref, dev: ref.at[dev], g.shape, blk, blk))
        else:
            rows, cols = g.shape[0] // N_DEV, g.shape[1]
            pieces.append(_Piece(i, _rows_of(rows), lambda ref, dev: ref.at[dev], (N_DEV, rows, cols),
                                 ((rows, cols), lambda dev: (dev, 0)), ((None, rows, cols), lead)))
    return pieces


def _all_reduce_small(vec):
    rows = vec.shape[0]

    def body(v_ref, o_ref, buf_ref, send_sems, recv_sems):
        x, y, c, me = _my_position()
        buf_ref[me] = v_ref[...]
        sends = []
        for k in PEER_ORDER:
            peer, _ = _peer(x, y, c, k)
            cp = pltpu.make_async_remote_copy(src_ref=v_ref, dst_ref=buf_ref.at[me], send_sem=send_sems.at[k - 1],
                                              recv_sem=recv_sems.at[k - 1], device_id=peer, device_id_type=MESH)
            cp.start()
            sends.append(cp)
        for k in PEER_ORDER:
            peer, pid = _peer(x, y, c, k)
            pltpu.make_async_remote_copy(src_ref=v_ref, dst_ref=buf_ref.at[pid], send_sem=send_sems.at[k - 1],
                                         recv_sem=recv_sems.at[k - 1], device_id=peer, device_id_type=MESH).wait_recv()
        for cp in sends:
            cp.wait_send()
        total = buf_ref[0]
        for dev in range(1, N_DEV):
            total = total + buf_ref[dev]
        o_ref[...] = total

    vm = pl.BlockSpec(memory_space=pltpu.VMEM)
    return pl.pallas_call(
        body, in_specs=[vm], out_specs=vm, out_shape=jax.ShapeDtypeStruct(vec.shape, F32),
        scratch_shapes=[pltpu.VMEM((N_DEV, rows, LANES), F32), pltpu.SemaphoreType.DMA((N_DEV - 1,)),
                        pltpu.SemaphoreType.DMA((N_DEV - 1,))],
        name="all_reduce_small",
    )(vec)


def _resident(block_shape, index_map):
    return pl.BlockSpec(block_shape, index_map, pipeline_mode=pl.Buffered(1))


def _fwd_inproj(layer, x, gain, g_in, cos, sin, tm=512):
    s, d = x.shape

    def body(x_ref, gain_ref, w_ref, cos_ref, sin_ref, proj_ref, h_ref):
        h = _rms_fwd(x_ref[...], gain_ref[...]).astype(BF)
        h_ref[...] = h
        cs = _tile_lanes(cos_ref[...], SEG // LANES)
        sn = _tile_lanes(sin_ref[...], SEG // LANES)
        for seg in range(N_SEG):
            acc = _dot(h, w_ref[seg * SEG:(seg + 1) * SEG, :], NT)
            if seg < 2:
                acc = acc * cs + _rope_partner(acc) * sn
            if seg == 0:
                acc = acc * ATTN_SCALE
            proj_ref[:, seg * SEG:(seg + 1) * SEG] = acc

    return pl.pallas_call(
        body, grid=(s // tm,),
        in_specs=[pl.BlockSpec((tm, d), lambda i: (i, 0)), pl.BlockSpec((None, 1, d), lambda i: (layer, 0, 0)),
                  _resident((None, PROJ_W, d), lambda i: (0, 0, 0)),
                  pl.BlockSpec((tm, LANES), lambda i: (i, 0)), pl.BlockSpec((tm, LANES), lambda i: (i, 0))],
        out_specs=[pl.BlockSpec((tm, PROJ_W), lambda i: (i, 0)), pl.BlockSpec((tm, d), lambda i: (i, 0))],
        out_shape=[jax.ShapeDtypeStruct((s, PROJ_W), F32), jax.ShapeDtypeStruct((s, d), BF)],
        compiler_params=_cparams(("parallel",)), name=f"fwd_inproj_l{layer}",
    )(x, gain, g_in, cos, sin)


ATTN_UNIT = SPAN * max(DILATIONS)
ATTN_GROUP = 4


def _attn_masks(first_block_has_prev):
    row = lax.broadcasted_iota(jnp.int32, (SPAN, 2 * SPAN), 0)
    col = lax.broadcasted_iota(jnp.int32, (SPAN, 2 * SPAN), 1)
    band = (col >= row) & (col <= row + SPAN)
    lane = lax.broadcasted_iota(jnp.int32, (SPAN, LANES), 1)
    return band & ((col >= SPAN) | first_block_has_prev), band, lane < 64


def _attn_specs(n_in_extra, unit_of=lambda n: n):
    pairs = ATTN_W // LANES
    q_spec = pl.BlockSpec((ATTN_UNIT, LANES), lambda p, n: (unit_of(n), p))

    def prev(seg):
        return pl.BlockSpec((ATTN_UNIT, LANES), lambda p, n: (jnp.maximum(unit_of(n) - 1, 0), seg * pairs + p))

    def cur(seg):
        return pl.BlockSpec((ATTN_UNIT, LANES), lambda p, n: (unit_of(n), seg * pairs + p))

    return [q_spec, prev(1), cur(1), prev(2), cur(2)] + [q_spec] * n_in_extra


def _attn_groups(dil):
    blocks = ATTN_UNIT // (SPAN * dil)
    pairs = [(r, b) for r in range(dil) for b in range(blocks)]
    return [pairs[i:i + ATTN_GROUP] for i in range(0, len(pairs), ATTN_GROUP)]


def _block_rows(dil, r, b, n=1):
    start = r + dil * SPAN * b
    return pl.ds(start, n * SPAN, stride=dil) if dil > 1 else pl.ds(start, n * SPAN)


def _block_keys(prev_ref, cur_ref, dil, r, b):
    if b > 0:
        return cur_ref[_block_rows(dil, r, b - 1, 2), :]
    last = ATTN_UNIT // (SPAN * dil) - 1
    return jnp.concatenate([prev_ref[_block_rows(dil, r, last), :], cur_ref[_block_rows(dil, r, 0), :]], axis=0)


def _attn_fwd(layer, proj):
    s = proj.shape[0]
    n_pat = len(DILATIONS)
    merge_rows = 256

    def body(q_ref, kp_ref, kc_ref, vp_ref, vc_ref, o_ref, lse_ref, o_scr, lse_scr):
        m_first, m_rest, is_a = _attn_masks(pl.program_id(1) > 0)
        sels = (is_a, jnp.logical_not(is_a))
        is_a_keys = lax.broadcasted_iota(jnp.int32, (2 * SPAN, LANES), 1) < 64
        for pi, dil in enumerate(DILATIONS):
            for group in _attn_groups(dil):
                items = [(r, b, h) for r, b in group for h in range(2)]
                q = {rb: q_ref[_block_rows(dil, *rb), :] for rb in group}
                k = {rb: _block_keys(kp_ref, kc_ref, dil, *rb).astype(BF) for rb in group}
                v = {rb: _block_keys(vp_ref, vc_ref, dil, *rb).astype(BF) for rb in group}
                v_sum = {rb: (jnp.where(is_a_keys, v[rb], 1.0), jnp.where(is_a_keys, 1.0, v[rb])) for rb in group}
                sc = [jnp.where(m_first if b == 0 else m_rest,
                                _dot(jnp.where(sels[h], q[r, b], 0.0).astype(BF), k[r, b], NT), MASK_VALUE)
                      for r, b, h in items]
                mx = [jnp.max(jnp.maximum(t[:, :SPAN], t[:, SPAN:]), axis=-1, keepdims=True) for t in sc]
                p = [jnp.exp(t - m).astype(BF) for t, m in zip(sc, mx)]
                both = [_dot(t, v_sum[r, b][h], NN) for t, (r, b, h) in zip(p, items)]
                for j, (r, b) in enumerate(group):
                    t_a, t_b = both[2 * j], both[2 * j + 1]
                    den = pltpu.roll(jnp.where(is_a, t_b, t_a), 64, 1)
                    o_scr[pi, _block_rows(dil, r, b), :] = jnp.where(is_a, t_a, t_b) / den
                    lse_scr[pi, _block_rows(dil, r, b), :] = jnp.where(is_a, mx[2 * j], mx[2 * j + 1]) + jnp.log(den)
        for c in range(ATTN_UNIT // merge_rows):
            rows = slice(c * merge_rows, (c + 1) * merge_rows)
            ls = [lse_scr[pi, rows, :] for pi in range(n_pat)]
            mx = functools.reduce(jnp.maximum, ls)
            ws = [jnp.exp(l - mx) for l in ls]
            den = functools.reduce(jnp.add, ws)
            o_ref[rows, :] = functools.reduce(jnp.add, [w * o_scr[pi, rows, :] for pi, w in enumerate(ws)]) / den
            lse_ref[rows, :] = mx + jnp.log(den)

    out_spec = pl.BlockSpec((ATTN_UNIT, LANES), lambda p, n: (n, p))
    return pl.pallas_call(
        body, grid=(ATTN_W // LANES, s // ATTN_UNIT), in_specs=_attn_specs(0), out_specs=[out_spec, out_spec],
        out_shape=[jax.ShapeDtypeStruct((s, ATTN_W), F32)] * 2,
        scratch_shapes=[pltpu.VMEM((n_pat, ATTN_UNIT, LANES), F32)] * 2,
        compiler_params=_cparams(("parallel", "arbitrary")), name=f"attn_fwd_l{layer}",
    )(proj, proj, proj, proj, proj)


def _attn_norm(layer, o, gain, mixed, tm=512):
    s = o.shape[0]

    def body(o_ref, gain_ref, mixed_ref, n_ref):
        n_ref[...] = _rms_fwd(o_ref[...], gain_ref[...]).astype(BF)

    blk = pl.BlockSpec((tm, ATTN_W), lambda i: (i, 0))
    return pl.pallas_call(
        body, grid=(s // tm,),
        in_specs=[blk, pl.BlockSpec((None, 1, ATTN_W), lambda i: (layer, 0, 0)), pl.BlockSpec(memory_space=pl.ANY)],
        out_specs=blk, out_shape=jax.ShapeDtypeStruct(mixed.shape, BF), input_output_aliases={2: 0},
        compiler_params=_cparams(("parallel",)), name=f"attn_norm_l{layer}",
    )(o, gain, mixed)


def _chunk_cumsum(x, reverse=False):
    n = x.shape[0]
    pos = lax.broadcasted_iota(jnp.int32, x.shape, 0) % HGRN_CHUNK
    for sh in (1, 2, 4, 8):
        if reverse:
            x = x + jnp.where(pos < HGRN_CHUNK - sh, pltpu.roll(x, n - sh, 0), 0.0)
        else:
            x = x + jnp.where(pos >= sh, pltpu.roll(x, sh, 0), 0.0)
    return x


def _chunk_row(x, row):
    r, n = x.shape
    x3 = x.reshape(r // HGRN_CHUNK, HGRN_CHUNK, n)
    return jnp.broadcast_to(x3[:, row:row + 1, :], x3.shape).reshape(r, n)


def _hgrn_pre(qh, z, lb):
    sig = _sigmoid(z)
    f = lb + (1.0 - lb) * sig
    k = (1.0 - lb) * _sigmoid(-z)
    sq = _sigmoid(qh)
    q = qh * sq * HGRN_SCALE
    g = _chunk_cumsum(jnp.log(f))
    g_mid = _chunk_row(g, HGRN_CHUNK // 2 - 1)
    g_last = _chunk_row(g, HGRN_CHUNK - 1)
    e_q, e_k = jnp.exp(g - g_mid), jnp.exp(g_mid - g)
    e_in, e_out = jnp.exp(g), jnp.exp(g_last - g)
    return dict(sig=sig, f=f, k=k, sq=sq, q=q, g_last=g_last, e_q=e_q, e_k=e_k, e_in=e_in, e_out=e_out,
                qt=q * e_q, kt=k * e_k, qg=q * e_in, kout=k * e_out)


def _hgrn_mask():
    row = lax.broadcasted_iota(jnp.int32, (LANES, LANES), 0)
    col = lax.broadcasted_iota(jnp.int32, (LANES, LANES), 1)
    return (row // HGRN_CHUNK == col // HGRN_CHUNK) & (col <= row)


def _hgrn_in_specs(layer, rev, nblk):
    def blk(b):
        return nblk - 1 - b if rev else b
    first = 3 * ATTN_W // HGRN_W
    specs = [pl.BlockSpec((HGRN_ROWS, HGRN_W), functools.partial(lambda b, seg: (blk(b), first + seg), seg=seg))
             for seg in range(4)]
    specs.append(pl.BlockSpec((None, 1, HGRN_W), lambda b: (layer, 0, 0)))
    specs.append(pl.BlockSpec((None, 1, HGRN_DIM), lambda b: (layer, 0, 0)))
    return specs, blk


def _head(x, h):
    return x[:, h * HGRN_DIM:(h + 1) * HGRN_DIM]


def _chunk(x, c):
    return x[c * HGRN_CHUNK:(c + 1) * HGRN_CHUNK]


def _sub(x, sb):
    return x[sb * LANES:(sb + 1) * LANES]


HGRN_ROWS = 256
HEADS = range(HGRN_HEADS)
SUBS = range(HGRN_ROWS // LANES)
CHUNKS = range(HGRN_ROWS // HGRN_CHUNK)


def _hgrn_fwd(layer, proj, lb, gain):
    s = proj.shape[0]
    nblk = s // HGRN_ROWS
    cpb = len(CHUNKS)

    def body(q_ref, f_ref, i_ref, g_ref, lb_ref, gain_ref, o_ref, rec_ref, st_ref, state):
        @pl.when(pl.program_id(0) == 0)
        def _():
            state[...] = jnp.zeros(state.shape, F32)

        pre = _hgrn_pre(q_ref[...], f_ref[...], lb_ref[...])
        v = i_ref[...].astype(BF)
        qt, kt, qg, kout = (pre[n].astype(BF) for n in ("qt", "kt", "qg", "kout"))
        dec = jnp.exp(pre["g_last"])
        mask = _hgrn_mask()
        a = [[jnp.where(mask, _dot(_sub(_head(qt, h), sb), _sub(_head(kt, h), sb), NT), 0.0).astype(BF) for sb in SUBS]
             for h in HEADS]
        o_intra = [[_dot(a[h][sb], _sub(_head(v, h), sb), NN) for sb in SUBS] for h in HEADS]
        update = [[_dot(_chunk(_head(v, h), c), _chunk(_head(kout, h), c), TN) for c in CHUNKS] for h in HEADS]
        for h in HEADS:
            st = state[h]
            for c in CHUNKS:
                st_ref[h, c * LANES:(c + 1) * LANES, :] = st.astype(BF)
                st = st * _head(dec, h)[c * HGRN_CHUNK:c * HGRN_CHUNK + 1, :] + update[h][c]
            state[h] = st
        inter = [[_dot(_chunk(_head(qg, h), c), st_ref[h, c * LANES:(c + 1) * LANES, :].astype(BF), NT) for c in CHUNKS]
                 for h in HEADS]
        o = [jnp.concatenate(o_intra[h], axis=0) + jnp.concatenate(inter[h], axis=0) for h in HEADS]
        o_ref[...] = jnp.concatenate(o, axis=1)
        gate = g_ref[...]
        normed = jnp.concatenate([_rms_fwd(o[h], gain_ref[...]) for h in HEADS], axis=1)
        rec_ref[...] = (normed * (gate * _sigmoid(gate))).astype(BF)

    specs, _ = _hgrn_in_specs(layer, False, nblk)
    return pl.pallas_call(
        body, grid=(nblk,), in_specs=specs,
        out_specs=[pl.BlockSpec((HGRN_ROWS, HGRN_W), lambda b: (b, 0)), pl.BlockSpec((HGRN_ROWS, HGRN_W), lambda b: (b, 1)),
                   pl.BlockSpec((HGRN_HEADS, cpb * LANES, LANES), lambda b: (0, b, 0))],
        out_shape=[jax.ShapeDtypeStruct((s, HGRN_W), F32), jax.ShapeDtypeStruct((s, MIX_W), BF),
                   jax.ShapeDtypeStruct((HGRN_HEADS, nblk * cpb * LANES, LANES), BF)],
        scratch_shapes=[pltpu.VMEM((HGRN_HEADS, LANES, LANES), F32)],
        compiler_params=_cparams(("arbitrary",)), name=f"hgrn_fwd_l{layer}",
    )(proj, proj, proj, proj, lb, gain)


def _relu2(u):
    return jnp.square(jnp.maximum(u, 0)).astype(BF)


def _mlp_fwd(layer, x, mixed, gain, g_out, g_up, g_down, tm=256):
    s, d = x.shape
    mw = mixed.shape[1]
    nblk, hs = g_up.shape[1], g_up.shape[3]

    def body(x_ref, m_ref, gain_ref, out_w_ref, up_ref, down_ref, o_ref, mid_ref, u_ref, h_ref, a_buf):
        xv = x_ref[...] + _dot(m_ref[...], out_w_ref[...], NN)
        mid_ref[...] = xv
        h = _rms_fwd(xv, gain_ref[...]).astype(BF)
        h_ref[...] = h
        for j in range(nblk):
            u = _dot(h, up_ref[j], NN)
            u_ref[:, j * hs:(j + 1) * hs] = u.astype(BF)
            a_buf[:, j * hs:(j + 1) * hs] = _relu2(u)
        acc = xv
        for j in range(nblk):
            acc = acc + _dot(a_buf[:, j * hs:(j + 1) * hs], down_ref[j * hs:(j + 1) * hs, :], NN)
        o_ref[...] = acc

    row = pl.BlockSpec((tm, d), lambda i: (i, 0))
    return pl.pallas_call(
        body, grid=(s // tm,),
        in_specs=[row, pl.BlockSpec((tm, mw), lambda i: (i, 0)), pl.BlockSpec((None, 1, d), lambda i: (layer, 0, 0)),
                  _resident((None, mw, d), lambda i: (0, 0, 0)),
                  _resident((None, nblk, d, hs), lambda i: (0, 0, 0, 0)),
                  _resident((None, nblk * hs, d), lambda i: (0, 0, 0))],
        out_specs=[row, row, pl.BlockSpec((tm, nblk * hs), lambda i: (i, 0)), row],
        out_shape=[jax.ShapeDtypeStruct((s, d), F32), jax.ShapeDtypeStruct((s, d), F32),
                   jax.ShapeDtypeStruct((s, nblk * hs), BF), jax.ShapeDtypeStruct((s, d), BF)],
        scratch_shapes=[pltpu.VMEM((tm, nblk * hs), BF)],
        compiler_params=_cparams(("parallel",)), name=f"mlp_fwd_l{layer}",
    )(x, mixed, gain, g_out, g_up, g_down)


def _loss_head(x, gain, target, tm=512):
    s, d = x.shape

    def body(x_ref, gain_ref, t_ref, dx_ref, dxb_ref, dgain_ref, loss_ref):
        i = pl.program_id(0)
        xv, gv = x_ref[...], gain_ref[...]
        err = _rms_fwd(xv, gv) - t_ref[...]
        dx, dgain = _rms_bwd(err * (1.0 / d), xv, gv)
        dx_ref[...] = dx
        dxb_ref[...] = dx.astype(BF)
        part = _part8(dgain)
        lpart = _part8(0.5 * jnp.mean(err * err, axis=-1, keepdims=True) * jnp.ones((1, LANES), F32))

        @pl.when(i == 0)
        def _():
            dgain_ref[...] = part
            loss_ref[...] = lpart

        @pl.when(i > 0)
        def _():
            dgain_ref[...] += part
            loss_ref[...] += lpart

    row = pl.BlockSpec((tm, d), lambda i: (i, 0))
    return pl.pallas_call(
        body, grid=(s // tm,),
        in_specs=[row, pl.BlockSpec((1, d), lambda i: (0, 0)), row],
        out_specs=[row, row, pl.BlockSpec((8, d), lambda i: (0, 0)), pl.BlockSpec((8, LANES), lambda i: (0, 0))],
        out_shape=[jax.ShapeDtypeStruct((s, d), F32), jax.ShapeDtypeStruct((s, d), BF), jax.ShapeDtypeStruct((8, d), F32),
                   jax.ShapeDtypeStruct((8, LANES), F32)],
        compiler_params=_cparams(("arbitrary",)), name="loss_head",
    )(x, gain, target)


def _accumulate_rows(i, ref, part):
    @pl.when(i == 0)
    def _():
        ref[...] = part

    @pl.when(i > 0)
    def _():
        ref[...] += part


def _mlp_bwd(layer, dx, dxb, x, gain, u, g_out, g_up, g_down, tm=256):
    s, d = x.shape
    mw = g_out.shape[1]
    nblk, hs = g_up.shape[1], g_up.shape[3]

    def body(dx_ref, dxb_ref, x_ref, gain_ref, u_ref, out_w_ref, up_ref, down_ref, o_ref, ob_ref, du_ref, dm_ref,
             dgain_ref):
        dxb_v = dxb_ref[...]
        for j in range(nblk):
            cols = slice(j * hs, (j + 1) * hs)
            da = _dot(dxb_v, down_ref[cols, :], NT)
            du_ref[:, cols] = (da * (2.0 * jnp.maximum(u_ref[:, cols].astype(F32), 0.0))).astype(BF)
        acc = jnp.zeros((tm, d), F32)
        for j in range(nblk):
            acc = acc + _dot(du_ref[:, j * hs:(j + 1) * hs], up_ref[j], NT)
        dxn, dgain = _rms_bwd(acc, x_ref[...], gain_ref[...])
        out = dx_ref[...] + dxn
        out_b = out.astype(BF)
        o_ref[...] = out
        ob_ref[...] = out_b
        dm_ref[...] = _dot(out_b, out_w_ref[...], NT)
        _accumulate_rows(pl.program_id(0), dgain_ref, _part8(dgain))

    row = pl.BlockSpec((tm, d), lambda i: (i, 0))
    wide = pl.BlockSpec((tm, nblk * hs), lambda i: (i, 0))
    return pl.pallas_call(
        body, grid=(s // tm,),
        in_specs=[row, row, row, pl.BlockSpec((None, 1, d), lambda i: (layer, 0, 0)), wide,
                  _resident((None, mw, d), lambda i: (0, 0, 0)),
                  _resident((None, nblk, d, hs), lambda i: (0, 0, 0, 0)),
                  _resident((None, nblk * hs, d), lambda i: (0, 0, 0))],
        out_specs=[row, row, wide, pl.BlockSpec((tm, mw), lambda i: (i, 0)), pl.BlockSpec((8, d), lambda i: (0, 0))],
        out_shape=[jax.ShapeDtypeStruct((s, d), F32), jax.ShapeDtypeStruct((s, d), BF),
                   jax.ShapeDtypeStruct((s, nblk * hs), BF), jax.ShapeDtypeStruct((s, mw), F32),
                   jax.ShapeDtypeStruct((8, d), F32)],
        compiler_params=_cparams(("arbitrary",)), name=f"mlp_bwd_l{layer}",
    )(dx, dxb, x, gain, u, g_out, g_up, g_down)


def _attn_norm_bwd(layer, dmixed, o, gain, tm=512):
    s = o.shape[0]

    def body(dm_ref, o_ref, gain_ref, do_ref, delta_ref, dgain_ref):
        i = pl.program_id(0)
        ov = o_ref[...]
        do, dgain = _rms_bwd(dm_ref[...], ov, gain_ref[...])
        do_ref[...] = do
        row = lax.broadcasted_iota(jnp.int32, (ATTN_W, ATTN_W), 0)
        col = lax.broadcasted_iota(jnp.int32, (ATTN_W, ATTN_W), 1)
        same_head = jnp.where(row // 64 == col // 64, 1.0, 0.0).astype(BF)
        prod = do * ov
        high = prod.astype(BF)
        low = (prod - high.astype(F32)).astype(BF)
        delta_ref[...] = _dot(high, same_head, NN) + _dot(low, same_head, NN)
        part = _part8(dgain)

        @pl.when(i == 0)
        def _():
            dgain_ref[...] = part

        @pl.when(i > 0)
        def _():
            dgain_ref[...] += part

    blk = pl.BlockSpec((tm, ATTN_W), lambda i: (i, 0))
    return pl.pallas_call(
        body, grid=(s // tm,), in_specs=[blk, blk, pl.BlockSpec((None, 1, ATTN_W), lambda i: (layer, 0, 0))],
        out_specs=[blk, blk, pl.BlockSpec((8, ATTN_W), lambda i: (0, 0))],
        out_shape=[jax.ShapeDtypeStruct((s, ATTN_W), F32), jax.ShapeDtypeStruct((s, ATTN_W), F32),
                   jax.ShapeDtypeStruct((8, ATTN_W), F32)],
        compiler_params=_cparams(("arbitrary",)), name=f"attn_norm_bwd_l{layer}",
    )(dmixed, o, gain)


def _attn_bwd(layer, proj, do, lse, delta, cos, sin):
    s = proj.shape[0]
    n_units = s // ATTN_UNIT
    out_rows = 256

    def unit_of(n):
        return n_units - 1 - n

    def body(q_ref, kp_ref, kc_ref, vp_ref, vc_ref, do_ref, lse_ref, delta_ref, cos_ref, sin_ref, out_ref,
             dq_ref, dk_ref, dkp_ref, dv_ref, dvp_ref, carry_k, carry_v):
        step = pl.program_id(1)
        m_first, m_rest, is_a = _attn_masks(unit_of(step) > 0)
        sels = (is_a, jnp.logical_not(is_a))
        for ref in (dq_ref, dk_ref, dkp_ref, dv_ref, dvp_ref):
            ref[...] = jnp.zeros(ref.shape, F32)
        for dil in DILATIONS:
            last = ATTN_UNIT // (SPAN * dil) - 1
            for group in _attn_groups(dil):
                items = [(r, b, h) for r, b in group for h in range(2)]
                q = {rb: q_ref[_block_rows(dil, *rb), :] for rb in group}
                dov = {rb: do_ref[_block_rows(dil, *rb), :] for rb in group}
                lse_v = {rb: lse_ref[_block_rows(dil, *rb), :] for rb in group}
                delta_v = {rb: delta_ref[_block_rows(dil, *rb), :] for rb in group}
                k = {rb: _block_keys(kp_ref, kc_ref, dil, *rb).astype(BF) for rb in group}
                v = {rb: _block_keys(vp_ref, vc_ref, dil, *rb).astype(BF) for rb in group}
                qh = [jnp.where(sels[h], q[r, b], 0.0).astype(BF) for r, b, h in items]
                doh = [jnp.where(sels[h], dov[r, b], 0.0).astype(BF) for r, b, h in items]
                sc = [jnp.where(m_first if b == 0 else m_rest, _dot(qh[i], k[r, b], NT), MASK_VALUE)
                      for i, (r, b, h) in enumerate(items)]
                p = [jnp.exp(sc[i] - lse_v[r, b][:, 64 * h:64 * h + 1]) for i, (r, b, h) in enumerate(items)]
                ds = [(p[i] * (_dot(doh[i], v[r, b], NT) - delta_v[r, b][:, 64 * h:64 * h + 1])).astype(BF)
                      for i, (r, b, h) in enumerate(items)]
                dv = [_dot(p[i].astype(BF), doh[i], TN) for i in range(len(items))]
                dq = [_dot(ds[i], k[r, b], NN) for i, (r, b, h) in enumerate(items)]
                dk = [_dot(ds[i], qh[i], TN) for i in range(len(items))]
                for j, (r, b) in enumerate(group):
                    own = _block_rows(dil, r, b)
                    dq_ref[own, :] += jnp.where(is_a, dq[2 * j], dq[2 * j + 1])
                    dk2, dv2 = dk[2 * j] + dk[2 * j + 1], dv[2 * j] + dv[2 * j + 1]
                    dk_ref[own, :] += dk2[SPAN:]
                    dv_ref[own, :] += dv2[SPAN:]
                    if b > 0:
                        before = _block_rows(dil, r, b - 1)
                        dk_ref[before, :] += dk2[:SPAN]
                        dv_ref[before, :] += dv2[:SPAN]
                    else:
                        before = _block_rows(dil, r, last)
                        dkp_ref[before, :] += dk2[:SPAN]
                        dvp_ref[before, :] += dv2[:SPAN]
        has_next = step > 0
        for c in range(ATTN_UNIT // out_rows):
            rows = slice(c * out_rows, (c + 1) * out_rows)
            cs, sn = cos_ref[rows, :], sin_ref[rows, :]
            dqv = dq_ref[rows, :]
            dkv = dk_ref[rows, :] + jnp.where(has_next, carry_k[rows, :], 0.0)
            dvv = dv_ref[rows, :] + jnp.where(has_next, carry_v[rows, :], 0.0)
            out_ref[0, rows, :] = ((dqv * cs - _rope_partner(dqv) * sn) * ATTN_SCALE).astype(BF)
            out_ref[1, rows, :] = (dkv * cs - _rope_partner(dkv) * sn).astype(BF)
            out_ref[2, rows, :] = dvv.astype(BF)
        carry_k[...] = dkp_ref[...]
        carry_v[...] = dvp_ref[...]

    tab = pl.BlockSpec((ATTN_UNIT, LANES), lambda p, n: (unit_of(n), 0))
    return pl.pallas_call(
        body, grid=(ATTN_W // LANES, n_units), in_specs=_attn_specs(3, unit_of) + [tab, tab],
        out_specs=pl.BlockSpec((3, ATTN_UNIT, LANES), lambda p, n: (0, unit_of(n), p)),
        out_shape=jax.ShapeDtypeStruct((3, s, ATTN_W), BF),
        scratch_shapes=[pltpu.VMEM((ATTN_UNIT, LANES), F32)] * 7,
        compiler_params=_cparams(("parallel", "arbitrary")), name=f"attn_bwd_l{layer}",
    )(proj, proj, proj, proj, proj, do, lse, delta, cos, sin)


def _hgrn_bwd(layer, proj, lb, gain, o, dmixed, states):
    s = proj.shape[0]
    nblk = s // HGRN_ROWS
    cpb = len(CHUNKS)

    def body(q_ref, f_ref, i_ref, g_ref, lb_ref, gain_ref, o_ref, drec_ref, st_ref, dseg_ref, dlb_ref, dgain_ref,
             dstate, dst_buf):
        step = pl.program_id(0)

        @pl.when(step == 0)
        def _():
            dstate[...] = jnp.zeros(dstate.shape, F32)

        lbv, gv = lb_ref[...], gain_ref[...]
        qh, z, gate_in = q_ref[...], f_ref[...], g_ref[...]
        pre = _hgrn_pre(qh, z, lbv)
        v = i_ref[...].astype(BF)
        sg = _sigmoid(gate_in)
        ov, drec = o_ref[...], drec_ref[...]
        dnormed = drec * (gate_in * sg)
        back = [_rms_bwd(_head(dnormed, h), _head(ov, h), gv) for h in HEADS]
        do_b = jnp.concatenate([b[0] for b in back], axis=1).astype(BF)
        dgain = back[0][1] + back[1][1] + back[2][1] + back[3][1]
        normed = jnp.concatenate([_rms_fwd(_head(ov, h), gv) for h in HEADS], axis=1)
        dgate_in = drec * normed * (sg * (1.0 + gate_in * (1.0 - sg)))
        mask = _hgrn_mask()
        qt, kt, qg, kout = (pre[n].astype(BF) for n in ("qt", "kt", "qg", "kout"))
        dec = jnp.exp(pre["g_last"])
        def intra(fn):
            return jnp.concatenate([jnp.concatenate([fn(h, sb) for sb in SUBS], axis=0) for h in HEADS], axis=1)

        def hs(x, h, sb):
            return _sub(_head(x, h), sb)

        a = [[jnp.where(mask, _dot(hs(qt, h, sb), hs(kt, h, sb), NT), 0.0).astype(BF) for sb in SUBS] for h in HEADS]
        da = [[jnp.where(mask, _dot(hs(do_b, h, sb), hs(v, h, sb), NT), 0.0).astype(BF) for sb in SUBS] for h in HEADS]
        dv_intra = intra(lambda h, sb: _dot(a[h][sb], hs(do_b, h, sb), TN))
        dqt = intra(lambda h, sb: _dot(da[h][sb], hs(kt, h, sb), NN))
        dkt = intra(lambda h, sb: _dot(da[h][sb], hs(qt, h, sb), TN))
        feed = [[_dot(_chunk(_head(do_b, h), c), _chunk(_head(qg, h), c), TN) for c in CHUNKS] for h in HEADS]
        for h in HEADS:
            dst = dstate[h]
            for c in reversed(CHUNKS):
                dst_buf[h, c * LANES:(c + 1) * LANES, :] = dst
                dst = dst * _head(dec, h)[c * HGRN_CHUNK:c * HGRN_CHUNK + 1, :] + feed[h][c]
            dstate[h] = dst

        def per_chunk(fn):
            cols = []
            for h in HEADS:
                rows = [jnp.broadcast_to(t, (HGRN_CHUNK, HGRN_DIM)) for t in (fn(h, c) for c in CHUNKS)]
                cols.append(jnp.concatenate(rows, axis=0))
            return jnp.concatenate(cols, axis=1)

        def st_prev(h, c):
            return st_ref[h, c * LANES:(c + 1) * LANES, :]

        def dst_at(h, c):
            return dst_buf[h, c * LANES:(c + 1) * LANES, :]

        dqg = per_chunk(lambda h, c: _dot(_chunk(_head(do_b, h), c), st_prev(h, c).astype(BF), NN))
        dkout = per_chunk(lambda h, c: _dot(_chunk(_head(v, h), c), dst_at(h, c).astype(BF), NN))
        dv_inter = per_chunk(lambda h, c: _dot(_chunk(_head(kout, h), c), dst_at(h, c).astype(BF), NT))
        dg_state = per_chunk(lambda h, c: jnp.sum(dst_at(h, c) * st_prev(h, c).astype(F32), axis=0, keepdims=True))
        dg_kout = per_chunk(lambda h, c: jnp.sum(_chunk(_head(dkout * pre["kout"], h), c), axis=0, keepdims=True))
        dv = dv_intra + dv_inter
        pos = lax.broadcasted_iota(jnp.int32, (HGRN_ROWS, HGRN_W), 0) % HGRN_CHUNK
        dq = dqt * pre["e_q"] + dqg * pre["e_in"]
        dk = dkt * pre["e_k"] + dkout * pre["e_out"]
        dg = (dqt * pre["qt"] - dkt * pre["kt"] + dqg * pre["qg"] - dkout * pre["kout"]
              + jnp.where(pos == HGRN_CHUNK - 1, dg_state * dec + dg_kout, 0.0))
        dlogf = _chunk_cumsum(dg, reverse=True)
        sig, sq = pre["sig"], pre["sq"]
        df = dlogf / pre["f"] - dk
        dseg_ref[0] = (dq * HGRN_SCALE * (sq * (1.0 + qh * (1.0 - sq)))).astype(BF)
        dseg_ref[1] = (df * (1.0 - lbv) * sig * (1.0 - sig)).astype(BF)
        dseg_ref[2] = dv.astype(BF)
        dseg_ref[3] = dgate_in.astype(BF)
        _accumulate_rows(step, dlb_ref, _part8(df * (1.0 - sig)))
        _accumulate_rows(step, dgain_ref, _part8(dgain))

    specs, blk = _hgrn_in_specs(layer, True, nblk)
    specs += [pl.BlockSpec((HGRN_ROWS, HGRN_W), lambda b: (blk(b), 0)),
              pl.BlockSpec((HGRN_ROWS, HGRN_W), lambda b: (blk(b), 1)),
              pl.BlockSpec((HGRN_HEADS, cpb * LANES, LANES), lambda b: (0, blk(b), 0))]
    return pl.pallas_call(
        body, grid=(nblk,), in_specs=specs,
        out_specs=[pl.BlockSpec((4, HGRN_ROWS, HGRN_W), lambda b: (0, blk(b), 0)),
                   pl.BlockSpec((8, HGRN_W), lambda b: (0, 0)), pl.BlockSpec((8, HGRN_DIM), lambda b: (0, 0))],
        out_shape=[jax.ShapeDtypeStruct((4, s, HGRN_W), BF), jax.ShapeDtypeStruct((8, HGRN_W), F32),
                   jax.ShapeDtypeStruct((8, HGRN_DIM), F32)],
        scratch_shapes=[pltpu.VMEM((HGRN_HEADS, LANES, LANES), F32), pltpu.VMEM((HGRN_HEADS, cpb * LANES, LANES), F32)],
        compiler_params=_cparams(("arbitrary",)), name=f"hgrn_bwd_l{layer}",
    )(proj, proj, proj, proj, lb, gain, o, dmixed, states)


def _bwd_inproj(layer, dqkv, dhg, g_in, x, gain, dres, tm=512):
    s, d = x.shape

    def body(dqkv_ref, dhg_ref, w_ref, x_ref, gain_ref, dres_ref, dx_ref, dxb_ref, dgain_ref):
        acc = jnp.zeros((tm, d), F32)
        for seg in range(N_SEG):
            a = dqkv_ref[seg] if seg < 3 else dhg_ref[seg - 3]
            acc = acc + _dot(a, w_ref[seg * SEG:(seg + 1) * SEG, :], NN)
        dx, dgain = _rms_bwd(acc, x_ref[...], gain_ref[...])
        out = dres_ref[...] + dx
        dx_ref[...] = out
        dxb_ref[...] = out.astype(BF)
        _accumulate_rows(pl.program_id(0), dgain_ref, _part8(dgain))

    row = pl.BlockSpec((tm, d), lambda i: (i, 0))
    return pl.pallas_call(
        body, grid=(s // tm,),
        in_specs=[pl.BlockSpec((3, tm, SEG), lambda i: (0, i, 0)), pl.BlockSpec((4, tm, SEG), lambda i: (0, i, 0)),
                  _resident((None, PROJ_W, d), lambda i: (0, 0, 0)), row,
                  pl.BlockSpec((None, 1, d), lambda i: (layer, 0, 0)), row],
        out_specs=[row, row, pl.BlockSpec((8, d), lambda i: (0, 0))],
        out_shape=[jax.ShapeDtypeStruct((s, d), F32), jax.ShapeDtypeStruct((s, d), BF), jax.ShapeDtypeStruct((8, d), F32)],
        compiler_params=_cparams(("arbitrary",)), name=f"bwd_inproj_l{layer}",
    )(dqkv, dhg, g_in, x, gain, dres)


def _adamw(w, g, m, v):
    m2 = ADAM_B1 * m + (1.0 - ADAM_B1) * g
    v2 = ADAM_B2 * v + (1.0 - ADAM_B2) * (g * g)
    m_hat = m2 / (1.0 - ADAM_B1 ** ADAM_STEP)
    v_hat = v2 / (1.0 - ADAM_B2 ** ADAM_STEP)
    delta = -ADAM_LR * (m_hat / (jnp.sqrt(v_hat) + ADAM_EPS) + ADAM_WD * w)
    return delta, m2, v2


def _adam_big(name, parts, w, m, v, row_tiles):
    depth = w.shape[0]
    r, c = parts[0].shape[1], parts[0].shape[2]
    tr = r // row_tiles
    p_spec = pl.BlockSpec((N_DEV, tr, c), lambda t: (0, t, 0))
    w_spec = pl.BlockSpec((depth, tr, c), lambda t: (0, t, 0))

    def body(*refs):
        p_refs = refs[:depth]
        w_ref, m_ref, v_ref, g_ref, d_ref, m2_ref, v2_ref, token = refs[depth:]
        token[...] = jnp.zeros(token.shape, F32)
        for l in range(depth):
            g = p_refs[l][0].astype(F32)
            for dev in range(1, N_DEV):
                g = g + p_refs[l][dev].astype(F32)
            delta, m2, v2 = _adamw(w_ref[l], g, m_ref[l], v_ref[l])
            g_ref[l] = g
            d_ref[l] = delta
            m2_ref[l] = m2
            v2_ref[l] = v2

    return pl.pallas_call(
        body, grid=(row_tiles,), in_specs=[p_spec] * depth + [w_spec] * 3,
        out_specs=[w_spec] * 4 + [pl.BlockSpec((8, LANES), lambda t: (0, 0))],
        out_shape=[jax.ShapeDtypeStruct(w.shape, F32)] * 4 + [jax.ShapeDtypeStruct((8, LANES), F32)],
        compiler_params=_cparams(("arbitrary",)), name=name,
    )(*parts, w, m, v)


def _adam_small(g, w, m, v):
    def body(g_ref, w_ref, m_ref, v_ref, d_ref, m2_ref, v2_ref):
        delta, m2, v2 = _adamw(w_ref[...], g_ref[...], m_ref[...], v_ref[...])
        d_ref[...] = delta
        m2_ref[...] = m2
        v2_ref[...] = v2

    vm = pl.BlockSpec(memory_space=pltpu.VMEM)
    return pl.pallas_call(body, in_specs=[vm] * 4, out_specs=[vm] * 3, out_shape=[jax.ShapeDtypeStruct(g.shape, F32)] * 3,
                          name="adam_small")(g, w, m, v)


def _lower_bounds(logits):
    def body(l_ref, lb_ref, jac_ref):
        l0, l1 = l_ref[0:1, :], l_ref[1:2, :]
        mx = jnp.maximum(l0, l1)
        e0, e1 = jnp.exp(l0 - mx), jnp.exp(l1 - mx)
        p0, p1 = e0 / (e0 + e1), e1 / (e0 + e1)
        lb_ref[0:1, :] = p0 - p0
        lb_ref[1:2, :] = (p0 + p1) - p0
        jac_ref[0:1, :] = -p0 * p1
        jac_ref[1:2, :] = p0 * p1

    vm = pl.BlockSpec(memory_space=pltpu.VMEM)
    return pl.pallas_call(body, in_specs=[vm], out_specs=[vm, vm], out_shape=[jax.ShapeDtypeStruct(logits.shape, F32)] * 2,
                          name="hgrn_lower_bounds")(logits)


def _rope_tables(s):
    half = 32
    inv_freq = ROPE_THETA ** (-jnp.arange(half, dtype=F32) / half)
    ang = jnp.arange(s, dtype=jnp.int32).astype(F32)[:, None] * inv_freq[None, :]
    cos, sin = jnp.cos(ang), jnp.sin(ang)
    return jnp.concatenate([cos] * 4, axis=1), jnp.concatenate([-sin, sin, -sin, sin], axis=1)


SMALL_NAMES = ("norm_mix", "attn_out_gain", "hgrn_lb_logits", "hgrn_out_gain", "norm_mlp", "norm_final")


def _pack_small(vals):
    flat = jnp.concatenate([v.reshape(-1) for v in vals])
    rows = -(-flat.shape[0] // (8 * LANES)) * 8
    return jnp.pad(flat, (0, rows * LANES - flat.shape[0])).reshape(rows, LANES)


def _unpack_small(packed, like):
    flat, out, off = packed.reshape(-1), [], 0
    for v in like:
        out.append(flat[off:off + v.size].reshape(v.shape))
        off += v.size
    return out


def kernel(x, norm_mix, w_in, attn_out_gain, hgrn_lb_logits, hgrn_out_gain, w_out, norm_mlp, w_up, w_down, norm_final, loss_target, m_norm_mix, m_w_in, m_attn_out_gain, m_hgrn_lb_logits, m_hgrn_out_gain, m_w_out, m_norm_mlp, m_w_up, m_w_down, m_norm_final, v_norm_mix, v_w_in, v_attn_out_gain, v_hgrn_lb_logits, v_hgrn_out_gain, v_w_out, v_norm_mlp, v_w_up, v_w_down, v_norm_final):
    depth = w_in.shape[0]
    assert depth == 2 and x.shape[0] == 1
    s, d = x.shape[1], x.shape[2]
    x0 = x[0]
    target = loss_target[0]
    cos, sin = _rope_tables(s)
    g_mix, g_attn, g_hg, g_mlp = (norm_mix[:, None, :], attn_out_gain[:, None, :], hgrn_out_gain[:, None, :],
                                  norm_mlp[:, None, :])
    lb, lb_jac = _lower_bounds(hgrn_lb_logits)
    lb3 = lb[:, None, :]

    def flip(a):
        return jnp.swapaxes(a, 1, 2)

    shards = list(_pack_weights(flip(w_in), w_out, w_up, w_down))
    w_pieces = _weight_pieces(*shards)
    w_groups = [[0], [1, 2, 3], [4], [5, 6, 7]]
    me = (4 * lax.axis_index("x") + 2 * lax.axis_index("y") + lax.axis_index("c")).astype(jnp.int32).reshape(1)
    lands = _exchange_own("all_gather_own", me, shards, w_pieces)
    w_sems, shards, lands, token = _exchange_start("all_gather_start", shards, lands, w_pieces, w_groups)

    def weights_ready(group, after):
        nonlocal shards
        idxs = w_groups[group]
        shards, got = _exchange_wait(f"all_gather_wait{group}", shards, [lands[i] for i in idxs], w_pieces,
                                     [(idxs, *w_sems[group])], after)
        return got

    def tied(small_arr, tok):
        return small_arr + tok[0, 0]

    saved = []
    xl = x0
    full = [None] * depth
    for l in range(depth):
        (full_in,) = weights_ready(2 * l, token if l == 0 else xl)
        proj, h = _fwd_inproj(l, xl, g_mix, full_in, cos, sin)
        o_attn, lse = _attn_fwd(l, proj)
        o_hg, mixed, states = _hgrn_fwd(l, proj, lb3, g_hg)
        mixed = _attn_norm(l, o_attn, g_attn, mixed)
        full_out, full_up, full_down = weights_ready(2 * l + 1, mixed)
        x_next, x_mid, u, h2 = _mlp_fwd(l, xl, mixed, g_mlp, full_out, full_up, full_down)
        saved.append((xl, proj, h, o_attn, lse, o_hg, states, mixed, x_mid, u, h2))
        full[l] = (full_in, full_out, full_up, full_down)
        xl = x_next
    dx, dxb, dnorm_final8, loss8 = _loss_head(xl, norm_final[None, :], target)
    loss = lax.psum(jnp.sum(loss8[:, 0]), ("x", "y", "c"))

    exchanges = []

    def scatter(tag, grads, kinds):
        pieces = _grad_pieces(grads, kinds)
        own = _exchange_own(f"reduce_scatter_own_{tag}", me, grads, pieces)
        sems, grads, own, tok = _exchange_start(f"reduce_scatter_start_{tag}", grads, own, pieces, [list(range(len(pieces)))])
        exchanges.append((grads, own, pieces, sems[0]))
        return tok

    small = {}
    for l in reversed(range(depth)):
        xl, proj, h, o_attn, lse, o_hg, states, mixed, x_mid, u, h2 = saved[l]
        full_in, full_out, full_up, full_down = full[l]
        hs = full_up.shape[3]
        gw_down = _mm_tn(f"grad_w_down_l{l}", u, dxb, u.shape[1], a_fn=_relu2)
        dx_mid, dx_mid_b, du, dmixed, dmlp8 = _mlp_bwd(l, dx, dxb, x_mid, g_mlp, u, full_out, full_up, full_down)
        gw_up = _mm_tn(f"grad_w_up_l{l}", h2, du, d, out_block_w=hs)
        gw_out = _mm_tn(f"grad_w_out_l{l}", mixed, dx_mid_b, mixed.shape[1])
        g_attn_t = tied(g_attn, scatter(f"mlp_l{l}", [gw_down, gw_up, gw_out], ["rows", "up", "rows"]))
        do, delta, dattn8 = _attn_norm_bwd(l, dmixed, o_attn, g_attn_t)
        dqkv = _attn_bwd(l, proj, do, lse, delta, cos, sin)
        dhg, dlb8, dhgain8 = _hgrn_bwd(l, proj, lb3, g_hg, o_hg, dmixed, states)
        gin = _mm_tn(f"grad_w_in_qkv_l{l}", dqkv, h, PROJ_W, a_lead=True)
        gw_in = _mm_tn(f"grad_w_in_hg_l{l}", dhg, h, PROJ_W, a_lead=True, out_block_off=3, prev=gin)
        g_mix_t = tied(g_mix, scatter(f"mix_l{l}", [gw_in], ["rows"]))
        dx, dxb, dmix8 = _bwd_inproj(l, dqkv, dhg, full_in, xl, g_mix_t, dx_mid)
        small[l] = (dmix8, dattn8, dlb8, dhgain8, dmlp8)

    def fin(p8):
        return jnp.sum(p8, axis=0)
    dlogits = lb_jac * fin(small[1][2])[None, :]
    small_grads = [jnp.stack([fin(small[l][0]) for l in range(depth)]), jnp.stack([fin(small[l][1]) for l in range(depth)]),
                   dlogits, jnp.stack([fin(small[l][3]) for l in range(depth)]),
                   jnp.stack([fin(small[l][4]) for l in range(depth)]), fin(dnorm_final8)]
    small_w = [norm_mix, attn_out_gain, hgrn_lb_logits, hgrn_out_gain, norm_mlp, norm_final]
    small_m = [m_norm_mix, m_attn_out_gain, m_hgrn_lb_logits, m_hgrn_out_gain, m_norm_mlp, m_norm_final]
    small_v = [v_norm_mix, v_attn_out_gain, v_hgrn_lb_logits, v_hgrn_out_gain, v_norm_mlp, v_norm_final]
    def scattered(name, which, after):
        grads, lands, pieces, waits = [], [], [], []
        for grads_e, own, pieces_e, (send, recv) in (exchanges[i] for i in which):
            first = len(pieces)
            pieces += [p._replace(src=p.src + len(grads)) for p in pieces_e]
            waits.append((list(range(first, first + len(pieces_e))), send, recv))
            grads += grads_e
            lands += own
        return _exchange_wait(name, grads, lands, pieces, waits, after)[1]

    down1, up1, out1, in1, down0, up0, out0 = scattered("reduce_scatter_wait_early", (0, 1, 2), dx)
    big = {
        "w_down": _adam_big("adam_w_down", [down0, down1], w_down, m_w_down, v_w_down, 4),
        "w_up": _adam_big("adam_w_up", [up0, up1], w_up, m_w_up, v_w_up, 2),
        "w_out": _adam_big("adam_w_out", [out0, out1], w_out, m_w_out, v_w_out, 1),
    }
    g_small = _all_reduce_small(tied(_pack_small(small_grads), big["w_out"][4]))
    d_small, m_small, v_small = _adam_small(g_small, _pack_small(small_w), _pack_small(small_m), _pack_small(small_v))
    gs, ds, ms, vs = (_unpack_small(t, small_w) for t in (g_small, d_small, m_small, v_small))
    (in0,) = scattered("reduce_scatter_wait_last", (3,), d_small)
    big["w_in"] = [flip(t) for t in _adam_big("adam_w_in", [in0, in1], flip(w_in), flip(m_w_in), flip(v_w_in), 2)[:4]]

    def gather(idx, small_list):
        by_name = dict(zip(SMALL_NAMES, small_list))
        return [by_name["norm_mix"], big["w_in"][idx], by_name["attn_out_gain"], by_name["hgrn_lb_logits"],
                by_name["hgrn_out_gain"], big["w_out"][idx], by_name["norm_mlp"], big["w_up"][idx], big["w_down"][idx],
                by_name["norm_final"]]

    return (loss, dx[None], *gather(0, gs), *gather(1, ds), *gather(2, ms), *gather(3, vs))
```

```python
import functools
from typing import Callable, NamedTuple

import jax
import jax.numpy as jnp
from jax import lax
from jax.experimental import pallas as pl
from jax.experimental.pallas import tpu as pltpu

F32 = jnp.float32
BF = jnp.bfloat16

N_DEV = 8
ATTN_W = 512
HGRN_W = 512
HGRN_HEADS = 4
HGRN_DIM = 128
SEG = 512
N_SEG = 7
PROJ_W = N_SEG * SEG
MIX_W = ATTN_W + HGRN_W
SPAN = 128
DILATIONS = (1, 4, 16)
HGRN_CHUNK = 16
ROPE_THETA = 10000.0
NORM_EPS = 1e-6
MASK_VALUE = -1e30
ATTN_SCALE = 0.125
HGRN_SCALE = HGRN_DIM ** -0.5
ADAM_LR = 0.001
ADAM_B1 = 0.9
ADAM_B2 = 0.999
ADAM_EPS = 1e-08
ADAM_WD = 0.01
ADAM_STEP = 10
LANES = 128
VMEM_LIMIT = 56 * 1024 * 1024

NN = ((1,), (0,))
NT = ((1,), (1,))
TN = ((0,), (0,))
MESH = pl.DeviceIdType.MESH


def _dot(a, b, dims):
    return lax.dot_general(a, b, (dims, ((), ())), preferred_element_type=F32)


def _cparams(sem):
    return pltpu.CompilerParams(dimension_semantics=sem, vmem_limit_bytes=VMEM_LIMIT)


def _part8(x):
    r, n = x.shape
    return jnp.sum(x.reshape(r // 8, 8, n), axis=0)


def _sigmoid(x):
    return 1.0 / (1.0 + jnp.exp(-x))


def _rms_fwd(x, gain):
    r = lax.rsqrt(jnp.mean(x * x, axis=-1, keepdims=True) + NORM_EPS)
    return x * r * gain


def _rms_bwd(dy, x, gain):
    r = lax.rsqrt(jnp.mean(x * x, axis=-1, keepdims=True) + NORM_EPS)
    xn = x * r
    dxn = dy * gain
    dx = r * (dxn - xn * jnp.mean(dxn * xn, axis=-1, keepdims=True))
    return dx, dy * xn


def _rope_partner(x):
    n = x.shape[-1]
    lane = lax.broadcasted_iota(jnp.int32, x.shape, x.ndim - 1)
    return jnp.where((lane % 64) < 32, pltpu.roll(x, n - 32, x.ndim - 1), pltpu.roll(x, 32, x.ndim - 1))


def _tile_lanes(t, reps):
    return jnp.concatenate([t] * reps, axis=-1)


def _mm_tn(name, a, b, out_rows, a_lead=False, out_block_off=0, prev=None, out_block_w=None, a_fn=None,
           tm=512, tn=1024, sub=512):
    kdim, n = b.shape
    m = a.shape[-1]
    tm, tn, sub = min(tm, m), min(tn, n), min(sub, kdim)
    mt = m // tm
    n_lead = a.shape[0] if a_lead else 1
    if a_lead:
        a_spec = pl.BlockSpec((None, kdim, tm), lambda j, i: (i // mt, 0, i % mt))
    else:
        a_spec = pl.BlockSpec((kdim, tm), lambda j, i: (0, i))
    b_spec = pl.BlockSpec((kdim, tn), lambda j, i: (0, j))
    if out_block_w:
        nb = tn // out_block_w
        o_shape = jax.ShapeDtypeStruct((n // out_block_w, out_rows, out_block_w), BF)
        o_spec = pl.BlockSpec((nb, tm, out_block_w), lambda j, i: (j, i + out_block_off, 0))
    else:
        nb = 0
        o_shape = jax.ShapeDtypeStruct((out_rows, n), BF)
        o_spec = pl.BlockSpec((tm, tn), lambda j, i: (i + out_block_off, j))
    arrays, specs, aliases = [a, b], [a_spec, b_spec], {}
    if prev is not None:
        arrays.append(prev)
        specs.append(pl.BlockSpec(memory_space=pl.ANY))
        aliases = {2: 0}

    def body(*refs):
        a_ref, b_ref, o_ref = refs[0], refs[1], refs[-1]
        acc = None
        for k in range(kdim // sub):
            av = a_ref[k * sub:(k + 1) * sub, :]
            if a_fn is not None:
                av = a_fn(av)
            part = _dot(av, b_ref[k * sub:(k + 1) * sub, :], TN)
            acc = part if acc is None else acc + part
        if nb:
            for t in range(nb):
                o_ref[t] = acc[:, t * out_block_w:(t + 1) * out_block_w].astype(BF)
        else:
            o_ref[...] = acc.astype(BF)

    return pl.pallas_call(
        body, grid=(n // tn, n_lead * mt), in_specs=specs, out_specs=o_spec, out_shape=o_shape,
        compiler_params=_cparams(("parallel", "parallel")), name=name, input_output_aliases=aliases,
    )(*arrays)


def _pack_weights(w_in_t, w_out, w_up, w_down):
    depth = w_in_t.shape[0]
    arrays = (w_in_t, w_out, w_up, w_down)

    def body(*refs):
        for src, dst in zip(refs[:4], refs[4:]):
            dst[...] = src[...].astype(BF)

    specs = [pl.BlockSpec((None,) + a.shape[1:], lambda l: (l, 0, 0)) for a in arrays]
    return pl.pallas_call(
        body, grid=(depth,), in_specs=specs, out_specs=specs,
        out_shape=[jax.ShapeDtypeStruct(a.shape, BF) for a in arrays],
        compiler_params=_cparams(("arbitrary",)), name="pack_weights",
    )(*arrays)


def _my_position():
    x, y, c = lax.axis_index("x"), lax.axis_index("y"), lax.axis_index("c")
    return x, y, c, 4 * x + 2 * y + c


def _peer(x, y, c, k):
    px = 1 - x if k & 4 else x
    py = 1 - y if k & 2 else y
    pc = 1 - c if k & 1 else c
    return (px, py, pc), 4 * px + 2 * py + pc


PEER_ORDER = (1, 2, 4, 3, 5, 6, 7)


class _Piece(NamedTuple):
    src: int
    send: Callable
    slot: Callable
    land_shape: tuple
    own_src: tuple
    own_slot: tuple


HBM_SPEC = pl.BlockSpec(memory_space=pltpu.HBM)
SEM_SPEC = pl.BlockSpec(memory_space=pltpu.SEMAPHORE)
ANY_SPEC = pl.BlockSpec(memory_space=pl.ANY)


def _in_hbm(arrays):
    return [pltpu.with_memory_space_constraint(a, pltpu.HBM) for a in arrays]


def _hbm_like(arrays):
    return [pltpu.HBM(a.shape, a.dtype) for a in arrays]


def _rows_of(rows):
    return lambda ref, dev: ref.at[pl.ds(pl.multiple_of(dev * rows, 16), rows), :]


def _exchange_own(name, me, srcs, pieces):
    n = len(pieces)

    def body(me_ref, *refs):
        for i in range(n):
            refs[n + i][...] = refs[i][...]

    def spec(block_and_index):
        block, index = block_and_index
        return pl.BlockSpec(block, lambda i, me_ref: index(me_ref[0]))

    return pl.pallas_call(
        body,
        grid_spec=pltpu.PrefetchScalarGridSpec(
            num_scalar_prefetch=1, grid=(1,), in_specs=[spec(p.own_src) for p in pieces],
            out_specs=[spec(p.own_slot) for p in pieces]),
        out_shape=[jax.ShapeDtypeStruct(p.land_shape, BF) for p in pieces],
        compiler_params=_cparams(("arbitrary",)), name=name,
    )(me, *[srcs[p.src] for p in pieces])


def _exchange_start(name, srcs, lands, pieces, groups):
    n_src, n, n_g = len(srcs), len(pieces), len(groups)

    def body(*refs):
        src_refs, land_refs = refs[:n_src], refs[n_src:n_src + n]
        sems, token = refs[n_src + n:n_src + n + 2 * n_g], refs[-1]
        x, y, c, me = _my_position()
        for g, idxs in enumerate(groups):
            for k in PEER_ORDER:
                peer, pid = _peer(x, y, c, k)
                for j, i in enumerate(idxs):
                    p = pieces[i]
                    pltpu.make_async_remote_copy(
                        src_ref=p.send(src_refs[p.src], pid), dst_ref=p.slot(land_refs[i], me),
                        send_sem=sems[2 * g].at[(k - 1) * len(idxs) + j], recv_sem=sems[2 * g + 1].at[(k - 1) * len(idxs) + j],
                        device_id=peer, device_id_type=MESH).start()
        token[...] = jnp.zeros(token.shape, F32)

    sem_shapes = [pltpu.SemaphoreType.DMA(((N_DEV - 1) * len(idxs),)) for idxs in groups for _ in range(2)]
    res = pl.pallas_call(
        body, in_specs=[HBM_SPEC] * (n_src + n),
        out_specs=[SEM_SPEC] * (2 * n_g) + [HBM_SPEC] * (n_src + n) + [pl.BlockSpec(memory_space=pltpu.VMEM)],
        out_shape=sem_shapes + _hbm_like(srcs) + _hbm_like(lands) + [jax.ShapeDtypeStruct((8, LANES), F32)],
        input_output_aliases={i: 2 * n_g + i for i in range(n_src + n)},
        compiler_params=pltpu.CompilerParams(has_side_effects=pltpu.SideEffectType.DATAFLOW_SIDE_EFFECTING),
        name=name,
    )(*_in_hbm(srcs), *_in_hbm(lands))
    sems = [(res[2 * g], res[2 * g + 1]) for g in range(n_g)]
    return sems, list(res[2 * n_g:2 * n_g + n_src]), list(res[2 * n_g + n_src:2 * n_g + n_src + n]), res[-1]


def _exchange_wait(name, srcs, lands, pieces, waits, after):
    n_src, n, n_g = len(srcs), len(lands), len(waits)

    def body(*refs):
        src_refs, land_refs = refs[:n_src], refs[n_src:n_src + n]
        sems = refs[n_src + n:n_src + n + 2 * n_g]
        x, y, c, me = _my_position()
        at = 0
        for g, (idxs, _, _) in enumerate(waits):
            for k in PEER_ORDER:
                peer, pid = _peer(x, y, c, k)
                for j, i in enumerate(idxs):
                    p = pieces[i]
                    cp = pltpu.make_async_remote_copy(
                        src_ref=p.send(src_refs[p.src], pid), dst_ref=p.slot(land_refs[at + j], pid),
                        send_sem=sems[2 * g].at[(k - 1) * len(idxs) + j], recv_sem=sems[2 * g + 1].at[(k - 1) * len(idxs) + j],
                        device_id=peer, device_id_type=MESH)
                    cp.wait_send()
                    cp.wait_recv()
            at += len(idxs)

    sem_args = [s for _, send, recv in waits for s in (send, recv)]
    res = pl.pallas_call(
        body, in_specs=[HBM_SPEC] * (n_src + n) + [SEM_SPEC] * (2 * n_g) + [ANY_SPEC],
        out_specs=[HBM_SPEC] * (n_src + n), out_shape=_hbm_like(srcs) + _hbm_like(lands),
        input_output_aliases={i: i for i in range(n_src + n)},
        compiler_params=pltpu.CompilerParams(has_side_effects=pltpu.SideEffectType.DATAFLOW_SIDE_EFFECTING),
        name=name,
    )(*srcs, *lands, *sem_args, after)
    return list(res[:n_src]), list(res[n_src:])


def _weight_pieces(p_in, p_out, p_up, p_down):
    depth, cin, d = p_in.shape
    rout, hs = p_out.shape[1], p_up.shape[2]
    pieces = []
    for l in range(depth):
        whole = functools.partial(lambda ref, dev, l: ref.at[l], l=l)
        layer = functools.partial(lambda dev, l: (l, 0, 0), l=l)

        def rows(src, n_rows, whole=whole, layer=layer):
            return _Piece(src, whole, lambda ref, dev: _rows_of(n_rows)(ref.at[0], dev), (1, N_DEV * n_rows, d),
                          ((None, n_rows, d), layer), ((None, n_rows, d), lambda dev: (0, dev, 0)))

        pieces += [
            rows(0, cin), rows(1, rout),
            _Piece(2, whole, lambda ref, dev: ref.at[0, dev], (1, N_DEV, d, hs),
                   ((None, d, hs), layer), ((None, None, d, hs), lambda dev: (0, dev, 0, 0))),
            rows(3, hs),
        ]
    return pieces


def _grad_pieces(g_pair, kinds):
    pieces = []
    for i, (g, kind) in enumerate(zip(g_pair, kinds)):
        lead = lambda dev: (dev, 0, 0)
        if kind == "up":
            blk = ((None,) + g.shape[1:], lead)
            pieces.append(_Piece(i, lambda ref, dev: ref.at[dev], lambda ref, dev: ref.at[dev], g.shape, blk, blk))
        else:
            rows, cols = g.shape[0] // N_DEV, g.shape[1]
            pieces.append(_Piece(i, _rows_of(rows), lambda ref, dev: ref.at[dev], (N_DEV, rows, cols),
                                 ((rows, cols), lambda dev: (dev, 0)), ((None, rows, cols), lead)))
    return pieces


SMALL_W = 1024


def _all_reduce_small(mix8, mlp8, final8, attn8, lb8_last, lb_jac, hg8, after):
    def body(mix0, mix1, mlp0, mlp1, fin, attn0, attn1, lb, jac, hg0, hg1, after_ref, o_ref, src_ref, buf_ref,
             send_sems, recv_sems):
        def total(ref):
            return jnp.sum(ref[...], axis=0, keepdims=True)

        dlb = total(lb)
        hg = jnp.concatenate([total(hg0), total(hg1)], axis=1)
        src_ref[...] = jnp.concatenate([
            total(mix0), total(mix1), total(mlp0), total(mlp1), total(fin),
            jnp.concatenate([total(attn0), total(attn1)], axis=1),
            jnp.concatenate([jac[0:1, :] * dlb, jac[1:2, :] * dlb], axis=1),
            jnp.concatenate([hg, jnp.zeros((1, SMALL_W - hg.shape[1]), F32)], axis=1)], axis=0)
        x, y, c, me = _my_position()
        buf_ref[me] = src_ref[...]
        sends = []
        for k in PEER_ORDER:
            peer, _ = _peer(x, y, c, k)
            cp = pltpu.make_async_remote_copy(src_ref=src_ref, dst_ref=buf_ref.at[me], send_sem=send_sems.at[k - 1],
                                              recv_sem=recv_sems.at[k - 1], device_id=peer, device_id_type=MESH)
            cp.start()
            sends.append(cp)
        for k in PEER_ORDER:
            peer, pid = _peer(x, y, c, k)
            pltpu.make_async_remote_copy(src_ref=src_ref, dst_ref=buf_ref.at[pid], send_sem=send_sems.at[k - 1],
                                         recv_sem=recv_sems.at[k - 1], device_id=peer, device_id_type=MESH).wait_recv()
        for cp in sends:
            cp.wait_send()
        acc = buf_ref[0]
        for dev in range(1, N_DEV):
            acc = acc + buf_ref[dev]
        o_ref[...] = acc

    assert mix8[0].shape[1] == SMALL_W
    vm = pl.BlockSpec(memory_space=pltpu.VMEM)
    return pl.pallas_call(
        body, in_specs=[vm] * 11 + [ANY_SPEC], out_specs=vm, out_shape=jax.ShapeDtypeStruct((8, SMALL_W), F32),
        scratch_shapes=[pltpu.VMEM((8, SMALL_W), F32), pltpu.VMEM((N_DEV, 8, SMALL_W), F32),
                        pltpu.SemaphoreType.DMA((N_DEV - 1,)), pltpu.SemaphoreType.DMA((N_DEV - 1,))],
        name="all_reduce_small",
    )(*mix8, *mlp8, final8, *attn8, lb8_last, lb_jac, *hg8, after)


def _resident(block_shape, index_map):
    return pl.BlockSpec(block_shape, index_map, pipeline_mode=pl.Buffered(1))


def _fwd_inproj(layer, x, gain, g_in, cos, sin, tm=512):
    s, d = x.shape

    def body(x_ref, gain_ref, w_ref, cos_ref, sin_ref, proj_ref, h_ref):
        h = _rms_fwd(x_ref[...], gain_ref[...]).astype(BF)
        h_ref[...] = h
        cs = _tile_lanes(cos_ref[...], SEG // LANES)
        sn = _tile_lanes(sin_ref[...], SEG // LANES)
        for seg in range(N_SEG):
            acc = _dot(h, w_ref[seg * SEG:(seg + 1) * SEG, :], NT)
            if seg < 2:
                acc = acc * cs + _rope_partner(acc) * sn
            if seg == 0:
                acc = acc * ATTN_SCALE
            proj_ref[:, seg * SEG:(seg + 1) * SEG] = acc

    return pl.pallas_call(
        body, grid=(s // tm,),
        in_specs=[pl.BlockSpec((tm, d), lambda i: (i, 0)), pl.BlockSpec((None, 1, d), lambda i: (layer, 0, 0)),
                  _resident((None, PROJ_W, d), lambda i: (0, 0, 0)),
                  pl.BlockSpec((tm, LANES), lambda i: (i, 0)), pl.BlockSpec((tm, LANES), lambda i: (i, 0))],
        out_specs=[pl.BlockSpec((tm, PROJ_W), lambda i: (i, 0)), pl.BlockSpec((tm, d), lambda i: (i, 0))],
        out_shape=[jax.ShapeDtypeStruct((s, PROJ_W), F32), jax.ShapeDtypeStruct((s, d), BF)],
        compiler_params=_cparams(("parallel",)), name=f"fwd_inproj_l{layer}",
    )(x, gain, g_in, cos, sin)


ATTN_UNIT = SPAN * max(DILATIONS)
ATTN_GROUP = 4


def _attn_masks(first_block_has_prev):
    row = lax.broadcasted_iota(jnp.int32, (SPAN, 2 * SPAN), 0)
    col = lax.broadcasted_iota(jnp.int32, (SPAN, 2 * SPAN), 1)
    band = (col >= row) & (col <= row + SPAN)
    lane = lax.broadcasted_iota(jnp.int32, (SPAN, LANES), 1)
    return band & ((col >= SPAN) | first_block_has_prev), band, lane < 64


def _attn_specs(n_in_extra, unit_of=lambda n: n):
    pairs = ATTN_W // LANES
    q_spec = pl.BlockSpec((ATTN_UNIT, LANES), lambda p, n: (unit_of(n), p))

    def prev(seg):
        return pl.BlockSpec((ATTN_UNIT, LANES), lambda p, n: (jnp.maximum(unit_of(n) - 1, 0), seg * pairs + p))

    def cur(seg):
        return pl.BlockSpec((ATTN_UNIT, LANES), lambda p, n: (unit_of(n), seg * pairs + p))

    return [q_spec, prev(1), cur(1), prev(2), cur(2)] + [q_spec] * n_in_extra


def _attn_groups(dil):
    blocks = ATTN_UNIT // (SPAN * dil)
    pairs = [(r, b) for r in range(dil) for b in range(blocks)]
    return [pairs[i:i + ATTN_GROUP] for i in range(0, len(pairs), ATTN_GROUP)]


def _block_rows(dil, r, b, n=1):
    start = r + dil * SPAN * b
    return pl.ds(start, n * SPAN, stride=dil) if dil > 1 else pl.ds(start, n * SPAN)


def _block_keys(prev_ref, cur_ref, dil, r, b):
    if b > 0:
        return cur_ref[_block_rows(dil, r, b - 1, 2), :]
    last = ATTN_UNIT // (SPAN * dil) - 1
    return jnp.concatenate([prev_ref[_block_rows(dil, r, last), :], cur_ref[_block_rows(dil, r, 0), :]], axis=0)


def _attn_fwd(layer, proj):
    s = proj.shape[0]
    n_pat = len(DILATIONS)
    merge_rows = 256

    def body(q_ref, kp_ref, kc_ref, vp_ref, vc_ref, o_ref, lse_ref, o_scr, lse_scr):
        m_first, m_rest, is_a = _attn_masks(pl.program_id(1) > 0)
        sels = (is_a, jnp.logical_not(is_a))
        is_a_keys = lax.broadcasted_iota(jnp.int32, (2 * SPAN, LANES), 1) < 64
        for pi, dil in enumerate(DILATIONS):
            for group in _attn_groups(dil):
                items = [(r, b, h) for r, b in group for h in range(2)]
                q = {rb: q_ref[_block_rows(dil, *rb), :] for rb in group}
                k = {rb: _block_keys(kp_ref, kc_ref, dil, *rb).astype(BF) for rb in group}
                v = {rb: _block_keys(vp_ref, vc_ref, dil, *rb).astype(BF) for rb in group}
                v_sum = {rb: (jnp.where(is_a_keys, v[rb], 1.0), jnp.where(is_a_keys, 1.0, v[rb])) for rb in group}
                sc = [jnp.where(m_first if b == 0 else m_rest,
                                _dot(jnp.where(sels[h], q[r, b], 0.0).astype(BF), k[r, b], NT), MASK_VALUE)
                      for r, b, h in items]
                mx = [jnp.max(jnp.maximum(t[:, :SPAN], t[:, SPAN:]), axis=-1, keepdims=True) for t in sc]
                p = [jnp.exp(t - m).astype(BF) for t, m in zip(sc, mx)]
                both = [_dot(t, v_sum[r, b][h], NN) for t, (r, b, h) in zip(p, items)]
                for j, (r, b) in enumerate(group):
                    t_a, t_b = both[2 * j], both[2 * j + 1]
                    den = pltpu.roll(jnp.where(is_a, t_b, t_a), 64, 1)
                    o_scr[pi, _block_rows(dil, r, b), :] = jnp.where(is_a, t_a, t_b) / den
                    lse_scr[pi, _block_rows(dil, r, b), :] = jnp.where(is_a, mx[2 * j], mx[2 * j + 1]) + jnp.log(den)
        for c in range(ATTN_UNIT // merge_rows):
            rows = slice(c * merge_rows, (c + 1) * merge_rows)
            ls = [lse_scr[pi, rows, :] for pi in range(n_pat)]
            mx = functools.reduce(jnp.maximum, ls)
            ws = [jnp.exp(l - mx) for l in ls]
            den = functools.reduce(jnp.add, ws)
            o_ref[rows, :] = functools.reduce(jnp.add, [w * o_scr[pi, rows, :] for pi, w in enumerate(ws)]) / den
            lse_ref[rows, :] = mx + jnp.log(den)

    out_spec = pl.BlockSpec((ATTN_UNIT, LANES), lambda p, n: (n, p))
    return pl.pallas_call(
        body, grid=(ATTN_W // LANES, s // ATTN_UNIT), in_specs=_attn_specs(0), out_specs=[out_spec, out_spec],
        out_shape=[jax.ShapeDtypeStruct((s, ATTN_W), F32)] * 2,
        scratch_shapes=[pltpu.VMEM((n_pat, ATTN_UNIT, LANES), F32)] * 2,
        compiler_params=_cparams(("parallel", "arbitrary")), name=f"attn_fwd_l{layer}",
    )(proj, proj, proj, proj, proj)


def _attn_norm(layer, o, gain, mixed, tm=512):
    s = o.shape[0]

    def body(o_ref, gain_ref, mixed_ref, n_ref):
        n_ref[...] = _rms_fwd(o_ref[...], gain_ref[...]).astype(BF)

    blk = pl.BlockSpec((tm, ATTN_W), lambda i: (i, 0))
    return pl.pallas_call(
        body, grid=(s // tm,),
        in_specs=[blk, pl.BlockSpec((None, 1, ATTN_W), lambda i: (layer, 0, 0)), pl.BlockSpec(memory_space=pl.ANY)],
        out_specs=blk, out_shape=jax.ShapeDtypeStruct(mixed.shape, BF), input_output_aliases={2: 0},
        compiler_params=_cparams(("parallel",)), name=f"attn_norm_l{layer}",
    )(o, gain, mixed)


def _chunk_cumsum(x, reverse=False):
    n = x.shape[0]
    pos = lax.broadcasted_iota(jnp.int32, x.shape, 0) % HGRN_CHUNK
    for sh in (1, 2, 4, 8):
        if reverse:
            x = x + jnp.where(pos < HGRN_CHUNK - sh, pltpu.roll(x, n - sh, 0), 0.0)
        else:
            x = x + jnp.where(pos >= sh, pltpu.roll(x, sh, 0), 0.0)
    return x


def _chunk_row(x, row):
    r, n = x.shape
    x3 = x.reshape(r // HGRN_CHUNK, HGRN_CHUNK, n)
    return jnp.broadcast_to(x3[:, row:row + 1, :], x3.shape).reshape(r, n)


def _hgrn_pre(qh, z, lb):
    sig = _sigmoid(z)
    f = lb + (1.0 - lb) * sig
    k = 1.0 - f
    sq = _sigmoid(qh)
    q = qh * sq * HGRN_SCALE
    g = _chunk_cumsum(jnp.log(f))
    g_mid = _chunk_row(g, HGRN_CHUNK // 2 - 1)
    g_last = _chunk_row(g, HGRN_CHUNK - 1)
    e_q, e_k = jnp.exp(g - g_mid), jnp.exp(g_mid - g)
    e_in, e_out = jnp.exp(g), jnp.exp(g_last - g)
    return dict(sig=sig, f=f, k=k, sq=sq, q=q, g_last=g_last, e_q=e_q, e_k=e_k, e_in=e_in, e_out=e_out,
                qt=q * e_q, kt=k * e_k, qg=q * e_in, kout=k * e_out)


def _hgrn_mask():
    row = lax.broadcasted_iota(jnp.int32, (LANES, LANES), 0)
    col = lax.broadcasted_iota(jnp.int32, (LANES, LANES), 1)
    return (row // HGRN_CHUNK == col // HGRN_CHUNK) & (col <= row)


def _hgrn_in_specs(layer, rev, nblk):
    def blk(b):
        return nblk - 1 - b if rev else b
    first = 3 * ATTN_W // HGRN_W
    specs = [pl.BlockSpec((HGRN_ROWS, HGRN_W), functools.partial(lambda b, seg: (blk(b), first + seg), seg=seg))
             for seg in range(4)]
    specs.append(pl.BlockSpec((None, 1, HGRN_W), lambda b: (layer, 0, 0)))
    specs.append(pl.BlockSpec((None, 1, HGRN_DIM), lambda b: (layer, 0, 0)))
    return specs, blk


def _head(x, h):
    return x[:, h * HGRN_DIM:(h + 1) * HGRN_DIM]


def _chunk(x, c):
    return x[c * HGRN_CHUNK:(c + 1) * HGRN_CHUNK]


def _sub(x, sb):
    return x[sb * LANES:(sb + 1) * LANES]


HGRN_ROWS = 256
HEADS = range(HGRN_HEADS)
SUBS = range(HGRN_ROWS // LANES)
CHUNKS = range(HGRN_ROWS // HGRN_CHUNK)


def _hgrn_fwd(layer, proj, lb, gain):
    s = proj.shape[0]
    nblk = s // HGRN_ROWS
    cpb = len(CHUNKS)

    def body(q_ref, f_ref, i_ref, g_ref, lb_ref, gain_ref, o_ref, rec_ref, st_ref, state):
        @pl.when(pl.program_id(0) == 0)
        def _():
            state[...] = jnp.zeros(state.shape, F32)

        pre = _hgrn_pre(q_ref[...], f_ref[...], lb_ref[...])
        v = i_ref[...].astype(BF)
        qt, kt, qg, kout = (pre[n].astype(BF) for n in ("qt", "kt", "qg", "kout"))
        dec = jnp.exp(pre["g_last"])
        mask = _hgrn_mask()
        a = [[jnp.where(mask, _dot(_sub(_head(qt, h), sb), _sub(_head(kt, h), sb), NT), 0.0).astype(BF) for sb in SUBS]
             for h in HEADS]
        o_intra = [[_dot(a[h][sb], _sub(_head(v, h), sb), NN) for sb in SUBS] for h in HEADS]
        update = [[_dot(_chunk(_head(v, h), c), _chunk(_head(kout, h), c), TN) for c in CHUNKS] for h in HEADS]
        for h in HEADS:
            st = state[h]
            for c in CHUNKS:
                st_ref[h, c * LANES:(c + 1) * LANES, :] = st.astype(BF)
                st = st * _head(dec, h)[c * HGRN_CHUNK:c * HGRN_CHUNK + 1, :] + update[h][c]
            state[h] = st
        inter = [[_dot(_chunk(_head(qg, h), c), st_ref[h, c * LANES:(c + 1) * LANES, :].astype(BF), NT) for c in CHUNKS]
                 for h in HEADS]
        o = [jnp.concatenate(o_intra[h], axis=0) + jnp.concatenate(inter[h], axis=0) for h in HEADS]
        o_ref[...] = jnp.concatenate(o, axis=1)
        gate = g_ref[...]
        normed = jnp.concatenate([_rms_fwd(o[h], gain_ref[...]) for h in HEADS], axis=1)
        rec_ref[...] = (normed * (gate * _sigmoid(gate))).astype(BF)

    specs, _ = _hgrn_in_specs(layer, False, nblk)
    return pl.pallas_call(
        body, grid=(nblk,), in_specs=specs,
        out_specs=[pl.BlockSpec((HGRN_ROWS, HGRN_W), lambda b: (b, 0)), pl.BlockSpec((HGRN_ROWS, HGRN_W), lambda b: (b, 1)),
                   pl.BlockSpec((HGRN_HEADS, cpb * LANES, LANES), lambda b: (0, b, 0))],
        out_shape=[jax.ShapeDtypeStruct((s, HGRN_W), F32), jax.ShapeDtypeStruct((s, MIX_W), BF),
                   jax.ShapeDtypeStruct((HGRN_HEADS, nblk * cpb * LANES, LANES), BF)],
        scratch_shapes=[pltpu.VMEM((HGRN_HEADS, LANES, LANES), F32)],
        compiler_params=_cparams(("arbitrary",)), name=f"hgrn_fwd_l{layer}",
    )(proj, proj, proj, proj, lb, gain)


def _relu2(u):
    return jnp.square(jnp.maximum(u, 0)).astype(BF)


def _mlp_fwd(layer, x, mixed, gain, g_out, g_up, g_down, tm=256):
    s, d = x.shape
    mw = mixed.shape[1]
    nblk, hs = g_up.shape[1], g_up.shape[3]

    def body(x_ref, m_ref, gain_ref, out_w_ref, up_ref, down_ref, o_ref, mid_ref, u_ref, h_ref, a_buf):
        xv = x_ref[...] + _dot(m_ref[...], out_w_ref[...], NN)
        mid_ref[...] = xv
        h = _rms_fwd(xv, gain_ref[...]).astype(BF)
        h_ref[...] = h
        for j in range(nblk):
            u = _dot(h, up_ref[j], NN)
            u_ref[:, j * hs:(j + 1) * hs] = u.astype(BF)
            a_buf[:, j * hs:(j + 1) * hs] = _relu2(u)
        acc = xv
        for j in range(nblk):
            acc = acc + _dot(a_buf[:, j * hs:(j + 1) * hs], down_ref[j * hs:(j + 1) * hs, :], NN)
        o_ref[...] = acc

    row = pl.BlockSpec((tm, d), lambda i: (i, 0))
    return pl.pallas_call(
        body, grid=(s // tm,),
        in_specs=[row, pl.BlockSpec((tm, mw), lambda i: (i, 0)), pl.BlockSpec((None, 1, d), lambda i: (layer, 0, 0)),
                  _resident((None, mw, d), lambda i: (0, 0, 0)),
                  _resident((None, nblk, d, hs), lambda i: (0, 0, 0, 0)),
                  _resident((None, nblk * hs, d), lambda i: (0, 0, 0))],
        out_specs=[row, row, pl.BlockSpec((tm, nblk * hs), lambda i: (i, 0)), row],
        out_shape=[jax.ShapeDtypeStruct((s, d), F32), jax.ShapeDtypeStruct((s, d), F32),
                   jax.ShapeDtypeStruct((s, nblk * hs), BF), jax.ShapeDtypeStruct((s, d), BF)],
        scratch_shapes=[pltpu.VMEM((tm, nblk * hs), BF)],
        compiler_params=_cparams(("parallel",)), name=f"mlp_fwd_l{layer}",
    )(x, mixed, gain, g_out, g_up, g_down)


def _loss_head(x, gain, target, tm=512):
    s, d = x.shape

    def body(x_ref, gain_ref, t_ref, dx_ref, dxb_ref, dgain_ref, loss_ref):
        i = pl.program_id(0)
        xv, gv = x_ref[...], gain_ref[...]
        err = _rms_fwd(xv, gv) - t_ref[...]
        dx, dgain = _rms_bwd(err * (1.0 / d), xv, gv)
        dx_ref[...] = dx
        dxb_ref[...] = dx.astype(BF)
        part = _part8(dgain)
        lpart = _part8(0.5 * jnp.mean(err * err, axis=-1, keepdims=True) * jnp.ones((1, LANES), F32))

        @pl.when(i == 0)
        def _():
            dgain_ref[...] = part
            loss_ref[...] = lpart

        @pl.when(i > 0)
        def _():
            dgain_ref[...] += part
            loss_ref[...] += lpart

    row = pl.BlockSpec((tm, d), lambda i: (i, 0))
    return pl.pallas_call(
        body, grid=(s // tm,),
        in_specs=[row, pl.BlockSpec((1, d), lambda i: (0, 0)), row],
        out_specs=[row, row, pl.BlockSpec((8, d), lambda i: (0, 0)), pl.BlockSpec((8, LANES), lambda i: (0, 0))],
        out_shape=[jax.ShapeDtypeStruct((s, d), F32), jax.ShapeDtypeStruct((s, d), BF), jax.ShapeDtypeStruct((8, d), F32),
                   jax.ShapeDtypeStruct((8, LANES), F32)],
        compiler_params=_cparams(("arbitrary",)), name="loss_head",
    )(x, gain, target)


def _accumulate_rows(i, ref, part):
    @pl.when(i == 0)
    def _():
        ref[...] = part

    @pl.when(i > 0)
    def _():
        ref[...] += part


def _mlp_bwd(layer, dx, dxb, x, gain, u, g_out, g_up, g_down, tm=256):
    s, d = x.shape
    mw = g_out.shape[1]
    nblk, hs = g_up.shape[1], g_up.shape[3]

    def body(dx_ref, dxb_ref, x_ref, gain_ref, u_ref, out_w_ref, up_ref, down_ref, o_ref, ob_ref, du_ref, dm_ref,
             dgain_ref):
        dxb_v = dxb_ref[...]
        for j in range(nblk):
            cols = slice(j * hs, (j + 1) * hs)
            da = _dot(dxb_v, down_ref[cols, :], NT)
            du_ref[:, cols] = (da * (2.0 * jnp.maximum(u_ref[:, cols].astype(F32), 0.0))).astype(BF)
        acc = jnp.zeros((tm, d), F32)
        for j in range(nblk):
            acc = acc + _dot(du_ref[:, j * hs:(j + 1) * hs], up_ref[j], NT)
        dxn, dgain = _rms_bwd(acc, x_ref[...], gain_ref[...])
        out = dx_ref[...] + dxn
        out_b = out.astype(BF)
        o_ref[...] = out
        ob_ref[...] = out_b
        dm_ref[...] = _dot(out_b, out_w_ref[...], NT)
        _accumulate_rows(pl.program_id(0), dgain_ref, _part8(dgain))

    row = pl.BlockSpec((tm, d), lambda i: (i, 0))
    wide = pl.BlockSpec((tm, nblk * hs), lambda i: (i, 0))
    return pl.pallas_call(
        body, grid=(s // tm,),
        in_specs=[row, row, row, pl.BlockSpec((None, 1, d), lambda i: (layer, 0, 0)), wide,
                  _resident((None, mw, d), lambda i: (0, 0, 0)),
                  _resident((None, nblk, d, hs), lambda i: (0, 0, 0, 0)),
                  _resident((None, nblk * hs, d), lambda i: (0, 0, 0))],
        out_specs=[row, row, wide, pl.BlockSpec((tm, mw), lambda i: (i, 0)), pl.BlockSpec((8, d), lambda i: (0, 0))],
        out_shape=[jax.ShapeDtypeStruct((s, d), F32), jax.ShapeDtypeStruct((s, d), BF),
                   jax.ShapeDtypeStruct((s, nblk * hs), BF), jax.ShapeDtypeStruct((s, mw), F32),
                   jax.ShapeDtypeStruct((8, d), F32)],
        compiler_params=_cparams(("arbitrary",)), name=f"mlp_bwd_l{layer}",
    )(dx, dxb, x, gain, u, g_out, g_up, g_down)


def _attn_norm_bwd(layer, dmixed, o, gain, tm=512):
    s = o.shape[0]

    def body(dm_ref, o_ref, gain_ref, do_ref, delta_ref, dgain_ref):
        i = pl.program_id(0)
        ov = o_ref[...]
        do, dgain = _rms_bwd(dm_ref[...], ov, gain_ref[...])
        do_ref[...] = do
        row = lax.broadcasted_iota(jnp.int32, (ATTN_W, ATTN_W), 0)
        col = lax.broadcasted_iota(jnp.int32, (ATTN_W, ATTN_W), 1)
        same_head = jnp.where(row // 64 == col // 64, 1.0, 0.0).astype(BF)
        prod = do * ov
        high = prod.astype(BF)
        low = (prod - high.astype(F32)).astype(BF)
        delta_ref[...] = _dot(high, same_head, NN) + _dot(low, same_head, NN)
        part = _part8(dgain)

        @pl.when(i == 0)
        def _():
            dgain_ref[...] = part

        @pl.when(i > 0)
        def _():
            dgain_ref[...] += part

    blk = pl.BlockSpec((tm, ATTN_W), lambda i: (i, 0))
    return pl.pallas_call(
        body, grid=(s // tm,), in_specs=[blk, blk, pl.BlockSpec((None, 1, ATTN_W), lambda i: (layer, 0, 0))],
        out_specs=[blk, blk, pl.BlockSpec((8, ATTN_W), lambda i: (0, 0))],
        out_shape=[jax.ShapeDtypeStruct((s, ATTN_W), F32), jax.ShapeDtypeStruct((s, ATTN_W), F32),
                   jax.ShapeDtypeStruct((8, ATTN_W), F32)],
        compiler_params=_cparams(("arbitrary",)), name=f"attn_norm_bwd_l{layer}",
    )(dmixed, o, gain)


def _attn_bwd(layer, proj, do, lse, delta, cos, sin):
    s = proj.shape[0]
    n_units = s // ATTN_UNIT
    out_rows = 256

    def unit_of(n):
        return n_units - 1 - n

    def body(q_ref, kp_ref, kc_ref, vp_ref, vc_ref, do_ref, lse_ref, delta_ref, cos_ref, sin_ref, out_ref,
             dq_ref, dk_ref, dkp_ref, dv_ref, dvp_ref, carry_k, carry_v):
        step = pl.program_id(1)
        m_first, m_rest, is_a = _attn_masks(unit_of(step) > 0)
        sels = (is_a, jnp.logical_not(is_a))
        for ref in (dq_ref, dk_ref, dkp_ref, dv_ref, dvp_ref):
            ref[...] = jnp.zeros(ref.shape, F32)
        for dil in DILATIONS:
            last = ATTN_UNIT // (SPAN * dil) - 1
            for group in _attn_groups(dil):
                items = [(r, b, h) for r, b in group for h in range(2)]
                q = {rb: q_ref[_block_rows(dil, *rb), :] for rb in group}
                dov = {rb: do_ref[_block_rows(dil, *rb), :] for rb in group}
                lse_v = {rb: lse_ref[_block_rows(dil, *rb), :] for rb in group}
                delta_v = {rb: delta_ref[_block_rows(dil, *rb), :] for rb in group}
                k = {rb: _block_keys(kp_ref, kc_ref, dil, *rb).astype(BF) for rb in group}
                v = {rb: _block_keys(vp_ref, vc_ref, dil, *rb).astype(BF) for rb in group}
                qh = [jnp.where(sels[h], q[r, b], 0.0).astype(BF) for r, b, h in items]
                doh = [jnp.where(sels[h], dov[r, b], 0.0).astype(BF) for r, b, h in items]
                sc = [jnp.where(m_first if b == 0 else m_rest, _dot(qh[i], k[r, b], NT), MASK_VALUE)
                      for i, (r, b, h) in enumerate(items)]
                p = [jnp.exp(sc[i] - lse_v[r, b][:, 64 * h:64 * h + 1]) for i, (r, b, h) in enumerate(items)]
                ds = [(p[i] * (_dot(doh[i], v[r, b], NT) - delta_v[r, b][:, 64 * h:64 * h + 1])).astype(BF)
                      for i, (r, b, h) in enumerate(items)]
                dv = [_dot(p[i].astype(BF), doh[i], TN) for i in range(len(items))]
                dq = [_dot(ds[i], k[r, b], NN) for i, (r, b, h) in enumerate(items)]
                dk = [_dot(ds[i], qh[i], TN) for i in range(len(items))]
                for j, (r, b) in enumerate(group):
                    own = _block_rows(dil, r, b)
                    dq_ref[own, :] += jnp.where(is_a, dq[2 * j], dq[2 * j + 1])
                    dk2, dv2 = dk[2 * j] + dk[2 * j + 1], dv[2 * j] + dv[2 * j + 1]
                    dk_ref[own, :] += dk2[SPAN:]
                    dv_ref[own, :] += dv2[SPAN:]
                    if b > 0:
                        before = _block_rows(dil, r, b - 1)
                        dk_ref[before, :] += dk2[:SPAN]
                        dv_ref[before, :] += dv2[:SPAN]
                    else:
                        before = _block_rows(dil, r, last)
                        dkp_ref[before, :] += dk2[:SPAN]
                        dvp_ref[before, :] += dv2[:SPAN]
        has_next = step > 0
        for c in range(ATTN_UNIT // out_rows):
            rows = slice(c * out_rows, (c + 1) * out_rows)
            cs, sn = cos_ref[rows, :], sin_ref[rows, :]
            dqv = dq_ref[rows, :]
            dkv = dk_ref[rows, :] + jnp.where(has_next, carry_k[rows, :], 0.0)
            dvv = dv_ref[rows, :] + jnp.where(has_next, carry_v[rows, :], 0.0)
            out_ref[0, rows, :] = ((dqv * cs - _rope_partner(dqv) * sn) * ATTN_SCALE).astype(BF)
            out_ref[1, rows, :] = (dkv * cs - _rope_partner(dkv) * sn).astype(BF)
            out_ref[2, rows, :] = dvv.astype(BF)
        carry_k[...] = dkp_ref[...]
        carry_v[...] = dvp_ref[...]

    tab = pl.BlockSpec((ATTN_UNIT, LANES), lambda p, n: (unit_of(n), 0))
    return pl.pallas_call(
        body, grid=(ATTN_W // LANES, n_units), in_specs=_attn_specs(3, unit_of) + [tab, tab],
        out_specs=pl.BlockSpec((3, ATTN_UNIT, LANES), lambda p, n: (0, unit_of(n), p)),
        out_shape=jax.ShapeDtypeStruct((3, s, ATTN_W), BF),
        scratch_shapes=[pltpu.VMEM((ATTN_UNIT, LANES), F32)] * 7,
        compiler_params=_cparams(("parallel", "arbitrary")), name=f"attn_bwd_l{layer}",
    )(proj, proj, proj, proj, proj, do, lse, delta, cos, sin)


def _hgrn_bwd(layer, proj, lb, gain, o, dmixed, states):
    s = proj.shape[0]
    nblk = s // HGRN_ROWS
    cpb = len(CHUNKS)

    def body(q_ref, f_ref, i_ref, g_ref, lb_ref, gain_ref, o_ref, drec_ref, st_ref, dseg_ref, dlb_ref, dgain_ref,
             dstate, dst_buf):
        step = pl.program_id(0)

        @pl.when(step == 0)
        def _():
            dstate[...] = jnp.zeros(dstate.shape, F32)

        lbv, gv = lb_ref[...], gain_ref[...]
        qh, z, gate_in = q_ref[...], f_ref[...], g_ref[...]
        pre = _hgrn_pre(qh, z, lbv)
        v = i_ref[...].astype(BF)
        sg = _sigmoid(gate_in)
        ov, drec = o_ref[...], drec_ref[...]
        dnormed = drec * (gate_in * sg)
        back = [_rms_bwd(_head(dnormed, h), _head(ov, h), gv) for h in HEADS]
        do_b = jnp.concatenate([b[0] for b in back], axis=1).astype(BF)
        dgain = back[0][1] + back[1][1] + back[2][1] + back[3][1]
        normed = jnp.concatenate([_rms_fwd(_head(ov, h), gv) for h in HEADS], axis=1)
        dgate_in = drec * normed * (sg * (1.0 + gate_in * (1.0 - sg)))
        mask = _hgrn_mask()
        qt, kt, qg, kout = (pre[n].astype(BF) for n in ("qt", "kt", "qg", "kout"))
        dec = jnp.exp(pre["g_last"])
        def intra(fn):
            return jnp.concatenate([jnp.concatenate([fn(h, sb) for sb in SUBS], axis=0) for h in HEADS], axis=1)

        def hs(x, h, sb):
            return _sub(_head(x, h), sb)

        a = [[jnp.where(mask, _dot(hs(qt, h, sb), hs(kt, h, sb), NT), 0.0).astype(BF) for sb in SUBS] for h in HEADS]
        da = [[jnp.where(mask, _dot(hs(do_b, h, sb), hs(v, h, sb), NT), 0.0).astype(BF) for sb in SUBS] for h in HEADS]
        dv_intra = intra(lambda h, sb: _dot(a[h][sb], hs(do_b, h, sb), TN))
        dqt = intra(lambda h, sb: _dot(da[h][sb], hs(kt, h, sb), NN))
        dkt = intra(lambda h, sb: _dot(da[h][sb], hs(qt, h, sb), TN))
        feed = [[_dot(_chunk(_head(do_b, h), c), _chunk(_head(qg, h), c), TN) for c in CHUNKS] for h in HEADS]
        for h in HEADS:
            dst = dstate[h]
            for c in reversed(CHUNKS):
                dst_buf[h, c * LANES:(c + 1) * LANES, :] = dst
                dst = dst * _head(dec, h)[c * HGRN_CHUNK:c * HGRN_CHUNK + 1, :] + feed[h][c]
            dstate[h] = dst

        def per_chunk(fn):
            cols = []
            for h in HEADS:
                rows = [jnp.broadcast_to(t, (HGRN_CHUNK, HGRN_DIM)) for t in (fn(h, c) for c in CHUNKS)]
                cols.append(jnp.concatenate(rows, axis=0))
            return jnp.concatenate(cols, axis=1)

        def st_prev(h, c):
            return st_ref[h, c * LANES:(c + 1) * LANES, :]

        def dst_at(h, c):
            return dst_buf[h, c * LANES:(c + 1) * LANES, :]

        dqg = per_chunk(lambda h, c: _dot(_chunk(_head(do_b, h), c), st_prev(h, c).astype(BF), NN))
        dkout = per_chunk(lambda h, c: _dot(_chunk(_head(v, h), c), dst_at(h, c).astype(BF), NN))
        dv_inter = per_chunk(lambda h, c: _dot(_chunk(_head(kout, h), c), dst_at(h, c).astype(BF), NT))
        dg_state = per_chunk(lambda h, c: jnp.sum(dst_at(h, c) * st_prev(h, c).astype(F32), axis=0, keepdims=True))
        dg_kout = per_chunk(lambda h, c: jnp.sum(_chunk(_head(dkout * pre["kout"], h), c), axis=0, keepdims=True))
        dv = dv_intra + dv_inter
        pos = lax.broadcasted_iota(jnp.int32, (HGRN_ROWS, HGRN_W), 0) % HGRN_CHUNK
        dq = dqt * pre["e_q"] + dqg * pre["e_in"]
        dk = dkt * pre["e_k"] + dkout * pre["e_out"]
        dg = (dqt * pre["qt"] - dkt * pre["kt"] + dqg * pre["qg"] - dkout * pre["kout"]
              + jnp.where(pos == HGRN_CHUNK - 1, dg_state * dec + dg_kout, 0.0))
        dlogf = _chunk_cumsum(dg, reverse=True)
        sig, sq = pre["sig"], pre["sq"]
        df = dlogf / pre["f"] - dk
        dseg_ref[0] = (dq * HGRN_SCALE * (sq * (1.0 + qh * (1.0 - sq)))).astype(BF)
        dseg_ref[1] = (df * (1.0 - lbv) * sig * (1.0 - sig)).astype(BF)
        dseg_ref[2] = dv.astype(BF)
        dseg_ref[3] = dgate_in.astype(BF)
        _accumulate_rows(step, dlb_ref, _part8(df * (1.0 - sig)))
        _accumulate_rows(step, dgain_ref, _part8(dgain))

    specs, blk = _hgrn_in_specs(layer, True, nblk)
    specs += [pl.BlockSpec((HGRN_ROWS, HGRN_W), lambda b: (blk(b), 0)),
              pl.BlockSpec((HGRN_ROWS, HGRN_W), lambda b: (blk(b), 1)),
              pl.BlockSpec((HGRN_HEADS, cpb * LANES, LANES), lambda b: (0, blk(b), 0))]
    return pl.pallas_call(
        body, grid=(nblk,), in_specs=specs,
        out_specs=[pl.BlockSpec((4, HGRN_ROWS, HGRN_W), lambda b: (0, blk(b), 0)),
                   pl.BlockSpec((8, HGRN_W), lambda b: (0, 0)), pl.BlockSpec((8, HGRN_DIM), lambda b: (0, 0))],
        out_shape=[jax.ShapeDtypeStruct((4, s, HGRN_W), BF), jax.ShapeDtypeStruct((8, HGRN_W), F32),
                   jax.ShapeDtypeStruct((8, HGRN_DIM), F32)],
        scratch_shapes=[pltpu.VMEM((HGRN_HEADS, LANES, LANES), F32), pltpu.VMEM((HGRN_HEADS, cpb * LANES, LANES), F32)],
        compiler_params=_cparams(("arbitrary",)), name=f"hgrn_bwd_l{layer}",
    )(proj, proj, proj, proj, lb, gain, o, dmixed, states)


def _bwd_inproj(layer, dqkv, dhg, g_in, x, gain, dres, tm=512):
    s, d = x.shape

    def body(dqkv_ref, dhg_ref, w_ref, x_ref, gain_ref, dres_ref, dx_ref, dxb_ref, dgain_ref):
        acc = jnp.zeros((tm, d), F32)
        for seg in range(N_SEG):
            a = dqkv_ref[seg] if seg < 3 else dhg_ref[seg - 3]
            acc = acc + _dot(a, w_ref[seg * SEG:(seg + 1) * SEG, :], NN)
        dx, dgain = _rms_bwd(acc, x_ref[...], gain_ref[...])
        out = dres_ref[...] + dx
        dx_ref[...] = out
        dxb_ref[...] = out.astype(BF)
        _accumulate_rows(pl.program_id(0), dgain_ref, _part8(dgain))

    row = pl.BlockSpec((tm, d), lambda i: (i, 0))
    return pl.pallas_call(
        body, grid=(s // tm,),
        in_specs=[pl.BlockSpec((3, tm, SEG), lambda i: (0, i, 0)), pl.BlockSpec((4, tm, SEG), lambda i: (0, i, 0)),
                  _resident((None, PROJ_W, d), lambda i: (0, 0, 0)), row,
                  pl.BlockSpec((None, 1, d), lambda i: (layer, 0, 0)), row],
        out_specs=[row, row, pl.BlockSpec((8, d), lambda i: (0, 0))],
        out_shape=[jax.ShapeDtypeStruct((s, d), F32), jax.ShapeDtypeStruct((s, d), BF), jax.ShapeDtypeStruct((8, d), F32)],
        compiler_params=_cparams(("arbitrary",)), name=f"bwd_inproj_l{layer}",
    )(dqkv, dhg, g_in, x, gain, dres)


def _adamw(w, g, m, v):
    m2 = ADAM_B1 * m + (1.0 - ADAM_B1) * g
    v2 = ADAM_B2 * v + (1.0 - ADAM_B2) * (g * g)
    m_hat = m2 / (1.0 - ADAM_B1 ** ADAM_STEP)
    v_hat = v2 / (1.0 - ADAM_B2 ** ADAM_STEP)
    delta = -ADAM_LR * (m_hat / (jnp.sqrt(v_hat) + ADAM_EPS) + ADAM_WD * w)
    return delta, m2, v2


def _adam_big(name, parts, w, m, v, row_tiles):
    depth = w.shape[0]
    r, c = parts[0].shape[1], parts[0].shape[2]
    tr = r // row_tiles
    p_spec = pl.BlockSpec((N_DEV, tr, c), lambda t: (0, t, 0))
    w_spec = pl.BlockSpec((depth, tr, c), lambda t: (0, t, 0))

    def body(*refs):
        p_refs = refs[:depth]
        w_ref, m_ref, v_ref, g_ref, d_ref, m2_ref, v2_ref, token = refs[depth:]
        token[...] = jnp.zeros(token.shape, F32)
        for l in range(depth):
            g = p_refs[l][0].astype(F32)
            for dev in range(1, N_DEV):
                g = g + p_refs[l][dev].astype(F32)
            delta, m2, v2 = _adamw(w_ref[l], g, m_ref[l], v_ref[l])
            g_ref[l] = g
            d_ref[l] = delta
            m2_ref[l] = m2
            v2_ref[l] = v2

    return pl.pallas_call(
        body, grid=(row_tiles,), in_specs=[p_spec] * depth + [w_spec] * 3,
        out_specs=[w_spec] * 4 + [pl.BlockSpec((8, LANES), lambda t: (0, 0))],
        out_shape=[jax.ShapeDtypeStruct(w.shape, F32)] * 4 + [jax.ShapeDtypeStruct((8, LANES), F32)],
        compiler_params=_cparams(("arbitrary",)), name=name,
    )(*parts, w, m, v)


def _adam_small(g, ws, ms, vs):
    n = len(ws)

    def split(row, width):
        return jnp.concatenate([row[:, :width], row[:, width:2 * width]], axis=0)

    def body(g_ref, *refs):
        ins, outs = refs[:3 * n], refs[3 * n:]
        grads = [g_ref[0:2, :], split(g_ref[5:6, :], ATTN_W), split(g_ref[6:7, :], HGRN_W), split(g_ref[7:8, :], HGRN_DIM),
                 g_ref[2:4, :], g_ref[4:5, :]]
        for i, g_i in enumerate(grads):
            delta, m2, v2 = _adamw(ins[i][...], g_i, ins[n + i][...], ins[2 * n + i][...])
            for j, val in enumerate((g_i, delta, m2, v2)):
                outs[4 * i + j][...] = val

    vm = pl.BlockSpec(memory_space=pltpu.VMEM)
    res = pl.pallas_call(
        body, in_specs=[vm] * (1 + 3 * n), out_specs=[vm] * (4 * n),
        out_shape=[jax.ShapeDtypeStruct(w.shape, F32) for w in ws for _ in range(4)], name="adam_small",
    )(g, *ws, *ms, *vs)
    return [res[4 * i:4 * i + 4] for i in range(n)]


def _lower_bounds(logits):
    def body(l_ref, lb_ref, jac_ref):
        l0, l1 = l_ref[0:1, :], l_ref[1:2, :]
        mx = jnp.maximum(l0, l1)
        e0, e1 = jnp.exp(l0 - mx), jnp.exp(l1 - mx)
        p0, p1 = e0 / (e0 + e1), e1 / (e0 + e1)
        lb_ref[0:1, :] = p0 - p0
        lb_ref[1:2, :] = (p0 + p1) - p0
        jac_ref[0:1, :] = -p0 * p1
        jac_ref[1:2, :] = p0 * p1

    vm = pl.BlockSpec(memory_space=pltpu.VMEM)
    return pl.pallas_call(body, in_specs=[vm], out_specs=[vm, vm], out_shape=[jax.ShapeDtypeStruct(logits.shape, F32)] * 2,
                          name="hgrn_lower_bounds")(logits)


def _rope_tables(s, after):
    half = 32
    inv_freq = ROPE_THETA ** (-jnp.arange(half, dtype=F32) / half)
    ang = (jnp.arange(s, dtype=jnp.int32).astype(F32) + after[0, 0])[:, None] * inv_freq[None, :]
    cos, sin = jnp.cos(ang), jnp.sin(ang)
    return jnp.concatenate([cos] * 4, axis=1), jnp.concatenate([-sin, sin, -sin, sin], axis=1)


def kernel(x, norm_mix, w_in, attn_out_gain, hgrn_lb_logits, hgrn_out_gain, w_out, norm_mlp, w_up, w_down, norm_final, loss_target, m_norm_mix, m_w_in, m_attn_out_gain, m_hgrn_lb_logits, m_hgrn_out_gain, m_w_out, m_norm_mlp, m_w_up, m_w_down, m_norm_final, v_norm_mix, v_w_in, v_attn_out_gain, v_hgrn_lb_logits, v_hgrn_out_gain, v_w_out, v_norm_mlp, v_w_up, v_w_down, v_norm_final):
    depth = w_in.shape[0]
    assert depth == 2 and x.shape[0] == 1
    s, d = x.shape[1], x.shape[2]
    x0 = x[0]
    target = loss_target[0]
    g_mix, g_attn, g_hg, g_mlp = (norm_mix[:, None, :], attn_out_gain[:, None, :], hgrn_out_gain[:, None, :],
                                  norm_mlp[:, None, :])
    lb, lb_jac = _lower_bounds(hgrn_lb_logits)
    lb3 = lb[:, None, :]

    def flip(a):
        return jnp.swapaxes(a, 1, 2)

    shards = list(_pack_weights(flip(w_in), w_out, w_up, w_down))
    w_pieces = _weight_pieces(*shards)
    w_groups = [[0], [1, 2, 3], [4], [5, 6, 7]]
    me = (4 * lax.axis_index("x") + 2 * lax.axis_index("y") + lax.axis_index("c")).astype(jnp.int32).reshape(1)
    lands = _exchange_own("all_gather_own", me, shards, w_pieces)
    w_sems, shards, lands, token = _exchange_start("all_gather_start", shards, lands, w_pieces, w_groups)

    def weights_ready(group, after):
        nonlocal shards
        idxs = w_groups[group]
        shards, got = _exchange_wait(f"all_gather_wait{group}", shards, [lands[i] for i in idxs], w_pieces,
                                     [(idxs, *w_sems[group])], after)
        return got

    cos, sin = _rope_tables(s, token)

    def tied(small_arr, tok):
        return small_arr + tok[0, 0]

    saved = []
    xl = x0
    full = [None] * depth
    for l in range(depth):
        (full_in,) = weights_ready(2 * l, cos if l == 0 else xl)
        proj, h = _fwd_inproj(l, xl, g_mix, full_in, cos, sin)
        o_attn, lse = _attn_fwd(l, proj)
        o_hg, mixed, states = _hgrn_fwd(l, proj, lb3, g_hg)
        mixed = _attn_norm(l, o_attn, g_attn, mixed)
        full_out, full_up, full_down = weights_ready(2 * l + 1, mixed)
        x_next, x_mid, u, h2 = _mlp_fwd(l, xl, mixed, g_mlp, full_out, full_up, full_down)
        saved.append((xl, proj, h, o_attn, lse, o_hg, states, mixed, x_mid, u, h2))
        full[l] = (full_in, full_out, full_up, full_down)
        xl = x_next
    dx, dxb, dnorm_final8, loss8 = _loss_head(xl, norm_final[None, :], target)
    loss = lax.psum(jnp.sum(loss8[:, 0]), ("x", "y", "c"))

    exchanges = []

    def scatter(tag, grads, kinds):
        pieces = _grad_pieces(grads, kinds)
        own = _exchange_own(f"reduce_scatter_own_{tag}", me, grads, pieces)
        sems, grads, own, tok = _exchange_start(f"reduce_scatter_start_{tag}", grads, own, pieces, [list(range(len(pieces)))])
        exchanges.append((grads, own, pieces, sems[0]))
        return tok

    small = {}
    for l in reversed(range(depth)):
        xl, proj, h, o_attn, lse, o_hg, states, mixed, x_mid, u, h2 = saved[l]
        full_in, full_out, full_up, full_down = full[l]
        hs = full_up.shape[3]
        gw_down = _mm_tn(f"grad_w_down_l{l}", u, dxb, u.shape[1], a_fn=_relu2)
        dx_mid, dx_mid_b, du, dmixed, dmlp8 = _mlp_bwd(l, dx, dxb, x_mid, g_mlp, u, full_out, full_up, full_down)
        gw_up = _mm_tn(f"grad_w_up_l{l}", h2, du, d, out_block_w=hs)
        gw_out = _mm_tn(f"grad_w_out_l{l}", mixed, dx_mid_b, mixed.shape[1])
        g_attn_t = tied(g_attn, scatter(f"mlp_l{l}", [gw_down, gw_up, gw_out], ["rows", "up", "rows"]))
        do, delta, dattn8 = _attn_norm_bwd(l, dmixed, o_attn, g_attn_t)
        dqkv = _attn_bwd(l, proj, do, lse, delta, cos, sin)
        dhg, dlb8, dhgain8 = _hgrn_bwd(l, proj, lb3, g_hg, o_hg, dmixed, states)
        gin = _mm_tn(f"grad_w_in_qkv_l{l}", dqkv, h, PROJ_W, a_lead=True)
        gw_in = _mm_tn(f"grad_w_in_hg_l{l}", dhg, h, PROJ_W, a_lead=True, out_block_off=3, prev=gin)
        g_mix_t = tied(g_mix, scatter(f"mix_l{l}", [gw_in], ["rows"]))
        dx, dxb, dmix8 = _bwd_inproj(l, dqkv, dhg, full_in, xl, g_mix_t, dx_mid)
        small[l] = (dmix8, dattn8, dlb8, dhgain8, dmlp8)

    def scattered(name, which, after):
        grads, lands, pieces, waits = [], [], [], []
        for grads_e, own, pieces_e, (send, recv) in (exchanges[i] for i in which):
            first = len(pieces)
            pieces += [p._replace(src=p.src + len(grads)) for p in pieces_e]
            waits.append((list(range(first, first + len(pieces_e))), send, recv))
            grads += grads_e
            lands += own
        return _exchange_wait(name, grads, lands, pieces, waits, after)[1]

    down1, up1, out1, in1, down0, up0, out0 = scattered("reduce_scatter_wait_early", (0, 1, 2), dx)
    big = {
        "w_down": _adam_big("adam_w_down", [down0, down1], w_down, m_w_down, v_w_down, 4),
        "w_up": _adam_big("adam_w_up", [up0, up1], w_up, m_w_up, v_w_up, 2),
        "w_out": _adam_big("adam_w_out", [out0, out1], w_out, m_w_out, v_w_out, 1),
    }
    g_small = _all_reduce_small([small[l][0] for l in range(depth)], [small[l][4] for l in range(depth)], dnorm_final8,
                                [small[l][1] for l in range(depth)], small[depth - 1][2], lb_jac,
                                [small[l][3] for l in range(depth)], big["w_out"][4])
    row = lambda a: a[None, :]
    small_out = _adam_small(
        g_small, [norm_mix, attn_out_gain, hgrn_lb_logits, hgrn_out_gain, norm_mlp, row(norm_final)],
        [m_norm_mix, m_attn_out_gain, m_hgrn_lb_logits, m_hgrn_out_gain, m_norm_mlp, row(m_norm_final)],
        [v_norm_mix, v_attn_out_gain, v_hgrn_lb_logits, v_hgrn_out_gain, v_norm_mlp, row(v_norm_final)])
    small_out[5] = [t[0] for t in small_out[5]]
    (in0,) = scattered("reduce_scatter_wait_last", (3,), small_out[0][1])
    big["w_in"] = [flip(t) for t in _adam_big("adam_w_in", [in0, in1], flip(w_in), flip(m_w_in), flip(v_w_in), 2)[:4]]

    def gather(idx):
        mix, attn, lbl, hg, mlp, final = (t[idx] for t in small_out)
        return [mix, big["w_in"][idx], attn, lbl, hg, big["w_out"][idx], mlp, big["w_up"][idx], big["w_down"][idx], final]

    return (loss, dx[None], *gather(0), *gather(1), *gather(2), *gather(3))
```

```python
import functools
from typing import Callable, NamedTuple

import jax
import jax.numpy as jnp
from jax import lax
from jax.experimental import pallas as pl
from jax.experimental.pallas import tpu as pltpu

F32 = jnp.float32
BF = jnp.bfloat16

N_DEV = 8
ATTN_W = 512
HGRN_W = 512
HGRN_HEADS = 4
HGRN_DIM = 128
SEG = 512
N_SEG = 7
PROJ_W = N_SEG * SEG
MIX_W = ATTN_W + HGRN_W
SPAN = 128
DILATIONS = (1, 4, 16)
HGRN_CHUNK = 16
ROPE_THETA = 10000.0
NORM_EPS = 1e-6
MASK_VALUE = -1e30
ATTN_SCALE = 0.125
HGRN_SCALE = HGRN_DIM ** -0.5
ADAM_LR = 0.001
ADAM_B1 = 0.9
ADAM_B2 = 0.999
ADAM_EPS = 1e-08
ADAM_WD = 0.01
ADAM_STEP = 10
LANES = 128
VMEM_LIMIT = 56 * 1024 * 1024

NN = ((1,), (0,))
NT = ((1,), (1,))
TN = ((0,), (0,))
MESH = pl.DeviceIdType.MESH


def _dot(a, b, dims):
    return lax.dot_general(a, b, (dims, ((), ())), preferred_element_type=F32)


def _cparams(sem):
    return pltpu.CompilerParams(dimension_semantics=sem, vmem_limit_bytes=VMEM_LIMIT)


def _part8(x):
    r, n = x.shape
    return jnp.sum(x.reshape(r // 8, 8, n), axis=0)


def _sigmoid(x):
    return 1.0 / (1.0 + jnp.exp(-x))


def _rms_fwd(x, gain):
    r = lax.rsqrt(jnp.mean(x * x, axis=-1, keepdims=True) + NORM_EPS)
    return x * r * gain


def _rms_bwd(dy, x, gain):
    r = lax.rsqrt(jnp.mean(x * x, axis=-1, keepdims=True) + NORM_EPS)
    xn = x * r
    dxn = dy * gain
    dx = r * (dxn - xn * jnp.mean(dxn * xn, axis=-1, keepdims=True))
    return dx, dy * xn


def _rope_partner(x):
    n = x.shape[-1]
    lane = lax.broadcasted_iota(jnp.int32, x.shape, x.ndim - 1)
    return jnp.where((lane % 64) < 32, pltpu.roll(x, n - 32, x.ndim - 1), pltpu.roll(x, 32, x.ndim - 1))


def _tile_lanes(t, reps):
    return jnp.concatenate([t] * reps, axis=-1)


def _mm_tn(name, a, b, out_rows, a_lead=False, out_block_off=0, prev=None, out_block_w=None, a_fn=None,
           tm=512, tn=1024, sub=512):
    kdim, n = b.shape
    m = a.shape[-1]
    tm, tn, sub = min(tm, m), min(tn, n), min(sub, kdim)
    mt = m // tm
    n_lead = a.shape[0] if a_lead else 1
    if a_lead:
        a_spec = pl.BlockSpec((None, kdim, tm), lambda j, i: (i // mt, 0, i % mt))
    else:
        a_spec = pl.BlockSpec((kdim, tm), lambda j, i: (0, i))
    b_spec = pl.BlockSpec((kdim, tn), lambda j, i: (0, j))
    if out_block_w:
        nb = tn // out_block_w
        o_shape = jax.ShapeDtypeStruct((n // out_block_w, out_rows, out_block_w), BF)
        o_spec = pl.BlockSpec((nb, tm, out_block_w), lambda j, i: (j, i + out_block_off, 0))
    else:
        nb = 0
        o_shape = jax.ShapeDtypeStruct((out_rows, n), BF)
        o_spec = pl.BlockSpec((tm, tn), lambda j, i: (i + out_block_off, j))
    arrays, specs, aliases = [a, b], [a_spec, b_spec], {}
    if prev is not None:
        arrays.append(prev)
        specs.append(pl.BlockSpec(memory_space=pl.ANY))
        aliases = {2: 0}

    def body(*refs):
        a_ref, b_ref, o_ref = refs[0], refs[1], refs[-1]
        acc = None
        for k in range(kdim // sub):
            av = a_ref[k * sub:(k + 1) * sub, :]
            if a_fn is not None:
                av = a_fn(av)
            part = _dot(av, b_ref[k * sub:(k + 1) * sub, :], TN)
            acc = part if acc is None else acc + part
        if nb:
            for t in range(nb):
                o_ref[t] = acc[:, t * out_block_w:(t + 1) * out_block_w].astype(BF)
        else:
            o_ref[...] = acc.astype(BF)

    return pl.pallas_call(
        body, grid=(n // tn, n_lead * mt), in_specs=specs, out_specs=o_spec, out_shape=o_shape,
        compiler_params=_cparams(("parallel", "parallel")), name=name, input_output_aliases=aliases,
    )(*arrays)


def _pack_weights(w_in_t, w_out, w_up, w_down):
    depth = w_in_t.shape[0]
    arrays = (w_in_t, w_out, w_up, w_down)

    def body(*refs):
        for src, dst in zip(refs[:4], refs[4:]):
            dst[...] = src[...].astype(BF)

    specs = [pl.BlockSpec((None,) + a.shape[1:], lambda l: (l, 0, 0)) for a in arrays]
    return pl.pallas_call(
        body, grid=(depth,), in_specs=specs, out_specs=specs,
        out_shape=[jax.ShapeDtypeStruct(a.shape, BF) for a in arrays],
        compiler_params=_cparams(("arbitrary",)), name="pack_weights",
    )(*arrays)


def _my_position():
    x, y, c = lax.axis_index("x"), lax.axis_index("y"), lax.axis_index("c")
    return x, y, c, 4 * x + 2 * y + c


def _peer(x, y, c, k):
    px = 1 - x if k & 4 else x
    py = 1 - y if k & 2 else y
    pc = 1 - c if k & 1 else c
    return (px, py, pc), 4 * px + 2 * py + pc


PEER_ORDER = (1, 2, 4, 3, 5, 6, 7)


class _Piece(NamedTuple):
    src: int
    send: Callable
    slot: Callable
    land_shape: tuple
    own_src: tuple
    own_slot: tuple


HBM_SPEC = pl.BlockSpec(memory_space=pltpu.HBM)
SEM_SPEC = pl.BlockSpec(memory_space=pltpu.SEMAPHORE)
ANY_SPEC = pl.BlockSpec(memory_space=pl.ANY)


def _in_hbm(arrays):
    return [pltpu.with_memory_space_constraint(a, pltpu.HBM) for a in arrays]


def _hbm_like(arrays):
    return [pltpu.HBM(a.shape, a.dtype) for a in arrays]


def _rows_of(rows):
    return lambda ref, dev: ref.at[pl.ds(pl.multiple_of(dev * rows, 16), rows), :]


def _exchange_own(name, me, srcs, pieces):
    n = len(pieces)

    def body(me_ref, *refs):
        for i in range(n):
            refs[n + i][...] = refs[i][...]

    def spec(block_and_index):
        block, index = block_and_index
        return pl.BlockSpec(block, lambda i, me_ref: index(me_ref[0]))

    return pl.pallas_call(
        body,
        grid_spec=pltpu.PrefetchScalarGridSpec(
            num_scalar_prefetch=1, grid=(1,), in_specs=[spec(p.own_src) for p in pieces],
            out_specs=[spec(p.own_slot) for p in pieces]),
        out_shape=[jax.ShapeDtypeStruct(p.land_shape, BF) for p in pieces],
        compiler_params=_cparams(("arbitrary",)), name=name,
    )(me, *[srcs[p.src] for p in pieces])


def _exchange_start(name, srcs, lands, pieces, groups):
    n_src, n, n_g = len(srcs), len(pieces), len(groups)

    def body(*refs):
        src_refs, land_refs = refs[:n_src], refs[n_src:n_src + n]
        sems, token = refs[n_src + n:n_src + n + 2 * n_g], refs[-1]
        x, y, c, me = _my_position()
        for g, idxs in enumerate(groups):
            for k in PEER_ORDER:
                peer, pid = _peer(x, y, c, k)
                for j, i in enumerate(idxs):
                    p = pieces[i]
                    pltpu.make_async_remote_copy(
                        src_ref=p.send(src_refs[p.src], pid), dst_ref=p.slot(land_refs[i], me),
                        send_sem=sems[2 * g].at[(k - 1) * len(idxs) + j], recv_sem=sems[2 * g + 1].at[(k - 1) * len(idxs) + j],
                        device_id=peer, device_id_type=MESH).start()
        token[...] = jnp.zeros(token.shape, F32)

    sem_shapes = [pltpu.SemaphoreType.DMA(((N_DEV - 1) * len(idxs),)) for idxs in groups for _ in range(2)]
    res = pl.pallas_call(
        body, in_specs=[HBM_SPEC] * (n_src + n),
        out_specs=[SEM_SPEC] * (2 * n_g) + [HBM_SPEC] * (n_src + n) + [pl.BlockSpec(memory_space=pltpu.VMEM)],
        out_shape=sem_shapes + _hbm_like(srcs) + _hbm_like(lands) + [jax.ShapeDtypeStruct((8, LANES), F32)],
        input_output_aliases={i: 2 * n_g + i for i in range(n_src + n)},
        compiler_params=pltpu.CompilerParams(has_side_effects=pltpu.SideEffectType.DATAFLOW_SIDE_EFFECTING),
        name=name,
    )(*_in_hbm(srcs), *_in_hbm(lands))
    sems = [(res[2 * g], res[2 * g + 1]) for g in range(n_g)]
    return sems, list(res[2 * n_g:2 * n_g + n_src]), list(res[2 * n_g + n_src:2 * n_g + n_src + n]), res[-1]


def _exchange_wait(name, srcs, lands, pieces, waits, after):
    n_src, n, n_g = len(srcs), len(lands), len(waits)

    def body(*refs):
        src_refs, land_refs = refs[:n_src], refs[n_src:n_src + n]
        sems = refs[n_src + n:n_src + n + 2 * n_g]
        x, y, c, me = _my_position()
        at = 0
        for g, (idxs, _, _) in enumerate(waits):
            for k in PEER_ORDER:
                peer, pid = _peer(x, y, c, k)
                for j, i in enumerate(idxs):
                    p = pieces[i]
                    cp = pltpu.make_async_remote_copy(
                        src_ref=p.send(src_refs[p.src], pid), dst_ref=p.slot(land_refs[at + j], pid),
                        send_sem=sems[2 * g].at[(k - 1) * len(idxs) + j], recv_sem=sems[2 * g + 1].at[(k - 1) * len(idxs) + j],
                        device_id=peer, device_id_type=MESH)
                    cp.wait_send()
                    cp.wait_recv()
            at += len(idxs)

    sem_args = [s for _, send, recv in waits for s in (send, recv)]
    res = pl.pallas_call(
        body, in_specs=[HBM_SPEC] * (n_src + n) + [SEM_SPEC] * (2 * n_g) + [ANY_SPEC],
        out_specs=[HBM_SPEC] * (n_src + n), out_shape=_hbm_like(srcs) + _hbm_like(lands),
        input_output_aliases={i: i for i in range(n_src + n)},
        compiler_params=pltpu.CompilerParams(has_side_effects=pltpu.SideEffectType.DATAFLOW_SIDE_EFFECTING),
        name=name,
    )(*srcs, *lands, *sem_args, after)
    return list(res[:n_src]), list(res[n_src:])


def _weight_pieces(p_in, p_out, p_up, p_down):
    depth, cin, d = p_in.shape
    rout, hs = p_out.shape[1], p_up.shape[2]
    pieces = []
    for l in range(depth):
        whole = functools.partial(lambda ref, dev, l: ref.at[l], l=l)
        layer = functools.partial(lambda dev, l: (l, 0, 0), l=l)

        def rows(src, n_rows, whole=whole, layer=layer):
            return _Piece(src, whole, lambda ref, dev: _rows_of(n_rows)(ref.at[0], dev), (1, N_DEV * n_rows, d),
                          ((None, n_rows, d), layer), ((None, n_rows, d), lambda dev: (0, dev, 0)))

        pieces += [
            rows(0, cin), rows(1, rout),
            _Piece(2, whole, lambda ref, dev: ref.at[0, dev], (1, N_DEV, d, hs),
                   ((None, d, hs), layer), ((None, None, d, hs), lambda dev: (0, dev, 0, 0))),
            rows(3, hs),
        ]
    return pieces


def _grad_pieces(g_pair, kinds):
    pieces = []
    for i, (g, kind) in enumerate(zip(g_pair, kinds)):
        lead = lambda dev: (dev, 0, 0)
        if kind == "up":
            blk = ((None,) + g.shape[1:], lead)
            pieces.append(_Piece(i, lambda ref, dev: ref.at[dev], lambda ref, dev: ref.at[dev], g.shape, blk, blk))
        else:
            rows, cols = g.shape[0] // N_DEV, g.shape[1]
            pieces.append(_Piece(i, _rows_of(rows), lambda ref, dev: ref.at[dev], (N_DEV, rows, cols),
                                 ((rows, cols), lambda dev: (dev, 0)), ((None, rows, cols), lead)))
    return pieces


SMALL_W = 1024


def _all_reduce_small(mix8, mlp8, final8, attn8, lb8_last, lb_jac, hg8, loss8, after):
    def body(mix0, mix1, mlp0, mlp1, fin, attn0, attn1, lb, jac, hg0, hg1, loss, after_ref, o_ref, src_ref, buf_ref,
             send_sems, recv_sems):
        def total(ref):
            return jnp.sum(ref[...], axis=0, keepdims=True)

        dlb = total(lb)
        hg = jnp.concatenate([total(hg0), total(hg1), total(loss)], axis=1)
        src_ref[...] = jnp.concatenate([
            total(mix0), total(mix1), total(mlp0), total(mlp1), total(fin),
            jnp.concatenate([total(attn0), total(attn1)], axis=1),
            jnp.concatenate([jac[0:1, :] * dlb, jac[1:2, :] * dlb], axis=1),
            jnp.concatenate([hg, jnp.zeros((1, SMALL_W - hg.shape[1]), F32)], axis=1)], axis=0)
        x, y, c, me = _my_position()
        buf_ref[me] = src_ref[...]
        sends = []
        for k in PEER_ORDER:
            peer, _ = _peer(x, y, c, k)
            cp = pltpu.make_async_remote_copy(src_ref=src_ref, dst_ref=buf_ref.at[me], send_sem=send_sems.at[k - 1],
                                              recv_sem=recv_sems.at[k - 1], device_id=peer, device_id_type=MESH)
            cp.start()
            sends.append(cp)
        for k in PEER_ORDER:
            peer, pid = _peer(x, y, c, k)
            pltpu.make_async_remote_copy(src_ref=src_ref, dst_ref=buf_ref.at[pid], send_sem=send_sems.at[k - 1],
                                         recv_sem=recv_sems.at[k - 1], device_id=peer, device_id_type=MESH).wait_recv()
        for cp in sends:
            cp.wait_send()
        acc = buf_ref[0]
        for dev in range(1, N_DEV):
            acc = acc + buf_ref[dev]
        o_ref[...] = acc

    assert mix8[0].shape[1] == SMALL_W
    vm = pl.BlockSpec(memory_space=pltpu.VMEM)
    return pl.pallas_call(
        body, in_specs=[vm] * 12 + [ANY_SPEC], out_specs=vm, out_shape=jax.ShapeDtypeStruct((8, SMALL_W), F32),
        scratch_shapes=[pltpu.VMEM((8, SMALL_W), F32), pltpu.VMEM((N_DEV, 8, SMALL_W), F32),
                        pltpu.SemaphoreType.DMA((N_DEV - 1,)), pltpu.SemaphoreType.DMA((N_DEV - 1,))],
        name="all_reduce_small",
    )(*mix8, *mlp8, final8, *attn8, lb8_last, lb_jac, *hg8, loss8, after)


def _resident(block_shape, index_map):
    return pl.BlockSpec(block_shape, index_map, pipeline_mode=pl.Buffered(1))


def _fwd_inproj(layer, x, gain, g_in, cos, sin, tm=512):
    s, d = x.shape

    def body(x_ref, gain_ref, w_ref, cos_ref, sin_ref, proj_ref, h_ref):
        h = _rms_fwd(x_ref[...], gain_ref[...]).astype(BF)
        h_ref[...] = h
        cs = _tile_lanes(cos_ref[...], SEG // LANES)
        sn = _tile_lanes(sin_ref[...], SEG // LANES)
        for seg in range(N_SEG):
            acc = _dot(h, w_ref[seg * SEG:(seg + 1) * SEG, :], NT)
            if seg < 2:
                acc = acc * cs + _rope_partner(acc) * sn
            if seg == 0:
                acc = acc * ATTN_SCALE
            proj_ref[:, seg * SEG:(seg + 1) * SEG] = acc

    return pl.pallas_call(
        body, grid=(s // tm,),
        in_specs=[pl.BlockSpec((tm, d), lambda i: (i, 0)), pl.BlockSpec((None, 1, d), lambda i: (layer, 0, 0)),
                  _resident((None, PROJ_W, d), lambda i: (0, 0, 0)),
                  pl.BlockSpec((tm, LANES), lambda i: (i, 0)), pl.BlockSpec((tm, LANES), lambda i: (i, 0))],
        out_specs=[pl.BlockSpec((tm, PROJ_W), lambda i: (i, 0)), pl.BlockSpec((tm, d), lambda i: (i, 0))],
        out_shape=[jax.ShapeDtypeStruct((s, PROJ_W), F32), jax.ShapeDtypeStruct((s, d), BF)],
        compiler_params=_cparams(("parallel",)), name=f"fwd_inproj_l{layer}",
    )(x, gain, g_in, cos, sin)


ATTN_UNIT = SPAN * max(DILATIONS)
ATTN_GROUP = 4


def _attn_masks(first_block_has_prev):
    row = lax.broadcasted_iota(jnp.int32, (SPAN, 2 * SPAN), 0)
    col = lax.broadcasted_iota(jnp.int32, (SPAN, 2 * SPAN), 1)
    band = (col >= row) & (col <= row + SPAN)
    lane = lax.broadcasted_iota(jnp.int32, (SPAN, LANES), 1)
    return band & ((col >= SPAN) | first_block_has_prev), band, lane < 64


def _attn_specs(n_in_extra, unit_of=lambda n: n):
    pairs = ATTN_W // LANES
    q_spec = pl.BlockSpec((ATTN_UNIT, LANES), lambda p, n: (unit_of(n), p))

    def prev(seg):
        return pl.BlockSpec((ATTN_UNIT, LANES), lambda p, n: (jnp.maximum(unit_of(n) - 1, 0), seg * pairs + p))

    def cur(seg):
        return pl.BlockSpec((ATTN_UNIT, LANES), lambda p, n: (unit_of(n), seg * pairs + p))

    return [q_spec, prev(1), cur(1), prev(2), cur(2)] + [q_spec] * n_in_extra


def _attn_groups(dil):
    blocks = ATTN_UNIT // (SPAN * dil)
    pairs = [(r, b) for r in range(dil) for b in range(blocks)]
    return [pairs[i:i + ATTN_GROUP] for i in range(0, len(pairs), ATTN_GROUP)]


def _block_rows(dil, r, b, n=1):
    start = r + dil * SPAN * b
    return pl.ds(start, n * SPAN, stride=dil) if dil > 1 else pl.ds(start, n * SPAN)


def _block_keys(prev_ref, cur_ref, dil, r, b):
    if b > 0:
        return cur_ref[_block_rows(dil, r, b - 1, 2), :]
    last = ATTN_UNIT // (SPAN * dil) - 1
    return jnp.concatenate([prev_ref[_block_rows(dil, r, last), :], cur_ref[_block_rows(dil, r, 0), :]], axis=0)


def _attn_fwd(layer, proj):
    s = proj.shape[0]
    n_pat = len(DILATIONS)
    merge_rows = 256

    def body(q_ref, kp_ref, kc_ref, vp_ref, vc_ref, o_ref, lse_ref, o_scr, lse_scr):
        m_first, m_rest, is_a = _attn_masks(pl.program_id(1) > 0)
        sels = (is_a, jnp.logical_not(is_a))
        is_a_keys = lax.broadcasted_iota(jnp.int32, (2 * SPAN, LANES), 1) < 64
        for pi, dil in enumerate(DILATIONS):
            for group in _attn_groups(dil):
                items = [(r, b, h) for r, b in group for h in range(2)]
                q = {rb: q_ref[_block_rows(dil, *rb), :] for rb in group}
                k = {rb: _block_keys(kp_ref, kc_ref, dil, *rb).astype(BF) for rb in group}
                v = {rb: _block_keys(vp_ref, vc_ref, dil, *rb).astype(BF) for rb in group}
                v_sum = {rb: (jnp.where(is_a_keys, v[rb], 1.0), jnp.where(is_a_keys, 1.0, v[rb])) for rb in group}
                sc = [jnp.where(m_first if b == 0 else m_rest,
                                _dot(jnp.where(sels[h], q[r, b], 0.0).astype(BF), k[r, b], NT), MASK_VALUE)
                      for r, b, h in items]
                mx = [jnp.max(jnp.maximum(t[:, :SPAN], t[:, SPAN:]), axis=-1, keepdims=True) for t in sc]
                p = [jnp.exp(t - m).astype(BF) for t, m in zip(sc, mx)]
                both = [_dot(t, v_sum[r, b][h], NN) for t, (r, b, h) in zip(p, items)]
                for j, (r, b) in enumerate(group):
                    t_a, t_b = both[2 * j], both[2 * j + 1]
                    den = pltpu.roll(jnp.where(is_a, t_b, t_a), 64, 1)
                    o_scr[pi, _block_rows(dil, r, b), :] = jnp.where(is_a, t_a, t_b) / den
                    lse_scr[pi, _block_rows(dil, r, b), :] = jnp.where(is_a, mx[2 * j], mx[2 * j + 1]) + jnp.log(den)
        for c in range(ATTN_UNIT // merge_rows):
            rows = slice(c * merge_rows, (c + 1) * merge_rows)
            ls = [lse_scr[pi, rows, :] for pi in range(n_pat)]
            mx = functools.reduce(jnp.maximum, ls)
            ws = [jnp.exp(l - mx) for l in ls]
            den = functools.reduce(jnp.add, ws)
            o_ref[rows, :] = functools.reduce(jnp.add, [w * o_scr[pi, rows, :] for pi, w in enumerate(ws)]) / den
            lse_ref[rows, :] = mx + jnp.log(den)

    out_spec = pl.BlockSpec((ATTN_UNIT, LANES), lambda p, n: (n, p))
    return pl.pallas_call(
        body, grid=(ATTN_W // LANES, s // ATTN_UNIT), in_specs=_attn_specs(0), out_specs=[out_spec, out_spec],
        out_shape=[jax.ShapeDtypeStruct((s, ATTN_W), F32)] * 2,
        scratch_shapes=[pltpu.VMEM((n_pat, ATTN_UNIT, LANES), F32)] * 2,
        compiler_params=_cparams(("parallel", "arbitrary")), name=f"attn_fwd_l{layer}",
    )(proj, proj, proj, proj, proj)


def _attn_norm(layer, o, gain, mixed, tm=512):
    s = o.shape[0]

    def body(o_ref, gain_ref, mixed_ref, n_ref):
        n_ref[...] = _rms_fwd(o_ref[...], gain_ref[...]).astype(BF)

    blk = pl.BlockSpec((tm, ATTN_W), lambda i: (i, 0))
    return pl.pallas_call(
        body, grid=(s // tm,),
        in_specs=[blk, pl.BlockSpec((None, 1, ATTN_W), lambda i: (layer, 0, 0)), pl.BlockSpec(memory_space=pl.ANY)],
        out_specs=blk, out_shape=jax.ShapeDtypeStruct(mixed.shape, BF), input_output_aliases={2: 0},
        compiler_params=_cparams(("parallel",)), name=f"attn_norm_l{layer}",
    )(o, gain, mixed)


def _chunk_cumsum(x, reverse=False):
    n = x.shape[0]
    pos = lax.broadcasted_iota(jnp.int32, x.shape, 0) % HGRN_CHUNK
    for sh in (1, 2, 4, 8):
        if reverse:
            x = x + jnp.where(pos < HGRN_CHUNK - sh, pltpu.roll(x, n - sh, 0), 0.0)
        else:
            x = x + jnp.where(pos >= sh, pltpu.roll(x, sh, 0), 0.0)
    return x


def _chunk_row(x, row):
    r, n = x.shape
    x3 = x.reshape(r // HGRN_CHUNK, HGRN_CHUNK, n)
    return jnp.broadcast_to(x3[:, row:row + 1, :], x3.shape).reshape(r, n)


def _hgrn_pre(qh, z, lb):
    sig = _sigmoid(z)
    f = lb + (1.0 - lb) * sig
    k = 1.0 - f
    sq = _sigmoid(qh)
    q = qh * sq * HGRN_SCALE
    g = _chunk_cumsum(jnp.log(f))
    g_mid = _chunk_row(g, HGRN_CHUNK // 2 - 1)
    g_last = _chunk_row(g, HGRN_CHUNK - 1)
    e_q, e_k = jnp.exp(g - g_mid), jnp.exp(g_mid - g)
    e_in, e_out = jnp.exp(g), jnp.exp(g_last - g)
    return dict(sig=sig, f=f, k=k, sq=sq, q=q, g_last=g_last, e_q=e_q, e_k=e_k, e_in=e_in, e_out=e_out,
                qt=q * e_q, kt=k * e_k, qg=q * e_in, kout=k * e_out)


def _hgrn_mask():
    row = lax.broadcasted_iota(jnp.int32, (LANES, LANES), 0)
    col = lax.broadcasted_iota(jnp.int32, (LANES, LANES), 1)
    return (row // HGRN_CHUNK == col // HGRN_CHUNK) & (col <= row)


def _hgrn_in_specs(layer, rev, nblk):
    def blk(b):
        return nblk - 1 - b if rev else b
    first = 3 * ATTN_W // HGRN_W
    specs = [pl.BlockSpec((HGRN_ROWS, HGRN_W), functools.partial(lambda b, seg: (blk(b), first + seg), seg=seg))
             for seg in range(4)]
    specs.append(pl.BlockSpec((None, 1, HGRN_W), lambda b: (layer, 0, 0)))
    specs.append(pl.BlockSpec((None, 1, HGRN_DIM), lambda b: (layer, 0, 0)))
    return specs, blk


def _head(x, h):
    return x[:, h * HGRN_DIM:(h + 1) * HGRN_DIM]


def _chunk(x, c):
    return x[c * HGRN_CHUNK:(c + 1) * HGRN_CHUNK]


def _sub(x, sb):
    return x[sb * LANES:(sb + 1) * LANES]


HGRN_ROWS = 256
HEADS = range(HGRN_HEADS)
SUBS = range(HGRN_ROWS // LANES)
CHUNKS = range(HGRN_ROWS // HGRN_CHUNK)


def _hgrn_fwd(layer, proj, lb, gain):
    s = proj.shape[0]
    nblk = s // HGRN_ROWS
    cpb = len(CHUNKS)

    def body(q_ref, f_ref, i_ref, g_ref, lb_ref, gain_ref, o_ref, rec_ref, st_ref, state):
        @pl.when(pl.program_id(0) == 0)
        def _():
            state[...] = jnp.zeros(state.shape, F32)

        pre = _hgrn_pre(q_ref[...], f_ref[...], lb_ref[...])
        v = i_ref[...].astype(BF)
        qt, kt, qg, kout = (pre[n].astype(BF) for n in ("qt", "kt", "qg", "kout"))
        dec = jnp.exp(pre["g_last"])
        mask = _hgrn_mask()
        a = [[jnp.where(mask, _dot(_sub(_head(qt, h), sb), _sub(_head(kt, h), sb), NT), 0.0).astype(BF) for sb in SUBS]
             for h in HEADS]
        o_intra = [[_dot(a[h][sb], _sub(_head(v, h), sb), NN) for sb in SUBS] for h in HEADS]
        update = [[_dot(_chunk(_head(v, h), c), _chunk(_head(kout, h), c), TN) for c in CHUNKS] for h in HEADS]
        for h in HEADS:
            st = state[h]
            for c in CHUNKS:
                st_ref[h, c * LANES:(c + 1) * LANES, :] = st.astype(BF)
                st = st * _head(dec, h)[c * HGRN_CHUNK:c * HGRN_CHUNK + 1, :] + update[h][c]
            state[h] = st
        inter = [[_dot(_chunk(_head(qg, h), c), st_ref[h, c * LANES:(c + 1) * LANES, :].astype(BF), NT) for c in CHUNKS]
                 for h in HEADS]
        o = [jnp.concatenate(o_intra[h], axis=0) + jnp.concatenate(inter[h], axis=0) for h in HEADS]
        o_ref[...] = jnp.concatenate(o, axis=1)
        gate = g_ref[...]
        normed = jnp.concatenate([_rms_fwd(o[h], gain_ref[...]) for h in HEADS], axis=1)
        rec_ref[...] = (normed * (gate * _sigmoid(gate))).astype(BF)

    specs, _ = _hgrn_in_specs(layer, False, nblk)
    return pl.pallas_call(
        body, grid=(nblk,), in_specs=specs,
        out_specs=[pl.BlockSpec((HGRN_ROWS, HGRN_W), lambda b: (b, 0)), pl.BlockSpec((HGRN_ROWS, HGRN_W), lambda b: (b, 1)),
                   pl.BlockSpec((HGRN_HEADS, cpb * LANES, LANES), lambda b: (0, b, 0))],
        out_shape=[jax.ShapeDtypeStruct((s, HGRN_W), F32), jax.ShapeDtypeStruct((s, MIX_W), BF),
                   jax.ShapeDtypeStruct((HGRN_HEADS, nblk * cpb * LANES, LANES), BF)],
        scratch_shapes=[pltpu.VMEM((HGRN_HEADS, LANES, LANES), F32)],
        compiler_params=_cparams(("arbitrary",)), name=f"hgrn_fwd_l{layer}",
    )(proj, proj, proj, proj, lb, gain)


def _relu2(u):
    return jnp.square(jnp.maximum(u, 0)).astype(BF)


def _mlp_fwd(layer, x, mixed, gain, g_out, g_up, g_down, tm=512):
    s, d = x.shape
    mw = mixed.shape[1]
    nblk, hs = g_up.shape[1], g_up.shape[3]

    def body(x_ref, m_ref, gain_ref, out_w_ref, up_ref, down_ref, o_ref, mid_ref, u_ref, h_ref, a_buf):
        xv = x_ref[...] + _dot(m_ref[...], out_w_ref[...], NN)
        mid_ref[...] = xv
        h = _rms_fwd(xv, gain_ref[...]).astype(BF)
        h_ref[...] = h
        for j in range(nblk):
            u = _dot(h, up_ref[j], NN)
            u_ref[:, j * hs:(j + 1) * hs] = u.astype(BF)
            a_buf[:, j * hs:(j + 1) * hs] = _relu2(u)
        acc = xv
        for j in range(nblk):
            acc = acc + _dot(a_buf[:, j * hs:(j + 1) * hs], down_ref[j * hs:(j + 1) * hs, :], NN)
        o_ref[...] = acc

    row = pl.BlockSpec((tm, d), lambda i: (i, 0))
    return pl.pallas_call(
        body, grid=(s // tm,),
        in_specs=[row, pl.BlockSpec((tm, mw), lambda i: (i, 0)), pl.BlockSpec((None, 1, d), lambda i: (layer, 0, 0)),
                  _resident((None, mw, d), lambda i: (0, 0, 0)),
                  _resident((None, nblk, d, hs), lambda i: (0, 0, 0, 0)),
                  _resident((None, nblk * hs, d), lambda i: (0, 0, 0))],
        out_specs=[row, row, pl.BlockSpec((tm, nblk * hs), lambda i: (i, 0)), row],
        out_shape=[jax.ShapeDtypeStruct((s, d), F32), jax.ShapeDtypeStruct((s, d), F32),
                   jax.ShapeDtypeStruct((s, nblk * hs), BF), jax.ShapeDtypeStruct((s, d), BF)],
        scratch_shapes=[pltpu.VMEM((tm, nblk * hs), BF)],
        compiler_params=_cparams(("parallel",)), name=f"mlp_fwd_l{layer}",
    )(x, mixed, gain, g_out, g_up, g_down)


def _loss_head(x, gain, target, tm=512):
    s, d = x.shape

    def body(x_ref, gain_ref, t_ref, dx_ref, dxb_ref, dgain_ref, loss_ref):
        i = pl.program_id(0)
        xv, gv = x_ref[...], gain_ref[...]
        err = _rms_fwd(xv, gv) - t_ref[...]
        dx, dgain = _rms_bwd(err * (1.0 / d), xv, gv)
        dx_ref[...] = dx
        dxb_ref[...] = dx.astype(BF)
        part = _part8(dgain)
        lpart = _part8(0.5 * jnp.mean(err * err, axis=-1, keepdims=True) * jnp.ones((1, LANES), F32))

        @pl.when(i == 0)
        def _():
            dgain_ref[...] = part
            loss_ref[...] = lpart

        @pl.when(i > 0)
        def _():
            dgain_ref[...] += part
            loss_ref[...] += lpart

    row = pl.BlockSpec((tm, d), lambda i: (i, 0))
    return pl.pallas_call(
        body, grid=(s // tm,),
        in_specs=[row, pl.BlockSpec((1, d), lambda i: (0, 0)), row],
        out_specs=[row, row, pl.BlockSpec((8, d), lambda i: (0, 0)), pl.BlockSpec((8, LANES), lambda i: (0, 0))],
        out_shape=[jax.ShapeDtypeStruct((s, d), F32), jax.ShapeDtypeStruct((s, d), BF), jax.ShapeDtypeStruct((8, d), F32),
                   jax.ShapeDtypeStruct((8, LANES), F32)],
        compiler_params=_cparams(("arbitrary",)), name="loss_head",
    )(x, gain, target)


def _accumulate_rows(i, ref, part):
    @pl.when(i == 0)
    def _():
        ref[...] = part

    @pl.when(i > 0)
    def _():
        ref[...] += part


def _mlp_bwd(layer, dx, dxb, x, gain, u, g_out, g_up, g_down, tm=256):
    s, d = x.shape
    mw = g_out.shape[1]
    nblk, hs = g_up.shape[1], g_up.shape[3]

    def body(dx_ref, dxb_ref, x_ref, gain_ref, u_ref, out_w_ref, up_ref, down_ref, o_ref, ob_ref, du_ref, dm_ref,
             dgain_ref):
        dxb_v = dxb_ref[...]
        for j in range(nblk):
            cols = slice(j * hs, (j + 1) * hs)
            da = _dot(dxb_v, down_ref[cols, :], NT)
            du_ref[:, cols] = (da * (2.0 * jnp.maximum(u_ref[:, cols].astype(F32), 0.0))).astype(BF)
        acc = jnp.zeros((tm, d), F32)
        for j in range(nblk):
            acc = acc + _dot(du_ref[:, j * hs:(j + 1) * hs], up_ref[j], NT)
        dxn, dgain = _rms_bwd(acc, x_ref[...], gain_ref[...])
        out = dx_ref[...] + dxn
        out_b = out.astype(BF)
        o_ref[...] = out
        ob_ref[...] = out_b
        dm_ref[...] = _dot(out_b, out_w_ref[...], NT)
        _accumulate_rows(pl.program_id(0), dgain_ref, _part8(dgain))

    row = pl.BlockSpec((tm, d), lambda i: (i, 0))
    wide = pl.BlockSpec((tm, nblk * hs), lambda i: (i, 0))
    return pl.pallas_call(
        body, grid=(s // tm,),
        in_specs=[row, row, row, pl.BlockSpec((None, 1, d), lambda i: (layer, 0, 0)), wide,
                  _resident((None, mw, d), lambda i: (0, 0, 0)),
                  _resident((None, nblk, d, hs), lambda i: (0, 0, 0, 0)),
                  _resident((None, nblk * hs, d), lambda i: (0, 0, 0))],
        out_specs=[row, row, wide, pl.BlockSpec((tm, mw), lambda i: (i, 0)), pl.BlockSpec((8, d), lambda i: (0, 0))],
        out_shape=[jax.ShapeDtypeStruct((s, d), F32), jax.ShapeDtypeStruct((s, d), BF),
                   jax.ShapeDtypeStruct((s, nblk * hs), BF), jax.ShapeDtypeStruct((s, mw), F32),
                   jax.ShapeDtypeStruct((8, d), F32)],
        compiler_params=_cparams(("arbitrary",)), name=f"mlp_bwd_l{layer}",
    )(dx, dxb, x, gain, u, g_out, g_up, g_down)


def _attn_norm_bwd(layer, dmixed, o, gain, tm=512):
    s = o.shape[0]

    def body(dm_ref, o_ref, gain_ref, do_ref, delta_ref, dgain_ref):
        i = pl.program_id(0)
        ov = o_ref[...]
        do, dgain = _rms_bwd(dm_ref[...], ov, gain_ref[...])
        do_ref[...] = do
        row = lax.broadcasted_iota(jnp.int32, (ATTN_W, ATTN_W), 0)
        col = lax.broadcasted_iota(jnp.int32, (ATTN_W, ATTN_W), 1)
        same_head = jnp.where(row // 64 == col // 64, 1.0, 0.0).astype(BF)
        prod = do * ov
        high = prod.astype(BF)
        low = (prod - high.astype(F32)).astype(BF)
        delta_ref[...] = _dot(high, same_head, NN) + _dot(low, same_head, NN)
        part = _part8(dgain)

        @pl.when(i == 0)
        def _():
            dgain_ref[...] = part

        @pl.when(i > 0)
        def _():
            dgain_ref[...] += part

    blk = pl.BlockSpec((tm, ATTN_W), lambda i: (i, 0))
    return pl.pallas_call(
        body, grid=(s // tm,), in_specs=[blk, blk, pl.BlockSpec((None, 1, ATTN_W), lambda i: (layer, 0, 0))],
        out_specs=[blk, blk, pl.BlockSpec((8, ATTN_W), lambda i: (0, 0))],
        out_shape=[jax.ShapeDtypeStruct((s, ATTN_W), F32), jax.ShapeDtypeStruct((s, ATTN_W), F32),
                   jax.ShapeDtypeStruct((8, ATTN_W), F32)],
        compiler_params=_cparams(("arbitrary",)), name=f"attn_norm_bwd_l{layer}",
    )(dmixed, o, gain)


def _attn_bwd(layer, proj, do, lse, delta, cos, sin):
    s = proj.shape[0]
    n_units = s // ATTN_UNIT
    out_rows = 256

    def unit_of(n):
        return n_units - 1 - n

    def body(q_ref, kp_ref, kc_ref, vp_ref, vc_ref, do_ref, lse_ref, delta_ref, cos_ref, sin_ref, out_ref,
             dq_ref, dk_ref, dkp_ref, dv_ref, dvp_ref, carry_k, carry_v):
        step = pl.program_id(1)
        m_first, m_rest, is_a = _attn_masks(unit_of(step) > 0)
        sels = (is_a, jnp.logical_not(is_a))
        for ref in (dq_ref, dk_ref, dkp_ref, dv_ref, dvp_ref):
            ref[...] = jnp.zeros(ref.shape, F32)
        for dil in DILATIONS:
            last = ATTN_UNIT // (SPAN * dil) - 1
            for group in _attn_groups(dil):
                items = [(r, b, h) for r, b in group for h in range(2)]
                q = {rb: q_ref[_block_rows(dil, *rb), :] for rb in group}
                dov = {rb: do_ref[_block_rows(dil, *rb), :] for rb in group}
                lse_v = {rb: lse_ref[_block_rows(dil, *rb), :] for rb in group}
                delta_v = {rb: delta_ref[_block_rows(dil, *rb), :] for rb in group}
                k = {rb: _block_keys(kp_ref, kc_ref, dil, *rb).astype(BF) for rb in group}
                v = {rb: _block_keys(vp_ref, vc_ref, dil, *rb).astype(BF) for rb in group}
                qh = [jnp.where(sels[h], q[r, b], 0.0).astype(BF) for r, b, h in items]
                doh = [jnp.where(sels[h], dov[r, b], 0.0).astype(BF) for r, b, h in items]
                sc = [jnp.where(m_first if b == 0 else m_rest, _dot(qh[i], k[r, b], NT), MASK_VALUE)
                      for i, (r, b, h) in enumerate(items)]
                p = [jnp.exp(sc[i] - lse_v[r, b][:, 64 * h:64 * h + 1]) for i, (r, b, h) in enumerate(items)]
                ds = [(p[i] * (_dot(doh[i], v[r, b], NT) - delta_v[r, b][:, 64 * h:64 * h + 1])).astype(BF)
                      for i, (r, b, h) in enumerate(items)]
                dv = [_dot(p[i].astype(BF), doh[i], TN) for i in range(len(items))]
                dq = [_dot(ds[i], k[r, b], NN) for i, (r, b, h) in enumerate(items)]
                dk = [_dot(ds[i], qh[i], TN) for i in range(len(items))]
                for j, (r, b) in enumerate(group):
                    own = _block_rows(dil, r, b)
                    dq_ref[own, :] += jnp.where(is_a, dq[2 * j], dq[2 * j + 1])
                    dk2, dv2 = dk[2 * j] + dk[2 * j + 1], dv[2 * j] + dv[2 * j + 1]
                    dk_ref[own, :] += dk2[SPAN:]
                    dv_ref[own, :] += dv2[SPAN:]
                    if b > 0:
                        before = _block_rows(dil, r, b - 1)
                        dk_ref[before, :] += dk2[:SPAN]
                        dv_ref[before, :] += dv2[:SPAN]
                    else:
                        before = _block_rows(dil, r, last)
                        dkp_ref[before, :] += dk2[:SPAN]
                        dvp_ref[before, :] += dv2[:SPAN]
        has_next = step > 0
        for c in range(ATTN_UNIT // out_rows):
            rows = slice(c * out_rows, (c + 1) * out_rows)
            cs, sn = cos_ref[rows, :], sin_ref[rows, :]
            dqv = dq_ref[rows, :]
            dkv = dk_ref[rows, :] + jnp.where(has_next, carry_k[rows, :], 0.0)
            dvv = dv_ref[rows, :] + jnp.where(has_next, carry_v[rows, :], 0.0)
            out_ref[0, rows, :] = ((dqv * cs - _rope_partner(dqv) * sn) * ATTN_SCALE).astype(BF)
            out_ref[1, rows, :] = (dkv * cs - _rope_partner(dkv) * sn).astype(BF)
            out_ref[2, rows, :] = dvv.astype(BF)
        carry_k[...] = dkp_ref[...]
        carry_v[...] = dvp_ref[...]

    tab = pl.BlockSpec((ATTN_UNIT, LANES), lambda p, n: (unit_of(n), 0))
    return pl.pallas_call(
        body, grid=(ATTN_W // LANES, n_units), in_specs=_attn_specs(3, unit_of) + [tab, tab],
        out_specs=pl.BlockSpec((3, ATTN_UNIT, LANES), lambda p, n: (0, unit_of(n), p)),
        out_shape=jax.ShapeDtypeStruct((3, s, ATTN_W), BF),
        scratch_shapes=[pltpu.VMEM((ATTN_UNIT, LANES), F32)] * 7,
        compiler_params=_cparams(("parallel", "arbitrary")), name=f"attn_bwd_l{layer}",
    )(proj, proj, proj, proj, proj, do, lse, delta, cos, sin)


def _hgrn_bwd(layer, proj, lb, gain, o, dmixed, states):
    s = proj.shape[0]
    nblk = s // HGRN_ROWS
    cpb = len(CHUNKS)

    def body(q_ref, f_ref, i_ref, g_ref, lb_ref, gain_ref, o_ref, drec_ref, st_ref, dseg_ref, dlb_ref, dgain_ref,
             dstate, dst_buf):
        step = pl.program_id(0)

        @pl.when(step == 0)
        def _():
            dstate[...] = jnp.zeros(dstate.shape, F32)

        lbv, gv = lb_ref[...], gain_ref[...]
        qh, z, gate_in = q_ref[...], f_ref[...], g_ref[...]
        pre = _hgrn_pre(qh, z, lbv)
        v = i_ref[...].astype(BF)
        sg = _sigmoid(gate_in)
        ov, drec = o_ref[...], drec_ref[...]
        dnormed = drec * (gate_in * sg)
        back = [_rms_bwd(_head(dnormed, h), _head(ov, h), gv) for h in HEADS]
        do_b = jnp.concatenate([b[0] for b in back], axis=1).astype(BF)
        dgain = back[0][1] + back[1][1] + back[2][1] + back[3][1]
        normed = jnp.concatenate([_rms_fwd(_head(ov, h), gv) for h in HEADS], axis=1)
        dgate_in = drec * normed * (sg * (1.0 + gate_in * (1.0 - sg)))
        mask = _hgrn_mask()
        qt, kt, qg, kout = (pre[n].astype(BF) for n in ("qt", "kt", "qg", "kout"))
        dec = jnp.exp(pre["g_last"])
        def intra(fn):
            return jnp.concatenate([jnp.concatenate([fn(h, sb) for sb in SUBS], axis=0) for h in HEADS], axis=1)

        def hs(x, h, sb):
            return _sub(_head(x, h), sb)

        a = [[jnp.where(mask, _dot(hs(qt, h, sb), hs(kt, h, sb), NT), 0.0).astype(BF) for sb in SUBS] for h in HEADS]
        da = [[jnp.where(mask, _dot(hs(do_b, h, sb), hs(v, h, sb), NT), 0.0).astype(BF) for sb in SUBS] for h in HEADS]
        dv_intra = intra(lambda h, sb: _dot(a[h][sb], hs(do_b, h, sb), TN))
        dqt = intra(lambda h, sb: _dot(da[h][sb], hs(kt, h, sb), NN))
        dkt = intra(lambda h, sb: _dot(da[h][sb], hs(qt, h, sb), TN))
        feed = [[_dot(_chunk(_head(do_b, h), c), _chunk(_head(qg, h), c), TN) for c in CHUNKS] for h in HEADS]
        for h in HEADS:
            dst = dstate[h]
            for c in reversed(CHUNKS):
                dst_buf[h, c * LANES:(c + 1) * LANES, :] = dst
                dst = dst * _head(dec, h)[c * HGRN_CHUNK:c * HGRN_CHUNK + 1, :] + feed[h][c]
            dstate[h] = dst

        def per_chunk(fn):
            cols = []
            for h in HEADS:
                rows = [jnp.broadcast_to(t, (HGRN_CHUNK, HGRN_DIM)) for t in (fn(h, c) for c in CHUNKS)]
                cols.append(jnp.concatenate(rows, axis=0))
            return jnp.concatenate(cols, axis=1)

        def st_prev(h, c):
            return st_ref[h, c * LANES:(c + 1) * LANES, :]

        def dst_at(h, c):
            return dst_buf[h, c * LANES:(c + 1) * LANES, :]

        dqg = per_chunk(lambda h, c: _dot(_chunk(_head(do_b, h), c), st_prev(h, c).astype(BF), NN))
        dkout = per_chunk(lambda h, c: _dot(_chunk(_head(v, h), c), dst_at(h, c).astype(BF), NN))
        dv_inter = per_chunk(lambda h, c: _dot(_chunk(_head(kout, h), c), dst_at(h, c).astype(BF), NT))
        dg_state = per_chunk(lambda h, c: jnp.sum(dst_at(h, c) * st_prev(h, c).astype(F32), axis=0, keepdims=True))
        dg_kout = per_chunk(lambda h, c: jnp.sum(_chunk(_head(dkout * pre["kout"], h), c), axis=0, keepdims=True))
        dv = dv_intra + dv_inter
        pos = lax.broadcasted_iota(jnp.int32, (HGRN_ROWS, HGRN_W), 0) % HGRN_CHUNK
        dq = dqt * pre["e_q"] + dqg * pre["e_in"]
        dk = dkt * pre["e_k"] + dkout * pre["e_out"]
        dg = (dqt * pre["qt"] - dkt * pre["kt"] + dqg * pre["qg"] - dkout * pre["kout"]
              + jnp.where(pos == HGRN_CHUNK - 1, dg_state * dec + dg_kout, 0.0))
        dlogf = _chunk_cumsum(dg, reverse=True)
        sig, sq = pre["sig"], pre["sq"]
        df = dlogf / pre["f"] - dk
        dseg_ref[0] = (dq * HGRN_SCALE * (sq * (1.0 + qh * (1.0 - sq)))).astype(BF)
        dseg_ref[1] = (df * (1.0 - lbv) * sig * (1.0 - sig)).astype(BF)
        dseg_ref[2] = dv.astype(BF)
        dseg_ref[3] = dgate_in.astype(BF)
        _accumulate_rows(step, dlb_ref, _part8(df * (1.0 - sig)))
        _accumulate_rows(step, dgain_ref, _part8(dgain))

    specs, blk = _hgrn_in_specs(layer, True, nblk)
    specs += [pl.BlockSpec((HGRN_ROWS, HGRN_W), lambda b: (blk(b), 0)),
              pl.BlockSpec((HGRN_ROWS, HGRN_W), lambda b: (blk(b), 1)),
              pl.BlockSpec((HGRN_HEADS, cpb * LANES, LANES), lambda b: (0, blk(b), 0))]
    return pl.pallas_call(
        body, grid=(nblk,), in_specs=specs,
        out_specs=[pl.BlockSpec((4, HGRN_ROWS, HGRN_W), lambda b: (0, blk(b), 0)),
                   pl.BlockSpec((8, HGRN_W), lambda b: (0, 0)), pl.BlockSpec((8, HGRN_DIM), lambda b: (0, 0))],
        out_shape=[jax.ShapeDtypeStruct((4, s, HGRN_W), BF), jax.ShapeDtypeStruct((8, HGRN_W), F32),
                   jax.ShapeDtypeStruct((8, HGRN_DIM), F32)],
        scratch_shapes=[pltpu.VMEM((HGRN_HEADS, LANES, LANES), F32), pltpu.VMEM((HGRN_HEADS, cpb * LANES, LANES), F32)],
        compiler_params=_cparams(("arbitrary",)), name=f"hgrn_bwd_l{layer}",
    )(proj, proj, proj, proj, lb, gain, o, dmixed, states)


def _bwd_inproj(layer, dqkv, dhg, g_in, x, gain, dres, tm=512):
    s, d = x.shape

    def body(dqkv_ref, dhg_ref, w_ref, x_ref, gain_ref, dres_ref, dx_ref, dxb_ref, dgain_ref):
        acc = jnp.zeros((tm, d), F32)
        for seg in range(N_SEG):
            a = dqkv_ref[seg] if seg < 3 else dhg_ref[seg - 3]
            acc = acc + _dot(a, w_ref[seg * SEG:(seg + 1) * SEG, :], NN)
        dx, dgain = _rms_bwd(acc, x_ref[...], gain_ref[...])
        out = dres_ref[...] + dx
        dx_ref[...] = out
        dxb_ref[...] = out.astype(BF)
        _accumulate_rows(pl.program_id(0), dgain_ref, _part8(dgain))

    row = pl.BlockSpec((tm, d), lambda i: (i, 0))
    return pl.pallas_call(
        body, grid=(s // tm,),
        in_specs=[pl.BlockSpec((3, tm, SEG), lambda i: (0, i, 0)), pl.BlockSpec((4, tm, SEG), lambda i: (0, i, 0)),
                  _resident((None, PROJ_W, d), lambda i: (0, 0, 0)), row,
                  pl.BlockSpec((None, 1, d), lambda i: (layer, 0, 0)), row],
        out_specs=[row, row, pl.BlockSpec((8, d), lambda i: (0, 0))],
        out_shape=[jax.ShapeDtypeStruct((s, d), F32), jax.ShapeDtypeStruct((s, d), BF), jax.ShapeDtypeStruct((8, d), F32)],
        compiler_params=_cparams(("arbitrary",)), name=f"bwd_inproj_l{layer}",
    )(dqkv, dhg, g_in, x, gain, dres)


def _adamw(w, g, m, v):
    m2 = ADAM_B1 * m + (1.0 - ADAM_B1) * g
    v2 = ADAM_B2 * v + (1.0 - ADAM_B2) * (g * g)
    m_hat = m2 / (1.0 - ADAM_B1 ** ADAM_STEP)
    v_hat = v2 / (1.0 - ADAM_B2 ** ADAM_STEP)
    delta = -ADAM_LR * (m_hat / (jnp.sqrt(v_hat) + ADAM_EPS) + ADAM_WD * w)
    return delta, m2, v2


def _adam_big(name, parts, w, m, v, row_tiles):
    depth = w.shape[0]
    r, c = parts[0].shape[1], parts[0].shape[2]
    tr = r // row_tiles
    p_spec = pl.BlockSpec((N_DEV, tr, c), lambda t: (0, t, 0))
    w_spec = pl.BlockSpec((depth, tr, c), lambda t: (0, t, 0))

    def body(*refs):
        p_refs = refs[:depth]
        w_ref, m_ref, v_ref, g_ref, d_ref, m2_ref, v2_ref, token = refs[depth:]
        token[...] = jnp.zeros(token.shape, F32)
        for l in range(depth):
            g = p_refs[l][0].astype(F32)
            for dev in range(1, N_DEV):
                g = g + p_refs[l][dev].astype(F32)
            delta, m2, v2 = _adamw(w_ref[l], g, m_ref[l], v_ref[l])
            g_ref[l] = g
            d_ref[l] = delta
            m2_ref[l] = m2
            v2_ref[l] = v2

    return pl.pallas_call(
        body, grid=(row_tiles,), in_specs=[p_spec] * depth + [w_spec] * 3,
        out_specs=[w_spec] * 4 + [pl.BlockSpec((8, LANES), lambda t: (0, 0))],
        out_shape=[jax.ShapeDtypeStruct(w.shape, F32)] * 4 + [jax.ShapeDtypeStruct((8, LANES), F32)],
        compiler_params=_cparams(("arbitrary",)), name=name,
    )(*parts, w, m, v)


def _adam_small(g, ws, ms, vs):
    n = len(ws)

    def split(row, width):
        return jnp.concatenate([row[:, :width], row[:, width:2 * width]], axis=0)

    def body(g_ref, *refs):
        ins, outs = refs[:3 * n], refs[3 * n:]
        grads = [g_ref[0:2, :], split(g_ref[5:6, :], ATTN_W), split(g_ref[6:7, :], HGRN_W), split(g_ref[7:8, :], HGRN_DIM),
                 g_ref[2:4, :], g_ref[4:5, :]]
        for i, g_i in enumerate(grads):
            delta, m2, v2 = _adamw(ins[i][...], g_i, ins[n + i][...], ins[2 * n + i][...])
            for j, val in enumerate((g_i, delta, m2, v2)):
                outs[4 * i + j][...] = val

    vm = pl.BlockSpec(memory_space=pltpu.VMEM)
    res = pl.pallas_call(
        body, in_specs=[vm] * (1 + 3 * n), out_specs=[vm] * (4 * n),
        out_shape=[jax.ShapeDtypeStruct(w.shape, F32) for w in ws for _ in range(4)], name="adam_small",
    )(g, *ws, *ms, *vs)
    return [res[4 * i:4 * i + 4] for i in range(n)]


def _lower_bounds(logits):
    def body(l_ref, lb_ref, jac_ref):
        l0, l1 = l_ref[0:1, :], l_ref[1:2, :]
        mx = jnp.maximum(l0, l1)
        e0, e1 = jnp.exp(l0 - mx), jnp.exp(l1 - mx)
        p0, p1 = e0 / (e0 + e1), e1 / (e0 + e1)
        lb_ref[0:1, :] = p0 - p0
        lb_ref[1:2, :] = (p0 + p1) - p0
        jac_ref[0:1, :] = -p0 * p1
        jac_ref[1:2, :] = p0 * p1

    vm = pl.BlockSpec(memory_space=pltpu.VMEM)
    return pl.pallas_call(body, in_specs=[vm], out_specs=[vm, vm], out_shape=[jax.ShapeDtypeStruct(logits.shape, F32)] * 2,
                          name="hgrn_lower_bounds")(logits)


def _rope_tables(s, after):
    half = 32
    inv_freq = ROPE_THETA ** (-jnp.arange(half, dtype=F32) / half)
    ang = (jnp.arange(s, dtype=jnp.int32).astype(F32) + after[0, 0])[:, None] * inv_freq[None, :]
    cos, sin = jnp.cos(ang), jnp.sin(ang)
    return jnp.concatenate([cos] * 4, axis=1), jnp.concatenate([-sin, sin, -sin, sin], axis=1)


def kernel(x, norm_mix, w_in, attn_out_gain, hgrn_lb_logits, hgrn_out_gain, w_out, norm_mlp, w_up, w_down, norm_final, loss_target, m_norm_mix, m_w_in, m_attn_out_gain, m_hgrn_lb_logits, m_hgrn_out_gain, m_w_out, m_norm_mlp, m_w_up, m_w_down, m_norm_final, v_norm_mix, v_w_in, v_attn_out_gain, v_hgrn_lb_logits, v_hgrn_out_gain, v_w_out, v_norm_mlp, v_w_up, v_w_down, v_norm_final):
    depth = w_in.shape[0]
    assert depth == 2 and x.shape[0] == 1
    s, d = x.shape[1], x.shape[2]
    x0 = x[0]
    target = loss_target[0]
    g_mix, g_attn, g_hg, g_mlp = (norm_mix[:, None, :], attn_out_gain[:, None, :], hgrn_out_gain[:, None, :],
                                  norm_mlp[:, None, :])
    lb, lb_jac = _lower_bounds(hgrn_lb_logits)
    lb3 = lb[:, None, :]

    def flip(a):
        return jnp.swapaxes(a, 1, 2)

    shards = list(_pack_weights(flip(w_in), w_out, w_up, w_down))
    w_pieces = _weight_pieces(*shards)
    w_groups = [[0], [1, 2, 3], [4], [5, 6, 7]]
    me = (4 * lax.axis_index("x") + 2 * lax.axis_index("y") + lax.axis_index("c")).astype(jnp.int32).reshape(1)
    lands = _exchange_own("all_gather_own", me, shards, w_pieces)
    w_sems, shards, lands, token = _exchange_start("all_gather_start", shards, lands, w_pieces, w_groups)

    def weights_ready(group, after):
        nonlocal shards
        idxs = w_groups[group]
        shards, got = _exchange_wait(f"all_gather_wait{group}", shards, [lands[i] for i in idxs], w_pieces,
                                     [(idxs, *w_sems[group])], after)
        return got

    cos, sin = _rope_tables(s, token)

    def tied(small_arr, tok):
        return small_arr + tok[0, 0]

    saved = []
    xl = x0
    full = [None] * depth
    for l in range(depth):
        (full_in,) = weights_ready(2 * l, cos if l == 0 else xl)
        proj, h = _fwd_inproj(l, xl, g_mix, full_in, cos, sin)
        o_attn, lse = _attn_fwd(l, proj)
        o_hg, mixed, states = _hgrn_fwd(l, proj, lb3, g_hg)
        mixed = _attn_norm(l, o_attn, g_attn, mixed)
        full_out, full_up, full_down = weights_ready(2 * l + 1, mixed)
        x_next, x_mid, u, h2 = _mlp_fwd(l, xl, mixed, g_mlp, full_out, full_up, full_down)
        saved.append((xl, proj, h, o_attn, lse, o_hg, states, mixed, x_mid, u, h2))
        full[l] = (full_in, full_out, full_up, full_down)
        xl = x_next
    dx, dxb, dnorm_final8, loss8 = _loss_head(xl, norm_final[None, :], target)

    exchanges = []

    def scatter(tag, grads, kinds):
        pieces = _grad_pieces(grads, kinds)
        own = _exchange_own(f"reduce_scatter_own_{tag}", me, grads, pieces)
        sems, grads, own, tok = _exchange_start(f"reduce_scatter_start_{tag}", grads, own, pieces, [list(range(len(pieces)))])
        exchanges.append((grads, own, pieces, sems[0]))
        return tok

    small = {}
    for l in reversed(range(depth)):
        xl, proj, h, o_attn, lse, o_hg, states, mixed, x_mid, u, h2 = saved[l]
        full_in, full_out, full_up, full_down = full[l]
        hs = full_up.shape[3]
        gw_down = _mm_tn(f"grad_w_down_l{l}", u, dxb, u.shape[1], a_fn=_relu2)
        dx_mid, dx_mid_b, du, dmixed, dmlp8 = _mlp_bwd(l, dx, dxb, x_mid, g_mlp, u, full_out, full_up, full_down)
        gw_up = _mm_tn(f"grad_w_up_l{l}", h2, du, d, out_block_w=hs)
        gw_out = _mm_tn(f"grad_w_out_l{l}", mixed, dx_mid_b, mixed.shape[1])
        g_attn_t = tied(g_attn, scatter(f"mlp_l{l}", [gw_down, gw_up, gw_out], ["rows", "up", "rows"]))
        do, delta, dattn8 = _attn_norm_bwd(l, dmixed, o_attn, g_attn_t)
        dqkv = _attn_bwd(l, proj, do, lse, delta, cos, sin)
        dhg, dlb8, dhgain8 = _hgrn_bwd(l, proj, lb3, g_hg, o_hg, dmixed, states)
        gin = _mm_tn(f"grad_w_in_qkv_l{l}", dqkv, h, PROJ_W, a_lead=True)
        gw_in = _mm_tn(f"grad_w_in_hg_l{l}", dhg, h, PROJ_W, a_lead=True, out_block_off=3, prev=gin)
        g_mix_t = tied(g_mix, scatter(f"mix_l{l}", [gw_in], ["rows"]))
        dx, dxb, dmix8 = _bwd_inproj(l, dqkv, dhg, full_in, xl, g_mix_t, dx_mid)
        small[l] = (dmix8, dattn8, dlb8, dhgain8, dmlp8)

    def scattered(name, which, after):
        grads, lands, pieces, waits = [], [], [], []
        for grads_e, own, pieces_e, (send, recv) in (exchanges[i] for i in which):
            first = len(pieces)
            pieces += [p._replace(src=p.src + len(grads)) for p in pieces_e]
            waits.append((list(range(first, first + len(pieces_e))), send, recv))
            grads += grads_e
            lands += own
        return _exchange_wait(name, grads, lands, pieces, waits, after)[1]

    down1, up1, out1, in1, down0, up0, out0 = scattered("reduce_scatter_wait_early", (0, 1, 2), dx)
    big = {
        "w_down": _adam_big("adam_w_down", [down0, down1], w_down, m_w_down, v_w_down, 4),
        "w_up": _adam_big("adam_w_up", [up0, up1], w_up, m_w_up, v_w_up, 2),
        "w_out": _adam_big("adam_w_out", [out0, out1], w_out, m_w_out, v_w_out, 1),
    }
    g_small = _all_reduce_small([small[l][0] for l in range(depth)], [small[l][4] for l in range(depth)], dnorm_final8,
                                [small[l][1] for l in range(depth)], small[depth - 1][2], lb_jac,
                                [small[l][3] for l in range(depth)], loss8, big["w_out"][4])
    loss = g_small[7, 2 * HGRN_DIM]
    row = lambda a: a[None, :]
    small_out = _adam_small(
        g_small, [norm_mix, attn_out_gain, hgrn_lb_logits, hgrn_out_gain, norm_mlp, row(norm_final)],
        [m_norm_mix, m_attn_out_gain, m_hgrn_lb_logits, m_hgrn_out_gain, m_norm_mlp, row(m_norm_final)],
        [v_norm_mix, v_attn_out_gain, v_hgrn_lb_logits, v_hgrn_out_gain, v_norm_mlp, row(v_norm_final)])
    small_out[5] = [t[0] for t in small_out[5]]
    (in0,) = scattered("reduce_scatter_wait_last", (3,), small_out[0][1])
    big["w_in"] = [flip(t) for t in _adam_big("adam_w_in", [in0, in1], flip(w_in), flip(m_w_in), flip(v_w_in), 2)[:4]]

    def gather(idx):
        mix, attn, lbl, hg, mlp, final = (t[idx] for t in small_out)
        return [mix, big["w_in"][idx], attn, lbl, hg, big["w_out"][idx], mlp, big["w_up"][idx], big["w_down"][idx], final]

    return (loss, dx[None], *gather(0), *gather(1), *gather(2), *gather(3))
```

```python
import functools
from typing import Callable, NamedTuple

import jax
import jax.numpy as jnp
from jax import lax
from jax.experimental import pallas as pl
from jax.experimental.pallas import tpu as pltpu

F32 = jnp.float32
BF = jnp.bfloat16

N_DEV = 8
ATTN_W = 512
HGRN_W = 512
HGRN_HEADS = 4
HGRN_DIM = 128
SEG = 512
N_SEG = 7
PROJ_W = N_SEG * SEG
MIX_W = ATTN_W + HGRN_W
SPAN = 128
DILATIONS = (1, 4, 16)
HGRN_CHUNK = 16
ROPE_THETA = 10000.0
NORM_EPS = 1e-6
MASK_VALUE = -1e30
ATTN_SCALE = 0.125
HGRN_SCALE = HGRN_DIM ** -0.5
ADAM_LR = 0.001
ADAM_B1 = 0.9
ADAM_B2 = 0.999
ADAM_EPS = 1e-08
ADAM_WD = 0.01
ADAM_STEP = 10
LANES = 128
VMEM_LIMIT = 56 * 1024 * 1024

NN = ((1,), (0,))
NT = ((1,), (1,))
TN = ((0,), (0,))
MESH = pl.DeviceIdType.MESH


def _dot(a, b, dims):
    return lax.dot_general(a, b, (dims, ((), ())), preferred_element_type=F32)


def _cparams(sem):
    return pltpu.CompilerParams(dimension_semantics=sem, vmem_limit_bytes=VMEM_LIMIT)


def _part8(x):
    r, n = x.shape
    return jnp.sum(x.reshape(r // 8, 8, n), axis=0)


def _sigmoid(x):
    return 1.0 / (1.0 + jnp.exp(-x))


def _rms_fwd(x, gain):
    r = lax.rsqrt(jnp.mean(x * x, axis=-1, keepdims=True) + NORM_EPS)
    return x * r * gain


def _rms_bwd(dy, x, gain):
    r = lax.rsqrt(jnp.mean(x * x, axis=-1, keepdims=True) + NORM_EPS)
    xn = x * r
    dxn = dy * gain
    dx = r * (dxn - xn * jnp.mean(dxn * xn, axis=-1, keepdims=True))
    return dx, dy * xn


def _rope_partner(x):
    n = x.shape[-1]
    lane = lax.broadcasted_iota(jnp.int32, x.shape, x.ndim - 1)
    return jnp.where((lane % 64) < 32, pltpu.roll(x, n - 32, x.ndim - 1), pltpu.roll(x, 32, x.ndim - 1))


def _tile_lanes(t, reps):
    return jnp.concatenate([t] * reps, axis=-1)


def _mm_tn(name, a, b, out_rows, a_lead=False, out_block_off=0, prev=None, out_block_w=None, a_fn=None,
           tm=512, tn=1024, sub=512):
    kdim, n = b.shape
    m = a.shape[-1]
    tm, tn, sub = min(tm, m), min(tn, n), min(sub, kdim)
    mt = m // tm
    n_lead = a.shape[0] if a_lead else 1
    if a_lead:
        a_spec = pl.BlockSpec((None, kdim, tm), lambda j, i: (i // mt, 0, i % mt))
    else:
        a_spec = pl.BlockSpec((kdim, tm), lambda j, i: (0, i))
    b_spec = pl.BlockSpec((kdim, tn), lambda j, i: (0, j))
    if out_block_w:
        nb = tn // out_block_w
        o_shape = jax.ShapeDtypeStruct((n // out_block_w, out_rows, out_block_w), BF)
        o_spec = pl.BlockSpec((nb, tm, out_block_w), lambda j, i: (j, i + out_block_off, 0))
    else:
        nb = 0
        o_shape = jax.ShapeDtypeStruct((out_rows, n), BF)
        o_spec = pl.BlockSpec((tm, tn), lambda j, i: (i + out_block_off, j))
    arrays, specs, aliases = [a, b], [a_spec, b_spec], {}
    if prev is not None:
        arrays.append(prev)
        specs.append(pl.BlockSpec(memory_space=pl.ANY))
        aliases = {2: 0}

    def body(*refs):
        a_ref, b_ref, o_ref = refs[0], refs[1], refs[-1]
        acc = None
        for k in range(kdim // sub):
            av = a_ref[k * sub:(k + 1) * sub, :]
            if a_fn is not None:
                av = a_fn(av)
            part = _dot(av, b_ref[k * sub:(k + 1) * sub, :], TN)
            acc = part if acc is None else acc + part
        if nb:
            for t in range(nb):
                o_ref[t] = acc[:, t * out_block_w:(t + 1) * out_block_w].astype(BF)
        else:
            o_ref[...] = acc.astype(BF)

    return pl.pallas_call(
        body, grid=(n // tn, n_lead * mt), in_specs=specs, out_specs=o_spec, out_shape=o_shape,
        compiler_params=_cparams(("parallel", "parallel")), name=name, input_output_aliases=aliases,
    )(*arrays)


def _pack_weights(w_in_t, w_out, w_up, w_down):
    depth = w_in_t.shape[0]
    arrays = (w_in_t, w_out, w_up, w_down)

    def body(*refs):
        for src, dst in zip(refs[:4], refs[4:]):
            dst[...] = src[...].astype(BF)

    specs = [pl.BlockSpec((None,) + a.shape[1:], lambda l: (l, 0, 0)) for a in arrays]
    return pl.pallas_call(
        body, grid=(depth,), in_specs=specs, out_specs=specs,
        out_shape=[jax.ShapeDtypeStruct(a.shape, BF) for a in arrays],
        compiler_params=_cparams(("arbitrary",)), name="pack_weights",
    )(*arrays)


def _my_position():
    x, y, c = lax.axis_index("x"), lax.axis_index("y"), lax.axis_index("c")
    return x, y, c, 4 * x + 2 * y + c


def _peer(x, y, c, k):
    px = 1 - x if k & 4 else x
    py = 1 - y if k & 2 else y
    pc = 1 - c if k & 1 else c
    return (px, py, pc), 4 * px + 2 * py + pc


PEER_ORDER = (1, 2, 4, 3, 5, 6, 7)


class _Piece(NamedTuple):
    src: int
    send: Callable
    slot: Callable
    land_shape: tuple
    own_src: tuple
    own_slot: tuple


HBM_SPEC = pl.BlockSpec(memory_space=pltpu.HBM)
SEM_SPEC = pl.BlockSpec(memory_space=pltpu.SEMAPHORE)
ANY_SPEC = pl.BlockSpec(memory_space=pl.ANY)


def _in_hbm(arrays):
    return [pltpu.with_memory_space_constraint(a, pltpu.HBM) for a in arrays]


def _hbm_like(arrays):
    return [pltpu.HBM(a.shape, a.dtype) for a in arrays]


def _rows_of(rows):
    return lambda ref, dev: ref.at[pl.ds(pl.multiple_of(dev * rows, 16), rows), :]


def _exchange_own(name, me, srcs, pieces):
    n = len(pieces)

    def body(me_ref, *refs):
        for i in range(n):
            refs[n + i][...] = refs[i][...]

    def spec(block_and_index):
        block, index = block_and_index
        return pl.BlockSpec(block, lambda i, me_ref: index(me_ref[0]))

    return pl.pallas_call(
        body,
        grid_spec=pltpu.PrefetchScalarGridSpec(
            num_scalar_prefetch=1, grid=(1,), in_specs=[spec(p.own_src) for p in pieces],
            out_specs=[spec(p.own_slot) for p in pieces]),
        out_shape=[jax.ShapeDtypeStruct(p.land_shape, BF) for p in pieces],
        compiler_params=_cparams(("arbitrary",)), name=name,
    )(me, *[srcs[p.src] for p in pieces])


def _exchange_start(name, srcs, lands, pieces, groups):
    n_src, n, n_g = len(srcs), len(pieces), len(groups)

    def body(*refs):
        src_refs, land_refs = refs[:n_src], refs[n_src:n_src + n]
        sems, token = refs[n_src + n:n_src + n + 2 * n_g], refs[-1]
        x, y, c, me = _my_position()
        for g, idxs in enumerate(groups):
            for k in PEER_ORDER:
                peer, pid = _peer(x, y, c, k)
                for j, i in enumerate(idxs):
                    p = pieces[i]
                    pltpu.make_async_remote_copy(
                        src_ref=p.send(src_refs[p.src], pid), dst_ref=p.slot(land_refs[i], me),
                        send_sem=sems[2 * g].at[(k - 1) * len(idxs) + j], recv_sem=sems[2 * g + 1].at[(k - 1) * len(idxs) + j],
                        device_id=peer, device_id_type=MESH).start()
        token[...] = jnp.zeros(token.shape, F32)

    sem_shapes = [pltpu.SemaphoreType.DMA(((N_DEV - 1) * len(idxs),)) for idxs in groups for _ in range(2)]
    res = pl.pallas_call(
        body, in_specs=[HBM_SPEC] * (n_src + n),
        out_specs=[SEM_SPEC] * (2 * n_g) + [HBM_SPEC] * (n_src + n) + [pl.BlockSpec(memory_space=pltpu.VMEM)],
        out_shape=sem_shapes + _hbm_like(srcs) + _hbm_like(lands) + [jax.ShapeDtypeStruct((8, LANES), F32)],
        input_output_aliases={i: 2 * n_g + i for i in range(n_src + n)},
        compiler_params=pltpu.CompilerParams(has_side_effects=pltpu.SideEffectType.DATAFLOW_SIDE_EFFECTING),
        name=name,
    )(*_in_hbm(srcs), *_in_hbm(lands))
    sems = [(res[2 * g], res[2 * g + 1]) for g in range(n_g)]
    return sems, list(res[2 * n_g:2 * n_g + n_src]), list(res[2 * n_g + n_src:2 * n_g + n_src + n]), res[-1]


def _exchange_wait(name, srcs, lands, pieces, waits, after):
    n_src, n, n_g = len(srcs), len(lands), len(waits)

    def body(*refs):
        src_refs, land_refs = refs[:n_src], refs[n_src:n_src + n]
        sems = refs[n_src + n:n_src + n + 2 * n_g]
        x, y, c, me = _my_position()
        at = 0
        for g, (idxs, _, _) in enumerate(waits):
            for k in PEER_ORDER:
                peer, pid = _peer(x, y, c, k)
                for j, i in enumerate(idxs):
                    p = pieces[i]
                    cp = pltpu.make_async_remote_copy(
                        src_ref=p.send(src_refs[p.src], pid), dst_ref=p.slot(land_refs[at + j], pid),
                        send_sem=sems[2 * g].at[(k - 1) * len(idxs) + j], recv_sem=sems[2 * g + 1].at[(k - 1) * len(idxs) + j],
                        device_id=peer, device_id_type=MESH)
                    cp.wait_send()
                    cp.wait_recv()
            at += len(idxs)

    sem_args = [s for _, send, recv in waits for s in (send, recv)]
    res = pl.pallas_call(
        body, in_specs=[HBM_SPEC] * (n_src + n) + [SEM_SPEC] * (2 * n_g) + [ANY_SPEC],
        out_specs=[HBM_SPEC] * (n_src + n), out_shape=_hbm_like(srcs) + _hbm_like(lands),
        input_output_aliases={i: i for i in range(n_src + n)},
        compiler_params=pltpu.CompilerParams(has_side_effects=pltpu.SideEffectType.DATAFLOW_SIDE_EFFECTING),
        name=name,
    )(*srcs, *lands, *sem_args, after)
    return list(res[:n_src]), list(res[n_src:])


def _weight_pieces(p_in, p_out, p_up, p_down):
    depth, cin, d = p_in.shape
    rout, hs = p_out.shape[1], p_up.shape[2]
    pieces = []
    for l in range(depth):
        whole = functools.partial(lambda ref, dev, l: ref.at[l], l=l)
        layer = functools.partial(lambda dev, l: (l, 0, 0), l=l)

        def rows(src, n_rows, whole=whole, layer=layer):
            return _Piece(src, whole, lambda ref, dev: _rows_of(n_rows)(ref.at[0], dev), (1, N_DEV * n_rows, d),
                          ((None, n_rows, d), layer), ((None, n_rows, d), lambda dev: (0, dev, 0)))

        pieces += [
            rows(0, cin), rows(1, rout),
            _Piece(2, whole, lambda ref, dev: ref.at[0, dev], (1, N_DEV, d, hs),
                   ((None, d, hs), layer), ((None, None, d, hs), lambda dev: (0, dev, 0, 0))),
            rows(3, hs),
        ]
    return pieces


def _grad_pieces(g_pair, kinds):
    pieces = []
    for i, (g, kind) in enumerate(zip(g_pair, kinds)):
        lead = lambda dev: (dev, 0, 0)
        if kind == "up":
            blk = ((None,) + g.shape[1:], lead)
            pieces.append(_Piece(i, lambda ref, dev: ref.at[dev], lambda ref, dev: ref.at[dev], g.shape, blk, blk))
        else:
            rows, cols = g.shape[0] // N_DEV, g.shape[1]
            pieces.append(_Piece(i, _rows_of(rows), lambda ref, dev: ref.at[dev], (N_DEV, rows, cols),
                                 ((rows, cols), lambda dev: (dev, 0)), ((None, rows, cols), lead)))
    return pieces


SMALL_W = 1024


def _all_reduce_small(mix8, mlp8, final8, attn8, lb8_last, lb_jac, hg8, loss8, after):
    def body(mix0, mix1, mlp0, mlp1, fin, attn0, attn1, lb, jac, hg0, hg1, loss, after_ref, o_ref, src_ref, buf_ref,
             send_sems, recv_sems):
        def total(ref):
            return jnp.sum(ref[...], axis=0, keepdims=True)

        dlb = total(lb)
        hg = jnp.concatenate([total(hg0), total(hg1), total(loss)], axis=1)
        src_ref[...] = jnp.concatenate([
            total(mix0), total(mix1), total(mlp0), total(mlp1), total(fin),
            jnp.concatenate([total(attn0), total(attn1)], axis=1),
            jnp.concatenate([jac[0:1, :] * dlb, jac[1:2, :] * dlb], axis=1),
            jnp.concatenate([hg, jnp.zeros((1, SMALL_W - hg.shape[1]), F32)], axis=1)], axis=0)
        x, y, c, me = _my_position()
        buf_ref[me] = src_ref[...]
        sends = []
        for k in PEER_ORDER:
            peer, _ = _peer(x, y, c, k)
            cp = pltpu.make_async_remote_copy(src_ref=src_ref, dst_ref=buf_ref.at[me], send_sem=send_sems.at[k - 1],
                                              recv_sem=recv_sems.at[k - 1], device_id=peer, device_id_type=MESH)
            cp.start()
            sends.append(cp)
        for k in PEER_ORDER:
            peer, pid = _peer(x, y, c, k)
            pltpu.make_async_remote_copy(src_ref=src_ref, dst_ref=buf_ref.at[pid], send_sem=send_sems.at[k - 1],
                                         recv_sem=recv_sems.at[k - 1], device_id=peer, device_id_type=MESH).wait_recv()
        for cp in sends:
            cp.wait_send()
        acc = buf_ref[0]
        for dev in range(1, N_DEV):
            acc = acc + buf_ref[dev]
        o_ref[...] = acc

    assert mix8[0].shape[1] == SMALL_W
    vm = pl.BlockSpec(memory_space=pltpu.VMEM)
    return pl.pallas_call(
        body, in_specs=[vm] * 12 + [ANY_SPEC], out_specs=vm, out_shape=jax.ShapeDtypeStruct((8, SMALL_W), F32),
        scratch_shapes=[pltpu.VMEM((8, SMALL_W), F32), pltpu.VMEM((N_DEV, 8, SMALL_W), F32),
                        pltpu.SemaphoreType.DMA((N_DEV - 1,)), pltpu.SemaphoreType.DMA((N_DEV - 1,))],
        name="all_reduce_small",
    )(*mix8, *mlp8, final8, *attn8, lb8_last, lb_jac, *hg8, loss8, after)


def _resident(block_shape, index_map):
    return pl.BlockSpec(block_shape, index_map, pipeline_mode=pl.Buffered(1))


def _fwd_inproj(layer, x, gain, g_in, cos, sin, tm=512):
    s, d = x.shape

    def body(x_ref, gain_ref, w_ref, cos_ref, sin_ref, proj_ref, h_ref):
        h = _rms_fwd(x_ref[...], gain_ref[...]).astype(BF)
        h_ref[...] = h
        cs = _tile_lanes(cos_ref[...], SEG // LANES)
        sn = _tile_lanes(sin_ref[...], SEG // LANES)
        for seg in range(N_SEG):
            acc = _dot(h, w_ref[seg * SEG:(seg + 1) * SEG, :], NT)
            if seg < 2:
                acc = acc * cs + _rope_partner(acc) * sn
            if seg == 0:
                acc = acc * ATTN_SCALE
            proj_ref[:, seg * SEG:(seg + 1) * SEG] = acc

    return pl.pallas_call(
        body, grid=(s // tm,),
        in_specs=[pl.BlockSpec((tm, d), lambda i: (i, 0)), pl.BlockSpec((None, 1, d), lambda i: (layer, 0, 0)),
                  _resident((None, PROJ_W, d), lambda i: (0, 0, 0)),
                  pl.BlockSpec((tm, LANES), lambda i: (i, 0)), pl.BlockSpec((tm, LANES), lambda i: (i, 0))],
        out_specs=[pl.BlockSpec((tm, PROJ_W), lambda i: (i, 0)), pl.BlockSpec((tm, d), lambda i: (i, 0))],
        out_shape=[jax.ShapeDtypeStruct((s, PROJ_W), F32), jax.ShapeDtypeStruct((s, d), BF)],
        compiler_params=_cparams(("parallel",)), name=f"fwd_inproj_l{layer}",
    )(x, gain, g_in, cos, sin)


ATTN_UNIT = SPAN * max(DILATIONS)
ATTN_GROUP = 4


def _attn_masks(first_block_has_prev):
    row = lax.broadcasted_iota(jnp.int32, (SPAN, 2 * SPAN), 0)
    col = lax.broadcasted_iota(jnp.int32, (SPAN, 2 * SPAN), 1)
    band = (col >= row) & (col <= row + SPAN)
    lane = lax.broadcasted_iota(jnp.int32, (SPAN, LANES), 1)
    return band & ((col >= SPAN) | first_block_has_prev), band, lane < 64


def _attn_specs(n_in_extra, unit_of=lambda n: n):
    pairs = ATTN_W // LANES
    q_spec = pl.BlockSpec((ATTN_UNIT, LANES), lambda p, n: (unit_of(n), p))

    def prev(seg):
        return pl.BlockSpec((ATTN_UNIT, LANES), lambda p, n: (jnp.maximum(unit_of(n) - 1, 0), seg * pairs + p))

    def cur(seg):
        return pl.BlockSpec((ATTN_UNIT, LANES), lambda p, n: (unit_of(n), seg * pairs + p))

    return [q_spec, prev(1), cur(1), prev(2), cur(2)] + [q_spec] * n_in_extra


def _attn_groups(dil):
    blocks = ATTN_UNIT // (SPAN * dil)
    pairs = [(r, b) for r in range(dil) for b in range(blocks)]
    return [pairs[i:i + ATTN_GROUP] for i in range(0, len(pairs), ATTN_GROUP)]


def _block_rows(dil, r, b, n=1):
    start = r + dil * SPAN * b
    return pl.ds(start, n * SPAN, stride=dil) if dil > 1 else pl.ds(start, n * SPAN)


def _block_keys(prev_ref, cur_ref, dil, r, b):
    if b > 0:
        return cur_ref[_block_rows(dil, r, b - 1, 2), :]
    last = ATTN_UNIT // (SPAN * dil) - 1
    return jnp.concatenate([prev_ref[_block_rows(dil, r, last), :], cur_ref[_block_rows(dil, r, 0), :]], axis=0)


def _attn_fwd(layer, proj):
    s = proj.shape[0]
    n_pat = len(DILATIONS)
    merge_rows = 256

    def body(q_ref, kp_ref, kc_ref, vp_ref, vc_ref, o_ref, lse_ref, o_scr, lse_scr):
        m_first, m_rest, is_a = _attn_masks(pl.program_id(1) > 0)
        sels = (is_a, jnp.logical_not(is_a))
        is_a_keys = lax.broadcasted_iota(jnp.int32, (2 * SPAN, LANES), 1) < 64
        for pi, dil in enumerate(DILATIONS):
            for group in _attn_groups(dil):
                items = [(r, b, h) for r, b in group for h in range(2)]
                q = {rb: q_ref[_block_rows(dil, *rb), :] for rb in group}
                k = {rb: _block_keys(kp_ref, kc_ref, dil, *rb).astype(BF) for rb in group}
                v = {rb: _block_keys(vp_ref, vc_ref, dil, *rb).astype(BF) for rb in group}
                v_sum = {rb: (jnp.where(is_a_keys, v[rb], 1.0), jnp.where(is_a_keys, 1.0, v[rb])) for rb in group}
                sc = [jnp.where(m_first if b == 0 else m_rest,
                                _dot(jnp.where(sels[h], q[r, b], 0.0).astype(BF), k[r, b], NT), MASK_VALUE)
                      for r, b, h in items]
                mx = [jnp.max(jnp.maximum(t[:, :SPAN], t[:, SPAN:]), axis=-1, keepdims=True) for t in sc]
                p = [jnp.exp(t - m).astype(BF) for t, m in zip(sc, mx)]
                both = [_dot(t, v_sum[r, b][h], NN) for t, (r, b, h) in zip(p, items)]
                for j, (r, b) in enumerate(group):
                    t_a, t_b = both[2 * j], both[2 * j + 1]
                    den = pltpu.roll(jnp.where(is_a, t_b, t_a), 64, 1)
                    o_scr[pi, _block_rows(dil, r, b), :] = jnp.where(is_a, t_a, t_b) / den
                    lse_scr[pi, _block_rows(dil, r, b), :] = jnp.where(is_a, mx[2 * j], mx[2 * j + 1]) + jnp.log(den)
        for c in range(ATTN_UNIT // merge_rows):
            rows = slice(c * merge_rows, (c + 1) * merge_rows)
            ls = [lse_scr[pi, rows, :] for pi in range(n_pat)]
            mx = functools.reduce(jnp.maximum, ls)
            ws = [jnp.exp(l - mx) for l in ls]
            den = functools.reduce(jnp.add, ws)
            o_ref[rows, :] = functools.reduce(jnp.add, [w * o_scr[pi, rows, :] for pi, w in enumerate(ws)]) / den
            lse_ref[rows, :] = mx + jnp.log(den)

    out_spec = pl.BlockSpec((ATTN_UNIT, LANES), lambda p, n: (n, p))
    return pl.pallas_call(
        body, grid=(ATTN_W // LANES, s // ATTN_UNIT), in_specs=_attn_specs(0), out_specs=[out_spec, out_spec],
        out_shape=[jax.ShapeDtypeStruct((s, ATTN_W), F32)] * 2,
        scratch_shapes=[pltpu.VMEM((n_pat, ATTN_UNIT, LANES), F32)] * 2,
        compiler_params=_cparams(("parallel", "arbitrary")), name=f"attn_fwd_l{layer}",
    )(proj, proj, proj, proj, proj)


def _attn_norm(layer, o, gain, mixed, tm=512):
    s = o.shape[0]

    def body(o_ref, gain_ref, mixed_ref, n_ref):
        n_ref[...] = _rms_fwd(o_ref[...], gain_ref[...]).astype(BF)

    blk = pl.BlockSpec((tm, ATTN_W), lambda i: (i, 0))
    return pl.pallas_call(
        body, grid=(s // tm,),
        in_specs=[blk, pl.BlockSpec((None, 1, ATTN_W), lambda i: (layer, 0, 0)), pl.BlockSpec(memory_space=pl.ANY)],
        out_specs=blk, out_shape=jax.ShapeDtypeStruct(mixed.shape, BF), input_output_aliases={2: 0},
        compiler_params=_cparams(("parallel",)), name=f"attn_norm_l{layer}",
    )(o, gain, mixed)


def _chunk_cumsum(x, reverse=False):
    n = x.shape[0]
    pos = lax.broadcasted_iota(jnp.int32, x.shape, 0) % HGRN_CHUNK
    for sh in (1, 2, 4, 8):
        if reverse:
            x = x + jnp.where(pos < HGRN_CHUNK - sh, pltpu.roll(x, n - sh, 0), 0.0)
        else:
            x = x + jnp.where(pos >= sh, pltpu.roll(x, sh, 0), 0.0)
    return x


def _chunk_row(x, row):
    r, n = x.shape
    x3 = x.reshape(r // HGRN_CHUNK, HGRN_CHUNK, n)
    return jnp.broadcast_to(x3[:, row:row + 1, :], x3.shape).reshape(r, n)


def _hgrn_pre(qh, z, lb):
    sig = _sigmoid(z)
    f = lb + (1.0 - lb) * sig
    k = 1.0 - f
    sq = _sigmoid(qh)
    q = qh * sq * HGRN_SCALE
    g = _chunk_cumsum(jnp.log(f))
    g_mid = _chunk_row(g, HGRN_CHUNK // 2 - 1)
    g_last = _chunk_row(g, HGRN_CHUNK - 1)
    e_q, e_k = jnp.exp(g - g_mid), jnp.exp(g_mid - g)
    e_in, e_out = jnp.exp(g), jnp.exp(g_last - g)
    return dict(sig=sig, f=f, k=k, sq=sq, q=q, g_last=g_last, e_q=e_q, e_k=e_k, e_in=e_in, e_out=e_out,
                qt=q * e_q, kt=k * e_k, qg=q * e_in, kout=k * e_out)


def _hgrn_mask():
    row = lax.broadcasted_iota(jnp.int32, (LANES, LANES), 0)
    col = lax.broadcasted_iota(jnp.int32, (LANES, LANES), 1)
    return (row // HGRN_CHUNK == col // HGRN_CHUNK) & (col <= row)


def _hgrn_in_specs(layer, rev, nblk):
    def blk(b):
        return nblk - 1 - b if rev else b
    first = 3 * ATTN_W // HGRN_W
    specs = [pl.BlockSpec((HGRN_ROWS, HGRN_W), functools.partial(lambda b, seg: (blk(b), first + seg), seg=seg))
             for seg in range(4)]
    specs.append(pl.BlockSpec((None, 1, HGRN_W), lambda b: (layer, 0, 0)))
    specs.append(pl.BlockSpec((None, 1, HGRN_DIM), lambda b: (layer, 0, 0)))
    return specs, blk


def _head(x, h):
    return x[:, h * HGRN_DIM:(h + 1) * HGRN_DIM]


def _chunk(x, c):
    return x[c * HGRN_CHUNK:(c + 1) * HGRN_CHUNK]


def _sub(x, sb):
    return x[sb * LANES:(sb + 1) * LANES]


HGRN_ROWS = 512
HEADS = range(HGRN_HEADS)
SUBS = range(HGRN_ROWS // LANES)
CHUNKS = range(HGRN_ROWS // HGRN_CHUNK)


def _hgrn_fwd(layer, proj, lb, gain):
    s = proj.shape[0]
    nblk = s // HGRN_ROWS
    cpb = len(CHUNKS)

    def body(q_ref, f_ref, i_ref, g_ref, lb_ref, gain_ref, o_ref, rec_ref, st_ref, state):
        @pl.when(pl.program_id(0) == 0)
        def _():
            state[...] = jnp.zeros(state.shape, F32)

        pre = _hgrn_pre(q_ref[...], f_ref[...], lb_ref[...])
        v = i_ref[...].astype(BF)
        qt, kt, qg, kout = (pre[n].astype(BF) for n in ("qt", "kt", "qg", "kout"))
        dec = jnp.exp(pre["g_last"])
        mask = _hgrn_mask()
        a = [[jnp.where(mask, _dot(_sub(_head(qt, h), sb), _sub(_head(kt, h), sb), NT), 0.0).astype(BF) for sb in SUBS]
             for h in HEADS]
        o_intra = [[_dot(a[h][sb], _sub(_head(v, h), sb), NN) for sb in SUBS] for h in HEADS]
        update = [[_dot(_chunk(_head(v, h), c), _chunk(_head(kout, h), c), TN) for c in CHUNKS] for h in HEADS]
        for h in HEADS:
            st = state[h]
            for c in CHUNKS:
                st_ref[h, c * LANES:(c + 1) * LANES, :] = st.astype(BF)
                st = st * _head(dec, h)[c * HGRN_CHUNK:c * HGRN_CHUNK + 1, :] + update[h][c]
            state[h] = st
        inter = [[_dot(_chunk(_head(qg, h), c), st_ref[h, c * LANES:(c + 1) * LANES, :].astype(BF), NT) for c in CHUNKS]
                 for h in HEADS]
        o = [jnp.concatenate(o_intra[h], axis=0) + jnp.concatenate(inter[h], axis=0) for h in HEADS]
        o_ref[...] = jnp.concatenate(o, axis=1)
        gate = g_ref[...]
        normed = jnp.concatenate([_rms_fwd(o[h], gain_ref[...]) for h in HEADS], axis=1)
        rec_ref[...] = (normed * (gate * _sigmoid(gate))).astype(BF)

    specs, _ = _hgrn_in_specs(layer, False, nblk)
    return pl.pallas_call(
        body, grid=(nblk,), in_specs=specs,
        out_specs=[pl.BlockSpec((HGRN_ROWS, HGRN_W), lambda b: (b, 0)), pl.BlockSpec((HGRN_ROWS, HGRN_W), lambda b: (b, 1)),
                   pl.BlockSpec((HGRN_HEADS, cpb * LANES, LANES), lambda b: (0, b, 0))],
        out_shape=[jax.ShapeDtypeStruct((s, HGRN_W), F32), jax.ShapeDtypeStruct((s, MIX_W), BF),
                   jax.ShapeDtypeStruct((HGRN_HEADS, nblk * cpb * LANES, LANES), BF)],
        scratch_shapes=[pltpu.VMEM((HGRN_HEADS, LANES, LANES), F32)],
        compiler_params=_cparams(("arbitrary",)), name=f"hgrn_fwd_l{layer}",
    )(proj, proj, proj, proj, lb, gain)


def _relu2(u):
    return jnp.square(jnp.maximum(u, 0)).astype(BF)


def _mlp_fwd(layer, x, mixed, gain, g_out, g_up, g_down, tm=512):
    s, d = x.shape
    mw = mixed.shape[1]
    nblk, hs = g_up.shape[1], g_up.shape[3]

    def body(x_ref, m_ref, gain_ref, out_w_ref, up_ref, down_ref, o_ref, mid_ref, u_ref, h_ref, a_buf):
        xv = x_ref[...] + _dot(m_ref[...], out_w_ref[...], NN)
        mid_ref[...] = xv
        h = _rms_fwd(xv, gain_ref[...]).astype(BF)
        h_ref[...] = h
        for j in range(nblk):
            u = _dot(h, up_ref[j], NN)
            u_ref[:, j * hs:(j + 1) * hs] = u.astype(BF)
            a_buf[:, j * hs:(j + 1) * hs] = _relu2(u)
        acc = xv
        for j in range(nblk):
            acc = acc + _dot(a_buf[:, j * hs:(j + 1) * hs], down_ref[j * hs:(j + 1) * hs, :], NN)
        o_ref[...] = acc

    row = pl.BlockSpec((tm, d), lambda i: (i, 0))
    return pl.pallas_call(
        body, grid=(s // tm,),
        in_specs=[row, pl.BlockSpec((tm, mw), lambda i: (i, 0)), pl.BlockSpec((None, 1, d), lambda i: (layer, 0, 0)),
                  _resident((None, mw, d), lambda i: (0, 0, 0)),
                  _resident((None, nblk, d, hs), lambda i: (0, 0, 0, 0)),
                  _resident((None, nblk * hs, d), lambda i: (0, 0, 0))],
        out_specs=[row, row, pl.BlockSpec((tm, nblk * hs), lambda i: (i, 0)), row],
        out_shape=[jax.ShapeDtypeStruct((s, d), F32), jax.ShapeDtypeStruct((s, d), F32),
                   jax.ShapeDtypeStruct((s, nblk * hs), BF), jax.ShapeDtypeStruct((s, d), BF)],
        scratch_shapes=[pltpu.VMEM((tm, nblk * hs), BF)],
        compiler_params=_cparams(("parallel",)), name=f"mlp_fwd_l{layer}",
    )(x, mixed, gain, g_out, g_up, g_down)


def _loss_head(x, gain, target, tm=512):
    s, d = x.shape

    def body(x_ref, gain_ref, t_ref, dx_ref, dxb_ref, dgain_ref, loss_ref):
        i = pl.program_id(0)
        xv, gv = x_ref[...], gain_ref[...]
        err = _rms_fwd(xv, gv) - t_ref[...]
        dx, dgain = _rms_bwd(err * (1.0 / d), xv, gv)
        dx_ref[...] = dx
        dxb_ref[...] = dx.astype(BF)
        part = _part8(dgain)
        lpart = _part8(0.5 * jnp.mean(err * err, axis=-1, keepdims=True) * jnp.ones((1, LANES), F32))

        @pl.when(i == 0)
        def _():
            dgain_ref[...] = part
            loss_ref[...] = lpart

        @pl.when(i > 0)
        def _():
            dgain_ref[...] += part
            loss_ref[...] += lpart

    row = pl.BlockSpec((tm, d), lambda i: (i, 0))
    return pl.pallas_call(
        body, grid=(s // tm,),
        in_specs=[row, pl.BlockSpec((1, d), lambda i: (0, 0)), row],
        out_specs=[row, row, pl.BlockSpec((8, d), lambda i: (0, 0)), pl.BlockSpec((8, LANES), lambda i: (0, 0))],
        out_shape=[jax.ShapeDtypeStruct((s, d), F32), jax.ShapeDtypeStruct((s, d), BF), jax.ShapeDtypeStruct((8, d), F32),
                   jax.ShapeDtypeStruct((8, LANES), F32)],
        compiler_params=_cparams(("arbitrary",)), name="loss_head",
    )(x, gain, target)


def _accumulate_rows(i, ref, part):
    @pl.when(i == 0)
    def _():
        ref[...] = part

    @pl.when(i > 0)
    def _():
        ref[...] += part


def _mlp_bwd(layer, dx, dxb, x, gain, u, g_out, g_up, g_down, tm=256):
    s, d = x.shape
    mw = g_out.shape[1]
    nblk, hs = g_up.shape[1], g_up.shape[3]

    def body(dx_ref, dxb_ref, x_ref, gain_ref, u_ref, out_w_ref, up_ref, down_ref, o_ref, ob_ref, du_ref, dm_ref,
             dgain_ref):
        dxb_v = dxb_ref[...]
        for j in range(nblk):
            cols = slice(j * hs, (j + 1) * hs)
            da = _dot(dxb_v, down_ref[cols, :], NT)
            du_ref[:, cols] = (da * (2.0 * jnp.maximum(u_ref[:, cols].astype(F32), 0.0))).astype(BF)
        acc = jnp.zeros((tm, d), F32)
        for j in range(nblk):
            acc = acc + _dot(du_ref[:, j * hs:(j + 1) * hs], up_ref[j], NT)
        dxn, dgain = _rms_bwd(acc, x_ref[...], gain_ref[...])
        out = dx_ref[...] + dxn
        out_b = out.astype(BF)
        o_ref[...] = out
        ob_ref[...] = out_b
        dm_ref[...] = _dot(out_b, out_w_ref[...], NT)
        _accumulate_rows(pl.program_id(0), dgain_ref, _part8(dgain))

    row = pl.BlockSpec((tm, d), lambda i: (i, 0))
    wide = pl.BlockSpec((tm, nblk * hs), lambda i: (i, 0))
    return pl.pallas_call(
        body, grid=(s // tm,),
        in_specs=[row, row, row, pl.BlockSpec((None, 1, d), lambda i: (layer, 0, 0)), wide,
                  _resident((None, mw, d), lambda i: (0, 0, 0)),
                  _resident((None, nblk, d, hs), lambda i: (0, 0, 0, 0)),
                  _resident((None, nblk * hs, d), lambda i: (0, 0, 0))],
        out_specs=[row, row, wide, pl.BlockSpec((tm, mw), lambda i: (i, 0)), pl.BlockSpec((8, d), lambda i: (0, 0))],
        out_shape=[jax.ShapeDtypeStruct((s, d), F32), jax.ShapeDtypeStruct((s, d), BF),
                   jax.ShapeDtypeStruct((s, nblk * hs), BF), jax.ShapeDtypeStruct((s, mw), F32),
                   jax.ShapeDtypeStruct((8, d), F32)],
        compiler_params=_cparams(("arbitrary",)), name=f"mlp_bwd_l{layer}",
    )(dx, dxb, x, gain, u, g_out, g_up, g_down)


def _attn_norm_bwd(layer, dmixed, o, gain, tm=512):
    s = o.shape[0]

    def body(dm_ref, o_ref, gain_ref, do_ref, delta_ref, dgain_ref):
        i = pl.program_id(0)
        ov = o_ref[...]
        do, dgain = _rms_bwd(dm_ref[...], ov, gain_ref[...])
        do_ref[...] = do
        row = lax.broadcasted_iota(jnp.int32, (ATTN_W, ATTN_W), 0)
        col = lax.broadcasted_iota(jnp.int32, (ATTN_W, ATTN_W), 1)
        same_head = jnp.where(row // 64 == col // 64, 1.0, 0.0).astype(BF)
        prod = do * ov
        high = prod.astype(BF)
        low = (prod - high.astype(F32)).astype(BF)
        delta_ref[...] = _dot(high, same_head, NN) + _dot(low, same_head, NN)
        part = _part8(dgain)

        @pl.when(i == 0)
        def _():
            dgain_ref[...] = part

        @pl.when(i > 0)
        def _():
            dgain_ref[...] += part

    blk = pl.BlockSpec((tm, ATTN_W), lambda i: (i, 0))
    return pl.pallas_call(
        body, grid=(s // tm,), in_specs=[blk, blk, pl.BlockSpec((None, 1, ATTN_W), lambda i: (layer, 0, 0))],
        out_specs=[blk, blk, pl.BlockSpec((8, ATTN_W), lambda i: (0, 0))],
        out_shape=[jax.ShapeDtypeStruct((s, ATTN_W), F32), jax.ShapeDtypeStruct((s, ATTN_W), F32),
                   jax.ShapeDtypeStruct((8, ATTN_W), F32)],
        compiler_params=_cparams(("arbitrary",)), name=f"attn_norm_bwd_l{layer}",
    )(dmixed, o, gain)


def _attn_bwd(layer, proj, do, lse, delta, cos, sin):
    s = proj.shape[0]
    n_units = s // ATTN_UNIT
    out_rows = 256

    def unit_of(n):
        return n_units - 1 - n

    def body(q_ref, kp_ref, kc_ref, vp_ref, vc_ref, do_ref, lse_ref, delta_ref, cos_ref, sin_ref, out_ref,
             dq_ref, dk_ref, dkp_ref, dv_ref, dvp_ref, carry_k, carry_v):
        step = pl.program_id(1)
        m_first, m_rest, is_a = _attn_masks(unit_of(step) > 0)
        sels = (is_a, jnp.logical_not(is_a))
        for ref in (dq_ref, dk_ref, dkp_ref, dv_ref, dvp_ref):
            ref[...] = jnp.zeros(ref.shape, F32)
        for dil in DILATIONS:
            last = ATTN_UNIT // (SPAN * dil) - 1
            for group in _attn_groups(dil):
                items = [(r, b, h) for r, b in group for h in range(2)]
                q = {rb: q_ref[_block_rows(dil, *rb), :] for rb in group}
                dov = {rb: do_ref[_block_rows(dil, *rb), :] for rb in group}
                lse_v = {rb: lse_ref[_block_rows(dil, *rb), :] for rb in group}
                delta_v = {rb: delta_ref[_block_rows(dil, *rb), :] for rb in group}
                k = {rb: _block_keys(kp_ref, kc_ref, dil, *rb).astype(BF) for rb in group}
                v = {rb: _block_keys(vp_ref, vc_ref, dil, *rb).astype(BF) for rb in group}
                qh = [jnp.where(sels[h], q[r, b], 0.0).astype(BF) for r, b, h in items]
                doh = [jnp.where(sels[h], dov[r, b], 0.0).astype(BF) for r, b, h in items]
                sc = [jnp.where(m_first if b == 0 else m_rest, _dot(qh[i], k[r, b], NT), MASK_VALUE)
                      for i, (r, b, h) in enumerate(items)]
                p = [jnp.exp(sc[i] - lse_v[r, b][:, 64 * h:64 * h + 1]) for i, (r, b, h) in enumerate(items)]
                ds = [(p[i] * (_dot(doh[i], v[r, b], NT) - delta_v[r, b][:, 64 * h:64 * h + 1])).astype(BF)
                      for i, (r, b, h) in enumerate(items)]
                dv = [_dot(p[i].astype(BF), doh[i], TN) for i in range(len(items))]
                dq = [_dot(ds[i], k[r, b], NN) for i, (r, b, h) in enumerate(items)]
                dk = [_dot(ds[i], qh[i], TN) for i in range(len(items))]
                for j, (r, b) in enumerate(group):
                    own = _block_rows(dil, r, b)
                    dq_ref[own, :] += jnp.where(is_a, dq[2 * j], dq[2 * j + 1])
                    dk2, dv2 = dk[2 * j] + dk[2 * j + 1], dv[2 * j] + dv[2 * j + 1]
                    dk_ref[own, :] += dk2[SPAN:]
                    dv_ref[own, :] += dv2[SPAN:]
                    if b > 0:
                        before = _block_rows(dil, r, b - 1)
                        dk_ref[before, :] += dk2[:SPAN]
                        dv_ref[before, :] += dv2[:SPAN]
                    else:
                        before = _block_rows(dil, r, last)
                        dkp_ref[before, :] += dk2[:SPAN]
                        dvp_ref[before, :] += dv2[:SPAN]
        has_next = step > 0
        for c in range(ATTN_UNIT // out_rows):
            rows = slice(c * out_rows, (c + 1) * out_rows)
            cs, sn = cos_ref[rows, :], sin_ref[rows, :]
            dqv = dq_ref[rows, :]
            dkv = dk_ref[rows, :] + jnp.where(has_next, carry_k[rows, :], 0.0)
            dvv = dv_ref[rows, :] + jnp.where(has_next, carry_v[rows, :], 0.0)
            out_ref[0, rows, :] = ((dqv * cs - _rope_partner(dqv) * sn) * ATTN_SCALE).astype(BF)
            out_ref[1, rows, :] = (dkv * cs - _rope_partner(dkv) * sn).astype(BF)
            out_ref[2, rows, :] = dvv.astype(BF)
        carry_k[...] = dkp_ref[...]
        carry_v[...] = dvp_ref[...]

    tab = pl.BlockSpec((ATTN_UNIT, LANES), lambda p, n: (unit_of(n), 0))
    return pl.pallas_call(
        body, grid=(ATTN_W // LANES, n_units), in_specs=_attn_specs(3, unit_of) + [tab, tab],
        out_specs=pl.BlockSpec((3, ATTN_UNIT, LANES), lambda p, n: (0, unit_of(n), p)),
        out_shape=jax.ShapeDtypeStruct((3, s, ATTN_W), BF),
        scratch_shapes=[pltpu.VMEM((ATTN_UNIT, LANES), F32)] * 7,
        compiler_params=_cparams(("parallel", "arbitrary")), name=f"attn_bwd_l{layer}",
    )(proj, proj, proj, proj, proj, do, lse, delta, cos, sin)


def _hgrn_bwd(layer, proj, lb, gain, o, dmixed, states):
    s = proj.shape[0]
    nblk = s // HGRN_ROWS
    cpb = len(CHUNKS)

    def body(q_ref, f_ref, i_ref, g_ref, lb_ref, gain_ref, o_ref, drec_ref, st_ref, dseg_ref, dlb_ref, dgain_ref,
             dstate, dst_buf):
        step = pl.program_id(0)

        @pl.when(step == 0)
        def _():
            dstate[...] = jnp.zeros(dstate.shape, F32)

        lbv, gv = lb_ref[...], gain_ref[...]
        qh, z, gate_in = q_ref[...], f_ref[...], g_ref[...]
        pre = _hgrn_pre(qh, z, lbv)
        v = i_ref[...].astype(BF)
        sg = _sigmoid(gate_in)
        ov, drec = o_ref[...], drec_ref[...]
        dnormed = drec * (gate_in * sg)
        back = [_rms_bwd(_head(dnormed, h), _head(ov, h), gv) for h in HEADS]
        do_b = jnp.concatenate([b[0] for b in back], axis=1).astype(BF)
        dgain = back[0][1] + back[1][1] + back[2][1] + back[3][1]
        normed = jnp.concatenate([_rms_fwd(_head(ov, h), gv) for h in HEADS], axis=1)
        dgate_in = drec * normed * (sg * (1.0 + gate_in * (1.0 - sg)))
        mask = _hgrn_mask()
        qt, kt, qg, kout = (pre[n].astype(BF) for n in ("qt", "kt", "qg", "kout"))
        dec = jnp.exp(pre["g_last"])
        def intra(fn):
            return jnp.concatenate([jnp.concatenate([fn(h, sb) for sb in SUBS], axis=0) for h in HEADS], axis=1)

        def hs(x, h, sb):
            return _sub(_head(x, h), sb)

        a = [[jnp.where(mask, _dot(hs(qt, h, sb), hs(kt, h, sb), NT), 0.0).astype(BF) for sb in SUBS] for h in HEADS]
        da = [[jnp.where(mask, _dot(hs(do_b, h, sb), hs(v, h, sb), NT), 0.0).astype(BF) for sb in SUBS] for h in HEADS]
        dv_intra = intra(lambda h, sb: _dot(a[h][sb], hs(do_b, h, sb), TN))
        dqt = intra(lambda h, sb: _dot(da[h][sb], hs(kt, h, sb), NN))
        dkt = intra(lambda h, sb: _dot(da[h][sb], hs(qt, h, sb), TN))
        feed = [[_dot(_chunk(_head(do_b, h), c), _chunk(_head(qg, h), c), TN) for c in CHUNKS] for h in HEADS]
        for h in HEADS:
            dst = dstate[h]
            for c in reversed(CHUNKS):
                dst_buf[h, c * LANES:(c + 1) * LANES, :] = dst
                dst = dst * _head(dec, h)[c * HGRN_CHUNK:c * HGRN_CHUNK + 1, :] + feed[h][c]
            dstate[h] = dst

        def per_chunk(fn):
            cols = []
            for h in HEADS:
                rows = [jnp.broadcast_to(t, (HGRN_CHUNK, HGRN_DIM)) for t in (fn(h, c) for c in CHUNKS)]
                cols.append(jnp.concatenate(rows, axis=0))
            return jnp.concatenate(cols, axis=1)

        def st_prev(h, c):
            return st_ref[h, c * LANES:(c + 1) * LANES, :]

        def dst_at(h, c):
            return dst_buf[h, c * LANES:(c + 1) * LANES, :]

        dqg = per_chunk(lambda h, c: _dot(_chunk(_head(do_b, h), c), st_prev(h, c).astype(BF), NN))
        dkout = per_chunk(lambda h, c: _dot(_chunk(_head(v, h), c), dst_at(h, c).astype(BF), NN))
        dv_inter = per_chunk(lambda h, c: _dot(_chunk(_head(kout, h), c), dst_at(h, c).astype(BF), NT))
        dg_state = per_chunk(lambda h, c: jnp.sum(dst_at(h, c) * st_prev(h, c).astype(F32), axis=0, keepdims=True))
        dg_kout = per_chunk(lambda h, c: jnp.sum(_chunk(_head(dkout * pre["kout"], h), c), axis=0, keepdims=True))
        dv = dv_intra + dv_inter
        pos = lax.broadcasted_iota(jnp.int32, (HGRN_ROWS, HGRN_W), 0) % HGRN_CHUNK
        dq = dqt * pre["e_q"] + dqg * pre["e_in"]
        dk = dkt * pre["e_k"] + dkout * pre["e_out"]
        dg = (dqt * pre["qt"] - dkt * pre["kt"] + dqg * pre["qg"] - dkout * pre["kout"]
              + jnp.where(pos == HGRN_CHUNK - 1, dg_state * dec + dg_kout, 0.0))
        dlogf = _chunk_cumsum(dg, reverse=True)
        sig, sq = pre["sig"], pre["sq"]
        df = dlogf / pre["f"] - dk
        dseg_ref[0] = (dq * HGRN_SCALE * (sq * (1.0 + qh * (1.0 - sq)))).astype(BF)
        dseg_ref[1] = (df * (1.0 - lbv) * sig * (1.0 - sig)).astype(BF)
        dseg_ref[2] = dv.astype(BF)
        dseg_ref[3] = dgate_in.astype(BF)
        _accumulate_rows(step, dlb_ref, _part8(df * (1.0 - sig)))
        _accumulate_rows(step, dgain_ref, _part8(dgain))

    specs, blk = _hgrn_in_specs(layer, True, nblk)
    specs += [pl.BlockSpec((HGRN_ROWS, HGRN_W), lambda b: (blk(b), 0)),
              pl.BlockSpec((HGRN_ROWS, HGRN_W), lambda b: (blk(b), 1)),
              pl.BlockSpec((HGRN_HEADS, cpb * LANES, LANES), lambda b: (0, blk(b), 0))]
    return pl.pallas_call(
        body, grid=(nblk,), in_specs=specs,
        out_specs=[pl.BlockSpec((4, HGRN_ROWS, HGRN_W), lambda b: (0, blk(b), 0)),
                   pl.BlockSpec((8, HGRN_W), lambda b: (0, 0)), pl.BlockSpec((8, HGRN_DIM), lambda b: (0, 0))],
        out_shape=[jax.ShapeDtypeStruct((4, s, HGRN_W), BF), jax.ShapeDtypeStruct((8, HGRN_W), F32),
                   jax.ShapeDtypeStruct((8, HGRN_DIM), F32)],
        scratch_shapes=[pltpu.VMEM((HGRN_HEADS, LANES, LANES), F32), pltpu.VMEM((HGRN_HEADS, cpb * LANES, LANES), F32)],
        compiler_params=_cparams(("arbitrary",)), name=f"hgrn_bwd_l{layer}",
    )(proj, proj, proj, proj, lb, gain, o, dmixed, states)


def _bwd_inproj(layer, dqkv, dhg, g_in, x, gain, dres, tm=512):
    s, d = x.shape

    def body(dqkv_ref, dhg_ref, w_ref, x_ref, gain_ref, dres_ref, dx_ref, dxb_ref, dgain_ref):
        acc = jnp.zeros((tm, d), F32)
        for seg in range(N_SEG):
            a = dqkv_ref[seg] if seg < 3 else dhg_ref[seg - 3]
            acc = acc + _dot(a, w_ref[seg * SEG:(seg + 1) * SEG, :], NN)
        dx, dgain = _rms_bwd(acc, x_ref[...], gain_ref[...])
        out = dres_ref[...] + dx
        dx_ref[...] = out
        dxb_ref[...] = out.astype(BF)
        _accumulate_rows(pl.program_id(0), dgain_ref, _part8(dgain))

    row = pl.BlockSpec((tm, d), lambda i: (i, 0))
    return pl.pallas_call(
        body, grid=(s // tm,),
        in_specs=[pl.BlockSpec((3, tm, SEG), lambda i: (0, i, 0)), pl.BlockSpec((4, tm, SEG), lambda i: (0, i, 0)),
                  _resident((None, PROJ_W, d), lambda i: (0, 0, 0)), row,
                  pl.BlockSpec((None, 1, d), lambda i: (layer, 0, 0)), row],
        out_specs=[row, row, pl.BlockSpec((8, d), lambda i: (0, 0))],
        out_shape=[jax.ShapeDtypeStruct((s, d), F32), jax.ShapeDtypeStruct((s, d), BF), jax.ShapeDtypeStruct((8, d), F32)],
        compiler_params=_cparams(("arbitrary",)), name=f"bwd_inproj_l{layer}",
    )(dqkv, dhg, g_in, x, gain, dres)


def _adamw(w, g, m, v):
    m2 = ADAM_B1 * m + (1.0 - ADAM_B1) * g
    v2 = ADAM_B2 * v + (1.0 - ADAM_B2) * (g * g)
    m_hat = m2 / (1.0 - ADAM_B1 ** ADAM_STEP)
    v_hat = v2 / (1.0 - ADAM_B2 ** ADAM_STEP)
    delta = -ADAM_LR * (m_hat / (jnp.sqrt(v_hat) + ADAM_EPS) + ADAM_WD * w)
    return delta, m2, v2


def _adam_big(name, parts, w, m, v, row_tiles):
    depth = w.shape[0]
    r, c = parts[0].shape[1], parts[0].shape[2]
    tr = r // row_tiles
    p_spec = pl.BlockSpec((N_DEV, tr, c), lambda t: (0, t, 0))
    w_spec = pl.BlockSpec((depth, tr, c), lambda t: (0, t, 0))

    def body(*refs):
        p_refs = refs[:depth]
        w_ref, m_ref, v_ref, g_ref, d_ref, m2_ref, v2_ref, token = refs[depth:]
        token[...] = jnp.zeros(token.shape, F32)
        for l in range(depth):
            g = p_refs[l][0].astype(F32)
            for dev in range(1, N_DEV):
                g = g + p_refs[l][dev].astype(F32)
            delta, m2, v2 = _adamw(w_ref[l], g, m_ref[l], v_ref[l])
            g_ref[l] = g
            d_ref[l] = delta
            m2_ref[l] = m2
            v2_ref[l] = v2

    return pl.pallas_call(
        body, grid=(row_tiles,), in_specs=[p_spec] * depth + [w_spec] * 3,
        out_specs=[w_spec] * 4 + [pl.BlockSpec((8, LANES), lambda t: (0, 0))],
        out_shape=[jax.ShapeDtypeStruct(w.shape, F32)] * 4 + [jax.ShapeDtypeStruct((8, LANES), F32)],
        compiler_params=_cparams(("arbitrary",)), name=name,
    )(*parts, w, m, v)


def _adam_small(g, ws, ms, vs):
    n = len(ws)

    def split(row, width):
        return jnp.concatenate([row[:, :width], row[:, width:2 * width]], axis=0)

    def body(g_ref, *refs):
        ins, outs = refs[:3 * n], refs[3 * n:]
        grads = [g_ref[0:2, :], split(g_ref[5:6, :], ATTN_W), split(g_ref[6:7, :], HGRN_W), split(g_ref[7:8, :], HGRN_DIM),
                 g_ref[2:4, :], g_ref[4:5, :]]
        for i, g_i in enumerate(grads):
            delta, m2, v2 = _adamw(ins[i][...], g_i, ins[n + i][...], ins[2 * n + i][...])
            for j, val in enumerate((g_i, delta, m2, v2)):
                outs[4 * i + j][...] = val

    vm = pl.BlockSpec(memory_space=pltpu.VMEM)
    res = pl.pallas_call(
        body, in_specs=[vm] * (1 + 3 * n), out_specs=[vm] * (4 * n),
        out_shape=[jax.ShapeDtypeStruct(w.shape, F32) for w in ws for _ in range(4)], name="adam_small",
    )(g, *ws, *ms, *vs)
    return [res[4 * i:4 * i + 4] for i in range(n)]


def _lower_bounds(logits):
    def body(l_ref, lb_ref, jac_ref):
        l0, l1 = l_ref[0:1, :], l_ref[1:2, :]
        mx = jnp.maximum(l0, l1)
        e0, e1 = jnp.exp(l0 - mx), jnp.exp(l1 - mx)
        p0, p1 = e0 / (e0 + e1), e1 / (e0 + e1)
        lb_ref[0:1, :] = p0 - p0
        lb_ref[1:2, :] = (p0 + p1) - p0
        jac_ref[0:1, :] = -p0 * p1
        jac_ref[1:2, :] = p0 * p1

    vm = pl.BlockSpec(memory_space=pltpu.VMEM)
    return pl.pallas_call(body, in_specs=[vm], out_specs=[vm, vm], out_shape=[jax.ShapeDtypeStruct(logits.shape, F32)] * 2,
                          name="hgrn_lower_bounds")(logits)


def _rope_tables(s, after):
    half = 32
    inv_freq = ROPE_THETA ** (-jnp.arange(half, dtype=F32) / half)
    ang = (jnp.arange(s, dtype=jnp.int32).astype(F32) + after[0, 0])[:, None] * inv_freq[None, :]
    cos, sin = jnp.cos(ang), jnp.sin(ang)
    return jnp.concatenate([cos] * 4, axis=1), jnp.concatenate([-sin, sin, -sin, sin], axis=1)


def kernel(x, norm_mix, w_in, attn_out_gain, hgrn_lb_logits, hgrn_out_gain, w_out, norm_mlp, w_up, w_down, norm_final, loss_target, m_norm_mix, m_w_in, m_attn_out_gain, m_hgrn_lb_logits, m_hgrn_out_gain, m_w_out, m_norm_mlp, m_w_up, m_w_down, m_norm_final, v_norm_mix, v_w_in, v_attn_out_gain, v_hgrn_lb_logits, v_hgrn_out_gain, v_w_out, v_norm_mlp, v_w_up, v_w_down, v_norm_final):
    depth = w_in.shape[0]
    assert depth == 2 and x.shape[0] == 1
    s, d = x.shape[1], x.shape[2]
    x0 = x[0]
    target = loss_target[0]
    g_mix, g_attn, g_hg, g_mlp = (norm_mix[:, None, :], attn_out_gain[:, None, :], hgrn_out_gain[:, None, :],
                                  norm_mlp[:, None, :])
    lb, lb_jac = _lower_bounds(hgrn_lb_logits)
    lb3 = lb[:, None, :]

    def flip(a):
        return jnp.swapaxes(a, 1, 2)

    shards = list(_pack_weights(flip(w_in), w_out, w_up, w_down))
    w_pieces = _weight_pieces(*shards)
    w_groups = [[0], [1, 2, 3], [4], [5, 6, 7]]
    me = (4 * lax.axis_index("x") + 2 * lax.axis_index("y") + lax.axis_index("c")).astype(jnp.int32).reshape(1)
    lands = _exchange_own("all_gather_own", me, shards, w_pieces)
    w_sems, shards, lands, token = _exchange_start("all_gather_start", shards, lands, w_pieces, w_groups)

    def weights_ready(group, after):
        nonlocal shards
        idxs = w_groups[group]
        shards, got = _exchange_wait(f"all_gather_wait{group}", shards, [lands[i] for i in idxs], w_pieces,
                                     [(idxs, *w_sems[group])], after)
        return got

    cos, sin = _rope_tables(s, token)

    def tied(small_arr, tok):
        return small_arr + tok[0, 0]

    saved = []
    xl = x0
    full = [None] * depth
    for l in range(depth):
        (full_in,) = weights_ready(2 * l, cos if l == 0 else xl)
        proj, h = _fwd_inproj(l, xl, g_mix, full_in, cos, sin)
        o_attn, lse = _attn_fwd(l, proj)
        o_hg, mixed, states = _hgrn_fwd(l, proj, lb3, g_hg)
        mixed = _attn_norm(l, o_attn, g_attn, mixed)
        full_out, full_up, full_down = weights_ready(2 * l + 1, mixed)
        x_next, x_mid, u, h2 = _mlp_fwd(l, xl, mixed, g_mlp, full_out, full_up, full_down)
        saved.append((xl, proj, h, o_attn, lse, o_hg, states, mixed, x_mid, u, h2))
        full[l] = (full_in, full_out, full_up, full_down)
        xl = x_next
    dx, dxb, dnorm_final8, loss8 = _loss_head(xl, norm_final[None, :], target)

    exchanges = []

    def scatter(tag, grads, kinds):
        pieces = _grad_pieces(grads, kinds)
        own = _exchange_own(f"reduce_scatter_own_{tag}", me, grads, pieces)
        sems, grads, own, tok = _exchange_start(f"reduce_scatter_start_{tag}", grads, own, pieces, [list(range(len(pieces)))])
        exchanges.append((grads, own, pieces, sems[0]))
        return tok

    small = {}
    for l in reversed(range(depth)):
        xl, proj, h, o_attn, lse, o_hg, states, mixed, x_mid, u, h2 = saved[l]
        full_in, full_out, full_up, full_down = full[l]
        hs = full_up.shape[3]
        gw_down = _mm_tn(f"grad_w_down_l{l}", u, dxb, u.shape[1], a_fn=_relu2)
        dx_mid, dx_mid_b, du, dmixed, dmlp8 = _mlp_bwd(l, dx, dxb, x_mid, g_mlp, u, full_out, full_up, full_down)
        gw_up = _mm_tn(f"grad_w_up_l{l}", h2, du, d, out_block_w=hs)
        gw_out = _mm_tn(f"grad_w_out_l{l}", mixed, dx_mid_b, mixed.shape[1])
        g_attn_t = tied(g_attn, scatter(f"mlp_l{l}", [gw_down, gw_up, gw_out], ["rows", "up", "rows"]))
        do, delta, dattn8 = _attn_norm_bwd(l, dmixed, o_attn, g_attn_t)
        dqkv = _attn_bwd(l, proj, do, lse, delta, cos, sin)
        dhg, dlb8, dhgain8 = _hgrn_bwd(l, proj, lb3, g_hg, o_hg, dmixed, states)
        gin = _mm_tn(f"grad_w_in_qkv_l{l}", dqkv, h, PROJ_W, a_lead=True)
        gw_in = _mm_tn(f"grad_w_in_hg_l{l}", dhg, h, PROJ_W, a_lead=True, out_block_off=3, prev=gin)
        g_mix_t = tied(g_mix, scatter(f"mix_l{l}", [gw_in], ["rows"]))
        dx, dxb, dmix8 = _bwd_inproj(l, dqkv, dhg, full_in, xl, g_mix_t, dx_mid)
        small[l] = (dmix8, dattn8, dlb8, dhgain8, dmlp8)

    def scattered(name, which, after):
        grads, lands, pieces, waits = [], [], [], []
        for grads_e, own, pieces_e, (send, recv) in (exchanges[i] for i in which):
            first = len(pieces)
            pieces += [p._replace(src=p.src + len(grads)) for p in pieces_e]
            waits.append((list(range(first, first + len(pieces_e))), send, recv))
            grads += grads_e
            lands += own
        return _exchange_wait(name, grads, lands, pieces, waits, after)[1]

    down1, up1, out1, in1, down0, up0, out0 = scattered("reduce_scatter_wait_early", (0, 1, 2), dx)
    big = {
        "w_down": _adam_big("adam_w_down", [down0, down1], w_down, m_w_down, v_w_down, 4),
        "w_up": _adam_big("adam_w_up", [up0, up1], w_up, m_w_up, v_w_up, 2),
        "w_out": _adam_big("adam_w_out", [out0, out1], w_out, m_w_out, v_w_out, 1),
    }
    g_small = _all_reduce_small([small[l][0] for l in range(depth)], [small[l][4] for l in range(depth)], dnorm_final8,
                                [small[l][1] for l in range(depth)], small[depth - 1][2], lb_jac,
                                [small[l][3] for l in range(depth)], loss8, big["w_out"][4])
    loss = g_small[7, 2 * HGRN_DIM]
    row = lambda a: a[None, :]
    small_out = _adam_small(
        g_small, [norm_mix, attn_out_gain, hgrn_lb_logits, hgrn_out_gain, norm_mlp, row(norm_final)],
        [m_norm_mix, m_attn_out_gain, m_hgrn_lb_logits, m_hgrn_out_gain, m_norm_mlp, row(m_norm_final)],
        [v_norm_mix, v_attn_out_gain, v_hgrn_lb_logits, v_hgrn_out_gain, v_norm_mlp, row(v_norm_final)])
    small_out[5] = [t[0] for t in small_out[5]]
    (in0,) = scattered("reduce_scatter_wait_last", (3,), small_out[0][1])
    big["w_in"] = [flip(t) for t in _adam_big("adam_w_in", [in0, in1], flip(w_in), flip(m_w_in), flip(v_w_in), 2)[:4]]

    def gather(idx):
        mix, attn, lbl, hg, mlp, final = (t[idx] for t in small_out)
        return [mix, big["w_in"][idx], attn, lbl, hg, big["w_out"][idx], mlp, big["w_up"][idx], big["w_down"][idx], final]

    return (loss, dx[None], *gather(0), *gather(1), *gather(2), *gather(3))
```

```python
import functools
from typing import Callable, NamedTuple

import jax
import jax.numpy as jnp
from jax import lax
from jax.experimental import pallas as pl
from jax.experimental.pallas import tpu as pltpu

F32 = jnp.float32
BF = jnp.bfloat16

N_DEV = 8
ATTN_W = 512
HGRN_W = 512
HGRN_HEADS = 4
HGRN_DIM = 128
SEG = 512
N_SEG = 7
PROJ_W = N_SEG * SEG
MIX_W = ATTN_W + HGRN_W
SPAN = 128
DILATIONS = (1, 4, 16)
HGRN_CHUNK = 16
ROPE_THETA = 10000.0
NORM_EPS = 1e-6
MASK_VALUE = -1e30
ATTN_SCALE = 0.125
HGRN_SCALE = HGRN_DIM ** -0.5
ADAM_LR = 0.001
ADAM_B1 = 0.9
ADAM_B2 = 0.999
ADAM_EPS = 1e-08
ADAM_WD = 0.01
ADAM_STEP = 10
LANES = 128
VMEM_LIMIT = 56 * 1024 * 1024

NN = ((1,), (0,))
NT = ((1,), (1,))
TN = ((0,), (0,))
MESH = pl.DeviceIdType.MESH


def _dot(a, b, dims):
    return lax.dot_general(a, b, (dims, ((), ())), preferred_element_type=F32)


def _cparams(sem):
    return pltpu.CompilerParams(dimension_semantics=sem, vmem_limit_bytes=VMEM_LIMIT)


def _part8(x):
    r, n = x.shape
    return jnp.sum(x.reshape(r // 8, 8, n), axis=0)


def _sigmoid(x):
    return 1.0 / (1.0 + jnp.exp(-x))


def _rms_fwd(x, gain):
    r = lax.rsqrt(jnp.mean(x * x, axis=-1, keepdims=True) + NORM_EPS)
    return x * r * gain


def _rms_bwd(dy, x, gain):
    r = lax.rsqrt(jnp.mean(x * x, axis=-1, keepdims=True) + NORM_EPS)
    xn = x * r
    dxn = dy * gain
    dx = r * (dxn - xn * jnp.mean(dxn * xn, axis=-1, keepdims=True))
    return dx, dy * xn


def _rope_partner(x):
    n = x.shape[-1]
    lane = lax.broadcasted_iota(jnp.int32, x.shape, x.ndim - 1)
    return jnp.where((lane % 64) < 32, pltpu.roll(x, n - 32, x.ndim - 1), pltpu.roll(x, 32, x.ndim - 1))


def _tile_lanes(t, reps):
    return jnp.concatenate([t] * reps, axis=-1)


def _mm_tn(name, a, b, out_rows, a_lead=False, out_block_off=0, prev=None, out_block_w=None, a_fn=None,
           tm=512, tn=1024, sub=512):
    kdim, n = b.shape
    m = a.shape[-1]
    tm, tn, sub = min(tm, m), min(tn, n), min(sub, kdim)
    mt = m // tm
    n_lead = a.shape[0] if a_lead else 1
    if a_lead:
        a_spec = pl.BlockSpec((None, kdim, tm), lambda j, i: (i // mt, 0, i % mt))
    else:
        a_spec = pl.BlockSpec((kdim, tm), lambda j, i: (0, i))
    b_spec = pl.BlockSpec((kdim, tn), lambda j, i: (0, j))
    if out_block_w:
        nb = tn // out_block_w
        o_shape = jax.ShapeDtypeStruct((n // out_block_w, out_rows, out_block_w), BF)
        o_spec = pl.BlockSpec((nb, tm, out_block_w), lambda j, i: (j, i + out_block_off, 0))
    else:
        nb = 0
        o_shape = jax.ShapeDtypeStruct((out_rows, n), BF)
        o_spec = pl.BlockSpec((tm, tn), lambda j, i: (i + out_block_off, j))
    arrays, specs, aliases = [a, b], [a_spec, b_spec], {}
    if prev is not None:
        arrays.append(prev)
        specs.append(pl.BlockSpec(memory_space=pl.ANY))
        aliases = {2: 0}

    def body(*refs):
        a_ref, b_ref, o_ref = refs[0], refs[1], refs[-1]
        acc = None
        for k in range(kdim // sub):
            av = a_ref[k * sub:(k + 1) * sub, :]
            if a_fn is not None:
                av = a_fn(av)
            part = _dot(av, b_ref[k * sub:(k + 1) * sub, :], TN)
            acc = part if acc is None else acc + part
        if nb:
            for t in range(nb):
                o_ref[t] = acc[:, t * out_block_w:(t + 1) * out_block_w].astype(BF)
        else:
            o_ref[...] = acc.astype(BF)

    return pl.pallas_call(
        body, grid=(n // tn, n_lead * mt), in_specs=specs, out_specs=o_spec, out_shape=o_shape,
        compiler_params=_cparams(("parallel", "parallel")), name=name, input_output_aliases=aliases,
    )(*arrays)


def _pack_weights(w_in_t, w_out, w_up, w_down):
    depth = w_in_t.shape[0]
    arrays = (w_in_t, w_out, w_up, w_down)

    def body(*refs):
        for src, dst in zip(refs[:4], refs[4:]):
            dst[...] = src[...].astype(BF)

    specs = [pl.BlockSpec((None,) + a.shape[1:], lambda l: (l, 0, 0)) for a in arrays]
    return pl.pallas_call(
        body, grid=(depth,), in_specs=specs, out_specs=specs,
        out_shape=[jax.ShapeDtypeStruct(a.shape, BF) for a in arrays],
        compiler_params=_cparams(("arbitrary",)), name="pack_weights",
    )(*arrays)


def _my_position():
    x, y, c = lax.axis_index("x"), lax.axis_index("y"), lax.axis_index("c")
    return x, y, c, 4 * x + 2 * y + c


def _peer(x, y, c, k):
    px = 1 - x if k & 4 else x
    py = 1 - y if k & 2 else y
    pc = 1 - c if k & 1 else c
    return (px, py, pc), 4 * px + 2 * py + pc


PEER_ORDER = (1, 2, 4, 3, 5, 6, 7)


class _Piece(NamedTuple):
    src: int
    send: Callable
    slot: Callable
    land_shape: tuple
    own_src: tuple
    own_slot: tuple


HBM_SPEC = pl.BlockSpec(memory_space=pltpu.HBM)
SEM_SPEC = pl.BlockSpec(memory_space=pltpu.SEMAPHORE)
ANY_SPEC = pl.BlockSpec(memory_space=pl.ANY)


def _in_hbm(arrays):
    return [pltpu.with_memory_space_constraint(a, pltpu.HBM) for a in arrays]


def _hbm_like(arrays):
    return [pltpu.HBM(a.shape, a.dtype) for a in arrays]


def _rows_of(rows):
    return lambda ref, dev: ref.at[pl.ds(pl.multiple_of(dev * rows, 16), rows), :]


def _exchange_own(name, me, srcs, pieces):
    n = len(pieces)

    def body(me_ref, *refs):
        for i in range(n):
            refs[n + i][...] = refs[i][...]

    def spec(block_and_index):
        block, index = block_and_index
        return pl.BlockSpec(block, lambda i, me_ref: index(me_ref[0]))

    return pl.pallas_call(
        body,
        grid_spec=pltpu.PrefetchScalarGridSpec(
            num_scalar_prefetch=1, grid=(1,), in_specs=[spec(p.own_src) for p in pieces],
            out_specs=[spec(p.own_slot) for p in pieces]),
        out_shape=[jax.ShapeDtypeStruct(p.land_shape, BF) for p in pieces],
        compiler_params=_cparams(("arbitrary",)), name=name,
    )(me, *[srcs[p.src] for p in pieces])


def _exchange_start(name, srcs, lands, pieces, groups):
    n_src, n, n_g = len(srcs), len(pieces), len(groups)

    def body(*refs):
        src_refs, land_refs = refs[:n_src], refs[n_src:n_src + n]
        sems, token = refs[n_src + n:n_src + n + 2 * n_g], refs[-1]
        x, y, c, me = _my_position()
        for g, idxs in enumerate(groups):
            for k in PEER_ORDER:
                peer, pid = _peer(x, y, c, k)
                for j, i in enumerate(idxs):
                    p = pieces[i]
                    pltpu.make_async_remote_copy(
                        src_ref=p.send(src_refs[p.src], pid), dst_ref=p.slot(land_refs[i], me),
                        send_sem=sems[2 * g].at[(k - 1) * len(idxs) + j], recv_sem=sems[2 * g + 1].at[(k - 1) * len(idxs) + j],
                        device_id=peer, device_id_type=MESH).start()
        token[...] = jnp.zeros(token.shape, F32)

    sem_shapes = [pltpu.SemaphoreType.DMA(((N_DEV - 1) * len(idxs),)) for idxs in groups for _ in range(2)]
    res = pl.pallas_call(
        body, in_specs=[HBM_SPEC] * (n_src + n),
        out_specs=[SEM_SPEC] * (2 * n_g) + [HBM_SPEC] * (n_src + n) + [pl.BlockSpec(memory_space=pltpu.VMEM)],
        out_shape=sem_shapes + _hbm_like(srcs) + _hbm_like(lands) + [jax.ShapeDtypeStruct((8, LANES), F32)],
        input_output_aliases={i: 2 * n_g + i for i in range(n_src + n)},
        compiler_params=pltpu.CompilerParams(has_side_effects=pltpu.SideEffectType.DATAFLOW_SIDE_EFFECTING),
        name=name,
    )(*_in_hbm(srcs), *_in_hbm(lands))
    sems = [(res[2 * g], res[2 * g + 1]) for g in range(n_g)]
    return sems, list(res[2 * n_g:2 * n_g + n_src]), list(res[2 * n_g + n_src:2 * n_g + n_src + n]), res[-1]


def _exchange_wait(name, srcs, lands, pieces, waits, after):
    n_src, n, n_g = len(srcs), len(lands), len(waits)

    def body(*refs):
        src_refs, land_refs = refs[:n_src], refs[n_src:n_src + n]
        sems = refs[n_src + n:n_src + n + 2 * n_g]
        x, y, c, me = _my_position()
        at = 0
        for g, (idxs, _, _) in enumerate(waits):
            for k in PEER_ORDER:
                peer, pid = _peer(x, y, c, k)
                for j, i in enumerate(idxs):
                    p = pieces[i]
                    cp = pltpu.make_async_remote_copy(
                        src_ref=p.send(src_refs[p.src], pid), dst_ref=p.slot(land_refs[at + j], pid),
                        send_sem=sems[2 * g].at[(k - 1) * len(idxs) + j], recv_sem=sems[2 * g + 1].at[(k - 1) * len(idxs) + j],
                        device_id=peer, device_id_type=MESH)
                    cp.wait_send()
                    cp.wait_recv()
            at += len(idxs)

    sem_args = [s for _, send, recv in waits for s in (send, recv)]
    res = pl.pallas_call(
        body, in_specs=[HBM_SPEC] * (n_src + n) + [SEM_SPEC] * (2 * n_g) + [ANY_SPEC],
        out_specs=[HBM_SPEC] * (n_src + n), out_shape=_hbm_like(srcs) + _hbm_like(lands),
        input_output_aliases={i: i for i in range(n_src + n)},
        compiler_params=pltpu.CompilerParams(has_side_effects=pltpu.SideEffectType.DATAFLOW_SIDE_EFFECTING),
        name=name,
    )(*srcs, *lands, *sem_args, after)
    return list(res[:n_src]), list(res[n_src:])


def _weight_pieces(p_in, p_out, p_up, p_down):
    depth, cin, d = p_in.shape
    rout, hs = p_out.shape[1], p_up.shape[2]
    pieces = []
    for l in range(depth):
        whole = functools.partial(lambda ref, dev, l: ref.at[l], l=l)
        layer = functools.partial(lambda dev, l: (l, 0, 0), l=l)

        def rows(src, n_rows, whole=whole, layer=layer):
            return _Piece(src, whole, lambda ref, dev: _rows_of(n_rows)(ref.at[0], dev), (1, N_DEV * n_rows, d),
                          ((None, n_rows, d), layer), ((None, n_rows, d), lambda dev: (0, dev, 0)))

        pieces += [
            rows(0, cin), rows(1, rout),
            _Piece(2, whole, lambda ref, dev: ref.at[0, dev], (1, N_DEV, d, hs),
                   ((None, d, hs), layer), ((None, None, d, hs), lambda dev: (0, dev, 0, 0))),
            rows(3, hs),
        ]
    return pieces


def _grad_pieces(g_pair, kinds):
    pieces = []
    for i, (g, kind) in enumerate(zip(g_pair, kinds)):
        lead = lambda dev: (dev, 0, 0)
        if kind == "up":
            blk = ((None,) + g.shape[1:], lead)
            pieces.append(_Piece(i, lambda ref, dev: ref.at[dev], lambda ref, dev: ref.at[dev], g.shape, blk, blk))
        else:
            rows, cols = g.shape[0] // N_DEV, g.shape[1]
            pieces.append(_Piece(i, _rows_of(rows), lambda ref, dev: ref.at[dev], (N_DEV, rows, cols),
                                 ((rows, cols), lambda dev: (dev, 0)), ((None, rows, cols), lead)))
    return pieces


SMALL_W = 1024


def _all_reduce_small(mix8, mlp8, final8, attn8, lb8_last, lb_jac, hg8, loss8, after):
    def body(mix0, mix1, mlp0, mlp1, fin, attn0, attn1, lb, jac, hg0, hg1, loss, after_ref, o_ref, src_ref, buf_ref,
             send_sems, recv_sems):
        def total(ref):
            return jnp.sum(ref[...], axis=0, keepdims=True)

        dlb = total(lb)
        hg = jnp.concatenate([total(hg0), total(hg1), total(loss)], axis=1)
        src_ref[...] = jnp.concatenate([
            total(mix0), total(mix1), total(mlp0), total(mlp1), total(fin),
            jnp.concatenate([total(attn0), total(attn1)], axis=1),
            jnp.concatenate([jac[0:1, :] * dlb, jac[1:2, :] * dlb], axis=1),
            jnp.concatenate([hg, jnp.zeros((1, SMALL_W - hg.shape[1]), F32)], axis=1)], axis=0)
        x, y, c, me = _my_position()
        buf_ref[me] = src_ref[...]
        sends = []
        for k in PEER_ORDER:
            peer, _ = _peer(x, y, c, k)
            cp = pltpu.make_async_remote_copy(src_ref=src_ref, dst_ref=buf_ref.at[me], send_sem=send_sems.at[k - 1],
                                              recv_sem=recv_sems.at[k - 1], device_id=peer, device_id_type=MESH)
            cp.start()
            sends.append(cp)
        for k in PEER_ORDER:
            peer, pid = _peer(x, y, c, k)
            pltpu.make_async_remote_copy(src_ref=src_ref, dst_ref=buf_ref.at[pid], send_sem=send_sems.at[k - 1],
                                         recv_sem=recv_sems.at[k - 1], device_id=peer, device_id_type=MESH).wait_recv()
        for cp in sends:
            cp.wait_send()
        acc = buf_ref[0]
        for dev in range(1, N_DEV):
            acc = acc + buf_ref[dev]
        o_ref[...] = acc

    assert mix8[0].shape[1] == SMALL_W
    vm = pl.BlockSpec(memory_space=pltpu.VMEM)
    return pl.pallas_call(
        body, in_specs=[vm] * 12 + [ANY_SPEC], out_specs=vm, out_shape=jax.ShapeDtypeStruct((8, SMALL_W), F32),
        scratch_shapes=[pltpu.VMEM((8, SMALL_W), F32), pltpu.VMEM((N_DEV, 8, SMALL_W), F32),
                        pltpu.SemaphoreType.DMA((N_DEV - 1,)), pltpu.SemaphoreType.DMA((N_DEV - 1,))],
        name="all_reduce_small",
    )(*mix8, *mlp8, final8, *attn8, lb8_last, lb_jac, *hg8, loss8, after)


def _resident(block_shape, index_map):
    return pl.BlockSpec(block_shape, index_map, pipeline_mode=pl.Buffered(1))


def _fwd_inproj(layer, x, gain, g_in, cos, sin, tm=512):
    s, d = x.shape

    def body(x_ref, gain_ref, w_ref, cos_ref, sin_ref, proj_ref, h_ref):
        h = _rms_fwd(x_ref[...], gain_ref[...]).astype(BF)
        h_ref[...] = h
        cs = _tile_lanes(cos_ref[...], SEG // LANES)
        sn = _tile_lanes(sin_ref[...], SEG // LANES)
        for seg in range(N_SEG):
            acc = _dot(h, w_ref[seg * SEG:(seg + 1) * SEG, :], NT)
            if seg < 2:
                acc = acc * cs + _rope_partner(acc) * sn
            if seg == 0:
                acc = acc * ATTN_SCALE
            proj_ref[:, seg * SEG:(seg + 1) * SEG] = acc

    return pl.pallas_call(
        body, grid=(s // tm,),
        in_specs=[pl.BlockSpec((tm, d), lambda i: (i, 0)), pl.BlockSpec((None, 1, d), lambda i: (layer, 0, 0)),
                  _resident((None, PROJ_W, d), lambda i: (0, 0, 0)),
                  pl.BlockSpec((tm, LANES), lambda i: (i, 0)), pl.BlockSpec((tm, LANES), lambda i: (i, 0))],
        out_specs=[pl.BlockSpec((tm, PROJ_W), lambda i: (i, 0)), pl.BlockSpec((tm, d), lambda i: (i, 0))],
        out_shape=[jax.ShapeDtypeStruct((s, PROJ_W), F32), jax.ShapeDtypeStruct((s, d), BF)],
        compiler_params=_cparams(("parallel",)), name=f"fwd_inproj_l{layer}",
    )(x, gain, g_in, cos, sin)


ATTN_UNIT = SPAN * max(DILATIONS)
ATTN_GROUP = 4


def _attn_masks(first_block_has_prev):
    row = lax.broadcasted_iota(jnp.int32, (SPAN, 2 * SPAN), 0)
    col = lax.broadcasted_iota(jnp.int32, (SPAN, 2 * SPAN), 1)
    band = (col >= row) & (col <= row + SPAN)
    lane = lax.broadcasted_iota(jnp.int32, (SPAN, LANES), 1)
    return band & ((col >= SPAN) | first_block_has_prev), band, lane < 64


def _attn_specs(n_in_extra, unit_of=lambda n: n):
    pairs = ATTN_W // LANES
    q_spec = pl.BlockSpec((ATTN_UNIT, LANES), lambda p, n: (unit_of(n), p))

    def prev(seg):
        return pl.BlockSpec((ATTN_UNIT, LANES), lambda p, n: (jnp.maximum(unit_of(n) - 1, 0), seg * pairs + p))

    def cur(seg):
        return pl.BlockSpec((ATTN_UNIT, LANES), lambda p, n: (unit_of(n), seg * pairs + p))

    return [q_spec, prev(1), cur(1), prev(2), cur(2)] + [q_spec] * n_in_extra


def _attn_groups(dil):
    blocks = ATTN_UNIT // (SPAN * dil)
    pairs = [(r, b) for r in range(dil) for b in range(blocks)]
    return [pairs[i:i + ATTN_GROUP] for i in range(0, len(pairs), ATTN_GROUP)]


def _block_rows(dil, r, b, n=1):
    start = r + dil * SPAN * b
    return pl.ds(start, n * SPAN, stride=dil) if dil > 1 else pl.ds(start, n * SPAN)


def _block_keys(prev_ref, cur_ref, dil, r, b):
    if b > 0:
        return cur_ref[_block_rows(dil, r, b - 1, 2), :]
    last = ATTN_UNIT // (SPAN * dil) - 1
    return jnp.concatenate([prev_ref[_block_rows(dil, r, last), :], cur_ref[_block_rows(dil, r, 0), :]], axis=0)


def _attn_fwd(layer, proj):
    s = proj.shape[0]
    n_pat = len(DILATIONS)
    merge_rows = 256

    def body(q_ref, kp_ref, kc_ref, vp_ref, vc_ref, o_ref, lse_ref, o_scr, lse_scr):
        m_first, m_rest, is_a = _attn_masks(pl.program_id(1) > 0)
        sels = (is_a, jnp.logical_not(is_a))
        is_a_keys = lax.broadcasted_iota(jnp.int32, (2 * SPAN, LANES), 1) < 64
        for pi, dil in enumerate(DILATIONS):
            for group in _attn_groups(dil):
                items = [(r, b, h) for r, b in group for h in range(2)]
                q = {rb: q_ref[_block_rows(dil, *rb), :] for rb in group}
                k = {rb: _block_keys(kp_ref, kc_ref, dil, *rb).astype(BF) for rb in group}
                v = {rb: _block_keys(vp_ref, vc_ref, dil, *rb).astype(BF) for rb in group}
                v_sum = {rb: (jnp.where(is_a_keys, v[rb], 1.0), jnp.where(is_a_keys, 1.0, v[rb])) for rb in group}
                sc = [jnp.where(m_first if b == 0 else m_rest,
                                _dot(jnp.where(sels[h], q[r, b], 0.0).astype(BF), k[r, b], NT), MASK_VALUE)
                      for r, b, h in items]
                mx = [jnp.max(jnp.maximum(t[:, :SPAN], t[:, SPAN:]), axis=-1, keepdims=True) for t in sc]
                p = [jnp.exp(t - m).astype(BF) for t, m in zip(sc, mx)]
                both = [_dot(t, v_sum[r, b][h], NN) for t, (r, b, h) in zip(p, items)]
                for j, (r, b) in enumerate(group):
                    t_a, t_b = both[2 * j], both[2 * j + 1]
                    den = pltpu.roll(jnp.where(is_a, t_b, t_a), 64, 1)
                    o_scr[pi, _block_rows(dil, r, b), :] = jnp.where(is_a, t_a, t_b) / den
                    lse_scr[pi, _block_rows(dil, r, b), :] = jnp.where(is_a, mx[2 * j], mx[2 * j + 1]) + jnp.log(den)
        for c in range(ATTN_UNIT // merge_rows):
            rows = slice(c * merge_rows, (c + 1) * merge_rows)
            ls = [lse_scr[pi, rows, :] for pi in range(n_pat)]
            mx = functools.reduce(jnp.maximum, ls)
            ws = [jnp.exp(l - mx) for l in ls]
            den = functools.reduce(jnp.add, ws)
            o_ref[rows, :] = functools.reduce(jnp.add, [w * o_scr[pi, rows, :] for pi, w in enumerate(ws)]) / den
            lse_ref[rows, :] = mx + jnp.log(den)

    out_spec = pl.BlockSpec((ATTN_UNIT, LANES), lambda p, n: (n, p))
    return pl.pallas_call(
        body, grid=(ATTN_W // LANES, s // ATTN_UNIT), in_specs=_attn_specs(0), out_specs=[out_spec, out_spec],
        out_shape=[jax.ShapeDtypeStruct((s, ATTN_W), F32)] * 2,
        scratch_shapes=[pltpu.VMEM((n_pat, ATTN_UNIT, LANES), F32)] * 2,
        compiler_params=_cparams(("parallel", "arbitrary")), name=f"attn_fwd_l{layer}",
    )(proj, proj, proj, proj, proj)


def _attn_norm(layer, o, gain, mixed, tm=512):
    s = o.shape[0]

    def body(o_ref, gain_ref, mixed_ref, n_ref):
        n_ref[...] = _rms_fwd(o_ref[...], gain_ref[...]).astype(BF)

    blk = pl.BlockSpec((tm, ATTN_W), lambda i: (i, 0))
    return pl.pallas_call(
        body, grid=(s // tm,),
        in_specs=[blk, pl.BlockSpec((None, 1, ATTN_W), lambda i: (layer, 0, 0)), pl.BlockSpec(memory_space=pl.ANY)],
        out_specs=blk, out_shape=jax.ShapeDtypeStruct(mixed.shape, BF), input_output_aliases={2: 0},
        compiler_params=_cparams(("parallel",)), name=f"attn_norm_l{layer}",
    )(o, gain, mixed)


def _chunk_cumsum(x, reverse=False):
    n = x.shape[0]
    pos = lax.broadcasted_iota(jnp.int32, x.shape, 0) % HGRN_CHUNK
    for sh in (1, 2, 4, 8):
        if reverse:
            x = x + jnp.where(pos < HGRN_CHUNK - sh, pltpu.roll(x, n - sh, 0), 0.0)
        else:
            x = x + jnp.where(pos >= sh, pltpu.roll(x, sh, 0), 0.0)
    return x


def _chunk_row(x, row):
    r, n = x.shape
    x3 = x.reshape(r // HGRN_CHUNK, HGRN_CHUNK, n)
    return jnp.broadcast_to(x3[:, row:row + 1, :], x3.shape).reshape(r, n)


def _hgrn_pre(qh, z, lb):
    sig = _sigmoid(z)
    f = lb + (1.0 - lb) * sig
    k = 1.0 - f
    sq = _sigmoid(qh)
    q = qh * sq * HGRN_SCALE
    g = _chunk_cumsum(jnp.log(f))
    g_mid = _chunk_row(g, HGRN_CHUNK // 2 - 1)
    g_last = _chunk_row(g, HGRN_CHUNK - 1)
    e_q, e_k = jnp.exp(g - g_mid), jnp.exp(g_mid - g)
    e_in, e_out = jnp.exp(g), jnp.exp(g_last - g)
    return dict(sig=sig, f=f, k=k, sq=sq, q=q, g_last=g_last, e_q=e_q, e_k=e_k, e_in=e_in, e_out=e_out,
                qt=q * e_q, kt=k * e_k, qg=q * e_in, kout=k * e_out)


def _hgrn_mask():
    row = lax.broadcasted_iota(jnp.int32, (LANES, LANES), 0)
    col = lax.broadcasted_iota(jnp.int32, (LANES, LANES), 1)
    return (row // HGRN_CHUNK == col // HGRN_CHUNK) & (col <= row)


def _hgrn_in_specs(layer, rev, nblk):
    def blk(b):
        return nblk - 1 - b if rev else b
    first = 3 * ATTN_W // HGRN_W
    specs = [pl.BlockSpec((HGRN_ROWS, HGRN_W), functools.partial(lambda b, seg: (blk(b), first + seg), seg=seg))
             for seg in range(4)]
    specs.append(pl.BlockSpec((None, 1, HGRN_W), lambda b: (layer, 0, 0)))
    specs.append(pl.BlockSpec((None, 1, HGRN_DIM), lambda b: (layer, 0, 0)))
    return specs, blk


def _head(x, h):
    return x[:, h * HGRN_DIM:(h + 1) * HGRN_DIM]


def _chunk(x, c):
    return x[c * HGRN_CHUNK:(c + 1) * HGRN_CHUNK]


def _sub(x, sb):
    return x[sb * LANES:(sb + 1) * LANES]


HGRN_ROWS = 256
HEADS = range(HGRN_HEADS)
SUBS = range(HGRN_ROWS // LANES)
CHUNKS = range(HGRN_ROWS // HGRN_CHUNK)


def _hgrn_fwd(layer, proj, lb, gain):
    s = proj.shape[0]
    nblk = s // HGRN_ROWS
    cpb = len(CHUNKS)

    def body(q_ref, f_ref, i_ref, g_ref, lb_ref, gain_ref, o_ref, rec_ref, st_ref, state):
        @pl.when(pl.program_id(0) == 0)
        def _():
            state[...] = jnp.zeros(state.shape, F32)

        pre = _hgrn_pre(q_ref[...], f_ref[...], lb_ref[...])
        v = i_ref[...].astype(BF)
        qt, kt, qg, kout = (pre[n].astype(BF) for n in ("qt", "kt", "qg", "kout"))
        dec = jnp.exp(pre["g_last"])
        mask = _hgrn_mask()
        a = [[jnp.where(mask, _dot(_sub(_head(qt, h), sb), _sub(_head(kt, h), sb), NT), 0.0).astype(BF) for sb in SUBS]
             for h in HEADS]
        o_intra = [[_dot(a[h][sb], _sub(_head(v, h), sb), NN) for sb in SUBS] for h in HEADS]
        update = [[_dot(_chunk(_head(v, h), c), _chunk(_head(kout, h), c), TN) for c in CHUNKS] for h in HEADS]
        for h in HEADS:
            st = state[h]
            for c in CHUNKS:
                st_ref[h, c * LANES:(c + 1) * LANES, :] = st.astype(BF)
                st = st * _head(dec, h)[c * HGRN_CHUNK:c * HGRN_CHUNK + 1, :] + update[h][c]
            state[h] = st
        inter = [[_dot(_chunk(_head(qg, h), c), st_ref[h, c * LANES:(c + 1) * LANES, :].astype(BF), NT) for c in CHUNKS]
                 for h in HEADS]
        o = [jnp.concatenate(o_intra[h], axis=0) + jnp.concatenate(inter[h], axis=0) for h in HEADS]
        o_ref[...] = jnp.concatenate(o, axis=1)
        gate = g_ref[...]
        normed = jnp.concatenate([_rms_fwd(o[h], gain_ref[...]) for h in HEADS], axis=1)
        rec_ref[...] = (normed * (gate * _sigmoid(gate))).astype(BF)

    specs, _ = _hgrn_in_specs(layer, False, nblk)
    return pl.pallas_call(
        body, grid=(nblk,), in_specs=specs,
        out_specs=[pl.BlockSpec((HGRN_ROWS, HGRN_W), lambda b: (b, 0)), pl.BlockSpec((HGRN_ROWS, HGRN_W), lambda b: (b, 1)),
                   pl.BlockSpec((HGRN_HEADS, cpb * LANES, LANES), lambda b: (0, b, 0))],
        out_shape=[jax.ShapeDtypeStruct((s, HGRN_W), F32), jax.ShapeDtypeStruct((s, MIX_W), BF),
                   jax.ShapeDtypeStruct((HGRN_HEADS, nblk * cpb * LANES, LANES), BF)],
        scratch_shapes=[pltpu.VMEM((HGRN_HEADS, LANES, LANES), F32)],
        compiler_params=_cparams(("arbitrary",)), name=f"hgrn_fwd_l{layer}",
    )(proj, proj, proj, proj, lb, gain)


def _relu2(u):
    return jnp.square(jnp.maximum(u, 0)).astype(BF)


def _mlp_fwd(layer, x, mixed, gain, g_out, g_up, g_down, head=None):
    s, d = x.shape
    mw = mixed.shape[1]
    nblk, hs = g_up.shape[1], g_up.shape[3]
    tm = 256 if head else 512

    def body(x_ref, m_ref, gain_ref, out_w_ref, up_ref, down_ref, *refs):
        if head:
            fin_ref, t_ref, o_ref, mid_ref, u_ref, h_ref, ob_ref, dfin_ref, loss_ref, a_buf = refs
        else:
            o_ref, mid_ref, u_ref, h_ref, a_buf = refs
        xv = x_ref[...] + _dot(m_ref[...], out_w_ref[...], NN)
        mid_ref[...] = xv
        h = _rms_fwd(xv, gain_ref[...]).astype(BF)
        h_ref[...] = h
        for j in range(nblk):
            u = _dot(h, up_ref[j], NN)
            u_ref[:, j * hs:(j + 1) * hs] = u.astype(BF)
            a_buf[:, j * hs:(j + 1) * hs] = _relu2(u)
        acc = xv
        for j in range(nblk):
            acc = acc + _dot(a_buf[:, j * hs:(j + 1) * hs], down_ref[j * hs:(j + 1) * hs, :], NN)
        if not head:
            o_ref[...] = acc
            return
        fin = fin_ref[...]
        err = _rms_fwd(acc, fin) - t_ref[...]
        dout, dfin = _rms_bwd(err * (1.0 / d), acc, fin)
        o_ref[...] = dout
        ob_ref[...] = dout.astype(BF)
        step = pl.program_id(0)
        _accumulate_rows(step, dfin_ref, _part8(dfin))
        _accumulate_rows(step, loss_ref, _part8(0.5 * jnp.mean(err * err, axis=-1, keepdims=True) * jnp.ones((1, LANES), F32)))

    row = pl.BlockSpec((tm, d), lambda i: (i, 0))
    in_specs = [row, pl.BlockSpec((tm, mw), lambda i: (i, 0)), pl.BlockSpec((None, 1, d), lambda i: (layer, 0, 0)),
                _resident((None, mw, d), lambda i: (0, 0, 0)),
                _resident((None, nblk, d, hs), lambda i: (0, 0, 0, 0)),
                _resident((None, nblk * hs, d), lambda i: (0, 0, 0))]
    out_specs = [row, row, pl.BlockSpec((tm, nblk * hs), lambda i: (i, 0)), row]
    out_shape = [jax.ShapeDtypeStruct((s, d), F32), jax.ShapeDtypeStruct((s, d), F32),
                 jax.ShapeDtypeStruct((s, nblk * hs), BF), jax.ShapeDtypeStruct((s, d), BF)]
    args = [x, mixed, gain, g_out, g_up, g_down]
    if head:
        in_specs += [pl.BlockSpec((1, d), lambda i: (0, 0)), row]
        out_specs += [row, pl.BlockSpec((8, d), lambda i: (0, 0)), pl.BlockSpec((8, LANES), lambda i: (0, 0))]
        out_shape += [jax.ShapeDtypeStruct((s, d), BF), jax.ShapeDtypeStruct((8, d), F32), jax.ShapeDtypeStruct((8, LANES), F32)]
        args += list(head)
    return pl.pallas_call(
        body, grid=(s // tm,), in_specs=in_specs, out_specs=out_specs, out_shape=out_shape,
        scratch_shapes=[pltpu.VMEM((tm, nblk * hs), BF)],
        compiler_params=_cparams(("arbitrary",) if head else ("parallel",)), name=f"mlp_fwd_l{layer}",
    )(*args)


def _accumulate_rows(i, ref, part):
    @pl.when(i == 0)
    def _():
        ref[...] = part

    @pl.when(i > 0)
    def _():
        ref[...] += part


def _mlp_bwd(layer, dx, dxb, x, gain, u, g_out, g_up, g_down, tm=256):
    s, d = x.shape
    mw = g_out.shape[1]
    nblk, hs = g_up.shape[1], g_up.shape[3]

    def body(dx_ref, dxb_ref, x_ref, gain_ref, u_ref, out_w_ref, up_ref, down_ref, o_ref, ob_ref, du_ref, dm_ref,
             dgain_ref):
        dxb_v = dxb_ref[...]
        for j in range(nblk):
            cols = slice(j * hs, (j + 1) * hs)
            da = _dot(dxb_v, down_ref[cols, :], NT)
            du_ref[:, cols] = (da * (2.0 * jnp.maximum(u_ref[:, cols].astype(F32), 0.0))).astype(BF)
        acc = jnp.zeros((tm, d), F32)
        for j in range(nblk):
            acc = acc + _dot(du_ref[:, j * hs:(j + 1) * hs], up_ref[j], NT)
        dxn, dgain = _rms_bwd(acc, x_ref[...], gain_ref[...])
        out = dx_ref[...] + dxn
        out_b = out.astype(BF)
        o_ref[...] = out
        ob_ref[...] = out_b
        dm_ref[...] = _dot(out_b, out_w_ref[...], NT)
        _accumulate_rows(pl.program_id(0), dgain_ref, _part8(dgain))

    row = pl.BlockSpec((tm, d), lambda i: (i, 0))
    wide = pl.BlockSpec((tm, nblk * hs), lambda i: (i, 0))
    return pl.pallas_call(
        body, grid=(s // tm,),
        in_specs=[row, row, row, pl.BlockSpec((None, 1, d), lambda i: (layer, 0, 0)), wide,
                  _resident((None, mw, d), lambda i: (0, 0, 0)),
                  _resident((None, nblk, d, hs), lambda i: (0, 0, 0, 0)),
                  _resident((None, nblk * hs, d), lambda i: (0, 0, 0))],
        out_specs=[row, row, wide, pl.BlockSpec((tm, mw), lambda i: (i, 0)), pl.BlockSpec((8, d), lambda i: (0, 0))],
        out_shape=[jax.ShapeDtypeStruct((s, d), F32), jax.ShapeDtypeStruct((s, d), BF),
                   jax.ShapeDtypeStruct((s, nblk * hs), BF), jax.ShapeDtypeStruct((s, mw), F32),
                   jax.ShapeDtypeStruct((8, d), F32)],
        compiler_params=_cparams(("arbitrary",)), name=f"mlp_bwd_l{layer}",
    )(dx, dxb, x, gain, u, g_out, g_up, g_down)


def _attn_norm_bwd(layer, dmixed, o, gain, tm=512):
    s = o.shape[0]

    def body(dm_ref, o_ref, gain_ref, do_ref, delta_ref, dgain_ref):
        i = pl.program_id(0)
        ov = o_ref[...]
        do, dgain = _rms_bwd(dm_ref[...], ov, gain_ref[...])
        do_ref[...] = do
        row = lax.broadcasted_iota(jnp.int32, (ATTN_W, ATTN_W), 0)
        col = lax.broadcasted_iota(jnp.int32, (ATTN_W, ATTN_W), 1)
        same_head = jnp.where(row // 64 == col // 64, 1.0, 0.0).astype(BF)
        prod = do * ov
        high = prod.astype(BF)
        low = (prod - high.astype(F32)).astype(BF)
        delta_ref[...] = _dot(high, same_head, NN) + _dot(low, same_head, NN)
        part = _part8(dgain)

        @pl.when(i == 0)
        def _():
            dgain_ref[...] = part

        @pl.when(i > 0)
        def _():
            dgain_ref[...] += part

    blk = pl.BlockSpec((tm, ATTN_W), lambda i: (i, 0))
    return pl.pallas_call(
        body, grid=(s // tm,), in_specs=[blk, blk, pl.BlockSpec((None, 1, ATTN_W), lambda i: (layer, 0, 0))],
        out_specs=[blk, blk, pl.BlockSpec((8, ATTN_W), lambda i: (0, 0))],
        out_shape=[jax.ShapeDtypeStruct((s, ATTN_W), F32), jax.ShapeDtypeStruct((s, ATTN_W), F32),
                   jax.ShapeDtypeStruct((8, ATTN_W), F32)],
        compiler_params=_cparams(("arbitrary",)), name=f"attn_norm_bwd_l{layer}",
    )(dmixed, o, gain)


def _attn_bwd(layer, proj, do, lse, delta, cos, sin):
    s = proj.shape[0]
    n_units = s // ATTN_UNIT
    out_rows = 256

    def unit_of(n):
        return n_units - 1 - n

    def body(q_ref, kp_ref, kc_ref, vp_ref, vc_ref, do_ref, lse_ref, delta_ref, cos_ref, sin_ref, out_ref,
             dq_ref, dk_ref, dkp_ref, dv_ref, dvp_ref, carry_k, carry_v):
        step = pl.program_id(1)
        m_first, m_rest, is_a = _attn_masks(unit_of(step) > 0)
        sels = (is_a, jnp.logical_not(is_a))
        for ref in (dq_ref, dk_ref, dkp_ref, dv_ref, dvp_ref):
            ref[...] = jnp.zeros(ref.shape, F32)
        for dil in DILATIONS:
            last = ATTN_UNIT // (SPAN * dil) - 1
            for group in _attn_groups(dil):
                items = [(r, b, h) for r, b in group for h in range(2)]
                q = {rb: q_ref[_block_rows(dil, *rb), :] for rb in group}
                dov = {rb: do_ref[_block_rows(dil, *rb), :] for rb in group}
                lse_v = {rb: lse_ref[_block_rows(dil, *rb), :] for rb in group}
                delta_v = {rb: delta_ref[_block_rows(dil, *rb), :] for rb in group}
                k = {rb: _block_keys(kp_ref, kc_ref, dil, *rb).astype(BF) for rb in group}
                v = {rb: _block_keys(vp_ref, vc_ref, dil, *rb).astype(BF) for rb in group}
                qh = [jnp.where(sels[h], q[r, b], 0.0).astype(BF) for r, b, h in items]
                doh = [jnp.where(sels[h], dov[r, b], 0.0).astype(BF) for r, b, h in items]
                sc = [jnp.where(m_first if b == 0 else m_rest, _dot(qh[i], k[r, b], NT), MASK_VALUE)
                      for i, (r, b, h) in enumerate(items)]
                p = [jnp.exp(sc[i] - lse_v[r, b][:, 64 * h:64 * h + 1]) for i, (r, b, h) in enumerate(items)]
                ds = [(p[i] * (_dot(doh[i], v[r, b], NT) - delta_v[r, b][:, 64 * h:64 * h + 1])).astype(BF)
                      for i, (r, b, h) in enumerate(items)]
                dv = [_dot(p[i].astype(BF), doh[i], TN) for i in range(len(items))]
                dq = [_dot(ds[i], k[r, b], NN) for i, (r, b, h) in enumerate(items)]
                dk = [_dot(ds[i], qh[i], TN) for i in range(len(items))]
                for j, (r, b) in enumerate(group):
                    own = _block_rows(dil, r, b)
                    dq_ref[own, :] += jnp.where(is_a, dq[2 * j], dq[2 * j + 1])
                    dk2, dv2 = dk[2 * j] + dk[2 * j + 1], dv[2 * j] + dv[2 * j + 1]
                    dk_ref[own, :] += dk2[SPAN:]
                    dv_ref[own, :] += dv2[SPAN:]
                    if b > 0:
                        before = _block_rows(dil, r, b - 1)
                        dk_ref[before, :] += dk2[:SPAN]
                        dv_ref[before, :] += dv2[:SPAN]
                    else:
                        before = _block_rows(dil, r, last)
                        dkp_ref[before, :] += dk2[:SPAN]
                        dvp_ref[before, :] += dv2[:SPAN]
        has_next = step > 0
        for c in range(ATTN_UNIT // out_rows):
            rows = slice(c * out_rows, (c + 1) * out_rows)
            cs, sn = cos_ref[rows, :], sin_ref[rows, :]
            dqv = dq_ref[rows, :]
            dkv = dk_ref[rows, :] + jnp.where(has_next, carry_k[rows, :], 0.0)
            dvv = dv_ref[rows, :] + jnp.where(has_next, carry_v[rows, :], 0.0)
            out_ref[0, rows, :] = ((dqv * cs - _rope_partner(dqv) * sn) * ATTN_SCALE).astype(BF)
            out_ref[1, rows, :] = (dkv * cs - _rope_partner(dkv) * sn).astype(BF)
            out_ref[2, rows, :] = dvv.astype(BF)
        carry_k[...] = dkp_ref[...]
        carry_v[...] = dvp_ref[...]

    tab = pl.BlockSpec((ATTN_UNIT, LANES), lambda p, n: (unit_of(n), 0))
    return pl.pallas_call(
        body, grid=(ATTN_W // LANES, n_units), in_specs=_attn_specs(3, unit_of) + [tab, tab],
        out_specs=pl.BlockSpec((3, ATTN_UNIT, LANES), lambda p, n: (0, unit_of(n), p)),
        out_shape=jax.ShapeDtypeStruct((3, s, ATTN_W), BF),
        scratch_shapes=[pltpu.VMEM((ATTN_UNIT, LANES), F32)] * 7,
        compiler_params=_cparams(("parallel", "arbitrary")), name=f"attn_bwd_l{layer}",
    )(proj, proj, proj, proj, proj, do, lse, delta, cos, sin)


def _hgrn_bwd(layer, proj, lb, gain, o, dmixed, states):
    s = proj.shape[0]
    nblk = s // HGRN_ROWS
    cpb = len(CHUNKS)

    def body(q_ref, f_ref, i_ref, g_ref, lb_ref, gain_ref, o_ref, drec_ref, st_ref, dseg_ref, dlb_ref, dgain_ref,
             dstate, dst_buf):
        step = pl.program_id(0)

        @pl.when(step == 0)
        def _():
            dstate[...] = jnp.zeros(dstate.shape, F32)

        lbv, gv = lb_ref[...], gain_ref[...]
        qh, z, gate_in = q_ref[...], f_ref[...], g_ref[...]
        pre = _hgrn_pre(qh, z, lbv)
        v = i_ref[...].astype(BF)
        sg = _sigmoid(gate_in)
        ov, drec = o_ref[...], drec_ref[...]
        dnormed = drec * (gate_in * sg)
        back = [_rms_bwd(_head(dnormed, h), _head(ov, h), gv) for h in HEADS]
        do_b = jnp.concatenate([b[0] for b in back], axis=1).astype(BF)
        dgain = back[0][1] + back[1][1] + back[2][1] + back[3][1]
        normed = jnp.concatenate([_rms_fwd(_head(ov, h), gv) for h in HEADS], axis=1)
        dgate_in = drec * normed * (sg * (1.0 + gate_in * (1.0 - sg)))
        mask = _hgrn_mask()
        qt, kt, qg, kout = (pre[n].astype(BF) for n in ("qt", "kt", "qg", "kout"))
        dec = jnp.exp(pre["g_last"])
        def intra(fn):
            return jnp.concatenate([jnp.concatenate([fn(h, sb) for sb in SUBS], axis=0) for h in HEADS], axis=1)

        def hs(x, h, sb):
            return _sub(_head(x, h), sb)

        a = [[jnp.where(mask, _dot(hs(qt, h, sb), hs(kt, h, sb), NT), 0.0).astype(BF) for sb in SUBS] for h in HEADS]
        da = [[jnp.where(mask, _dot(hs(do_b, h, sb), hs(v, h, sb), NT), 0.0).astype(BF) for sb in SUBS] for h in HEADS]
        dv_intra = intra(lambda h, sb: _dot(a[h][sb], hs(do_b, h, sb), TN))
        dqt = intra(lambda h, sb: _dot(da[h][sb], hs(kt, h, sb), NN))
        dkt = intra(lambda h, sb: _dot(da[h][sb], hs(qt, h, sb), TN))
        feed = [[_dot(_chunk(_head(do_b, h), c), _chunk(_head(qg, h), c), TN) for c in CHUNKS] for h in HEADS]
        for h in HEADS:
            dst = dstate[h]
            for c in reversed(CHUNKS):
                dst_buf[h, c * LANES:(c + 1) * LANES, :] = dst
                dst = dst * _head(dec, h)[c * HGRN_CHUNK:c * HGRN_CHUNK + 1, :] + feed[h][c]
            dstate[h] = dst

        def per_chunk(fn):
            cols = []
            for h in HEADS:
                rows = [jnp.broadcast_to(t, (HGRN_CHUNK, HGRN_DIM)) for t in (fn(h, c) for c in CHUNKS)]
                cols.append(jnp.concatenate(rows, axis=0))
            return jnp.concatenate(cols, axis=1)

        def st_prev(h, c):
            return st_ref[h, c * LANES:(c + 1) * LANES, :]

        def dst_at(h, c):
            return dst_buf[h, c * LANES:(c + 1) * LANES, :]

        dqg = per_chunk(lambda h, c: _dot(_chunk(_head(do_b, h), c), st_prev(h, c).astype(BF), NN))
        dkout = per_chunk(lambda h, c: _dot(_chunk(_head(v, h), c), dst_at(h, c).astype(BF), NN))
        dv_inter = per_chunk(lambda h, c: _dot(_chunk(_head(kout, h), c), dst_at(h, c).astype(BF), NT))
        dg_state = per_chunk(lambda h, c: jnp.sum(dst_at(h, c) * st_prev(h, c).astype(F32), axis=0, keepdims=True))
        dg_kout = per_chunk(lambda h, c: jnp.sum(_chunk(_head(dkout * pre["kout"], h), c), axis=0, keepdims=True))
        dv = dv_intra + dv_inter
        pos = lax.broadcasted_iota(jnp.int32, (HGRN_ROWS, HGRN_W), 0) % HGRN_CHUNK
        dq = dqt * pre["e_q"] + dqg * pre["e_in"]
        dk = dkt * pre["e_k"] + dkout * pre["e_out"]
        dg = (dqt * pre["qt"] - dkt * pre["kt"] + dqg * pre["qg"] - dkout * pre["kout"]
              + jnp.where(pos == HGRN_CHUNK - 1, dg_state * dec + dg_kout, 0.0))
        dlogf = _chunk_cumsum(dg, reverse=True)
        sig, sq = pre["sig"], pre["sq"]
        df = dlogf / pre["f"] - dk
        dseg_ref[0] = (dq * HGRN_SCALE * (sq * (1.0 + qh * (1.0 - sq)))).astype(BF)
        dseg_ref[1] = (df * (1.0 - lbv) * sig * (1.0 - sig)).astype(BF)
        dseg_ref[2] = dv.astype(BF)
        dseg_ref[3] = dgate_in.astype(BF)
        _accumulate_rows(step, dlb_ref, _part8(df * (1.0 - sig)))
        _accumulate_rows(step, dgain_ref, _part8(dgain))

    specs, blk = _hgrn_in_specs(layer, True, nblk)
    specs += [pl.BlockSpec((HGRN_ROWS, HGRN_W), lambda b: (blk(b), 0)),
              pl.BlockSpec((HGRN_ROWS, HGRN_W), lambda b: (blk(b), 1)),
              pl.BlockSpec((HGRN_HEADS, cpb * LANES, LANES), lambda b: (0, blk(b), 0))]
    return pl.pallas_call(
        body, grid=(nblk,), in_specs=specs,
        out_specs=[pl.BlockSpec((4, HGRN_ROWS, HGRN_W), lambda b: (0, blk(b), 0)),
                   pl.BlockSpec((8, HGRN_W), lambda b: (0, 0)), pl.BlockSpec((8, HGRN_DIM), lambda b: (0, 0))],
        out_shape=[jax.ShapeDtypeStruct((4, s, HGRN_W), BF), jax.ShapeDtypeStruct((8, HGRN_W), F32),
                   jax.ShapeDtypeStruct((8, HGRN_DIM), F32)],
        scratch_shapes=[pltpu.VMEM((HGRN_HEADS, LANES, LANES), F32), pltpu.VMEM((HGRN_HEADS, cpb * LANES, LANES), F32)],
        compiler_params=_cparams(("arbitrary",)), name=f"hgrn_bwd_l{layer}",
    )(proj, proj, proj, proj, lb, gain, o, dmixed, states)


def _bwd_inproj(layer, dqkv, dhg, g_in, x, gain, dres, tm=512):
    s, d = x.shape

    def body(dqkv_ref, dhg_ref, w_ref, x_ref, gain_ref, dres_ref, dx_ref, dxb_ref, dgain_ref):
        acc = jnp.zeros((tm, d), F32)
        for seg in range(N_SEG):
            a = dqkv_ref[seg] if seg < 3 else dhg_ref[seg - 3]
            acc = acc + _dot(a, w_ref[seg * SEG:(seg + 1) * SEG, :], NN)
        dx, dgain = _rms_bwd(acc, x_ref[...], gain_ref[...])
        out = dres_ref[...] + dx
        dx_ref[...] = out
        dxb_ref[...] = out.astype(BF)
        _accumulate_rows(pl.program_id(0), dgain_ref, _part8(dgain))

    row = pl.BlockSpec((tm, d), lambda i: (i, 0))
    return pl.pallas_call(
        body, grid=(s // tm,),
        in_specs=[pl.BlockSpec((3, tm, SEG), lambda i: (0, i, 0)), pl.BlockSpec((4, tm, SEG), lambda i: (0, i, 0)),
                  _resident((None, PROJ_W, d), lambda i: (0, 0, 0)), row,
                  pl.BlockSpec((None, 1, d), lambda i: (layer, 0, 0)), row],
        out_specs=[row, row, pl.BlockSpec((8, d), lambda i: (0, 0))],
        out_shape=[jax.ShapeDtypeStruct((s, d), F32), jax.ShapeDtypeStruct((s, d), BF), jax.ShapeDtypeStruct((8, d), F32)],
        compiler_params=_cparams(("arbitrary",)), name=f"bwd_inproj_l{layer}",
    )(dqkv, dhg, g_in, x, gain, dres)


def _adamw(w, g, m, v):
    m2 = ADAM_B1 * m + (1.0 - ADAM_B1) * g
    v2 = ADAM_B2 * v + (1.0 - ADAM_B2) * (g * g)
    m_hat = m2 / (1.0 - ADAM_B1 ** ADAM_STEP)
    v_hat = v2 / (1.0 - ADAM_B2 ** ADAM_STEP)
    delta = -ADAM_LR * (m_hat / (jnp.sqrt(v_hat) + ADAM_EPS) + ADAM_WD * w)
    return delta, m2, v2


def _adam_big(name, parts, w, m, v, row_tiles):
    depth = w.shape[0]
    r, c = parts[0].shape[1], parts[0].shape[2]
    tr = r // row_tiles
    p_spec = pl.BlockSpec((N_DEV, tr, c), lambda t: (0, t, 0))
    w_spec = pl.BlockSpec((depth, tr, c), lambda t: (0, t, 0))

    def body(*refs):
        p_refs = refs[:depth]
        w_ref, m_ref, v_ref, g_ref, d_ref, m2_ref, v2_ref, token = refs[depth:]
        token[...] = jnp.zeros(token.shape, F32)
        for l in range(depth):
            g = p_refs[l][0].astype(F32)
            for dev in range(1, N_DEV):
                g = g + p_refs[l][dev].astype(F32)
            delta, m2, v2 = _adamw(w_ref[l], g, m_ref[l], v_ref[l])
            g_ref[l] = g
            d_ref[l] = delta
            m2_ref[l] = m2
            v2_ref[l] = v2

    return pl.pallas_call(
        body, grid=(row_tiles,), in_specs=[p_spec] * depth + [w_spec] * 3,
        out_specs=[w_spec] * 4 + [pl.BlockSpec((8, LANES), lambda t: (0, 0))],
        out_shape=[jax.ShapeDtypeStruct(w.shape, F32)] * 4 + [jax.ShapeDtypeStruct((8, LANES), F32)],
        compiler_params=_cparams(("arbitrary",)), name=name,
    )(*parts, w, m, v)


def _adam_small(g, ws, ms, vs):
    n = len(ws)

    def split(row, width):
        return jnp.concatenate([row[:, :width], row[:, width:2 * width]], axis=0)

    def body(g_ref, *refs):
        ins, outs = refs[:3 * n], refs[3 * n:]
        grads = [g_ref[0:2, :], split(g_ref[5:6, :], ATTN_W), split(g_ref[6:7, :], HGRN_W), split(g_ref[7:8, :], HGRN_DIM),
                 g_ref[2:4, :], g_ref[4:5, :]]
        for i, g_i in enumerate(grads):
            delta, m2, v2 = _adamw(ins[i][...], g_i, ins[n + i][...], ins[2 * n + i][...])
            for j, val in enumerate((g_i, delta, m2, v2)):
                outs[4 * i + j][...] = val

    vm = pl.BlockSpec(memory_space=pltpu.VMEM)
    res = pl.pallas_call(
        body, in_specs=[vm] * (1 + 3 * n), out_specs=[vm] * (4 * n),
        out_shape=[jax.ShapeDtypeStruct(w.shape, F32) for w in ws for _ in range(4)], name="adam_small",
    )(g, *ws, *ms, *vs)
    return [res[4 * i:4 * i + 4] for i in range(n)]


def _lower_bounds(logits):
    def body(l_ref, lb_ref, jac_ref):
        l0, l1 = l_ref[0:1, :], l_ref[1:2, :]
        mx = jnp.maximum(l0, l1)
        e0, e1 = jnp.exp(l0 - mx), jnp.exp(l1 - mx)
        p0, p1 = e0 / (e0 + e1), e1 / (e0 + e1)
        lb_ref[0:1, :] = p0 - p0
        lb_ref[1:2, :] = (p0 + p1) - p0
        jac_ref[0:1, :] = -p0 * p1
        jac_ref[1:2, :] = p0 * p1

    vm = pl.BlockSpec(memory_space=pltpu.VMEM)
    return pl.pallas_call(body, in_specs=[vm], out_specs=[vm, vm], out_shape=[jax.ShapeDtypeStruct(logits.shape, F32)] * 2,
                          name="hgrn_lower_bounds")(logits)


def _rope_tables(s, after):
    half = 32
    inv_freq = ROPE_THETA ** (-jnp.arange(half, dtype=F32) / half)
    ang = (jnp.arange(s, dtype=jnp.int32).astype(F32) + after[0, 0])[:, None] * inv_freq[None, :]
    cos, sin = jnp.cos(ang), jnp.sin(ang)
    return jnp.concatenate([cos] * 4, axis=1), jnp.concatenate([-sin, sin, -sin, sin], axis=1)


def kernel(x, norm_mix, w_in, attn_out_gain, hgrn_lb_logits, hgrn_out_gain, w_out, norm_mlp, w_up, w_down, norm_final, loss_target, m_norm_mix, m_w_in, m_attn_out_gain, m_hgrn_lb_logits, m_hgrn_out_gain, m_w_out, m_norm_mlp, m_w_up, m_w_down, m_norm_final, v_norm_mix, v_w_in, v_attn_out_gain, v_hgrn_lb_logits, v_hgrn_out_gain, v_w_out, v_norm_mlp, v_w_up, v_w_down, v_norm_final):
    depth = w_in.shape[0]
    assert depth == 2 and x.shape[0] == 1
    s, d = x.shape[1], x.shape[2]
    x0 = x[0]
    target = loss_target[0]
    g_mix, g_attn, g_hg, g_mlp = (norm_mix[:, None, :], attn_out_gain[:, None, :], hgrn_out_gain[:, None, :],
                                  norm_mlp[:, None, :])
    lb, lb_jac = _lower_bounds(hgrn_lb_logits)
    lb3 = lb[:, None, :]

    def flip(a):
        return jnp.swapaxes(a, 1, 2)

    shards = list(_pack_weights(flip(w_in), w_out, w_up, w_down))
    w_pieces = _weight_pieces(*shards)
    w_groups = [[0], [1, 2, 3], [4], [5, 6, 7]]
    me = (4 * lax.axis_index("x") + 2 * lax.axis_index("y") + lax.axis_index("c")).astype(jnp.int32).reshape(1)
    lands = _exchange_own("all_gather_own", me, shards, w_pieces)
    w_sems, shards, lands, token = _exchange_start("all_gather_start", shards, lands, w_pieces, w_groups)

    def weights_ready(group, after):
        nonlocal shards
        idxs = w_groups[group]
        shards, got = _exchange_wait(f"all_gather_wait{group}", shards, [lands[i] for i in idxs], w_pieces,
                                     [(idxs, *w_sems[group])], after)
        return got

    cos, sin = _rope_tables(s, token)

    def tied(small_arr, tok):
        return small_arr + tok[0, 0]

    saved = []
    xl = x0
    full = [None] * depth
    for l in range(depth):
        (full_in,) = weights_ready(2 * l, cos if l == 0 else xl)
        saved_x = xl
        proj, h = _fwd_inproj(l, xl, g_mix, full_in, cos, sin)
        o_attn, lse = _attn_fwd(l, proj)
        o_hg, mixed, states = _hgrn_fwd(l, proj, lb3, g_hg)
        mixed = _attn_norm(l, o_attn, g_attn, mixed)
        full_out, full_up, full_down = weights_ready(2 * l + 1, mixed)
        head = (norm_final[None, :], target) if l == depth - 1 else None
        xl, x_mid, u, h2, *loss_side = _mlp_fwd(l, xl, mixed, g_mlp, full_out, full_up, full_down, head)
        saved.append((saved_x, proj, h, o_attn, lse, o_hg, states, mixed, x_mid, u, h2))
        full[l] = (full_in, full_out, full_up, full_down)
    dx, (dxb, dnorm_final8, loss8) = xl, loss_side

    exchanges = []

    def scatter(tag, grads, kinds):
        pieces = _grad_pieces(grads, kinds)
        own = _exchange_own(f"reduce_scatter_own_{tag}", me, grads, pieces)
        sems, grads, own, tok = _exchange_start(f"reduce_scatter_start_{tag}", grads, own, pieces, [list(range(len(pieces)))])
        exchanges.append((grads, own, pieces, sems[0]))
        return tok

    small = {}
    for l in reversed(range(depth)):
        xl, proj, h, o_attn, lse, o_hg, states, mixed, x_mid, u, h2 = saved[l]
        full_in, full_out, full_up, full_down = full[l]
        hs = full_up.shape[3]
        gw_down = _mm_tn(f"grad_w_down_l{l}", u, dxb, u.shape[1], a_fn=_relu2)
        dx_mid, dx_mid_b, du, dmixed, dmlp8 = _mlp_bwd(l, dx, dxb, x_mid, g_mlp, u, full_out, full_up, full_down)
        gw_up = _mm_tn(f"grad_w_up_l{l}", h2, du, d, out_block_w=hs)
        gw_out = _mm_tn(f"grad_w_out_l{l}", mixed, dx_mid_b, mixed.shape[1])
        g_attn_t = tied(g_attn, scatter(f"mlp_l{l}", [gw_down, gw_up, gw_out], ["rows", "up", "rows"]))
        do, delta, dattn8 = _attn_norm_bwd(l, dmixed, o_attn, g_attn_t)
        dqkv = _attn_bwd(l, proj, do, lse, delta, cos, sin)
        dhg, dlb8, dhgain8 = _hgrn_bwd(l, proj, lb3, g_hg, o_hg, dmixed, states)
        gin = _mm_tn(f"grad_w_in_qkv_l{l}", dqkv, h, PROJ_W, a_lead=True)
        gw_in = _mm_tn(f"grad_w_in_hg_l{l}", dhg, h, PROJ_W, a_lead=True, out_block_off=3, prev=gin)
        g_mix_t = tied(g_mix, scatter(f"mix_l{l}", [gw_in], ["rows"]))
        dx, dxb, dmix8 = _bwd_inproj(l, dqkv, dhg, full_in, xl, g_mix_t, dx_mid)
        small[l] = (dmix8, dattn8, dlb8, dhgain8, dmlp8)

    def scattered(name, which, after):
        grads, lands, pieces, waits = [], [], [], []
        for grads_e, own, pieces_e, (send, recv) in (exchanges[i] for i in which):
            first = len(pieces)
            pieces += [p._replace(src=p.src + len(grads)) for p in pieces_e]
            waits.append((list(range(first, first + len(pieces_e))), send, recv))
            grads += grads_e
            lands += own
        return _exchange_wait(name, grads, lands, pieces, waits, after)[1]

    down1, up1, out1, in1, down0, up0, out0 = scattered("reduce_scatter_wait_early", (0, 1, 2), dx)
    big = {
        "w_down": _adam_big("adam_w_down", [down0, down1], w_down, m_w_down, v_w_down, 4),
        "w_up": _adam_big("adam_w_up", [up0, up1], w_up, m_w_up, v_w_up, 2),
        "w_out": _adam_big("adam_w_out", [out0, out1], w_out, m_w_out, v_w_out, 1),
    }
    g_small = _all_reduce_small([small[l][0] for l in range(depth)], [small[l][4] for l in range(depth)], dnorm_final8,
                                [small[l][1] for l in range(depth)], small[depth - 1][2], lb_jac,
                                [small[l][3] for l in range(depth)], loss8, big["w_out"][4])
    loss = g_small[7, 2 * HGRN_DIM]
    row = lambda a: a[None, :]
    small_out = _adam_small(
        g_small, [norm_mix, attn_out_gain, hgrn_lb_logits, hgrn_out_gain, norm_mlp, row(norm_final)],
        [m_norm_mix, m_attn_out_gain, m_hgrn_lb_logits, m_hgrn_out_gain, m_norm_mlp, row(m_norm_final)],
        [v_norm_mix, v_attn_out_gain, v_hgrn_lb_logits, v_hgrn_out_gain, v_norm_mlp, row(v_norm_final)])
    small_out[5] = [t[0] for t in small_out[5]]
    (in0,) = scattered("reduce_scatter_wait_last", (3,), small_out[0][1])
    big["w_in"] = [flip(t) for t in _adam_big("adam_w_in", [in0, in1], flip(w_in), flip(m_w_in), flip(v_w_in), 2)[:4]]

    def gather(idx):
        mix, attn, lbl, hg, mlp, final = (t[idx] for t in small_out)
        return [mix, big["w_in"][idx], attn, lbl, hg, big["w_out"][idx], mlp, big["w_up"][idx], big["w_down"][idx], final]

    return (loss, dx[None], *gather(0), *gather(1), *gather(2), *gather(3))
```

```python
import functools
from typing import Callable, NamedTuple

import jax
import jax.numpy as jnp
from jax import lax
from jax.experimental import pallas as pl
from jax.experimental.pallas import tpu as pltpu

F32 = jnp.float32
BF = jnp.bfloat16

N_DEV = 8
ATTN_W = 512
HGRN_W = 512
HGRN_HEADS = 4
HGRN_DIM = 128
SEG = 512
N_SEG = 7
PROJ_W = N_SEG * SEG
MIX_W = ATTN_W + HGRN_W
SPAN = 128
DILATIONS = (1, 4, 16)
HGRN_CHUNK = 16
ROPE_THETA = 10000.0
NORM_EPS = 1e-6
MASK_VALUE = -1e30
ATTN_SCALE = 0.125
HGRN_SCALE = HGRN_DIM ** -0.5
ADAM_LR = 0.001
ADAM_B1 = 0.9
ADAM_B2 = 0.999
ADAM_EPS = 1e-08
ADAM_WD = 0.01
ADAM_STEP = 10
LANES = 128
VMEM_LIMIT = 56 * 1024 * 1024

NN = ((1,), (0,))
NT = ((1,), (1,))
TN = ((0,), (0,))
MESH = pl.DeviceIdType.MESH


def _dot(a, b, dims):
    return lax.dot_general(a, b, (dims, ((), ())), preferred_element_type=F32)


def _cparams(sem):
    return pltpu.CompilerParams(dimension_semantics=sem, vmem_limit_bytes=VMEM_LIMIT)


def _part8(x):
    r, n = x.shape
    return jnp.sum(x.reshape(r // 8, 8, n), axis=0)


def _sigmoid(x):
    return 1.0 / (1.0 + jnp.exp(-x))


def _rms_fwd(x, gain):
    r = lax.rsqrt(jnp.mean(x * x, axis=-1, keepdims=True) + NORM_EPS)
    return x * r * gain


def _rms_bwd(dy, x, gain):
    r = lax.rsqrt(jnp.mean(x * x, axis=-1, keepdims=True) + NORM_EPS)
    xn = x * r
    dxn = dy * gain
    dx = r * (dxn - xn * jnp.mean(dxn * xn, axis=-1, keepdims=True))
    return dx, dy * xn


def _rope_partner(x):
    n = x.shape[-1]
    lane = lax.broadcasted_iota(jnp.int32, x.shape, x.ndim - 1)
    return jnp.where((lane % 64) < 32, pltpu.roll(x, n - 32, x.ndim - 1), pltpu.roll(x, 32, x.ndim - 1))


def _tile_lanes(t, reps):
    return jnp.concatenate([t] * reps, axis=-1)


def _mm_tn(name, a, b, out_rows, a_lead=False, out_block_off=0, prev=None, out_block_w=None, a_fn=None,
           tm=512, tn=1024, sub=512):
    kdim, n = b.shape
    m = a.shape[-1]
    tm, tn, sub = min(tm, m), min(tn, n), min(sub, kdim)
    mt = m // tm
    n_lead = a.shape[0] if a_lead else 1
    if a_lead:
        a_spec = pl.BlockSpec((None, kdim, tm), lambda j, i: (i // mt, 0, i % mt))
    else:
        a_spec = pl.BlockSpec((kdim, tm), lambda j, i: (0, i))
    b_spec = pl.BlockSpec((kdim, tn), lambda j, i: (0, j))
    if out_block_w:
        nb = tn // out_block_w
        o_shape = jax.ShapeDtypeStruct((n // out_block_w, out_rows, out_block_w), BF)
        o_spec = pl.BlockSpec((nb, tm, out_block_w), lambda j, i: (j, i + out_block_off, 0))
    else:
        nb = 0
        o_shape = jax.ShapeDtypeStruct((out_rows, n), BF)
        o_spec = pl.BlockSpec((tm, tn), lambda j, i: (i + out_block_off, j))
    arrays, specs, aliases = [a, b], [a_spec, b_spec], {}
    if prev is not None:
        arrays.append(prev)
        specs.append(pl.BlockSpec(memory_space=pl.ANY))
        aliases = {2: 0}

    def body(*refs):
        a_ref, b_ref, o_ref = refs[0], refs[1], refs[-1]
        acc = None
        for k in range(kdim // sub):
            av = a_ref[k * sub:(k + 1) * sub, :]
            if a_fn is not None:
                av = a_fn(av)
            part = _dot(av, b_ref[k * sub:(k + 1) * sub, :], TN)
            acc = part if acc is None else acc + part
        if nb:
            for t in range(nb):
                o_ref[t] = acc[:, t * out_block_w:(t + 1) * out_block_w].astype(BF)
        else:
            o_ref[...] = acc.astype(BF)

    return pl.pallas_call(
        body, grid=(n // tn, n_lead * mt), in_specs=specs, out_specs=o_spec, out_shape=o_shape,
        compiler_params=_cparams(("parallel", "parallel")), name=name, input_output_aliases=aliases,
    )(*arrays)


def _pack_weights(w_in_t, w_out, w_up, w_down):
    depth = w_in_t.shape[0]
    arrays = (w_in_t, w_out, w_up, w_down)

    def body(*refs):
        for src, dst in zip(refs[:4], refs[4:]):
            dst[...] = src[...].astype(BF)

    specs = [pl.BlockSpec((None,) + a.shape[1:], lambda l: (l, 0, 0)) for a in arrays]
    return pl.pallas_call(
        body, grid=(depth,), in_specs=specs, out_specs=specs,
        out_shape=[jax.ShapeDtypeStruct(a.shape, BF) for a in arrays],
        compiler_params=_cparams(("arbitrary",)), name="pack_weights",
    )(*arrays)


def _my_position():
    x, y, c = lax.axis_index("x"), lax.axis_index("y"), lax.axis_index("c")
    return x, y, c, 4 * x + 2 * y + c


def _peer(x, y, c, k):
    px = 1 - x if k & 4 else x
    py = 1 - y if k & 2 else y
    pc = 1 - c if k & 1 else c
    return (px, py, pc), 4 * px + 2 * py + pc


PEER_ORDER = (1, 2, 4, 3, 5, 6, 7)


class _Piece(NamedTuple):
    src: int
    send: Callable
    slot: Callable
    land_shape: tuple
    own_src: tuple
    own_slot: tuple


HBM_SPEC = pl.BlockSpec(memory_space=pltpu.HBM)
SEM_SPEC = pl.BlockSpec(memory_space=pltpu.SEMAPHORE)
ANY_SPEC = pl.BlockSpec(memory_space=pl.ANY)


def _in_hbm(arrays):
    return [pltpu.with_memory_space_constraint(a, pltpu.HBM) for a in arrays]


def _hbm_like(arrays):
    return [pltpu.HBM(a.shape, a.dtype) for a in arrays]


def _rows_of(rows):
    return lambda ref, dev: ref.at[pl.ds(pl.multiple_of(dev * rows, 16), rows), :]


def _exchange_own(name, me, srcs, pieces):
    n = len(pieces)

    def body(me_ref, *refs):
        for i in range(n):
            refs[n + i][...] = refs[i][...]

    def spec(block_and_index):
        block, index = block_and_index
        return pl.BlockSpec(block, lambda i, me_ref: index(me_ref[0]))

    return pl.pallas_call(
        body,
        grid_spec=pltpu.PrefetchScalarGridSpec(
            num_scalar_prefetch=1, grid=(1,), in_specs=[spec(p.own_src) for p in pieces],
            out_specs=[spec(p.own_slot) for p in pieces]),
        out_shape=[jax.ShapeDtypeStruct(p.land_shape, BF) for p in pieces],
        compiler_params=_cparams(("arbitrary",)), name=name,
    )(me, *[srcs[p.src] for p in pieces])


def _exchange_start(name, srcs, lands, pieces, groups):
    n_src, n, n_g = len(srcs), len(pieces), len(groups)

    def body(*refs):
        src_refs, land_refs = refs[:n_src], refs[n_src:n_src + n]
        sems, token = refs[n_src + n:n_src + n + 2 * n_g], refs[-1]
        x, y, c, me = _my_position()
        for g, idxs in enumerate(groups):
            for k in PEER_ORDER:
                peer, pid = _peer(x, y, c, k)
                for j, i in enumerate(idxs):
                    p = pieces[i]
                    pltpu.make_async_remote_copy(
                        src_ref=p.send(src_refs[p.src], pid), dst_ref=p.slot(land_refs[i], me),
                        send_sem=sems[2 * g].at[(k - 1) * len(idxs) + j], recv_sem=sems[2 * g + 1].at[(k - 1) * len(idxs) + j],
                        device_id=peer, device_id_type=MESH).start()
        token[...] = jnp.zeros(token.shape, F32)

    sem_shapes = [pltpu.SemaphoreType.DMA(((N_DEV - 1) * len(idxs),)) for idxs in groups for _ in range(2)]
    res = pl.pallas_call(
        body, in_specs=[HBM_SPEC] * (n_src + n),
        out_specs=[SEM_SPEC] * (2 * n_g) + [HBM_SPEC] * (n_src + n) + [pl.BlockSpec(memory_space=pltpu.VMEM)],
        out_shape=sem_shapes + _hbm_like(srcs) + _hbm_like(lands) + [jax.ShapeDtypeStruct((8, LANES), F32)],
        input_output_aliases={i: 2 * n_g + i for i in range(n_src + n)},
        compiler_params=pltpu.CompilerParams(has_side_effects=pltpu.SideEffectType.DATAFLOW_SIDE_EFFECTING),
        name=name,
    )(*_in_hbm(srcs), *_in_hbm(lands))
    sems = [(res[2 * g], res[2 * g + 1]) for g in range(n_g)]
    return sems, list(res[2 * n_g:2 * n_g + n_src]), list(res[2 * n_g + n_src:2 * n_g + n_src + n]), res[-1]


def _exchange_wait(name, srcs, lands, pieces, waits, after):
    n_src, n, n_g = len(srcs), len(lands), len(waits)

    def body(*refs):
        src_refs, land_refs = refs[:n_src], refs[n_src:n_src + n]
        sems = refs[n_src + n:n_src + n + 2 * n_g]
        x, y, c, me = _my_position()
        at = 0
        for g, (idxs, _, _) in enumerate(waits):
            for k in PEER_ORDER:
                peer, pid = _peer(x, y, c, k)
                for j, i in enumerate(idxs):
                    p = pieces[i]
                    cp = pltpu.make_async_remote_copy(
                        src_ref=p.send(src_refs[p.src], pid), dst_ref=p.slot(land_refs[at + j], pid),
                        send_sem=sems[2 * g].at[(k - 1) * len(idxs) + j], recv_sem=sems[2 * g + 1].at[(k - 1) * len(idxs) + j],
                        device_id=peer, device_id_type=MESH)
                    cp.wait_send()
                    cp.wait_recv()
            at += len(idxs)

    sem_args = [s for _, send, recv in waits for s in (send, recv)]
    res = pl.pallas_call(
        body, in_specs=[HBM_SPEC] * (n_src + n) + [SEM_SPEC] * (2 * n_g) + [ANY_SPEC],
        out_specs=[HBM_SPEC] * (n_src + n), out_shape=_hbm_like(srcs) + _hbm_like(lands),
        input_output_aliases={i: i for i in range(n_src + n)},
        compiler_params=pltpu.CompilerParams(has_side_effects=pltpu.SideEffectType.DATAFLOW_SIDE_EFFECTING),
        name=name,
    )(*srcs, *lands, *sem_args, after)
    return list(res[:n_src]), list(res[n_src:])


def _weight_pieces(p_in, p_out, p_up, p_down):
    depth, cin, d = p_in.shape
    rout, hs = p_out.shape[1], p_up.shape[2]
    pieces = []
    for l in range(depth):
        whole = functools.partial(lambda ref, dev, l: ref.at[l], l=l)
        layer = functools.partial(lambda dev, l: (l, 0, 0), l=l)

        def rows(src, n_rows, whole=whole, layer=layer):
            return _Piece(src, whole, lambda ref, dev: _rows_of(n_rows)(ref.at[0], dev), (1, N_DEV * n_rows, d),
                          ((None, n_rows, d), layer), ((None, n_rows, d), lambda dev: (0, dev, 0)))

        pieces += [
            rows(0, cin), rows(1, rout),
            _Piece(2, whole, lambda ref, dev: ref.at[0, dev], (1, N_DEV, d, hs),
                   ((None, d, hs), layer), ((None, None, d, hs), lambda dev: (0, dev, 0, 0))),
            rows(3, hs),
        ]
    return pieces


def _grad_pieces(g_pair, kinds):
    pieces = []
    for i, (g, kind) in enumerate(zip(g_pair, kinds)):
        lead = lambda dev: (dev, 0, 0)
        if kind == "up":
            blk = ((None,) + g.shape[1:], lead)
            pieces.append(_Piece(i, lambda ref, dev: ref.at[dev], lambda ref, dev: ref.at[dev], g.shape, blk, blk))
        else:
            rows, cols = g.shape[0] // N_DEV, g.shape[1]
            pieces.append(_Piece(i, _rows_of(rows), lambda ref, dev: ref.at[dev], (N_DEV, rows, cols),
                                 ((rows, cols), lambda dev: (dev, 0)), ((None, rows, cols), lead)))
    return pieces


SMALL_W = 1024


def _all_reduce_small(mix8, mlp8, final8, attn8, lb8_last, lb_jac, hg8, loss8, after):
    def body(mix0, mix1, mlp0, mlp1, fin, attn0, attn1, lb, jac, hg0, hg1, loss, after_ref, o_ref, src_ref, buf_ref,
             send_sems, recv_sems):
        def total(ref):
            return jnp.sum(ref[...], axis=0, keepdims=True)

        dlb = total(lb)
        hg = jnp.concatenate([total(hg0), total(hg1), total(loss)], axis=1)
        src_ref[...] = jnp.concatenate([
            total(mix0), total(mix1), total(mlp0), total(mlp1), total(fin),
            jnp.concatenate([total(attn0), total(attn1)], axis=1),
            jnp.concatenate([jac[0:1, :] * dlb, jac[1:2, :] * dlb], axis=1),
            jnp.concatenate([hg, jnp.zeros((1, SMALL_W - hg.shape[1]), F32)], axis=1)], axis=0)
        x, y, c, me = _my_position()
        buf_ref[me] = src_ref[...]
        sends = []
        for k in PEER_ORDER:
            peer, _ = _peer(x, y, c, k)
            cp = pltpu.make_async_remote_copy(src_ref=src_ref, dst_ref=buf_ref.at[me], send_sem=send_sems.at[k - 1],
                                              recv_sem=recv_sems.at[k - 1], device_id=peer, device_id_type=MESH)
            cp.start()
            sends.append(cp)
        for k in PEER_ORDER:
            peer, pid = _peer(x, y, c, k)
            pltpu.make_async_remote_copy(src_ref=src_ref, dst_ref=buf_ref.at[pid], send_sem=send_sems.at[k - 1],
                                         recv_sem=recv_sems.at[k - 1], device_id=peer, device_id_type=MESH).wait_recv()
        for cp in sends:
            cp.wait_send()
        acc = buf_ref[0]
        for dev in range(1, N_DEV):
            acc = acc + buf_ref[dev]
        o_ref[...] = acc

    assert mix8[0].shape[1] == SMALL_W
    vm = pl.BlockSpec(memory_space=pltpu.VMEM)
    return pl.pallas_call(
        body, in_specs=[vm] * 12 + [ANY_SPEC], out_specs=vm, out_shape=jax.ShapeDtypeStruct((8, SMALL_W), F32),
        scratch_shapes=[pltpu.VMEM((8, SMALL_W), F32), pltpu.VMEM((N_DEV, 8, SMALL_W), F32),
                        pltpu.SemaphoreType.DMA((N_DEV - 1,)), pltpu.SemaphoreType.DMA((N_DEV - 1,))],
        name="all_reduce_small",
    )(*mix8, *mlp8, final8, *attn8, lb8_last, lb_jac, *hg8, loss8, after)


def _resident(block_shape, index_map):
    return pl.BlockSpec(block_shape, index_map, pipeline_mode=pl.Buffered(1))


def _fwd_inproj(layer, x, gain, g_in, cos, sin, tm=512):
    s, d = x.shape

    def body(x_ref, gain_ref, w_ref, cos_ref, sin_ref, proj_ref, h_ref):
        h = _rms_fwd(x_ref[...], gain_ref[...]).astype(BF)
        h_ref[...] = h
        cs = _tile_lanes(cos_ref[...], SEG // LANES)
        sn = _tile_lanes(sin_ref[...], SEG // LANES)
        for seg in range(N_SEG):
            acc = _dot(h, w_ref[seg * SEG:(seg + 1) * SEG, :], NT)
            if seg < 2:
                acc = acc * cs + _rope_partner(acc) * sn
            if seg == 0:
                acc = acc * ATTN_SCALE
            proj_ref[:, seg * SEG:(seg + 1) * SEG] = acc

    return pl.pallas_call(
        body, grid=(s // tm,),
        in_specs=[pl.BlockSpec((tm, d), lambda i: (i, 0)), pl.BlockSpec((None, 1, d), lambda i: (layer, 0, 0)),
                  _resident((None, PROJ_W, d), lambda i: (0, 0, 0)),
                  pl.BlockSpec((tm, LANES), lambda i: (i, 0)), pl.BlockSpec((tm, LANES), lambda i: (i, 0))],
        out_specs=[pl.BlockSpec((tm, PROJ_W), lambda i: (i, 0)), pl.BlockSpec((tm, d), lambda i: (i, 0))],
        out_shape=[jax.ShapeDtypeStruct((s, PROJ_W), F32), jax.ShapeDtypeStruct((s, d), BF)],
        compiler_params=_cparams(("parallel",)), name=f"fwd_inproj_l{layer}",
    )(x, gain, g_in, cos, sin)


ATTN_UNIT = SPAN * max(DILATIONS)
ATTN_GROUP = 4


def _attn_masks(first_block_has_prev):
    row = lax.broadcasted_iota(jnp.int32, (SPAN, 2 * SPAN), 0)
    col = lax.broadcasted_iota(jnp.int32, (SPAN, 2 * SPAN), 1)
    band = (col >= row) & (col <= row + SPAN)
    lane = lax.broadcasted_iota(jnp.int32, (SPAN, LANES), 1)
    return band & ((col >= SPAN) | first_block_has_prev), band, lane < 64


def _attn_specs(n_in_extra, unit_of=lambda n: n):
    pairs = ATTN_W // LANES
    q_spec = pl.BlockSpec((ATTN_UNIT, LANES), lambda p, n: (unit_of(n), p))

    def prev(seg):
        return pl.BlockSpec((ATTN_UNIT, LANES), lambda p, n: (jnp.maximum(unit_of(n) - 1, 0), seg * pairs + p))

    def cur(seg):
        return pl.BlockSpec((ATTN_UNIT, LANES), lambda p, n: (unit_of(n), seg * pairs + p))

    return [q_spec, prev(1), cur(1), prev(2), cur(2)] + [q_spec] * n_in_extra


def _attn_groups(dil):
    blocks = ATTN_UNIT // (SPAN * dil)
    pairs = [(r, b) for r in range(dil) for b in range(blocks)]
    return [pairs[i:i + ATTN_GROUP] for i in range(0, len(pairs), ATTN_GROUP)]


def _block_rows(dil, r, b, n=1):
    start = r + dil * SPAN * b
    return pl.ds(start, n * SPAN, stride=dil) if dil > 1 else pl.ds(start, n * SPAN)


def _block_keys(prev_ref, cur_ref, dil, r, b):
    if b > 0:
        return cur_ref[_block_rows(dil, r, b - 1, 2), :]
    last = ATTN_UNIT // (SPAN * dil) - 1
    return jnp.concatenate([prev_ref[_block_rows(dil, r, last), :], cur_ref[_block_rows(dil, r, 0), :]], axis=0)


def _attn_fwd(layer, proj):
    s = proj.shape[0]
    n_pat = len(DILATIONS)
    merge_rows = 256

    def body(q_ref, kp_ref, kc_ref, vp_ref, vc_ref, o_ref, lse_ref, o_scr, lse_scr):
        m_first, m_rest, is_a = _attn_masks(pl.program_id(1) > 0)
        sels = (is_a, jnp.logical_not(is_a))
        is_a_keys = lax.broadcasted_iota(jnp.int32, (2 * SPAN, LANES), 1) < 64
        for pi, dil in enumerate(DILATIONS):
            for group in _attn_groups(dil):
                items = [(r, b, h) for r, b in group for h in range(2)]
                q = {rb: q_ref[_block_rows(dil, *rb), :] for rb in group}
                k = {rb: _block_keys(kp_ref, kc_ref, dil, *rb).astype(BF) for rb in group}
                v = {rb: _block_keys(vp_ref, vc_ref, dil, *rb).astype(BF) for rb in group}
                v_sum = {rb: (jnp.where(is_a_keys, v[rb], 1.0), jnp.where(is_a_keys, 1.0, v[rb])) for rb in group}
                sc = [jnp.where(m_first if b == 0 else m_rest,
                                _dot(jnp.where(sels[h], q[r, b], 0.0).astype(BF), k[r, b], NT), MASK_VALUE)
                      for r, b, h in items]
                mx = [jnp.max(jnp.maximum(t[:, :SPAN], t[:, SPAN:]), axis=-1, keepdims=True) for t in sc]
                p = [jnp.exp(t - m).astype(BF) for t, m in zip(sc, mx)]
                both = [_dot(t, v_sum[r, b][h], NN) for t, (r, b, h) in zip(p, items)]
                for j, (r, b) in enumerate(group):
                    t_a, t_b = both[2 * j], both[2 * j + 1]
                    den = pltpu.roll(jnp.where(is_a, t_b, t_a), 64, 1)
                    o_scr[pi, _block_rows(dil, r, b), :] = jnp.where(is_a, t_a, t_b) / den
                    lse_scr[pi, _block_rows(dil, r, b), :] = jnp.where(is_a, mx[2 * j], mx[2 * j + 1]) + jnp.log(den)
        for c in range(ATTN_UNIT // merge_rows):
            rows = slice(c * merge_rows, (c + 1) * merge_rows)
            ls = [lse_scr[pi, rows, :] for pi in range(n_pat)]
            mx = functools.reduce(jnp.maximum, ls)
            ws = [jnp.exp(l - mx) for l in ls]
            den = functools.reduce(jnp.add, ws)
            o_ref[rows, :] = functools.reduce(jnp.add, [w * o_scr[pi, rows, :] for pi, w in enumerate(ws)]) / den
            lse_ref[rows, :] = mx + jnp.log(den)

    out_spec = pl.BlockSpec((ATTN_UNIT, LANES), lambda p, n: (n, p))
    return pl.pallas_call(
        body, grid=(ATTN_W // LANES, s // ATTN_UNIT), in_specs=_attn_specs(0), out_specs=[out_spec, out_spec],
        out_shape=[jax.ShapeDtypeStruct((s, ATTN_W), F32)] * 2,
        scratch_shapes=[pltpu.VMEM((n_pat, ATTN_UNIT, LANES), F32)] * 2,
        compiler_params=_cparams(("parallel", "arbitrary")), name=f"attn_fwd_l{layer}",
    )(proj, proj, proj, proj, proj)


def _attn_norm(layer, o, gain, mixed, tm=512):
    s = o.shape[0]

    def body(o_ref, gain_ref, mixed_ref, n_ref):
        n_ref[...] = _rms_fwd(o_ref[...], gain_ref[...]).astype(BF)

    blk = pl.BlockSpec((tm, ATTN_W), lambda i: (i, 0))
    return pl.pallas_call(
        body, grid=(s // tm,),
        in_specs=[blk, pl.BlockSpec((None, 1, ATTN_W), lambda i: (layer, 0, 0)), pl.BlockSpec(memory_space=pl.ANY)],
        out_specs=blk, out_shape=jax.ShapeDtypeStruct(mixed.shape, BF), input_output_aliases={2: 0},
        compiler_params=_cparams(("parallel",)), name=f"attn_norm_l{layer}",
    )(o, gain, mixed)


def _chunk_cumsum(x, reverse=False):
    n = x.shape[0]
    pos = lax.broadcasted_iota(jnp.int32, x.shape, 0) % HGRN_CHUNK
    for sh in (1, 2, 4, 8):
        if reverse:
            x = x + jnp.where(pos < HGRN_CHUNK - sh, pltpu.roll(x, n - sh, 0), 0.0)
        else:
            x = x + jnp.where(pos >= sh, pltpu.roll(x, sh, 0), 0.0)
    return x


def _chunk_row(x, row):
    r, n = x.shape
    x3 = x.reshape(r // HGRN_CHUNK, HGRN_CHUNK, n)
    return jnp.broadcast_to(x3[:, row:row + 1, :], x3.shape).reshape(r, n)


def _hgrn_pre(qh, z, lb):
    sig = _sigmoid(z)
    f = lb + (1.0 - lb) * sig
    k = 1.0 - f
    sq = _sigmoid(qh)
    q = qh * sq * HGRN_SCALE
    g = _chunk_cumsum(jnp.log(f))
    g_mid = _chunk_row(g, HGRN_CHUNK // 2 - 1)
    g_last = _chunk_row(g, HGRN_CHUNK - 1)
    e_q, e_k = jnp.exp(g - g_mid), jnp.exp(g_mid - g)
    e_in, e_out = jnp.exp(g), jnp.exp(g_last - g)
    return dict(sig=sig, f=f, k=k, sq=sq, q=q, g_last=g_last, e_q=e_q, e_k=e_k, e_in=e_in, e_out=e_out,
                qt=q * e_q, kt=k * e_k, qg=q * e_in, kout=k * e_out)


def _hgrn_mask():
    row = lax.broadcasted_iota(jnp.int32, (LANES, LANES), 0)
    col = lax.broadcasted_iota(jnp.int32, (LANES, LANES), 1)
    return (row // HGRN_CHUNK == col // HGRN_CHUNK) & (col <= row)


def _hgrn_in_specs(layer, rev, nblk):
    def blk(b):
        return nblk - 1 - b if rev else b
    first = 3 * ATTN_W // HGRN_W
    specs = [pl.BlockSpec((HGRN_ROWS, HGRN_W), functools.partial(lambda b, seg: (blk(b), first + seg), seg=seg))
             for seg in range(4)]
    specs.append(pl.BlockSpec((None, 1, HGRN_W), lambda b: (layer, 0, 0)))
    specs.append(pl.BlockSpec((None, 1, HGRN_DIM), lambda b: (layer, 0, 0)))
    return specs, blk


def _head(x, h):
    return x[:, h * HGRN_DIM:(h + 1) * HGRN_DIM]


def _chunk(x, c):
    return x[c * HGRN_CHUNK:(c + 1) * HGRN_CHUNK]


def _sub(x, sb):
    return x[sb * LANES:(sb + 1) * LANES]


HGRN_ROWS = 256
HEADS = range(HGRN_HEADS)
SUBS = range(HGRN_ROWS // LANES)
CHUNKS = range(HGRN_ROWS // HGRN_CHUNK)


def _hgrn_fwd(layer, proj, lb, gain):
    s = proj.shape[0]
    nblk = s // HGRN_ROWS
    cpb = len(CHUNKS)

    def body(q_ref, f_ref, i_ref, g_ref, lb_ref, gain_ref, o_ref, rec_ref, st_ref, state):
        @pl.when(pl.program_id(0) == 0)
        def _():
            state[...] = jnp.zeros(state.shape, F32)

        pre = _hgrn_pre(q_ref[...], f_ref[...], lb_ref[...])
        v = i_ref[...].astype(BF)
        qt, kt, qg, kout = (pre[n].astype(BF) for n in ("qt", "kt", "qg", "kout"))
        dec = jnp.exp(pre["g_last"])
        mask = _hgrn_mask()
        a = [[jnp.where(mask, _dot(_sub(_head(qt, h), sb), _sub(_head(kt, h), sb), NT), 0.0).astype(BF) for sb in SUBS]
             for h in HEADS]
        o_intra = [[_dot(a[h][sb], _sub(_head(v, h), sb), NN) for sb in SUBS] for h in HEADS]
        update = [[_dot(_chunk(_head(v, h), c), _chunk(_head(kout, h), c), TN) for c in CHUNKS] for h in HEADS]
        for h in HEADS:
            st = state[h]
            for c in CHUNKS:
                st_ref[h, c * LANES:(c + 1) * LANES, :] = st.astype(BF)
                st = st * _head(dec, h)[c * HGRN_CHUNK:c * HGRN_CHUNK + 1, :] + update[h][c]
            state[h] = st
        inter = [[_dot(_chunk(_head(qg, h), c), st_ref[h, c * LANES:(c + 1) * LANES, :].astype(BF), NT) for c in CHUNKS]
                 for h in HEADS]
        o = [jnp.concatenate(o_intra[h], axis=0) + jnp.concatenate(inter[h], axis=0) for h in HEADS]
        o_ref[...] = jnp.concatenate(o, axis=1)
        gate = g_ref[...]
        normed = jnp.concatenate([_rms_fwd(o[h], gain_ref[...]) for h in HEADS], axis=1)
        rec_ref[...] = (normed * (gate * _sigmoid(gate))).astype(BF)

    specs, _ = _hgrn_in_specs(layer, False, nblk)
    return pl.pallas_call(
        body, grid=(nblk,), in_specs=specs,
        out_specs=[pl.BlockSpec((HGRN_ROWS, HGRN_W), lambda b: (b, 0)), pl.BlockSpec((HGRN_ROWS, HGRN_W), lambda b: (b, 1)),
                   pl.BlockSpec((HGRN_HEADS, cpb * LANES, LANES), lambda b: (0, b, 0))],
        out_shape=[jax.ShapeDtypeStruct((s, HGRN_W), F32), jax.ShapeDtypeStruct((s, MIX_W), BF),
                   jax.ShapeDtypeStruct((HGRN_HEADS, nblk * cpb * LANES, LANES), BF)],
        scratch_shapes=[pltpu.VMEM((HGRN_HEADS, LANES, LANES), F32)],
        compiler_params=_cparams(("arbitrary",)), name=f"hgrn_fwd_l{layer}",
    )(proj, proj, proj, proj, lb, gain)


def _relu2(u):
    return jnp.square(jnp.maximum(u, 0)).astype(BF)


def _mlp_fwd(layer, x, mixed, gain, g_out, g_up, g_down, head=None):
    s, d = x.shape
    mw = mixed.shape[1]
    nblk, hs = g_up.shape[1], g_up.shape[3]
    tm = 256 if head else 512

    def body(x_ref, m_ref, gain_ref, out_w_ref, up_ref, down_ref, *refs):
        if head:
            fin_ref, t_ref, o_ref, mid_ref, u_ref, h_ref, ob_ref, dfin_ref, loss_ref, a_buf = refs
        else:
            o_ref, mid_ref, u_ref, h_ref, a_buf = refs
        xv = x_ref[...] + _dot(m_ref[...], out_w_ref[...], NN)
        mid_ref[...] = xv
        h = _rms_fwd(xv, gain_ref[...]).astype(BF)
        h_ref[...] = h
        for j in range(nblk):
            u = _dot(h, up_ref[j], NN)
            u_ref[:, j * hs:(j + 1) * hs] = u.astype(BF)
            a_buf[:, j * hs:(j + 1) * hs] = _relu2(u)
        acc = xv
        for j in range(nblk):
            acc = acc + _dot(a_buf[:, j * hs:(j + 1) * hs], down_ref[j * hs:(j + 1) * hs, :], NN)
        if not head:
            o_ref[...] = acc
            return
        fin = fin_ref[...]
        err = _rms_fwd(acc, fin) - t_ref[...]
        dout, dfin = _rms_bwd(err * (1.0 / d), acc, fin)
        o_ref[...] = dout
        ob_ref[...] = dout.astype(BF)
        step = pl.program_id(0)
        _accumulate_rows(step, dfin_ref, _part8(dfin))
        _accumulate_rows(step, loss_ref, _part8(0.5 * jnp.mean(err * err, axis=-1, keepdims=True) * jnp.ones((1, LANES), F32)))

    row = pl.BlockSpec((tm, d), lambda i: (i, 0))
    in_specs = [row, pl.BlockSpec((tm, mw), lambda i: (i, 0)), pl.BlockSpec((None, 1, d), lambda i: (layer, 0, 0)),
                _resident((None, mw, d), lambda i: (0, 0, 0)),
                _resident((None, nblk, d, hs), lambda i: (0, 0, 0, 0)),
                _resident((None, nblk * hs, d), lambda i: (0, 0, 0))]
    out_specs = [row, row, pl.BlockSpec((tm, nblk * hs), lambda i: (i, 0)), row]
    out_shape = [jax.ShapeDtypeStruct((s, d), F32), jax.ShapeDtypeStruct((s, d), F32),
                 jax.ShapeDtypeStruct((s, nblk * hs), BF), jax.ShapeDtypeStruct((s, d), BF)]
    args = [x, mixed, gain, g_out, g_up, g_down]
    if head:
        in_specs += [pl.BlockSpec((1, d), lambda i: (0, 0)), row]
        out_specs += [row, pl.BlockSpec((8, d), lambda i: (0, 0)), pl.BlockSpec((8, LANES), lambda i: (0, 0))]
        out_shape += [jax.ShapeDtypeStruct((s, d), BF), jax.ShapeDtypeStruct((8, d), F32), jax.ShapeDtypeStruct((8, LANES), F32)]
        args += list(head)
    return pl.pallas_call(
        body, grid=(s // tm,), in_specs=in_specs, out_specs=out_specs, out_shape=out_shape,
        scratch_shapes=[pltpu.VMEM((tm, nblk * hs), BF)],
        compiler_params=_cparams(("arbitrary",) if head else ("parallel",)), name=f"mlp_fwd_l{layer}",
    )(*args)


def _accumulate_rows(i, ref, part):
    @pl.when(i == 0)
    def _():
        ref[...] = part

    @pl.when(i > 0)
    def _():
        ref[...] += part


def _head_sums(prod):
    row = lax.broadcasted_iota(jnp.int32, (ATTN_W, ATTN_W), 0)
    col = lax.broadcasted_iota(jnp.int32, (ATTN_W, ATTN_W), 1)
    same_head = jnp.where(row // 64 == col // 64, 1.0, 0.0).astype(BF)
    high = prod.astype(BF)
    low = (prod - high.astype(F32)).astype(BF)
    return _dot(high, same_head, NN) + _dot(low, same_head, NN)


def _mlp_bwd(layer, dx, dxb, x, gain, u, o_attn, gain_attn, g_out, g_up, g_down, tm=256):
    s, d = x.shape
    mw = g_out.shape[1]
    nblk, hs = g_up.shape[1], g_up.shape[3]

    def body(dx_ref, dxb_ref, x_ref, gain_ref, u_ref, oa_ref, ga_ref, out_w_ref, up_ref, down_ref, o_ref, ob_ref, du_ref,
             drec_ref, do_ref, delta_ref, dgain_ref, dattn_ref):
        dxb_v = dxb_ref[...]
        for j in range(nblk):
            cols = slice(j * hs, (j + 1) * hs)
            da = _dot(dxb_v, down_ref[cols, :], NT)
            du_ref[:, cols] = (da * (2.0 * jnp.maximum(u_ref[:, cols].astype(F32), 0.0))).astype(BF)
        acc = jnp.zeros((tm, d), F32)
        for j in range(nblk):
            acc = acc + _dot(du_ref[:, j * hs:(j + 1) * hs], up_ref[j], NT)
        dxn, dgain = _rms_bwd(acc, x_ref[...], gain_ref[...])
        out = dx_ref[...] + dxn
        out_b = out.astype(BF)
        o_ref[...] = out
        ob_ref[...] = out_b
        dm = _dot(out_b, out_w_ref[...], NT)
        drec_ref[...] = dm[:, ATTN_W:]
        ov = oa_ref[...]
        do, dattn = _rms_bwd(dm[:, :ATTN_W], ov, ga_ref[...])
        do_ref[...] = do
        delta_ref[...] = _head_sums(do * ov)
        _accumulate_rows(pl.program_id(0), dgain_ref, _part8(dgain))
        _accumulate_rows(pl.program_id(0), dattn_ref, _part8(dattn))

    row = pl.BlockSpec((tm, d), lambda i: (i, 0))
    wide = pl.BlockSpec((tm, nblk * hs), lambda i: (i, 0))
    half = pl.BlockSpec((tm, ATTN_W), lambda i: (i, 0))
    return pl.pallas_call(
        body, grid=(s // tm,),
        in_specs=[row, row, row, pl.BlockSpec((None, 1, d), lambda i: (layer, 0, 0)), wide,
                  half, pl.BlockSpec((None, 1, ATTN_W), lambda i: (layer, 0, 0)),
                  _resident((None, mw, d), lambda i: (0, 0, 0)),
                  _resident((None, nblk, d, hs), lambda i: (0, 0, 0, 0)),
                  _resident((None, nblk * hs, d), lambda i: (0, 0, 0))],
        out_specs=[row, row, wide, half, half, half, pl.BlockSpec((8, d), lambda i: (0, 0)),
                   pl.BlockSpec((8, ATTN_W), lambda i: (0, 0))],
        out_shape=[jax.ShapeDtypeStruct((s, d), F32), jax.ShapeDtypeStruct((s, d), BF),
                   jax.ShapeDtypeStruct((s, nblk * hs), BF)] + [jax.ShapeDtypeStruct((s, ATTN_W), F32)] * 3
        + [jax.ShapeDtypeStruct((8, d), F32), jax.ShapeDtypeStruct((8, ATTN_W), F32)],
        compiler_params=_cparams(("arbitrary",)), name=f"mlp_bwd_l{layer}",
    )(dx, dxb, x, gain, u, o_attn, gain_attn, g_out, g_up, g_down)


def _attn_bwd(layer, proj, do, lse, delta, cos, sin, after):
    s = proj.shape[0]
    n_units = s // ATTN_UNIT
    out_rows = 256

    def unit_of(n):
        return n_units - 1 - n

    def body(q_ref, kp_ref, kc_ref, vp_ref, vc_ref, do_ref, lse_ref, delta_ref, cos_ref, sin_ref, after_ref, out_ref,
             dq_ref, dk_ref, dkp_ref, dv_ref, dvp_ref, carry_k, carry_v):
        step = pl.program_id(1)
        m_first, m_rest, is_a = _attn_masks(unit_of(step) > 0)
        sels = (is_a, jnp.logical_not(is_a))
        for ref in (dq_ref, dk_ref, dkp_ref, dv_ref, dvp_ref):
            ref[...] = jnp.zeros(ref.shape, F32)
        for dil in DILATIONS:
            last = ATTN_UNIT // (SPAN * dil) - 1
            for group in _attn_groups(dil):
                items = [(r, b, h) for r, b in group for h in range(2)]
                q = {rb: q_ref[_block_rows(dil, *rb), :] for rb in group}
                dov = {rb: do_ref[_block_rows(dil, *rb), :] for rb in group}
                lse_v = {rb: lse_ref[_block_rows(dil, *rb), :] for rb in group}
                delta_v = {rb: delta_ref[_block_rows(dil, *rb), :] for rb in group}
                k = {rb: _block_keys(kp_ref, kc_ref, dil, *rb).astype(BF) for rb in group}
                v = {rb: _block_keys(vp_ref, vc_ref, dil, *rb).astype(BF) for rb in group}
                qh = [jnp.where(sels[h], q[r, b], 0.0).astype(BF) for r, b, h in items]
                doh = [jnp.where(sels[h], dov[r, b], 0.0).astype(BF) for r, b, h in items]
                sc = [jnp.where(m_first if b == 0 else m_rest, _dot(qh[i], k[r, b], NT), MASK_VALUE)
                      for i, (r, b, h) in enumerate(items)]
                p = [jnp.exp(sc[i] - lse_v[r, b][:, 64 * h:64 * h + 1]) for i, (r, b, h) in enumerate(items)]
                ds = [(p[i] * (_dot(doh[i], v[r, b], NT) - delta_v[r, b][:, 64 * h:64 * h + 1])).astype(BF)
                      for i, (r, b, h) in enumerate(items)]
                dv = [_dot(p[i].astype(BF), doh[i], TN) for i in range(len(items))]
                dq = [_dot(ds[i], k[r, b], NN) for i, (r, b, h) in enumerate(items)]
                dk = [_dot(ds[i], qh[i], TN) for i in range(len(items))]
                for j, (r, b) in enumerate(group):
                    own = _block_rows(dil, r, b)
                    dq_ref[own, :] += jnp.where(is_a, dq[2 * j], dq[2 * j + 1])
                    dk2, dv2 = dk[2 * j] + dk[2 * j + 1], dv[2 * j] + dv[2 * j + 1]
                    dk_ref[own, :] += dk2[SPAN:]
                    dv_ref[own, :] += dv2[SPAN:]
                    if b > 0:
                        before = _block_rows(dil, r, b - 1)
                        dk_ref[before, :] += dk2[:SPAN]
                        dv_ref[before, :] += dv2[:SPAN]
                    else:
                        before = _block_rows(dil, r, last)
                        dkp_ref[before, :] += dk2[:SPAN]
                        dvp_ref[before, :] += dv2[:SPAN]
        has_next = step > 0
        for c in range(ATTN_UNIT // out_rows):
            rows = slice(c * out_rows, (c + 1) * out_rows)
            cs, sn = cos_ref[rows, :], sin_ref[rows, :]
            dqv = dq_ref[rows, :]
            dkv = dk_ref[rows, :] + jnp.where(has_next, carry_k[rows, :], 0.0)
            dvv = dv_ref[rows, :] + jnp.where(has_next, carry_v[rows, :], 0.0)
            out_ref[0, rows, :] = ((dqv * cs - _rope_partner(dqv) * sn) * ATTN_SCALE).astype(BF)
            out_ref[1, rows, :] = (dkv * cs - _rope_partner(dkv) * sn).astype(BF)
            out_ref[2, rows, :] = dvv.astype(BF)
        carry_k[...] = dkp_ref[...]
        carry_v[...] = dvp_ref[...]

    tab = pl.BlockSpec((ATTN_UNIT, LANES), lambda p, n: (unit_of(n), 0))
    return pl.pallas_call(
        body, grid=(ATTN_W // LANES, n_units), in_specs=_attn_specs(3, unit_of) + [tab, tab, ANY_SPEC],
        out_specs=pl.BlockSpec((3, ATTN_UNIT, LANES), lambda p, n: (0, unit_of(n), p)),
        out_shape=jax.ShapeDtypeStruct((3, s, ATTN_W), BF),
        scratch_shapes=[pltpu.VMEM((ATTN_UNIT, LANES), F32)] * 7,
        compiler_params=_cparams(("parallel", "arbitrary")), name=f"attn_bwd_l{layer}",
    )(proj, proj, proj, proj, proj, do, lse, delta, cos, sin, after)


def _hgrn_bwd(layer, proj, lb, gain, o, drec, states):
    s = proj.shape[0]
    nblk = s // HGRN_ROWS
    cpb = len(CHUNKS)

    def body(q_ref, f_ref, i_ref, g_ref, lb_ref, gain_ref, o_ref, drec_ref, st_ref, dseg_ref, dlb_ref, dgain_ref,
             dstate, dst_buf):
        step = pl.program_id(0)

        @pl.when(step == 0)
        def _():
            dstate[...] = jnp.zeros(dstate.shape, F32)

        lbv, gv = lb_ref[...], gain_ref[...]
        qh, z, gate_in = q_ref[...], f_ref[...], g_ref[...]
        pre = _hgrn_pre(qh, z, lbv)
        v = i_ref[...].astype(BF)
        sg = _sigmoid(gate_in)
        ov, drec = o_ref[...], drec_ref[...]
        dnormed = drec * (gate_in * sg)
        back = [_rms_bwd(_head(dnormed, h), _head(ov, h), gv) for h in HEADS]
        do_b = jnp.concatenate([b[0] for b in back], axis=1).astype(BF)
        dgain = back[0][1] + back[1][1] + back[2][1] + back[3][1]
        normed = jnp.concatenate([_rms_fwd(_head(ov, h), gv) for h in HEADS], axis=1)
        dgate_in = drec * normed * (sg * (1.0 + gate_in * (1.0 - sg)))
        mask = _hgrn_mask()
        qt, kt, qg, kout = (pre[n].astype(BF) for n in ("qt", "kt", "qg", "kout"))
        dec = jnp.exp(pre["g_last"])
        def intra(fn):
            return jnp.concatenate([jnp.concatenate([fn(h, sb) for sb in SUBS], axis=0) for h in HEADS], axis=1)

        def hs(x, h, sb):
            return _sub(_head(x, h), sb)

        a = [[jnp.where(mask, _dot(hs(qt, h, sb), hs(kt, h, sb), NT), 0.0).astype(BF) for sb in SUBS] for h in HEADS]
        da = [[jnp.where(mask, _dot(hs(do_b, h, sb), hs(v, h, sb), NT), 0.0).astype(BF) for sb in SUBS] for h in HEADS]
        dv_intra = intra(lambda h, sb: _dot(a[h][sb], hs(do_b, h, sb), TN))
        dqt = intra(lambda h, sb: _dot(da[h][sb], hs(kt, h, sb), NN))
        dkt = intra(lambda h, sb: _dot(da[h][sb], hs(qt, h, sb), TN))
        feed = [[_dot(_chunk(_head(do_b, h), c), _chunk(_head(qg, h), c), TN) for c in CHUNKS] for h in HEADS]
        for h in HEADS:
            dst = dstate[h]
            for c in reversed(CHUNKS):
                dst_buf[h, c * LANES:(c + 1) * LANES, :] = dst
                dst = dst * _head(dec, h)[c * HGRN_CHUNK:c * HGRN_CHUNK + 1, :] + feed[h][c]
            dstate[h] = dst

        def per_chunk(fn):
            cols = []
            for h in HEADS:
                rows = [jnp.broadcast_to(t, (HGRN_CHUNK, HGRN_DIM)) for t in (fn(h, c) for c in CHUNKS)]
                cols.append(jnp.concatenate(rows, axis=0))
            return jnp.concatenate(cols, axis=1)

        def st_prev(h, c):
            return st_ref[h, c * LANES:(c + 1) * LANES, :]

        def dst_at(h, c):
            return dst_buf[h, c * LANES:(c + 1) * LANES, :]

        dqg = per_chunk(lambda h, c: _dot(_chunk(_head(do_b, h), c), st_prev(h, c).astype(BF), NN))
        dkout = per_chunk(lambda h, c: _dot(_chunk(_head(v, h), c), dst_at(h, c).astype(BF), NN))
        dv_inter = per_chunk(lambda h, c: _dot(_chunk(_head(kout, h), c), dst_at(h, c).astype(BF), NT))
        dg_state = per_chunk(lambda h, c: jnp.sum(dst_at(h, c) * st_prev(h, c).astype(F32), axis=0, keepdims=True))
        dg_kout = per_chunk(lambda h, c: jnp.sum(_chunk(_head(dkout * pre["kout"], h), c), axis=0, keepdims=True))
        dv = dv_intra + dv_inter
        pos = lax.broadcasted_iota(jnp.int32, (HGRN_ROWS, HGRN_W), 0) % HGRN_CHUNK
        dq = dqt * pre["e_q"] + dqg * pre["e_in"]
        dk = dkt * pre["e_k"] + dkout * pre["e_out"]
        dg = (dqt * pre["qt"] - dkt * pre["kt"] + dqg * pre["qg"] - dkout * pre["kout"]
              + jnp.where(pos == HGRN_CHUNK - 1, dg_state * dec + dg_kout, 0.0))
        dlogf = _chunk_cumsum(dg, reverse=True)
        sig, sq = pre["sig"], pre["sq"]
        df = dlogf / pre["f"] - dk
        dseg_ref[0] = (dq * HGRN_SCALE * (sq * (1.0 + qh * (1.0 - sq)))).astype(BF)
        dseg_ref[1] = (df * (1.0 - lbv) * sig * (1.0 - sig)).astype(BF)
        dseg_ref[2] = dv.astype(BF)
        dseg_ref[3] = dgate_in.astype(BF)
        _accumulate_rows(step, dlb_ref, _part8(df * (1.0 - sig)))
        _accumulate_rows(step, dgain_ref, _part8(dgain))

    specs, blk = _hgrn_in_specs(layer, True, nblk)
    specs += [pl.BlockSpec((HGRN_ROWS, HGRN_W), lambda b: (blk(b), 0)),
              pl.BlockSpec((HGRN_ROWS, HGRN_W), lambda b: (blk(b), 0)),
              pl.BlockSpec((HGRN_HEADS, cpb * LANES, LANES), lambda b: (0, blk(b), 0))]
    return pl.pallas_call(
        body, grid=(nblk,), in_specs=specs,
        out_specs=[pl.BlockSpec((4, HGRN_ROWS, HGRN_W), lambda b: (0, blk(b), 0)),
                   pl.BlockSpec((8, HGRN_W), lambda b: (0, 0)), pl.BlockSpec((8, HGRN_DIM), lambda b: (0, 0))],
        out_shape=[jax.ShapeDtypeStruct((4, s, HGRN_W), BF), jax.ShapeDtypeStruct((8, HGRN_W), F32),
                   jax.ShapeDtypeStruct((8, HGRN_DIM), F32)],
        scratch_shapes=[pltpu.VMEM((HGRN_HEADS, LANES, LANES), F32), pltpu.VMEM((HGRN_HEADS, cpb * LANES, LANES), F32)],
        compiler_params=_cparams(("arbitrary",)), name=f"hgrn_bwd_l{layer}",
    )(proj, proj, proj, proj, lb, gain, o, drec, states)


def _bwd_inproj(layer, dqkv, dhg, g_in, x, gain, dres, tm=512):
    s, d = x.shape

    def body(dqkv_ref, dhg_ref, w_ref, x_ref, gain_ref, dres_ref, dx_ref, dxb_ref, dgain_ref):
        acc = jnp.zeros((tm, d), F32)
        for seg in range(N_SEG):
            a = dqkv_ref[seg] if seg < 3 else dhg_ref[seg - 3]
            acc = acc + _dot(a, w_ref[seg * SEG:(seg + 1) * SEG, :], NN)
        dx, dgain = _rms_bwd(acc, x_ref[...], gain_ref[...])
        out = dres_ref[...] + dx
        dx_ref[...] = out
        dxb_ref[...] = out.astype(BF)
        _accumulate_rows(pl.program_id(0), dgain_ref, _part8(dgain))

    row = pl.BlockSpec((tm, d), lambda i: (i, 0))
    return pl.pallas_call(
        body, grid=(s // tm,),
        in_specs=[pl.BlockSpec((3, tm, SEG), lambda i: (0, i, 0)), pl.BlockSpec((4, tm, SEG), lambda i: (0, i, 0)),
                  _resident((None, PROJ_W, d), lambda i: (0, 0, 0)), row,
                  pl.BlockSpec((None, 1, d), lambda i: (layer, 0, 0)), row],
        out_specs=[row, row, pl.BlockSpec((8, d), lambda i: (0, 0))],
        out_shape=[jax.ShapeDtypeStruct((s, d), F32), jax.ShapeDtypeStruct((s, d), BF), jax.ShapeDtypeStruct((8, d), F32)],
        compiler_params=_cparams(("arbitrary",)), name=f"bwd_inproj_l{layer}",
    )(dqkv, dhg, g_in, x, gain, dres)


def _adamw(w, g, m, v):
    m2 = ADAM_B1 * m + (1.0 - ADAM_B1) * g
    v2 = ADAM_B2 * v + (1.0 - ADAM_B2) * (g * g)
    m_hat = m2 / (1.0 - ADAM_B1 ** ADAM_STEP)
    v_hat = v2 / (1.0 - ADAM_B2 ** ADAM_STEP)
    delta = -ADAM_LR * (m_hat / (jnp.sqrt(v_hat) + ADAM_EPS) + ADAM_WD * w)
    return delta, m2, v2


def _adam_big(name, parts, w, m, v, row_tiles):
    depth = w.shape[0]
    r, c = parts[0].shape[1], parts[0].shape[2]
    tr = r // row_tiles
    p_spec = pl.BlockSpec((N_DEV, tr, c), lambda t: (0, t, 0))
    w_spec = pl.BlockSpec((depth, tr, c), lambda t: (0, t, 0))

    def body(*refs):
        p_refs = refs[:depth]
        w_ref, m_ref, v_ref, g_ref, d_ref, m2_ref, v2_ref, token = refs[depth:]
        token[...] = jnp.zeros(token.shape, F32)
        for l in range(depth):
            g = p_refs[l][0].astype(F32)
            for dev in range(1, N_DEV):
                g = g + p_refs[l][dev].astype(F32)
            delta, m2, v2 = _adamw(w_ref[l], g, m_ref[l], v_ref[l])
            g_ref[l] = g
            d_ref[l] = delta
            m2_ref[l] = m2
            v2_ref[l] = v2

    return pl.pallas_call(
        body, grid=(row_tiles,), in_specs=[p_spec] * depth + [w_spec] * 3,
        out_specs=[w_spec] * 4 + [pl.BlockSpec((8, LANES), lambda t: (0, 0))],
        out_shape=[jax.ShapeDtypeStruct(w.shape, F32)] * 4 + [jax.ShapeDtypeStruct((8, LANES), F32)],
        compiler_params=_cparams(("arbitrary",)), name=name,
    )(*parts, w, m, v)


def _adam_small(g, ws, ms, vs):
    n = len(ws)

    def split(row, width):
        return jnp.concatenate([row[:, :width], row[:, width:2 * width]], axis=0)

    def body(g_ref, *refs):
        ins, outs = refs[:3 * n], refs[3 * n:]
        grads = [g_ref[0:2, :], split(g_ref[5:6, :], ATTN_W), split(g_ref[6:7, :], HGRN_W), split(g_ref[7:8, :], HGRN_DIM),
                 g_ref[2:4, :], g_ref[4:5, :]]
        for i, g_i in enumerate(grads):
            delta, m2, v2 = _adamw(ins[i][...], g_i, ins[n + i][...], ins[2 * n + i][...])
            for j, val in enumerate((g_i, delta, m2, v2)):
                outs[4 * i + j][...] = val

    vm = pl.BlockSpec(memory_space=pltpu.VMEM)
    res = pl.pallas_call(
        body, in_specs=[vm] * (1 + 3 * n), out_specs=[vm] * (4 * n),
        out_shape=[jax.ShapeDtypeStruct(w.shape, F32) for w in ws for _ in range(4)], name="adam_small",
    )(g, *ws, *ms, *vs)
    return [res[4 * i:4 * i + 4] for i in range(n)]


def _lower_bounds(logits):
    def body(l_ref, lb_ref, jac_ref):
        l0, l1 = l_ref[0:1, :], l_ref[1:2, :]
        mx = jnp.maximum(l0, l1)
        e0, e1 = jnp.exp(l0 - mx), jnp.exp(l1 - mx)
        p0, p1 = e0 / (e0 + e1), e1 / (e0 + e1)
        lb_ref[0:1, :] = p0 - p0
        lb_ref[1:2, :] = (p0 + p1) - p0
        jac_ref[0:1, :] = -p0 * p1
        jac_ref[1:2, :] = p0 * p1

    vm = pl.BlockSpec(memory_space=pltpu.VMEM)
    return pl.pallas_call(body, in_specs=[vm], out_specs=[vm, vm], out_shape=[jax.ShapeDtypeStruct(logits.shape, F32)] * 2,
                          name="hgrn_lower_bounds")(logits)


def _rope_tables(s, after):
    half = 32
    inv_freq = ROPE_THETA ** (-jnp.arange(half, dtype=F32) / half)
    ang = (jnp.arange(s, dtype=jnp.int32).astype(F32) + after[0, 0])[:, None] * inv_freq[None, :]
    cos, sin = jnp.cos(ang), jnp.sin(ang)
    return jnp.concatenate([cos] * 4, axis=1), jnp.concatenate([-sin, sin, -sin, sin], axis=1)


def kernel(x, norm_mix, w_in, attn_out_gain, hgrn_lb_logits, hgrn_out_gain, w_out, norm_mlp, w_up, w_down, norm_final, loss_target, m_norm_mix, m_w_in, m_attn_out_gain, m_hgrn_lb_logits, m_hgrn_out_gain, m_w_out, m_norm_mlp, m_w_up, m_w_down, m_norm_final, v_norm_mix, v_w_in, v_attn_out_gain, v_hgrn_lb_logits, v_hgrn_out_gain, v_w_out, v_norm_mlp, v_w_up, v_w_down, v_norm_final):
    depth = w_in.shape[0]
    assert depth == 2 and x.shape[0] == 1
    s, d = x.shape[1], x.shape[2]
    x0 = x[0]
    target = loss_target[0]
    g_mix, g_attn, g_hg, g_mlp = (norm_mix[:, None, :], attn_out_gain[:, None, :], hgrn_out_gain[:, None, :],
                                  norm_mlp[:, None, :])
    lb, lb_jac = _lower_bounds(hgrn_lb_logits)
    lb3 = lb[:, None, :]

    def flip(a):
        return jnp.swapaxes(a, 1, 2)

    shards = list(_pack_weights(flip(w_in), w_out, w_up, w_down))
    w_pieces = _weight_pieces(*shards)
    w_groups = [[0], [1, 2, 3], [4], [5, 6, 7]]
    me = (4 * lax.axis_index("x") + 2 * lax.axis_index("y") + lax.axis_index("c")).astype(jnp.int32).reshape(1)
    lands = _exchange_own("all_gather_own", me, shards, w_pieces)
    w_sems, shards, lands, token = _exchange_start("all_gather_start", shards, lands, w_pieces, w_groups)

    def weights_ready(group, after):
        nonlocal shards
        idxs = w_groups[group]
        shards, got = _exchange_wait(f"all_gather_wait{group}", shards, [lands[i] for i in idxs], w_pieces,
                                     [(idxs, *w_sems[group])], after)
        return got

    cos, sin = _rope_tables(s, token)

    def tied(small_arr, tok):
        return small_arr + tok[0, 0]

    saved = []
    xl = x0
    full = [None] * depth
    for l in range(depth):
        (full_in,) = weights_ready(2 * l, cos if l == 0 else xl)
        saved_x = xl
        proj, h = _fwd_inproj(l, xl, g_mix, full_in, cos, sin)
        o_attn, lse = _attn_fwd(l, proj)
        o_hg, mixed, states = _hgrn_fwd(l, proj, lb3, g_hg)
        mixed = _attn_norm(l, o_attn, g_attn, mixed)
        full_out, full_up, full_down = weights_ready(2 * l + 1, mixed)
        head = (norm_final[None, :], target) if l == depth - 1 else None
        xl, x_mid, u, h2, *loss_side = _mlp_fwd(l, xl, mixed, g_mlp, full_out, full_up, full_down, head)
        saved.append((saved_x, proj, h, o_attn, lse, o_hg, states, mixed, x_mid, u, h2))
        full[l] = (full_in, full_out, full_up, full_down)
    dx, (dxb, dnorm_final8, loss8) = xl, loss_side

    exchanges = []

    def scatter(tag, grads, kinds):
        pieces = _grad_pieces(grads, kinds)
        own = _exchange_own(f"reduce_scatter_own_{tag}", me, grads, pieces)
        sems, grads, own, tok = _exchange_start(f"reduce_scatter_start_{tag}", grads, own, pieces, [list(range(len(pieces)))])
        exchanges.append((grads, own, pieces, sems[0]))
        return tok

    small = {}
    for l in reversed(range(depth)):
        xl, proj, h, o_attn, lse, o_hg, states, mixed, x_mid, u, h2 = saved[l]
        full_in, full_out, full_up, full_down = full[l]
        hs = full_up.shape[3]
        gw_down = _mm_tn(f"grad_w_down_l{l}", u, dxb, u.shape[1], a_fn=_relu2)
        dx_mid, dx_mid_b, du, drec, do, delta, dmlp8, dattn8 = _mlp_bwd(l, dx, dxb, x_mid, g_mlp, u, o_attn, g_attn,
                                                                         full_out, full_up, full_down)
        gw_up = _mm_tn(f"grad_w_up_l{l}", h2, du, d, out_block_w=hs)
        gw_out = _mm_tn(f"grad_w_out_l{l}", mixed, dx_mid_b, mixed.shape[1])
        started = scatter(f"mlp_l{l}", [gw_down, gw_up, gw_out], ["rows", "up", "rows"])
        dqkv = _attn_bwd(l, proj, do, lse, delta, cos, sin, started)
        dhg, dlb8, dhgain8 = _hgrn_bwd(l, proj, lb3, g_hg, o_hg, drec, states)
        gin = _mm_tn(f"grad_w_in_qkv_l{l}", dqkv, h, PROJ_W, a_lead=True)
        gw_in = _mm_tn(f"grad_w_in_hg_l{l}", dhg, h, PROJ_W, a_lead=True, out_block_off=3, prev=gin)
        g_mix_t = tied(g_mix, scatter(f"mix_l{l}", [gw_in], ["rows"]))
        dx, dxb, dmix8 = _bwd_inproj(l, dqkv, dhg, full_in, xl, g_mix_t, dx_mid)
        small[l] = (dmix8, dattn8, dlb8, dhgain8, dmlp8)

    def scattered(name, which, after):
        grads, lands, pieces, waits = [], [], [], []
        for grads_e, own, pieces_e, (send, recv) in (exchanges[i] for i in which):
            first = len(pieces)
            pieces += [p._replace(src=p.src + len(grads)) for p in pieces_e]
            waits.append((list(range(first, first + len(pieces_e))), send, recv))
            grads += grads_e
            lands += own
        return _exchange_wait(name, grads, lands, pieces, waits, after)[1]

    down1, up1, out1, in1, down0, up0, out0 = scattered("reduce_scatter_wait_early", (0, 1, 2), dx)
    big = {
        "w_down": _adam_big("adam_w_down", [down0, down1], w_down, m_w_down, v_w_down, 4),
        "w_up": _adam_big("adam_w_up", [up0, up1], w_up, m_w_up, v_w_up, 2),
        "w_out": _adam_big("adam_w_out", [out0, out1], w_out, m_w_out, v_w_out, 1),
    }
    g_small = _all_reduce_small([small[l][0] for l in range(depth)], [small[l][4] for l in range(depth)], dnorm_final8,
                                [small[l][1] for l in range(depth)], small[depth - 1][2], lb_jac,
                                [small[l][3] for l in range(depth)], loss8, big["w_out"][4])
    loss = g_small[7, 2 * HGRN_DIM]
    row = lambda a: a[None, :]
    small_out = _adam_small(
        g_small, [norm_mix, attn_out_gain, hgrn_lb_logits, hgrn_out_gain, norm_mlp, row(norm_final)],
        [m_norm_mix, m_attn_out_gain, m_hgrn_lb_logits, m_hgrn_out_gain, m_norm_mlp, row(m_norm_final)],
        [v_norm_mix, v_attn_out_gain, v_hgrn_lb_logits, v_hgrn_out_gain, v_norm_mlp, row(v_norm_final)])
    small_out[5] = [t[0] for t in small_out[5]]
    (in0,) = scattered("reduce_scatter_wait_last", (3,), small_out[0][1])
    big["w_in"] = [flip(t) for t in _adam_big("adam_w_in", [in0, in1], flip(w_in), flip(m_w_in), flip(v_w_in), 2)[:4]]

    def gather(idx):
        mix, attn, lbl, hg, mlp, final = (t[idx] for t in small_out)
        return [mix, big["w_in"][idx], attn, lbl, hg, big["w_out"][idx], mlp, big["w_up"][idx], big["w_down"][idx], final]

    return (loss, dx[None], *gather(0), *gather(1), *gather(2), *gather(3))
```

```python
import functools
from typing import Callable, NamedTuple

import jax
import jax.numpy as jnp
from jax import lax
from jax.experimental import pallas as pl
from jax.experimental.pallas import tpu as pltpu

F32 = jnp.float32
BF = jnp.bfloat16

N_DEV = 8
ATTN_W = 512
HGRN_W = 512
HGRN_HEADS = 4
HGRN_DIM = 128
SEG = 512
N_SEG = 7
PROJ_W = N_SEG * SEG
MIX_W = ATTN_W + HGRN_W
SPAN = 128
DILATIONS = (1, 4, 16)
HGRN_CHUNK = 16
ROPE_THETA = 10000.0
NORM_EPS = 1e-6
MASK_VALUE = -1e30
ATTN_SCALE = 0.125
HGRN_SCALE = HGRN_DIM ** -0.5
ADAM_LR = 0.001
ADAM_B1 = 0.9
ADAM_B2 = 0.999
ADAM_EPS = 1e-08
ADAM_WD = 0.01
ADAM_STEP = 10
LANES = 128
VMEM_LIMIT = 56 * 1024 * 1024

NN = ((1,), (0,))
NT = ((1,), (1,))
TN = ((0,), (0,))
MESH = pl.DeviceIdType.MESH


def _dot(a, b, dims):
    return lax.dot_general(a, b, (dims, ((), ())), preferred_element_type=F32)


def _cparams(sem):
    return pltpu.CompilerParams(dimension_semantics=sem, vmem_limit_bytes=VMEM_LIMIT)


def _part8(x):
    r, n = x.shape
    return jnp.sum(x.reshape(r // 8, 8, n), axis=0)


def _sigmoid(x):
    return 1.0 / (1.0 + jnp.exp(-x))


def _rms_fwd(x, gain):
    r = lax.rsqrt(jnp.mean(x * x, axis=-1, keepdims=True) + NORM_EPS)
    return x * r * gain


def _rms_bwd(dy, x, gain):
    r = lax.rsqrt(jnp.mean(x * x, axis=-1, keepdims=True) + NORM_EPS)
    xn = x * r
    dxn = dy * gain
    dx = r * (dxn - xn * jnp.mean(dxn * xn, axis=-1, keepdims=True))
    return dx, dy * xn


def _rope_partner(x):
    n = x.shape[-1]
    lane = lax.broadcasted_iota(jnp.int32, x.shape, x.ndim - 1)
    return jnp.where((lane % 64) < 32, pltpu.roll(x, n - 32, x.ndim - 1), pltpu.roll(x, 32, x.ndim - 1))


def _tile_lanes(t, reps):
    return jnp.concatenate([t] * reps, axis=-1)


def _mm_tn(name, a, b, out_rows, a_lead=False, out_block_off=0, prev=None, out_block_w=None, a_fn=None,
           tm=512, tn=1024, sub=512):
    kdim, n = b.shape
    m = a.shape[-1]
    tm, tn, sub = min(tm, m), min(tn, n), min(sub, kdim)
    mt = m // tm
    n_lead = a.shape[0] if a_lead else 1
    if a_lead:
        a_spec = pl.BlockSpec((None, kdim, tm), lambda j, i: (i // mt, 0, i % mt))
    else:
        a_spec = pl.BlockSpec((kdim, tm), lambda j, i: (0, i))
    b_spec = pl.BlockSpec((kdim, tn), lambda j, i: (0, j))
    if out_block_w:
        nb = tn // out_block_w
        o_shape = jax.ShapeDtypeStruct((n // out_block_w, out_rows, out_block_w), BF)
        o_spec = pl.BlockSpec((nb, tm, out_block_w), lambda j, i: (j, i + out_block_off, 0))
    else:
        nb = 0
        o_shape = jax.ShapeDtypeStruct((out_rows, n), BF)
        o_spec = pl.BlockSpec((tm, tn), lambda j, i: (i + out_block_off, j))
    arrays, specs, aliases = [a, b], [a_spec, b_spec], {}
    if prev is not None:
        arrays.append(prev)
        specs.append(pl.BlockSpec(memory_space=pl.ANY))
        aliases = {2: 0}

    def body(*refs):
        a_ref, b_ref, o_ref = refs[0], refs[1], refs[-1]
        acc = None
        for k in range(kdim // sub):
            av = a_ref[k * sub:(k + 1) * sub, :]
            if a_fn is not None:
                av = a_fn(av)
            part = _dot(av, b_ref[k * sub:(k + 1) * sub, :], TN)
            acc = part if acc is None else acc + part
        if nb:
            for t in range(nb):
                o_ref[t] = acc[:, t * out_block_w:(t + 1) * out_block_w].astype(BF)
        else:
            o_ref[...] = acc.astype(BF)

    return pl.pallas_call(
        body, grid=(n // tn, n_lead * mt), in_specs=specs, out_specs=o_spec, out_shape=o_shape,
        compiler_params=_cparams(("parallel", "parallel")), name=name, input_output_aliases=aliases,
    )(*arrays)


def _pack_weights(w_in_t, w_out, w_up, w_down):
    depth = w_in_t.shape[0]
    arrays = (w_in_t, w_out, w_up, w_down)

    def body(*refs):
        for src, dst in zip(refs[:4], refs[4:]):
            dst[...] = src[...].astype(BF)

    specs = [pl.BlockSpec((None,) + a.shape[1:], lambda l: (l, 0, 0)) for a in arrays]
    return pl.pallas_call(
        body, grid=(depth,), in_specs=specs, out_specs=specs,
        out_shape=[jax.ShapeDtypeStruct(a.shape, BF) for a in arrays],
        compiler_params=_cparams(("arbitrary",)), name="pack_weights",
    )(*arrays)


def _my_position():
    x, y, c = lax.axis_index("x"), lax.axis_index("y"), lax.axis_index("c")
    return x, y, c, 4 * x + 2 * y + c


def _peer(x, y, c, k):
    px = 1 - x if k & 4 else x
    py = 1 - y if k & 2 else y
    pc = 1 - c if k & 1 else c
    return (px, py, pc), 4 * px + 2 * py + pc


PEER_ORDER = (1, 2, 4, 3, 5, 6, 7)


class _Piece(NamedTuple):
    src: int
    send: Callable
    slot: Callable
    land_shape: tuple
    own_src: tuple
    own_slot: tuple


HBM_SPEC = pl.BlockSpec(memory_space=pltpu.HBM)
SEM_SPEC = pl.BlockSpec(memory_space=pltpu.SEMAPHORE)
ANY_SPEC = pl.BlockSpec(memory_space=pl.ANY)


def _in_hbm(arrays):
    return [pltpu.with_memory_space_constraint(a, pltpu.HBM) for a in arrays]


def _hbm_like(arrays):
    return [pltpu.HBM(a.shape, a.dtype) for a in arrays]


def _rows_of(rows):
    return lambda ref, dev: ref.at[pl.ds(pl.multiple_of(dev * rows, 16), rows), :]


def _exchange_own(name, me, srcs, pieces):
    n = len(pieces)

    def body(me_ref, *refs):
        for i in range(n):
            refs[n + i][...] = refs[i][...]

    def spec(block_and_index):
        block, index = block_and_index
        return pl.BlockSpec(block, lambda i, me_ref: index(me_ref[0]))

    return pl.pallas_call(
        body,
        grid_spec=pltpu.PrefetchScalarGridSpec(
            num_scalar_prefetch=1, grid=(1,), in_specs=[spec(p.own_src) for p in pieces],
            out_specs=[spec(p.own_slot) for p in pieces]),
        out_shape=[jax.ShapeDtypeStruct(p.land_shape, BF) for p in pieces],
        compiler_params=_cparams(("arbitrary",)), name=name,
    )(me, *[srcs[p.src] for p in pieces])


def _exchange_start(name, srcs, lands, pieces, groups):
    n_src, n, n_g = len(srcs), len(pieces), len(groups)

    def body(*refs):
        src_refs, land_refs = refs[:n_src], refs[n_src:n_src + n]
        sems, token = refs[n_src + n:n_src + n + 2 * n_g], refs[-1]
        x, y, c, me = _my_position()
        for g, idxs in enumerate(groups):
            for k in PEER_ORDER:
                peer, pid = _peer(x, y, c, k)
                for j, i in enumerate(idxs):
                    p = pieces[i]
                    pltpu.make_async_remote_copy(
                        src_ref=p.send(src_refs[p.src], pid), dst_ref=p.slot(land_refs[i], me),
                        send_sem=sems[2 * g].at[(k - 1) * len(idxs) + j], recv_sem=sems[2 * g + 1].at[(k - 1) * len(idxs) + j],
                        device_id=peer, device_id_type=MESH).start()
        token[...] = jnp.zeros(token.shape, F32)

    sem_shapes = [pltpu.SemaphoreType.DMA(((N_DEV - 1) * len(idxs),)) for idxs in groups for _ in range(2)]
    res = pl.pallas_call(
        body, in_specs=[HBM_SPEC] * (n_src + n),
        out_specs=[SEM_SPEC] * (2 * n_g) + [HBM_SPEC] * (n_src + n) + [pl.BlockSpec(memory_space=pltpu.VMEM)],
        out_shape=sem_shapes + _hbm_like(srcs) + _hbm_like(lands) + [jax.ShapeDtypeStruct((8, LANES), F32)],
        input_output_aliases={i: 2 * n_g + i for i in range(n_src + n)},
        compiler_params=pltpu.CompilerParams(has_side_effects=pltpu.SideEffectType.DATAFLOW_SIDE_EFFECTING),
        name=name,
    )(*_in_hbm(srcs), *_in_hbm(lands))
    sems = [(res[2 * g], res[2 * g + 1]) for g in range(n_g)]
    return sems, list(res[2 * n_g:2 * n_g + n_src]), list(res[2 * n_g + n_src:2 * n_g + n_src + n]), res[-1]


def _exchange_wait(name, srcs, lands, pieces, waits, after):
    n_src, n, n_g = len(srcs), len(lands), len(waits)

    def body(*refs):
        src_refs, land_refs = refs[:n_src], refs[n_src:n_src + n]
        sems = refs[n_src + n:n_src + n + 2 * n_g]
        x, y, c, me = _my_position()
        at = 0
        for g, (idxs, _, _) in enumerate(waits):
            for k in PEER_ORDER:
                peer, pid = _peer(x, y, c, k)
                for j, i in enumerate(idxs):
                    p = pieces[i]
                    cp = pltpu.make_async_remote_copy(
                        src_ref=p.send(src_refs[p.src], pid), dst_ref=p.slot(land_refs[at + j], pid),
                        send_sem=sems[2 * g].at[(k - 1) * len(idxs) + j], recv_sem=sems[2 * g + 1].at[(k - 1) * len(idxs) + j],
                        device_id=peer, device_id_type=MESH)
                    cp.wait_send()
                    cp.wait_recv()
            at += len(idxs)

    sem_args = [s for _, send, recv in waits for s in (send, recv)]
    res = pl.pallas_call(
        body, in_specs=[HBM_SPEC] * (n_src + n) + [SEM_SPEC] * (2 * n_g) + [ANY_SPEC],
        out_specs=[HBM_SPEC] * (n_src + n), out_shape=_hbm_like(srcs) + _hbm_like(lands),
        input_output_aliases={i: i for i in range(n_src + n)},
        compiler_params=pltpu.CompilerParams(has_side_effects=pltpu.SideEffectType.DATAFLOW_SIDE_EFFECTING),
        name=name,
    )(*srcs, *lands, *sem_args, after)
    return list(res[:n_src]), list(res[n_src:])


def _weight_pieces(p_in, p_out, p_up, p_down):
    depth, cin, d = p_in.shape
    rout, hs = p_out.shape[1], p_up.shape[2]
    pieces = []
    for l in range(depth):
        whole = functools.partial(lambda ref, dev, l: ref.at[l], l=l)
        layer = functools.partial(lambda dev, l: (l, 0, 0), l=l)

        def rows(src, n_rows, whole=whole, layer=layer):
            return _Piece(src, whole, lambda ref, dev: _rows_of(n_rows)(ref.at[0], dev), (1, N_DEV * n_rows, d),
                          ((None, n_rows, d), layer), ((None, n_rows, d), lambda dev: (0, dev, 0)))

        pieces += [
            rows(0, cin), rows(1, rout),
            _Piece(2, whole, lambda ref, dev: ref.at[0, dev], (1, N_DEV, d, hs),
                   ((None, d, hs), layer), ((None, None, d, hs), lambda dev: (0, dev, 0, 0))),
            rows(3, hs),
        ]
    return pieces


def _grad_pieces(g_pair, kinds):
    pieces = []
    for i, (g, kind) in enumerate(zip(g_pair, kinds)):
        lead = lambda dev: (dev, 0, 0)
        if kind == "up":
            blk = ((None,) + g.shape[1:], lead)
            pieces.append(_Piece(i, lambda ref, dev: ref.at[dev], lambda ref, dev: ref.at[dev], g.shape, blk, blk))
        else:
            rows, cols = g.shape[0] // N_DEV, g.shape[1]
            pieces.append(_Piece(i, _rows_of(rows), lambda ref, dev: ref.at[dev], (N_DEV, rows, cols),
                                 ((rows, cols), lambda dev: (dev, 0)), ((None, rows, cols), lead)))
    return pieces


SMALL_W = 1024


def _all_reduce_small(mix8, mlp8, final8, attn8, lb8_last, lb_jac, hg8, loss8, after):
    def body(mix0, mix1, mlp0, mlp1, fin, attn0, attn1, lb, jac, hg0, hg1, loss, after_ref, o_ref, src_ref, buf_ref,
             send_sems, recv_sems):
        def total(ref):
            return jnp.sum(ref[...], axis=0, keepdims=True)

        dlb = total(lb)
        hg = jnp.concatenate([total(hg0), total(hg1), total(loss)], axis=1)
        src_ref[...] = jnp.concatenate([
            total(mix0), total(mix1), total(mlp0), total(mlp1), total(fin),
            jnp.concatenate([total(attn0), total(attn1)], axis=1),
            jnp.concatenate([jac[0:1, :] * dlb, jac[1:2, :] * dlb], axis=1),
            jnp.concatenate([hg, jnp.zeros((1, SMALL_W - hg.shape[1]), F32)], axis=1)], axis=0)
        x, y, c, me = _my_position()
        buf_ref[me] = src_ref[...]
        sends = []
        for k in PEER_ORDER:
            peer, _ = _peer(x, y, c, k)
            cp = pltpu.make_async_remote_copy(src_ref=src_ref, dst_ref=buf_ref.at[me], send_sem=send_sems.at[k - 1],
                                              recv_sem=recv_sems.at[k - 1], device_id=peer, device_id_type=MESH)
            cp.start()
            sends.append(cp)
        for k in PEER_ORDER:
            peer, pid = _peer(x, y, c, k)
            pltpu.make_async_remote_copy(src_ref=src_ref, dst_ref=buf_ref.at[pid], send_sem=send_sems.at[k - 1],
                                         recv_sem=recv_sems.at[k - 1], device_id=peer, device_id_type=MESH).wait_recv()
        for cp in sends:
            cp.wait_send()
        acc = buf_ref[0]
        for dev in range(1, N_DEV):
            acc = acc + buf_ref[dev]
        o_ref[...] = acc

    assert mix8[0].shape[1] == SMALL_W
    vm = pl.BlockSpec(memory_space=pltpu.VMEM)
    return pl.pallas_call(
        body, in_specs=[vm] * 12 + [ANY_SPEC], out_specs=vm, out_shape=jax.ShapeDtypeStruct((8, SMALL_W), F32),
        scratch_shapes=[pltpu.VMEM((8, SMALL_W), F32), pltpu.VMEM((N_DEV, 8, SMALL_W), F32),
                        pltpu.SemaphoreType.DMA((N_DEV - 1,)), pltpu.SemaphoreType.DMA((N_DEV - 1,))],
        name="all_reduce_small",
    )(*mix8, *mlp8, final8, *attn8, lb8_last, lb_jac, *hg8, loss8, after)


def _resident(block_shape, index_map):
    return pl.BlockSpec(block_shape, index_map, pipeline_mode=pl.Buffered(1))


def _fwd_inproj(layer, x, gain, g_in, cos, sin, tm=512):
    s, d = x.shape

    def body(x_ref, gain_ref, w_ref, cos_ref, sin_ref, proj_ref, h_ref):
        h = _rms_fwd(x_ref[...], gain_ref[...]).astype(BF)
        h_ref[...] = h
        cs = _tile_lanes(cos_ref[...], SEG // LANES)
        sn = _tile_lanes(sin_ref[...], SEG // LANES)
        for seg in range(N_SEG):
            acc = _dot(h, w_ref[seg * SEG:(seg + 1) * SEG, :], NT)
            if seg < 2:
                acc = acc * cs + _rope_partner(acc) * sn
            if seg == 0:
                acc = acc * ATTN_SCALE
            proj_ref[:, seg * SEG:(seg + 1) * SEG] = acc

    return pl.pallas_call(
        body, grid=(s // tm,),
        in_specs=[pl.BlockSpec((tm, d), lambda i: (i, 0)), pl.BlockSpec((None, 1, d), lambda i: (layer, 0, 0)),
                  _resident((None, PROJ_W, d), lambda i: (0, 0, 0)),
                  pl.BlockSpec((tm, LANES), lambda i: (i, 0)), pl.BlockSpec((tm, LANES), lambda i: (i, 0))],
        out_specs=[pl.BlockSpec((tm, PROJ_W), lambda i: (i, 0)), pl.BlockSpec((tm, d), lambda i: (i, 0))],
        out_shape=[jax.ShapeDtypeStruct((s, PROJ_W), F32), jax.ShapeDtypeStruct((s, d), BF)],
        compiler_params=_cparams(("parallel",)), name=f"fwd_inproj_l{layer}",
    )(x, gain, g_in, cos, sin)


ATTN_UNIT = SPAN * max(DILATIONS)
ATTN_GROUP = 4


def _attn_masks(first_block_has_prev):
    row = lax.broadcasted_iota(jnp.int32, (SPAN, 2 * SPAN), 0)
    col = lax.broadcasted_iota(jnp.int32, (SPAN, 2 * SPAN), 1)
    band = (col >= row) & (col <= row + SPAN)
    lane = lax.broadcasted_iota(jnp.int32, (SPAN, LANES), 1)
    return band & ((col >= SPAN) | first_block_has_prev), band, lane < 64


def _attn_specs(n_in_extra, unit_of=lambda n: n):
    pairs = ATTN_W // LANES
    q_spec = pl.BlockSpec((ATTN_UNIT, LANES), lambda p, n: (unit_of(n), p))

    def prev(seg):
        return pl.BlockSpec((ATTN_UNIT, LANES), lambda p, n: (jnp.maximum(unit_of(n) - 1, 0), seg * pairs + p))

    def cur(seg):
        return pl.BlockSpec((ATTN_UNIT, LANES), lambda p, n: (unit_of(n), seg * pairs + p))

    return [q_spec, prev(1), cur(1), prev(2), cur(2)] + [q_spec] * n_in_extra


def _attn_groups(dil):
    blocks = ATTN_UNIT // (SPAN * dil)
    pairs = [(r, b) for r in range(dil) for b in range(blocks)]
    return [pairs[i:i + ATTN_GROUP] for i in range(0, len(pairs), ATTN_GROUP)]


def _block_rows(dil, r, b, n=1):
    start = r + dil * SPAN * b
    return pl.ds(start, n * SPAN, stride=dil) if dil > 1 else pl.ds(start, n * SPAN)


def _block_keys(prev_ref, cur_ref, dil, r, b):
    if b > 0:
        return cur_ref[_block_rows(dil, r, b - 1, 2), :]
    last = ATTN_UNIT // (SPAN * dil) - 1
    return jnp.concatenate([prev_ref[_block_rows(dil, r, last), :], cur_ref[_block_rows(dil, r, 0), :]], axis=0)


def _attn_fwd(layer, proj):
    s = proj.shape[0]
    n_pat = len(DILATIONS)
    merge_rows = 256

    def body(q_ref, kp_ref, kc_ref, vp_ref, vc_ref, o_ref, lse_ref, o_scr, lse_scr):
        m_first, m_rest, is_a = _attn_masks(pl.program_id(1) > 0)
        sels = (is_a, jnp.logical_not(is_a))
        is_a_keys = lax.broadcasted_iota(jnp.int32, (2 * SPAN, LANES), 1) < 64
        for pi, dil in enumerate(DILATIONS):
            for group in _attn_groups(dil):
                items = [(r, b, h) for r, b in group for h in range(2)]
                q = {rb: q_ref[_block_rows(dil, *rb), :] for rb in group}
                k = {rb: _block_keys(kp_ref, kc_ref, dil, *rb).astype(BF) for rb in group}
                v = {rb: _block_keys(vp_ref, vc_ref, dil, *rb).astype(BF) for rb in group}
                v_sum = {rb: (jnp.where(is_a_keys, v[rb], 1.0), jnp.where(is_a_keys, 1.0, v[rb])) for rb in group}
                sc = [jnp.where(m_first if b == 0 else m_rest,
                                _dot(jnp.where(sels[h], q[r, b], 0.0).astype(BF), k[r, b], NT), MASK_VALUE)
                      for r, b, h in items]
                mx = [jnp.max(jnp.maximum(t[:, :SPAN], t[:, SPAN:]), axis=-1, keepdims=True) for t in sc]
                p = [jnp.exp(t - m).astype(BF) for t, m in zip(sc, mx)]
                both = [_dot(t, v_sum[r, b][h], NN) for t, (r, b, h) in zip(p, items)]
                for j, (r, b) in enumerate(group):
                    t_a, t_b = both[2 * j], both[2 * j + 1]
                    den = pltpu.roll(jnp.where(is_a, t_b, t_a), 64, 1)
                    o_scr[pi, _block_rows(dil, r, b), :] = jnp.where(is_a, t_a, t_b) / den
                    lse_scr[pi, _block_rows(dil, r, b), :] = jnp.where(is_a, mx[2 * j], mx[2 * j + 1]) + jnp.log(den)
        for c in range(ATTN_UNIT // merge_rows):
            rows = slice(c * merge_rows, (c + 1) * merge_rows)
            ls = [lse_scr[pi, rows, :] for pi in range(n_pat)]
            mx = functools.reduce(jnp.maximum, ls)
            ws = [jnp.exp(l - mx) for l in ls]
            den = functools.reduce(jnp.add, ws)
            o_ref[rows, :] = functools.reduce(jnp.add, [w * o_scr[pi, rows, :] for pi, w in enumerate(ws)]) / den
            lse_ref[rows, :] = mx + jnp.log(den)

    out_spec = pl.BlockSpec((ATTN_UNIT, LANES), lambda p, n: (n, p))
    return pl.pallas_call(
        body, grid=(ATTN_W // LANES, s // ATTN_UNIT), in_specs=_attn_specs(0), out_specs=[out_spec, out_spec],
        out_shape=[jax.ShapeDtypeStruct((s, ATTN_W), F32)] * 2,
        scratch_shapes=[pltpu.VMEM((n_pat, ATTN_UNIT, LANES), F32)] * 2,
        compiler_params=_cparams(("parallel", "arbitrary")), name=f"attn_fwd_l{layer}",
    )(proj, proj, proj, proj, proj)


def _chunk_cumsum(x, reverse=False):
    n = x.shape[0]
    pos = lax.broadcasted_iota(jnp.int32, x.shape, 0) % HGRN_CHUNK
    for sh in (1, 2, 4, 8):
        if reverse:
            x = x + jnp.where(pos < HGRN_CHUNK - sh, pltpu.roll(x, n - sh, 0), 0.0)
        else:
            x = x + jnp.where(pos >= sh, pltpu.roll(x, sh, 0), 0.0)
    return x


def _chunk_row(x, row):
    r, n = x.shape
    x3 = x.reshape(r // HGRN_CHUNK, HGRN_CHUNK, n)
    return jnp.broadcast_to(x3[:, row:row + 1, :], x3.shape).reshape(r, n)


def _hgrn_pre(qh, z, lb):
    sig = _sigmoid(z)
    f = lb + (1.0 - lb) * sig
    k = 1.0 - f
    sq = _sigmoid(qh)
    q = qh * sq * HGRN_SCALE
    g = _chunk_cumsum(jnp.log(f))
    g_mid = _chunk_row(g, HGRN_CHUNK // 2 - 1)
    g_last = _chunk_row(g, HGRN_CHUNK - 1)
    e_q, e_k = jnp.exp(g - g_mid), jnp.exp(g_mid - g)
    e_in, e_out = jnp.exp(g), jnp.exp(g_last - g)
    return dict(sig=sig, f=f, k=k, sq=sq, q=q, g_last=g_last, e_q=e_q, e_k=e_k, e_in=e_in, e_out=e_out,
                qt=q * e_q, kt=k * e_k, qg=q * e_in, kout=k * e_out)


def _hgrn_mask():
    row = lax.broadcasted_iota(jnp.int32, (LANES, LANES), 0)
    col = lax.broadcasted_iota(jnp.int32, (LANES, LANES), 1)
    return (row // HGRN_CHUNK == col // HGRN_CHUNK) & (col <= row)


def _hgrn_in_specs(layer, rev, nblk):
    def blk(b):
        return nblk - 1 - b if rev else b
    first = 3 * ATTN_W // HGRN_W
    specs = [pl.BlockSpec((HGRN_ROWS, HGRN_W), functools.partial(lambda b, seg: (blk(b), first + seg), seg=seg))
             for seg in range(4)]
    specs.append(pl.BlockSpec((None, 1, HGRN_W), lambda b: (layer, 0, 0)))
    specs.append(pl.BlockSpec((None, 1, HGRN_DIM), lambda b: (layer, 0, 0)))
    return specs, blk


def _head(x, h):
    return x[:, h * HGRN_DIM:(h + 1) * HGRN_DIM]


def _chunk(x, c):
    return x[c * HGRN_CHUNK:(c + 1) * HGRN_CHUNK]


def _sub(x, sb):
    return x[sb * LANES:(sb + 1) * LANES]


HGRN_ROWS = 256
HEADS = range(HGRN_HEADS)
SUBS = range(HGRN_ROWS // LANES)
CHUNKS = range(HGRN_ROWS // HGRN_CHUNK)


def _hgrn_fwd(layer, proj, lb, gain, o_attn, gain_attn):
    s = proj.shape[0]
    nblk = s // HGRN_ROWS
    cpb = len(CHUNKS)

    def body(q_ref, f_ref, i_ref, g_ref, lb_ref, gain_ref, oa_ref, ga_ref, o_ref, mixed_ref, st_ref, state):
        @pl.when(pl.program_id(0) == 0)
        def _():
            state[...] = jnp.zeros(state.shape, F32)

        pre = _hgrn_pre(q_ref[...], f_ref[...], lb_ref[...])
        v = i_ref[...].astype(BF)
        qt, kt, qg, kout = (pre[n].astype(BF) for n in ("qt", "kt", "qg", "kout"))
        dec = jnp.exp(pre["g_last"])
        mask = _hgrn_mask()
        a = [[jnp.where(mask, _dot(_sub(_head(qt, h), sb), _sub(_head(kt, h), sb), NT), 0.0).astype(BF) for sb in SUBS]
             for h in HEADS]
        o_intra = [[_dot(a[h][sb], _sub(_head(v, h), sb), NN) for sb in SUBS] for h in HEADS]
        update = [[_dot(_chunk(_head(v, h), c), _chunk(_head(kout, h), c), TN) for c in CHUNKS] for h in HEADS]
        for h in HEADS:
            st = state[h]
            for c in CHUNKS:
                st_ref[h, c * LANES:(c + 1) * LANES, :] = st.astype(BF)
                st = st * _head(dec, h)[c * HGRN_CHUNK:c * HGRN_CHUNK + 1, :] + update[h][c]
            state[h] = st
        inter = [[_dot(_chunk(_head(qg, h), c), st_ref[h, c * LANES:(c + 1) * LANES, :].astype(BF), NT) for c in CHUNKS]
                 for h in HEADS]
        o = [jnp.concatenate(o_intra[h], axis=0) + jnp.concatenate(inter[h], axis=0) for h in HEADS]
        o_ref[...] = jnp.concatenate(o, axis=1)
        gate = g_ref[...]
        normed = jnp.concatenate([_rms_fwd(o[h], gain_ref[...]) for h in HEADS], axis=1)
        mixed_ref[:, :ATTN_W] = _rms_fwd(oa_ref[...], ga_ref[...]).astype(BF)
        mixed_ref[:, ATTN_W:] = (normed * (gate * _sigmoid(gate))).astype(BF)

    specs, _ = _hgrn_in_specs(layer, False, nblk)
    specs += [pl.BlockSpec((HGRN_ROWS, ATTN_W), lambda b: (b, 0)), pl.BlockSpec((None, 1, ATTN_W), lambda b: (layer, 0, 0))]
    return pl.pallas_call(
        body, grid=(nblk,), in_specs=specs,
        out_specs=[pl.BlockSpec((HGRN_ROWS, HGRN_W), lambda b: (b, 0)), pl.BlockSpec((HGRN_ROWS, MIX_W), lambda b: (b, 0)),
                   pl.BlockSpec((HGRN_HEADS, cpb * LANES, LANES), lambda b: (0, b, 0))],
        out_shape=[jax.ShapeDtypeStruct((s, HGRN_W), F32), jax.ShapeDtypeStruct((s, MIX_W), BF),
                   jax.ShapeDtypeStruct((HGRN_HEADS, nblk * cpb * LANES, LANES), BF)],
        scratch_shapes=[pltpu.VMEM((HGRN_HEADS, LANES, LANES), F32)],
        compiler_params=_cparams(("arbitrary",)), name=f"hgrn_fwd_l{layer}",
    )(proj, proj, proj, proj, lb, gain, o_attn, gain_attn)


def _relu2(u):
    return jnp.square(jnp.maximum(u, 0)).astype(BF)


def _mlp_fwd(layer, x, mixed, gain, g_out, g_up, g_down, head=None):
    s, d = x.shape
    mw = mixed.shape[1]
    nblk, hs = g_up.shape[1], g_up.shape[3]
    tm = 256 if head else 512

    def body(x_ref, m_ref, gain_ref, out_w_ref, up_ref, down_ref, *refs):
        if head:
            fin_ref, t_ref, o_ref, mid_ref, u_ref, h_ref, ob_ref, dfin_ref, loss_ref, a_buf = refs
        else:
            o_ref, mid_ref, u_ref, h_ref, a_buf = refs
        xv = x_ref[...] + _dot(m_ref[...], out_w_ref[...], NN)
        mid_ref[...] = xv
        h = _rms_fwd(xv, gain_ref[...]).astype(BF)
        h_ref[...] = h
        for j in range(nblk):
            u = _dot(h, up_ref[j], NN)
            u_ref[:, j * hs:(j + 1) * hs] = u.astype(BF)
            a_buf[:, j * hs:(j + 1) * hs] = _relu2(u)
        acc = xv
        for j in range(nblk):
            acc = acc + _dot(a_buf[:, j * hs:(j + 1) * hs], down_ref[j * hs:(j + 1) * hs, :], NN)
        if not head:
            o_ref[...] = acc
            return
        fin = fin_ref[...]
        err = _rms_fwd(acc, fin) - t_ref[...]
        dout, dfin = _rms_bwd(err * (1.0 / d), acc, fin)
        o_ref[...] = dout
        ob_ref[...] = dout.astype(BF)
        step = pl.program_id(0)
        _accumulate_rows(step, dfin_ref, _part8(dfin))
        _accumulate_rows(step, loss_ref, _part8(0.5 * jnp.mean(err * err, axis=-1, keepdims=True) * jnp.ones((1, LANES), F32)))

    row = pl.BlockSpec((tm, d), lambda i: (i, 0))
    in_specs = [row, pl.BlockSpec((tm, mw), lambda i: (i, 0)), pl.BlockSpec((None, 1, d), lambda i: (layer, 0, 0)),
                _resident((None, mw, d), lambda i: (0, 0, 0)),
                _resident((None, nblk, d, hs), lambda i: (0, 0, 0, 0)),
                _resident((None, nblk * hs, d), lambda i: (0, 0, 0))]
    out_specs = [row, row, pl.BlockSpec((tm, nblk * hs), lambda i: (i, 0)), row]
    out_shape = [jax.ShapeDtypeStruct((s, d), F32), jax.ShapeDtypeStruct((s, d), F32),
                 jax.ShapeDtypeStruct((s, nblk * hs), BF), jax.ShapeDtypeStruct((s, d), BF)]
    args = [x, mixed, gain, g_out, g_up, g_down]
    if head:
        in_specs += [pl.BlockSpec((1, d), lambda i: (0, 0)), row]
        out_specs += [row, pl.BlockSpec((8, d), lambda i: (0, 0)), pl.BlockSpec((8, LANES), lambda i: (0, 0))]
        out_shape += [jax.ShapeDtypeStruct((s, d), BF), jax.ShapeDtypeStruct((8, d), F32), jax.ShapeDtypeStruct((8, LANES), F32)]
        args += list(head)
    return pl.pallas_call(
        body, grid=(s // tm,), in_specs=in_specs, out_specs=out_specs, out_shape=out_shape,
        scratch_shapes=[pltpu.VMEM((tm, nblk * hs), BF)],
        compiler_params=_cparams(("arbitrary",) if head else ("parallel",)), name=f"mlp_fwd_l{layer}",
    )(*args)


def _accumulate_rows(i, ref, part):
    @pl.when(i == 0)
    def _():
        ref[...] = part

    @pl.when(i > 0)
    def _():
        ref[...] += part


def _head_sums(prod):
    row = lax.broadcasted_iota(jnp.int32, (ATTN_W, ATTN_W), 0)
    col = lax.broadcasted_iota(jnp.int32, (ATTN_W, ATTN_W), 1)
    same_head = jnp.where(row // 64 == col // 64, 1.0, 0.0).astype(BF)
    high = prod.astype(BF)
    low = (prod - high.astype(F32)).astype(BF)
    return _dot(high, same_head, NN) + _dot(low, same_head, NN)


def _mlp_bwd(layer, dx, dxb, x, gain, u, o_attn, gain_attn, g_out, g_up, g_down, tm=256):
    s, d = x.shape
    mw = g_out.shape[1]
    nblk, hs = g_up.shape[1], g_up.shape[3]

    def body(dx_ref, dxb_ref, x_ref, gain_ref, u_ref, oa_ref, ga_ref, out_w_ref, up_ref, down_ref, o_ref, ob_ref, du_ref,
             drec_ref, do_ref, delta_ref, dgain_ref, dattn_ref):
        dxb_v = dxb_ref[...]
        for j in range(nblk):
            cols = slice(j * hs, (j + 1) * hs)
            da = _dot(dxb_v, down_ref[cols, :], NT)
            du_ref[:, cols] = (da * (2.0 * jnp.maximum(u_ref[:, cols].astype(F32), 0.0))).astype(BF)
        acc = jnp.zeros((tm, d), F32)
        for j in range(nblk):
            acc = acc + _dot(du_ref[:, j * hs:(j + 1) * hs], up_ref[j], NT)
        dxn, dgain = _rms_bwd(acc, x_ref[...], gain_ref[...])
        out = dx_ref[...] + dxn
        out_b = out.astype(BF)
        o_ref[...] = out
        ob_ref[...] = out_b
        dm = _dot(out_b, out_w_ref[...], NT)
        drec_ref[...] = dm[:, ATTN_W:]
        ov = oa_ref[...]
        do, dattn = _rms_bwd(dm[:, :ATTN_W], ov, ga_ref[...])
        do_ref[...] = do
        delta_ref[...] = _head_sums(do * ov)
        _accumulate_rows(pl.program_id(0), dgain_ref, _part8(dgain))
        _accumulate_rows(pl.program_id(0), dattn_ref, _part8(dattn))

    row = pl.BlockSpec((tm, d), lambda i: (i, 0))
    wide = pl.BlockSpec((tm, nblk * hs), lambda i: (i, 0))
    half = pl.BlockSpec((tm, ATTN_W), lambda i: (i, 0))
    return pl.pallas_call(
        body, grid=(s // tm,),
        in_specs=[row, row, row, pl.BlockSpec((None, 1, d), lambda i: (layer, 0, 0)), wide,
                  half, pl.BlockSpec((None, 1, ATTN_W), lambda i: (layer, 0, 0)),
                  _resident((None, mw, d), lambda i: (0, 0, 0)),
                  _resident((None, nblk, d, hs), lambda i: (0, 0, 0, 0)),
                  _resident((None, nblk * hs, d), lambda i: (0, 0, 0))],
        out_specs=[row, row, wide, half, half, half, pl.BlockSpec((8, d), lambda i: (0, 0)),
                   pl.BlockSpec((8, ATTN_W), lambda i: (0, 0))],
        out_shape=[jax.ShapeDtypeStruct((s, d), F32), jax.ShapeDtypeStruct((s, d), BF),
                   jax.ShapeDtypeStruct((s, nblk * hs), BF)] + [jax.ShapeDtypeStruct((s, ATTN_W), F32)] * 3
        + [jax.ShapeDtypeStruct((8, d), F32), jax.ShapeDtypeStruct((8, ATTN_W), F32)],
        compiler_params=_cparams(("arbitrary",)), name=f"mlp_bwd_l{layer}",
    )(dx, dxb, x, gain, u, o_attn, gain_attn, g_out, g_up, g_down)


def _attn_bwd(layer, proj, do, lse, delta, cos, sin, after):
    s = proj.shape[0]
    n_units = s // ATTN_UNIT
    out_rows = 256

    def unit_of(n):
        return n_units - 1 - n

    def body(q_ref, kp_ref, kc_ref, vp_ref, vc_ref, do_ref, lse_ref, delta_ref, cos_ref, sin_ref, after_ref, out_ref,
             dq_ref, dk_ref, dkp_ref, dv_ref, dvp_ref, carry_k, carry_v):
        step = pl.program_id(1)
        m_first, m_rest, is_a = _attn_masks(unit_of(step) > 0)
        sels = (is_a, jnp.logical_not(is_a))
        for ref in (dq_ref, dk_ref, dkp_ref, dv_ref, dvp_ref):
            ref[...] = jnp.zeros(ref.shape, F32)
        for dil in DILATIONS:
            last = ATTN_UNIT // (SPAN * dil) - 1
            for group in _attn_groups(dil):
                items = [(r, b, h) for r, b in group for h in range(2)]
                q = {rb: q_ref[_block_rows(dil, *rb), :] for rb in group}
                dov = {rb: do_ref[_block_rows(dil, *rb), :] for rb in group}
                lse_v = {rb: lse_ref[_block_rows(dil, *rb), :] for rb in group}
                delta_v = {rb: delta_ref[_block_rows(dil, *rb), :] for rb in group}
                k = {rb: _block_keys(kp_ref, kc_ref, dil, *rb).astype(BF) for rb in group}
                v = {rb: _block_keys(vp_ref, vc_ref, dil, *rb).astype(BF) for rb in group}
                qh = [jnp.where(sels[h], q[r, b], 0.0).astype(BF) for r, b, h in items]
                doh = [jnp.where(sels[h], dov[r, b], 0.0).astype(BF) for r, b, h in items]
                sc = [jnp.where(m_first if b == 0 else m_rest, _dot(qh[i], k[r, b], NT), MASK_VALUE)
                      for i, (r, b, h) in enumerate(items)]
                p = [jnp.exp(sc[i] - lse_v[r, b][:, 64 * h:64 * h + 1]) for i, (r, b, h) in enumerate(items)]
                ds = [(p[i] * (_dot(doh[i], v[r, b], NT) - delta_v[r, b][:, 64 * h:64 * h + 1])).astype(BF)
                      for i, (r, b, h) in enumerate(items)]
                dv = [_dot(p[i].astype(BF), doh[i], TN) for i in range(len(items))]
                dq = [_dot(ds[i], k[r, b], NN) for i, (r, b, h) in enumerate(items)]
                dk = [_dot(ds[i], qh[i], TN) for i in range(len(items))]
                for j, (r, b) in enumerate(group):
                    own = _block_rows(dil, r, b)
                    dq_ref[own, :] += jnp.where(is_a, dq[2 * j], dq[2 * j + 1])
                    dk2, dv2 = dk[2 * j] + dk[2 * j + 1], dv[2 * j] + dv[2 * j + 1]
                    dk_ref[own, :] += dk2[SPAN:]
                    dv_ref[own, :] += dv2[SPAN:]
                    if b > 0:
                        before = _block_rows(dil, r, b - 1)
                        dk_ref[before, :] += dk2[:SPAN]
                        dv_ref[before, :] += dv2[:SPAN]
                    else:
                        before = _block_rows(dil, r, last)
                        dkp_ref[before, :] += dk2[:SPAN]
                        dvp_ref[before, :] += dv2[:SPAN]
        has_next = step > 0
        for c in range(ATTN_UNIT // out_rows):
            rows = slice(c * out_rows, (c + 1) * out_rows)
            cs, sn = cos_ref[rows, :], sin_ref[rows, :]
            dqv = dq_ref[rows, :]
            dkv = dk_ref[rows, :] + jnp.where(has_next, carry_k[rows, :], 0.0)
            dvv = dv_ref[rows, :] + jnp.where(has_next, carry_v[rows, :], 0.0)
            out_ref[0, rows, :] = ((dqv * cs - _rope_partner(dqv) * sn) * ATTN_SCALE).astype(BF)
            out_ref[1, rows, :] = (dkv * cs - _rope_partner(dkv) * sn).astype(BF)
            out_ref[2, rows, :] = dvv.astype(BF)
        carry_k[...] = dkp_ref[...]
        carry_v[...] = dvp_ref[...]

    tab = pl.BlockSpec((ATTN_UNIT, LANES), lambda p, n: (unit_of(n), 0))
    return pl.pallas_call(
        body, grid=(ATTN_W // LANES, n_units), in_specs=_attn_specs(3, unit_of) + [tab, tab, ANY_SPEC],
        out_specs=pl.BlockSpec((3, ATTN_UNIT, LANES), lambda p, n: (0, unit_of(n), p)),
        out_shape=jax.ShapeDtypeStruct((3, s, ATTN_W), BF),
        scratch_shapes=[pltpu.VMEM((ATTN_UNIT, LANES), F32)] * 7,
        compiler_params=_cparams(("parallel", "arbitrary")), name=f"attn_bwd_l{layer}",
    )(proj, proj, proj, proj, proj, do, lse, delta, cos, sin, after)


def _hgrn_bwd(layer, proj, lb, gain, o, drec, states):
    s = proj.shape[0]
    nblk = s // HGRN_ROWS
    cpb = len(CHUNKS)

    def body(q_ref, f_ref, i_ref, g_ref, lb_ref, gain_ref, o_ref, drec_ref, st_ref, dseg_ref, dlb_ref, dgain_ref,
             dstate, dst_buf):
        step = pl.program_id(0)

        @pl.when(step == 0)
        def _():
            dstate[...] = jnp.zeros(dstate.shape, F32)

        lbv, gv = lb_ref[...], gain_ref[...]
        qh, z, gate_in = q_ref[...], f_ref[...], g_ref[...]
        pre = _hgrn_pre(qh, z, lbv)
        v = i_ref[...].astype(BF)
        sg = _sigmoid(gate_in)
        ov, drec = o_ref[...], drec_ref[...]
        dnormed = drec * (gate_in * sg)
        back = [_rms_bwd(_head(dnormed, h), _head(ov, h), gv) for h in HEADS]
        do_b = jnp.concatenate([b[0] for b in back], axis=1).astype(BF)
        dgain = back[0][1] + back[1][1] + back[2][1] + back[3][1]
        normed = jnp.concatenate([_rms_fwd(_head(ov, h), gv) for h in HEADS], axis=1)
        dgate_in = drec * normed * (sg * (1.0 + gate_in * (1.0 - sg)))
        mask = _hgrn_mask()
        qt, kt, qg, kout = (pre[n].astype(BF) for n in ("qt", "kt", "qg", "kout"))
        dec = jnp.exp(pre["g_last"])
        def intra(fn):
            return jnp.concatenate([jnp.concatenate([fn(h, sb) for sb in SUBS], axis=0) for h in HEADS], axis=1)

        def hs(x, h, sb):
            return _sub(_head(x, h), sb)

        a = [[jnp.where(mask, _dot(hs(qt, h, sb), hs(kt, h, sb), NT), 0.0).astype(BF) for sb in SUBS] for h in HEADS]
        da = [[jnp.where(mask, _dot(hs(do_b, h, sb), hs(v, h, sb), NT), 0.0).astype(BF) for sb in SUBS] for h in HEADS]
        dv_intra = intra(lambda h, sb: _dot(a[h][sb], hs(do_b, h, sb), TN))
        dqt = intra(lambda h, sb: _dot(da[h][sb], hs(kt, h, sb), NN))
        dkt = intra(lambda h, sb: _dot(da[h][sb], hs(qt, h, sb), TN))
        feed = [[_dot(_chunk(_head(do_b, h), c), _chunk(_head(qg, h), c), TN) for c in CHUNKS] for h in HEADS]
        for h in HEADS:
            dst = dstate[h]
            for c in reversed(CHUNKS):
                dst_buf[h, c * LANES:(c + 1) * LANES, :] = dst
                dst = dst * _head(dec, h)[c * HGRN_CHUNK:c * HGRN_CHUNK + 1, :] + feed[h][c]
            dstate[h] = dst

        def per_chunk(fn):
            cols = []
            for h in HEADS:
                rows = [jnp.broadcast_to(t, (HGRN_CHUNK, HGRN_DIM)) for t in (fn(h, c) for c in CHUNKS)]
                cols.append(jnp.concatenate(rows, axis=0))
            return jnp.concatenate(cols, axis=1)

        def st_prev(h, c):
            return st_ref[h, c * LANES:(c + 1) * LANES, :]

        def dst_at(h, c):
            return dst_buf[h, c * LANES:(c + 1) * LANES, :]

        dqg = per_chunk(lambda h, c: _dot(_chunk(_head(do_b, h), c), st_prev(h, c).astype(BF), NN))
        dkout = per_chunk(lambda h, c: _dot(_chunk(_head(v, h), c), dst_at(h, c).astype(BF), NN))
        dv_inter = per_chunk(lambda h, c: _dot(_chunk(_head(kout, h), c), dst_at(h, c).astype(BF), NT))
        dg_state = per_chunk(lambda h, c: jnp.sum(dst_at(h, c) * st_prev(h, c).astype(F32), axis=0, keepdims=True))
        dg_kout = per_chunk(lambda h, c: jnp.sum(_chunk(_head(dkout * pre["kout"], h), c), axis=0, keepdims=True))
        dv = dv_intra + dv_inter
        pos = lax.broadcasted_iota(jnp.int32, (HGRN_ROWS, HGRN_W), 0) % HGRN_CHUNK
        dq = dqt * pre["e_q"] + dqg * pre["e_in"]
        dk = dkt * pre["e_k"] + dkout * pre["e_out"]
        dg = (dqt * pre["qt"] - dkt * pre["kt"] + dqg * pre["qg"] - dkout * pre["kout"]
              + jnp.where(pos == HGRN_CHUNK - 1, dg_state * dec + dg_kout, 0.0))
        dlogf = _chunk_cumsum(dg, reverse=True)
        sig, sq = pre["sig"], pre["sq"]
        df = dlogf / pre["f"] - dk
        dseg_ref[0] = (dq * HGRN_SCALE * (sq * (1.0 + qh * (1.0 - sq)))).astype(BF)
        dseg_ref[1] = (df * (1.0 - lbv) * sig * (1.0 - sig)).astype(BF)
        dseg_ref[2] = dv.astype(BF)
        dseg_ref[3] = dgate_in.astype(BF)
        _accumulate_rows(step, dlb_ref, _part8(df * (1.0 - sig)))
        _accumulate_rows(step, dgain_ref, _part8(dgain))

    specs, blk = _hgrn_in_specs(layer, True, nblk)
    specs += [pl.BlockSpec((HGRN_ROWS, HGRN_W), lambda b: (blk(b), 0)),
              pl.BlockSpec((HGRN_ROWS, HGRN_W), lambda b: (blk(b), 0)),
              pl.BlockSpec((HGRN_HEADS, cpb * LANES, LANES), lambda b: (0, blk(b), 0))]
    return pl.pallas_call(
        body, grid=(nblk,), in_specs=specs,
        out_specs=[pl.BlockSpec((4, HGRN_ROWS, HGRN_W), lambda b: (0, blk(b), 0)),
                   pl.BlockSpec((8, HGRN_W), lambda b: (0, 0)), pl.BlockSpec((8, HGRN_DIM), lambda b: (0, 0))],
        out_shape=[jax.ShapeDtypeStruct((4, s, HGRN_W), BF), jax.ShapeDtypeStruct((8, HGRN_W), F32),
                   jax.ShapeDtypeStruct((8, HGRN_DIM), F32)],
        scratch_shapes=[pltpu.VMEM((HGRN_HEADS, LANES, LANES), F32), pltpu.VMEM((HGRN_HEADS, cpb * LANES, LANES), F32)],
        compiler_params=_cparams(("arbitrary",)), name=f"hgrn_bwd_l{layer}",
    )(proj, proj, proj, proj, lb, gain, o, drec, states)


def _bwd_inproj(layer, dqkv, dhg, g_in, x, gain, dres, tm=512):
    s, d = x.shape

    def body(dqkv_ref, dhg_ref, w_ref, x_ref, gain_ref, dres_ref, dx_ref, dxb_ref, dgain_ref):
        acc = jnp.zeros((tm, d), F32)
        for seg in range(N_SEG):
            a = dqkv_ref[seg] if seg < 3 else dhg_ref[seg - 3]
            acc = acc + _dot(a, w_ref[seg * SEG:(seg + 1) * SEG, :], NN)
        dx, dgain = _rms_bwd(acc, x_ref[...], gain_ref[...])
        out = dres_ref[...] + dx
        dx_ref[...] = out
        dxb_ref[...] = out.astype(BF)
        _accumulate_rows(pl.program_id(0), dgain_ref, _part8(dgain))

    row = pl.BlockSpec((tm, d), lambda i: (i, 0))
    return pl.pallas_call(
        body, grid=(s // tm,),
        in_specs=[pl.BlockSpec((3, tm, SEG), lambda i: (0, i, 0)), pl.BlockSpec((4, tm, SEG), lambda i: (0, i, 0)),
                  _resident((None, PROJ_W, d), lambda i: (0, 0, 0)), row,
                  pl.BlockSpec((None, 1, d), lambda i: (layer, 0, 0)), row],
        out_specs=[row, row, pl.BlockSpec((8, d), lambda i: (0, 0))],
        out_shape=[jax.ShapeDtypeStruct((s, d), F32), jax.ShapeDtypeStruct((s, d), BF), jax.ShapeDtypeStruct((8, d), F32)],
        compiler_params=_cparams(("arbitrary",)), name=f"bwd_inproj_l{layer}",
    )(dqkv, dhg, g_in, x, gain, dres)


def _adamw(w, g, m, v):
    m2 = ADAM_B1 * m + (1.0 - ADAM_B1) * g
    v2 = ADAM_B2 * v + (1.0 - ADAM_B2) * (g * g)
    m_hat = m2 / (1.0 - ADAM_B1 ** ADAM_STEP)
    v_hat = v2 / (1.0 - ADAM_B2 ** ADAM_STEP)
    delta = -ADAM_LR * (m_hat / (jnp.sqrt(v_hat) + ADAM_EPS) + ADAM_WD * w)
    return delta, m2, v2


def _adam_big(name, parts, w, m, v, row_tiles):
    depth = w.shape[0]
    r, c = parts[0].shape[1], parts[0].shape[2]
    tr = r // row_tiles
    p_spec = pl.BlockSpec((N_DEV, tr, c), lambda t: (0, t, 0))
    w_spec = pl.BlockSpec((depth, tr, c), lambda t: (0, t, 0))

    def body(*refs):
        p_refs = refs[:depth]
        w_ref, m_ref, v_ref, g_ref, d_ref, m2_ref, v2_ref, token = refs[depth:]
        token[...] = jnp.zeros(token.shape, F32)
        for l in range(depth):
            g = p_refs[l][0].astype(F32)
            for dev in range(1, N_DEV):
                g = g + p_refs[l][dev].astype(F32)
            delta, m2, v2 = _adamw(w_ref[l], g, m_ref[l], v_ref[l])
            g_ref[l] = g
            d_ref[l] = delta
            m2_ref[l] = m2
            v2_ref[l] = v2

    return pl.pallas_call(
        body, grid=(row_tiles,), in_specs=[p_spec] * depth + [w_spec] * 3,
        out_specs=[w_spec] * 4 + [pl.BlockSpec((8, LANES), lambda t: (0, 0))],
        out_shape=[jax.ShapeDtypeStruct(w.shape, F32)] * 4 + [jax.ShapeDtypeStruct((8, LANES), F32)],
        compiler_params=_cparams(("arbitrary",)), name=name,
    )(*parts, w, m, v)


def _adam_small(g, ws, ms, vs):
    n = len(ws)

    def split(row, width):
        return jnp.concatenate([row[:, :width], row[:, width:2 * width]], axis=0)

    def body(g_ref, *refs):
        ins, outs = refs[:3 * n], refs[3 * n:]
        grads = [g_ref[0:2, :], split(g_ref[5:6, :], ATTN_W), split(g_ref[6:7, :], HGRN_W), split(g_ref[7:8, :], HGRN_DIM),
                 g_ref[2:4, :], g_ref[4:5, :]]
        for i, g_i in enumerate(grads):
            delta, m2, v2 = _adamw(ins[i][...], g_i, ins[n + i][...], ins[2 * n + i][...])
            for j, val in enumerate((g_i, delta, m2, v2)):
                outs[4 * i + j][...] = val

    vm = pl.BlockSpec(memory_space=pltpu.VMEM)
    res = pl.pallas_call(
        body, in_specs=[vm] * (1 + 3 * n), out_specs=[vm] * (4 * n),
        out_shape=[jax.ShapeDtypeStruct(w.shape, F32) for w in ws for _ in range(4)], name="adam_small",
    )(g, *ws, *ms, *vs)
    return [res[4 * i:4 * i + 4] for i in range(n)]


def _lower_bounds(logits):
    def body(l_ref, lb_ref, jac_ref):
        l0, l1 = l_ref[0:1, :], l_ref[1:2, :]
        mx = jnp.maximum(l0, l1)
        e0, e1 = jnp.exp(l0 - mx), jnp.exp(l1 - mx)
        p0, p1 = e0 / (e0 + e1), e1 / (e0 + e1)
        lb_ref[0:1, :] = p0 - p0
        lb_ref[1:2, :] = (p0 + p1) - p0
        jac_ref[0:1, :] = -p0 * p1
        jac_ref[1:2, :] = p0 * p1

    vm = pl.BlockSpec(memory_space=pltpu.VMEM)
    return pl.pallas_call(body, in_specs=[vm], out_specs=[vm, vm], out_shape=[jax.ShapeDtypeStruct(logits.shape, F32)] * 2,
                          name="hgrn_lower_bounds")(logits)


def _rope_tables(s, after):
    half = 32
    inv_freq = ROPE_THETA ** (-jnp.arange(half, dtype=F32) / half)
    ang = (jnp.arange(s, dtype=jnp.int32).astype(F32) + after[0, 0])[:, None] * inv_freq[None, :]
    cos, sin = jnp.cos(ang), jnp.sin(ang)
    return jnp.concatenate([cos] * 4, axis=1), jnp.concatenate([-sin, sin, -sin, sin], axis=1)


def kernel(x, norm_mix, w_in, attn_out_gain, hgrn_lb_logits, hgrn_out_gain, w_out, norm_mlp, w_up, w_down, norm_final, loss_target, m_norm_mix, m_w_in, m_attn_out_gain, m_hgrn_lb_logits, m_hgrn_out_gain, m_w_out, m_norm_mlp, m_w_up, m_w_down, m_norm_final, v_norm_mix, v_w_in, v_attn_out_gain, v_hgrn_lb_logits, v_hgrn_out_gain, v_w_out, v_norm_mlp, v_w_up, v_w_down, v_norm_final):
    depth = w_in.shape[0]
    assert depth == 2 and x.shape[0] == 1
    s, d = x.shape[1], x.shape[2]
    x0 = x[0]
    target = loss_target[0]
    g_mix, g_attn, g_hg, g_mlp = (norm_mix[:, None, :], attn_out_gain[:, None, :], hgrn_out_gain[:, None, :],
                                  norm_mlp[:, None, :])
    lb, lb_jac = _lower_bounds(hgrn_lb_logits)
    lb3 = lb[:, None, :]

    def flip(a):
        return jnp.swapaxes(a, 1, 2)

    shards = list(_pack_weights(flip(w_in), w_out, w_up, w_down))
    w_pieces = _weight_pieces(*shards)
    w_groups = [[0], [1, 2, 3], [4], [5, 6, 7]]
    me = (4 * lax.axis_index("x") + 2 * lax.axis_index("y") + lax.axis_index("c")).astype(jnp.int32).reshape(1)
    lands = _exchange_own("all_gather_own", me, shards, w_pieces)
    w_sems, shards, lands, token = _exchange_start("all_gather_start", shards, lands, w_pieces, w_groups)

    def weights_ready(group, after):
        nonlocal shards
        idxs = w_groups[group]
        shards, got = _exchange_wait(f"all_gather_wait{group}", shards, [lands[i] for i in idxs], w_pieces,
                                     [(idxs, *w_sems[group])], after)
        return got

    cos, sin = _rope_tables(s, token)

    def tied(small_arr, tok):
        return small_arr + tok[0, 0]

    saved = []
    xl = x0
    full = [None] * depth
    for l in range(depth):
        (full_in,) = weights_ready(2 * l, cos if l == 0 else xl)
        saved_x = xl
        proj, h = _fwd_inproj(l, xl, g_mix, full_in, cos, sin)
        o_attn, lse = _attn_fwd(l, proj)
        o_hg, mixed, states = _hgrn_fwd(l, proj, lb3, g_hg, o_attn, g_attn)
        full_out, full_up, full_down = weights_ready(2 * l + 1, mixed)
        head = (norm_final[None, :], target) if l == depth - 1 else None
        xl, x_mid, u, h2, *loss_side = _mlp_fwd(l, xl, mixed, g_mlp, full_out, full_up, full_down, head)
        saved.append((saved_x, proj, h, o_attn, lse, o_hg, states, mixed, x_mid, u, h2))
        full[l] = (full_in, full_out, full_up, full_down)
    dx, (dxb, dnorm_final8, loss8) = xl, loss_side

    exchanges = []

    def scatter(tag, grads, kinds):
        pieces = _grad_pieces(grads, kinds)
        own = _exchange_own(f"reduce_scatter_own_{tag}", me, grads, pieces)
        sems, grads, own, tok = _exchange_start(f"reduce_scatter_start_{tag}", grads, own, pieces, [list(range(len(pieces)))])
        exchanges.append((grads, own, pieces, sems[0]))
        return tok

    small = {}
    for l in reversed(range(depth)):
        xl, proj, h, o_attn, lse, o_hg, states, mixed, x_mid, u, h2 = saved[l]
        full_in, full_out, full_up, full_down = full[l]
        hs = full_up.shape[3]
        gw_down = _mm_tn(f"grad_w_down_l{l}", u, dxb, u.shape[1], a_fn=_relu2)
        dx_mid, dx_mid_b, du, drec, do, delta, dmlp8, dattn8 = _mlp_bwd(l, dx, dxb, x_mid, g_mlp, u, o_attn, g_attn,
                                                                         full_out, full_up, full_down)
        gw_up = _mm_tn(f"grad_w_up_l{l}", h2, du, d, out_block_w=hs)
        gw_out = _mm_tn(f"grad_w_out_l{l}", mixed, dx_mid_b, mixed.shape[1])
        started = scatter(f"mlp_l{l}", [gw_down, gw_up, gw_out], ["rows", "up", "rows"])
        dqkv = _attn_bwd(l, proj, do, lse, delta, cos, sin, started)
        dhg, dlb8, dhgain8 = _hgrn_bwd(l, proj, lb3, g_hg, o_hg, drec, states)
        gin = _mm_tn(f"grad_w_in_qkv_l{l}", dqkv, h, PROJ_W, a_lead=True)
        gw_in = _mm_tn(f"grad_w_in_hg_l{l}", dhg, h, PROJ_W, a_lead=True, out_block_off=3, prev=gin)
        g_mix_t = tied(g_mix, scatter(f"mix_l{l}", [gw_in], ["rows"]))
        dx, dxb, dmix8 = _bwd_inproj(l, dqkv, dhg, full_in, xl, g_mix_t, dx_mid)
        small[l] = (dmix8, dattn8, dlb8, dhgain8, dmlp8)

    def scattered(name, which, after):
        grads, lands, pieces, waits = [], [], [], []
        for grads_e, own, pieces_e, (send, recv) in (exchanges[i] for i in which):
            first = len(pieces)
            pieces += [p._replace(src=p.src + len(grads)) for p in pieces_e]
            waits.append((list(range(first, first + len(pieces_e))), send, recv))
            grads += grads_e
            lands += own
        return _exchange_wait(name, grads, lands, pieces, waits, after)[1]

    down1, up1, out1, in1, down0, up0, out0 = scattered("reduce_scatter_wait_early", (0, 1, 2), dx)
    big = {
        "w_down": _adam_big("adam_w_down", [down0, down1], w_down, m_w_down, v_w_down, 4),
        "w_up": _adam_big("adam_w_up", [up0, up1], w_up, m_w_up, v_w_up, 2),
        "w_out": _adam_big("adam_w_out", [out0, out1], w_out, m_w_out, v_w_out, 1),
    }
    g_small = _all_reduce_small([small[l][0] for l in range(depth)], [small[l][4] for l in range(depth)], dnorm_final8,
                                [small[l][1] for l in range(depth)], small[depth - 1][2], lb_jac,
                                [small[l][3] for l in range(depth)], loss8, big["w_out"][4])
    loss = g_small[7, 2 * HGRN_DIM]
    row = lambda a: a[None, :]
    small_out = _adam_small(
        g_small, [norm_mix, attn_out_gain, hgrn_lb_logits, hgrn_out_gain, norm_mlp, row(norm_final)],
        [m_norm_mix, m_attn_out_gain, m_hgrn_lb_logits, m_hgrn_out_gain, m_norm_mlp, row(m_norm_final)],
        [v_norm_mix, v_attn_out_gain, v_hgrn_lb_logits, v_hgrn_out_gain, v_norm_mlp, row(v_norm_final)])
    small_out[5] = [t[0] for t in small_out[5]]
    (in0,) = scattered("reduce_scatter_wait_last", (3,), small_out[0][1])
    big["w_in"] = [flip(t) for t in _adam_big("adam_w_in", [in0, in1], flip(w_in), flip(m_w_in), flip(v_w_in), 2)[:4]]

    def gather(idx):
        mix, attn, lbl, hg, mlp, final = (t[idx] for t in small_out)
        return [mix, big["w_in"][idx], attn, lbl, hg, big["w_out"][idx], mlp, big["w_up"][idx], big["w_down"][idx], final]

    return (loss, dx[None], *gather(0), *gather(1), *gather(2), *gather(3))
```

```python
import functools
from typing import Callable, NamedTuple

import jax
import jax.numpy as jnp
from jax import lax
from jax.experimental import pallas as pl
from jax.experimental.pallas import tpu as pltpu

F32 = jnp.float32
BF = jnp.bfloat16

N_DEV = 8
ATTN_W = 512
HGRN_W = 512
HGRN_HEADS = 4
HGRN_DIM = 128
SEG = 512
N_SEG = 7
PROJ_W = N_SEG * SEG
MIX_W = ATTN_W + HGRN_W
SPAN = 128
DILATIONS = (1, 4, 16)
HGRN_CHUNK = 16
ROPE_THETA = 10000.0
NORM_EPS = 1e-6
MASK_VALUE = -1e30
ATTN_SCALE = 0.125
HGRN_SCALE = HGRN_DIM ** -0.5
ADAM_LR = 0.001
ADAM_B1 = 0.9
ADAM_B2 = 0.999
ADAM_EPS = 1e-08
ADAM_WD = 0.01
ADAM_STEP = 10
LANES = 128
VMEM_LIMIT = 56 * 1024 * 1024

NN = ((1,), (0,))
NT = ((1,), (1,))
TN = ((0,), (0,))
MESH = pl.DeviceIdType.MESH


def _dot(a, b, dims):
    return lax.dot_general(a, b, (dims, ((), ())), preferred_element_type=F32)


def _cparams(sem):
    return pltpu.CompilerParams(dimension_semantics=sem, vmem_limit_bytes=VMEM_LIMIT)


def _part8(x):
    r, n = x.shape
    return jnp.sum(x.reshape(r // 8, 8, n), axis=0)


def _sigmoid(x):
    return 1.0 / (1.0 + jnp.exp(-x))


def _rms_fwd(x, gain):
    r = lax.rsqrt(jnp.mean(x * x, axis=-1, keepdims=True) + NORM_EPS)
    return x * r * gain


def _rms_bwd(dy, x, gain):
    r = lax.rsqrt(jnp.mean(x * x, axis=-1, keepdims=True) + NORM_EPS)
    xn = x * r
    dxn = dy * gain
    dx = r * (dxn - xn * jnp.mean(dxn * xn, axis=-1, keepdims=True))
    return dx, dy * xn


def _rope_partner(x):
    n = x.shape[-1]
    lane = lax.broadcasted_iota(jnp.int32, x.shape, x.ndim - 1)
    return jnp.where((lane % 64) < 32, pltpu.roll(x, n - 32, x.ndim - 1), pltpu.roll(x, 32, x.ndim - 1))


def _tile_lanes(t, reps):
    return jnp.concatenate([t] * reps, axis=-1)


def _mm_tn(name, a, b, out_rows, a_lead=False, out_block_off=0, prev=None, out_block_w=None, a_fn=None,
           tm=512, tn=1024, sub=512):
    kdim, n = b.shape
    m = a.shape[-1]
    tm, tn, sub = min(tm, m), min(tn, n), min(sub, kdim)
    mt = m // tm
    n_lead = a.shape[0] if a_lead else 1
    if a_lead:
        a_spec = pl.BlockSpec((None, kdim, tm), lambda j, i: (i // mt, 0, i % mt))
    else:
        a_spec = pl.BlockSpec((kdim, tm), lambda j, i: (0, i))
    b_spec = pl.BlockSpec((kdim, tn), lambda j, i: (0, j))
    if out_block_w:
        nb = tn // out_block_w
        o_shape = jax.ShapeDtypeStruct((n // out_block_w, out_rows, out_block_w), BF)
        o_spec = pl.BlockSpec((nb, tm, out_block_w), lambda j, i: (j, i + out_block_off, 0))
    else:
        nb = 0
        o_shape = jax.ShapeDtypeStruct((out_rows, n), BF)
        o_spec = pl.BlockSpec((tm, tn), lambda j, i: (i + out_block_off, j))
    arrays, specs, aliases = [a, b], [a_spec, b_spec], {}
    if prev is not None:
        arrays.append(prev)
        specs.append(pl.BlockSpec(memory_space=pl.ANY))
        aliases = {2: 0}

    def body(*refs):
        a_ref, b_ref, o_ref = refs[0], refs[1], refs[-1]
        acc = None
        for k in range(kdim // sub):
            av = a_ref[k * sub:(k + 1) * sub, :]
            if a_fn is not None:
                av = a_fn(av)
            part = _dot(av, b_ref[k * sub:(k + 1) * sub, :], TN)
            acc = part if acc is None else acc + part
        if nb:
            for t in range(nb):
                o_ref[t] = acc[:, t * out_block_w:(t + 1) * out_block_w].astype(BF)
        else:
            o_ref[...] = acc.astype(BF)

    return pl.pallas_call(
        body, grid=(n // tn, n_lead * mt), in_specs=specs, out_specs=o_spec, out_shape=o_shape,
        compiler_params=_cparams(("parallel", "parallel")), name=name, input_output_aliases=aliases,
    )(*arrays)


def _pack_weights(w_in_t, w_out, w_up, w_down):
    depth = w_in_t.shape[0]
    arrays = (w_in_t, w_out, w_up, w_down)

    def body(*refs):
        for src, dst in zip(refs[:4], refs[4:]):
            dst[...] = src[...].astype(BF)

    specs = [pl.BlockSpec((None,) + a.shape[1:], lambda l: (l, 0, 0)) for a in arrays]
    return pl.pallas_call(
        body, grid=(depth,), in_specs=specs, out_specs=specs,
        out_shape=[jax.ShapeDtypeStruct(a.shape, BF) for a in arrays],
        compiler_params=_cparams(("arbitrary",)), name="pack_weights",
    )(*arrays)


def _my_position():
    x, y, c = lax.axis_index("x"), lax.axis_index("y"), lax.axis_index("c")
    return x, y, c, 4 * x + 2 * y + c


def _peer(x, y, c, k):
    px = 1 - x if k & 4 else x
    py = 1 - y if k & 2 else y
    pc = 1 - c if k & 1 else c
    return (px, py, pc), 4 * px + 2 * py + pc


PEER_ORDER = (1, 2, 4, 3, 5, 6, 7)


class _Piece(NamedTuple):
    src: int
    send: Callable
    slot: Callable
    land_shape: tuple
    own_src: tuple
    own_slot: tuple


HBM_SPEC = pl.BlockSpec(memory_space=pltpu.HBM)
SEM_SPEC = pl.BlockSpec(memory_space=pltpu.SEMAPHORE)
ANY_SPEC = pl.BlockSpec(memory_space=pl.ANY)


def _in_hbm(arrays):
    return [pltpu.with_memory_space_constraint(a, pltpu.HBM) for a in arrays]


def _hbm_like(arrays):
    return [pltpu.HBM(a.shape, a.dtype) for a in arrays]


def _rows_of(rows):
    return lambda ref, dev: ref.at[pl.ds(pl.multiple_of(dev * rows, 16), rows), :]


def _exchange_own(name, me, srcs, pieces):
    n = len(pieces)

    def body(me_ref, *refs):
        for i in range(n):
            refs[n + i][...] = refs[i][...]

    def spec(block_and_index):
        block, index = block_and_index
        return pl.BlockSpec(block, lambda i, me_ref: index(me_ref[0]))

    return pl.pallas_call(
        body,
        grid_spec=pltpu.PrefetchScalarGridSpec(
            num_scalar_prefetch=1, grid=(1,), in_specs=[spec(p.own_src) for p in pieces],
            out_specs=[spec(p.own_slot) for p in pieces]),
        out_shape=[jax.ShapeDtypeStruct(p.land_shape, BF) for p in pieces],
        compiler_params=_cparams(("arbitrary",)), name=name,
    )(me, *[srcs[p.src] for p in pieces])


def _exchange_start(name, srcs, lands, pieces, groups):
    n_src, n, n_g = len(srcs), len(pieces), len(groups)

    def body(*refs):
        src_refs, land_refs = refs[:n_src], refs[n_src:n_src + n]
        sems, token = refs[n_src + n:n_src + n + 2 * n_g], refs[-1]
        x, y, c, me = _my_position()
        for g, idxs in enumerate(groups):
            for k in PEER_ORDER:
                peer, pid = _peer(x, y, c, k)
                for j, i in enumerate(idxs):
                    p = pieces[i]
                    pltpu.make_async_remote_copy(
                        src_ref=p.send(src_refs[p.src], pid), dst_ref=p.slot(land_refs[i], me),
                        send_sem=sems[2 * g].at[(k - 1) * len(idxs) + j], recv_sem=sems[2 * g + 1].at[(k - 1) * len(idxs) + j],
                        device_id=peer, device_id_type=MESH).start()
        token[...] = jnp.zeros(token.shape, F32)

    sem_shapes = [pltpu.SemaphoreType.DMA(((N_DEV - 1) * len(idxs),)) for idxs in groups for _ in range(2)]
    res = pl.pallas_call(
        body, in_specs=[HBM_SPEC] * (n_src + n),
        out_specs=[SEM_SPEC] * (2 * n_g) + [HBM_SPEC] * (n_src + n) + [pl.BlockSpec(memory_space=pltpu.VMEM)],
        out_shape=sem_shapes + _hbm_like(srcs) + _hbm_like(lands) + [jax.ShapeDtypeStruct((8, LANES), F32)],
        input_output_aliases={i: 2 * n_g + i for i in range(n_src + n)},
        compiler_params=pltpu.CompilerParams(has_side_effects=pltpu.SideEffectType.DATAFLOW_SIDE_EFFECTING),
        name=name,
    )(*_in_hbm(srcs), *_in_hbm(lands))
    sems = [(res[2 * g], res[2 * g + 1]) for g in range(n_g)]
    return sems, list(res[2 * n_g:2 * n_g + n_src]), list(res[2 * n_g + n_src:2 * n_g + n_src + n]), res[-1]


def _exchange_wait(name, srcs, lands, pieces, waits, after):
    n_src, n, n_g = len(srcs), len(lands), len(waits)

    def body(*refs):
        src_refs, land_refs = refs[:n_src], refs[n_src:n_src + n]
        sems = refs[n_src + n:n_src + n + 2 * n_g]
        x, y, c, me = _my_position()
        at = 0
        for g, (idxs, _, _) in enumerate(waits):
            for k in PEER_ORDER:
                peer, pid = _peer(x, y, c, k)
                for j, i in enumerate(idxs):
                    p = pieces[i]
                    cp = pltpu.make_async_remote_copy(
                        src_ref=p.send(src_refs[p.src], pid), dst_ref=p.slot(land_refs[at + j], pid),
                        send_sem=sems[2 * g].at[(k - 1) * len(idxs) + j], recv_sem=sems[2 * g + 1].at[(k - 1) * len(idxs) + j],
                        device_id=peer, device_id_type=MESH)
                    cp.wait_send()
                    cp.wait_recv()
            at += len(idxs)

    sem_args = [s for _, send, recv in waits for s in (send, recv)]
    res = pl.pallas_call(
        body, in_specs=[HBM_SPEC] * (n_src + n) + [SEM_SPEC] * (2 * n_g) + [ANY_SPEC],
        out_specs=[HBM_SPEC] * (n_src + n), out_shape=_hbm_like(srcs) + _hbm_like(lands),
        input_output_aliases={i: i for i in range(n_src + n)},
        compiler_params=pltpu.CompilerParams(has_side_effects=pltpu.SideEffectType.DATAFLOW_SIDE_EFFECTING),
        name=name,
    )(*srcs, *lands, *sem_args, after)
    return list(res[:n_src]), list(res[n_src:])


def _weight_pieces(p_in, p_out, p_up, p_down):
    depth, cin, d = p_in.shape
    rout, hs = p_out.shape[1], p_up.shape[2]
    pieces = []
    for l in range(depth):
        whole = functools.partial(lambda ref, dev, l: ref.at[l], l=l)
        layer = functools.partial(lambda dev, l: (l, 0, 0), l=l)

        def rows(src, n_rows, whole=whole, layer=layer):
            return _Piece(src, whole, lambda ref, dev: _rows_of(n_rows)(ref.at[0], dev), (1, N_DEV * n_rows, d),
                          ((None, n_rows, d), layer), ((None, n_rows, d), lambda dev: (0, dev, 0)))

        pieces += [
            rows(0, cin), rows(1, rout),
            _Piece(2, whole, lambda ref, dev: ref.at[0, dev], (1, N_DEV, d, hs),
                   ((None, d, hs), layer), ((None, None, d, hs), lambda dev: (0, dev, 0, 0))),
            rows(3, hs),
        ]
    return pieces


def _grad_pieces(g_pair, kinds):
    pieces = []
    for i, (g, kind) in enumerate(zip(g_pair, kinds)):
        lead = lambda dev: (dev, 0, 0)
        if kind == "up":
            blk = ((None,) + g.shape[1:], lead)
            pieces.append(_Piece(i, lambda ref, dev: ref.at[dev], lambda ref, dev: ref.at[dev], g.shape, blk, blk))
        else:
            rows, cols = g.shape[0] // N_DEV, g.shape[1]
            pieces.append(_Piece(i, _rows_of(rows), lambda ref, dev: ref.at[dev], (N_DEV, rows, cols),
                                 ((rows, cols), lambda dev: (dev, 0)), ((None, rows, cols), lead)))
    return pieces


SMALL_W = 1024


def _all_reduce_small(mix8, mlp8, final8, attn8, lb8_last, lb_jac, hg8, loss8, after):
    def body(mix0, mix1, mlp0, mlp1, fin, attn0, attn1, lb, jac, hg0, hg1, loss, after_ref, o_ref, src_ref, buf_ref,
             send_sems, recv_sems):
        def total(ref):
            return jnp.sum(ref[...], axis=0, keepdims=True)

        dlb = total(lb)
        hg = jnp.concatenate([total(hg0), total(hg1), total(loss)], axis=1)
        src_ref[...] = jnp.concatenate([
            total(mix0), total(mix1), total(mlp0), total(mlp1), total(fin),
            jnp.concatenate([total(attn0), total(attn1)], axis=1),
            jnp.concatenate([jac[0:1, :] * dlb, jac[1:2, :] * dlb], axis=1),
            jnp.concatenate([hg, jnp.zeros((1, SMALL_W - hg.shape[1]), F32)], axis=1)], axis=0)
        x, y, c, me = _my_position()
        buf_ref[me] = src_ref[...]
        sends = []
        for k in PEER_ORDER:
            peer, _ = _peer(x, y, c, k)
            cp = pltpu.make_async_remote_copy(src_ref=src_ref, dst_ref=buf_ref.at[me], send_sem=send_sems.at[k - 1],
                                              recv_sem=recv_sems.at[k - 1], device_id=peer, device_id_type=MESH)
            cp.start()
            sends.append(cp)
        for k in PEER_ORDER:
            peer, pid = _peer(x, y, c, k)
            pltpu.make_async_remote_copy(src_ref=src_ref, dst_ref=buf_ref.at[pid], send_sem=send_sems.at[k - 1],
                                         recv_sem=recv_sems.at[k - 1], device_id=peer, device_id_type=MESH).wait_recv()
        for cp in sends:
            cp.wait_send()
        acc = buf_ref[0]
        for dev in range(1, N_DEV):
            acc = acc + buf_ref[dev]
        o_ref[...] = acc

    assert mix8[0].shape[1] == SMALL_W
    vm = pl.BlockSpec(memory_space=pltpu.VMEM)
    return pl.pallas_call(
        body, in_specs=[vm] * 12 + [ANY_SPEC], out_specs=vm, out_shape=jax.ShapeDtypeStruct((8, SMALL_W), F32),
        scratch_shapes=[pltpu.VMEM((8, SMALL_W), F32), pltpu.VMEM((N_DEV, 8, SMALL_W), F32),
                        pltpu.SemaphoreType.DMA((N_DEV - 1,)), pltpu.SemaphoreType.DMA((N_DEV - 1,))],
        name="all_reduce_small",
    )(*mix8, *mlp8, final8, *attn8, lb8_last, lb_jac, *hg8, loss8, after)


def _resident(block_shape, index_map):
    return pl.BlockSpec(block_shape, index_map, pipeline_mode=pl.Buffered(1))


def _fwd_inproj(layer, x, gain, g_in, cos, sin, tm=512):
    s, d = x.shape

    def body(x_ref, gain_ref, w_ref, cos_ref, sin_ref, proj_ref, h_ref):
        h = _rms_fwd(x_ref[...], gain_ref[...]).astype(BF)
        h_ref[...] = h
        cs = _tile_lanes(cos_ref[...], SEG // LANES)
        sn = _tile_lanes(sin_ref[...], SEG // LANES)
        for seg in range(N_SEG):
            acc = _dot(h, w_ref[seg * SEG:(seg + 1) * SEG, :], NT)
            if seg < 2:
                acc = acc * cs + _rope_partner(acc) * sn
            if seg == 0:
                acc = acc * ATTN_SCALE
            proj_ref[:, seg * SEG:(seg + 1) * SEG] = acc

    return pl.pallas_call(
        body, grid=(s // tm,),
        in_specs=[pl.BlockSpec((tm, d), lambda i: (i, 0)), pl.BlockSpec((None, 1, d), lambda i: (layer, 0, 0)),
                  _resident((None, PROJ_W, d), lambda i: (0, 0, 0)),
                  pl.BlockSpec((tm, LANES), lambda i: (i, 0)), pl.BlockSpec((tm, LANES), lambda i: (i, 0))],
        out_specs=[pl.BlockSpec((tm, PROJ_W), lambda i: (i, 0)), pl.BlockSpec((tm, d), lambda i: (i, 0))],
        out_shape=[jax.ShapeDtypeStruct((s, PROJ_W), F32), jax.ShapeDtypeStruct((s, d), BF)],
        compiler_params=_cparams(("parallel",)), name=f"fwd_inproj_l{layer}",
    )(x, gain, g_in, cos, sin)


ATTN_UNIT = SPAN * max(DILATIONS)
ATTN_GROUP = 4


def _attn_masks(first_block_has_prev):
    row = lax.broadcasted_iota(jnp.int32, (SPAN, 2 * SPAN), 0)
    col = lax.broadcasted_iota(jnp.int32, (SPAN, 2 * SPAN), 1)
    band = (col >= row) & (col <= row + SPAN)
    lane = lax.broadcasted_iota(jnp.int32, (SPAN, LANES), 1)
    return band & ((col >= SPAN) | first_block_has_prev), band, lane < 64


def _attn_specs(n_in_extra, unit_of=lambda n: n):
    pairs = ATTN_W // LANES
    q_spec = pl.BlockSpec((ATTN_UNIT, LANES), lambda p, n: (unit_of(n), p))

    def prev(seg):
        return pl.BlockSpec((ATTN_UNIT, LANES), lambda p, n: (jnp.maximum(unit_of(n) - 1, 0), seg * pairs + p))

    def cur(seg):
        return pl.BlockSpec((ATTN_UNIT, LANES), lambda p, n: (unit_of(n), seg * pairs + p))

    return [q_spec, prev(1), cur(1), prev(2), cur(2)] + [q_spec] * n_in_extra


def _attn_groups(dil):
    blocks = ATTN_UNIT // (SPAN * dil)
    pairs = [(r, b) for r in range(dil) for b in range(blocks)]
    return [pairs[i:i + ATTN_GROUP] for i in range(0, len(pairs), ATTN_GROUP)]


def _block_rows(dil, r, b, n=1):
    start = r + dil * SPAN * b
    return pl.ds(start, n * SPAN, stride=dil) if dil > 1 else pl.ds(start, n * SPAN)


def _block_keys(prev_ref, cur_ref, dil, r, b):
    if b > 0:
        return cur_ref[_block_rows(dil, r, b - 1, 2), :]
    last = ATTN_UNIT // (SPAN * dil) - 1
    return jnp.concatenate([prev_ref[_block_rows(dil, r, last), :], cur_ref[_block_rows(dil, r, 0), :]], axis=0)


def _attn_fwd(layer, proj):
    s = proj.shape[0]
    n_pat = len(DILATIONS)
    merge_rows = 256

    def body(q_ref, kp_ref, kc_ref, vp_ref, vc_ref, o_ref, lse_ref, o_scr, lse_scr):
        m_first, m_rest, is_a = _attn_masks(pl.program_id(1) > 0)
        sels = (is_a, jnp.logical_not(is_a))
        is_a_keys = lax.broadcasted_iota(jnp.int32, (2 * SPAN, LANES), 1) < 64
        for pi, dil in enumerate(DILATIONS):
            for group in _attn_groups(dil):
                items = [(r, b, h) for r, b in group for h in range(2)]
                q = {rb: q_ref[_block_rows(dil, *rb), :] for rb in group}
                k = {rb: _block_keys(kp_ref, kc_ref, dil, *rb).astype(BF) for rb in group}
                v = {rb: _block_keys(vp_ref, vc_ref, dil, *rb).astype(BF) for rb in group}
                v_sum = {rb: (jnp.where(is_a_keys, v[rb], 1.0), jnp.where(is_a_keys, 1.0, v[rb])) for rb in group}
                sc = [jnp.where(m_first if b == 0 else m_rest,
                                _dot(jnp.where(sels[h], q[r, b], 0.0).astype(BF), k[r, b], NT), MASK_VALUE)
                      for r, b, h in items]
                mx = [jnp.max(jnp.maximum(t[:, :SPAN], t[:, SPAN:]), axis=-1, keepdims=True) for t in sc]
                p = [jnp.exp(t - m).astype(BF) for t, m in zip(sc, mx)]
                both = [_dot(t, v_sum[r, b][h], NN) for t, (r, b, h) in zip(p, items)]
                for j, (r, b) in enumerate(group):
                    t_a, t_b = both[2 * j], both[2 * j + 1]
                    den = pltpu.roll(jnp.where(is_a, t_b, t_a), 64, 1)
                    o_scr[pi, _block_rows(dil, r, b), :] = jnp.where(is_a, t_a, t_b) / den
                    lse_scr[pi, _block_rows(dil, r, b), :] = jnp.where(is_a, mx[2 * j], mx[2 * j + 1]) + jnp.log(den)
        for c in range(ATTN_UNIT // merge_rows):
            rows = slice(c * merge_rows, (c + 1) * merge_rows)
            ls = [lse_scr[pi, rows, :] for pi in range(n_pat)]
            mx = functools.reduce(jnp.maximum, ls)
            ws = [jnp.exp(l - mx) for l in ls]
            den = functools.reduce(jnp.add, ws)
            o_ref[rows, :] = functools.reduce(jnp.add, [w * o_scr[pi, rows, :] for pi, w in enumerate(ws)]) / den
            lse_ref[rows, :] = mx + jnp.log(den)

    out_spec = pl.BlockSpec((ATTN_UNIT, LANES), lambda p, n: (n, p))
    return pl.pallas_call(
        body, grid=(ATTN_W // LANES, s // ATTN_UNIT), in_specs=_attn_specs(0), out_specs=[out_spec, out_spec],
        out_shape=[jax.ShapeDtypeStruct((s, ATTN_W), F32)] * 2,
        scratch_shapes=[pltpu.VMEM((n_pat, ATTN_UNIT, LANES), F32)] * 2,
        compiler_params=_cparams(("parallel", "arbitrary")), name=f"attn_fwd_l{layer}",
    )(proj, proj, proj, proj, proj)


def _attn_norm(layer, o, gain, mixed, tm=512):
    s = o.shape[0]

    def body(o_ref, gain_ref, mixed_ref, n_ref):
        n_ref[...] = _rms_fwd(o_ref[...], gain_ref[...]).astype(BF)

    blk = pl.BlockSpec((tm, ATTN_W), lambda i: (i, 0))
    return pl.pallas_call(
        body, grid=(s // tm,),
        in_specs=[blk, pl.BlockSpec((None, 1, ATTN_W), lambda i: (layer, 0, 0)), pl.BlockSpec(memory_space=pl.ANY)],
        out_specs=blk, out_shape=jax.ShapeDtypeStruct(mixed.shape, BF), input_output_aliases={2: 0},
        compiler_params=_cparams(("parallel",)), name=f"attn_norm_l{layer}",
    )(o, gain, mixed)


def _chunk_cumsum(x, reverse=False):
    n = x.shape[0]
    pos = lax.broadcasted_iota(jnp.int32, x.shape, 0) % HGRN_CHUNK
    for sh in (1, 2, 4, 8):
        if reverse:
            x = x + jnp.where(pos < HGRN_CHUNK - sh, pltpu.roll(x, n - sh, 0), 0.0)
        else:
            x = x + jnp.where(pos >= sh, pltpu.roll(x, sh, 0), 0.0)
    return x


def _chunk_row(x, row):
    r, n = x.shape
    x3 = x.reshape(r // HGRN_CHUNK, HGRN_CHUNK, n)
    return jnp.broadcast_to(x3[:, row:row + 1, :], x3.shape).reshape(r, n)


def _hgrn_pre(qh, z, lb):
    sig = _sigmoid(z)
    f = lb + (1.0 - lb) * sig
    k = 1.0 - f
    sq = _sigmoid(qh)
    q = qh * sq * HGRN_SCALE
    g = _chunk_cumsum(jnp.log(f))
    g_mid = _chunk_row(g, HGRN_CHUNK // 2 - 1)
    g_last = _chunk_row(g, HGRN_CHUNK - 1)
    e_q, e_k = jnp.exp(g - g_mid), jnp.exp(g_mid - g)
    e_in, e_out = jnp.exp(g), jnp.exp(g_last - g)
    return dict(sig=sig, f=f, k=k, sq=sq, q=q, g_last=g_last, e_q=e_q, e_k=e_k, e_in=e_in, e_out=e_out,
                qt=q * e_q, kt=k * e_k, qg=q * e_in, kout=k * e_out)


def _hgrn_mask():
    row = lax.broadcasted_iota(jnp.int32, (LANES, LANES), 0)
    col = lax.broadcasted_iota(jnp.int32, (LANES, LANES), 1)
    return (row // HGRN_CHUNK == col // HGRN_CHUNK) & (col <= row)


def _hgrn_in_specs(layer, rev, nblk):
    def blk(b):
        return nblk - 1 - b if rev else b
    first = 3 * ATTN_W // HGRN_W
    specs = [pl.BlockSpec((HGRN_ROWS, HGRN_W), functools.partial(lambda b, seg: (blk(b), first + seg), seg=seg))
             for seg in range(4)]
    specs.append(pl.BlockSpec((None, 1, HGRN_W), lambda b: (layer, 0, 0)))
    specs.append(pl.BlockSpec((None, 1, HGRN_DIM), lambda b: (layer, 0, 0)))
    return specs, blk


def _head(x, h):
    return x[:, h * HGRN_DIM:(h + 1) * HGRN_DIM]


def _chunk(x, c):
    return x[c * HGRN_CHUNK:(c + 1) * HGRN_CHUNK]


def _sub(x, sb):
    return x[sb * LANES:(sb + 1) * LANES]


HGRN_ROWS = 256
HEADS = range(HGRN_HEADS)
SUBS = range(HGRN_ROWS // LANES)
CHUNKS = range(HGRN_ROWS // HGRN_CHUNK)


def _hgrn_fwd(layer, proj, lb, gain):
    s = proj.shape[0]
    nblk = s // HGRN_ROWS
    cpb = len(CHUNKS)

    def body(q_ref, f_ref, i_ref, g_ref, lb_ref, gain_ref, o_ref, rec_ref, st_ref, state):
        @pl.when(pl.program_id(0) == 0)
        def _():
            state[...] = jnp.zeros(state.shape, F32)

        pre = _hgrn_pre(q_ref[...], f_ref[...], lb_ref[...])
        v = i_ref[...].astype(BF)
        qt, kt, qg, kout = (pre[n].astype(BF) for n in ("qt", "kt", "qg", "kout"))
        dec = jnp.exp(pre["g_last"])
        mask = _hgrn_mask()
        a = [[jnp.where(mask, _dot(_sub(_head(qt, h), sb), _sub(_head(kt, h), sb), NT), 0.0).astype(BF) for sb in SUBS]
             for h in HEADS]
        o_intra = [[_dot(a[h][sb], _sub(_head(v, h), sb), NN) for sb in SUBS] for h in HEADS]
        update = [[_dot(_chunk(_head(v, h), c), _chunk(_head(kout, h), c), TN) for c in CHUNKS] for h in HEADS]
        for h in HEADS:
            st = state[h]
            for c in CHUNKS:
                st_ref[h, c * LANES:(c + 1) * LANES, :] = st.astype(BF)
                st = st * _head(dec, h)[c * HGRN_CHUNK:c * HGRN_CHUNK + 1, :] + update[h][c]
            state[h] = st
        inter = [[_dot(_chunk(_head(qg, h), c), st_ref[h, c * LANES:(c + 1) * LANES, :].astype(BF), NT) for c in CHUNKS]
                 for h in HEADS]
        o = [jnp.concatenate(o_intra[h], axis=0) + jnp.concatenate(inter[h], axis=0) for h in HEADS]
        o_ref[...] = jnp.concatenate(o, axis=1)
        gate = g_ref[...]
        normed = jnp.concatenate([_rms_fwd(o[h], gain_ref[...]) for h in HEADS], axis=1)
        rec_ref[...] = (normed * (gate * _sigmoid(gate))).astype(BF)

    specs, _ = _hgrn_in_specs(layer, False, nblk)
    return pl.pallas_call(
        body, grid=(nblk,), in_specs=specs,
        out_specs=[pl.BlockSpec((HGRN_ROWS, HGRN_W), lambda b: (b, 0)), pl.BlockSpec((HGRN_ROWS, HGRN_W), lambda b: (b, 1)),
                   pl.BlockSpec((HGRN_HEADS, cpb * LANES, LANES), lambda b: (0, b, 0))],
        out_shape=[jax.ShapeDtypeStruct((s, HGRN_W), F32), jax.ShapeDtypeStruct((s, MIX_W), BF),
                   jax.ShapeDtypeStruct((HGRN_HEADS, nblk * cpb * LANES, LANES), BF)],
        scratch_shapes=[pltpu.VMEM((HGRN_HEADS, LANES, LANES), F32)],
        compiler_params=_cparams(("arbitrary",)), name=f"hgrn_fwd_l{layer}",
    )(proj, proj, proj, proj, lb, gain)


def _relu2(u):
    return jnp.square(jnp.maximum(u, 0)).astype(BF)


def _mlp_fwd(layer, x, mixed, gain, g_out, g_up, g_down, head=None):
    s, d = x.shape
    mw = mixed.shape[1]
    nblk, hs = g_up.shape[1], g_up.shape[3]
    tm = 512

    def body(x_ref, m_ref, gain_ref, out_w_ref, up_ref, down_ref, *refs):
        if head:
            fin_ref, t_ref, o_ref, mid_ref, u_ref, h_ref, ob_ref, dfin_ref, loss_ref, a_buf = refs
        else:
            o_ref, mid_ref, u_ref, h_ref, a_buf = refs
        xv = x_ref[...] + _dot(m_ref[...], out_w_ref[...], NN)
        mid_ref[...] = xv
        h = _rms_fwd(xv, gain_ref[...]).astype(BF)
        h_ref[...] = h
        for j in range(nblk):
            u = _dot(h, up_ref[j], NN)
            u_ref[:, j * hs:(j + 1) * hs] = u.astype(BF)
            a_buf[:, j * hs:(j + 1) * hs] = _relu2(u)
        acc = xv
        for j in range(nblk):
            acc = acc + _dot(a_buf[:, j * hs:(j + 1) * hs], down_ref[j * hs:(j + 1) * hs, :], NN)
        if not head:
            o_ref[...] = acc
            return
        fin = fin_ref[...]
        err = _rms_fwd(acc, fin) - t_ref[...]
        dout, dfin = _rms_bwd(err * (1.0 / d), acc, fin)
        o_ref[...] = dout
        ob_ref[...] = dout.astype(BF)
        step = pl.program_id(0)
        _accumulate_rows(step, dfin_ref, _part8(dfin))
        _accumulate_rows(step, loss_ref, _part8(0.5 * jnp.mean(err * err, axis=-1, keepdims=True) * jnp.ones((1, LANES), F32)))

    row = pl.BlockSpec((tm, d), lambda i: (i, 0))
    in_specs = [row, pl.BlockSpec((tm, mw), lambda i: (i, 0)), pl.BlockSpec((None, 1, d), lambda i: (layer, 0, 0)),
                _resident((None, mw, d), lambda i: (0, 0, 0)),
                _resident((None, nblk, d, hs), lambda i: (0, 0, 0, 0)),
                _resident((None, nblk * hs, d), lambda i: (0, 0, 0))]
    out_specs = [row, row, pl.BlockSpec((tm, nblk * hs), lambda i: (i, 0)), row]
    out_shape = [jax.ShapeDtypeStruct((s, d), F32), jax.ShapeDtypeStruct((s, d), F32),
                 jax.ShapeDtypeStruct((s, nblk * hs), BF), jax.ShapeDtypeStruct((s, d), BF)]
    args = [x, mixed, gain, g_out, g_up, g_down]
    if head:
        in_specs += [pl.BlockSpec((1, d), lambda i: (0, 0)), row]
        out_specs += [row, pl.BlockSpec((8, d), lambda i: (0, 0)), pl.BlockSpec((8, LANES), lambda i: (0, 0))]
        out_shape += [jax.ShapeDtypeStruct((s, d), BF), jax.ShapeDtypeStruct((8, d), F32), jax.ShapeDtypeStruct((8, LANES), F32)]
        args += list(head)
    return pl.pallas_call(
        body, grid=(s // tm,), in_specs=in_specs, out_specs=out_specs, out_shape=out_shape,
        scratch_shapes=[pltpu.VMEM((tm, nblk * hs), BF)],
        compiler_params=_cparams(("arbitrary",) if head else ("parallel",)), name=f"mlp_fwd_l{layer}",
    )(*args)


def _accumulate_rows(i, ref, part):
    @pl.when(i == 0)
    def _():
        ref[...] = part

    @pl.when(i > 0)
    def _():
        ref[...] += part


def _head_sums(prod):
    row = lax.broadcasted_iota(jnp.int32, (ATTN_W, ATTN_W), 0)
    col = lax.broadcasted_iota(jnp.int32, (ATTN_W, ATTN_W), 1)
    same_head = jnp.where(row // 64 == col // 64, 1.0, 0.0).astype(BF)
    high = prod.astype(BF)
    low = (prod - high.astype(F32)).astype(BF)
    return _dot(high, same_head, NN) + _dot(low, same_head, NN)


def _mlp_bwd(layer, dx, dxb, x, gain, u, o_attn, gain_attn, g_out, g_up, g_down, tm=256):
    s, d = x.shape
    mw = g_out.shape[1]
    nblk, hs = g_up.shape[1], g_up.shape[3]

    def body(dx_ref, dxb_ref, x_ref, gain_ref, u_ref, oa_ref, ga_ref, out_w_ref, up_ref, down_ref, o_ref, ob_ref, du_ref,
             drec_ref, do_ref, delta_ref, dgain_ref, dattn_ref):
        dxb_v = dxb_ref[...]
        for j in range(nblk):
            cols = slice(j * hs, (j + 1) * hs)
            da = _dot(dxb_v, down_ref[cols, :], NT)
            du_ref[:, cols] = (da * (2.0 * jnp.maximum(u_ref[:, cols].astype(F32), 0.0))).astype(BF)
        acc = jnp.zeros((tm, d), F32)
        for j in range(nblk):
            acc = acc + _dot(du_ref[:, j * hs:(j + 1) * hs], up_ref[j], NT)
        dxn, dgain = _rms_bwd(acc, x_ref[...], gain_ref[...])
        out = dx_ref[...] + dxn
        out_b = out.astype(BF)
        o_ref[...] = out
        ob_ref[...] = out_b
        dm = _dot(out_b, out_w_ref[...], NT)
        drec_ref[...] = dm[:, ATTN_W:]
        ov = oa_ref[...]
        do, dattn = _rms_bwd(dm[:, :ATTN_W], ov, ga_ref[...])
        do_ref[...] = do
        delta_ref[...] = _head_sums(do * ov)
        _accumulate_rows(pl.program_id(0), dgain_ref, _part8(dgain))
        _accumulate_rows(pl.program_id(0), dattn_ref, _part8(dattn))

    row = pl.BlockSpec((tm, d), lambda i: (i, 0))
    wide = pl.BlockSpec((tm, nblk * hs), lambda i: (i, 0))
    half = pl.BlockSpec((tm, ATTN_W), lambda i: (i, 0))
    return pl.pallas_call(
        body, grid=(s // tm,),
        in_specs=[row, row, row, pl.BlockSpec((None, 1, d), lambda i: (layer, 0, 0)), wide,
                  half, pl.BlockSpec((None, 1, ATTN_W), lambda i: (layer, 0, 0)),
                  _resident((None, mw, d), lambda i: (0, 0, 0)),
                  _resident((None, nblk, d, hs), lambda i: (0, 0, 0, 0)),
                  _resident((None, nblk * hs, d), lambda i: (0, 0, 0))],
        out_specs=[row, row, wide, half, half, half, pl.BlockSpec((8, d), lambda i: (0, 0)),
                   pl.BlockSpec((8, ATTN_W), lambda i: (0, 0))],
        out_shape=[jax.ShapeDtypeStruct((s, d), F32), jax.ShapeDtypeStruct((s, d), BF),
                   jax.ShapeDtypeStruct((s, nblk * hs), BF)] + [jax.ShapeDtypeStruct((s, ATTN_W), F32)] * 3
        + [jax.ShapeDtypeStruct((8, d), F32), jax.ShapeDtypeStruct((8, ATTN_W), F32)],
        compiler_params=_cparams(("arbitrary",)), name=f"mlp_bwd_l{layer}",
    )(dx, dxb, x, gain, u, o_attn, gain_attn, g_out, g_up, g_down)


def _attn_bwd(layer, proj, do, lse, delta, cos, sin, after):
    s = proj.shape[0]
    n_units = s // ATTN_UNIT
    out_rows = 256

    def unit_of(n):
        return n_units - 1 - n

    def body(q_ref, kp_ref, kc_ref, vp_ref, vc_ref, do_ref, lse_ref, delta_ref, cos_ref, sin_ref, after_ref, out_ref,
             dq_ref, dk_ref, dkp_ref, dv_ref, dvp_ref, carry_k, carry_v):
        step = pl.program_id(1)
        m_first, m_rest, is_a = _attn_masks(unit_of(step) > 0)
        sels = (is_a, jnp.logical_not(is_a))
        for ref in (dq_ref, dk_ref, dkp_ref, dv_ref, dvp_ref):
            ref[...] = jnp.zeros(ref.shape, F32)
        for dil in DILATIONS:
            last = ATTN_UNIT // (SPAN * dil) - 1
            for group in _attn_groups(dil):
                items = [(r, b, h) for r, b in group for h in range(2)]
                q = {rb: q_ref[_block_rows(dil, *rb), :] for rb in group}
                dov = {rb: do_ref[_block_rows(dil, *rb), :] for rb in group}
                lse_v = {rb: lse_ref[_block_rows(dil, *rb), :] for rb in group}
                delta_v = {rb: delta_ref[_block_rows(dil, *rb), :] for rb in group}
                k = {rb: _block_keys(kp_ref, kc_ref, dil, *rb).astype(BF) for rb in group}
                v = {rb: _block_keys(vp_ref, vc_ref, dil, *rb).astype(BF) for rb in group}
                qh = [jnp.where(sels[h], q[r, b], 0.0).astype(BF) for r, b, h in items]
                doh = [jnp.where(sels[h], dov[r, b], 0.0).astype(BF) for r, b, h in items]
                sc = [jnp.where(m_first if b == 0 else m_rest, _dot(qh[i], k[r, b], NT), MASK_VALUE)
                      for i, (r, b, h) in enumerate(items)]
                p = [jnp.exp(sc[i] - lse_v[r, b][:, 64 * h:64 * h + 1]) for i, (r, b, h) in enumerate(items)]
                ds = [(p[i] * (_dot(doh[i], v[r, b], NT) - delta_v[r, b][:, 64 * h:64 * h + 1])).astype(BF)
                      for i, (r, b, h) in enumerate(items)]
                dv = [_dot(p[i].astype(BF), doh[i], TN) for i in range(len(items))]
                dq = [_dot(ds[i], k[r, b], NN) for i, (r, b, h) in enumerate(items)]
                dk = [_dot(ds[i], qh[i], TN) for i in range(len(items))]
                for j, (r, b) in enumerate(group):
                    own = _block_rows(dil, r, b)
                    dq_ref[own, :] += jnp.where(is_a, dq[2 * j], dq[2 * j + 1])
                    dk2, dv2 = dk[2 * j] + dk[2 * j + 1], dv[2 * j] + dv[2 * j + 1]
                    dk_ref[own, :] += dk2[SPAN:]
                    dv_ref[own, :] += dv2[SPAN:]
                    if b > 0:
                        before = _block_rows(dil, r, b - 1)
                        dk_ref[before, :] += dk2[:SPAN]
                        dv_ref[before, :] += dv2[:SPAN]
                    else:
                        before = _block_rows(dil, r, last)
                        dkp_ref[before, :] += dk2[:SPAN]
                        dvp_ref[before, :] += dv2[:SPAN]
        has_next = step > 0
        for c in range(ATTN_UNIT // out_rows):
            rows = slice(c * out_rows, (c + 1) * out_rows)
            cs, sn = cos_ref[rows, :], sin_ref[rows, :]
            dqv = dq_ref[rows, :]
            dkv = dk_ref[rows, :] + jnp.where(has_next, carry_k[rows, :], 0.0)
            dvv = dv_ref[rows, :] + jnp.where(has_next, carry_v[rows, :], 0.0)
            out_ref[0, rows, :] = ((dqv * cs - _rope_partner(dqv) * sn) * ATTN_SCALE).astype(BF)
            out_ref[1, rows, :] = (dkv * cs - _rope_partner(dkv) * sn).astype(BF)
            out_ref[2, rows, :] = dvv.astype(BF)
        carry_k[...] = dkp_ref[...]
        carry_v[...] = dvp_ref[...]

    tab = pl.BlockSpec((ATTN_UNIT, LANES), lambda p, n: (unit_of(n), 0))
    return pl.pallas_call(
        body, grid=(ATTN_W // LANES, n_units), in_specs=_attn_specs(3, unit_of) + [tab, tab, ANY_SPEC],
        out_specs=pl.BlockSpec((3, ATTN_UNIT, LANES), lambda p, n: (0, unit_of(n), p)),
        out_shape=jax.ShapeDtypeStruct((3, s, ATTN_W), BF),
        scratch_shapes=[pltpu.VMEM((ATTN_UNIT, LANES), F32)] * 7,
        compiler_params=_cparams(("parallel", "arbitrary")), name=f"attn_bwd_l{layer}",
    )(proj, proj, proj, proj, proj, do, lse, delta, cos, sin, after)


def _hgrn_bwd(layer, proj, lb, gain, o, drec, states):
    s = proj.shape[0]
    nblk = s // HGRN_ROWS
    cpb = len(CHUNKS)

    def body(q_ref, f_ref, i_ref, g_ref, lb_ref, gain_ref, o_ref, drec_ref, st_ref, dseg_ref, dlb_ref, dgain_ref,
             dstate, dst_buf):
        step = pl.program_id(0)

        @pl.when(step == 0)
        def _():
            dstate[...] = jnp.zeros(dstate.shape, F32)

        lbv, gv = lb_ref[...], gain_ref[...]
        qh, z, gate_in = q_ref[...], f_ref[...], g_ref[...]
        pre = _hgrn_pre(qh, z, lbv)
        v = i_ref[...].astype(BF)
        sg = _sigmoid(gate_in)
        ov, drec = o_ref[...], drec_ref[...]
        dnormed = drec * (gate_in * sg)
        back = [_rms_bwd(_head(dnormed, h), _head(ov, h), gv) for h in HEADS]
        do_b = jnp.concatenate([b[0] for b in back], axis=1).astype(BF)
        dgain = back[0][1] + back[1][1] + back[2][1] + back[3][1]
        normed = jnp.concatenate([_rms_fwd(_head(ov, h), gv) for h in HEADS], axis=1)
        dgate_in = drec * normed * (sg * (1.0 + gate_in * (1.0 - sg)))
        mask = _hgrn_mask()
        qt, kt, qg, kout = (pre[n].astype(BF) for n in ("qt", "kt", "qg", "kout"))
        dec = jnp.exp(pre["g_last"])
        def intra(fn):
            return jnp.concatenate([jnp.concatenate([fn(h, sb) for sb in SUBS], axis=0) for h in HEADS], axis=1)

        def hs(x, h, sb):
            return _sub(_head(x, h), sb)

        a = [[jnp.where(mask, _dot(hs(qt, h, sb), hs(kt, h, sb), NT), 0.0).astype(BF) for sb in SUBS] for h in HEADS]
        da = [[jnp.where(mask, _dot(hs(do_b, h, sb), hs(v, h, sb), NT), 0.0).astype(BF) for sb in SUBS] for h in HEADS]
        dv_intra = intra(lambda h, sb: _dot(a[h][sb], hs(do_b, h, sb), TN))
        dqt = intra(lambda h, sb: _dot(da[h][sb], hs(kt, h, sb), NN))
        dkt = intra(lambda h, sb: _dot(da[h][sb], hs(qt, h, sb), TN))
        feed = [[_dot(_chunk(_head(do_b, h), c), _chunk(_head(qg, h), c), TN) for c in CHUNKS] for h in HEADS]
        for h in HEADS:
            dst = dstate[h]
            for c in reversed(CHUNKS):
                dst_buf[h, c * LANES:(c + 1) * LANES, :] = dst
                dst = dst * _head(dec, h)[c * HGRN_CHUNK:c * HGRN_CHUNK + 1, :] + feed[h][c]
            dstate[h] = dst

        def per_chunk(fn):
            cols = []
            for h in HEADS:
                rows = [jnp.broadcast_to(t, (HGRN_CHUNK, HGRN_DIM)) for t in (fn(h, c) for c in CHUNKS)]
                cols.append(jnp.concatenate(rows, axis=0))
            return jnp.concatenate(cols, axis=1)

        def st_prev(h, c):
            return st_ref[h, c * LANES:(c + 1) * LANES, :]

        def dst_at(h, c):
            return dst_buf[h, c * LANES:(c + 1) * LANES, :]

        dqg = per_chunk(lambda h, c: _dot(_chunk(_head(do_b, h), c), st_prev(h, c).astype(BF), NN))
        dkout = per_chunk(lambda h, c: _dot(_chunk(_head(v, h), c), dst_at(h, c).astype(BF), NN))
        dv_inter = per_chunk(lambda h, c: _dot(_chunk(_head(kout, h), c), dst_at(h, c).astype(BF), NT))
        dg_state = per_chunk(lambda h, c: jnp.sum(dst_at(h, c) * st_prev(h, c).astype(F32), axis=0, keepdims=True))
        dg_kout = per_chunk(lambda h, c: jnp.sum(_chunk(_head(dkout * pre["kout"], h), c), axis=0, keepdims=True))
        dv = dv_intra + dv_inter
        pos = lax.broadcasted_iota(jnp.int32, (HGRN_ROWS, HGRN_W), 0) % HGRN_CHUNK
        dq = dqt * pre["e_q"] + dqg * pre["e_in"]
        dk = dkt * pre["e_k"] + dkout * pre["e_out"]
        dg = (dqt * pre["qt"] - dkt * pre["kt"] + dqg * pre["qg"] - dkout * pre["kout"]
              + jnp.where(pos == HGRN_CHUNK - 1, dg_state * dec + dg_kout, 0.0))
        dlogf = _chunk_cumsum(dg, reverse=True)
        sig, sq = pre["sig"], pre["sq"]
        df = dlogf / pre["f"] - dk
        dseg_ref[0] = (dq * HGRN_SCALE * (sq * (1.0 + qh * (1.0 - sq)))).astype(BF)
        dseg_ref[1] = (df * (1.0 - lbv) * sig * (1.0 - sig)).astype(BF)
        dseg_ref[2] = dv.astype(BF)
        dseg_ref[3] = dgate_in.astype(BF)
        _accumulate_rows(step, dlb_ref, _part8(df * (1.0 - sig)))
        _accumulate_rows(step, dgain_ref, _part8(dgain))

    specs, blk = _hgrn_in_specs(layer, True, nblk)
    specs += [pl.BlockSpec((HGRN_ROWS, HGRN_W), lambda b: (blk(b), 0)),
              pl.BlockSpec((HGRN_ROWS, HGRN_W), lambda b: (blk(b), 0)),
              pl.BlockSpec((HGRN_HEADS, cpb * LANES, LANES), lambda b: (0, blk(b), 0))]
    return pl.pallas_call(
        body, grid=(nblk,), in_specs=specs,
        out_specs=[pl.BlockSpec((4, HGRN_ROWS, HGRN_W), lambda b: (0, blk(b), 0)),
                   pl.BlockSpec((8, HGRN_W), lambda b: (0, 0)), pl.BlockSpec((8, HGRN_DIM), lambda b: (0, 0))],
        out_shape=[jax.ShapeDtypeStruct((4, s, HGRN_W), BF), jax.ShapeDtypeStruct((8, HGRN_W), F32),
                   jax.ShapeDtypeStruct((8, HGRN_DIM), F32)],
        scratch_shapes=[pltpu.VMEM((HGRN_HEADS, LANES, LANES), F32), pltpu.VMEM((HGRN_HEADS, cpb * LANES, LANES), F32)],
        compiler_params=_cparams(("arbitrary",)), name=f"hgrn_bwd_l{layer}",
    )(proj, proj, proj, proj, lb, gain, o, drec, states)


def _bwd_inproj(layer, dqkv, dhg, g_in, x, gain, dres, tm=512):
    s, d = x.shape

    def body(dqkv_ref, dhg_ref, w_ref, x_ref, gain_ref, dres_ref, dx_ref, dxb_ref, dgain_ref):
        acc = jnp.zeros((tm, d), F32)
        for seg in range(N_SEG):
            a = dqkv_ref[seg] if seg < 3 else dhg_ref[seg - 3]
            acc = acc + _dot(a, w_ref[seg * SEG:(seg + 1) * SEG, :], NN)
        dx, dgain = _rms_bwd(acc, x_ref[...], gain_ref[...])
        out = dres_ref[...] + dx
        dx_ref[...] = out
        dxb_ref[...] = out.astype(BF)
        _accumulate_rows(pl.program_id(0), dgain_ref, _part8(dgain))

    row = pl.BlockSpec((tm, d), lambda i: (i, 0))
    return pl.pallas_call(
        body, grid=(s // tm,),
        in_specs=[pl.BlockSpec((3, tm, SEG), lambda i: (0, i, 0)), pl.BlockSpec((4, tm, SEG), lambda i: (0, i, 0)),
                  _resident((None, PROJ_W, d), lambda i: (0, 0, 0)), row,
                  pl.BlockSpec((None, 1, d), lambda i: (layer, 0, 0)), row],
        out_specs=[row, row, pl.BlockSpec((8, d), lambda i: (0, 0))],
        out_shape=[jax.ShapeDtypeStruct((s, d), F32), jax.ShapeDtypeStruct((s, d), BF), jax.ShapeDtypeStruct((8, d), F32)],
        compiler_params=_cparams(("arbitrary",)), name=f"bwd_inproj_l{layer}",
    )(dqkv, dhg, g_in, x, gain, dres)


def _adamw(w, g, m, v):
    m2 = ADAM_B1 * m + (1.0 - ADAM_B1) * g
    v2 = ADAM_B2 * v + (1.0 - ADAM_B2) * (g * g)
    m_hat = m2 / (1.0 - ADAM_B1 ** ADAM_STEP)
    v_hat = v2 / (1.0 - ADAM_B2 ** ADAM_STEP)
    delta = -ADAM_LR * (m_hat / (jnp.sqrt(v_hat) + ADAM_EPS) + ADAM_WD * w)
    return delta, m2, v2


def _adam_big(name, parts, w, m, v, row_tiles):
    depth = w.shape[0]
    r, c = parts[0].shape[1], parts[0].shape[2]
    tr = r // row_tiles
    p_spec = pl.BlockSpec((N_DEV, tr, c), lambda t: (0, t, 0))
    w_spec = pl.BlockSpec((depth, tr, c), lambda t: (0, t, 0))

    def body(*refs):
        p_refs = refs[:depth]
        w_ref, m_ref, v_ref, g_ref, d_ref, m2_ref, v2_ref, token = refs[depth:]
        token[...] = jnp.zeros(token.shape, F32)
        for l in range(depth):
            g = p_refs[l][0].astype(F32)
            for dev in range(1, N_DEV):
                g = g + p_refs[l][dev].astype(F32)
            delta, m2, v2 = _adamw(w_ref[l], g, m_ref[l], v_ref[l])
            g_ref[l] = g
            d_ref[l] = delta
            m2_ref[l] = m2
            v2_ref[l] = v2

    return pl.pallas_call(
        body, grid=(row_tiles,), in_specs=[p_spec] * depth + [w_spec] * 3,
        out_specs=[w_spec] * 4 + [pl.BlockSpec((8, LANES), lambda t: (0, 0))],
        out_shape=[jax.ShapeDtypeStruct(w.shape, F32)] * 4 + [jax.ShapeDtypeStruct((8, LANES), F32)],
        compiler_params=_cparams(("arbitrary",)), name=name,
    )(*parts, w, m, v)


def _adam_small(g, ws, ms, vs):
    n = len(ws)

    def split(row, width):
        return jnp.concatenate([row[:, :width], row[:, width:2 * width]], axis=0)

    def body(g_ref, *refs):
        ins, outs = refs[:3 * n], refs[3 * n:]
        grads = [g_ref[0:2, :], split(g_ref[5:6, :], ATTN_W), split(g_ref[6:7, :], HGRN_W), split(g_ref[7:8, :], HGRN_DIM),
                 g_ref[2:4, :], g_ref[4:5, :]]
        for i, g_i in enumerate(grads):
            delta, m2, v2 = _adamw(ins[i][...], g_i, ins[n + i][...], ins[2 * n + i][...])
            for j, val in enumerate((g_i, delta, m2, v2)):
                outs[4 * i + j][...] = val

    vm = pl.BlockSpec(memory_space=pltpu.VMEM)
    res = pl.pallas_call(
        body, in_specs=[vm] * (1 + 3 * n), out_specs=[vm] * (4 * n),
        out_shape=[jax.ShapeDtypeStruct(w.shape, F32) for w in ws for _ in range(4)], name="adam_small",
    )(g, *ws, *ms, *vs)
    return [res[4 * i:4 * i + 4] for i in range(n)]


def _lower_bounds(logits):
    def body(l_ref, lb_ref, jac_ref):
        l0, l1 = l_ref[0:1, :], l_ref[1:2, :]
        mx = jnp.maximum(l0, l1)
        e0, e1 = jnp.exp(l0 - mx), jnp.exp(l1 - mx)
        p0, p1 = e0 / (e0 + e1), e1 / (e0 + e1)
        lb_ref[0:1, :] = p0 - p0
        lb_ref[1:2, :] = (p0 + p1) - p0
        jac_ref[0:1, :] = -p0 * p1
        jac_ref[1:2, :] = p0 * p1

    vm = pl.BlockSpec(memory_space=pltpu.VMEM)
    return pl.pallas_call(body, in_specs=[vm], out_specs=[vm, vm], out_shape=[jax.ShapeDtypeStruct(logits.shape, F32)] * 2,
                          name="hgrn_lower_bounds")(logits)


def _rope_tables(s, after):
    half = 32
    inv_freq = ROPE_THETA ** (-jnp.arange(half, dtype=F32) / half)
    ang = (jnp.arange(s, dtype=jnp.int32).astype(F32) + after[0, 0])[:, None] * inv_freq[None, :]
    cos, sin = jnp.cos(ang), jnp.sin(ang)
    return jnp.concatenate([cos] * 4, axis=1), jnp.concatenate([-sin, sin, -sin, sin], axis=1)


def kernel(x, norm_mix, w_in, attn_out_gain, hgrn_lb_logits, hgrn_out_gain, w_out, norm_mlp, w_up, w_down, norm_final, loss_target, m_norm_mix, m_w_in, m_attn_out_gain, m_hgrn_lb_logits, m_hgrn_out_gain, m_w_out, m_norm_mlp, m_w_up, m_w_down, m_norm_final, v_norm_mix, v_w_in, v_attn_out_gain, v_hgrn_lb_logits, v_hgrn_out_gain, v_w_out, v_norm_mlp, v_w_up, v_w_down, v_norm_final):
    depth = w_in.shape[0]
    assert depth == 2 and x.shape[0] == 1
    s, d = x.shape[1], x.shape[2]
    x0 = x[0]
    target = loss_target[0]
    g_mix, g_attn, g_hg, g_mlp = (norm_mix[:, None, :], attn_out_gain[:, None, :], hgrn_out_gain[:, None, :],
                                  norm_mlp[:, None, :])
    lb, lb_jac = _lower_bounds(hgrn_lb_logits)
    lb3 = lb[:, None, :]

    def flip(a):
        return jnp.swapaxes(a, 1, 2)

    shards = list(_pack_weights(flip(w_in), w_out, w_up, w_down))
    w_pieces = _weight_pieces(*shards)
    w_groups = [[0], [1, 2, 3], [4], [5, 6, 7]]
    me = (4 * lax.axis_index("x") + 2 * lax.axis_index("y") + lax.axis_index("c")).astype(jnp.int32).reshape(1)
    lands = _exchange_own("all_gather_own", me, shards, w_pieces)
    w_sems, shards, lands, token = _exchange_start("all_gather_start", shards, lands, w_pieces, w_groups)

    def weights_ready(group, after):
        nonlocal shards
        idxs = w_groups[group]
        shards, got = _exchange_wait(f"all_gather_wait{group}", shards, [lands[i] for i in idxs], w_pieces,
                                     [(idxs, *w_sems[group])], after)
        return got

    cos, sin = _rope_tables(s, token)

    def tied(small_arr, tok):
        return small_arr + tok[0, 0]

    saved = []
    xl = x0
    full = [None] * depth
    for l in range(depth):
        (full_in,) = weights_ready(2 * l, cos if l == 0 else xl)
        saved_x = xl
        proj, h = _fwd_inproj(l, xl, g_mix, full_in, cos, sin)
        o_attn, lse = _attn_fwd(l, proj)
        o_hg, mixed, states = _hgrn_fwd(l, proj, lb3, g_hg)
        mixed = _attn_norm(l, o_attn, g_attn, mixed)
        full_out, full_up, full_down = weights_ready(2 * l + 1, mixed)
        head = (norm_final[None, :], target) if l == depth - 1 else None
        xl, x_mid, u, h2, *loss_side = _mlp_fwd(l, xl, mixed, g_mlp, full_out, full_up, full_down, head)
        saved.append((saved_x, proj, h, o_attn, lse, o_hg, states, mixed, x_mid, u, h2))
        full[l] = (full_in, full_out, full_up, full_down)
    dx, (dxb, dnorm_final8, loss8) = xl, loss_side

    exchanges = []

    def scatter(tag, grads, kinds):
        pieces = _grad_pieces(grads, kinds)
        own = _exchange_own(f"reduce_scatter_own_{tag}", me, grads, pieces)
        sems, grads, own, tok = _exchange_start(f"reduce_scatter_start_{tag}", grads, own, pieces, [list(range(len(pieces)))])
        exchanges.append((grads, own, pieces, sems[0]))
        return tok

    small = {}
    for l in reversed(range(depth)):
        xl, proj, h, o_attn, lse, o_hg, states, mixed, x_mid, u, h2 = saved[l]
        full_in, full_out, full_up, full_down = full[l]
        hs = full_up.shape[3]
        gw_down = _mm_tn(f"grad_w_down_l{l}", u, dxb, u.shape[1], a_fn=_relu2)
        dx_mid, dx_mid_b, du, drec, do, delta, dmlp8, dattn8 = _mlp_bwd(l, dx, dxb, x_mid, g_mlp, u, o_attn, g_attn,
                                                                         full_out, full_up, full_down)
        gw_up = _mm_tn(f"grad_w_up_l{l}", h2, du, d, out_block_w=hs)
        gw_out = _mm_tn(f"grad_w_out_l{l}", mixed, dx_mid_b, mixed.shape[1])
        started = scatter(f"mlp_l{l}", [gw_down, gw_up, gw_out], ["rows", "up", "rows"])
        dqkv = _attn_bwd(l, proj, do, lse, delta, cos, sin, started)
        dhg, dlb8, dhgain8 = _hgrn_bwd(l, proj, lb3, g_hg, o_hg, drec, states)
        gin = _mm_tn(f"grad_w_in_qkv_l{l}", dqkv, h, PROJ_W, a_lead=True)
        gw_in = _mm_tn(f"grad_w_in_hg_l{l}", dhg, h, PROJ_W, a_lead=True, out_block_off=3, prev=gin)
        g_mix_t = tied(g_mix, scatter(f"mix_l{l}", [gw_in], ["rows"]))
        dx, dxb, dmix8 = _bwd_inproj(l, dqkv, dhg, full_in, xl, g_mix_t, dx_mid)
        small[l] = (dmix8, dattn8, dlb8, dhgain8, dmlp8)

    def scattered(name, which, after):
        grads, lands, pieces, waits = [], [], [], []
        for grads_e, own, pieces_e, (send, recv) in (exchanges[i] for i in which):
            first = len(pieces)
            pieces += [p._replace(src=p.src + len(grads)) for p in pieces_e]
            waits.append((list(range(first, first + len(pieces_e))), send, recv))
            grads += grads_e
            lands += own
        return _exchange_wait(name, grads, lands, pieces, waits, after)[1]

    down1, up1, out1, in1, down0, up0, out0 = scattered("reduce_scatter_wait_early", (0, 1, 2), dx)
    big = {
        "w_down": _adam_big("adam_w_down", [down0, down1], w_down, m_w_down, v_w_down, 4),
        "w_up": _adam_big("adam_w_up", [up0, up1], w_up, m_w_up, v_w_up, 2),
        "w_out": _adam_big("adam_w_out", [out0, out1], w_out, m_w_out, v_w_out, 1),
    }
    g_small = _all_reduce_small([small[l][0] for l in range(depth)], [small[l][4] for l in range(depth)], dnorm_final8,
                                [small[l][1] for l in range(depth)], small[depth - 1][2], lb_jac,
                                [small[l][3] for l in range(depth)], loss8, big["w_out"][4])
    loss = g_small[7, 2 * HGRN_DIM]
    row = lambda a: a[None, :]
    small_out = _adam_small(
        g_small, [norm_mix, attn_out_gain, hgrn_lb_logits, hgrn_out_gain, norm_mlp, row(norm_final)],
        [m_norm_mix, m_attn_out_gain, m_hgrn_lb_logits, m_hgrn_out_gain, m_norm_mlp, row(m_norm_final)],
        [v_norm_mix, v_attn_out_gain, v_hgrn_lb_logits, v_hgrn_out_gain, v_norm_mlp, row(v_norm_final)])
    small_out[5] = [t[0] for t in small_out[5]]
    (in0,) = scattered("reduce_scatter_wait_last", (3,), small_out[0][1])
    big["w_in"] = [flip(t) for t in _adam_big("adam_w_in", [in0, in1], flip(w_in), flip(m_w_in), flip(v_w_in), 2)[:4]]

    def gather(idx):
        mix, attn, lbl, hg, mlp, final = (t[idx] for t in small_out)
        return [mix, big["w_in"][idx], attn, lbl, hg, big["w_out"][idx], mlp, big["w_up"][idx], big["w_down"][idx], final]

    return (loss, dx[None], *gather(0), *gather(1), *gather(2), *gather(3))
```

```python
import functools
from typing import Callable, NamedTuple

import jax
import jax.numpy as jnp
from jax import lax
from jax.experimental import pallas as pl
from jax.experimental.pallas import tpu as pltpu

F32 = jnp.float32
BF = jnp.bfloat16

N_DEV = 8
ATTN_W = 512
HGRN_W = 512
HGRN_HEADS = 4
HGRN_DIM = 128
SEG = 512
N_SEG = 7
PROJ_W = N_SEG * SEG
MIX_W = ATTN_W + HGRN_W
SPAN = 128
DILATIONS = (1, 4, 16)
HGRN_CHUNK = 16
ROPE_THETA = 10000.0
NORM_EPS = 1e-6
MASK_VALUE = -1e30
ATTN_SCALE = 0.125
HGRN_SCALE = HGRN_DIM ** -0.5
ADAM_LR = 0.001
ADAM_B1 = 0.9
ADAM_B2 = 0.999
ADAM_EPS = 1e-08
ADAM_WD = 0.01
ADAM_STEP = 10
LANES = 128
VMEM_LIMIT = 56 * 1024 * 1024

NN = ((1,), (0,))
NT = ((1,), (1,))
TN = ((0,), (0,))
MESH = pl.DeviceIdType.MESH


def _dot(a, b, dims):
    return lax.dot_general(a, b, (dims, ((), ())), preferred_element_type=F32)


def _cparams(sem):
    return pltpu.CompilerParams(dimension_semantics=sem, vmem_limit_bytes=VMEM_LIMIT)


def _part8(x):
    r, n = x.shape
    return jnp.sum(x.reshape(r // 8, 8, n), axis=0)


def _sigmoid(x):
    return 1.0 / (1.0 + jnp.exp(-x))


def _rms_fwd(x, gain):
    r = lax.rsqrt(jnp.mean(x * x, axis=-1, keepdims=True) + NORM_EPS)
    return x * r * gain


def _rms_bwd(dy, x, gain):
    r = lax.rsqrt(jnp.mean(x * x, axis=-1, keepdims=True) + NORM_EPS)
    xn = x * r
    dxn = dy * gain
    dx = r * (dxn - xn * jnp.mean(dxn * xn, axis=-1, keepdims=True))
    return dx, dy * xn


def _rope_partner(x):
    n = x.shape[-1]
    lane = lax.broadcasted_iota(jnp.int32, x.shape, x.ndim - 1)
    return jnp.where((lane % 64) < 32, pltpu.roll(x, n - 32, x.ndim - 1), pltpu.roll(x, 32, x.ndim - 1))


def _tile_lanes(t, reps):
    return jnp.concatenate([t] * reps, axis=-1)


def _mm_tn(name, a, b, out_rows, a_lead=False, out_block_off=0, prev=None, out_block_w=None, a_fn=None,
           tm=512, tn=1024, sub=512):
    kdim, n = b.shape
    m = a.shape[-1]
    tm, tn, sub = min(tm, m), min(tn, n), min(sub, kdim)
    mt = m // tm
    n_lead = a.shape[0] if a_lead else 1
    if a_lead:
        a_spec = pl.BlockSpec((None, kdim, tm), lambda j, i: (i // mt, 0, i % mt))
    else:
        a_spec = pl.BlockSpec((kdim, tm), lambda j, i: (0, i))
    b_spec = pl.BlockSpec((kdim, tn), lambda j, i: (0, j))
    if out_block_w:
        nb = tn // out_block_w
        o_shape = jax.ShapeDtypeStruct((n // out_block_w, out_rows, out_block_w), BF)
        o_spec = pl.BlockSpec((nb, tm, out_block_w), lambda j, i: (j, i + out_block_off, 0))
    else:
        nb = 0
        o_shape = jax.ShapeDtypeStruct((out_rows, n), BF)
        o_spec = pl.BlockSpec((tm, tn), lambda j, i: (i + out_block_off, j))
    arrays, specs, aliases = [a, b], [a_spec, b_spec], {}
    if prev is not None:
        arrays.append(prev)
        specs.append(pl.BlockSpec(memory_space=pl.ANY))
        aliases = {2: 0}

    def body(*refs):
        a_ref, b_ref, o_ref = refs[0], refs[1], refs[-1]
        acc = None
        for k in range(kdim // sub):
            av = a_ref[k * sub:(k + 1) * sub, :]
            if a_fn is not None:
                av = a_fn(av)
            part = _dot(av, b_ref[k * sub:(k + 1) * sub, :], TN)
            acc = part if acc is None else acc + part
        if nb:
            for t in range(nb):
                o_ref[t] = acc[:, t * out_block_w:(t + 1) * out_block_w].astype(BF)
        else:
            o_ref[...] = acc.astype(BF)

    return pl.pallas_call(
        body, grid=(n // tn, n_lead * mt), in_specs=specs, out_specs=o_spec, out_shape=o_shape,
        compiler_params=_cparams(("parallel", "parallel")), name=name, input_output_aliases=aliases,
    )(*arrays)


def _pack_weights(w_in_t, w_out, w_up, w_down):
    depth = w_in_t.shape[0]
    arrays = (w_in_t, w_out, w_up, w_down)

    def body(*refs):
        for src, dst in zip(refs[:4], refs[4:]):
            dst[...] = src[...].astype(BF)

    specs = [pl.BlockSpec((None,) + a.shape[1:], lambda l: (l, 0, 0)) for a in arrays]
    return pl.pallas_call(
        body, grid=(depth,), in_specs=specs, out_specs=specs,
        out_shape=[jax.ShapeDtypeStruct(a.shape, BF) for a in arrays],
        compiler_params=_cparams(("arbitrary",)), name="pack_weights",
    )(*arrays)


def _my_position():
    x, y, c = lax.axis_index("x"), lax.axis_index("y"), lax.axis_index("c")
    return x, y, c, 4 * x + 2 * y + c


def _peer(x, y, c, k):
    px = 1 - x if k & 4 else x
    py = 1 - y if k & 2 else y
    pc = 1 - c if k & 1 else c
    return (px, py, pc), 4 * px + 2 * py + pc


PEER_ORDER = (1, 2, 4, 3, 5, 6, 7)


class _Piece(NamedTuple):
    src: int
    send: Callable
    slot: Callable
    land_shape: tuple
    own_src: tuple
    own_slot: tuple


HBM_SPEC = pl.BlockSpec(memory_space=pltpu.HBM)
SEM_SPEC = pl.BlockSpec(memory_space=pltpu.SEMAPHORE)
ANY_SPEC = pl.BlockSpec(memory_space=pl.ANY)


def _in_hbm(arrays):
    return [pltpu.with_memory_space_constraint(a, pltpu.HBM) for a in arrays]


def _hbm_like(arrays):
    return [pltpu.HBM(a.shape, a.dtype) for a in arrays]


def _rows_of(rows):
    return lambda ref, dev: ref.at[pl.ds(pl.multiple_of(dev * rows, 16), rows), :]


def _exchange_own(name, me, srcs, pieces):
    n = len(pieces)

    def body(me_ref, *refs):
        for i in range(n):
            refs[n + i][...] = refs[i][...]

    def spec(block_and_index):
        block, index = block_and_index
        return pl.BlockSpec(block, lambda i, me_ref: index(me_ref[0]))

    return pl.pallas_call(
        body,
        grid_spec=pltpu.PrefetchScalarGridSpec(
            num_scalar_prefetch=1, grid=(1,), in_specs=[spec(p.own_src) for p in pieces],
            out_specs=[spec(p.own_slot) for p in pieces]),
        out_shape=[jax.ShapeDtypeStruct(p.land_shape, BF) for p in pieces],
        compiler_params=_cparams(("arbitrary",)), name=name,
    )(me, *[srcs[p.src] for p in pieces])


def _exchange_start(name, srcs, lands, pieces, groups):
    n_src, n, n_g = len(srcs), len(pieces), len(groups)

    def body(*refs):
        src_refs, land_refs = refs[:n_src], refs[n_src:n_src + n]
        sems, token = refs[n_src + n:n_src + n + 2 * n_g], refs[-1]
        x, y, c, me = _my_position()
        for g, idxs in enumerate(groups):
            for k in PEER_ORDER:
                peer, pid = _peer(x, y, c, k)
                for j, i in enumerate(idxs):
                    p = pieces[i]
                    pltpu.make_async_remote_copy(
                        src_ref=p.send(src_refs[p.src], pid), dst_ref=p.slot(land_refs[i], me),
                        send_sem=sems[2 * g].at[(k - 1) * len(idxs) + j], recv_sem=sems[2 * g + 1].at[(k - 1) * len(idxs) + j],
                        device_id=peer, device_id_type=MESH).start()
        token[...] = jnp.zeros(token.shape, F32)

    sem_shapes = [pltpu.SemaphoreType.DMA(((N_DEV - 1) * len(idxs),)) for idxs in groups for _ in range(2)]
    res = pl.pallas_call(
        body, in_specs=[HBM_SPEC] * (n_src + n),
        out_specs=[SEM_SPEC] * (2 * n_g) + [HBM_SPEC] * (n_src + n) + [pl.BlockSpec(memory_space=pltpu.VMEM)],
        out_shape=sem_shapes + _hbm_like(srcs) + _hbm_like(lands) + [jax.ShapeDtypeStruct((8, LANES), F32)],
        input_output_aliases={i: 2 * n_g + i for i in range(n_src + n)},
        compiler_params=pltpu.CompilerParams(has_side_effects=pltpu.SideEffectType.DATAFLOW_SIDE_EFFECTING),
        name=name,
    )(*_in_hbm(srcs), *_in_hbm(lands))
    sems = [(res[2 * g], res[2 * g + 1]) for g in range(n_g)]
    return sems, list(res[2 * n_g:2 * n_g + n_src]), list(res[2 * n_g + n_src:2 * n_g + n_src + n]), res[-1]


def _exchange_wait(name, srcs, lands, pieces, waits, after):
    n_src, n, n_g = len(srcs), len(lands), len(waits)

    def body(*refs):
        src_refs, land_refs = refs[:n_src], refs[n_src:n_src + n]
        sems = refs[n_src + n:n_src + n + 2 * n_g]
        x, y, c, me = _my_position()
        at = 0
        for g, (idxs, _, _) in enumerate(waits):
            for k in PEER_ORDER:
                peer, pid = _peer(x, y, c, k)
                for j, i in enumerate(idxs):
                    p = pieces[i]
                    cp = pltpu.make_async_remote_copy(
                        src_ref=p.send(src_refs[p.src], pid), dst_ref=p.slot(land_refs[at + j], pid),
                        send_sem=sems[2 * g].at[(k - 1) * len(idxs) + j], recv_sem=sems[2 * g + 1].at[(k - 1) * len(idxs) + j],
                        device_id=peer, device_id_type=MESH)
                    cp.wait_send()
                    cp.wait_recv()
            at += len(idxs)

    sem_args = [s for _, send, recv in waits for s in (send, recv)]
    res = pl.pallas_call(
        body, in_specs=[HBM_SPEC] * (n_src + n) + [SEM_SPEC] * (2 * n_g) + [ANY_SPEC],
        out_specs=[HBM_SPEC] * (n_src + n), out_shape=_hbm_like(srcs) + _hbm_like(lands),
        input_output_aliases={i: i for i in range(n_src + n)},
        compiler_params=pltpu.CompilerParams(has_side_effects=pltpu.SideEffectType.DATAFLOW_SIDE_EFFECTING),
        name=name,
    )(*srcs, *lands, *sem_args, after)
    return list(res[:n_src]), list(res[n_src:])


def _weight_pieces(p_in, p_out, p_up, p_down):
    depth, cin, d = p_in.shape
    rout, hs = p_out.shape[1], p_up.shape[2]
    pieces = []
    for l in range(depth):
        whole = functools.partial(lambda ref, dev, l: ref.at[l], l=l)
        layer = functools.partial(lambda dev, l: (l, 0, 0), l=l)

        def rows(src, n_rows, whole=whole, layer=layer):
            return _Piece(src, whole, lambda ref, dev: _rows_of(n_rows)(ref.at[0], dev), (1, N_DEV * n_rows, d),
                          ((None, n_rows, d), layer), ((None, n_rows, d), lambda dev: (0, dev, 0)))

        pieces += [
            rows(0, cin), rows(1, rout),
            _Piece(2, whole, lambda ref, dev: ref.at[0, dev], (1, N_DEV, d, hs),
                   ((None, d, hs), layer), ((None, None, d, hs), lambda dev: (0, dev, 0, 0))),
            rows(3, hs),
        ]
    return pieces


def _grad_pieces(g_pair, kinds):
    pieces = []
    for i, (g, kind) in enumerate(zip(g_pair, kinds)):
        lead = lambda dev: (dev, 0, 0)
        if kind == "up":
            blk = ((None,) + g.shape[1:], lead)
            pieces.append(_Piece(i, lambda ref, dev: ref.at[dev], lambda ref, dev: ref.at[dev], g.shape, blk, blk))
        else:
            rows, cols = g.shape[0] // N_DEV, g.shape[1]
            pieces.append(_Piece(i, _rows_of(rows), lambda ref, dev: ref.at[dev], (N_DEV, rows, cols),
                                 ((rows, cols), lambda dev: (dev, 0)), ((None, rows, cols), lead)))
    return pieces


SMALL_W = 1024


def _all_reduce_small(mix8, mlp8, final8, attn8, lb8_last, lb_jac, hg8, loss8, after):
    def body(mix0, mix1, mlp0, mlp1, fin, attn0, attn1, lb, jac, hg0, hg1, loss, after_ref, o_ref, src_ref, buf_ref,
             send_sems, recv_sems):
        def total(ref):
            return jnp.sum(ref[...], axis=0, keepdims=True)

        dlb = total(lb)
        hg = jnp.concatenate([total(hg0), total(hg1), total(loss)], axis=1)
        src_ref[...] = jnp.concatenate([
            total(mix0), total(mix1), total(mlp0), total(mlp1), total(fin),
            jnp.concatenate([total(attn0), total(attn1)], axis=1),
            jnp.concatenate([jac[0:1, :] * dlb, jac[1:2, :] * dlb], axis=1),
            jnp.concatenate([hg, jnp.zeros((1, SMALL_W - hg.shape[1]), F32)], axis=1)], axis=0)
        x, y, c, me = _my_position()
        buf_ref[me] = src_ref[...]
        sends = []
        for k in PEER_ORDER:
            peer, _ = _peer(x, y, c, k)
            cp = pltpu.make_async_remote_copy(src_ref=src_ref, dst_ref=buf_ref.at[me], send_sem=send_sems.at[k - 1],
                                              recv_sem=recv_sems.at[k - 1], device_id=peer, device_id_type=MESH)
            cp.start()
            sends.append(cp)
        for k in PEER_ORDER:
            peer, pid = _peer(x, y, c, k)
            pltpu.make_async_remote_copy(src_ref=src_ref, dst_ref=buf_ref.at[pid], send_sem=send_sems.at[k - 1],
                                         recv_sem=recv_sems.at[k - 1], device_id=peer, device_id_type=MESH).wait_recv()
        for cp in sends:
            cp.wait_send()
        acc = buf_ref[0]
        for dev in range(1, N_DEV):
            acc = acc + buf_ref[dev]
        o_ref[...] = acc

    assert mix8[0].shape[1] == SMALL_W
    vm = pl.BlockSpec(memory_space=pltpu.VMEM)
    return pl.pallas_call(
        body, in_specs=[vm] * 12 + [ANY_SPEC], out_specs=vm, out_shape=jax.ShapeDtypeStruct((8, SMALL_W), F32),
        scratch_shapes=[pltpu.VMEM((8, SMALL_W), F32), pltpu.VMEM((N_DEV, 8, SMALL_W), F32),
                        pltpu.SemaphoreType.DMA((N_DEV - 1,)), pltpu.SemaphoreType.DMA((N_DEV - 1,))],
        name="all_reduce_small",
    )(*mix8, *mlp8, final8, *attn8, lb8_last, lb_jac, *hg8, loss8, after)


def _resident(block_shape, index_map):
    return pl.BlockSpec(block_shape, index_map, pipeline_mode=pl.Buffered(1))


def _fwd_inproj(layer, x, gain, g_in, cos, sin, tm=512):
    s, d = x.shape

    def body(x_ref, gain_ref, w_ref, cos_ref, sin_ref, proj_ref, h_ref):
        h = _rms_fwd(x_ref[...], gain_ref[...]).astype(BF)
        h_ref[...] = h
        cs = _tile_lanes(cos_ref[...], SEG // LANES)
        sn = _tile_lanes(sin_ref[...], SEG // LANES)
        for seg in range(N_SEG):
            acc = _dot(h, w_ref[seg * SEG:(seg + 1) * SEG, :], NT)
            if seg < 2:
                acc = acc * cs + _rope_partner(acc) * sn
            if seg == 0:
                acc = acc * ATTN_SCALE
            proj_ref[:, seg * SEG:(seg + 1) * SEG] = acc

    return pl.pallas_call(
        body, grid=(s // tm,),
        in_specs=[pl.BlockSpec((tm, d), lambda i: (i, 0)), pl.BlockSpec((None, 1, d), lambda i: (layer, 0, 0)),
                  _resident((None, PROJ_W, d), lambda i: (0, 0, 0)),
                  pl.BlockSpec((tm, LANES), lambda i: (i, 0)), pl.BlockSpec((tm, LANES), lambda i: (i, 0))],
        out_specs=[pl.BlockSpec((tm, PROJ_W), lambda i: (i, 0)), pl.BlockSpec((tm, d), lambda i: (i, 0))],
        out_shape=[jax.ShapeDtypeStruct((s, PROJ_W), F32), jax.ShapeDtypeStruct((s, d), BF)],
        compiler_params=_cparams(("parallel",)), name=f"fwd_inproj_l{layer}",
    )(x, gain, g_in, cos, sin)


ATTN_UNIT = SPAN * max(DILATIONS)
ATTN_GROUP = 4


def _attn_masks(first_block_has_prev):
    row = lax.broadcasted_iota(jnp.int32, (SPAN, 2 * SPAN), 0)
    col = lax.broadcasted_iota(jnp.int32, (SPAN, 2 * SPAN), 1)
    band = (col >= row) & (col <= row + SPAN)
    lane = lax.broadcasted_iota(jnp.int32, (SPAN, LANES), 1)
    return band & ((col >= SPAN) | first_block_has_prev), band, lane < 64


def _attn_specs(n_in_extra, unit_of=lambda n: n):
    pairs = ATTN_W // LANES
    q_spec = pl.BlockSpec((ATTN_UNIT, LANES), lambda p, n: (unit_of(n), p))

    def prev(seg):
        return pl.BlockSpec((ATTN_UNIT, LANES), lambda p, n: (jnp.maximum(unit_of(n) - 1, 0), seg * pairs + p))

    def cur(seg):
        return pl.BlockSpec((ATTN_UNIT, LANES), lambda p, n: (unit_of(n), seg * pairs + p))

    return [q_spec, prev(1), cur(1), prev(2), cur(2)] + [q_spec] * n_in_extra


def _attn_groups(dil):
    blocks = ATTN_UNIT // (SPAN * dil)
    pairs = [(r, b) for r in range(dil) for b in range(blocks)]
    return [pairs[i:i + ATTN_GROUP] for i in range(0, len(pairs), ATTN_GROUP)]


def _block_rows(dil, r, b, n=1):
    start = r + dil * SPAN * b
    return pl.ds(start, n * SPAN, stride=dil) if dil > 1 else pl.ds(start, n * SPAN)


def _block_keys(prev_ref, cur_ref, dil, r, b):
    if b > 0:
        return cur_ref[_block_rows(dil, r, b - 1, 2), :]
    last = ATTN_UNIT // (SPAN * dil) - 1
    return jnp.concatenate([prev_ref[_block_rows(dil, r, last), :], cur_ref[_block_rows(dil, r, 0), :]], axis=0)


def _attn_fwd(layer, proj):
    s = proj.shape[0]
    n_pat = len(DILATIONS)
    merge_rows = 256

    def body(q_ref, kp_ref, kc_ref, vp_ref, vc_ref, o_ref, lse_ref, o_scr, lse_scr):
        m_first, m_rest, is_a = _attn_masks(pl.program_id(1) > 0)
        sels = (is_a, jnp.logical_not(is_a))
        is_a_keys = lax.broadcasted_iota(jnp.int32, (2 * SPAN, LANES), 1) < 64
        for pi, dil in enumerate(DILATIONS):
            for group in _attn_groups(dil):
                items = [(r, b, h) for r, b in group for h in range(2)]
                q = {rb: q_ref[_block_rows(dil, *rb), :] for rb in group}
                k = {rb: _block_keys(kp_ref, kc_ref, dil, *rb).astype(BF) for rb in group}
                v = {rb: _block_keys(vp_ref, vc_ref, dil, *rb).astype(BF) for rb in group}
                v_sum = {rb: (jnp.where(is_a_keys, v[rb], 1.0), jnp.where(is_a_keys, 1.0, v[rb])) for rb in group}
                sc = [jnp.where(m_first if b == 0 else m_rest,
                                _dot(jnp.where(sels[h], q[r, b], 0.0).astype(BF), k[r, b], NT), MASK_VALUE)
                      for r, b, h in items]
                mx = [jnp.max(jnp.maximum(t[:, :SPAN], t[:, SPAN:]), axis=-1, keepdims=True) for t in sc]
                p = [jnp.exp(t - m).astype(BF) for t, m in zip(sc, mx)]
                both = [_dot(t, v_sum[r, b][h], NN) for t, (r, b, h) in zip(p, items)]
                for j, (r, b) in enumerate(group):
                    t_a, t_b = both[2 * j], both[2 * j + 1]
                    den = pltpu.roll(jnp.where(is_a, t_b, t_a), 64, 1)
                    o_scr[pi, _block_rows(dil, r, b), :] = jnp.where(is_a, t_a, t_b) / den
                    lse_scr[pi, _block_rows(dil, r, b), :] = jnp.where(is_a, mx[2 * j], mx[2 * j + 1]) + jnp.log(den)
        for c in range(ATTN_UNIT // merge_rows):
            rows = slice(c * merge_rows, (c + 1) * merge_rows)
            ls = [lse_scr[pi, rows, :] for pi in range(n_pat)]
            mx = functools.reduce(jnp.maximum, ls)
            ws = [jnp.exp(l - mx) for l in ls]
            den = functools.reduce(jnp.add, ws)
            o_ref[rows, :] = functools.reduce(jnp.add, [w * o_scr[pi, rows, :] for pi, w in enumerate(ws)]) / den
            lse_ref[rows, :] = mx + jnp.log(den)

    out_spec = pl.BlockSpec((ATTN_UNIT, LANES), lambda p, n: (n, p))
    return pl.pallas_call(
        body, grid=(ATTN_W // LANES, s // ATTN_UNIT), in_specs=_attn_specs(0), out_specs=[out_spec, out_spec],
        out_shape=[jax.ShapeDtypeStruct((s, ATTN_W), F32)] * 2,
        scratch_shapes=[pltpu.VMEM((n_pat, ATTN_UNIT, LANES), F32)] * 2,
        compiler_params=_cparams(("parallel", "arbitrary")), name=f"attn_fwd_l{layer}",
    )(proj, proj, proj, proj, proj)


def _attn_norm(layer, o, gain, mixed, tm=512):
    s = o.shape[0]

    def body(o_ref, gain_ref, mixed_ref, n_ref):
        n_ref[...] = _rms_fwd(o_ref[...], gain_ref[...]).astype(BF)

    blk = pl.BlockSpec((tm, ATTN_W), lambda i: (i, 0))
    return pl.pallas_call(
        body, grid=(s // tm,),
        in_specs=[blk, pl.BlockSpec((None, 1, ATTN_W), lambda i: (layer, 0, 0)), pl.BlockSpec(memory_space=pl.ANY)],
        out_specs=blk, out_shape=jax.ShapeDtypeStruct(mixed.shape, BF), input_output_aliases={2: 0},
        compiler_params=_cparams(("parallel",)), name=f"attn_norm_l{layer}",
    )(o, gain, mixed)


def _chunk_cumsum(x, reverse=False):
    n = x.shape[0]
    pos = lax.broadcasted_iota(jnp.int32, x.shape, 0) % HGRN_CHUNK
    for sh in (1, 2, 4, 8):
        if reverse:
            x = x + jnp.where(pos < HGRN_CHUNK - sh, pltpu.roll(x, n - sh, 0), 0.0)
        else:
            x = x + jnp.where(pos >= sh, pltpu.roll(x, sh, 0), 0.0)
    return x


def _chunk_row(x, row):
    r, n = x.shape
    x3 = x.reshape(r // HGRN_CHUNK, HGRN_CHUNK, n)
    return jnp.broadcast_to(x3[:, row:row + 1, :], x3.shape).reshape(r, n)


def _hgrn_pre(qh, z, lb):
    sig = _sigmoid(z)
    f = lb + (1.0 - lb) * sig
    k = 1.0 - f
    sq = _sigmoid(qh)
    q = qh * sq * HGRN_SCALE
    g = _chunk_cumsum(jnp.log(f))
    g_mid = _chunk_row(g, HGRN_CHUNK // 2 - 1)
    g_last = _chunk_row(g, HGRN_CHUNK - 1)
    e_q, e_k = jnp.exp(g - g_mid), jnp.exp(g_mid - g)
    e_in, e_out = jnp.exp(g), jnp.exp(g_last - g)
    return dict(sig=sig, f=f, k=k, sq=sq, q=q, g_last=g_last, e_q=e_q, e_k=e_k, e_in=e_in, e_out=e_out,
                qt=q * e_q, kt=k * e_k, qg=q * e_in, kout=k * e_out)


def _hgrn_mask():
    row = lax.broadcasted_iota(jnp.int32, (LANES, LANES), 0)
    col = lax.broadcasted_iota(jnp.int32, (LANES, LANES), 1)
    return (row // HGRN_CHUNK == col // HGRN_CHUNK) & (col <= row)


def _hgrn_in_specs(layer, rev, nblk):
    def blk(b):
        return nblk - 1 - b if rev else b
    first = 3 * ATTN_W // HGRN_W
    specs = [pl.BlockSpec((HGRN_ROWS, HGRN_W), functools.partial(lambda b, seg: (blk(b), first + seg), seg=seg))
             for seg in range(4)]
    specs.append(pl.BlockSpec((None, 1, HGRN_W), lambda b: (layer, 0, 0)))
    specs.append(pl.BlockSpec((None, 1, HGRN_DIM), lambda b: (layer, 0, 0)))
    return specs, blk


def _head(x, h):
    return x[:, h * HGRN_DIM:(h + 1) * HGRN_DIM]


def _chunk(x, c):
    return x[c * HGRN_CHUNK:(c + 1) * HGRN_CHUNK]


def _sub(x, sb):
    return x[sb * LANES:(sb + 1) * LANES]


HGRN_ROWS = 256
HEADS = range(HGRN_HEADS)
SUBS = range(HGRN_ROWS // LANES)
CHUNKS = range(HGRN_ROWS // HGRN_CHUNK)


def _hgrn_fwd(layer, proj, lb, gain):
    s = proj.shape[0]
    nblk = s // HGRN_ROWS
    cpb = len(CHUNKS)

    def body(q_ref, f_ref, i_ref, g_ref, lb_ref, gain_ref, o_ref, rec_ref, st_ref, state):
        @pl.when(pl.program_id(0) == 0)
        def _():
            state[...] = jnp.zeros(state.shape, F32)

        pre = _hgrn_pre(q_ref[...], f_ref[...], lb_ref[...])
        v = i_ref[...].astype(BF)
        qt, kt, qg, kout = (pre[n].astype(BF) for n in ("qt", "kt", "qg", "kout"))
        dec = jnp.exp(pre["g_last"])
        mask = _hgrn_mask()
        a = [[jnp.where(mask, _dot(_sub(_head(qt, h), sb), _sub(_head(kt, h), sb), NT), 0.0).astype(BF) for sb in SUBS]
             for h in HEADS]
        o_intra = [[_dot(a[h][sb], _sub(_head(v, h), sb), NN) for sb in SUBS] for h in HEADS]
        update = [[_dot(_chunk(_head(v, h), c), _chunk(_head(kout, h), c), TN) for c in CHUNKS] for h in HEADS]
        for h in HEADS:
            st = state[h]
            for c in CHUNKS:
                st_ref[h, c * LANES:(c + 1) * LANES, :] = st.astype(BF)
                st = st * _head(dec, h)[c * HGRN_CHUNK:c * HGRN_CHUNK + 1, :] + update[h][c]
            state[h] = st
        inter = [[_dot(_chunk(_head(qg, h), c), st_ref[h, c * LANES:(c + 1) * LANES, :].astype(BF), NT) for c in CHUNKS]
                 for h in HEADS]
        o = [jnp.concatenate(o_intra[h], axis=0) + jnp.concatenate(inter[h], axis=0) for h in HEADS]
        o_ref[...] = jnp.concatenate(o, axis=1)
        gate = g_ref[...]
        normed = jnp.concatenate([_rms_fwd(o[h], gain_ref[...]) for h in HEADS], axis=1)
        rec_ref[...] = (normed * (gate * _sigmoid(gate))).astype(BF)

    specs, _ = _hgrn_in_specs(layer, False, nblk)
    return pl.pallas_call(
        body, grid=(nblk,), in_specs=specs,
        out_specs=[pl.BlockSpec((HGRN_ROWS, HGRN_W), lambda b: (b, 0)), pl.BlockSpec((HGRN_ROWS, HGRN_W), lambda b: (b, 1)),
                   pl.BlockSpec((HGRN_HEADS, cpb * LANES, LANES), lambda b: (0, b, 0))],
        out_shape=[jax.ShapeDtypeStruct((s, HGRN_W), F32), jax.ShapeDtypeStruct((s, MIX_W), BF),
                   jax.ShapeDtypeStruct((HGRN_HEADS, nblk * cpb * LANES, LANES), BF)],
        scratch_shapes=[pltpu.VMEM((HGRN_HEADS, LANES, LANES), F32)],
        compiler_params=_cparams(("arbitrary",)), name=f"hgrn_fwd_l{layer}",
    )(proj, proj, proj, proj, lb, gain)


def _relu2(u):
    return jnp.square(jnp.maximum(u, 0)).astype(BF)


def _mlp_fwd(layer, x, mixed, gain, g_out, g_up, g_down, head=None):
    s, d = x.shape
    mw = mixed.shape[1]
    nblk, hs = g_up.shape[1], g_up.shape[3]
    tm = 512

    def body(x_ref, m_ref, gain_ref, out_w_ref, up_ref, down_ref, *refs):
        if head:
            fin_ref, t_ref, o_ref, mid_ref, u_ref, h_ref, ob_ref, dfin_ref, loss_ref, a_buf = refs
        else:
            o_ref, mid_ref, u_ref, h_ref, a_buf = refs
        xv = x_ref[...] + _dot(m_ref[...], out_w_ref[...], NN)
        mid_ref[...] = xv
        h = _rms_fwd(xv, gain_ref[...]).astype(BF)
        h_ref[...] = h
        for j in range(nblk):
            u = _dot(h, up_ref[j], NN)
            u_ref[:, j * hs:(j + 1) * hs] = u.astype(BF)
            a_buf[:, j * hs:(j + 1) * hs] = _relu2(u)
        def down(rows):
            acc = xv[rows]
            for j in range(nblk):
                acc = acc + _dot(a_buf[rows, j * hs:(j + 1) * hs], down_ref[j * hs:(j + 1) * hs, :], NN)
            return acc

        if not head:
            o_ref[...] = down(slice(None))
            return
        fin = fin_ref[...]
        dfin8, loss8 = 0.0, 0.0
        for r in range(2):
            rows = slice(r * tm // 2, (r + 1) * tm // 2)
            acc = down(rows)
            err = _rms_fwd(acc, fin) - t_ref[rows, :]
            dout, dfin = _rms_bwd(err * (1.0 / d), acc, fin)
            o_ref[rows, :] = dout
            ob_ref[rows, :] = dout.astype(BF)
            dfin8 = dfin8 + _part8(dfin)
            loss8 = loss8 + _part8(0.5 * jnp.mean(err * err, axis=-1, keepdims=True) * jnp.ones((1, LANES), F32))
        step = pl.program_id(0)
        _accumulate_rows(step, dfin_ref, dfin8)
        _accumulate_rows(step, loss_ref, loss8)

    row = pl.BlockSpec((tm, d), lambda i: (i, 0))
    in_specs = [row, pl.BlockSpec((tm, mw), lambda i: (i, 0)), pl.BlockSpec((None, 1, d), lambda i: (layer, 0, 0)),
                _resident((None, mw, d), lambda i: (0, 0, 0)),
                _resident((None, nblk, d, hs), lambda i: (0, 0, 0, 0)),
                _resident((None, nblk * hs, d), lambda i: (0, 0, 0))]
    out_specs = [row, row, pl.BlockSpec((tm, nblk * hs), lambda i: (i, 0)), row]
    out_shape = [jax.ShapeDtypeStruct((s, d), F32), jax.ShapeDtypeStruct((s, d), F32),
                 jax.ShapeDtypeStruct((s, nblk * hs), BF), jax.ShapeDtypeStruct((s, d), BF)]
    args = [x, mixed, gain, g_out, g_up, g_down]
    if head:
        in_specs += [pl.BlockSpec((1, d), lambda i: (0, 0)), row]
        out_specs += [row, pl.BlockSpec((8, d), lambda i: (0, 0)), pl.BlockSpec((8, LANES), lambda i: (0, 0))]
        out_shape += [jax.ShapeDtypeStruct((s, d), BF), jax.ShapeDtypeStruct((8, d), F32), jax.ShapeDtypeStruct((8, LANES), F32)]
        args += list(head)
    return pl.pallas_call(
        body, grid=(s // tm,), in_specs=in_specs, out_specs=out_specs, out_shape=out_shape,
        scratch_shapes=[pltpu.VMEM((tm, nblk * hs), BF)],
        compiler_params=_cparams(("arbitrary",) if head else ("parallel",)), name=f"mlp_fwd_l{layer}",
    )(*args)


def _accumulate_rows(i, ref, part):
    @pl.when(i == 0)
    def _():
        ref[...] = part

    @pl.when(i > 0)
    def _():
        ref[...] += part


def _head_sums(prod):
    row = lax.broadcasted_iota(jnp.int32, (ATTN_W, ATTN_W), 0)
    col = lax.broadcasted_iota(jnp.int32, (ATTN_W, ATTN_W), 1)
    same_head = jnp.where(row // 64 == col // 64, 1.0, 0.0).astype(BF)
    high = prod.astype(BF)
    low = (prod - high.astype(F32)).astype(BF)
    return _dot(high, same_head, NN) + _dot(low, same_head, NN)


def _mlp_bwd(layer, dx, dxb, x, gain, u, o_attn, gain_attn, g_out, g_up, g_down, tm=256):
    s, d = x.shape
    mw = g_out.shape[1]
    nblk, hs = g_up.shape[1], g_up.shape[3]

    def body(dx_ref, dxb_ref, x_ref, gain_ref, u_ref, oa_ref, ga_ref, out_w_ref, up_ref, down_ref, o_ref, ob_ref, du_ref,
             drec_ref, do_ref, delta_ref, dgain_ref, dattn_ref):
        dxb_v = dxb_ref[...]
        for j in range(nblk):
            cols = slice(j * hs, (j + 1) * hs)
            da = _dot(dxb_v, down_ref[cols, :], NT)
            du_ref[:, cols] = (da * (2.0 * jnp.maximum(u_ref[:, cols].astype(F32), 0.0))).astype(BF)
        acc = jnp.zeros((tm, d), F32)
        for j in range(nblk):
            acc = acc + _dot(du_ref[:, j * hs:(j + 1) * hs], up_ref[j], NT)
        dxn, dgain = _rms_bwd(acc, x_ref[...], gain_ref[...])
        out = dx_ref[...] + dxn
        out_b = out.astype(BF)
        o_ref[...] = out
        ob_ref[...] = out_b
        dm = _dot(out_b, out_w_ref[...], NT)
        drec_ref[...] = dm[:, ATTN_W:]
        ov = oa_ref[...]
        do, dattn = _rms_bwd(dm[:, :ATTN_W], ov, ga_ref[...])
        do_ref[...] = do
        delta_ref[...] = _head_sums(do * ov)
        _accumulate_rows(pl.program_id(0), dgain_ref, _part8(dgain))
        _accumulate_rows(pl.program_id(0), dattn_ref, _part8(dattn))

    row = pl.BlockSpec((tm, d), lambda i: (i, 0))
    wide = pl.BlockSpec((tm, nblk * hs), lambda i: (i, 0))
    half = pl.BlockSpec((tm, ATTN_W), lambda i: (i, 0))
    return pl.pallas_call(
        body, grid=(s // tm,),
        in_specs=[row, row, row, pl.BlockSpec((None, 1, d), lambda i: (layer, 0, 0)), wide,
                  half, pl.BlockSpec((None, 1, ATTN_W), lambda i: (layer, 0, 0)),
                  _resident((None, mw, d), lambda i: (0, 0, 0)),
                  _resident((None, nblk, d, hs), lambda i: (0, 0, 0, 0)),
                  _resident((None, nblk * hs, d), lambda i: (0, 0, 0))],
        out_specs=[row, row, wide, half, half, half, pl.BlockSpec((8, d), lambda i: (0, 0)),
                   pl.BlockSpec((8, ATTN_W), lambda i: (0, 0))],
        out_shape=[jax.ShapeDtypeStruct((s, d), F32), jax.ShapeDtypeStruct((s, d), BF),
                   jax.ShapeDtypeStruct((s, nblk * hs), BF)] + [jax.ShapeDtypeStruct((s, ATTN_W), F32)] * 3
        + [jax.ShapeDtypeStruct((8, d), F32), jax.ShapeDtypeStruct((8, ATTN_W), F32)],
        compiler_params=_cparams(("arbitrary",)), name=f"mlp_bwd_l{layer}",
    )(dx, dxb, x, gain, u, o_attn, gain_attn, g_out, g_up, g_down)


def _attn_bwd(layer, proj, do, lse, delta, cos, sin, after):
    s = proj.shape[0]
    n_units = s // ATTN_UNIT
    out_rows = 256

    def unit_of(n):
        return n_units - 1 - n

    def body(q_ref, kp_ref, kc_ref, vp_ref, vc_ref, do_ref, lse_ref, delta_ref, cos_ref, sin_ref, after_ref, out_ref,
             dq_ref, dk_ref, dkp_ref, dv_ref, dvp_ref, carry_k, carry_v):
        step = pl.program_id(1)
        m_first, m_rest, is_a = _attn_masks(unit_of(step) > 0)
        sels = (is_a, jnp.logical_not(is_a))
        for ref in (dq_ref, dk_ref, dkp_ref, dv_ref, dvp_ref):
            ref[...] = jnp.zeros(ref.shape, F32)
        for dil in DILATIONS:
            last = ATTN_UNIT // (SPAN * dil) - 1
            for group in _attn_groups(dil):
                items = [(r, b, h) for r, b in group for h in range(2)]
                q = {rb: q_ref[_block_rows(dil, *rb), :] for rb in group}
                dov = {rb: do_ref[_block_rows(dil, *rb), :] for rb in group}
                lse_v = {rb: lse_ref[_block_rows(dil, *rb), :] for rb in group}
                delta_v = {rb: delta_ref[_block_rows(dil, *rb), :] for rb in group}
                k = {rb: _block_keys(kp_ref, kc_ref, dil, *rb).astype(BF) for rb in group}
                v = {rb: _block_keys(vp_ref, vc_ref, dil, *rb).astype(BF) for rb in group}
                qh = [jnp.where(sels[h], q[r, b], 0.0).astype(BF) for r, b, h in items]
                doh = [jnp.where(sels[h], dov[r, b], 0.0).astype(BF) for r, b, h in items]
                sc = [jnp.where(m_first if b == 0 else m_rest, _dot(qh[i], k[r, b], NT), MASK_VALUE)
                      for i, (r, b, h) in enumerate(items)]
                p = [jnp.exp(sc[i] - lse_v[r, b][:, 64 * h:64 * h + 1]) for i, (r, b, h) in enumerate(items)]
                ds = [(p[i] * (_dot(doh[i], v[r, b], NT) - delta_v[r, b][:, 64 * h:64 * h + 1])).astype(BF)
                      for i, (r, b, h) in enumerate(items)]
                dv = [_dot(p[i].astype(BF), doh[i], TN) for i in range(len(items))]
                dq = [_dot(ds[i], k[r, b], NN) for i, (r, b, h) in enumerate(items)]
                dk = [_dot(ds[i], qh[i], TN) for i in range(len(items))]
                for j, (r, b) in enumerate(group):
                    own = _block_rows(dil, r, b)
                    dq_ref[own, :] += jnp.where(is_a, dq[2 * j], dq[2 * j + 1])
                    dk2, dv2 = dk[2 * j] + dk[2 * j + 1], dv[2 * j] + dv[2 * j + 1]
                    dk_ref[own, :] += dk2[SPAN:]
                    dv_ref[own, :] += dv2[SPAN:]
                    if b > 0:
                        before = _block_rows(dil, r, b - 1)
                        dk_ref[before, :] += dk2[:SPAN]
                        dv_ref[before, :] += dv2[:SPAN]
                    else:
                        before = _block_rows(dil, r, last)
                        dkp_ref[before, :] += dk2[:SPAN]
                        dvp_ref[before, :] += dv2[:SPAN]
        has_next = step > 0
        for c in range(ATTN_UNIT // out_rows):
            rows = slice(c * out_rows, (c + 1) * out_rows)
            cs, sn = cos_ref[rows, :], sin_ref[rows, :]
            dqv = dq_ref[rows, :]
            dkv = dk_ref[rows, :] + jnp.where(has_next, carry_k[rows, :], 0.0)
            dvv = dv_ref[rows, :] + jnp.where(has_next, carry_v[rows, :], 0.0)
            out_ref[0, rows, :] = ((dqv * cs - _rope_partner(dqv) * sn) * ATTN_SCALE).astype(BF)
            out_ref[1, rows, :] = (dkv * cs - _rope_partner(dkv) * sn).astype(BF)
            out_ref[2, rows, :] = dvv.astype(BF)
        carry_k[...] = dkp_ref[...]
        carry_v[...] = dvp_ref[...]

    tab = pl.BlockSpec((ATTN_UNIT, LANES), lambda p, n: (unit_of(n), 0))
    return pl.pallas_call(
        body, grid=(ATTN_W // LANES, n_units), in_specs=_attn_specs(3, unit_of) + [tab, tab, ANY_SPEC],
        out_specs=pl.BlockSpec((3, ATTN_UNIT, LANES), lambda p, n: (0, unit_of(n), p)),
        out_shape=jax.ShapeDtypeStruct((3, s, ATTN_W), BF),
        scratch_shapes=[pltpu.VMEM((ATTN_UNIT, LANES), F32)] * 7,
        compiler_params=_cparams(("parallel", "arbitrary")), name=f"attn_bwd_l{layer}",
    )(proj, proj, proj, proj, proj, do, lse, delta, cos, sin, after)


def _hgrn_bwd(layer, proj, lb, gain, o, drec, states):
    s = proj.shape[0]
    nblk = s // HGRN_ROWS
    cpb = len(CHUNKS)

    def body(q_ref, f_ref, i_ref, g_ref, lb_ref, gain_ref, o_ref, drec_ref, st_ref, dseg_ref, dlb_ref, dgain_ref,
             dstate, dst_buf):
        step = pl.program_id(0)

        @pl.when(step == 0)
        def _():
            dstate[...] = jnp.zeros(dstate.shape, F32)

        lbv, gv = lb_ref[...], gain_ref[...]
        qh, z, gate_in = q_ref[...], f_ref[...], g_ref[...]
        pre = _hgrn_pre(qh, z, lbv)
        v = i_ref[...].astype(BF)
        sg = _sigmoid(gate_in)
        ov, drec = o_ref[...], drec_ref[...]
        dnormed = drec * (gate_in * sg)
        back = [_rms_bwd(_head(dnormed, h), _head(ov, h), gv) for h in HEADS]
        do_b = jnp.concatenate([b[0] for b in back], axis=1).astype(BF)
        dgain = back[0][1] + back[1][1] + back[2][1] + back[3][1]
        normed = jnp.concatenate([_rms_fwd(_head(ov, h), gv) for h in HEADS], axis=1)
        dgate_in = drec * normed * (sg * (1.0 + gate_in * (1.0 - sg)))
        mask = _hgrn_mask()
        qt, kt, qg, kout = (pre[n].astype(BF) for n in ("qt", "kt", "qg", "kout"))
        dec = jnp.exp(pre["g_last"])
        def intra(fn):
            return jnp.concatenate([jnp.concatenate([fn(h, sb) for sb in SUBS], axis=0) for h in HEADS], axis=1)

        def hs(x, h, sb):
            return _sub(_head(x, h), sb)

        a = [[jnp.where(mask, _dot(hs(qt, h, sb), hs(kt, h, sb), NT), 0.0).astype(BF) for sb in SUBS] for h in HEADS]
        da = [[jnp.where(mask, _dot(hs(do_b, h, sb), hs(v, h, sb), NT), 0.0).astype(BF) for sb in SUBS] for h in HEADS]
        dv_intra = intra(lambda h, sb: _dot(a[h][sb], hs(do_b, h, sb), TN))
        dqt = intra(lambda h, sb: _dot(da[h][sb], hs(kt, h, sb), NN))
        dkt = intra(lambda h, sb: _dot(da[h][sb], hs(qt, h, sb), TN))
        feed = [[_dot(_chunk(_head(do_b, h), c), _chunk(_head(qg, h), c), TN) for c in CHUNKS] for h in HEADS]
        for h in HEADS:
            dst = dstate[h]
            for c in reversed(CHUNKS):
                dst_buf[h, c * LANES:(c + 1) * LANES, :] = dst
                dst = dst * _head(dec, h)[c * HGRN_CHUNK:c * HGRN_CHUNK + 1, :] + feed[h][c]
            dstate[h] = dst

        def per_chunk(fn):
            cols = []
            for h in HEADS:
                rows = [jnp.broadcast_to(t, (HGRN_CHUNK, HGRN_DIM)) for t in (fn(h, c) for c in CHUNKS)]
                cols.append(jnp.concatenate(rows, axis=0))
            return jnp.concatenate(cols, axis=1)

        def st_prev(h, c):
            return st_ref[h, c * LANES:(c + 1) * LANES, :]

        def dst_at(h, c):
            return dst_buf[h, c * LANES:(c + 1) * LANES, :]

        dqg = per_chunk(lambda h, c: _dot(_chunk(_head(do_b, h), c), st_prev(h, c).astype(BF), NN))
        dkout = per_chunk(lambda h, c: _dot(_chunk(_head(v, h), c), dst_at(h, c).astype(BF), NN))
        dv_inter = per_chunk(lambda h, c: _dot(_chunk(_head(kout, h), c), dst_at(h, c).astype(BF), NT))
        dg_state = per_chunk(lambda h, c: jnp.sum(dst_at(h, c) * st_prev(h, c).astype(F32), axis=0, keepdims=True))
        dg_kout = per_chunk(lambda h, c: jnp.sum(_chunk(_head(dkout * pre["kout"], h), c), axis=0, keepdims=True))
        dv = dv_intra + dv_inter
        pos = lax.broadcasted_iota(jnp.int32, (HGRN_ROWS, HGRN_W), 0) % HGRN_CHUNK
        dq = dqt * pre["e_q"] + dqg * pre["e_in"]
        dk = dkt * pre["e_k"] + dkout * pre["e_out"]
        dg = (dqt * pre["qt"] - dkt * pre["kt"] + dqg * pre["qg"] - dkout * pre["kout"]
              + jnp.where(pos == HGRN_CHUNK - 1, dg_state * dec + dg_kout, 0.0))
        dlogf = _chunk_cumsum(dg, reverse=True)
        sig, sq = pre["sig"], pre["sq"]
        df = dlogf / pre["f"] - dk
        dseg_ref[0] = (dq * HGRN_SCALE * (sq * (1.0 + qh * (1.0 - sq)))).astype(BF)
        dseg_ref[1] = (df * (1.0 - lbv) * sig * (1.0 - sig)).astype(BF)
        dseg_ref[2] = dv.astype(BF)
        dseg_ref[3] = dgate_in.astype(BF)
        _accumulate_rows(step, dlb_ref, _part8(df * (1.0 - sig)))
        _accumulate_rows(step, dgain_ref, _part8(dgain))

    specs, blk = _hgrn_in_specs(layer, True, nblk)
    specs += [pl.BlockSpec((HGRN_ROWS, HGRN_W), lambda b: (blk(b), 0)),
              pl.BlockSpec((HGRN_ROWS, HGRN_W), lambda b: (blk(b), 0)),
              pl.BlockSpec((HGRN_HEADS, cpb * LANES, LANES), lambda b: (0, blk(b), 0))]
    return pl.pallas_call(
        body, grid=(nblk,), in_specs=specs,
        out_specs=[pl.BlockSpec((4, HGRN_ROWS, HGRN_W), lambda b: (0, blk(b), 0)),
                   pl.BlockSpec((8, HGRN_W), lambda b: (0, 0)), pl.BlockSpec((8, HGRN_DIM), lambda b: (0, 0))],
        out_shape=[jax.ShapeDtypeStruct((4, s, HGRN_W), BF), jax.ShapeDtypeStruct((8, HGRN_W), F32),
                   jax.ShapeDtypeStruct((8, HGRN_DIM), F32)],
        scratch_shapes=[pltpu.VMEM((HGRN_HEADS, LANES, LANES), F32), pltpu.VMEM((HGRN_HEADS, cpb * LANES, LANES), F32)],
        compiler_params=_cparams(("arbitrary",)), name=f"hgrn_bwd_l{layer}",
    )(proj, proj, proj, proj, lb, gain, o, drec, states)


def _bwd_inproj(layer, dqkv, dhg, g_in, x, gain, dres, tm=512):
    s, d = x.shape

    def body(dqkv_ref, dhg_ref, w_ref, x_ref, gain_ref, dres_ref, dx_ref, dxb_ref, dgain_ref):
        acc = jnp.zeros((tm, d), F32)
        for seg in range(N_SEG):
            a = dqkv_ref[seg] if seg < 3 else dhg_ref[seg - 3]
            acc = acc + _dot(a, w_ref[seg * SEG:(seg + 1) * SEG, :], NN)
        dx, dgain = _rms_bwd(acc, x_ref[...], gain_ref[...])
        out = dres_ref[...] + dx
        dx_ref[...] = out
        dxb_ref[...] = out.astype(BF)
        _accumulate_rows(pl.program_id(0), dgain_ref, _part8(dgain))

    row = pl.BlockSpec((tm, d), lambda i: (i, 0))
    return pl.pallas_call(
        body, grid=(s // tm,),
        in_specs=[pl.BlockSpec((3, tm, SEG), lambda i: (0, i, 0)), pl.BlockSpec((4, tm, SEG), lambda i: (0, i, 0)),
                  _resident((None, PROJ_W, d), lambda i: (0, 0, 0)), row,
                  pl.BlockSpec((None, 1, d), lambda i: (layer, 0, 0)), row],
        out_specs=[row, row, pl.BlockSpec((8, d), lambda i: (0, 0))],
        out_shape=[jax.ShapeDtypeStruct((s, d), F32), jax.ShapeDtypeStruct((s, d), BF), jax.ShapeDtypeStruct((8, d), F32)],
        compiler_params=_cparams(("arbitrary",)), name=f"bwd_inproj_l{layer}",
    )(dqkv, dhg, g_in, x, gain, dres)


def _adamw(w, g, m, v):
    m2 = ADAM_B1 * m + (1.0 - ADAM_B1) * g
    v2 = ADAM_B2 * v + (1.0 - ADAM_B2) * (g * g)
    m_hat = m2 / (1.0 - ADAM_B1 ** ADAM_STEP)
    v_hat = v2 / (1.0 - ADAM_B2 ** ADAM_STEP)
    delta = -ADAM_LR * (m_hat / (jnp.sqrt(v_hat) + ADAM_EPS) + ADAM_WD * w)
    return delta, m2, v2


def _adam_big(name, parts, w, m, v, row_tiles):
    depth = w.shape[0]
    r, c = parts[0].shape[1], parts[0].shape[2]
    tr = r // row_tiles
    p_spec = pl.BlockSpec((N_DEV, tr, c), lambda t: (0, t, 0))
    w_spec = pl.BlockSpec((depth, tr, c), lambda t: (0, t, 0))

    def body(*refs):
        p_refs = refs[:depth]
        w_ref, m_ref, v_ref, g_ref, d_ref, m2_ref, v2_ref, token = refs[depth:]
        token[...] = jnp.zeros(token.shape, F32)
        for l in range(depth):
            g = p_refs[l][0].astype(F32)
            for dev in range(1, N_DEV):
                g = g + p_refs[l][dev].astype(F32)
            delta, m2, v2 = _adamw(w_ref[l], g, m_ref[l], v_ref[l])
            g_ref[l] = g
            d_ref[l] = delta
            m2_ref[l] = m2
            v2_ref[l] = v2

    return pl.pallas_call(
        body, grid=(row_tiles,), in_specs=[p_spec] * depth + [w_spec] * 3,
        out_specs=[w_spec] * 4 + [pl.BlockSpec((8, LANES), lambda t: (0, 0))],
        out_shape=[jax.ShapeDtypeStruct(w.shape, F32)] * 4 + [jax.ShapeDtypeStruct((8, LANES), F32)],
        compiler_params=_cparams(("arbitrary",)), name=name,
    )(*parts, w, m, v)


def _adam_small(g, ws, ms, vs):
    n = len(ws)

    def split(row, width):
        return jnp.concatenate([row[:, :width], row[:, width:2 * width]], axis=0)

    def body(g_ref, *refs):
        ins, outs = refs[:3 * n], refs[3 * n:]
        grads = [g_ref[0:2, :], split(g_ref[5:6, :], ATTN_W), split(g_ref[6:7, :], HGRN_W), split(g_ref[7:8, :], HGRN_DIM),
                 g_ref[2:4, :], g_ref[4:5, :]]
        for i, g_i in enumerate(grads):
            delta, m2, v2 = _adamw(ins[i][...], g_i, ins[n + i][...], ins[2 * n + i][...])
            for j, val in enumerate((g_i, delta, m2, v2)):
                outs[4 * i + j][...] = val

    vm = pl.BlockSpec(memory_space=pltpu.VMEM)
    res = pl.pallas_call(
        body, in_specs=[vm] * (1 + 3 * n), out_specs=[vm] * (4 * n),
        out_shape=[jax.ShapeDtypeStruct(w.shape, F32) for w in ws for _ in range(4)], name="adam_small",
    )(g, *ws, *ms, *vs)
    return [res[4 * i:4 * i + 4] for i in range(n)]


def _lower_bounds(logits):
    def body(l_ref, lb_ref, jac_ref):
        l0, l1 = l_ref[0:1, :], l_ref[1:2, :]
        mx = jnp.maximum(l0, l1)
        e0, e1 = jnp.exp(l0 - mx), jnp.exp(l1 - mx)
        p0, p1 = e0 / (e0 + e1), e1 / (e0 + e1)
        lb_ref[0:1, :] = p0 - p0
        lb_ref[1:2, :] = (p0 + p1) - p0
        jac_ref[0:1, :] = -p0 * p1
        jac_ref[1:2, :] = p0 * p1

    vm = pl.BlockSpec(memory_space=pltpu.VMEM)
    return pl.pallas_call(body, in_specs=[vm], out_specs=[vm, vm], out_shape=[jax.ShapeDtypeStruct(logits.shape, F32)] * 2,
                          name="hgrn_lower_bounds")(logits)


def _rope_tables(s, after):
    half = 32
    inv_freq = ROPE_THETA ** (-jnp.arange(half, dtype=F32) / half)
    ang = (jnp.arange(s, dtype=jnp.int32).astype(F32) + after[0, 0])[:, None] * inv_freq[None, :]
    cos, sin = jnp.cos(ang), jnp.sin(ang)
    return jnp.concatenate([cos] * 4, axis=1), jnp.concatenate([-sin, sin, -sin, sin], axis=1)


def kernel(x, norm_mix, w_in, attn_out_gain, hgrn_lb_logits, hgrn_out_gain, w_out, norm_mlp, w_up, w_down, norm_final, loss_target, m_norm_mix, m_w_in, m_attn_out_gain, m_hgrn_lb_logits, m_hgrn_out_gain, m_w_out, m_norm_mlp, m_w_up, m_w_down, m_norm_final, v_norm_mix, v_w_in, v_attn_out_gain, v_hgrn_lb_logits, v_hgrn_out_gain, v_w_out, v_norm_mlp, v_w_up, v_w_down, v_norm_final):
    depth = w_in.shape[0]
    assert depth == 2 and x.shape[0] == 1
    s, d = x.shape[1], x.shape[2]
    x0 = x[0]
    target = loss_target[0]
    g_mix, g_attn, g_hg, g_mlp = (norm_mix[:, None, :], attn_out_gain[:, None, :], hgrn_out_gain[:, None, :],
                                  norm_mlp[:, None, :])
    lb, lb_jac = _lower_bounds(hgrn_lb_logits)
    lb3 = lb[:, None, :]

    def flip(a):
        return jnp.swapaxes(a, 1, 2)

    shards = list(_pack_weights(flip(w_in), w_out, w_up, w_down))
    w_pieces = _weight_pieces(*shards)
    w_groups = [[0], [1, 2, 3], [4], [5, 6, 7]]
    me = (4 * lax.axis_index("x") + 2 * lax.axis_index("y") + lax.axis_index("c")).astype(jnp.int32).reshape(1)
    lands = _exchange_own("all_gather_own", me, shards, w_pieces)
    w_sems, shards, lands, token = _exchange_start("all_gather_start", shards, lands, w_pieces, w_groups)

    def weights_ready(group, after):
        nonlocal shards
        idxs = w_groups[group]
        shards, got = _exchange_wait(f"all_gather_wait{group}", shards, [lands[i] for i in idxs], w_pieces,
                                     [(idxs, *w_sems[group])], after)
        return got

    cos, sin = _rope_tables(s, token)

    def tied(small_arr, tok):
        return small_arr + tok[0, 0]

    saved = []
    xl = x0
    full = [None] * depth
    for l in range(depth):
        (full_in,) = weights_ready(2 * l, cos if l == 0 else xl)
        saved_x = xl
        proj, h = _fwd_inproj(l, xl, g_mix, full_in, cos, sin)
        o_attn, lse = _attn_fwd(l, proj)
        o_hg, mixed, states = _hgrn_fwd(l, proj, lb3, g_hg)
        mixed = _attn_norm(l, o_attn, g_attn, mixed)
        full_out, full_up, full_down = weights_ready(2 * l + 1, mixed)
        head = (norm_final[None, :], target) if l == depth - 1 else None
        xl, x_mid, u, h2, *loss_side = _mlp_fwd(l, xl, mixed, g_mlp, full_out, full_up, full_down, head)
        saved.append((saved_x, proj, h, o_attn, lse, o_hg, states, mixed, x_mid, u, h2))
        full[l] = (full_in, full_out, full_up, full_down)
    dx, (dxb, dnorm_final8, loss8) = xl, loss_side

    exchanges = []

    def scatter(tag, grads, kinds):
        pieces = _grad_pieces(grads, kinds)
        own = _exchange_own(f"reduce_scatter_own_{tag}", me, grads, pieces)
        sems, grads, own, tok = _exchange_start(f"reduce_scatter_start_{tag}", grads, own, pieces, [list(range(len(pieces)))])
        exchanges.append((grads, own, pieces, sems[0]))
        return tok

    small = {}
    for l in reversed(range(depth)):
        xl, proj, h, o_attn, lse, o_hg, states, mixed, x_mid, u, h2 = saved[l]
        full_in, full_out, full_up, full_down = full[l]
        hs = full_up.shape[3]
        gw_down = _mm_tn(f"grad_w_down_l{l}", u, dxb, u.shape[1], a_fn=_relu2)
        dx_mid, dx_mid_b, du, drec, do, delta, dmlp8, dattn8 = _mlp_bwd(l, dx, dxb, x_mid, g_mlp, u, o_attn, g_attn,
                                                                         full_out, full_up, full_down)
        gw_up = _mm_tn(f"grad_w_up_l{l}", h2, du, d, out_block_w=hs)
        gw_out = _mm_tn(f"grad_w_out_l{l}", mixed, dx_mid_b, mixed.shape[1])
        started = scatter(f"mlp_l{l}", [gw_down, gw_up, gw_out], ["rows", "up", "rows"])
        dqkv = _attn_bwd(l, proj, do, lse, delta, cos, sin, started)
        dhg, dlb8, dhgain8 = _hgrn_bwd(l, proj, lb3, g_hg, o_hg, drec, states)
        gin = _mm_tn(f"grad_w_in_qkv_l{l}", dqkv, h, PROJ_W, a_lead=True)
        gw_in = _mm_tn(f"grad_w_in_hg_l{l}", dhg, h, PROJ_W, a_lead=True, out_block_off=3, prev=gin)
        g_mix_t = tied(g_mix, scatter(f"mix_l{l}", [gw_in], ["rows"]))
        dx, dxb, dmix8 = _bwd_inproj(l, dqkv, dhg, full_in, xl, g_mix_t, dx_mid)
        small[l] = (dmix8, dattn8, dlb8, dhgain8, dmlp8)

    def scattered(name, which, after):
        grads, lands, pieces, waits = [], [], [], []
        for grads_e, own, pieces_e, (send, recv) in (exchanges[i] for i in which):
            first = len(pieces)
            pieces += [p._replace(src=p.src + len(grads)) for p in pieces_e]
            waits.append((list(range(first, first + len(pieces_e))), send, recv))
            grads += grads_e
            lands += own
        return _exchange_wait(name, grads, lands, pieces, waits, after)[1]

    down1, up1, out1, in1, down0, up0, out0 = scattered("reduce_scatter_wait_early", (0, 1, 2), dx)
    big = {
        "w_down": _adam_big("adam_w_down", [down0, down1], w_down, m_w_down, v_w_down, 4),
        "w_up": _adam_big("adam_w_up", [up0, up1], w_up, m_w_up, v_w_up, 2),
        "w_out": _adam_big("adam_w_out", [out0, out1], w_out, m_w_out, v_w_out, 1),
    }
    g_small = _all_reduce_small([small[l][0] for l in range(depth)], [small[l][4] for l in range(depth)], dnorm_final8,
                                [small[l][1] for l in range(depth)], small[depth - 1][2], lb_jac,
                                [small[l][3] for l in range(depth)], loss8, big["w_out"][4])
    loss = g_small[7, 2 * HGRN_DIM]
    row = lambda a: a[None, :]
    small_out = _adam_small(
        g_small, [norm_mix, attn_out_gain, hgrn_lb_logits, hgrn_out_gain, norm_mlp, row(norm_final)],
        [m_norm_mix, m_attn_out_gain, m_hgrn_lb_logits, m_hgrn_out_gain, m_norm_mlp, row(m_norm_final)],
        [v_norm_mix, v_attn_out_gain, v_hgrn_lb_logits, v_hgrn_out_gain, v_norm_mlp, row(v_norm_final)])
    small_out[5] = [t[0] for t in small_out[5]]
    (in0,) = scattered("reduce_scatter_wait_last", (3,), small_out[0][1])
    big["w_in"] = [flip(t) for t in _adam_big("adam_w_in", [in0, in1], flip(w_in), flip(m_w_in), flip(v_w_in), 2)[:4]]

    def gather(idx):
        mix, attn, lbl, hg, mlp, final = (t[idx] for t in small_out)
        return [mix, big["w_in"][idx], attn, lbl, hg, big["w_out"][idx], mlp, big["w_up"][idx], big["w_down"][idx], final]

    return (loss, dx[None], *gather(0), *gather(1), *gather(2), *gather(3))
```

```python
import functools
from typing import Callable, NamedTuple

import jax
import jax.numpy as jnp
from jax import lax
from jax.experimental import pallas as pl
from jax.experimental.pallas import tpu as pltpu

F32 = jnp.float32
BF = jnp.bfloat16

N_DEV = 8
ATTN_W = 512
HGRN_W = 512
HGRN_HEADS = 4
HGRN_DIM = 128
SEG = 512
N_SEG = 7
PROJ_W = N_SEG * SEG
MIX_W = ATTN_W + HGRN_W
SPAN = 128
DILATIONS = (1, 4, 16)
HGRN_CHUNK = 16
ROPE_THETA = 10000.0
NORM_EPS = 1e-6
MASK_VALUE = -1e30
ATTN_SCALE = 0.125
HGRN_SCALE = HGRN_DIM ** -0.5
ADAM_LR = 0.001
ADAM_B1 = 0.9
ADAM_B2 = 0.999
ADAM_EPS = 1e-08
ADAM_WD = 0.01
ADAM_STEP = 10
LANES = 128
VMEM_LIMIT = 56 * 1024 * 1024

NN = ((1,), (0,))
NT = ((1,), (1,))
TN = ((0,), (0,))
MESH = pl.DeviceIdType.MESH


def _dot(a, b, dims):
    return lax.dot_general(a, b, (dims, ((), ())), preferred_element_type=F32)


def _cparams(sem):
    return pltpu.CompilerParams(dimension_semantics=sem, vmem_limit_bytes=VMEM_LIMIT)


def _part8(x):
    r, n = x.shape
    return jnp.sum(x.reshape(r // 8, 8, n), axis=0)


def _sigmoid(x):
    return 1.0 / (1.0 + jnp.exp(-x))


def _rms_fwd(x, gain):
    r = lax.rsqrt(jnp.mean(x * x, axis=-1, keepdims=True) + NORM_EPS)
    return x * r * gain


def _rms_bwd(dy, x, gain):
    r = lax.rsqrt(jnp.mean(x * x, axis=-1, keepdims=True) + NORM_EPS)
    xn = x * r
    dxn = dy * gain
    dx = r * (dxn - xn * jnp.mean(dxn * xn, axis=-1, keepdims=True))
    return dx, dy * xn


def _rope_partner(x):
    n = x.shape[-1]
    lane = lax.broadcasted_iota(jnp.int32, x.shape, x.ndim - 1)
    return jnp.where((lane % 64) < 32, pltpu.roll(x, n - 32, x.ndim - 1), pltpu.roll(x, 32, x.ndim - 1))


def _tile_lanes(t, reps):
    return jnp.concatenate([t] * reps, axis=-1)


def _mm_tn(name, a, b, out_rows, a_lead=False, out_block_off=0, prev=None, out_block_w=None, a_fn=None,
           tm=512, tn=1024, sub=512):
    kdim, n = b.shape
    m = a.shape[-1]
    tm, tn, sub = min(tm, m), min(tn, n), min(sub, kdim)
    mt = m // tm
    n_lead = a.shape[0] if a_lead else 1
    if a_lead:
        a_spec = pl.BlockSpec((None, kdim, tm), lambda j, i: (i // mt, 0, i % mt))
    else:
        a_spec = pl.BlockSpec((kdim, tm), lambda j, i: (0, i))
    b_spec = pl.BlockSpec((kdim, tn), lambda j, i: (0, j))
    if out_block_w:
        nb = tn // out_block_w
        o_shape = jax.ShapeDtypeStruct((n // out_block_w, out_rows, out_block_w), BF)
        o_spec = pl.BlockSpec((nb, tm, out_block_w), lambda j, i: (j, i + out_block_off, 0))
    else:
        nb = 0
        o_shape = jax.ShapeDtypeStruct((out_rows, n), BF)
        o_spec = pl.BlockSpec((tm, tn), lambda j, i: (i + out_block_off, j))
    arrays, specs, aliases = [a, b], [a_spec, b_spec], {}
    if prev is not None:
        arrays.append(prev)
        specs.append(pl.BlockSpec(memory_space=pl.ANY))
        aliases = {2: 0}

    def body(*refs):
        a_ref, b_ref, o_ref = refs[0], refs[1], refs[-1]
        acc = None
        for k in range(kdim // sub):
            av = a_ref[k * sub:(k + 1) * sub, :]
            if a_fn is not None:
                av = a_fn(av)
            part = _dot(av, b_ref[k * sub:(k + 1) * sub, :], TN)
            acc = part if acc is None else acc + part
        if nb:
            for t in range(nb):
                o_ref[t] = acc[:, t * out_block_w:(t + 1) * out_block_w].astype(BF)
        else:
            o_ref[...] = acc.astype(BF)

    return pl.pallas_call(
        body, grid=(n // tn, n_lead * mt), in_specs=specs, out_specs=o_spec, out_shape=o_shape,
        compiler_params=_cparams(("parallel", "parallel")), name=name, input_output_aliases=aliases,
    )(*arrays)


def _pack_weights(w_in_t, w_out, w_up, w_down):
    depth = w_in_t.shape[0]
    arrays = (w_in_t, w_out, w_up, w_down)

    def body(*refs):
        for src, dst in zip(refs[:4], refs[4:]):
            dst[...] = src[...].astype(BF)

    specs = [pl.BlockSpec((None,) + a.shape[1:], lambda l: (l, 0, 0)) for a in arrays]
    return pl.pallas_call(
        body, grid=(depth,), in_specs=specs, out_specs=specs,
        out_shape=[jax.ShapeDtypeStruct(a.shape, BF) for a in arrays],
        compiler_params=_cparams(("arbitrary",)), name="pack_weights",
    )(*arrays)


def _my_position():
    x, y, c = lax.axis_index("x"), lax.axis_index("y"), lax.axis_index("c")
    return x, y, c, 4 * x + 2 * y + c


def _peer(x, y, c, k):
    px = 1 - x if k & 4 else x
    py = 1 - y if k & 2 else y
    pc = 1 - c if k & 1 else c
    return (px, py, pc), 4 * px + 2 * py + pc


PEER_ORDER = (1, 2, 4, 3, 5, 6, 7)
RELAYED_PEERS = (2, 4, 6)
DIRECT_PEERS = (1,) + RELAYED_PEERS


class _Piece(NamedTuple):
    src: int
    send: Callable
    slot: Callable
    land_shape: tuple
    own_src: tuple
    own_slot: tuple


HBM_SPEC = pl.BlockSpec(memory_space=pltpu.HBM)
SEM_SPEC = pl.BlockSpec(memory_space=pltpu.SEMAPHORE)
ANY_SPEC = pl.BlockSpec(memory_space=pl.ANY)


def _in_hbm(arrays):
    return [pltpu.with_memory_space_constraint(a, pltpu.HBM) for a in arrays]


def _hbm_like(arrays):
    return [pltpu.HBM(a.shape, a.dtype) for a in arrays]


def _rows_of(rows):
    return lambda ref, dev: ref.at[pl.ds(pl.multiple_of(dev * rows, 16), rows), :]


def _exchange_own(name, me, srcs, pieces):
    n = len(pieces)

    def body(me_ref, *refs):
        for i in range(n):
            refs[n + i][...] = refs[i][...]

    def spec(block_and_index):
        block, index = block_and_index
        return pl.BlockSpec(block, lambda i, me_ref: index(me_ref[0]))

    return pl.pallas_call(
        body,
        grid_spec=pltpu.PrefetchScalarGridSpec(
            num_scalar_prefetch=1, grid=(1,), in_specs=[spec(p.own_src) for p in pieces],
            out_specs=[spec(p.own_slot) for p in pieces]),
        out_shape=[jax.ShapeDtypeStruct(p.land_shape, BF) for p in pieces],
        compiler_params=_cparams(("arbitrary",)), name=name,
    )(me, *[srcs[p.src] for p in pieces])


def _exchange_start(name, srcs, lands, pieces, groups, peers=PEER_ORDER):
    n_src, n, n_g = len(srcs), len(pieces), len(groups)

    def body(*refs):
        src_refs, land_refs = refs[:n_src], refs[n_src:n_src + n]
        sems, token = refs[n_src + n:n_src + n + 2 * n_g], refs[-1]
        x, y, c, me = _my_position()
        for g, idxs in enumerate(groups):
            for k in peers:
                peer, pid = _peer(x, y, c, k)
                for j, i in enumerate(idxs):
                    p = pieces[i]
                    pltpu.make_async_remote_copy(
                        src_ref=p.send(src_refs[p.src], pid), dst_ref=p.slot(land_refs[i], me),
                        send_sem=sems[2 * g].at[(k - 1) * len(idxs) + j], recv_sem=sems[2 * g + 1].at[(k - 1) * len(idxs) + j],
                        device_id=peer, device_id_type=MESH).start()
        token[...] = jnp.zeros(token.shape, F32)

    sem_shapes = [pltpu.SemaphoreType.DMA(((N_DEV - 1) * len(idxs),)) for idxs in groups for _ in range(2)]
    res = pl.pallas_call(
        body, in_specs=[HBM_SPEC] * (n_src + n),
        out_specs=[SEM_SPEC] * (2 * n_g) + [HBM_SPEC] * (n_src + n) + [pl.BlockSpec(memory_space=pltpu.VMEM)],
        out_shape=sem_shapes + _hbm_like(srcs) + _hbm_like(lands) + [jax.ShapeDtypeStruct((8, LANES), F32)],
        input_output_aliases={i: 2 * n_g + i for i in range(n_src + n)},
        compiler_params=pltpu.CompilerParams(has_side_effects=pltpu.SideEffectType.DATAFLOW_SIDE_EFFECTING),
        name=name,
    )(*_in_hbm(srcs), *_in_hbm(lands))
    sems = [(res[2 * g], res[2 * g + 1]) for g in range(n_g)]
    return sems, list(res[2 * n_g:2 * n_g + n_src]), list(res[2 * n_g + n_src:2 * n_g + n_src + n]), res[-1]


def _relay_to_sibling(name, lands, pieces):
    n = len(lands)

    def body(*refs):
        land_refs, send_sems, recv_sems = refs[:n], refs[-2], refs[-1]
        x, y, c, me = _my_position()
        sibling, _ = _peer(x, y, c, 1)

        def passed_on(k_from, k_as, j):
            _, pid = _peer(x, y, c, k_from)
            slot = pieces[j].slot(land_refs[j], pid)
            at = RELAYED_PEERS.index(k_as - 1) * n + j
            return pltpu.make_async_remote_copy(src_ref=slot, dst_ref=slot, send_sem=send_sems.at[at], recv_sem=recv_sems.at[at],
                                                device_id=sibling, device_id_type=MESH)

        for k in RELAYED_PEERS:
            for j in range(n):
                passed_on(k, k + 1, j).start()
        for k in RELAYED_PEERS:
            for j in range(n):
                passed_on(k + 1, k + 1, j).wait_recv()
                passed_on(k, k + 1, j).wait_send()

    res = pl.pallas_call(
        body, in_specs=[HBM_SPEC] * n, out_specs=[HBM_SPEC] * n, out_shape=_hbm_like(lands),
        input_output_aliases={i: i for i in range(n)},
        scratch_shapes=[pltpu.SemaphoreType.DMA((len(RELAYED_PEERS) * n,))] * 2,
        compiler_params=pltpu.CompilerParams(has_side_effects=pltpu.SideEffectType.DATAFLOW_SIDE_EFFECTING),
        name=name,
    )(*_in_hbm(lands))
    return list(res)


def _exchange_wait(name, srcs, lands, pieces, waits, after, peers=PEER_ORDER):
    n_src, n, n_g = len(srcs), len(lands), len(waits)

    def body(*refs):
        src_refs, land_refs = refs[:n_src], refs[n_src:n_src + n]
        sems = refs[n_src + n:n_src + n + 2 * n_g]
        x, y, c, me = _my_position()
        at = 0
        for g, (idxs, _, _) in enumerate(waits):
            for k in peers:
                peer, pid = _peer(x, y, c, k)
                for j, i in enumerate(idxs):
                    p = pieces[i]
                    cp = pltpu.make_async_remote_copy(
                        src_ref=p.send(src_refs[p.src], pid), dst_ref=p.slot(land_refs[at + j], pid),
                        send_sem=sems[2 * g].at[(k - 1) * len(idxs) + j], recv_sem=sems[2 * g + 1].at[(k - 1) * len(idxs) + j],
                        device_id=peer, device_id_type=MESH)
                    cp.wait_send()
                    cp.wait_recv()
            at += len(idxs)

    sem_args = [s for _, send, recv in waits for s in (send, recv)]
    res = pl.pallas_call(
        body, in_specs=[HBM_SPEC] * (n_src + n) + [SEM_SPEC] * (2 * n_g) + [ANY_SPEC],
        out_specs=[HBM_SPEC] * (n_src + n), out_shape=_hbm_like(srcs) + _hbm_like(lands),
        input_output_aliases={i: i for i in range(n_src + n)},
        compiler_params=pltpu.CompilerParams(has_side_effects=pltpu.SideEffectType.DATAFLOW_SIDE_EFFECTING),
        name=name,
    )(*srcs, *lands, *sem_args, after)
    return list(res[:n_src]), list(res[n_src:])


def _weight_pieces(p_in, p_out, p_up, p_down):
    depth, cin, d = p_in.shape
    rout, hs = p_out.shape[1], p_up.shape[2]
    pieces = []
    for l in range(depth):
        whole = functools.partial(lambda ref, dev, l: ref.at[l], l=l)
        layer = functools.partial(lambda dev, l: (l, 0, 0), l=l)

        def rows(src, n_rows, whole=whole, layer=layer):
            return _Piece(src, whole, lambda ref, dev: _rows_of(n_rows)(ref.at[0], dev), (1, N_DEV * n_rows, d),
                          ((None, n_rows, d), layer), ((None, n_rows, d), lambda dev: (0, dev, 0)))

        pieces += [
            rows(0, cin), rows(1, rout),
            _Piece(2, whole, lambda ref, dev: ref.at[0, dev], (1, N_DEV, d, hs),
                   ((None, d, hs), layer), ((None, None, d, hs), lambda dev: (0, dev, 0, 0))),
            rows(3, hs),
        ]
    return pieces


def _grad_pieces(g_pair, kinds):
    pieces = []
    for i, (g, kind) in enumerate(zip(g_pair, kinds)):
        lead = lambda dev: (dev, 0, 0)
        if kind == "up":
            blk = ((None,) + g.shape[1:], lead)
            pieces.append(_Piece(i, lambda ref, dev: ref.at[dev], lambda ref, dev: ref.at[dev], g.shape, blk, blk))
        else:
            rows, cols = g.shape[0] // N_DEV, g.shape[1]
            pieces.append(_Piece(i, _rows_of(rows), lambda ref, dev: ref.at[dev], (N_DEV, rows, cols),
                                 ((rows, cols), lambda dev: (dev, 0)), ((None, rows, cols), lead)))
    return pieces


SMALL_W = 1024


def _all_reduce_small(mix8, mlp8, final8, attn8, lb8_last, lb_jac, hg8, loss8, after):
    def body(mix0, mix1, mlp0, mlp1, fin, attn0, attn1, lb, jac, hg0, hg1, loss, after_ref, o_ref, src_ref, buf_ref,
             send_sems, recv_sems):
        def total(ref):
            return jnp.sum(ref[...], axis=0, keepdims=True)

        dlb = total(lb)
        hg = jnp.concatenate([total(hg0), total(hg1), total(loss)], axis=1)
        src_ref[...] = jnp.concatenate([
            total(mix0), total(mix1), total(mlp0), total(mlp1), total(fin),
            jnp.concatenate([total(attn0), total(attn1)], axis=1),
            jnp.concatenate([jac[0:1, :] * dlb, jac[1:2, :] * dlb], axis=1),
            jnp.concatenate([hg, jnp.zeros((1, SMALL_W - hg.shape[1]), F32)], axis=1)], axis=0)
        x, y, c, me = _my_position()
        buf_ref[me] = src_ref[...]
        sends = []
        for k in PEER_ORDER:
            peer, _ = _peer(x, y, c, k)
            cp = pltpu.make_async_remote_copy(src_ref=src_ref, dst_ref=buf_ref.at[me], send_sem=send_sems.at[k - 1],
                                              recv_sem=recv_sems.at[k - 1], device_id=peer, device_id_type=MESH)
            cp.start()
            sends.append(cp)
        for k in PEER_ORDER:
            peer, pid = _peer(x, y, c, k)
            pltpu.make_async_remote_copy(src_ref=src_ref, dst_ref=buf_ref.at[pid], send_sem=send_sems.at[k - 1],
                                         recv_sem=recv_sems.at[k - 1], device_id=peer, device_id_type=MESH).wait_recv()
        for cp in sends:
            cp.wait_send()
        acc = buf_ref[0]
        for dev in range(1, N_DEV):
            acc = acc + buf_ref[dev]
        o_ref[...] = acc

    assert mix8[0].shape[1] == SMALL_W
    vm = pl.BlockSpec(memory_space=pltpu.VMEM)
    return pl.pallas_call(
        body, in_specs=[vm] * 12 + [ANY_SPEC], out_specs=vm, out_shape=jax.ShapeDtypeStruct((8, SMALL_W), F32),
        scratch_shapes=[pltpu.VMEM((8, SMALL_W), F32), pltpu.VMEM((N_DEV, 8, SMALL_W), F32),
                        pltpu.SemaphoreType.DMA((N_DEV - 1,)), pltpu.SemaphoreType.DMA((N_DEV - 1,))],
        name="all_reduce_small",
    )(*mix8, *mlp8, final8, *attn8, lb8_last, lb_jac, *hg8, loss8, after)


def _resident(block_shape, index_map):
    return pl.BlockSpec(block_shape, index_map, pipeline_mode=pl.Buffered(1))


def _fwd_inproj(layer, x, gain, g_in, cos, sin, tm=512):
    s, d = x.shape

    def body(x_ref, gain_ref, w_ref, cos_ref, sin_ref, proj_ref, h_ref):
        h = _rms_fwd(x_ref[...], gain_ref[...]).astype(BF)
        h_ref[...] = h
        cs = _tile_lanes(cos_ref[...], SEG // LANES)
        sn = _tile_lanes(sin_ref[...], SEG // LANES)
        for seg in range(N_SEG):
            acc = _dot(h, w_ref[seg * SEG:(seg + 1) * SEG, :], NT)
            if seg < 2:
                acc = acc * cs + _rope_partner(acc) * sn
            if seg == 0:
                acc = acc * ATTN_SCALE
            proj_ref[:, seg * SEG:(seg + 1) * SEG] = acc

    return pl.pallas_call(
        body, grid=(s // tm,),
        in_specs=[pl.BlockSpec((tm, d), lambda i: (i, 0)), pl.BlockSpec((None, 1, d), lambda i: (layer, 0, 0)),
                  _resident((None, PROJ_W, d), lambda i: (0, 0, 0)),
                  pl.BlockSpec((tm, LANES), lambda i: (i, 0)), pl.BlockSpec((tm, LANES), lambda i: (i, 0))],
        out_specs=[pl.BlockSpec((tm, PROJ_W), lambda i: (i, 0)), pl.BlockSpec((tm, d), lambda i: (i, 0))],
        out_shape=[jax.ShapeDtypeStruct((s, PROJ_W), F32), jax.ShapeDtypeStruct((s, d), BF)],
        compiler_params=_cparams(("parallel",)), name=f"fwd_inproj_l{layer}",
    )(x, gain, g_in, cos, sin)


ATTN_UNIT = SPAN * max(DILATIONS)
ATTN_GROUP = 4


def _attn_masks(first_block_has_prev):
    row = lax.broadcasted_iota(jnp.int32, (SPAN, 2 * SPAN), 0)
    col = lax.broadcasted_iota(jnp.int32, (SPAN, 2 * SPAN), 1)
    band = (col >= row) & (col <= row + SPAN)
    lane = lax.broadcasted_iota(jnp.int32, (SPAN, LANES), 1)
    return band & ((col >= SPAN) | first_block_has_prev), band, lane < 64


def _attn_specs(n_in_extra, unit_of=lambda n: n):
    pairs = ATTN_W // LANES
    q_spec = pl.BlockSpec((ATTN_UNIT, LANES), lambda p, n: (unit_of(n), p))

    def prev(seg):
        return pl.BlockSpec((ATTN_UNIT, LANES), lambda p, n: (jnp.maximum(unit_of(n) - 1, 0), seg * pairs + p))

    def cur(seg):
        return pl.BlockSpec((ATTN_UNIT, LANES), lambda p, n: (unit_of(n), seg * pairs + p))

    return [q_spec, prev(1), cur(1), prev(2), cur(2)] + [q_spec] * n_in_extra


def _attn_groups(dil):
    blocks = ATTN_UNIT // (SPAN * dil)
    pairs = [(r, b) for r in range(dil) for b in range(blocks)]
    return [pairs[i:i + ATTN_GROUP] for i in range(0, len(pairs), ATTN_GROUP)]


def _block_rows(dil, r, b, n=1):
    start = r + dil * SPAN * b
    return pl.ds(start, n * SPAN, stride=dil) if dil > 1 else pl.ds(start, n * SPAN)


def _block_keys(prev_ref, cur_ref, dil, r, b):
    if b > 0:
        return cur_ref[_block_rows(dil, r, b - 1, 2), :]
    last = ATTN_UNIT // (SPAN * dil) - 1
    return jnp.concatenate([prev_ref[_block_rows(dil, r, last), :], cur_ref[_block_rows(dil, r, 0), :]], axis=0)


def _attn_fwd(layer, proj):
    s = proj.shape[0]
    n_pat = len(DILATIONS)
    merge_rows = 256

    def body(q_ref, kp_ref, kc_ref, vp_ref, vc_ref, o_ref, lse_ref, o_scr, lse_scr):
        m_first, m_rest, is_a = _attn_masks(pl.program_id(1) > 0)
        sels = (is_a, jnp.logical_not(is_a))
        is_a_keys = lax.broadcasted_iota(jnp.int32, (2 * SPAN, LANES), 1) < 64
        for pi, dil in enumerate(DILATIONS):
            for group in _attn_groups(dil):
                items = [(r, b, h) for r, b in group for h in range(2)]
                q = {rb: q_ref[_block_rows(dil, *rb), :] for rb in group}
                k = {rb: _block_keys(kp_ref, kc_ref, dil, *rb).astype(BF) for rb in group}
                v = {rb: _block_keys(vp_ref, vc_ref, dil, *rb).astype(BF) for rb in group}
                v_sum = {rb: (jnp.where(is_a_keys, v[rb], 1.0), jnp.where(is_a_keys, 1.0, v[rb])) for rb in group}
                sc = [jnp.where(m_first if b == 0 else m_rest,
                                _dot(jnp.where(sels[h], q[r, b], 0.0).astype(BF), k[r, b], NT), MASK_VALUE)
                      for r, b, h in items]
                mx = [jnp.max(jnp.maximum(t[:, :SPAN], t[:, SPAN:]), axis=-1, keepdims=True) for t in sc]
                p = [jnp.exp(t - m).astype(BF) for t, m in zip(sc, mx)]
                both = [_dot(t, v_sum[r, b][h], NN) for t, (r, b, h) in zip(p, items)]
                for j, (r, b) in enumerate(group):
                    t_a, t_b = both[2 * j], both[2 * j + 1]
                    den = pltpu.roll(jnp.where(is_a, t_b, t_a), 64, 1)
                    o_scr[pi, _block_rows(dil, r, b), :] = jnp.where(is_a, t_a, t_b) / den
                    lse_scr[pi, _block_rows(dil, r, b), :] = jnp.where(is_a, mx[2 * j], mx[2 * j + 1]) + jnp.log(den)
        for c in range(ATTN_UNIT // merge_rows):
            rows = slice(c * merge_rows, (c + 1) * merge_rows)
            ls = [lse_scr[pi, rows, :] for pi in range(n_pat)]
            mx = functools.reduce(jnp.maximum, ls)
            ws = [jnp.exp(l - mx) for l in ls]
            den = functools.reduce(jnp.add, ws)
            o_ref[rows, :] = functools.reduce(jnp.add, [w * o_scr[pi, rows, :] for pi, w in enumerate(ws)]) / den
            lse_ref[rows, :] = mx + jnp.log(den)

    out_spec = pl.BlockSpec((ATTN_UNIT, LANES), lambda p, n: (n, p))
    return pl.pallas_call(
        body, grid=(ATTN_W // LANES, s // ATTN_UNIT), in_specs=_attn_specs(0), out_specs=[out_spec, out_spec],
        out_shape=[jax.ShapeDtypeStruct((s, ATTN_W), F32)] * 2,
        scratch_shapes=[pltpu.VMEM((n_pat, ATTN_UNIT, LANES), F32)] * 2,
        compiler_params=_cparams(("parallel", "arbitrary")), name=f"attn_fwd_l{layer}",
    )(proj, proj, proj, proj, proj)


def _attn_norm(layer, o, gain, mixed, tm=512):
    s = o.shape[0]

    def body(o_ref, gain_ref, mixed_ref, n_ref):
        n_ref[...] = _rms_fwd(o_ref[...], gain_ref[...]).astype(BF)

    blk = pl.BlockSpec((tm, ATTN_W), lambda i: (i, 0))
    return pl.pallas_call(
        body, grid=(s // tm,),
        in_specs=[blk, pl.BlockSpec((None, 1, ATTN_W), lambda i: (layer, 0, 0)), pl.BlockSpec(memory_space=pl.ANY)],
        out_specs=blk, out_shape=jax.ShapeDtypeStruct(mixed.shape, BF), input_output_aliases={2: 0},
        compiler_params=_cparams(("parallel",)), name=f"attn_norm_l{layer}",
    )(o, gain, mixed)


def _chunk_cumsum(x, reverse=False):
    n = x.shape[0]
    pos = lax.broadcasted_iota(jnp.int32, x.shape, 0) % HGRN_CHUNK
    for sh in (1, 2, 4, 8):
        if reverse:
            x = x + jnp.where(pos < HGRN_CHUNK - sh, pltpu.roll(x, n - sh, 0), 0.0)
        else:
            x = x + jnp.where(pos >= sh, pltpu.roll(x, sh, 0), 0.0)
    return x


def _chunk_row(x, row):
    r, n = x.shape
    x3 = x.reshape(r // HGRN_CHUNK, HGRN_CHUNK, n)
    return jnp.broadcast_to(x3[:, row:row + 1, :], x3.shape).reshape(r, n)


def _hgrn_pre(qh, z, lb):
    sig = _sigmoid(z)
    f = lb + (1.0 - lb) * sig
    k = 1.0 - f
    sq = _sigmoid(qh)
    q = qh * sq * HGRN_SCALE
    g = _chunk_cumsum(jnp.log(f))
    g_mid = _chunk_row(g, HGRN_CHUNK // 2 - 1)
    g_last = _chunk_row(g, HGRN_CHUNK - 1)
    e_q, e_k = jnp.exp(g - g_mid), jnp.exp(g_mid - g)
    e_in, e_out = jnp.exp(g), jnp.exp(g_last - g)
    return dict(sig=sig, f=f, k=k, sq=sq, q=q, g_last=g_last, e_q=e_q, e_k=e_k, e_in=e_in, e_out=e_out,
                qt=q * e_q, kt=k * e_k, qg=q * e_in, kout=k * e_out)


def _hgrn_mask():
    row = lax.broadcasted_iota(jnp.int32, (LANES, LANES), 0)
    col = lax.broadcasted_iota(jnp.int32, (LANES, LANES), 1)
    return (row // HGRN_CHUNK == col // HGRN_CHUNK) & (col <= row)


def _hgrn_in_specs(layer, rev, nblk):
    def blk(b):
        return nblk - 1 - b if rev else b
    first = 3 * ATTN_W // HGRN_W
    specs = [pl.BlockSpec((HGRN_ROWS, HGRN_W), functools.partial(lambda b, seg: (blk(b), first + seg), seg=seg))
             for seg in range(4)]
    specs.append(pl.BlockSpec((None, 1, HGRN_W), lambda b: (layer, 0, 0)))
    specs.append(pl.BlockSpec((None, 1, HGRN_DIM), lambda b: (layer, 0, 0)))
    return specs, blk


def _head(x, h):
    return x[:, h * HGRN_DIM:(h + 1) * HGRN_DIM]


def _chunk(x, c):
    return x[c * HGRN_CHUNK:(c + 1) * HGRN_CHUNK]


def _sub(x, sb):
    return x[sb * LANES:(sb + 1) * LANES]


HGRN_ROWS = 256
HEADS = range(HGRN_HEADS)
SUBS = range(HGRN_ROWS // LANES)
CHUNKS = range(HGRN_ROWS // HGRN_CHUNK)


def _hgrn_fwd(layer, proj, lb, gain):
    s = proj.shape[0]
    nblk = s // HGRN_ROWS
    cpb = len(CHUNKS)

    def body(q_ref, f_ref, i_ref, g_ref, lb_ref, gain_ref, o_ref, rec_ref, st_ref, state):
        @pl.when(pl.program_id(0) == 0)
        def _():
            state[...] = jnp.zeros(state.shape, F32)

        pre = _hgrn_pre(q_ref[...], f_ref[...], lb_ref[...])
        v = i_ref[...].astype(BF)
        qt, kt, qg, kout = (pre[n].astype(BF) for n in ("qt", "kt", "qg", "kout"))
        dec = jnp.exp(pre["g_last"])
        mask = _hgrn_mask()
        a = [[jnp.where(mask, _dot(_sub(_head(qt, h), sb), _sub(_head(kt, h), sb), NT), 0.0).astype(BF) for sb in SUBS]
             for h in HEADS]
        o_intra = [[_dot(a[h][sb], _sub(_head(v, h), sb), NN) for sb in SUBS] for h in HEADS]
        update = [[_dot(_chunk(_head(v, h), c), _chunk(_head(kout, h), c), TN) for c in CHUNKS] for h in HEADS]
        for h in HEADS:
            st = state[h]
            for c in CHUNKS:
                st_ref[h, c * LANES:(c + 1) * LANES, :] = st.astype(BF)
                st = st * _head(dec, h)[c * HGRN_CHUNK:c * HGRN_CHUNK + 1, :] + update[h][c]
            state[h] = st
        inter = [[_dot(_chunk(_head(qg, h), c), st_ref[h, c * LANES:(c + 1) * LANES, :].astype(BF), NT) for c in CHUNKS]
                 for h in HEADS]
        o = [jnp.concatenate(o_intra[h], axis=0) + jnp.concatenate(inter[h], axis=0) for h in HEADS]
        o_ref[...] = jnp.concatenate(o, axis=1)
        gate = g_ref[...]
        normed = jnp.concatenate([_rms_fwd(o[h], gain_ref[...]) for h in HEADS], axis=1)
        rec_ref[...] = (normed * (gate * _sigmoid(gate))).astype(BF)

    specs, _ = _hgrn_in_specs(layer, False, nblk)
    return pl.pallas_call(
        body, grid=(nblk,), in_specs=specs,
        out_specs=[pl.BlockSpec((HGRN_ROWS, HGRN_W), lambda b: (b, 0)), pl.BlockSpec((HGRN_ROWS, HGRN_W), lambda b: (b, 1)),
                   pl.BlockSpec((HGRN_HEADS, cpb * LANES, LANES), lambda b: (0, b, 0))],
        out_shape=[jax.ShapeDtypeStruct((s, HGRN_W), F32), jax.ShapeDtypeStruct((s, MIX_W), BF),
                   jax.ShapeDtypeStruct((HGRN_HEADS, nblk * cpb * LANES, LANES), BF)],
        scratch_shapes=[pltpu.VMEM((HGRN_HEADS, LANES, LANES), F32)],
        compiler_params=_cparams(("arbitrary",)), name=f"hgrn_fwd_l{layer}",
    )(proj, proj, proj, proj, lb, gain)


def _relu2(u):
    return jnp.square(jnp.maximum(u, 0)).astype(BF)


def _mlp_fwd(layer, x, mixed, gain, g_out, g_up, g_down, head=None):
    s, d = x.shape
    mw = mixed.shape[1]
    nblk, hs = g_up.shape[1], g_up.shape[3]
    tm = 512

    def body(x_ref, m_ref, gain_ref, out_w_ref, up_ref, down_ref, *refs):
        if head:
            fin_ref, t_ref, o_ref, mid_ref, u_ref, h_ref, ob_ref, dfin_ref, loss_ref, a_buf = refs
        else:
            o_ref, mid_ref, u_ref, h_ref, a_buf = refs
        xv = x_ref[...] + _dot(m_ref[...], out_w_ref[...], NN)
        mid_ref[...] = xv
        h = _rms_fwd(xv, gain_ref[...]).astype(BF)
        h_ref[...] = h
        for j in range(nblk):
            u = _dot(h, up_ref[j], NN)
            u_ref[:, j * hs:(j + 1) * hs] = u.astype(BF)
            a_buf[:, j * hs:(j + 1) * hs] = _relu2(u)
        acc = xv
        for j in range(nblk):
            acc = acc + _dot(a_buf[:, j * hs:(j + 1) * hs], down_ref[j * hs:(j + 1) * hs, :], NN)
        if not head:
            o_ref[...] = acc
            return
        fin = fin_ref[...]
        err = _rms_fwd(acc, fin) - t_ref[...]
        dout, dfin = _rms_bwd(err * (1.0 / d), acc, fin)
        o_ref[...] = dout
        ob_ref[...] = dout.astype(BF)
        step = pl.program_id(0)
        _accumulate_rows(step, dfin_ref, _part8(dfin))
        _accumulate_rows(step, loss_ref, _part8(0.5 * jnp.mean(err * err, axis=-1, keepdims=True) * jnp.ones((1, LANES), F32)))

    row = pl.BlockSpec((tm, d), lambda i: (i, 0))
    in_specs = [row, pl.BlockSpec((tm, mw), lambda i: (i, 0)), pl.BlockSpec((None, 1, d), lambda i: (layer, 0, 0)),
                _resident((None, mw, d), lambda i: (0, 0, 0)),
                _resident((None, nblk, d, hs), lambda i: (0, 0, 0, 0)),
                _resident((None, nblk * hs, d), lambda i: (0, 0, 0))]
    out_specs = [row, row, pl.BlockSpec((tm, nblk * hs), lambda i: (i, 0)), row]
    out_shape = [jax.ShapeDtypeStruct((s, d), F32), jax.ShapeDtypeStruct((s, d), F32),
                 jax.ShapeDtypeStruct((s, nblk * hs), BF), jax.ShapeDtypeStruct((s, d), BF)]
    args = [x, mixed, gain, g_out, g_up, g_down]
    if head:
        in_specs += [pl.BlockSpec((1, d), lambda i: (0, 0)), row]
        out_specs += [row, pl.BlockSpec((8, d), lambda i: (0, 0)), pl.BlockSpec((8, LANES), lambda i: (0, 0))]
        out_shape += [jax.ShapeDtypeStruct((s, d), BF), jax.ShapeDtypeStruct((8, d), F32), jax.ShapeDtypeStruct((8, LANES), F32)]
        args += list(head)
    return pl.pallas_call(
        body, grid=(s // tm,), in_specs=in_specs, out_specs=out_specs, out_shape=out_shape,
        scratch_shapes=[pltpu.VMEM((tm, nblk * hs), BF)],
        compiler_params=_cparams(("arbitrary",) if head else ("parallel",)), name=f"mlp_fwd_l{layer}",
    )(*args)


def _accumulate_rows(i, ref, part):
    @pl.when(i == 0)
    def _():
        ref[...] = part

    @pl.when(i > 0)
    def _():
        ref[...] += part


def _head_sums(prod):
    row = lax.broadcasted_iota(jnp.int32, (ATTN_W, ATTN_W), 0)
    col = lax.broadcasted_iota(jnp.int32, (ATTN_W, ATTN_W), 1)
    same_head = jnp.where(row // 64 == col // 64, 1.0, 0.0).astype(BF)
    high = prod.astype(BF)
    low = (prod - high.astype(F32)).astype(BF)
    return _dot(high, same_head, NN) + _dot(low, same_head, NN)


def _mlp_bwd(layer, dx, dxb, x, gain, u, o_attn, gain_attn, g_out, g_up, g_down, tm=256):
    s, d = x.shape
    mw = g_out.shape[1]
    nblk, hs = g_up.shape[1], g_up.shape[3]

    def body(dx_ref, dxb_ref, x_ref, gain_ref, u_ref, oa_ref, ga_ref, out_w_ref, up_ref, down_ref, o_ref, ob_ref, du_ref,
             drec_ref, do_ref, delta_ref, dgain_ref, dattn_ref):
        dxb_v = dxb_ref[...]
        for j in range(nblk):
            cols = slice(j * hs, (j + 1) * hs)
            da = _dot(dxb_v, down_ref[cols, :], NT)
            du_ref[:, cols] = (da * (2.0 * jnp.maximum(u_ref[:, cols].astype(F32), 0.0))).astype(BF)
        acc = jnp.zeros((tm, d), F32)
        for j in range(nblk):
            acc = acc + _dot(du_ref[:, j * hs:(j + 1) * hs], up_ref[j], NT)
        dxn, dgain = _rms_bwd(acc, x_ref[...], gain_ref[...])
        out = dx_ref[...] + dxn
        out_b = out.astype(BF)
        o_ref[...] = out
        ob_ref[...] = out_b
        dm = _dot(out_b, out_w_ref[...], NT)
        drec_ref[...] = dm[:, ATTN_W:]
        ov = oa_ref[...]
        do, dattn = _rms_bwd(dm[:, :ATTN_W], ov, ga_ref[...])
        do_ref[...] = do
        delta_ref[...] = _head_sums(do * ov)
        _accumulate_rows(pl.program_id(0), dgain_ref, _part8(dgain))
        _accumulate_rows(pl.program_id(0), dattn_ref, _part8(dattn))

    row = pl.BlockSpec((tm, d), lambda i: (i, 0))
    wide = pl.BlockSpec((tm, nblk * hs), lambda i: (i, 0))
    half = pl.BlockSpec((tm, ATTN_W), lambda i: (i, 0))
    return pl.pallas_call(
        body, grid=(s // tm,),
        in_specs=[row, row, row, pl.BlockSpec((None, 1, d), lambda i: (layer, 0, 0)), wide,
                  half, pl.BlockSpec((None, 1, ATTN_W), lambda i: (layer, 0, 0)),
                  _resident((None, mw, d), lambda i: (0, 0, 0)),
                  _resident((None, nblk, d, hs), lambda i: (0, 0, 0, 0)),
                  _resident((None, nblk * hs, d), lambda i: (0, 0, 0))],
        out_specs=[row, row, wide, half, half, half, pl.BlockSpec((8, d), lambda i: (0, 0)),
                   pl.BlockSpec((8, ATTN_W), lambda i: (0, 0))],
        out_shape=[jax.ShapeDtypeStruct((s, d), F32), jax.ShapeDtypeStruct((s, d), BF),
                   jax.ShapeDtypeStruct((s, nblk * hs), BF)] + [jax.ShapeDtypeStruct((s, ATTN_W), F32)] * 3
        + [jax.ShapeDtypeStruct((8, d), F32), jax.ShapeDtypeStruct((8, ATTN_W), F32)],
        compiler_params=_cparams(("arbitrary",)), name=f"mlp_bwd_l{layer}",
    )(dx, dxb, x, gain, u, o_attn, gain_attn, g_out, g_up, g_down)


def _attn_bwd(layer, proj, do, lse, delta, cos, sin, after):
    s = proj.shape[0]
    n_units = s // ATTN_UNIT
    out_rows = 256

    def unit_of(n):
        return n_units - 1 - n

    def body(q_ref, kp_ref, kc_ref, vp_ref, vc_ref, do_ref, lse_ref, delta_ref, cos_ref, sin_ref, after_ref, out_ref,
             dq_ref, dk_ref, dkp_ref, dv_ref, dvp_ref, carry_k, carry_v):
        step = pl.program_id(1)
        m_first, m_rest, is_a = _attn_masks(unit_of(step) > 0)
        sels = (is_a, jnp.logical_not(is_a))
        for ref in (dq_ref, dk_ref, dkp_ref, dv_ref, dvp_ref):
            ref[...] = jnp.zeros(ref.shape, F32)
        for dil in DILATIONS:
            last = ATTN_UNIT // (SPAN * dil) - 1
            for group in _attn_groups(dil):
                items = [(r, b, h) for r, b in group for h in range(2)]
                q = {rb: q_ref[_block_rows(dil, *rb), :] for rb in group}
                dov = {rb: do_ref[_block_rows(dil, *rb), :] for rb in group}
                lse_v = {rb: lse_ref[_block_rows(dil, *rb), :] for rb in group}
                delta_v = {rb: delta_ref[_block_rows(dil, *rb), :] for rb in group}
                k = {rb: _block_keys(kp_ref, kc_ref, dil, *rb).astype(BF) for rb in group}
                v = {rb: _block_keys(vp_ref, vc_ref, dil, *rb).astype(BF) for rb in group}
                qh = [jnp.where(sels[h], q[r, b], 0.0).astype(BF) for r, b, h in items]
                doh = [jnp.where(sels[h], dov[r, b], 0.0).astype(BF) for r, b, h in items]
                sc = [jnp.where(m_first if b == 0 else m_rest, _dot(qh[i], k[r, b], NT), MASK_VALUE)
                      for i, (r, b, h) in enumerate(items)]
                p = [jnp.exp(sc[i] - lse_v[r, b][:, 64 * h:64 * h + 1]) for i, (r, b, h) in enumerate(items)]
                ds = [(p[i] * (_dot(doh[i], v[r, b], NT) - delta_v[r, b][:, 64 * h:64 * h + 1])).astype(BF)
                      for i, (r, b, h) in enumerate(items)]
                dv = [_dot(p[i].astype(BF), doh[i], TN) for i in range(len(items))]
                dq = [_dot(ds[i], k[r, b], NN) for i, (r, b, h) in enumerate(items)]
                dk = [_dot(ds[i], qh[i], TN) for i in range(len(items))]
                for j, (r, b) in enumerate(group):
                    own = _block_rows(dil, r, b)
                    dq_ref[own, :] += jnp.where(is_a, dq[2 * j], dq[2 * j + 1])
                    dk2, dv2 = dk[2 * j] + dk[2 * j + 1], dv[2 * j] + dv[2 * j + 1]
                    dk_ref[own, :] += dk2[SPAN:]
                    dv_ref[own, :] += dv2[SPAN:]
                    if b > 0:
                        before = _block_rows(dil, r, b - 1)
                        dk_ref[before, :] += dk2[:SPAN]
                        dv_ref[before, :] += dv2[:SPAN]
                    else:
                        before = _block_rows(dil, r, last)
                        dkp_ref[before, :] += dk2[:SPAN]
                        dvp_ref[before, :] += dv2[:SPAN]
        has_next = step > 0
        for c in range(ATTN_UNIT // out_rows):
            rows = slice(c * out_rows, (c + 1) * out_rows)
            cs, sn = cos_ref[rows, :], sin_ref[rows, :]
            dqv = dq_ref[rows, :]
            dkv = dk_ref[rows, :] + jnp.where(has_next, carry_k[rows, :], 0.0)
            dvv = dv_ref[rows, :] + jnp.where(has_next, carry_v[rows, :], 0.0)
            out_ref[0, rows, :] = ((dqv * cs - _rope_partner(dqv) * sn) * ATTN_SCALE).astype(BF)
            out_ref[1, rows, :] = (dkv * cs - _rope_partner(dkv) * sn).astype(BF)
            out_ref[2, rows, :] = dvv.astype(BF)
        carry_k[...] = dkp_ref[...]
        carry_v[...] = dvp_ref[...]

    tab = pl.BlockSpec((ATTN_UNIT, LANES), lambda p, n: (unit_of(n), 0))
    return pl.pallas_call(
        body, grid=(ATTN_W // LANES, n_units), in_specs=_attn_specs(3, unit_of) + [tab, tab, ANY_SPEC],
        out_specs=pl.BlockSpec((3, ATTN_UNIT, LANES), lambda p, n: (0, unit_of(n), p)),
        out_shape=jax.ShapeDtypeStruct((3, s, ATTN_W), BF),
        scratch_shapes=[pltpu.VMEM((ATTN_UNIT, LANES), F32)] * 7,
        compiler_params=_cparams(("parallel", "arbitrary")), name=f"attn_bwd_l{layer}",
    )(proj, proj, proj, proj, proj, do, lse, delta, cos, sin, after)


def _hgrn_bwd(layer, proj, lb, gain, o, drec, states):
    s = proj.shape[0]
    nblk = s // HGRN_ROWS
    cpb = len(CHUNKS)

    def body(q_ref, f_ref, i_ref, g_ref, lb_ref, gain_ref, o_ref, drec_ref, st_ref, dseg_ref, dlb_ref, dgain_ref,
             dstate, dst_buf):
        step = pl.program_id(0)

        @pl.when(step == 0)
        def _():
            dstate[...] = jnp.zeros(dstate.shape, F32)

        lbv, gv = lb_ref[...], gain_ref[...]
        qh, z, gate_in = q_ref[...], f_ref[...], g_ref[...]
        pre = _hgrn_pre(qh, z, lbv)
        v = i_ref[...].astype(BF)
        sg = _sigmoid(gate_in)
        ov, drec = o_ref[...], drec_ref[...]
        dnormed = drec * (gate_in * sg)
        back = [_rms_bwd(_head(dnormed, h), _head(ov, h), gv) for h in HEADS]
        do_b = jnp.concatenate([b[0] for b in back], axis=1).astype(BF)
        dgain = back[0][1] + back[1][1] + back[2][1] + back[3][1]
        normed = jnp.concatenate([_rms_fwd(_head(ov, h), gv) for h in HEADS], axis=1)
        dgate_in = drec * normed * (sg * (1.0 + gate_in * (1.0 - sg)))
        mask = _hgrn_mask()
        qt, kt, qg, kout = (pre[n].astype(BF) for n in ("qt", "kt", "qg", "kout"))
        dec = jnp.exp(pre["g_last"])
        def intra(fn):
            return jnp.concatenate([jnp.concatenate([fn(h, sb) for sb in SUBS], axis=0) for h in HEADS], axis=1)

        def hs(x, h, sb):
            return _sub(_head(x, h), sb)

        a = [[jnp.where(mask, _dot(hs(qt, h, sb), hs(kt, h, sb), NT), 0.0).astype(BF) for sb in SUBS] for h in HEADS]
        da = [[jnp.where(mask, _dot(hs(do_b, h, sb), hs(v, h, sb), NT), 0.0).astype(BF) for sb in SUBS] for h in HEADS]
        dv_intra = intra(lambda h, sb: _dot(a[h][sb], hs(do_b, h, sb), TN))
        dqt = intra(lambda h, sb: _dot(da[h][sb], hs(kt, h, sb), NN))
        dkt = intra(lambda h, sb: _dot(da[h][sb], hs(qt, h, sb), TN))
        feed = [[_dot(_chunk(_head(do_b, h), c), _chunk(_head(qg, h), c), TN) for c in CHUNKS] for h in HEADS]
        for h in HEADS:
            dst = dstate[h]
            for c in reversed(CHUNKS):
                dst_buf[h, c * LANES:(c + 1) * LANES, :] = dst
                dst = dst * _head(dec, h)[c * HGRN_CHUNK:c * HGRN_CHUNK + 1, :] + feed[h][c]
            dstate[h] = dst

        def per_chunk(fn):
            cols = []
            for h in HEADS:
                rows = [jnp.broadcast_to(t, (HGRN_CHUNK, HGRN_DIM)) for t in (fn(h, c) for c in CHUNKS)]
                cols.append(jnp.concatenate(rows, axis=0))
            return jnp.concatenate(cols, axis=1)

        def st_prev(h, c):
            return st_ref[h, c * LANES:(c + 1) * LANES, :]

        def dst_at(h, c):
            return dst_buf[h, c * LANES:(c + 1) * LANES, :]

        dqg = per_chunk(lambda h, c: _dot(_chunk(_head(do_b, h), c), st_prev(h, c).astype(BF), NN))
        dkout = per_chunk(lambda h, c: _dot(_chunk(_head(v, h), c), dst_at(h, c).astype(BF), NN))
        dv_inter = per_chunk(lambda h, c: _dot(_chunk(_head(kout, h), c), dst_at(h, c).astype(BF), NT))
        dg_state = per_chunk(lambda h, c: jnp.sum(dst_at(h, c) * st_prev(h, c).astype(F32), axis=0, keepdims=True))
        dg_kout = per_chunk(lambda h, c: jnp.sum(_chunk(_head(dkout * pre["kout"], h), c), axis=0, keepdims=True))
        dv = dv_intra + dv_inter
        pos = lax.broadcasted_iota(jnp.int32, (HGRN_ROWS, HGRN_W), 0) % HGRN_CHUNK
        dq = dqt * pre["e_q"] + dqg * pre["e_in"]
        dk = dkt * pre["e_k"] + dkout * pre["e_out"]
        dg = (dqt * pre["qt"] - dkt * pre["kt"] + dqg * pre["qg"] - dkout * pre["kout"]
              + jnp.where(pos == HGRN_CHUNK - 1, dg_state * dec + dg_kout, 0.0))
        dlogf = _chunk_cumsum(dg, reverse=True)
        sig, sq = pre["sig"], pre["sq"]
        df = dlogf / pre["f"] - dk
        dseg_ref[0] = (dq * HGRN_SCALE * (sq * (1.0 + qh * (1.0 - sq)))).astype(BF)
        dseg_ref[1] = (df * (1.0 - lbv) * sig * (1.0 - sig)).astype(BF)
        dseg_ref[2] = dv.astype(BF)
        dseg_ref[3] = dgate_in.astype(BF)
        _accumulate_rows(step, dlb_ref, _part8(df * (1.0 - sig)))
        _accumulate_rows(step, dgain_ref, _part8(dgain))

    specs, blk = _hgrn_in_specs(layer, True, nblk)
    specs += [pl.BlockSpec((HGRN_ROWS, HGRN_W), lambda b: (blk(b), 0)),
              pl.BlockSpec((HGRN_ROWS, HGRN_W), lambda b: (blk(b), 0)),
              pl.BlockSpec((HGRN_HEADS, cpb * LANES, LANES), lambda b: (0, blk(b), 0))]
    return pl.pallas_call(
        body, grid=(nblk,), in_specs=specs,
        out_specs=[pl.BlockSpec((4, HGRN_ROWS, HGRN_W), lambda b: (0, blk(b), 0)),
                   pl.BlockSpec((8, HGRN_W), lambda b: (0, 0)), pl.BlockSpec((8, HGRN_DIM), lambda b: (0, 0))],
        out_shape=[jax.ShapeDtypeStruct((4, s, HGRN_W), BF), jax.ShapeDtypeStruct((8, HGRN_W), F32),
                   jax.ShapeDtypeStruct((8, HGRN_DIM), F32)],
        scratch_shapes=[pltpu.VMEM((HGRN_HEADS, LANES, LANES), F32), pltpu.VMEM((HGRN_HEADS, cpb * LANES, LANES), F32)],
        compiler_params=_cparams(("arbitrary",)), name=f"hgrn_bwd_l{layer}",
    )(proj, proj, proj, proj, lb, gain, o, drec, states)


def _bwd_inproj(layer, dqkv, dhg, g_in, x, gain, dres, tm=512):
    s, d = x.shape

    def body(dqkv_ref, dhg_ref, w_ref, x_ref, gain_ref, dres_ref, dx_ref, dxb_ref, dgain_ref):
        acc = jnp.zeros((tm, d), F32)
        for seg in range(N_SEG):
            a = dqkv_ref[seg] if seg < 3 else dhg_ref[seg - 3]
            acc = acc + _dot(a, w_ref[seg * SEG:(seg + 1) * SEG, :], NN)
        dx, dgain = _rms_bwd(acc, x_ref[...], gain_ref[...])
        out = dres_ref[...] + dx
        dx_ref[...] = out
        dxb_ref[...] = out.astype(BF)
        _accumulate_rows(pl.program_id(0), dgain_ref, _part8(dgain))

    row = pl.BlockSpec((tm, d), lambda i: (i, 0))
    return pl.pallas_call(
        body, grid=(s // tm,),
        in_specs=[pl.BlockSpec((3, tm, SEG), lambda i: (0, i, 0)), pl.BlockSpec((4, tm, SEG), lambda i: (0, i, 0)),
                  _resident((None, PROJ_W, d), lambda i: (0, 0, 0)), row,
                  pl.BlockSpec((None, 1, d), lambda i: (layer, 0, 0)), row],
        out_specs=[row, row, pl.BlockSpec((8, d), lambda i: (0, 0))],
        out_shape=[jax.ShapeDtypeStruct((s, d), F32), jax.ShapeDtypeStruct((s, d), BF), jax.ShapeDtypeStruct((8, d), F32)],
        compiler_params=_cparams(("arbitrary",)), name=f"bwd_inproj_l{layer}",
    )(dqkv, dhg, g_in, x, gain, dres)


def _adamw(w, g, m, v):
    m2 = ADAM_B1 * m + (1.0 - ADAM_B1) * g
    v2 = ADAM_B2 * v + (1.0 - ADAM_B2) * (g * g)
    m_hat = m2 / (1.0 - ADAM_B1 ** ADAM_STEP)
    v_hat = v2 / (1.0 - ADAM_B2 ** ADAM_STEP)
    delta = -ADAM_LR * (m_hat / (jnp.sqrt(v_hat) + ADAM_EPS) + ADAM_WD * w)
    return delta, m2, v2


def _adam_big(name, parts, w, m, v, row_tiles):
    depth = w.shape[0]
    r, c = parts[0].shape[1], parts[0].shape[2]
    tr = r // row_tiles
    p_spec = pl.BlockSpec((N_DEV, tr, c), lambda t: (0, t, 0))
    w_spec = pl.BlockSpec((depth, tr, c), lambda t: (0, t, 0))

    def body(*refs):
        p_refs = refs[:depth]
        w_ref, m_ref, v_ref, g_ref, d_ref, m2_ref, v2_ref, token = refs[depth:]
        token[...] = jnp.zeros(token.shape, F32)
        for l in range(depth):
            g = p_refs[l][0].astype(F32)
            for dev in range(1, N_DEV):
                g = g + p_refs[l][dev].astype(F32)
            delta, m2, v2 = _adamw(w_ref[l], g, m_ref[l], v_ref[l])
            g_ref[l] = g
            d_ref[l] = delta
            m2_ref[l] = m2
            v2_ref[l] = v2

    return pl.pallas_call(
        body, grid=(row_tiles,), in_specs=[p_spec] * depth + [w_spec] * 3,
        out_specs=[w_spec] * 4 + [pl.BlockSpec((8, LANES), lambda t: (0, 0))],
        out_shape=[jax.ShapeDtypeStruct(w.shape, F32)] * 4 + [jax.ShapeDtypeStruct((8, LANES), F32)],
        compiler_params=_cparams(("arbitrary",)), name=name,
    )(*parts, w, m, v)


def _adam_small(g, ws, ms, vs):
    n = len(ws)

    def split(row, width):
        return jnp.concatenate([row[:, :width], row[:, width:2 * width]], axis=0)

    def body(g_ref, *refs):
        ins, outs = refs[:3 * n], refs[3 * n:]
        grads = [g_ref[0:2, :], split(g_ref[5:6, :], ATTN_W), split(g_ref[6:7, :], HGRN_W), split(g_ref[7:8, :], HGRN_DIM),
                 g_ref[2:4, :], g_ref[4:5, :]]
        for i, g_i in enumerate(grads):
            delta, m2, v2 = _adamw(ins[i][...], g_i, ins[n + i][...], ins[2 * n + i][...])
            for j, val in enumerate((g_i, delta, m2, v2)):
                outs[4 * i + j][...] = val

    vm = pl.BlockSpec(memory_space=pltpu.VMEM)
    res = pl.pallas_call(
        body, in_specs=[vm] * (1 + 3 * n), out_specs=[vm] * (4 * n),
        out_shape=[jax.ShapeDtypeStruct(w.shape, F32) for w in ws for _ in range(4)], name="adam_small",
    )(g, *ws, *ms, *vs)
    return [res[4 * i:4 * i + 4] for i in range(n)]


def _lower_bounds(logits):
    def body(l_ref, lb_ref, jac_ref):
        l0, l1 = l_ref[0:1, :], l_ref[1:2, :]
        mx = jnp.maximum(l0, l1)
        e0, e1 = jnp.exp(l0 - mx), jnp.exp(l1 - mx)
        p0, p1 = e0 / (e0 + e1), e1 / (e0 + e1)
        lb_ref[0:1, :] = p0 - p0
        lb_ref[1:2, :] = (p0 + p1) - p0
        jac_ref[0:1, :] = -p0 * p1
        jac_ref[1:2, :] = p0 * p1

    vm = pl.BlockSpec(memory_space=pltpu.VMEM)
    return pl.pallas_call(body, in_specs=[vm], out_specs=[vm, vm], out_shape=[jax.ShapeDtypeStruct(logits.shape, F32)] * 2,
                          name="hgrn_lower_bounds")(logits)


def _rope_tables(s, after):
    half = 32
    inv_freq = ROPE_THETA ** (-jnp.arange(half, dtype=F32) / half)
    ang = (jnp.arange(s, dtype=jnp.int32).astype(F32) + after[0, 0])[:, None] * inv_freq[None, :]
    cos, sin = jnp.cos(ang), jnp.sin(ang)
    return jnp.concatenate([cos] * 4, axis=1), jnp.concatenate([-sin, sin, -sin, sin], axis=1)


def kernel(x, norm_mix, w_in, attn_out_gain, hgrn_lb_logits, hgrn_out_gain, w_out, norm_mlp, w_up, w_down, norm_final, loss_target, m_norm_mix, m_w_in, m_attn_out_gain, m_hgrn_lb_logits, m_hgrn_out_gain, m_w_out, m_norm_mlp, m_w_up, m_w_down, m_norm_final, v_norm_mix, v_w_in, v_attn_out_gain, v_hgrn_lb_logits, v_hgrn_out_gain, v_w_out, v_norm_mlp, v_w_up, v_w_down, v_norm_final):
    depth = w_in.shape[0]
    assert depth == 2 and x.shape[0] == 1
    s, d = x.shape[1], x.shape[2]
    x0 = x[0]
    target = loss_target[0]
    g_mix, g_attn, g_hg, g_mlp = (norm_mix[:, None, :], attn_out_gain[:, None, :], hgrn_out_gain[:, None, :],
                                  norm_mlp[:, None, :])
    lb, lb_jac = _lower_bounds(hgrn_lb_logits)
    lb3 = lb[:, None, :]

    def flip(a):
        return jnp.swapaxes(a, 1, 2)

    shards = list(_pack_weights(flip(w_in), w_out, w_up, w_down))
    w_pieces = _weight_pieces(*shards)
    w_groups = [[0], [1, 2, 3], [4], [5, 6, 7]]
    me = (4 * lax.axis_index("x") + 2 * lax.axis_index("y") + lax.axis_index("c")).astype(jnp.int32).reshape(1)
    lands = _exchange_own("all_gather_own", me, shards, w_pieces)
    w_sems, shards, lands, token = _exchange_start("all_gather_start", shards, lands, w_pieces, w_groups, DIRECT_PEERS)

    def weights_ready(group, after):
        nonlocal shards
        idxs = w_groups[group]
        shards, got = _exchange_wait(f"all_gather_wait{group}", shards, [lands[i] for i in idxs], w_pieces,
                                     [(idxs, *w_sems[group])], after, DIRECT_PEERS)
        return _relay_to_sibling(f"all_gather_relay{group}", got, [w_pieces[i] for i in idxs])

    cos, sin = _rope_tables(s, token)

    def tied(small_arr, tok):
        return small_arr + tok[0, 0]

    saved = []
    xl = x0
    full = [None] * depth
    for l in range(depth):
        (full_in,) = weights_ready(2 * l, cos if l == 0 else xl)
        saved_x = xl
        proj, h = _fwd_inproj(l, xl, g_mix, full_in, cos, sin)
        o_attn, lse = _attn_fwd(l, proj)
        o_hg, mixed, states = _hgrn_fwd(l, proj, lb3, g_hg)
        mixed = _attn_norm(l, o_attn, g_attn, mixed)
        full_out, full_up, full_down = weights_ready(2 * l + 1, mixed)
        head = (norm_final[None, :], target) if l == depth - 1 else None
        xl, x_mid, u, h2, *loss_side = _mlp_fwd(l, xl, mixed, g_mlp, full_out, full_up, full_down, head)
        saved.append((saved_x, proj, h, o_attn, lse, o_hg, states, mixed, x_mid, u, h2))
        full[l] = (full_in, full_out, full_up, full_down)
    dx, (dxb, dnorm_final8, loss8) = xl, loss_side

    exchanges = []

    def scatter(tag, grads, kinds):
        pieces = _grad_pieces(grads, kinds)
        own = _exchange_own(f"reduce_scatter_own_{tag}", me, grads, pieces)
        sems, grads, own, tok = _exchange_start(f"reduce_scatter_start_{tag}", grads, own, pieces, [list(range(len(pieces)))])
        exchanges.append((grads, own, pieces, sems[0]))
        return tok

    small = {}
    for l in reversed(range(depth)):
        xl, proj, h, o_attn, lse, o_hg, states, mixed, x_mid, u, h2 = saved[l]
        full_in, full_out, full_up, full_down = full[l]
        hs = full_up.shape[3]
        gw_down = _mm_tn(f"grad_w_down_l{l}", u, dxb, u.shape[1], a_fn=_relu2)
        dx_mid, dx_mid_b, du, drec, do, delta, dmlp8, dattn8 = _mlp_bwd(l, dx, dxb, x_mid, g_mlp, u, o_attn, g_attn,
                                                                         full_out, full_up, full_down)
        gw_up = _mm_tn(f"grad_w_up_l{l}", h2, du, d, out_block_w=hs)
        gw_out = _mm_tn(f"grad_w_out_l{l}", mixed, dx_mid_b, mixed.shape[1])
        started = scatter(f"mlp_l{l}", [gw_down, gw_up, gw_out], ["rows", "up", "rows"])
        dqkv = _attn_bwd(l, proj, do, lse, delta, cos, sin, started)
        dhg, dlb8, dhgain8 = _hgrn_bwd(l, proj, lb3, g_hg, o_hg, drec, states)
        gin = _mm_tn(f"grad_w_in_qkv_l{l}", dqkv, h, PROJ_W, a_lead=True)
        gw_in = _mm_tn(f"grad_w_in_hg_l{l}", dhg, h, PROJ_W, a_lead=True, out_block_off=3, prev=gin)
        g_mix_t = tied(g_mix, scatter(f"mix_l{l}", [gw_in], ["rows"]))
        dx, dxb, dmix8 = _bwd_inproj(l, dqkv, dhg, full_in, xl, g_mix_t, dx_mid)
        small[l] = (dmix8, dattn8, dlb8, dhgain8, dmlp8)

    def scattered(name, which, after):
        grads, lands, pieces, waits = [], [], [], []
        for grads_e, own, pieces_e, (send, recv) in (exchanges[i] for i in which):
            first = len(pieces)
            pieces += [p._replace(src=p.src + len(grads)) for p in pieces_e]
            waits.append((list(range(first, first + len(pieces_e))), send, recv))
            grads += grads_e
            lands += own
        return _exchange_wait(name, grads, lands, pieces, waits, after)[1]

    down1, up1, out1, in1, down0, up0, out0 = scattered("reduce_scatter_wait_early", (0, 1, 2), dx)
    big = {
        "w_down": _adam_big("adam_w_down", [down0, down1], w_down, m_w_down, v_w_down, 4),
        "w_up": _adam_big("adam_w_up", [up0, up1], w_up, m_w_up, v_w_up, 2),
        "w_out": _adam_big("adam_w_out", [out0, out1], w_out, m_w_out, v_w_out, 1),
    }
    g_small = _all_reduce_small([small[l][0] for l in range(depth)], [small[l][4] for l in range(depth)], dnorm_final8,
                                [small[l][1] for l in range(depth)], small[depth - 1][2], lb_jac,
                                [small[l][3] for l in range(depth)], loss8, big["w_out"][4])
    loss = g_small[7, 2 * HGRN_DIM]
    row = lambda a: a[None, :]
    small_out = _adam_small(
        g_small, [norm_mix, attn_out_gain, hgrn_lb_logits, hgrn_out_gain, norm_mlp, row(norm_final)],
        [m_norm_mix, m_attn_out_gain, m_hgrn_lb_logits, m_hgrn_out_gain, m_norm_mlp, row(m_norm_final)],
        [v_norm_mix, v_attn_out_gain, v_hgrn_lb_logits, v_hgrn_out_gain, v_norm_mlp, row(v_norm_final)])
    small_out[5] = [t[0] for t in small_out[5]]
    (in0,) = scattered("reduce_scatter_wait_last", (3,), small_out[0][1])
    big["w_in"] = [flip(t) for t in _adam_big("adam_w_in", [in0, in1], flip(w_in), flip(m_w_in), flip(v_w_in), 2)[:4]]

    def gather(idx):
        mix, attn, lbl, hg, mlp, final = (t[idx] for t in small_out)
        return [mix, big["w_in"][idx], attn, lbl, hg, big["w_out"][idx], mlp, big["w_up"][idx], big["w_down"][idx], final]

    return (loss, dx[None], *gather(0), *gather(1), *gather(2), *gather(3))
```

```python
import functools
from typing import Callable, NamedTuple

import jax
import jax.numpy as jnp
from jax import lax
from jax.experimental import pallas as pl
from jax.experimental.pallas import tpu as pltpu

F32 = jnp.float32
BF = jnp.bfloat16

N_DEV = 8
ATTN_W = 512
HGRN_W = 512
HGRN_HEADS = 4
HGRN_DIM = 128
SEG = 512
N_SEG = 7
PROJ_W = N_SEG * SEG
MIX_W = ATTN_W + HGRN_W
SPAN = 128
DILATIONS = (1, 4, 16)
HGRN_CHUNK = 16
ROPE_THETA = 10000.0
NORM_EPS = 1e-6
MASK_VALUE = -1e30
ATTN_SCALE = 0.125
HGRN_SCALE = HGRN_DIM ** -0.5
ADAM_LR = 0.001
ADAM_B1 = 0.9
ADAM_B2 = 0.999
ADAM_EPS = 1e-08
ADAM_WD = 0.01
ADAM_STEP = 10
LANES = 128
VMEM_LIMIT = 56 * 1024 * 1024

NN = ((1,), (0,))
NT = ((1,), (1,))
TN = ((0,), (0,))
MESH = pl.DeviceIdType.MESH


def _dot(a, b, dims):
    return lax.dot_general(a, b, (dims, ((), ())), preferred_element_type=F32)


def _cparams(sem):
    return pltpu.CompilerParams(dimension_semantics=sem, vmem_limit_bytes=VMEM_LIMIT)


def _part8(x):
    r, n = x.shape
    return jnp.sum(x.reshape(r // 8, 8, n), axis=0)


def _sigmoid(x):
    return 1.0 / (1.0 + jnp.exp(-x))


def _rms_fwd(x, gain):
    r = lax.rsqrt(jnp.mean(x * x, axis=-1, keepdims=True) + NORM_EPS)
    return x * r * gain


def _rms_bwd(dy, x, gain):
    r = lax.rsqrt(jnp.mean(x * x, axis=-1, keepdims=True) + NORM_EPS)
    xn = x * r
    dxn = dy * gain
    dx = r * (dxn - xn * jnp.mean(dxn * xn, axis=-1, keepdims=True))
    return dx, dy * xn


def _rope_partner(x):
    n = x.shape[-1]
    lane = lax.broadcasted_iota(jnp.int32, x.shape, x.ndim - 1)
    return jnp.where((lane % 64) < 32, pltpu.roll(x, n - 32, x.ndim - 1), pltpu.roll(x, 32, x.ndim - 1))


def _tile_lanes(t, reps):
    return jnp.concatenate([t] * reps, axis=-1)


def _mm_tn(name, a, b, out_rows, a_lead=False, out_block_off=0, prev=None, out_block_w=None, a_fn=None,
           tm=512, tn=1024, sub=512):
    kdim, n = b.shape
    m = a.shape[-1]
    tm, tn, sub = min(tm, m), min(tn, n), min(sub, kdim)
    mt = m // tm
    n_lead = a.shape[0] if a_lead else 1
    if a_lead:
        a_spec = pl.BlockSpec((None, kdim, tm), lambda j, i: (i // mt, 0, i % mt))
    else:
        a_spec = pl.BlockSpec((kdim, tm), lambda j, i: (0, i))
    b_spec = pl.BlockSpec((kdim, tn), lambda j, i: (0, j))
    if out_block_w:
        nb = tn // out_block_w
        o_shape = jax.ShapeDtypeStruct((n // out_block_w, out_rows, out_block_w), BF)
        o_spec = pl.BlockSpec((nb, tm, out_block_w), lambda j, i: (j, i + out_block_off, 0))
    else:
        nb = 0
        o_shape = jax.ShapeDtypeStruct((out_rows, n), BF)
        o_spec = pl.BlockSpec((tm, tn), lambda j, i: (i + out_block_off, j))
    arrays, specs, aliases = [a, b], [a_spec, b_spec], {}
    if prev is not None:
        arrays.append(prev)
        specs.append(pl.BlockSpec(memory_space=pl.ANY))
        aliases = {2: 0}

    def body(*refs):
        a_ref, b_ref, o_ref = refs[0], refs[1], refs[-1]
        acc = None
        for k in range(kdim // sub):
            av = a_ref[k * sub:(k + 1) * sub, :]
            if a_fn is not None:
                av = a_fn(av)
            part = _dot(av, b_ref[k * sub:(k + 1) * sub, :], TN)
            acc = part if acc is None else acc + part
        if nb:
            for t in range(nb):
                o_ref[t] = acc[:, t * out_block_w:(t + 1) * out_block_w].astype(BF)
        else:
            o_ref[...] = acc.astype(BF)

    return pl.pallas_call(
        body, grid=(n // tn, n_lead * mt), in_specs=specs, out_specs=o_spec, out_shape=o_shape,
        compiler_params=_cparams(("parallel", "parallel")), name=name, input_output_aliases=aliases,
    )(*arrays)


def _pack_weights(w_in_t, w_out, w_up, w_down):
    depth = w_in_t.shape[0]
    arrays = (w_in_t, w_out, w_up, w_down)

    def body(*refs):
        for src, dst in zip(refs[:4], refs[4:]):
            dst[...] = src[...].astype(BF)

    specs = [pl.BlockSpec((None,) + a.shape[1:], lambda l: (l, 0, 0)) for a in arrays]
    return pl.pallas_call(
        body, grid=(depth,), in_specs=specs, out_specs=specs,
        out_shape=[jax.ShapeDtypeStruct(a.shape, BF) for a in arrays],
        compiler_params=_cparams(("arbitrary",)), name="pack_weights",
    )(*arrays)


def _my_position():
    x, y, c = lax.axis_index("x"), lax.axis_index("y"), lax.axis_index("c")
    return x, y, c, 4 * x + 2 * y + c


def _peer(x, y, c, k):
    px = 1 - x if k & 4 else x
    py = 1 - y if k & 2 else y
    pc = 1 - c if k & 1 else c
    return (px, py, pc), 4 * px + 2 * py + pc


PEER_ORDER = (1, 2, 4, 3, 5, 6, 7)
RELAYED_PEERS = (2, 4, 6)
DIRECT_PEERS = (1,) + RELAYED_PEERS


class _Piece(NamedTuple):
    src: int
    send: Callable
    slot: Callable
    land_shape: tuple
    own_src: tuple
    own_slot: tuple


HBM_SPEC = pl.BlockSpec(memory_space=pltpu.HBM)
SEM_SPEC = pl.BlockSpec(memory_space=pltpu.SEMAPHORE)
ANY_SPEC = pl.BlockSpec(memory_space=pl.ANY)


def _in_hbm(arrays):
    return [pltpu.with_memory_space_constraint(a, pltpu.HBM) for a in arrays]


def _hbm_like(arrays):
    return [pltpu.HBM(a.shape, a.dtype) for a in arrays]


def _rows_of(rows):
    return lambda ref, dev: ref.at[pl.ds(pl.multiple_of(dev * rows, 16), rows), :]


def _exchange_own(name, me, srcs, pieces):
    n = len(pieces)

    def body(me_ref, *refs):
        for i in range(n):
            refs[n + i][...] = refs[i][...]

    def spec(block_and_index):
        block, index = block_and_index
        return pl.BlockSpec(block, lambda i, me_ref: index(me_ref[0]))

    return pl.pallas_call(
        body,
        grid_spec=pltpu.PrefetchScalarGridSpec(
            num_scalar_prefetch=1, grid=(1,), in_specs=[spec(p.own_src) for p in pieces],
            out_specs=[spec(p.own_slot) for p in pieces]),
        out_shape=[jax.ShapeDtypeStruct(p.land_shape, BF) for p in pieces],
        compiler_params=_cparams(("arbitrary",)), name=name,
    )(me, *[srcs[p.src] for p in pieces])


def _exchange_start(name, srcs, lands, pieces, groups, peers=PEER_ORDER):
    n_src, n, n_g = len(srcs), len(pieces), len(groups)

    def body(*refs):
        src_refs, land_refs = refs[:n_src], refs[n_src:n_src + n]
        sems, token = refs[n_src + n:n_src + n + 2 * n_g], refs[-1]
        x, y, c, me = _my_position()
        for g, idxs in enumerate(groups):
            for k in peers:
                peer, pid = _peer(x, y, c, k)
                for j, i in enumerate(idxs):
                    p = pieces[i]
                    pltpu.make_async_remote_copy(
                        src_ref=p.send(src_refs[p.src], pid), dst_ref=p.slot(land_refs[i], me),
                        send_sem=sems[2 * g].at[(k - 1) * len(idxs) + j], recv_sem=sems[2 * g + 1].at[(k - 1) * len(idxs) + j],
                        device_id=peer, device_id_type=MESH).start()
        token[...] = jnp.zeros(token.shape, F32)

    sem_shapes = [pltpu.SemaphoreType.DMA(((N_DEV - 1) * len(idxs),)) for idxs in groups for _ in range(2)]
    res = pl.pallas_call(
        body, in_specs=[HBM_SPEC] * (n_src + n),
        out_specs=[SEM_SPEC] * (2 * n_g) + [HBM_SPEC] * (n_src + n) + [pl.BlockSpec(memory_space=pltpu.VMEM)],
        out_shape=sem_shapes + _hbm_like(srcs) + _hbm_like(lands) + [jax.ShapeDtypeStruct((8, LANES), F32)],
        input_output_aliases={i: 2 * n_g + i for i in range(n_src + n)},
        compiler_params=pltpu.CompilerParams(has_side_effects=pltpu.SideEffectType.DATAFLOW_SIDE_EFFECTING),
        name=name,
    )(*_in_hbm(srcs), *_in_hbm(lands))
    sems = [(res[2 * g], res[2 * g + 1]) for g in range(n_g)]
    return sems, list(res[2 * n_g:2 * n_g + n_src]), list(res[2 * n_g + n_src:2 * n_g + n_src + n]), res[-1]


def _relay_to_sibling(name, lands, pieces):
    n = len(lands)

    def body(*refs):
        land_refs, send_sems, recv_sems = refs[:n], refs[-2], refs[-1]
        x, y, c, me = _my_position()
        sibling, _ = _peer(x, y, c, 1)

        def passed_on(k_from, k_as, j):
            _, pid = _peer(x, y, c, k_from)
            slot = pieces[j].slot(land_refs[j], pid)
            at = RELAYED_PEERS.index(k_as - 1) * n + j
            return pltpu.make_async_remote_copy(src_ref=slot, dst_ref=slot, send_sem=send_sems.at[at], recv_sem=recv_sems.at[at],
                                                device_id=sibling, device_id_type=MESH)

        for k in RELAYED_PEERS:
            for j in range(n):
                passed_on(k, k + 1, j).start()
        for k in RELAYED_PEERS:
            for j in range(n):
                passed_on(k + 1, k + 1, j).wait_recv()
                passed_on(k, k + 1, j).wait_send()

    res = pl.pallas_call(
        body, in_specs=[HBM_SPEC] * n, out_specs=[HBM_SPEC] * n, out_shape=_hbm_like(lands),
        input_output_aliases={i: i for i in range(n)},
        scratch_shapes=[pltpu.SemaphoreType.DMA((len(RELAYED_PEERS) * n,))] * 2,
        compiler_params=pltpu.CompilerParams(has_side_effects=pltpu.SideEffectType.DATAFLOW_SIDE_EFFECTING),
        name=name,
    )(*_in_hbm(lands))
    return list(res)


def _relay_copies(pieces, land_refs, send_sems, recv_sems):
    n = len(pieces)
    x, y, c, me = _my_position()
    sibling, _ = _peer(x, y, c, 1)

    def passed_on(k_from, k_as, j):
        _, pid = _peer(x, y, c, k_from)
        slot = pieces[j].slot(land_refs[j], pid)
        at = RELAYED_PEERS.index(k_as - 1) * n + j
        return pltpu.make_async_remote_copy(src_ref=slot, dst_ref=slot, send_sem=send_sems.at[at], recv_sem=recv_sems.at[at],
                                            device_id=sibling, device_id_type=MESH)

    pairs = [(k, j) for k in RELAYED_PEERS for j in range(n)]
    return [passed_on(k, k + 1, j) for k, j in pairs], [passed_on(k + 1, k + 1, j) for k, j in pairs]


def _relay_start(name, lands, pieces):
    n = len(lands)

    def body(*refs):
        out, _ = _relay_copies(pieces, refs[:n], refs[n], refs[n + 1])
        for cp in out:
            cp.start()
        refs[-1][...] = jnp.zeros(refs[-1].shape, F32)

    sems = [pltpu.SemaphoreType.DMA((len(RELAYED_PEERS) * n,)) for _ in range(2)]
    res = pl.pallas_call(
        body, in_specs=[HBM_SPEC] * n, out_specs=[SEM_SPEC] * 2 + [HBM_SPEC] * n + [pl.BlockSpec(memory_space=pltpu.VMEM)],
        out_shape=sems + _hbm_like(lands) + [jax.ShapeDtypeStruct((8, LANES), F32)],
        input_output_aliases={i: 2 + i for i in range(n)},
        compiler_params=pltpu.CompilerParams(has_side_effects=pltpu.SideEffectType.DATAFLOW_SIDE_EFFECTING),
        name=name,
    )(*_in_hbm(lands))
    return (res[0], res[1]), list(res[2:2 + n]), res[-1]


def _relay_wait(name, lands, pieces, sems, after):
    n = len(lands)

    def body(*refs):
        out, arriving = _relay_copies(pieces, refs[:n], refs[n], refs[n + 1])
        for mine, theirs in zip(out, arriving):
            theirs.wait_recv()
            mine.wait_send()

    res = pl.pallas_call(
        body, in_specs=[HBM_SPEC] * n + [SEM_SPEC] * 2 + [ANY_SPEC], out_specs=[HBM_SPEC] * n, out_shape=_hbm_like(lands),
        input_output_aliases={i: i for i in range(n)},
        compiler_params=pltpu.CompilerParams(has_side_effects=pltpu.SideEffectType.DATAFLOW_SIDE_EFFECTING),
        name=name,
    )(*lands, *sems, after)
    return list(res)


def _exchange_wait(name, srcs, lands, pieces, waits, after, peers=PEER_ORDER):
    n_src, n, n_g = len(srcs), len(lands), len(waits)

    def body(*refs):
        src_refs, land_refs = refs[:n_src], refs[n_src:n_src + n]
        sems = refs[n_src + n:n_src + n + 2 * n_g]
        x, y, c, me = _my_position()
        at = 0
        for g, (idxs, _, _) in enumerate(waits):
            for k in peers:
                peer, pid = _peer(x, y, c, k)
                for j, i in enumerate(idxs):
                    p = pieces[i]
                    cp = pltpu.make_async_remote_copy(
                        src_ref=p.send(src_refs[p.src], pid), dst_ref=p.slot(land_refs[at + j], pid),
                        send_sem=sems[2 * g].at[(k - 1) * len(idxs) + j], recv_sem=sems[2 * g + 1].at[(k - 1) * len(idxs) + j],
                        device_id=peer, device_id_type=MESH)
                    cp.wait_send()
                    cp.wait_recv()
            at += len(idxs)

    sem_args = [s for _, send, recv in waits for s in (send, recv)]
    res = pl.pallas_call(
        body, in_specs=[HBM_SPEC] * (n_src + n) + [SEM_SPEC] * (2 * n_g) + [ANY_SPEC],
        out_specs=[HBM_SPEC] * (n_src + n), out_shape=_hbm_like(srcs) + _hbm_like(lands),
        input_output_aliases={i: i for i in range(n_src + n)},
        compiler_params=pltpu.CompilerParams(has_side_effects=pltpu.SideEffectType.DATAFLOW_SIDE_EFFECTING),
        name=name,
    )(*srcs, *lands, *sem_args, after)
    return list(res[:n_src]), list(res[n_src:])


def _weight_pieces(p_in, p_out, p_up, p_down):
    depth, cin, d = p_in.shape
    rout, hs = p_out.shape[1], p_up.shape[2]
    pieces = []
    for l in range(depth):
        whole = functools.partial(lambda ref, dev, l: ref.at[l], l=l)
        layer = functools.partial(lambda dev, l: (l, 0, 0), l=l)

        def rows(src, n_rows, whole=whole, layer=layer):
            return _Piece(src, whole, lambda ref, dev: _rows_of(n_rows)(ref.at[0], dev), (1, N_DEV * n_rows, d),
                          ((None, n_rows, d), layer), ((None, n_rows, d), lambda dev: (0, dev, 0)))

        pieces += [
            rows(0, cin), rows(1, rout),
            _Piece(2, whole, lambda ref, dev: ref.at[0, dev], (1, N_DEV, d, hs),
                   ((None, d, hs), layer), ((None, None, d, hs), lambda dev: (0, dev, 0, 0))),
            rows(3, hs),
        ]
    return pieces


def _grad_pieces(g_pair, kinds):
    pieces = []
    for i, (g, kind) in enumerate(zip(g_pair, kinds)):
        lead = lambda dev: (dev, 0, 0)
        if kind == "up":
            blk = ((None,) + g.shape[1:], lead)
            pieces.append(_Piece(i, lambda ref, dev: ref.at[dev], lambda ref, dev: ref.at[dev], g.shape, blk, blk))
        else:
            rows, cols = g.shape[0] // N_DEV, g.shape[1]
            pieces.append(_Piece(i, _rows_of(rows), lambda ref, dev: ref.at[dev], (N_DEV, rows, cols),
                                 ((rows, cols), lambda dev: (dev, 0)), ((None, rows, cols), lead)))
    return pieces


SMALL_W = 1024


def _all_reduce_small(mix8, mlp8, final8, attn8, lb8_last, lb_jac, hg8, loss8, after):
    def body(mix0, mix1, mlp0, mlp1, fin, attn0, attn1, lb, jac, hg0, hg1, loss, after_ref, o_ref, src_ref, buf_ref,
             send_sems, recv_sems):
        def total(ref):
            return jnp.sum(ref[...], axis=0, keepdims=True)

        dlb = total(lb)
        hg = jnp.concatenate([total(hg0), total(hg1), total(loss)], axis=1)
        src_ref[...] = jnp.concatenate([
            total(mix0), total(mix1), total(mlp0), total(mlp1), total(fin),
            jnp.concatenate([total(attn0), total(attn1)], axis=1),
            jnp.concatenate([jac[0:1, :] * dlb, jac[1:2, :] * dlb], axis=1),
            jnp.concatenate([hg, jnp.zeros((1, SMALL_W - hg.shape[1]), F32)], axis=1)], axis=0)
        x, y, c, me = _my_position()
        buf_ref[me] = src_ref[...]
        sends = []
        for k in PEER_ORDER:
            peer, _ = _peer(x, y, c, k)
            cp = pltpu.make_async_remote_copy(src_ref=src_ref, dst_ref=buf_ref.at[me], send_sem=send_sems.at[k - 1],
                                              recv_sem=recv_sems.at[k - 1], device_id=peer, device_id_type=MESH)
            cp.start()
            sends.append(cp)
        for k in PEER_ORDER:
            peer, pid = _peer(x, y, c, k)
            pltpu.make_async_remote_copy(src_ref=src_ref, dst_ref=buf_ref.at[pid], send_sem=send_sems.at[k - 1],
                                         recv_sem=recv_sems.at[k - 1], device_id=peer, device_id_type=MESH).wait_recv()
        for cp in sends:
            cp.wait_send()
        acc = buf_ref[0]
        for dev in range(1, N_DEV):
            acc = acc + buf_ref[dev]
        o_ref[...] = acc

    assert mix8[0].shape[1] == SMALL_W
    vm = pl.BlockSpec(memory_space=pltpu.VMEM)
    return pl.pallas_call(
        body, in_specs=[vm] * 12 + [ANY_SPEC], out_specs=vm, out_shape=jax.ShapeDtypeStruct((8, SMALL_W), F32),
        scratch_shapes=[pltpu.VMEM((8, SMALL_W), F32), pltpu.VMEM((N_DEV, 8, SMALL_W), F32),
                        pltpu.SemaphoreType.DMA((N_DEV - 1,)), pltpu.SemaphoreType.DMA((N_DEV - 1,))],
        name="all_reduce_small",
    )(*mix8, *mlp8, final8, *attn8, lb8_last, lb_jac, *hg8, loss8, after)


def _resident(block_shape, index_map):
    return pl.BlockSpec(block_shape, index_map, pipeline_mode=pl.Buffered(1))


def _fwd_inproj(layer, x, gain, g_in, cos, sin, tm=512):
    s, d = x.shape

    def body(x_ref, gain_ref, w_ref, cos_ref, sin_ref, proj_ref, h_ref):
        h = _rms_fwd(x_ref[...], gain_ref[...]).astype(BF)
        h_ref[...] = h
        cs = _tile_lanes(cos_ref[...], SEG // LANES)
        sn = _tile_lanes(sin_ref[...], SEG // LANES)
        for seg in range(N_SEG):
            acc = _dot(h, w_ref[seg * SEG:(seg + 1) * SEG, :], NT)
            if seg < 2:
                acc = acc * cs + _rope_partner(acc) * sn
            if seg == 0:
                acc = acc * ATTN_SCALE
            proj_ref[:, seg * SEG:(seg + 1) * SEG] = acc

    return pl.pallas_call(
        body, grid=(s // tm,),
        in_specs=[pl.BlockSpec((tm, d), lambda i: (i, 0)), pl.BlockSpec((None, 1, d), lambda i: (layer, 0, 0)),
                  _resident((None, PROJ_W, d), lambda i: (0, 0, 0)),
                  pl.BlockSpec((tm, LANES), lambda i: (i, 0)), pl.BlockSpec((tm, LANES), lambda i: (i, 0))],
        out_specs=[pl.BlockSpec((tm, PROJ_W), lambda i: (i, 0)), pl.BlockSpec((tm, d), lambda i: (i, 0))],
        out_shape=[jax.ShapeDtypeStruct((s, PROJ_W), F32), jax.ShapeDtypeStruct((s, d), BF)],
        compiler_params=_cparams(("parallel",)), name=f"fwd_inproj_l{layer}",
    )(x, gain, g_in, cos, sin)


ATTN_UNIT = SPAN * max(DILATIONS)
ATTN_GROUP = 4


def _attn_masks(first_block_has_prev):
    row = lax.broadcasted_iota(jnp.int32, (SPAN, 2 * SPAN), 0)
    col = lax.broadcasted_iota(jnp.int32, (SPAN, 2 * SPAN), 1)
    band = (col >= row) & (col <= row + SPAN)
    lane = lax.broadcasted_iota(jnp.int32, (SPAN, LANES), 1)
    return band & ((col >= SPAN) | first_block_has_prev), band, lane < 64


def _attn_specs(n_in_extra, unit_of=lambda n: n):
    pairs = ATTN_W // LANES
    q_spec = pl.BlockSpec((ATTN_UNIT, LANES), lambda p, n: (unit_of(n), p))

    def prev(seg):
        return pl.BlockSpec((ATTN_UNIT, LANES), lambda p, n: (jnp.maximum(unit_of(n) - 1, 0), seg * pairs + p))

    def cur(seg):
        return pl.BlockSpec((ATTN_UNIT, LANES), lambda p, n: (unit_of(n), seg * pairs + p))

    return [q_spec, prev(1), cur(1), prev(2), cur(2)] + [q_spec] * n_in_extra


def _attn_groups(dil):
    blocks = ATTN_UNIT // (SPAN * dil)
    pairs = [(r, b) for r in range(dil) for b in range(blocks)]
    return [pairs[i:i + ATTN_GROUP] for i in range(0, len(pairs), ATTN_GROUP)]


def _block_rows(dil, r, b, n=1):
    start = r + dil * SPAN * b
    return pl.ds(start, n * SPAN, stride=dil) if dil > 1 else pl.ds(start, n * SPAN)


def _block_keys(prev_ref, cur_ref, dil, r, b):
    if b > 0:
        return cur_ref[_block_rows(dil, r, b - 1, 2), :]
    last = ATTN_UNIT // (SPAN * dil) - 1
    return jnp.concatenate([prev_ref[_block_rows(dil, r, last), :], cur_ref[_block_rows(dil, r, 0), :]], axis=0)


def _attn_fwd(layer, proj):
    s = proj.shape[0]
    n_pat = len(DILATIONS)
    merge_rows = 256

    def body(q_ref, kp_ref, kc_ref, vp_ref, vc_ref, o_ref, lse_ref, o_scr, lse_scr):
        m_first, m_rest, is_a = _attn_masks(pl.program_id(1) > 0)
        sels = (is_a, jnp.logical_not(is_a))
        is_a_keys = lax.broadcasted_iota(jnp.int32, (2 * SPAN, LANES), 1) < 64
        for pi, dil in enumerate(DILATIONS):
            for group in _attn_groups(dil):
                items = [(r, b, h) for r, b in group for h in range(2)]
                q = {rb: q_ref[_block_rows(dil, *rb), :] for rb in group}
                k = {rb: _block_keys(kp_ref, kc_ref, dil, *rb).astype(BF) for rb in group}
                v = {rb: _block_keys(vp_ref, vc_ref, dil, *rb).astype(BF) for rb in group}
                v_sum = {rb: (jnp.where(is_a_keys, v[rb], 1.0), jnp.where(is_a_keys, 1.0, v[rb])) for rb in group}
                sc = [jnp.where(m_first if b == 0 else m_rest,
                                _dot(jnp.where(sels[h], q[r, b], 0.0).astype(BF), k[r, b], NT), MASK_VALUE)
                      for r, b, h in items]
                mx = [jnp.max(jnp.maximum(t[:, :SPAN], t[:, SPAN:]), axis=-1, keepdims=True) for t in sc]
                p = [jnp.exp(t - m).astype(BF) for t, m in zip(sc, mx)]
                both = [_dot(t, v_sum[r, b][h], NN) for t, (r, b, h) in zip(p, items)]
                for j, (r, b) in enumerate(group):
                    t_a, t_b = both[2 * j], both[2 * j + 1]
                    den = pltpu.roll(jnp.where(is_a, t_b, t_a), 64, 1)
                    o_scr[pi, _block_rows(dil, r, b), :] = jnp.where(is_a, t_a, t_b) / den
                    lse_scr[pi, _block_rows(dil, r, b), :] = jnp.where(is_a, mx[2 * j], mx[2 * j + 1]) + jnp.log(den)
        for c in range(ATTN_UNIT // merge_rows):
            rows = slice(c * merge_rows, (c + 1) * merge_rows)
            ls = [lse_scr[pi, rows, :] for pi in range(n_pat)]
            mx = functools.reduce(jnp.maximum, ls)
            ws = [jnp.exp(l - mx) for l in ls]
            den = functools.reduce(jnp.add, ws)
            o_ref[rows, :] = functools.reduce(jnp.add, [w * o_scr[pi, rows, :] for pi, w in enumerate(ws)]) / den
            lse_ref[rows, :] = mx + jnp.log(den)

    out_spec = pl.BlockSpec((ATTN_UNIT, LANES), lambda p, n: (n, p))
    return pl.pallas_call(
        body, grid=(ATTN_W // LANES, s // ATTN_UNIT), in_specs=_attn_specs(0), out_specs=[out_spec, out_spec],
        out_shape=[jax.ShapeDtypeStruct((s, ATTN_W), F32)] * 2,
        scratch_shapes=[pltpu.VMEM((n_pat, ATTN_UNIT, LANES), F32)] * 2,
        compiler_params=_cparams(("parallel", "arbitrary")), name=f"attn_fwd_l{layer}",
    )(proj, proj, proj, proj, proj)


def _attn_norm(layer, o, gain, mixed, tm=512):
    s = o.shape[0]

    def body(o_ref, gain_ref, mixed_ref, n_ref):
        n_ref[...] = _rms_fwd(o_ref[...], gain_ref[...]).astype(BF)

    blk = pl.BlockSpec((tm, ATTN_W), lambda i: (i, 0))
    return pl.pallas_call(
        body, grid=(s // tm,),
        in_specs=[blk, pl.BlockSpec((None, 1, ATTN_W), lambda i: (layer, 0, 0)), pl.BlockSpec(memory_space=pl.ANY)],
        out_specs=blk, out_shape=jax.ShapeDtypeStruct(mixed.shape, BF), input_output_aliases={2: 0},
        compiler_params=_cparams(("parallel",)), name=f"attn_norm_l{layer}",
    )(o, gain, mixed)


def _chunk_cumsum(x, reverse=False):
    n = x.shape[0]
    pos = lax.broadcasted_iota(jnp.int32, x.shape, 0) % HGRN_CHUNK
    for sh in (1, 2, 4, 8):
        if reverse:
            x = x + jnp.where(pos < HGRN_CHUNK - sh, pltpu.roll(x, n - sh, 0), 0.0)
        else:
            x = x + jnp.where(pos >= sh, pltpu.roll(x, sh, 0), 0.0)
    return x


def _chunk_row(x, row):
    r, n = x.shape
    x3 = x.reshape(r // HGRN_CHUNK, HGRN_CHUNK, n)
    return jnp.broadcast_to(x3[:, row:row + 1, :], x3.shape).reshape(r, n)


def _hgrn_pre(qh, z, lb):
    sig = _sigmoid(z)
    f = lb + (1.0 - lb) * sig
    k = 1.0 - f
    sq = _sigmoid(qh)
    q = qh * sq * HGRN_SCALE
    g = _chunk_cumsum(jnp.log(f))
    g_mid = _chunk_row(g, HGRN_CHUNK // 2 - 1)
    g_last = _chunk_row(g, HGRN_CHUNK - 1)
    e_q, e_k = jnp.exp(g - g_mid), jnp.exp(g_mid - g)
    e_in, e_out = jnp.exp(g), jnp.exp(g_last - g)
    return dict(sig=sig, f=f, k=k, sq=sq, q=q, g_last=g_last, e_q=e_q, e_k=e_k, e_in=e_in, e_out=e_out,
                qt=q * e_q, kt=k * e_k, qg=q * e_in, kout=k * e_out)


def _hgrn_mask():
    row = lax.broadcasted_iota(jnp.int32, (LANES, LANES), 0)
    col = lax.broadcasted_iota(jnp.int32, (LANES, LANES), 1)
    return (row // HGRN_CHUNK == col // HGRN_CHUNK) & (col <= row)


def _hgrn_in_specs(layer, rev, nblk):
    def blk(b):
        return nblk - 1 - b if rev else b
    first = 3 * ATTN_W // HGRN_W
    specs = [pl.BlockSpec((HGRN_ROWS, HGRN_W), functools.partial(lambda b, seg: (blk(b), first + seg), seg=seg))
             for seg in range(4)]
    specs.append(pl.BlockSpec((None, 1, HGRN_W), lambda b: (layer, 0, 0)))
    specs.append(pl.BlockSpec((None, 1, HGRN_DIM), lambda b: (layer, 0, 0)))
    return specs, blk


def _head(x, h):
    return x[:, h * HGRN_DIM:(h + 1) * HGRN_DIM]


def _chunk(x, c):
    return x[c * HGRN_CHUNK:(c + 1) * HGRN_CHUNK]


def _sub(x, sb):
    return x[sb * LANES:(sb + 1) * LANES]


HGRN_ROWS = 256
HEADS = range(HGRN_HEADS)
SUBS = range(HGRN_ROWS // LANES)
CHUNKS = range(HGRN_ROWS // HGRN_CHUNK)


def _hgrn_fwd(layer, proj, lb, gain):
    s = proj.shape[0]
    nblk = s // HGRN_ROWS
    cpb = len(CHUNKS)

    def body(q_ref, f_ref, i_ref, g_ref, lb_ref, gain_ref, o_ref, rec_ref, st_ref, state):
        @pl.when(pl.program_id(0) == 0)
        def _():
            state[...] = jnp.zeros(state.shape, F32)

        pre = _hgrn_pre(q_ref[...], f_ref[...], lb_ref[...])
        v = i_ref[...].astype(BF)
        qt, kt, qg, kout = (pre[n].astype(BF) for n in ("qt", "kt", "qg", "kout"))
        dec = jnp.exp(pre["g_last"])
        mask = _hgrn_mask()
        a = [[jnp.where(mask, _dot(_sub(_head(qt, h), sb), _sub(_head(kt, h), sb), NT), 0.0).astype(BF) for sb in SUBS]
             for h in HEADS]
        o_intra = [[_dot(a[h][sb], _sub(_head(v, h), sb), NN) for sb in SUBS] for h in HEADS]
        update = [[_dot(_chunk(_head(v, h), c), _chunk(_head(kout, h), c), TN) for c in CHUNKS] for h in HEADS]
        for h in HEADS:
            st = state[h]
            for c in CHUNKS:
                st_ref[h, c * LANES:(c + 1) * LANES, :] = st.astype(BF)
                st = st * _head(dec, h)[c * HGRN_CHUNK:c * HGRN_CHUNK + 1, :] + update[h][c]
            state[h] = st
        inter = [[_dot(_chunk(_head(qg, h), c), st_ref[h, c * LANES:(c + 1) * LANES, :].astype(BF), NT) for c in CHUNKS]
                 for h in HEADS]
        o = [jnp.concatenate(o_intra[h], axis=0) + jnp.concatenate(inter[h], axis=0) for h in HEADS]
        o_ref[...] = jnp.concatenate(o, axis=1)
        gate = g_ref[...]
        normed = jnp.concatenate([_rms_fwd(o[h], gain_ref[...]) for h in HEADS], axis=1)
        rec_ref[...] = (normed * (gate * _sigmoid(gate))).astype(BF)

    specs, _ = _hgrn_in_specs(layer, False, nblk)
    return pl.pallas_call(
        body, grid=(nblk,), in_specs=specs,
        out_specs=[pl.BlockSpec((HGRN_ROWS, HGRN_W), lambda b: (b, 0)), pl.BlockSpec((HGRN_ROWS, HGRN_W), lambda b: (b, 1)),
                   pl.BlockSpec((HGRN_HEADS, cpb * LANES, LANES), lambda b: (0, b, 0))],
        out_shape=[jax.ShapeDtypeStruct((s, HGRN_W), F32), jax.ShapeDtypeStruct((s, MIX_W), BF),
                   jax.ShapeDtypeStruct((HGRN_HEADS, nblk * cpb * LANES, LANES), BF)],
        scratch_shapes=[pltpu.VMEM((HGRN_HEADS, LANES, LANES), F32)],
        compiler_params=_cparams(("arbitrary",)), name=f"hgrn_fwd_l{layer}",
    )(proj, proj, proj, proj, lb, gain)


def _relu2(u):
    return jnp.square(jnp.maximum(u, 0)).astype(BF)


def _mlp_fwd(layer, x, mixed, gain, g_out, g_up, g_down, head=None):
    s, d = x.shape
    mw = mixed.shape[1]
    nblk, hs = g_up.shape[1], g_up.shape[3]
    tm = 512

    def body(x_ref, m_ref, gain_ref, out_w_ref, up_ref, down_ref, *refs):
        if head:
            fin_ref, t_ref, o_ref, mid_ref, u_ref, h_ref, ob_ref, dfin_ref, loss_ref, a_buf = refs
        else:
            o_ref, mid_ref, u_ref, h_ref, a_buf = refs
        xv = x_ref[...] + _dot(m_ref[...], out_w_ref[...], NN)
        mid_ref[...] = xv
        h = _rms_fwd(xv, gain_ref[...]).astype(BF)
        h_ref[...] = h
        for j in range(nblk):
            u = _dot(h, up_ref[j], NN)
            u_ref[:, j * hs:(j + 1) * hs] = u.astype(BF)
            a_buf[:, j * hs:(j + 1) * hs] = _relu2(u)
        acc = xv
        for j in range(nblk):
            acc = acc + _dot(a_buf[:, j * hs:(j + 1) * hs], down_ref[j * hs:(j + 1) * hs, :], NN)
        if not head:
            o_ref[...] = acc
            return
        fin = fin_ref[...]
        err = _rms_fwd(acc, fin) - t_ref[...]
        dout, dfin = _rms_bwd(err * (1.0 / d), acc, fin)
        o_ref[...] = dout
        ob_ref[...] = dout.astype(BF)
        step = pl.program_id(0)
        _accumulate_rows(step, dfin_ref, _part8(dfin))
        _accumulate_rows(step, loss_ref, _part8(0.5 * jnp.mean(err * err, axis=-1, keepdims=True) * jnp.ones((1, LANES), F32)))

    row = pl.BlockSpec((tm, d), lambda i: (i, 0))
    in_specs = [row, pl.BlockSpec((tm, mw), lambda i: (i, 0)), pl.BlockSpec((None, 1, d), lambda i: (layer, 0, 0)),
                _resident((None, mw, d), lambda i: (0, 0, 0)),
                _resident((None, nblk, d, hs), lambda i: (0, 0, 0, 0)),
                _resident((None, nblk * hs, d), lambda i: (0, 0, 0))]
    out_specs = [row, row, pl.BlockSpec((tm, nblk * hs), lambda i: (i, 0)), row]
    out_shape = [jax.ShapeDtypeStruct((s, d), F32), jax.ShapeDtypeStruct((s, d), F32),
                 jax.ShapeDtypeStruct((s, nblk * hs), BF), jax.ShapeDtypeStruct((s, d), BF)]
    args = [x, mixed, gain, g_out, g_up, g_down]
    if head:
        in_specs += [pl.BlockSpec((1, d), lambda i: (0, 0)), row]
        out_specs += [row, pl.BlockSpec((8, d), lambda i: (0, 0)), pl.BlockSpec((8, LANES), lambda i: (0, 0))]
        out_shape += [jax.ShapeDtypeStruct((s, d), BF), jax.ShapeDtypeStruct((8, d), F32), jax.ShapeDtypeStruct((8, LANES), F32)]
        args += list(head)
    return pl.pallas_call(
        body, grid=(s // tm,), in_specs=in_specs, out_specs=out_specs, out_shape=out_shape,
        scratch_shapes=[pltpu.VMEM((tm, nblk * hs), BF)],
        compiler_params=_cparams(("arbitrary",) if head else ("parallel",)), name=f"mlp_fwd_l{layer}",
    )(*args)


def _accumulate_rows(i, ref, part):
    @pl.when(i == 0)
    def _():
        ref[...] = part

    @pl.when(i > 0)
    def _():
        ref[...] += part


def _head_sums(prod):
    row = lax.broadcasted_iota(jnp.int32, (ATTN_W, ATTN_W), 0)
    col = lax.broadcasted_iota(jnp.int32, (ATTN_W, ATTN_W), 1)
    same_head = jnp.where(row // 64 == col // 64, 1.0, 0.0).astype(BF)
    high = prod.astype(BF)
    low = (prod - high.astype(F32)).astype(BF)
    return _dot(high, same_head, NN) + _dot(low, same_head, NN)


def _mlp_bwd(layer, dx, dxb, x, gain, u, o_attn, gain_attn, g_out, g_up, g_down, tm=256):
    s, d = x.shape
    mw = g_out.shape[1]
    nblk, hs = g_up.shape[1], g_up.shape[3]

    def body(dx_ref, dxb_ref, x_ref, gain_ref, u_ref, oa_ref, ga_ref, out_w_ref, up_ref, down_ref, o_ref, ob_ref, du_ref,
             drec_ref, do_ref, delta_ref, dgain_ref, dattn_ref):
        dxb_v = dxb_ref[...]
        for j in range(nblk):
            cols = slice(j * hs, (j + 1) * hs)
            da = _dot(dxb_v, down_ref[cols, :], NT)
            du_ref[:, cols] = (da * (2.0 * jnp.maximum(u_ref[:, cols].astype(F32), 0.0))).astype(BF)
        acc = jnp.zeros((tm, d), F32)
        for j in range(nblk):
            acc = acc + _dot(du_ref[:, j * hs:(j + 1) * hs], up_ref[j], NT)
        dxn, dgain = _rms_bwd(acc, x_ref[...], gain_ref[...])
        out = dx_ref[...] + dxn
        out_b = out.astype(BF)
        o_ref[...] = out
        ob_ref[...] = out_b
        dm = _dot(out_b, out_w_ref[...], NT)
        drec_ref[...] = dm[:, ATTN_W:]
        ov = oa_ref[...]
        do, dattn = _rms_bwd(dm[:, :ATTN_W], ov, ga_ref[...])
        do_ref[...] = do
        delta_ref[...] = _head_sums(do * ov)
        _accumulate_rows(pl.program_id(0), dgain_ref, _part8(dgain))
        _accumulate_rows(pl.program_id(0), dattn_ref, _part8(dattn))

    row = pl.BlockSpec((tm, d), lambda i: (i, 0))
    wide = pl.BlockSpec((tm, nblk * hs), lambda i: (i, 0))
    half = pl.BlockSpec((tm, ATTN_W), lambda i: (i, 0))
    return pl.pallas_call(
        body, grid=(s // tm,),
        in_specs=[row, row, row, pl.BlockSpec((None, 1, d), lambda i: (layer, 0, 0)), wide,
                  half, pl.BlockSpec((None, 1, ATTN_W), lambda i: (layer, 0, 0)),
                  _resident((None, mw, d), lambda i: (0, 0, 0)),
                  _resident((None, nblk, d, hs), lambda i: (0, 0, 0, 0)),
                  _resident((None, nblk * hs, d), lambda i: (0, 0, 0))],
        out_specs=[row, row, wide, half, half, half, pl.BlockSpec((8, d), lambda i: (0, 0)),
                   pl.BlockSpec((8, ATTN_W), lambda i: (0, 0))],
        out_shape=[jax.ShapeDtypeStruct((s, d), F32), jax.ShapeDtypeStruct((s, d), BF),
                   jax.ShapeDtypeStruct((s, nblk * hs), BF)] + [jax.ShapeDtypeStruct((s, ATTN_W), F32)] * 3
        + [jax.ShapeDtypeStruct((8, d), F32), jax.ShapeDtypeStruct((8, ATTN_W), F32)],
        compiler_params=_cparams(("arbitrary",)), name=f"mlp_bwd_l{layer}",
    )(dx, dxb, x, gain, u, o_attn, gain_attn, g_out, g_up, g_down)


def _attn_bwd(layer, proj, do, lse, delta, cos, sin, after):
    s = proj.shape[0]
    n_units = s // ATTN_UNIT
    out_rows = 256

    def unit_of(n):
        return n_units - 1 - n

    def body(q_ref, kp_ref, kc_ref, vp_ref, vc_ref, do_ref, lse_ref, delta_ref, cos_ref, sin_ref, after_ref, out_ref,
             dq_ref, dk_ref, dkp_ref, dv_ref, dvp_ref, carry_k, carry_v):
        step = pl.program_id(1)
        m_first, m_rest, is_a = _attn_masks(unit_of(step) > 0)
        sels = (is_a, jnp.logical_not(is_a))
        for ref in (dq_ref, dk_ref, dkp_ref, dv_ref, dvp_ref):
            ref[...] = jnp.zeros(ref.shape, F32)
        for dil in DILATIONS:
            last = ATTN_UNIT // (SPAN * dil) - 1
            for group in _attn_groups(dil):
                items = [(r, b, h) for r, b in group for h in range(2)]
                q = {rb: q_ref[_block_rows(dil, *rb), :] for rb in group}
                dov = {rb: do_ref[_block_rows(dil, *rb), :] for rb in group}
                lse_v = {rb: lse_ref[_block_rows(dil, *rb), :] for rb in group}
                delta_v = {rb: delta_ref[_block_rows(dil, *rb), :] for rb in group}
                k = {rb: _block_keys(kp_ref, kc_ref, dil, *rb).astype(BF) for rb in group}
                v = {rb: _block_keys(vp_ref, vc_ref, dil, *rb).astype(BF) for rb in group}
                qh = [jnp.where(sels[h], q[r, b], 0.0).astype(BF) for r, b, h in items]
                doh = [jnp.where(sels[h], dov[r, b], 0.0).astype(BF) for r, b, h in items]
                sc = [jnp.where(m_first if b == 0 else m_rest, _dot(qh[i], k[r, b], NT), MASK_VALUE)
                      for i, (r, b, h) in enumerate(items)]
                p = [jnp.exp(sc[i] - lse_v[r, b][:, 64 * h:64 * h + 1]) for i, (r, b, h) in enumerate(items)]
                ds = [(p[i] * (_dot(doh[i], v[r, b], NT) - delta_v[r, b][:, 64 * h:64 * h + 1])).astype(BF)
                      for i, (r, b, h) in enumerate(items)]
                dv = [_dot(p[i].astype(BF), doh[i], TN) for i in range(len(items))]
                dq = [_dot(ds[i], k[r, b], NN) for i, (r, b, h) in enumerate(items)]
                dk = [_dot(ds[i], qh[i], TN) for i in range(len(items))]
                for j, (r, b) in enumerate(group):
                    own = _block_rows(dil, r, b)
                    dq_ref[own, :] += jnp.where(is_a, dq[2 * j], dq[2 * j + 1])
                    dk2, dv2 = dk[2 * j] + dk[2 * j + 1], dv[2 * j] + dv[2 * j + 1]
                    dk_ref[own, :] += dk2[SPAN:]
                    dv_ref[own, :] += dv2[SPAN:]
                    if b > 0:
                        before = _block_rows(dil, r, b - 1)
                        dk_ref[before, :] += dk2[:SPAN]
                        dv_ref[before, :] += dv2[:SPAN]
                    else:
                        before = _block_rows(dil, r, last)
                        dkp_ref[before, :] += dk2[:SPAN]
                        dvp_ref[before, :] += dv2[:SPAN]
        has_next = step > 0
        for c in range(ATTN_UNIT // out_rows):
            rows = slice(c * out_rows, (c + 1) * out_rows)
            cs, sn = cos_ref[rows, :], sin_ref[rows, :]
            dqv = dq_ref[rows, :]
            dkv = dk_ref[rows, :] + jnp.where(has_next, carry_k[rows, :], 0.0)
            dvv = dv_ref[rows, :] + jnp.where(has_next, carry_v[rows, :], 0.0)
            out_ref[0, rows, :] = ((dqv * cs - _rope_partner(dqv) * sn) * ATTN_SCALE).astype(BF)
            out_ref[1, rows, :] = (dkv * cs - _rope_partner(dkv) * sn).astype(BF)
            out_ref[2, rows, :] = dvv.astype(BF)
        carry_k[...] = dkp_ref[...]
        carry_v[...] = dvp_ref[...]

    tab = pl.BlockSpec((ATTN_UNIT, LANES), lambda p, n: (unit_of(n), 0))
    return pl.pallas_call(
        body, grid=(ATTN_W // LANES, n_units), in_specs=_attn_specs(3, unit_of) + [tab, tab, ANY_SPEC],
        out_specs=pl.BlockSpec((3, ATTN_UNIT, LANES), lambda p, n: (0, unit_of(n), p)),
        out_shape=jax.ShapeDtypeStruct((3, s, ATTN_W), BF),
        scratch_shapes=[pltpu.VMEM((ATTN_UNIT, LANES), F32)] * 7,
        compiler_params=_cparams(("parallel", "arbitrary")), name=f"attn_bwd_l{layer}",
    )(proj, proj, proj, proj, proj, do, lse, delta, cos, sin, after)


def _hgrn_bwd(layer, proj, lb, gain, o, drec, states):
    s = proj.shape[0]
    nblk = s // HGRN_ROWS
    cpb = len(CHUNKS)

    def body(q_ref, f_ref, i_ref, g_ref, lb_ref, gain_ref, o_ref, drec_ref, st_ref, dseg_ref, dlb_ref, dgain_ref,
             dstate, dst_buf):
        step = pl.program_id(0)

        @pl.when(step == 0)
        def _():
            dstate[...] = jnp.zeros(dstate.shape, F32)

        lbv, gv = lb_ref[...], gain_ref[...]
        qh, z, gate_in = q_ref[...], f_ref[...], g_ref[...]
        pre = _hgrn_pre(qh, z, lbv)
        v = i_ref[...].astype(BF)
        sg = _sigmoid(gate_in)
        ov, drec = o_ref[...], drec_ref[...]
        dnormed = drec * (gate_in * sg)
        back = [_rms_bwd(_head(dnormed, h), _head(ov, h), gv) for h in HEADS]
        do_b = jnp.concatenate([b[0] for b in back], axis=1).astype(BF)
        dgain = back[0][1] + back[1][1] + back[2][1] + back[3][1]
        normed = jnp.concatenate([_rms_fwd(_head(ov, h), gv) for h in HEADS], axis=1)
        dgate_in = drec * normed * (sg * (1.0 + gate_in * (1.0 - sg)))
        mask = _hgrn_mask()
        qt, kt, qg, kout = (pre[n].astype(BF) for n in ("qt", "kt", "qg", "kout"))
        dec = jnp.exp(pre["g_last"])
        def intra(fn):
            return jnp.concatenate([jnp.concatenate([fn(h, sb) for sb in SUBS], axis=0) for h in HEADS], axis=1)

        def hs(x, h, sb):
            return _sub(_head(x, h), sb)

        a = [[jnp.where(mask, _dot(hs(qt, h, sb), hs(kt, h, sb), NT), 0.0).astype(BF) for sb in SUBS] for h in HEADS]
        da = [[jnp.where(mask, _dot(hs(do_b, h, sb), hs(v, h, sb), NT), 0.0).astype(BF) for sb in SUBS] for h in HEADS]
        dv_intra = intra(lambda h, sb: _dot(a[h][sb], hs(do_b, h, sb), TN))
        dqt = intra(lambda h, sb: _dot(da[h][sb], hs(kt, h, sb), NN))
        dkt = intra(lambda h, sb: _dot(da[h][sb], hs(qt, h, sb), TN))
        feed = [[_dot(_chunk(_head(do_b, h), c), _chunk(_head(qg, h), c), TN) for c in CHUNKS] for h in HEADS]
        for h in HEADS:
            dst = dstate[h]
            for c in reversed(CHUNKS):
                dst_buf[h, c * LANES:(c + 1) * LANES, :] = dst
                dst = dst * _head(dec, h)[c * HGRN_CHUNK:c * HGRN_CHUNK + 1, :] + feed[h][c]
            dstate[h] = dst

        def per_chunk(fn):
            cols = []
            for h in HEADS:
                rows = [jnp.broadcast_to(t, (HGRN_CHUNK, HGRN_DIM)) for t in (fn(h, c) for c in CHUNKS)]
                cols.append(jnp.concatenate(rows, axis=0))
            return jnp.concatenate(cols, axis=1)

        def st_prev(h, c):
            return st_ref[h, c * LANES:(c + 1) * LANES, :]

        def dst_at(h, c):
            return dst_buf[h, c * LANES:(c + 1) * LANES, :]

        dqg = per_chunk(lambda h, c: _dot(_chunk(_head(do_b, h), c), st_prev(h, c).astype(BF), NN))
        dkout = per_chunk(lambda h, c: _dot(_chunk(_head(v, h), c), dst_at(h, c).astype(BF), NN))
        dv_inter = per_chunk(lambda h, c: _dot(_chunk(_head(kout, h), c), dst_at(h, c).astype(BF), NT))
        dg_state = per_chunk(lambda h, c: jnp.sum(dst_at(h, c) * st_prev(h, c).astype(F32), axis=0, keepdims=True))
        dg_kout = per_chunk(lambda h, c: jnp.sum(_chunk(_head(dkout * pre["kout"], h), c), axis=0, keepdims=True))
        dv = dv_intra + dv_inter
        pos = lax.broadcasted_iota(jnp.int32, (HGRN_ROWS, HGRN_W), 0) % HGRN_CHUNK
        dq = dqt * pre["e_q"] + dqg * pre["e_in"]
        dk = dkt * pre["e_k"] + dkout * pre["e_out"]
        dg = (dqt * pre["qt"] - dkt * pre["kt"] + dqg * pre["qg"] - dkout * pre["kout"]
              + jnp.where(pos == HGRN_CHUNK - 1, dg_state * dec + dg_kout, 0.0))
        dlogf = _chunk_cumsum(dg, reverse=True)
        sig, sq = pre["sig"], pre["sq"]
        df = dlogf / pre["f"] - dk
        dseg_ref[0] = (dq * HGRN_SCALE * (sq * (1.0 + qh * (1.0 - sq)))).astype(BF)
        dseg_ref[1] = (df * (1.0 - lbv) * sig * (1.0 - sig)).astype(BF)
        dseg_ref[2] = dv.astype(BF)
        dseg_ref[3] = dgate_in.astype(BF)
        _accumulate_rows(step, dlb_ref, _part8(df * (1.0 - sig)))
        _accumulate_rows(step, dgain_ref, _part8(dgain))

    specs, blk = _hgrn_in_specs(layer, True, nblk)
    specs += [pl.BlockSpec((HGRN_ROWS, HGRN_W), lambda b: (blk(b), 0)),
              pl.BlockSpec((HGRN_ROWS, HGRN_W), lambda b: (blk(b), 0)),
              pl.BlockSpec((HGRN_HEADS, cpb * LANES, LANES), lambda b: (0, blk(b), 0))]
    return pl.pallas_call(
        body, grid=(nblk,), in_specs=specs,
        out_specs=[pl.BlockSpec((4, HGRN_ROWS, HGRN_W), lambda b: (0, blk(b), 0)),
                   pl.BlockSpec((8, HGRN_W), lambda b: (0, 0)), pl.BlockSpec((8, HGRN_DIM), lambda b: (0, 0))],
        out_shape=[jax.ShapeDtypeStruct((4, s, HGRN_W), BF), jax.ShapeDtypeStruct((8, HGRN_W), F32),
                   jax.ShapeDtypeStruct((8, HGRN_DIM), F32)],
        scratch_shapes=[pltpu.VMEM((HGRN_HEADS, LANES, LANES), F32), pltpu.VMEM((HGRN_HEADS, cpb * LANES, LANES), F32)],
        compiler_params=_cparams(("arbitrary",)), name=f"hgrn_bwd_l{layer}",
    )(proj, proj, proj, proj, lb, gain, o, drec, states)


def _bwd_inproj(layer, dqkv, dhg, g_in, x, gain, dres, tm=512):
    s, d = x.shape

    def body(dqkv_ref, dhg_ref, w_ref, x_ref, gain_ref, dres_ref, dx_ref, dxb_ref, dgain_ref):
        acc = jnp.zeros((tm, d), F32)
        for seg in range(N_SEG):
            a = dqkv_ref[seg] if seg < 3 else dhg_ref[seg - 3]
            acc = acc + _dot(a, w_ref[seg * SEG:(seg + 1) * SEG, :], NN)
        dx, dgain = _rms_bwd(acc, x_ref[...], gain_ref[...])
        out = dres_ref[...] + dx
        dx_ref[...] = out
        dxb_ref[...] = out.astype(BF)
        _accumulate_rows(pl.program_id(0), dgain_ref, _part8(dgain))

    row = pl.BlockSpec((tm, d), lambda i: (i, 0))
    return pl.pallas_call(
        body, grid=(s // tm,),
        in_specs=[pl.BlockSpec((3, tm, SEG), lambda i: (0, i, 0)), pl.BlockSpec((4, tm, SEG), lambda i: (0, i, 0)),
                  _resident((None, PROJ_W, d), lambda i: (0, 0, 0)), row,
                  pl.BlockSpec((None, 1, d), lambda i: (layer, 0, 0)), row],
        out_specs=[row, row, pl.BlockSpec((8, d), lambda i: (0, 0))],
        out_shape=[jax.ShapeDtypeStruct((s, d), F32), jax.ShapeDtypeStruct((s, d), BF), jax.ShapeDtypeStruct((8, d), F32)],
        compiler_params=_cparams(("arbitrary",)), name=f"bwd_inproj_l{layer}",
    )(dqkv, dhg, g_in, x, gain, dres)


def _adamw(w, g, m, v):
    m2 = ADAM_B1 * m + (1.0 - ADAM_B1) * g
    v2 = ADAM_B2 * v + (1.0 - ADAM_B2) * (g * g)
    m_hat = m2 / (1.0 - ADAM_B1 ** ADAM_STEP)
    v_hat = v2 / (1.0 - ADAM_B2 ** ADAM_STEP)
    delta = -ADAM_LR * (m_hat / (jnp.sqrt(v_hat) + ADAM_EPS) + ADAM_WD * w)
    return delta, m2, v2


def _adam_big(name, parts, w, m, v, row_tiles):
    depth = w.shape[0]
    r, c = parts[0].shape[1], parts[0].shape[2]
    tr = r // row_tiles
    p_spec = pl.BlockSpec((N_DEV, tr, c), lambda t: (0, t, 0))
    w_spec = pl.BlockSpec((depth, tr, c), lambda t: (0, t, 0))

    def body(*refs):
        p_refs = refs[:depth]
        w_ref, m_ref, v_ref, g_ref, d_ref, m2_ref, v2_ref, token = refs[depth:]
        token[...] = jnp.zeros(token.shape, F32)
        for l in range(depth):
            g = p_refs[l][0].astype(F32)
            for dev in range(1, N_DEV):
                g = g + p_refs[l][dev].astype(F32)
            delta, m2, v2 = _adamw(w_ref[l], g, m_ref[l], v_ref[l])
            g_ref[l] = g
            d_ref[l] = delta
            m2_ref[l] = m2
            v2_ref[l] = v2

    return pl.pallas_call(
        body, grid=(row_tiles,), in_specs=[p_spec] * depth + [w_spec] * 3,
        out_specs=[w_spec] * 4 + [pl.BlockSpec((8, LANES), lambda t: (0, 0))],
        out_shape=[jax.ShapeDtypeStruct(w.shape, F32)] * 4 + [jax.ShapeDtypeStruct((8, LANES), F32)],
        compiler_params=_cparams(("arbitrary",)), name=name,
    )(*parts, w, m, v)


def _adam_small(g, ws, ms, vs):
    n = len(ws)

    def split(row, width):
        return jnp.concatenate([row[:, :width], row[:, width:2 * width]], axis=0)

    def body(g_ref, *refs):
        ins, outs = refs[:3 * n], refs[3 * n:]
        grads = [g_ref[0:2, :], split(g_ref[5:6, :], ATTN_W), split(g_ref[6:7, :], HGRN_W), split(g_ref[7:8, :], HGRN_DIM),
                 g_ref[2:4, :], g_ref[4:5, :]]
        for i, g_i in enumerate(grads):
            delta, m2, v2 = _adamw(ins[i][...], g_i, ins[n + i][...], ins[2 * n + i][...])
            for j, val in enumerate((g_i, delta, m2, v2)):
                outs[4 * i + j][...] = val

    vm = pl.BlockSpec(memory_space=pltpu.VMEM)
    res = pl.pallas_call(
        body, in_specs=[vm] * (1 + 3 * n), out_specs=[vm] * (4 * n),
        out_shape=[jax.ShapeDtypeStruct(w.shape, F32) for w in ws for _ in range(4)], name="adam_small",
    )(g, *ws, *ms, *vs)
    return [res[4 * i:4 * i + 4] for i in range(n)]


def _lower_bounds(logits):
    def body(l_ref, lb_ref, jac_ref):
        l0, l1 = l_ref[0:1, :], l_ref[1:2, :]
        mx = jnp.maximum(l0, l1)
        e0, e1 = jnp.exp(l0 - mx), jnp.exp(l1 - mx)
        p0, p1 = e0 / (e0 + e1), e1 / (e0 + e1)
        lb_ref[0:1, :] = p0 - p0
        lb_ref[1:2, :] = (p0 + p1) - p0
        jac_ref[0:1, :] = -p0 * p1
        jac_ref[1:2, :] = p0 * p1

    vm = pl.BlockSpec(memory_space=pltpu.VMEM)
    return pl.pallas_call(body, in_specs=[vm], out_specs=[vm, vm], out_shape=[jax.ShapeDtypeStruct(logits.shape, F32)] * 2,
                          name="hgrn_lower_bounds")(logits)


def _rope_tables(s, after):
    half = 32
    inv_freq = ROPE_THETA ** (-jnp.arange(half, dtype=F32) / half)
    ang = (jnp.arange(s, dtype=jnp.int32).astype(F32) + after[0, 0])[:, None] * inv_freq[None, :]
    cos, sin = jnp.cos(ang), jnp.sin(ang)
    return jnp.concatenate([cos] * 4, axis=1), jnp.concatenate([-sin, sin, -sin, sin], axis=1)


def kernel(x, norm_mix, w_in, attn_out_gain, hgrn_lb_logits, hgrn_out_gain, w_out, norm_mlp, w_up, w_down, norm_final, loss_target, m_norm_mix, m_w_in, m_attn_out_gain, m_hgrn_lb_logits, m_hgrn_out_gain, m_w_out, m_norm_mlp, m_w_up, m_w_down, m_norm_final, v_norm_mix, v_w_in, v_attn_out_gain, v_hgrn_lb_logits, v_hgrn_out_gain, v_w_out, v_norm_mlp, v_w_up, v_w_down, v_norm_final):
    depth = w_in.shape[0]
    assert depth == 2 and x.shape[0] == 1
    s, d = x.shape[1], x.shape[2]
    x0 = x[0]
    target = loss_target[0]
    g_mix, g_attn, g_hg, g_mlp = (norm_mix[:, None, :], attn_out_gain[:, None, :], hgrn_out_gain[:, None, :],
                                  norm_mlp[:, None, :])
    lb, lb_jac = _lower_bounds(hgrn_lb_logits)
    lb3 = lb[:, None, :]

    def flip(a):
        return jnp.swapaxes(a, 1, 2)

    shards = list(_pack_weights(flip(w_in), w_out, w_up, w_down))
    w_pieces = _weight_pieces(*shards)
    w_groups = [[0], [1, 2, 3], [4], [5, 6, 7]]
    me = (4 * lax.axis_index("x") + 2 * lax.axis_index("y") + lax.axis_index("c")).astype(jnp.int32).reshape(1)
    lands = _exchange_own("all_gather_own", me, shards, w_pieces)
    w_sems, shards, lands, token = _exchange_start("all_gather_start", shards, lands, w_pieces, w_groups, DIRECT_PEERS)

    def arrived(group, after):
        nonlocal shards
        idxs = w_groups[group]
        shards, got = _exchange_wait(f"all_gather_wait{group}", shards, [lands[i] for i in idxs], w_pieces,
                                     [(idxs, *w_sems[group])], after, DIRECT_PEERS)
        return got, [w_pieces[i] for i in idxs]

    def relaying(group, after):
        got, pieces = arrived(group, after)
        sems, got, tok = _relay_start(f"all_gather_relay_start{group}", got, pieces)
        return (group, got, pieces, sems), tok

    def relayed(handle, after):
        group, got, pieces, sems = handle
        return _relay_wait(f"all_gather_relay_wait{group}", got, pieces, sems, after)

    cos, sin = _rope_tables(s, token)

    def tied(small_arr, tok):
        return small_arr + tok[0, 0]

    saved = []
    xl = x0
    full = [None] * depth
    next_in = None
    for l in range(depth):
        (full_in,) = _relay_to_sibling("all_gather_relay0", *arrived(0, cos)) if l == 0 else relayed(next_in, xl)
        saved_x = xl
        proj, h = _fwd_inproj(l, xl, g_mix, full_in, cos, sin)
        o_attn, lse = _attn_fwd(l, proj)
        mlp_weights, tok = relaying(2 * l + 1, lse)
        o_hg, mixed, states = _hgrn_fwd(l, proj, lb3, tied(g_hg, tok))
        mixed = _attn_norm(l, o_attn, g_attn, mixed)
        full_out, full_up, full_down = relayed(mlp_weights, mixed)
        g_mlp_l = g_mlp
        if l + 1 < depth:
            next_in, tok = relaying(2 * l + 2, mixed)
            g_mlp_l = tied(g_mlp, tok)
        head = (norm_final[None, :], target) if l == depth - 1 else None
        xl, x_mid, u, h2, *loss_side = _mlp_fwd(l, xl, mixed, g_mlp_l, full_out, full_up, full_down, head)
        saved.append((saved_x, proj, h, o_attn, lse, o_hg, states, mixed, x_mid, u, h2))
        full[l] = (full_in, full_out, full_up, full_down)
    dx, (dxb, dnorm_final8, loss8) = xl, loss_side

    exchanges = []

    def scatter(tag, grads, kinds):
        pieces = _grad_pieces(grads, kinds)
        own = _exchange_own(f"reduce_scatter_own_{tag}", me, grads, pieces)
        sems, grads, own, tok = _exchange_start(f"reduce_scatter_start_{tag}", grads, own, pieces, [list(range(len(pieces)))])
        exchanges.append((grads, own, pieces, sems[0]))
        return tok

    small = {}
    for l in reversed(range(depth)):
        xl, proj, h, o_attn, lse, o_hg, states, mixed, x_mid, u, h2 = saved[l]
        full_in, full_out, full_up, full_down = full[l]
        hs = full_up.shape[3]
        gw_down = _mm_tn(f"grad_w_down_l{l}", u, dxb, u.shape[1], a_fn=_relu2)
        dx_mid, dx_mid_b, du, drec, do, delta, dmlp8, dattn8 = _mlp_bwd(l, dx, dxb, x_mid, g_mlp, u, o_attn, g_attn,
                                                                         full_out, full_up, full_down)
        gw_up = _mm_tn(f"grad_w_up_l{l}", h2, du, d, out_block_w=hs)
        gw_out = _mm_tn(f"grad_w_out_l{l}", mixed, dx_mid_b, mixed.shape[1])
        started = scatter(f"mlp_l{l}", [gw_down, gw_up, gw_out], ["rows", "up", "rows"])
        dqkv = _attn_bwd(l, proj, do, lse, delta, cos, sin, started)
        dhg, dlb8, dhgain8 = _hgrn_bwd(l, proj, lb3, g_hg, o_hg, drec, states)
        gin = _mm_tn(f"grad_w_in_qkv_l{l}", dqkv, h, PROJ_W, a_lead=True)
        gw_in = _mm_tn(f"grad_w_in_hg_l{l}", dhg, h, PROJ_W, a_lead=True, out_block_off=3, prev=gin)
        g_mix_t = tied(g_mix, scatter(f"mix_l{l}", [gw_in], ["rows"]))
        dx, dxb, dmix8 = _bwd_inproj(l, dqkv, dhg, full_in, xl, g_mix_t, dx_mid)
        small[l] = (dmix8, dattn8, dlb8, dhgain8, dmlp8)

    def scattered(name, which, after):
        grads, lands, pieces, waits = [], [], [], []
        for grads_e, own, pieces_e, (send, recv) in (exchanges[i] for i in which):
            first = len(pieces)
            pieces += [p._replace(src=p.src + len(grads)) for p in pieces_e]
            waits.append((list(range(first, first + len(pieces_e))), send, recv))
            grads += grads_e
            lands += own
        return _exchange_wait(name, grads, lands, pieces, waits, after)[1]

    down1, up1, out1, in1, down0, up0, out0 = scattered("reduce_scatter_wait_early", (0, 1, 2), dx)
    big = {
        "w_down": _adam_big("adam_w_down", [down0, down1], w_down, m_w_down, v_w_down, 4),
        "w_up": _adam_big("adam_w_up", [up0, up1], w_up, m_w_up, v_w_up, 2),
        "w_out": _adam_big("adam_w_out", [out0, out1], w_out, m_w_out, v_w_out, 1),
    }
    g_small = _all_reduce_small([small[l][0] for l in range(depth)], [small[l][4] for l in range(depth)], dnorm_final8,
                                [small[l][1] for l in range(depth)], small[depth - 1][2], lb_jac,
                                [small[l][3] for l in range(depth)], loss8, big["w_out"][4])
    loss = g_small[7, 2 * HGRN_DIM]
    row = lambda a: a[None, :]
    small_out = _adam_small(
        g_small, [norm_mix, attn_out_gain, hgrn_lb_logits, hgrn_out_gain, norm_mlp, row(norm_final)],
        [m_norm_mix, m_attn_out_gain, m_hgrn_lb_logits, m_hgrn_out_gain, m_norm_mlp, row(m_norm_final)],
        [v_norm_mix, v_attn_out_gain, v_hgrn_lb_logits, v_hgrn_out_gain, v_norm_mlp, row(v_norm_final)])
    small_out[5] = [t[0] for t in small_out[5]]
    (in0,) = scattered("reduce_scatter_wait_last", (3,), small_out[0][1])
    big["w_in"] = [flip(t) for t in _adam_big("adam_w_in", [in0, in1], flip(w_in), flip(m_w_in), flip(v_w_in), 2)[:4]]

    def gather(idx):
        mix, attn, lbl, hg, mlp, final = (t[idx] for t in small_out)
        return [mix, big["w_in"][idx], attn, lbl, hg, big["w_out"][idx], mlp, big["w_up"][idx], big["w_down"][idx], final]

    return (loss, dx[None], *gather(0), *gather(1), *gather(2), *gather(3))
```

```python
import functools
from typing import Callable, NamedTuple

import jax
import jax.numpy as jnp
from jax import lax
from jax.experimental import pallas as pl
from jax.experimental.pallas import tpu as pltpu

F32 = jnp.float32
BF = jnp.bfloat16

N_DEV = 8
ATTN_W = 512
HGRN_W = 512
HGRN_HEADS = 4
HGRN_DIM = 128
SEG = 512
N_SEG = 7
PROJ_W = N_SEG * SEG
MIX_W = ATTN_W + HGRN_W
SPAN = 128
DILATIONS = (1, 4, 16)
HGRN_CHUNK = 16
ROPE_THETA = 10000.0
NORM_EPS = 1e-6
MASK_VALUE = -1e30
ATTN_SCALE = 0.125
HGRN_SCALE = HGRN_DIM ** -0.5
ADAM_LR = 0.001
ADAM_B1 = 0.9
ADAM_B2 = 0.999
ADAM_EPS = 1e-08
ADAM_WD = 0.01
ADAM_STEP = 10
LANES = 128
VMEM_LIMIT = 56 * 1024 * 1024

NN = ((1,), (0,))
NT = ((1,), (1,))
TN = ((0,), (0,))
MESH = pl.DeviceIdType.MESH


def _dot(a, b, dims):
    return lax.dot_general(a, b, (dims, ((), ())), preferred_element_type=F32)


def _cparams(sem):
    return pltpu.CompilerParams(dimension_semantics=sem, vmem_limit_bytes=VMEM_LIMIT)


def _part8(x):
    r, n = x.shape
    return jnp.sum(x.reshape(r // 8, 8, n), axis=0)


def _sigmoid(x):
    return 1.0 / (1.0 + jnp.exp(-x))


def _rms_fwd(x, gain):
    r = lax.rsqrt(jnp.mean(x * x, axis=-1, keepdims=True) + NORM_EPS)
    return x * r * gain


def _rms_bwd(dy, x, gain):
    r = lax.rsqrt(jnp.mean(x * x, axis=-1, keepdims=True) + NORM_EPS)
    xn = x * r
    dxn = dy * gain
    dx = r * (dxn - xn * jnp.mean(dxn * xn, axis=-1, keepdims=True))
    return dx, dy * xn


def _rope_partner(x):
    n = x.shape[-1]
    lane = lax.broadcasted_iota(jnp.int32, x.shape, x.ndim - 1)
    return jnp.where((lane % 64) < 32, pltpu.roll(x, n - 32, x.ndim - 1), pltpu.roll(x, 32, x.ndim - 1))


def _tile_lanes(t, reps):
    return jnp.concatenate([t] * reps, axis=-1)


def _mm_tn(name, a, b, out_rows, a_lead=False, out_block_off=0, prev=None, out_block_w=None, a_fn=None,
           tm=512, tn=1024, sub=512):
    kdim, n = b.shape
    m = a.shape[-1]
    tm, tn, sub = min(tm, m), min(tn, n), min(sub, kdim)
    mt = m // tm
    n_lead = a.shape[0] if a_lead else 1
    if a_lead:
        a_spec = pl.BlockSpec((None, kdim, tm), lambda j, i: (i // mt, 0, i % mt))
    else:
        a_spec = pl.BlockSpec((kdim, tm), lambda j, i: (0, i))
    b_spec = pl.BlockSpec((kdim, tn), lambda j, i: (0, j))
    if out_block_w:
        nb = tn // out_block_w
        o_shape = jax.ShapeDtypeStruct((n // out_block_w, out_rows, out_block_w), BF)
        o_spec = pl.BlockSpec((nb, tm, out_block_w), lambda j, i: (j, i + out_block_off, 0))
    else:
        nb = 0
        o_shape = jax.ShapeDtypeStruct((out_rows, n), BF)
        o_spec = pl.BlockSpec((tm, tn), lambda j, i: (i + out_block_off, j))
    arrays, specs, aliases = [a, b], [a_spec, b_spec], {}
    if prev is not None:
        arrays.append(prev)
        specs.append(pl.BlockSpec(memory_space=pl.ANY))
        aliases = {2: 0}

    def body(*refs):
        a_ref, b_ref, o_ref = refs[0], refs[1], refs[-1]
        acc = None
        for k in range(kdim // sub):
            av = a_ref[k * sub:(k + 1) * sub, :]
            if a_fn is not None:
                av = a_fn(av)
            part = _dot(av, b_ref[k * sub:(k + 1) * sub, :], TN)
            acc = part if acc is None else acc + part
        if nb:
            for t in range(nb):
                o_ref[t] = acc[:, t * out_block_w:(t + 1) * out_block_w].astype(BF)
        else:
            o_ref[...] = acc.astype(BF)

    return pl.pallas_call(
        body, grid=(n // tn, n_lead * mt), in_specs=specs, out_specs=o_spec, out_shape=o_shape,
        compiler_params=_cparams(("parallel", "parallel")), name=name, input_output_aliases=aliases,
    )(*arrays)


def _pack_weights(w_in_t, w_out, w_up, w_down):
    depth = w_in_t.shape[0]
    arrays = (w_in_t, w_out, w_up, w_down)

    def body(*refs):
        for src, dst in zip(refs[:4], refs[4:]):
            dst[...] = src[...].astype(BF)

    specs = [pl.BlockSpec((None,) + a.shape[1:], lambda l: (l, 0, 0)) for a in arrays]
    return pl.pallas_call(
        body, grid=(depth,), in_specs=specs, out_specs=specs,
        out_shape=[jax.ShapeDtypeStruct(a.shape, BF) for a in arrays],
        compiler_params=_cparams(("arbitrary",)), name="pack_weights",
    )(*arrays)


def _my_position():
    x, y, c = lax.axis_index("x"), lax.axis_index("y"), lax.axis_index("c")
    return x, y, c, 4 * x + 2 * y + c


def _peer(x, y, c, k):
    px = 1 - x if k & 4 else x
    py = 1 - y if k & 2 else y
    pc = 1 - c if k & 1 else c
    return (px, py, pc), 4 * px + 2 * py + pc


PEER_ORDER = (1, 2, 4, 3, 5, 6, 7)
RELAYED_PEERS = (2, 4, 6)
DIRECT_PEERS = (1,) + RELAYED_PEERS


class _Piece(NamedTuple):
    src: int
    send: Callable
    slot: Callable
    land_shape: tuple
    own_src: tuple
    own_slot: tuple


HBM_SPEC = pl.BlockSpec(memory_space=pltpu.HBM)
SEM_SPEC = pl.BlockSpec(memory_space=pltpu.SEMAPHORE)
ANY_SPEC = pl.BlockSpec(memory_space=pl.ANY)


def _in_hbm(arrays):
    return [pltpu.with_memory_space_constraint(a, pltpu.HBM) for a in arrays]


def _hbm_like(arrays):
    return [pltpu.HBM(a.shape, a.dtype) for a in arrays]


def _rows_of(rows):
    return lambda ref, dev: ref.at[pl.ds(pl.multiple_of(dev * rows, 16), rows), :]


def _exchange_own(name, me, srcs, pieces):
    n = len(pieces)

    def body(me_ref, *refs):
        for i in range(n):
            refs[n + i][...] = refs[i][...]

    def spec(block_and_index):
        block, index = block_and_index
        return pl.BlockSpec(block, lambda i, me_ref: index(me_ref[0]))

    return pl.pallas_call(
        body,
        grid_spec=pltpu.PrefetchScalarGridSpec(
            num_scalar_prefetch=1, grid=(1,), in_specs=[spec(p.own_src) for p in pieces],
            out_specs=[spec(p.own_slot) for p in pieces]),
        out_shape=[jax.ShapeDtypeStruct(p.land_shape, BF) for p in pieces],
        compiler_params=_cparams(("arbitrary",)), name=name,
    )(me, *[srcs[p.src] for p in pieces])


def _exchange_start(name, srcs, lands, pieces, groups, peers=PEER_ORDER):
    n_src, n, n_g = len(srcs), len(pieces), len(groups)

    def body(*refs):
        src_refs, land_refs = refs[:n_src], refs[n_src:n_src + n]
        sems, token = refs[n_src + n:n_src + n + 2 * n_g], refs[-1]
        x, y, c, me = _my_position()
        for g, idxs in enumerate(groups):
            for k in peers:
                peer, pid = _peer(x, y, c, k)
                for j, i in enumerate(idxs):
                    p = pieces[i]
                    pltpu.make_async_remote_copy(
                        src_ref=p.send(src_refs[p.src], pid), dst_ref=p.slot(land_refs[i], me),
                        send_sem=sems[2 * g].at[(k - 1) * len(idxs) + j], recv_sem=sems[2 * g + 1].at[(k - 1) * len(idxs) + j],
                        device_id=peer, device_id_type=MESH).start()
        token[...] = jnp.zeros(token.shape, F32)

    sem_shapes = [pltpu.SemaphoreType.DMA(((N_DEV - 1) * len(idxs),)) for idxs in groups for _ in range(2)]
    res = pl.pallas_call(
        body, in_specs=[HBM_SPEC] * (n_src + n),
        out_specs=[SEM_SPEC] * (2 * n_g) + [HBM_SPEC] * (n_src + n) + [pl.BlockSpec(memory_space=pltpu.VMEM)],
        out_shape=sem_shapes + _hbm_like(srcs) + _hbm_like(lands) + [jax.ShapeDtypeStruct((8, LANES), F32)],
        input_output_aliases={i: 2 * n_g + i for i in range(n_src + n)},
        compiler_params=pltpu.CompilerParams(has_side_effects=pltpu.SideEffectType.DATAFLOW_SIDE_EFFECTING),
        name=name,
    )(*_in_hbm(srcs), *_in_hbm(lands))
    sems = [(res[2 * g], res[2 * g + 1]) for g in range(n_g)]
    return sems, list(res[2 * n_g:2 * n_g + n_src]), list(res[2 * n_g + n_src:2 * n_g + n_src + n]), res[-1]


def _relay_to_sibling(name, lands, pieces):
    n = len(lands)

    def body(*refs):
        land_refs, send_sems, recv_sems = refs[:n], refs[-2], refs[-1]
        x, y, c, me = _my_position()
        sibling, _ = _peer(x, y, c, 1)

        def passed_on(k_from, k_as, j):
            _, pid = _peer(x, y, c, k_from)
            slot = pieces[j].slot(land_refs[j], pid)
            at = RELAYED_PEERS.index(k_as - 1) * n + j
            return pltpu.make_async_remote_copy(src_ref=slot, dst_ref=slot, send_sem=send_sems.at[at], recv_sem=recv_sems.at[at],
                                                device_id=sibling, device_id_type=MESH)

        for k in RELAYED_PEERS:
            for j in range(n):
                passed_on(k, k + 1, j).start()
        for k in RELAYED_PEERS:
            for j in range(n):
                passed_on(k + 1, k + 1, j).wait_recv()
                passed_on(k, k + 1, j).wait_send()

    res = pl.pallas_call(
        body, in_specs=[HBM_SPEC] * n, out_specs=[HBM_SPEC] * n, out_shape=_hbm_like(lands),
        input_output_aliases={i: i for i in range(n)},
        scratch_shapes=[pltpu.SemaphoreType.DMA((len(RELAYED_PEERS) * n,))] * 2,
        compiler_params=pltpu.CompilerParams(has_side_effects=pltpu.SideEffectType.DATAFLOW_SIDE_EFFECTING),
        name=name,
    )(*_in_hbm(lands))
    return list(res)


def _relay_copies(pieces, land_refs, send_sems, recv_sems):
    n = len(pieces)
    x, y, c, me = _my_position()
    sibling, _ = _peer(x, y, c, 1)

    def passed_on(k_from, k_as, j):
        _, pid = _peer(x, y, c, k_from)
        slot = pieces[j].slot(land_refs[j], pid)
        at = RELAYED_PEERS.index(k_as - 1) * n + j
        return pltpu.make_async_remote_copy(src_ref=slot, dst_ref=slot, send_sem=send_sems.at[at], recv_sem=recv_sems.at[at],
                                            device_id=sibling, device_id_type=MESH)

    pairs = [(k, j) for k in RELAYED_PEERS for j in range(n)]
    return [passed_on(k, k + 1, j) for k, j in pairs], [passed_on(k + 1, k + 1, j) for k, j in pairs]


def _arrive_and_relay(name, srcs, lands, pieces, wait, after):
    idxs, direct_send, direct_recv = wait
    n_src, n = len(srcs), len(lands)

    def body(*refs):
        src_refs, land_refs = refs[:n_src], refs[n_src:n_src + n]
        d_send, d_recv = refs[n_src + n], refs[n_src + n + 1]
        r_send, r_recv, token = refs[n_src + n + 3], refs[n_src + n + 4], refs[-1]
        x, y, c, me = _my_position()
        for k in DIRECT_PEERS:
            peer, pid = _peer(x, y, c, k)
            for j, i in enumerate(idxs):
                p = pieces[i]
                cp = pltpu.make_async_remote_copy(
                    src_ref=p.send(src_refs[p.src], pid), dst_ref=p.slot(land_refs[j], pid),
                    send_sem=d_send.at[(k - 1) * n + j], recv_sem=d_recv.at[(k - 1) * n + j],
                    device_id=peer, device_id_type=MESH)
                cp.wait_send()
                cp.wait_recv()
        out, _ = _relay_copies([pieces[i] for i in idxs], land_refs, r_send, r_recv)
        for cp in out:
            cp.start()
        token[...] = jnp.zeros(token.shape, F32)

    sems = [pltpu.SemaphoreType.DMA((len(RELAYED_PEERS) * n,)) for _ in range(2)]
    res = pl.pallas_call(
        body, in_specs=[HBM_SPEC] * (n_src + n) + [SEM_SPEC] * 2 + [ANY_SPEC],
        out_specs=[SEM_SPEC] * 2 + [HBM_SPEC] * (n_src + n) + [pl.BlockSpec(memory_space=pltpu.VMEM)],
        out_shape=sems + _hbm_like(srcs) + _hbm_like(lands) + [jax.ShapeDtypeStruct((8, LANES), F32)],
        input_output_aliases={i: 2 + i for i in range(n_src + n)},
        compiler_params=pltpu.CompilerParams(has_side_effects=pltpu.SideEffectType.DATAFLOW_SIDE_EFFECTING),
        name=name,
    )(*srcs, *lands, direct_send, direct_recv, after)
    return (res[0], res[1]), list(res[2:2 + n_src]), list(res[2 + n_src:2 + n_src + n]), res[-1]


def _relay_wait(name, lands, pieces, sems, after):
    n = len(lands)

    def body(*refs):
        out, arriving = _relay_copies(pieces, refs[:n], refs[n], refs[n + 1])
        for mine, theirs in zip(out, arriving):
            theirs.wait_recv()
            mine.wait_send()

    res = pl.pallas_call(
        body, in_specs=[HBM_SPEC] * n + [SEM_SPEC] * 2 + [ANY_SPEC], out_specs=[HBM_SPEC] * n, out_shape=_hbm_like(lands),
        input_output_aliases={i: i for i in range(n)},
        compiler_params=pltpu.CompilerParams(has_side_effects=pltpu.SideEffectType.DATAFLOW_SIDE_EFFECTING),
        name=name,
    )(*lands, *sems, after)
    return list(res)


def _exchange_wait(name, srcs, lands, pieces, waits, after, peers=PEER_ORDER):
    n_src, n, n_g = len(srcs), len(lands), len(waits)

    def body(*refs):
        src_refs, land_refs = refs[:n_src], refs[n_src:n_src + n]
        sems = refs[n_src + n:n_src + n + 2 * n_g]
        x, y, c, me = _my_position()
        at = 0
        for g, (idxs, _, _) in enumerate(waits):
            for k in peers:
                peer, pid = _peer(x, y, c, k)
                for j, i in enumerate(idxs):
                    p = pieces[i]
                    cp = pltpu.make_async_remote_copy(
                        src_ref=p.send(src_refs[p.src], pid), dst_ref=p.slot(land_refs[at + j], pid),
                        send_sem=sems[2 * g].at[(k - 1) * len(idxs) + j], recv_sem=sems[2 * g + 1].at[(k - 1) * len(idxs) + j],
                        device_id=peer, device_id_type=MESH)
                    cp.wait_send()
                    cp.wait_recv()
            at += len(idxs)

    sem_args = [s for _, send, recv in waits for s in (send, recv)]
    res = pl.pallas_call(
        body, in_specs=[HBM_SPEC] * (n_src + n) + [SEM_SPEC] * (2 * n_g) + [ANY_SPEC],
        out_specs=[HBM_SPEC] * (n_src + n), out_shape=_hbm_like(srcs) + _hbm_like(lands),
        input_output_aliases={i: i for i in range(n_src + n)},
        compiler_params=pltpu.CompilerParams(has_side_effects=pltpu.SideEffectType.DATAFLOW_SIDE_EFFECTING),
        name=name,
    )(*srcs, *lands, *sem_args, after)
    return list(res[:n_src]), list(res[n_src:])


def _weight_pieces(p_in, p_out, p_up, p_down):
    depth, cin, d = p_in.shape
    rout, hs = p_out.shape[1], p_up.shape[2]
    pieces = []
    for l in range(depth):
        whole = functools.partial(lambda ref, dev, l: ref.at[l], l=l)
        layer = functools.partial(lambda dev, l: (l, 0, 0), l=l)

        def rows(src, n_rows, whole=whole, layer=layer):
            return _Piece(src, whole, lambda ref, dev: _rows_of(n_rows)(ref.at[0], dev), (1, N_DEV * n_rows, d),
                          ((None, n_rows, d), layer), ((None, n_rows, d), lambda dev: (0, dev, 0)))

        pieces += [
            rows(0, cin), rows(1, rout),
            _Piece(2, whole, lambda ref, dev: ref.at[0, dev], (1, N_DEV, d, hs),
                   ((None, d, hs), layer), ((None, None, d, hs), lambda dev: (0, dev, 0, 0))),
            rows(3, hs),
        ]
    return pieces


def _grad_pieces(g_pair, kinds):
    pieces = []
    for i, (g, kind) in enumerate(zip(g_pair, kinds)):
        lead = lambda dev: (dev, 0, 0)
        if kind == "up":
            blk = ((None,) + g.shape[1:], lead)
            pieces.append(_Piece(i, lambda ref, dev: ref.at[dev], lambda ref, dev: ref.at[dev], g.shape, blk, blk))
        else:
            rows, cols = g.shape[0] // N_DEV, g.shape[1]
            pieces.append(_Piece(i, _rows_of(rows), lambda ref, dev: ref.at[dev], (N_DEV, rows, cols),
                                 ((rows, cols), lambda dev: (dev, 0)), ((None, rows, cols), lead)))
    return pieces


SMALL_W = 1024


def _all_reduce_small(mix8, mlp8, final8, attn8, lb8_last, lb_jac, hg8, loss8, after):
    def body(mix0, mix1, mlp0, mlp1, fin, attn0, attn1, lb, jac, hg0, hg1, loss, after_ref, o_ref, src_ref, buf_ref,
             send_sems, recv_sems):
        def total(ref):
            return jnp.sum(ref[...], axis=0, keepdims=True)

        dlb = total(lb)
        hg = jnp.concatenate([total(hg0), total(hg1), total(loss)], axis=1)
        src_ref[...] = jnp.concatenate([
            total(mix0), total(mix1), total(mlp0), total(mlp1), total(fin),
            jnp.concatenate([total(attn0), total(attn1)], axis=1),
            jnp.concatenate([jac[0:1, :] * dlb, jac[1:2, :] * dlb], axis=1),
            jnp.concatenate([hg, jnp.zeros((1, SMALL_W - hg.shape[1]), F32)], axis=1)], axis=0)
        x, y, c, me = _my_position()
        buf_ref[me] = src_ref[...]
        sends = []
        for k in PEER_ORDER:
            peer, _ = _peer(x, y, c, k)
            cp = pltpu.make_async_remote_copy(src_ref=src_ref, dst_ref=buf_ref.at[me], send_sem=send_sems.at[k - 1],
                                              recv_sem=recv_sems.at[k - 1], device_id=peer, device_id_type=MESH)
            cp.start()
            sends.append(cp)
        for k in PEER_ORDER:
            peer, pid = _peer(x, y, c, k)
            pltpu.make_async_remote_copy(src_ref=src_ref, dst_ref=buf_ref.at[pid], send_sem=send_sems.at[k - 1],
                                         recv_sem=recv_sems.at[k - 1], device_id=peer, device_id_type=MESH).wait_recv()
        for cp in sends:
            cp.wait_send()
        acc = buf_ref[0]
        for dev in range(1, N_DEV):
            acc = acc + buf_ref[dev]
        o_ref[...] = acc

    assert mix8[0].shape[1] == SMALL_W
    vm = pl.BlockSpec(memory_space=pltpu.VMEM)
    return pl.pallas_call(
        body, in_specs=[vm] * 12 + [ANY_SPEC], out_specs=vm, out_shape=jax.ShapeDtypeStruct((8, SMALL_W), F32),
        scratch_shapes=[pltpu.VMEM((8, SMALL_W), F32), pltpu.VMEM((N_DEV, 8, SMALL_W), F32),
                        pltpu.SemaphoreType.DMA((N_DEV - 1,)), pltpu.SemaphoreType.DMA((N_DEV - 1,))],
        name="all_reduce_small",
    )(*mix8, *mlp8, final8, *attn8, lb8_last, lb_jac, *hg8, loss8, after)


def _resident(block_shape, index_map):
    return pl.BlockSpec(block_shape, index_map, pipeline_mode=pl.Buffered(1))


def _fwd_inproj(layer, x, gain, g_in, cos, sin, tm=512):
    s, d = x.shape

    def body(x_ref, gain_ref, w_ref, cos_ref, sin_ref, proj_ref, h_ref):
        h = _rms_fwd(x_ref[...], gain_ref[...]).astype(BF)
        h_ref[...] = h
        cs = _tile_lanes(cos_ref[...], SEG // LANES)
        sn = _tile_lanes(sin_ref[...], SEG // LANES)
        for seg in range(N_SEG):
            acc = _dot(h, w_ref[seg * SEG:(seg + 1) * SEG, :], NT)
            if seg < 2:
                acc = acc * cs + _rope_partner(acc) * sn
            if seg == 0:
                acc = acc * ATTN_SCALE
            proj_ref[:, seg * SEG:(seg + 1) * SEG] = acc

    return pl.pallas_call(
        body, grid=(s // tm,),
        in_specs=[pl.BlockSpec((tm, d), lambda i: (i, 0)), pl.BlockSpec((None, 1, d), lambda i: (layer, 0, 0)),
                  _resident((None, PROJ_W, d), lambda i: (0, 0, 0)),
                  pl.BlockSpec((tm, LANES), lambda i: (i, 0)), pl.BlockSpec((tm, LANES), lambda i: (i, 0))],
        out_specs=[pl.BlockSpec((tm, PROJ_W), lambda i: (i, 0)), pl.BlockSpec((tm, d), lambda i: (i, 0))],
        out_shape=[jax.ShapeDtypeStruct((s, PROJ_W), F32), jax.ShapeDtypeStruct((s, d), BF)],
        compiler_params=_cparams(("parallel",)), name=f"fwd_inproj_l{layer}",
    )(x, gain, g_in, cos, sin)


ATTN_UNIT = SPAN * max(DILATIONS)
ATTN_GROUP = 4


def _attn_masks(first_block_has_prev):
    row = lax.broadcasted_iota(jnp.int32, (SPAN, 2 * SPAN), 0)
    col = lax.broadcasted_iota(jnp.int32, (SPAN, 2 * SPAN), 1)
    band = (col >= row) & (col <= row + SPAN)
    lane = lax.broadcasted_iota(jnp.int32, (SPAN, LANES), 1)
    return band & ((col >= SPAN) | first_block_has_prev), band, lane < 64


def _attn_specs(n_in_extra, unit_of=lambda n: n):
    pairs = ATTN_W // LANES
    q_spec = pl.BlockSpec((ATTN_UNIT, LANES), lambda p, n: (unit_of(n), p))

    def prev(seg):
        return pl.BlockSpec((ATTN_UNIT, LANES), lambda p, n: (jnp.maximum(unit_of(n) - 1, 0), seg * pairs + p))

    def cur(seg):
        return pl.BlockSpec((ATTN_UNIT, LANES), lambda p, n: (unit_of(n), seg * pairs + p))

    return [q_spec, prev(1), cur(1), prev(2), cur(2)] + [q_spec] * n_in_extra


def _attn_groups(dil):
    blocks = ATTN_UNIT // (SPAN * dil)
    pairs = [(r, b) for r in range(dil) for b in range(blocks)]
    return [pairs[i:i + ATTN_GROUP] for i in range(0, len(pairs), ATTN_GROUP)]


def _block_rows(dil, r, b, n=1):
    start = r + dil * SPAN * b
    return pl.ds(start, n * SPAN, stride=dil) if dil > 1 else pl.ds(start, n * SPAN)


def _block_keys(prev_ref, cur_ref, dil, r, b):
    if b > 0:
        return cur_ref[_block_rows(dil, r, b - 1, 2), :]
    last = ATTN_UNIT // (SPAN * dil) - 1
    return jnp.concatenate([prev_ref[_block_rows(dil, r, last), :], cur_ref[_block_rows(dil, r, 0), :]], axis=0)


def _attn_fwd(layer, proj):
    s = proj.shape[0]
    n_pat = len(DILATIONS)
    merge_rows = 256

    def body(q_ref, kp_ref, kc_ref, vp_ref, vc_ref, o_ref, lse_ref, o_scr, lse_scr):
        m_first, m_rest, is_a = _attn_masks(pl.program_id(1) > 0)
        sels = (is_a, jnp.logical_not(is_a))
        is_a_keys = lax.broadcasted_iota(jnp.int32, (2 * SPAN, LANES), 1) < 64
        for pi, dil in enumerate(DILATIONS):
            for group in _attn_groups(dil):
                items = [(r, b, h) for r, b in group for h in range(2)]
                q = {rb: q_ref[_block_rows(dil, *rb), :] for rb in group}
                k = {rb: _block_keys(kp_ref, kc_ref, dil, *rb).astype(BF) for rb in group}
                v = {rb: _block_keys(vp_ref, vc_ref, dil, *rb).astype(BF) for rb in group}
                v_sum = {rb: (jnp.where(is_a_keys, v[rb], 1.0), jnp.where(is_a_keys, 1.0, v[rb])) for rb in group}
                sc = [jnp.where(m_first if b == 0 else m_rest,
                                _dot(jnp.where(sels[h], q[r, b], 0.0).astype(BF), k[r, b], NT), MASK_VALUE)
                      for r, b, h in items]
                mx = [jnp.max(jnp.maximum(t[:, :SPAN], t[:, SPAN:]), axis=-1, keepdims=True) for t in sc]
                p = [jnp.exp(t - m).astype(BF) for t, m in zip(sc, mx)]
                both = [_dot(t, v_sum[r, b][h], NN) for t, (r, b, h) in zip(p, items)]
                for j, (r, b) in enumerate(group):
                    t_a, t_b = both[2 * j], both[2 * j + 1]
                    den = pltpu.roll(jnp.where(is_a, t_b, t_a), 64, 1)
                    o_scr[pi, _block_rows(dil, r, b), :] = jnp.where(is_a, t_a, t_b) / den
                    lse_scr[pi, _block_rows(dil, r, b), :] = jnp.where(is_a, mx[2 * j], mx[2 * j + 1]) + jnp.log(den)
        for c in range(ATTN_UNIT // merge_rows):
            rows = slice(c * merge_rows, (c + 1) * merge_rows)
            ls = [lse_scr[pi, rows, :] for pi in range(n_pat)]
            mx = functools.reduce(jnp.maximum, ls)
            ws = [jnp.exp(l - mx) for l in ls]
            den = functools.reduce(jnp.add, ws)
            o_ref[rows, :] = functools.reduce(jnp.add, [w * o_scr[pi, rows, :] for pi, w in enumerate(ws)]) / den
            lse_ref[rows, :] = mx + jnp.log(den)

    out_spec = pl.BlockSpec((ATTN_UNIT, LANES), lambda p, n: (n, p))
    return pl.pallas_call(
        body, grid=(ATTN_W // LANES, s // ATTN_UNIT), in_specs=_attn_specs(0), out_specs=[out_spec, out_spec],
        out_shape=[jax.ShapeDtypeStruct((s, ATTN_W), F32)] * 2,
        scratch_shapes=[pltpu.VMEM((n_pat, ATTN_UNIT, LANES), F32)] * 2,
        compiler_params=_cparams(("parallel", "arbitrary")), name=f"attn_fwd_l{layer}",
    )(proj, proj, proj, proj, proj)


def _attn_norm(layer, o, gain, mixed, tm=512):
    s = o.shape[0]

    def body(o_ref, gain_ref, mixed_ref, n_ref):
        n_ref[...] = _rms_fwd(o_ref[...], gain_ref[...]).astype(BF)

    blk = pl.BlockSpec((tm, ATTN_W), lambda i: (i, 0))
    return pl.pallas_call(
        body, grid=(s // tm,),
        in_specs=[blk, pl.BlockSpec((None, 1, ATTN_W), lambda i: (layer, 0, 0)), pl.BlockSpec(memory_space=pl.ANY)],
        out_specs=blk, out_shape=jax.ShapeDtypeStruct(mixed.shape, BF), input_output_aliases={2: 0},
        compiler_params=_cparams(("parallel",)), name=f"attn_norm_l{layer}",
    )(o, gain, mixed)


def _chunk_cumsum(x, reverse=False):
    n = x.shape[0]
    pos = lax.broadcasted_iota(jnp.int32, x.shape, 0) % HGRN_CHUNK
    for sh in (1, 2, 4, 8):
        if reverse:
            x = x + jnp.where(pos < HGRN_CHUNK - sh, pltpu.roll(x, n - sh, 0), 0.0)
        else:
            x = x + jnp.where(pos >= sh, pltpu.roll(x, sh, 0), 0.0)
    return x


def _chunk_row(x, row):
    r, n = x.shape
    x3 = x.reshape(r // HGRN_CHUNK, HGRN_CHUNK, n)
    return jnp.broadcast_to(x3[:, row:row + 1, :], x3.shape).reshape(r, n)


def _hgrn_pre(qh, z, lb):
    sig = _sigmoid(z)
    f = lb + (1.0 - lb) * sig
    k = 1.0 - f
    sq = _sigmoid(qh)
    q = qh * sq * HGRN_SCALE
    g = _chunk_cumsum(jnp.log(f))
    g_mid = _chunk_row(g, HGRN_CHUNK // 2 - 1)
    g_last = _chunk_row(g, HGRN_CHUNK - 1)
    e_q, e_k = jnp.exp(g - g_mid), jnp.exp(g_mid - g)
    e_in, e_out = jnp.exp(g), jnp.exp(g_last - g)
    return dict(sig=sig, f=f, k=k, sq=sq, q=q, g_last=g_last, e_q=e_q, e_k=e_k, e_in=e_in, e_out=e_out,
                qt=q * e_q, kt=k * e_k, qg=q * e_in, kout=k * e_out)


def _hgrn_mask():
    row = lax.broadcasted_iota(jnp.int32, (LANES, LANES), 0)
    col = lax.broadcasted_iota(jnp.int32, (LANES, LANES), 1)
    return (row // HGRN_CHUNK == col // HGRN_CHUNK) & (col <= row)


def _hgrn_in_specs(layer, rev, nblk):
    def blk(b):
        return nblk - 1 - b if rev else b
    first = 3 * ATTN_W // HGRN_W
    specs = [pl.BlockSpec((HGRN_ROWS, HGRN_W), functools.partial(lambda b, seg: (blk(b), first + seg), seg=seg))
             for seg in range(4)]
    specs.append(pl.BlockSpec((None, 1, HGRN_W), lambda b: (layer, 0, 0)))
    specs.append(pl.BlockSpec((None, 1, HGRN_DIM), lambda b: (layer, 0, 0)))
    return specs, blk


def _head(x, h):
    return x[:, h * HGRN_DIM:(h + 1) * HGRN_DIM]


def _chunk(x, c):
    return x[c * HGRN_CHUNK:(c + 1) * HGRN_CHUNK]


def _sub(x, sb):
    return x[sb * LANES:(sb + 1) * LANES]


HGRN_ROWS = 256
HEADS = range(HGRN_HEADS)
SUBS = range(HGRN_ROWS // LANES)
CHUNKS = range(HGRN_ROWS // HGRN_CHUNK)


def _hgrn_fwd(layer, proj, lb, gain):
    s = proj.shape[0]
    nblk = s // HGRN_ROWS
    cpb = len(CHUNKS)

    def body(q_ref, f_ref, i_ref, g_ref, lb_ref, gain_ref, o_ref, rec_ref, st_ref, state):
        @pl.when(pl.program_id(0) == 0)
        def _():
            state[...] = jnp.zeros(state.shape, F32)

        pre = _hgrn_pre(q_ref[...], f_ref[...], lb_ref[...])
        v = i_ref[...].astype(BF)
        qt, kt, qg, kout = (pre[n].astype(BF) for n in ("qt", "kt", "qg", "kout"))
        dec = jnp.exp(pre["g_last"])
        mask = _hgrn_mask()
        a = [[jnp.where(mask, _dot(_sub(_head(qt, h), sb), _sub(_head(kt, h), sb), NT), 0.0).astype(BF) for sb in SUBS]
             for h in HEADS]
        o_intra = [[_dot(a[h][sb], _sub(_head(v, h), sb), NN) for sb in SUBS] for h in HEADS]
        update = [[_dot(_chunk(_head(v, h), c), _chunk(_head(kout, h), c), TN) for c in CHUNKS] for h in HEADS]
        for h in HEADS:
            st = state[h]
            for c in CHUNKS:
                st_ref[h, c * LANES:(c + 1) * LANES, :] = st.astype(BF)
                st = st * _head(dec, h)[c * HGRN_CHUNK:c * HGRN_CHUNK + 1, :] + update[h][c]
            state[h] = st
        inter = [[_dot(_chunk(_head(qg, h), c), st_ref[h, c * LANES:(c + 1) * LANES, :].astype(BF), NT) for c in CHUNKS]
                 for h in HEADS]
        o = [jnp.concatenate(o_intra[h], axis=0) + jnp.concatenate(inter[h], axis=0) for h in HEADS]
        o_ref[...] = jnp.concatenate(o, axis=1)
        gate = g_ref[...]
        normed = jnp.concatenate([_rms_fwd(o[h], gain_ref[...]) for h in HEADS], axis=1)
        rec_ref[...] = (normed * (gate * _sigmoid(gate))).astype(BF)

    specs, _ = _hgrn_in_specs(layer, False, nblk)
    return pl.pallas_call(
        body, grid=(nblk,), in_specs=specs,
        out_specs=[pl.BlockSpec((HGRN_ROWS, HGRN_W), lambda b: (b, 0)), pl.BlockSpec((HGRN_ROWS, HGRN_W), lambda b: (b, 1)),
                   pl.BlockSpec((HGRN_HEADS, cpb * LANES, LANES), lambda b: (0, b, 0))],
        out_shape=[jax.ShapeDtypeStruct((s, HGRN_W), F32), jax.ShapeDtypeStruct((s, MIX_W), BF),
                   jax.ShapeDtypeStruct((HGRN_HEADS, nblk * cpb * LANES, LANES), BF)],
        scratch_shapes=[pltpu.VMEM((HGRN_HEADS, LANES, LANES), F32)],
        compiler_params=_cparams(("arbitrary",)), name=f"hgrn_fwd_l{layer}",
    )(proj, proj, proj, proj, lb, gain)


def _relu2(u):
    return jnp.square(jnp.maximum(u, 0)).astype(BF)


def _mlp_fwd(layer, x, mixed, gain, g_out, g_up, g_down, head=None):
    s, d = x.shape
    mw = mixed.shape[1]
    nblk, hs = g_up.shape[1], g_up.shape[3]
    tm = 512

    def body(x_ref, m_ref, gain_ref, out_w_ref, up_ref, down_ref, *refs):
        if head:
            fin_ref, t_ref, o_ref, mid_ref, u_ref, h_ref, ob_ref, dfin_ref, loss_ref, a_buf = refs
        else:
            o_ref, mid_ref, u_ref, h_ref, a_buf = refs
        xv = x_ref[...] + _dot(m_ref[...], out_w_ref[...], NN)
        mid_ref[...] = xv
        h = _rms_fwd(xv, gain_ref[...]).astype(BF)
        h_ref[...] = h
        for j in range(nblk):
            u = _dot(h, up_ref[j], NN)
            u_ref[:, j * hs:(j + 1) * hs] = u.astype(BF)
            a_buf[:, j * hs:(j + 1) * hs] = _relu2(u)
        acc = xv
        for j in range(nblk):
            acc = acc + _dot(a_buf[:, j * hs:(j + 1) * hs], down_ref[j * hs:(j + 1) * hs, :], NN)
        if not head:
            o_ref[...] = acc
            return
        fin = fin_ref[...]
        err = _rms_fwd(acc, fin) - t_ref[...]
        dout, dfin = _rms_bwd(err * (1.0 / d), acc, fin)
        o_ref[...] = dout
        ob_ref[...] = dout.astype(BF)
        step = pl.program_id(0)
        _accumulate_rows(step, dfin_ref, _part8(dfin))
        _accumulate_rows(step, loss_ref, _part8(0.5 * jnp.mean(err * err, axis=-1, keepdims=True) * jnp.ones((1, LANES), F32)))

    row = pl.BlockSpec((tm, d), lambda i: (i, 0))
    in_specs = [row, pl.BlockSpec((tm, mw), lambda i: (i, 0)), pl.BlockSpec((None, 1, d), lambda i: (layer, 0, 0)),
                _resident((None, mw, d), lambda i: (0, 0, 0)),
                _resident((None, nblk, d, hs), lambda i: (0, 0, 0, 0)),
                _resident((None, nblk * hs, d), lambda i: (0, 0, 0))]
    out_specs = [row, row, pl.BlockSpec((tm, nblk * hs), lambda i: (i, 0)), row]
    out_shape = [jax.ShapeDtypeStruct((s, d), F32), jax.ShapeDtypeStruct((s, d), F32),
                 jax.ShapeDtypeStruct((s, nblk * hs), BF), jax.ShapeDtypeStruct((s, d), BF)]
    args = [x, mixed, gain, g_out, g_up, g_down]
    if head:
        in_specs += [pl.BlockSpec((1, d), lambda i: (0, 0)), row]
        out_specs += [row, pl.BlockSpec((8, d), lambda i: (0, 0)), pl.BlockSpec((8, LANES), lambda i: (0, 0))]
        out_shape += [jax.ShapeDtypeStruct((s, d), BF), jax.ShapeDtypeStruct((8, d), F32), jax.ShapeDtypeStruct((8, LANES), F32)]
        args += list(head)
    return pl.pallas_call(
        body, grid=(s // tm,), in_specs=in_specs, out_specs=out_specs, out_shape=out_shape,
        scratch_shapes=[pltpu.VMEM((tm, nblk * hs), BF)],
        compiler_params=_cparams(("arbitrary",) if head else ("parallel",)), name=f"mlp_fwd_l{layer}",
    )(*args)


def _accumulate_rows(i, ref, part):
    @pl.when(i == 0)
    def _():
        ref[...] = part

    @pl.when(i > 0)
    def _():
        ref[...] += part


def _head_sums(prod):
    row = lax.broadcasted_iota(jnp.int32, (ATTN_W, ATTN_W), 0)
    col = lax.broadcasted_iota(jnp.int32, (ATTN_W, ATTN_W), 1)
    same_head = jnp.where(row // 64 == col // 64, 1.0, 0.0).astype(BF)
    high = prod.astype(BF)
    low = (prod - high.astype(F32)).astype(BF)
    return _dot(high, same_head, NN) + _dot(low, same_head, NN)


def _mlp_bwd(layer, dx, dxb, x, gain, u, o_attn, gain_attn, g_out, g_up, g_down, tm=256):
    s, d = x.shape
    mw = g_out.shape[1]
    nblk, hs = g_up.shape[1], g_up.shape[3]

    def body(dx_ref, dxb_ref, x_ref, gain_ref, u_ref, oa_ref, ga_ref, out_w_ref, up_ref, down_ref, o_ref, ob_ref, du_ref,
             drec_ref, do_ref, delta_ref, dgain_ref, dattn_ref):
        dxb_v = dxb_ref[...]
        for j in range(nblk):
            cols = slice(j * hs, (j + 1) * hs)
            da = _dot(dxb_v, down_ref[cols, :], NT)
            du_ref[:, cols] = (da * (2.0 * jnp.maximum(u_ref[:, cols].astype(F32), 0.0))).astype(BF)
        acc = jnp.zeros((tm, d), F32)
        for j in range(nblk):
            acc = acc + _dot(du_ref[:, j * hs:(j + 1) * hs], up_ref[j], NT)
        dxn, dgain = _rms_bwd(acc, x_ref[...], gain_ref[...])
        out = dx_ref[...] + dxn
        out_b = out.astype(BF)
        o_ref[...] = out
        ob_ref[...] = out_b
        dm = _dot(out_b, out_w_ref[...], NT)
        drec_ref[...] = dm[:, ATTN_W:]
        ov = oa_ref[...]
        do, dattn = _rms_bwd(dm[:, :ATTN_W], ov, ga_ref[...])
        do_ref[...] = do
        delta_ref[...] = _head_sums(do * ov)
        _accumulate_rows(pl.program_id(0), dgain_ref, _part8(dgain))
        _accumulate_rows(pl.program_id(0), dattn_ref, _part8(dattn))

    row = pl.BlockSpec((tm, d), lambda i: (i, 0))
    wide = pl.BlockSpec((tm, nblk * hs), lambda i: (i, 0))
    half = pl.BlockSpec((tm, ATTN_W), lambda i: (i, 0))
    return pl.pallas_call(
        body, grid=(s // tm,),
        in_specs=[row, row, row, pl.BlockSpec((None, 1, d), lambda i: (layer, 0, 0)), wide,
                  half, pl.BlockSpec((None, 1, ATTN_W), lambda i: (layer, 0, 0)),
                  _resident((None, mw, d), lambda i: (0, 0, 0)),
                  _resident((None, nblk, d, hs), lambda i: (0, 0, 0, 0)),
                  _resident((None, nblk * hs, d), lambda i: (0, 0, 0))],
        out_specs=[row, row, wide, half, half, half, pl.BlockSpec((8, d), lambda i: (0, 0)),
                   pl.BlockSpec((8, ATTN_W), lambda i: (0, 0))],
        out_shape=[jax.ShapeDtypeStruct((s, d), F32), jax.ShapeDtypeStruct((s, d), BF),
                   jax.ShapeDtypeStruct((s, nblk * hs), BF)] + [jax.ShapeDtypeStruct((s, ATTN_W), F32)] * 3
        + [jax.ShapeDtypeStruct((8, d), F32), jax.ShapeDtypeStruct((8, ATTN_W), F32)],
        compiler_params=_cparams(("arbitrary",)), name=f"mlp_bwd_l{layer}",
    )(dx, dxb, x, gain, u, o_attn, gain_attn, g_out, g_up, g_down)


def _attn_bwd(layer, proj, do, lse, delta, cos, sin, after):
    s = proj.shape[0]
    n_units = s // ATTN_UNIT
    out_rows = 256

    def unit_of(n):
        return n_units - 1 - n

    def body(q_ref, kp_ref, kc_ref, vp_ref, vc_ref, do_ref, lse_ref, delta_ref, cos_ref, sin_ref, after_ref, out_ref,
             dq_ref, dk_ref, dkp_ref, dv_ref, dvp_ref, carry_k, carry_v):
        step = pl.program_id(1)
        m_first, m_rest, is_a = _attn_masks(unit_of(step) > 0)
        sels = (is_a, jnp.logical_not(is_a))
        for ref in (dq_ref, dk_ref, dkp_ref, dv_ref, dvp_ref):
            ref[...] = jnp.zeros(ref.shape, F32)
        for dil in DILATIONS:
            last = ATTN_UNIT // (SPAN * dil) - 1
            for group in _attn_groups(dil):
                items = [(r, b, h) for r, b in group for h in range(2)]
                q = {rb: q_ref[_block_rows(dil, *rb), :] for rb in group}
                dov = {rb: do_ref[_block_rows(dil, *rb), :] for rb in group}
                lse_v = {rb: lse_ref[_block_rows(dil, *rb), :] for rb in group}
                delta_v = {rb: delta_ref[_block_rows(dil, *rb), :] for rb in group}
                k = {rb: _block_keys(kp_ref, kc_ref, dil, *rb).astype(BF) for rb in group}
                v = {rb: _block_keys(vp_ref, vc_ref, dil, *rb).astype(BF) for rb in group}
                qh = [jnp.where(sels[h], q[r, b], 0.0).astype(BF) for r, b, h in items]
                doh = [jnp.where(sels[h], dov[r, b], 0.0).astype(BF) for r, b, h in items]
                sc = [jnp.where(m_first if b == 0 else m_rest, _dot(qh[i], k[r, b], NT), MASK_VALUE)
                      for i, (r, b, h) in enumerate(items)]
                p = [jnp.exp(sc[i] - lse_v[r, b][:, 64 * h:64 * h + 1]) for i, (r, b, h) in enumerate(items)]
                ds = [(p[i] * (_dot(doh[i], v[r, b], NT) - delta_v[r, b][:, 64 * h:64 * h + 1])).astype(BF)
                      for i, (r, b, h) in enumerate(items)]
                dv = [_dot(p[i].astype(BF), doh[i], TN) for i in range(len(items))]
                dq = [_dot(ds[i], k[r, b], NN) for i, (r, b, h) in enumerate(items)]
                dk = [_dot(ds[i], qh[i], TN) for i in range(len(items))]
                for j, (r, b) in enumerate(group):
                    own = _block_rows(dil, r, b)
                    dq_ref[own, :] += jnp.where(is_a, dq[2 * j], dq[2 * j + 1])
                    dk2, dv2 = dk[2 * j] + dk[2 * j + 1], dv[2 * j] + dv[2 * j + 1]
                    dk_ref[own, :] += dk2[SPAN:]
                    dv_ref[own, :] += dv2[SPAN:]
                    if b > 0:
                        before = _block_rows(dil, r, b - 1)
                        dk_ref[before, :] += dk2[:SPAN]
                        dv_ref[before, :] += dv2[:SPAN]
                    else:
                        before = _block_rows(dil, r, last)
                        dkp_ref[before, :] += dk2[:SPAN]
                        dvp_ref[before, :] += dv2[:SPAN]
        has_next = step > 0
        for c in range(ATTN_UNIT // out_rows):
            rows = slice(c * out_rows, (c + 1) * out_rows)
            cs, sn = cos_ref[rows, :], sin_ref[rows, :]
            dqv = dq_ref[rows, :]
            dkv = dk_ref[rows, :] + jnp.where(has_next, carry_k[rows, :], 0.0)
            dvv = dv_ref[rows, :] + jnp.where(has_next, carry_v[rows, :], 0.0)
            out_ref[0, rows, :] = ((dqv * cs - _rope_partner(dqv) * sn) * ATTN_SCALE).astype(BF)
            out_ref[1, rows, :] = (dkv * cs - _rope_partner(dkv) * sn).astype(BF)
            out_ref[2, rows, :] = dvv.astype(BF)
        carry_k[...] = dkp_ref[...]
        carry_v[...] = dvp_ref[...]

    tab = pl.BlockSpec((ATTN_UNIT, LANES), lambda p, n: (unit_of(n), 0))
    return pl.pallas_call(
        body, grid=(ATTN_W // LANES, n_units), in_specs=_attn_specs(3, unit_of) + [tab, tab, ANY_SPEC],
        out_specs=pl.BlockSpec((3, ATTN_UNIT, LANES), lambda p, n: (0, unit_of(n), p)),
        out_shape=jax.ShapeDtypeStruct((3, s, ATTN_W), BF),
        scratch_shapes=[pltpu.VMEM((ATTN_UNIT, LANES), F32)] * 7,
        compiler_params=_cparams(("parallel", "arbitrary")), name=f"attn_bwd_l{layer}",
    )(proj, proj, proj, proj, proj, do, lse, delta, cos, sin, after)


def _hgrn_bwd(layer, proj, lb, gain, o, drec, states):
    s = proj.shape[0]
    nblk = s // HGRN_ROWS
    cpb = len(CHUNKS)

    def body(q_ref, f_ref, i_ref, g_ref, lb_ref, gain_ref, o_ref, drec_ref, st_ref, dseg_ref, dlb_ref, dgain_ref,
             dstate, dst_buf):
        step = pl.program_id(0)

        @pl.when(step == 0)
        def _():
            dstate[...] = jnp.zeros(dstate.shape, F32)

        lbv, gv = lb_ref[...], gain_ref[...]
        qh, z, gate_in = q_ref[...], f_ref[...], g_ref[...]
        pre = _hgrn_pre(qh, z, lbv)
        v = i_ref[...].astype(BF)
        sg = _sigmoid(gate_in)
        ov, drec = o_ref[...], drec_ref[...]
        dnormed = drec * (gate_in * sg)
        back = [_rms_bwd(_head(dnormed, h), _head(ov, h), gv) for h in HEADS]
        do_b = jnp.concatenate([b[0] for b in back], axis=1).astype(BF)
        dgain = back[0][1] + back[1][1] + back[2][1] + back[3][1]
        normed = jnp.concatenate([_rms_fwd(_head(ov, h), gv) for h in HEADS], axis=1)
        dgate_in = drec * normed * (sg * (1.0 + gate_in * (1.0 - sg)))
        mask = _hgrn_mask()
        qt, kt, qg, kout = (pre[n].astype(BF) for n in ("qt", "kt", "qg", "kout"))
        dec = jnp.exp(pre["g_last"])
        def intra(fn):
            return jnp.concatenate([jnp.concatenate([fn(h, sb) for sb in SUBS], axis=0) for h in HEADS], axis=1)

        def hs(x, h, sb):
            return _sub(_head(x, h), sb)

        a = [[jnp.where(mask, _dot(hs(qt, h, sb), hs(kt, h, sb), NT), 0.0).astype(BF) for sb in SUBS] for h in HEADS]
        da = [[jnp.where(mask, _dot(hs(do_b, h, sb), hs(v, h, sb), NT), 0.0).astype(BF) for sb in SUBS] for h in HEADS]
        dv_intra = intra(lambda h, sb: _dot(a[h][sb], hs(do_b, h, sb), TN))
        dqt = intra(lambda h, sb: _dot(da[h][sb], hs(kt, h, sb), NN))
        dkt = intra(lambda h, sb: _dot(da[h][sb], hs(qt, h, sb), TN))
        feed = [[_dot(_chunk(_head(do_b, h), c), _chunk(_head(qg, h), c), TN) for c in CHUNKS] for h in HEADS]
        for h in HEADS:
            dst = dstate[h]
            for c in reversed(CHUNKS):
                dst_buf[h, c * LANES:(c + 1) * LANES, :] = dst
                dst = dst * _head(dec, h)[c * HGRN_CHUNK:c * HGRN_CHUNK + 1, :] + feed[h][c]
            dstate[h] = dst

        def per_chunk(fn):
            cols = []
            for h in HEADS:
                rows = [jnp.broadcast_to(t, (HGRN_CHUNK, HGRN_DIM)) for t in (fn(h, c) for c in CHUNKS)]
                cols.append(jnp.concatenate(rows, axis=0))
            return jnp.concatenate(cols, axis=1)

        def st_prev(h, c):
            return st_ref[h, c * LANES:(c + 1) * LANES, :]

        def dst_at(h, c):
            return dst_buf[h, c * LANES:(c + 1) * LANES, :]

        dqg = per_chunk(lambda h, c: _dot(_chunk(_head(do_b, h), c), st_prev(h, c).astype(BF), NN))
        dkout = per_chunk(lambda h, c: _dot(_chunk(_head(v, h), c), dst_at(h, c).astype(BF), NN))
        dv_inter = per_chunk(lambda h, c: _dot(_chunk(_head(kout, h), c), dst_at(h, c).astype(BF), NT))
        dg_state = per_chunk(lambda h, c: jnp.sum(dst_at(h, c) * st_prev(h, c).astype(F32), axis=0, keepdims=True))
        dg_kout = per_chunk(lambda h, c: jnp.sum(_chunk(_head(dkout * pre["kout"], h), c), axis=0, keepdims=True))
        dv = dv_intra + dv_inter
        pos = lax.broadcasted_iota(jnp.int32, (HGRN_ROWS, HGRN_W), 0) % HGRN_CHUNK
        dq = dqt * pre["e_q"] + dqg * pre["e_in"]
        dk = dkt * pre["e_k"] + dkout * pre["e_out"]
        dg = (dqt * pre["qt"] - dkt * pre["kt"] + dqg * pre["qg"] - dkout * pre["kout"]
              + jnp.where(pos == HGRN_CHUNK - 1, dg_state * dec + dg_kout, 0.0))
        dlogf = _chunk_cumsum(dg, reverse=True)
        sig, sq = pre["sig"], pre["sq"]
        df = dlogf / pre["f"] - dk
        dseg_ref[0] = (dq * HGRN_SCALE * (sq * (1.0 + qh * (1.0 - sq)))).astype(BF)
        dseg_ref[1] = (df * (1.0 - lbv) * sig * (1.0 - sig)).astype(BF)
        dseg_ref[2] = dv.astype(BF)
        dseg_ref[3] = dgate_in.astype(BF)
        _accumulate_rows(step, dlb_ref, _part8(df * (1.0 - sig)))
        _accumulate_rows(step, dgain_ref, _part8(dgain))

    specs, blk = _hgrn_in_specs(layer, True, nblk)
    specs += [pl.BlockSpec((HGRN_ROWS, HGRN_W), lambda b: (blk(b), 0)),
              pl.BlockSpec((HGRN_ROWS, HGRN_W), lambda b: (blk(b), 0)),
              pl.BlockSpec((HGRN_HEADS, cpb * LANES, LANES), lambda b: (0, blk(b), 0))]
    return pl.pallas_call(
        body, grid=(nblk,), in_specs=specs,
        out_specs=[pl.BlockSpec((4, HGRN_ROWS, HGRN_W), lambda b: (0, blk(b), 0)),
                   pl.BlockSpec((8, HGRN_W), lambda b: (0, 0)), pl.BlockSpec((8, HGRN_DIM), lambda b: (0, 0))],
        out_shape=[jax.ShapeDtypeStruct((4, s, HGRN_W), BF), jax.ShapeDtypeStruct((8, HGRN_W), F32),
                   jax.ShapeDtypeStruct((8, HGRN_DIM), F32)],
        scratch_shapes=[pltpu.VMEM((HGRN_HEADS, LANES, LANES), F32), pltpu.VMEM((HGRN_HEADS, cpb * LANES, LANES), F32)],
        compiler_params=_cparams(("arbitrary",)), name=f"hgrn_bwd_l{layer}",
    )(proj, proj, proj, proj, lb, gain, o, drec, states)


def _bwd_inproj(layer, dqkv, dhg, g_in, x, gain, dres, tm=512):
    s, d = x.shape

    def body(dqkv_ref, dhg_ref, w_ref, x_ref, gain_ref, dres_ref, dx_ref, dxb_ref, dgain_ref):
        acc = jnp.zeros((tm, d), F32)
        for seg in range(N_SEG):
            a = dqkv_ref[seg] if seg < 3 else dhg_ref[seg - 3]
            acc = acc + _dot(a, w_ref[seg * SEG:(seg + 1) * SEG, :], NN)
        dx, dgain = _rms_bwd(acc, x_ref[...], gain_ref[...])
        out = dres_ref[...] + dx
        dx_ref[...] = out
        dxb_ref[...] = out.astype(BF)
        _accumulate_rows(pl.program_id(0), dgain_ref, _part8(dgain))

    row = pl.BlockSpec((tm, d), lambda i: (i, 0))
    return pl.pallas_call(
        body, grid=(s // tm,),
        in_specs=[pl.BlockSpec((3, tm, SEG), lambda i: (0, i, 0)), pl.BlockSpec((4, tm, SEG), lambda i: (0, i, 0)),
                  _resident((None, PROJ_W, d), lambda i: (0, 0, 0)), row,
                  pl.BlockSpec((None, 1, d), lambda i: (layer, 0, 0)), row],
        out_specs=[row, row, pl.BlockSpec((8, d), lambda i: (0, 0))],
        out_shape=[jax.ShapeDtypeStruct((s, d), F32), jax.ShapeDtypeStruct((s, d), BF), jax.ShapeDtypeStruct((8, d), F32)],
        compiler_params=_cparams(("arbitrary",)), name=f"bwd_inproj_l{layer}",
    )(dqkv, dhg, g_in, x, gain, dres)


def _adamw(w, g, m, v):
    m2 = ADAM_B1 * m + (1.0 - ADAM_B1) * g
    v2 = ADAM_B2 * v + (1.0 - ADAM_B2) * (g * g)
    m_hat = m2 / (1.0 - ADAM_B1 ** ADAM_STEP)
    v_hat = v2 / (1.0 - ADAM_B2 ** ADAM_STEP)
    delta = -ADAM_LR * (m_hat / (jnp.sqrt(v_hat) + ADAM_EPS) + ADAM_WD * w)
    return delta, m2, v2


def _adam_big(name, parts, w, m, v, row_tiles):
    depth = w.shape[0]
    r, c = parts[0].shape[1], parts[0].shape[2]
    tr = r // row_tiles
    p_spec = pl.BlockSpec((N_DEV, tr, c), lambda t: (0, t, 0))
    w_spec = pl.BlockSpec((depth, tr, c), lambda t: (0, t, 0))

    def body(*refs):
        p_refs = refs[:depth]
        w_ref, m_ref, v_ref, g_ref, d_ref, m2_ref, v2_ref, token = refs[depth:]
        token[...] = jnp.zeros(token.shape, F32)
        for l in range(depth):
            g = p_refs[l][0].astype(F32)
            for dev in range(1, N_DEV):
                g = g + p_refs[l][dev].astype(F32)
            delta, m2, v2 = _adamw(w_ref[l], g, m_ref[l], v_ref[l])
            g_ref[l] = g
            d_ref[l] = delta
            m2_ref[l] = m2
            v2_ref[l] = v2

    return pl.pallas_call(
        body, grid=(row_tiles,), in_specs=[p_spec] * depth + [w_spec] * 3,
        out_specs=[w_spec] * 4 + [pl.BlockSpec((8, LANES), lambda t: (0, 0))],
        out_shape=[jax.ShapeDtypeStruct(w.shape, F32)] * 4 + [jax.ShapeDtypeStruct((8, LANES), F32)],
        compiler_params=_cparams(("arbitrary",)), name=name,
    )(*parts, w, m, v)


def _adam_small(g, ws, ms, vs):
    n = len(ws)

    def split(row, width):
        return jnp.concatenate([row[:, :width], row[:, width:2 * width]], axis=0)

    def body(g_ref, *refs):
        ins, outs = refs[:3 * n], refs[3 * n:]
        grads = [g_ref[0:2, :], split(g_ref[5:6, :], ATTN_W), split(g_ref[6:7, :], HGRN_W), split(g_ref[7:8, :], HGRN_DIM),
                 g_ref[2:4, :], g_ref[4:5, :]]
        for i, g_i in enumerate(grads):
            delta, m2, v2 = _adamw(ins[i][...], g_i, ins[n + i][...], ins[2 * n + i][...])
            for j, val in enumerate((g_i, delta, m2, v2)):
                outs[4 * i + j][...] = val

    vm = pl.BlockSpec(memory_space=pltpu.VMEM)
    res = pl.pallas_call(
        body, in_specs=[vm] * (1 + 3 * n), out_specs=[vm] * (4 * n),
        out_shape=[jax.ShapeDtypeStruct(w.shape, F32) for w in ws for _ in range(4)], name="adam_small",
    )(g, *ws, *ms, *vs)
    return [res[4 * i:4 * i + 4] for i in range(n)]


def _lower_bounds(logits):
    def body(l_ref, lb_ref, jac_ref):
        l0, l1 = l_ref[0:1, :], l_ref[1:2, :]
        mx = jnp.maximum(l0, l1)
        e0, e1 = jnp.exp(l0 - mx), jnp.exp(l1 - mx)
        p0, p1 = e0 / (e0 + e1), e1 / (e0 + e1)
        lb_ref[0:1, :] = p0 - p0
        lb_ref[1:2, :] = (p0 + p1) - p0
        jac_ref[0:1, :] = -p0 * p1
        jac_ref[1:2, :] = p0 * p1

    vm = pl.BlockSpec(memory_space=pltpu.VMEM)
    return pl.pallas_call(body, in_specs=[vm], out_specs=[vm, vm], out_shape=[jax.ShapeDtypeStruct(logits.shape, F32)] * 2,
                          name="hgrn_lower_bounds")(logits)


def _rope_tables(s, after):
    half = 32
    inv_freq = ROPE_THETA ** (-jnp.arange(half, dtype=F32) / half)
    ang = (jnp.arange(s, dtype=jnp.int32).astype(F32) + after[0, 0])[:, None] * inv_freq[None, :]
    cos, sin = jnp.cos(ang), jnp.sin(ang)
    return jnp.concatenate([cos] * 4, axis=1), jnp.concatenate([-sin, sin, -sin, sin], axis=1)


def kernel(x, norm_mix, w_in, attn_out_gain, hgrn_lb_logits, hgrn_out_gain, w_out, norm_mlp, w_up, w_down, norm_final, loss_target, m_norm_mix, m_w_in, m_attn_out_gain, m_hgrn_lb_logits, m_hgrn_out_gain, m_w_out, m_norm_mlp, m_w_up, m_w_down, m_norm_final, v_norm_mix, v_w_in, v_attn_out_gain, v_hgrn_lb_logits, v_hgrn_out_gain, v_w_out, v_norm_mlp, v_w_up, v_w_down, v_norm_final):
    depth = w_in.shape[0]
    assert depth == 2 and x.shape[0] == 1
    s, d = x.shape[1], x.shape[2]
    x0 = x[0]
    target = loss_target[0]
    g_mix, g_attn, g_hg, g_mlp = (norm_mix[:, None, :], attn_out_gain[:, None, :], hgrn_out_gain[:, None, :],
                                  norm_mlp[:, None, :])
    lb, lb_jac = _lower_bounds(hgrn_lb_logits)
    lb3 = lb[:, None, :]

    def flip(a):
        return jnp.swapaxes(a, 1, 2)

    shards = list(_pack_weights(flip(w_in), w_out, w_up, w_down))
    w_pieces = _weight_pieces(*shards)
    w_groups = [[0], [1, 2, 3], [4], [5, 6, 7]]
    me = (4 * lax.axis_index("x") + 2 * lax.axis_index("y") + lax.axis_index("c")).astype(jnp.int32).reshape(1)
    lands = _exchange_own("all_gather_own", me, shards, w_pieces)
    w_sems, shards, lands, token = _exchange_start("all_gather_start", shards, lands, w_pieces, w_groups, DIRECT_PEERS)

    def arrived(group, after):
        nonlocal shards
        idxs = w_groups[group]
        shards, got = _exchange_wait(f"all_gather_wait{group}", shards, [lands[i] for i in idxs], w_pieces,
                                     [(idxs, *w_sems[group])], after, DIRECT_PEERS)
        return got, [w_pieces[i] for i in idxs]

    def relaying(group, after):
        nonlocal shards
        idxs = w_groups[group]
        sems, shards, got, tok = _arrive_and_relay(f"all_gather_wait{group}", shards, [lands[i] for i in idxs], w_pieces,
                                                   (idxs, *w_sems[group]), after)
        return (group, got, [w_pieces[i] for i in idxs], sems), tok

    def relayed(handle, after):
        group, got, pieces, sems = handle
        return _relay_wait(f"all_gather_relay_wait{group}", got, pieces, sems, after)

    cos, sin = _rope_tables(s, token)

    def tied(small_arr, tok):
        return small_arr + tok[0, 0]

    saved = []
    xl = x0
    full = [None] * depth
    next_in = None
    for l in range(depth):
        (full_in,) = _relay_to_sibling("all_gather_relay0", *arrived(0, cos)) if l == 0 else relayed(next_in, xl)
        saved_x = xl
        proj, h = _fwd_inproj(l, xl, g_mix, full_in, cos, sin)
        o_attn, lse = _attn_fwd(l, proj)
        mlp_weights, tok = relaying(2 * l + 1, lse)
        o_hg, mixed, states = _hgrn_fwd(l, proj, lb3, tied(g_hg, tok))
        mixed = _attn_norm(l, o_attn, g_attn, mixed)
        full_out, full_up, full_down = relayed(mlp_weights, mixed)
        g_mlp_l = g_mlp
        if l + 1 < depth:
            next_in, tok = relaying(2 * l + 2, mixed)
            g_mlp_l = tied(g_mlp, tok)
        head = (norm_final[None, :], target) if l == depth - 1 else None
        xl, x_mid, u, h2, *loss_side = _mlp_fwd(l, xl, mixed, g_mlp_l, full_out, full_up, full_down, head)
        saved.append((saved_x, proj, h, o_attn, lse, o_hg, states, mixed, x_mid, u, h2))
        full[l] = (full_in, full_out, full_up, full_down)
    dx, (dxb, dnorm_final8, loss8) = xl, loss_side

    exchanges = []

    def scatter(tag, grads, kinds):
        pieces = _grad_pieces(grads, kinds)
        own = _exchange_own(f"reduce_scatter_own_{tag}", me, grads, pieces)
        sems, grads, own, tok = _exchange_start(f"reduce_scatter_start_{tag}", grads, own, pieces, [list(range(len(pieces)))])
        exchanges.append((grads, own, pieces, sems[0]))
        return tok

    small = {}
    for l in reversed(range(depth)):
        xl, proj, h, o_attn, lse, o_hg, states, mixed, x_mid, u, h2 = saved[l]
        full_in, full_out, full_up, full_down = full[l]
        hs = full_up.shape[3]
        gw_down = _mm_tn(f"grad_w_down_l{l}", u, dxb, u.shape[1], a_fn=_relu2)
        dx_mid, dx_mid_b, du, drec, do, delta, dmlp8, dattn8 = _mlp_bwd(l, dx, dxb, x_mid, g_mlp, u, o_attn, g_attn,
                                                                         full_out, full_up, full_down)
        gw_up = _mm_tn(f"grad_w_up_l{l}", h2, du, d, out_block_w=hs)
        gw_out = _mm_tn(f"grad_w_out_l{l}", mixed, dx_mid_b, mixed.shape[1])
        started = scatter(f"mlp_l{l}", [gw_down, gw_up, gw_out], ["rows", "up", "rows"])
        dqkv = _attn_bwd(l, proj, do, lse, delta, cos, sin, started)
        dhg, dlb8, dhgain8 = _hgrn_bwd(l, proj, lb3, g_hg, o_hg, drec, states)
        gin = _mm_tn(f"grad_w_in_qkv_l{l}", dqkv, h, PROJ_W, a_lead=True)
        gw_in = _mm_tn(f"grad_w_in_hg_l{l}", dhg, h, PROJ_W, a_lead=True, out_block_off=3, prev=gin)
        g_mix_t = tied(g_mix, scatter(f"mix_l{l}", [gw_in], ["rows"]))
        dx, dxb, dmix8 = _bwd_inproj(l, dqkv, dhg, full_in, xl, g_mix_t, dx_mid)
        small[l] = (dmix8, dattn8, dlb8, dhgain8, dmlp8)

    def scattered(name, which, after):
        grads, lands, pieces, waits = [], [], [], []
        for grads_e, own, pieces_e, (send, recv) in (exchanges[i] for i in which):
            first = len(pieces)
            pieces += [p._replace(src=p.src + len(grads)) for p in pieces_e]
            waits.append((list(range(first, first + len(pieces_e))), send, recv))
            grads += grads_e
            lands += own
        return _exchange_wait(name, grads, lands, pieces, waits, after)[1]

    down1, up1, out1, in1, down0, up0, out0 = scattered("reduce_scatter_wait_early", (0, 1, 2), dx)
    big = {
        "w_down": _adam_big("adam_w_down", [down0, down1], w_down, m_w_down, v_w_down, 4),
        "w_up": _adam_big("adam_w_up", [up0, up1], w_up, m_w_up, v_w_up, 2),
        "w_out": _adam_big("adam_w_out", [out0, out1], w_out, m_w_out, v_w_out, 1),
    }
    g_small = _all_reduce_small([small[l][0] for l in range(depth)], [small[l][4] for l in range(depth)], dnorm_final8,
                                [small[l][1] for l in range(depth)], small[depth - 1][2], lb_jac,
                                [small[l][3] for l in range(depth)], loss8, big["w_out"][4])
    loss = g_small[7, 2 * HGRN_DIM]
    row = lambda a: a[None, :]
    small_out = _adam_small(
        g_small, [norm_mix, attn_out_gain, hgrn_lb_logits, hgrn_out_gain, norm_mlp, row(norm_final)],
        [m_norm_mix, m_attn_out_gain, m_hgrn_lb_logits, m_hgrn_out_gain, m_norm_mlp, row(m_norm_final)],
        [v_norm_mix, v_attn_out_gain, v_hgrn_lb_logits, v_hgrn_out_gain, v_norm_mlp, row(v_norm_final)])
    small_out[5] = [t[0] for t in small_out[5]]
    (in0,) = scattered("reduce_scatter_wait_last", (3,), small_out[0][1])
    big["w_in"] = [flip(t) for t in _adam_big("adam_w_in", [in0, in1], flip(w_in), flip(m_w_in), flip(v_w_in), 2)[:4]]

    def gather(idx):
        mix, attn, lbl, hg, mlp, final = (t[idx] for t in small_out)
        return [mix, big["w_in"][idx], attn, lbl, hg, big["w_out"][idx], mlp, big["w_up"][idx], big["w_down"][idx], final]

    return (loss, dx[None], *gather(0), *gather(1), *gather(2), *gather(3))
```
